```python
import jax
import jax.numpy as jnp
from jax import lax
import numpy as np

D_MODEL = 2048
BATCH = 8
SEQ = 2048
DEPTH = 1

N_META = 16
POOL_GROUPS = 4
POOL_WINDOWS = (2, 4, 8, 16)
POOL_WIDTH = D_MODEL // 2
POOL_GROUP_DIM = POOL_WIDTH // POOL_GROUPS
DN_HEADS = 16
DN_HEAD_DIM = 128
DN_WIDTH = DN_HEADS * DN_HEAD_DIM
CONV_WIDTH = 4
CHUNK = 64
NORM_EPS = 1e-6
IN_SPLIT_SIZES = (POOL_WIDTH, POOL_WIDTH, DN_WIDTH, DN_WIDTH, DN_WIDTH, DN_WIDTH, DN_HEADS, DN_HEADS, D_MODEL, D_MODEL)
IN_COLS = 2 * POOL_WIDTH + 4 * DN_WIDTH + 2 * DN_HEADS + 2 * D_MODEL

kernel_name = "hybrid_pool_gated_deltanet_block"


def rmsnorm(x, w):
    xf = x.astype(jnp.float32)
    y = xf * lax.rsqrt(jnp.mean(xf * xf, axis=-1, keepdims=True) + NORM_EPS)
    return (y * w.astype(jnp.float32)).astype(x.dtype)


def l2norm(x):
    return x * lax.rsqrt(jnp.sum(x * x, axis=-1, keepdims=True) + NORM_EPS)


def causal_multiscale_pool(u, mix_w, scale):
    Bsz, L, _ = u.shape
    uf = u.astype(jnp.float32)
    csum = jnp.concatenate([jnp.zeros((Bsz, 1, POOL_WIDTH), jnp.float32), jnp.cumsum(uf, axis=1)], axis=1)
    t = jnp.arange(1, L + 1)
    pooled = []
    for gi, w in enumerate(POOL_WINDOWS):
        c = csum[..., gi * POOL_GROUP_DIM:(gi + 1) * POOL_GROUP_DIM]
        lag = jnp.pad(c, ((0, 0), (w, 0), (0, 0)))[:, :L + 1]
        cnt = jnp.minimum(t, w).astype(jnp.float32)[None, :, None]
        pooled.append((c[:, 1:] - lag[:, 1:]) / cnt)
    pooled = jnp.concatenate(pooled, axis=-1) - uf
    pooled = pooled.reshape(Bsz, L, POOL_GROUPS, POOL_GROUP_DIM)
    mixed = jnp.einsum('blgc,gcd->blgd', pooled, mix_w.astype(jnp.float32)).reshape(Bsz, L, POOL_WIDTH)
    return mixed * scale.astype(jnp.float32)


def causal_depthwise_conv_silu(x, w):
    K = w.shape[0]
    L = x.shape[1]
    xp = jnp.pad(x, ((0, 0), (K - 1, 0), (0, 0)))
    y = xp[:, K - 1:K - 1 + L] * w[K - 1]
    for kk in range(K - 1):
        y = y + xp[:, kk:kk + L] * w[kk]
    return jax.nn.silu(y)


def chunk_gated_delta_rule(q, k, v, beta, g):
    Bsz, Lp, H, Dk = q.shape
    Dv = v.shape[-1]
    N = Lp // CHUNK

    def to_chunks(t):
        return jnp.moveaxis(t.reshape((Bsz, N, CHUNK) + t.shape[2:]), 3, 1)

    q, k, v, beta, g = to_chunks(q), to_chunks(k), to_chunks(v), to_chunks(beta), to_chunks(g)
    gcum = jnp.cumsum(g, axis=-1)
    causal = jnp.tril(jnp.ones((CHUNK, CHUNK), bool))
    strict = jnp.tril(jnp.ones((CHUNK, CHUNK), bool), -1)
    diff = gcum[..., :, None] - gcum[..., None, :]
    decay = jnp.where(causal, jnp.exp(jnp.where(causal, diff, 0.0)), 0.0)
    k_beta = k * beta[..., None]
    v_beta = v * beta[..., None]
    lmat = jnp.where(strict, jnp.einsum('bhncd,bhnsd->bhncs', k_beta, k) * decay, 0.0)
    eye = jnp.eye(CHUNK, dtype=jnp.float32)
    tmat = lax.linalg.triangular_solve(eye + lmat, jnp.broadcast_to(eye, lmat.shape), left_side=True, lower=True)
    u_c = jnp.einsum('bhncs,bhnsd->bhncd', tmat, v_beta)
    w_c = jnp.einsum('bhncs,bhnsd->bhncd', tmat, k_beta * jnp.exp(gcum)[..., None])
    qk = jnp.where(causal, jnp.einsum('bhncd,bhnsd->bhncs', q, k) * decay, 0.0)
    q_dec = q * jnp.exp(gcum)[..., None]
    k_dec = k * jnp.exp(gcum[..., -1:] - gcum)[..., None]
    g_last = jnp.exp(gcum[..., -1])

    def step(S, xs):
        u_i, w_i, q_i, k_i, qk_i, gl_i = xs
        v_new = u_i - jnp.einsum('bhcd,bhde->bhce', w_i, S)
        o_i = jnp.einsum('bhcd,bhde->bhce', q_i, S) + jnp.einsum('bhcs,bhse->bhce', qk_i, v_new)
        S = S * gl_i[..., None, None] + jnp.einsum('bhcd,bhce->bhde', k_i, v_new)
        return S, o_i

    xs = tuple(jnp.moveaxis(t, 2, 0) for t in (u_c, w_c, q_dec, k_dec, qk, g_last))
    S0 = jnp.zeros((Bsz, H, Dk, Dv), jnp.float32)
    _, o = lax.scan(step, S0, xs)
    o = jnp.transpose(o, (1, 0, 3, 2, 4))
    return o.reshape(Bsz, Lp, H, Dv)


def gated_deltanet_branch(q, k, v, z, b, a, conv_w, A_log, dt_bias, norm_w):
    Bsz, L, _ = q.shape
    qkv = causal_depthwise_conv_silu(jnp.concatenate([q, k, v], axis=-1), conv_w)
    q, k, v = jnp.split(qkv.astype(jnp.float32), 3, axis=-1)
    heads = lambda t: t.reshape(Bsz, L, DN_HEADS, DN_HEAD_DIM)
    q = l2norm(heads(q)) * (DN_HEAD_DIM ** -0.5)
    k = l2norm(heads(k))
    v = heads(v)
    beta = jax.nn.sigmoid(b.astype(jnp.float32))
    g = -jnp.exp(A_log.astype(jnp.float32)) * jax.nn.softplus(a.astype(jnp.float32) + dt_bias.astype(jnp.float32))
    pad = (-N_META) % CHUNK
    front = lambda t: jnp.pad(t, ((0, 0), (pad, 0)) + ((0, 0),) * (t.ndim - 2))
    o = chunk_gated_delta_rule(front(q), front(k), front(v), front(beta), front(g))[:, pad:]
    o = rmsnorm(o, norm_w) * jax.nn.silu(heads(z.astype(jnp.float32)))
    return o.reshape(Bsz, L, DN_WIDTH)


def hybrid_layer(h, norm_w, w_in, conv_w, A_log, dt_bias, pool_mix, pool_scale, dn_norm_w, w_pool_out, w_dn_out, w_o):
    xn = rmsnorm(h, norm_w)
    proj = xn @ w_in
    splits = np.cumsum(IN_SPLIT_SIZES[:-1]).tolist()
    u_pool, z_pool, q, k, v, z_dn, b, a, gate_pool, gate_dn = jnp.split(proj, splits, axis=-1)
    y_pool = (causal_multiscale_pool(u_pool, pool_mix, pool_scale) * jax.nn.silu(z_pool.astype(jnp.float32))).astype(h.dtype)
    y_dn = gated_deltanet_branch(q, k, v, z_dn, b, a, conv_w, A_log, dt_bias, dn_norm_w).astype(h.dtype)
    merged = jax.nn.sigmoid(gate_pool) * (y_pool @ w_pool_out) + jax.nn.sigmoid(gate_dn) * (y_dn @ w_dn_out)
    return h + merged @ w_o


def _fwd_setup_inputs(seed: int = 0) -> dict:
    key = jax.random.key(seed)
    ks = jax.random.split(key, 14)
    nrm = jax.random.normal
    x = nrm(ks[0], (BATCH, SEQ, D_MODEL), jnp.float32)
    meta_tokens = nrm(ks[1], (N_META, D_MODEL), jnp.float32)
    norm_w = 1.0 + 0.02 * nrm(ks[2], (DEPTH, D_MODEL), jnp.float32)
    w_in = nrm(ks[3], (DEPTH, D_MODEL, IN_COLS), jnp.float32) * D_MODEL ** -0.5
    conv_w = nrm(ks[4], (DEPTH, CONV_WIDTH, 3 * DN_WIDTH), jnp.float32) * CONV_WIDTH ** -0.5
    A_log = jnp.log(jax.random.uniform(ks[5], (DEPTH, DN_HEADS), jnp.float32, minval=1.0, maxval=16.0))
    dt = jnp.exp(jax.random.uniform(ks[6], (DEPTH, DN_HEADS), jnp.float32, minval=float(np.log(1e-3)), maxval=float(np.log(1e-1))))
    dt_bias = dt + jnp.log(-jnp.expm1(-dt))
    pool_mix = nrm(ks[7], (DEPTH, POOL_GROUPS, POOL_GROUP_DIM, POOL_GROUP_DIM), jnp.float32) * POOL_GROUP_DIM ** -0.5
    pool_scale = 1.0 + 0.02 * nrm(ks[8], (DEPTH, POOL_WIDTH), jnp.float32)
    dn_norm_w = 1.0 + 0.02 * nrm(ks[9], (DEPTH, DN_HEAD_DIM), jnp.float32)
    w_pool_out = nrm(ks[10], (DEPTH, POOL_WIDTH, D_MODEL), jnp.float32) * POOL_WIDTH ** -0.5
    w_dn_out = nrm(ks[11], (DEPTH, DN_WIDTH, D_MODEL), jnp.float32) * DN_WIDTH ** -0.5
    w_o = nrm(ks[12], (DEPTH, D_MODEL, D_MODEL), jnp.float32) * D_MODEL ** -0.5
    final_norm_w = 1.0 + 0.02 * nrm(ks[13], (D_MODEL,), jnp.float32)
    return {"x": x, "meta_tokens": meta_tokens, "norm_w": norm_w, "w_in": w_in, "conv_w": conv_w,
            "A_log": A_log, "dt_bias": dt_bias, "pool_mix": pool_mix, "pool_scale": pool_scale,
            "dn_norm_w": dn_norm_w, "w_pool_out": w_pool_out, "w_dn_out": w_dn_out, "w_o": w_o,
            "final_norm_w": final_norm_w}


def _fwd_reference(x, meta_tokens, norm_w, w_in, conv_w, A_log, dt_bias, pool_mix, pool_scale, dn_norm_w, w_pool_out, w_dn_out, w_o, final_norm_w):
    Bsz = x.shape[0]
    meta = jnp.broadcast_to(meta_tokens.astype(x.dtype)[None], (Bsz, N_META, D_MODEL))
    h = jnp.concatenate([meta, x], axis=1)
    for i in range(DEPTH):
        h = hybrid_layer(h, norm_w[i], w_in[i], conv_w[i], A_log[i], dt_bias[i], pool_mix[i], pool_scale[i],
                         dn_norm_w[i], w_pool_out[i], w_dn_out[i], w_o[i])
    return rmsnorm(h[:, N_META:], final_norm_w)


import jax as _jax
import jax.numpy as _jnp

TWIN_FORMAT = 'train_step'
FWD_PARAMS = ['x', 'meta_tokens', 'norm_w', 'w_in', 'conv_w', 'A_log', 'dt_bias', 'pool_mix', 'pool_scale', 'dn_norm_w', 'w_pool_out', 'w_dn_out', 'w_o', 'final_norm_w']
TWIN_WEIGHTS = ['meta_tokens', 'norm_w', 'w_in', 'conv_w', 'A_log', 'dt_bias', 'pool_mix', 'pool_scale', 'dn_norm_w', 'w_pool_out', 'w_dn_out', 'w_o', 'final_norm_w']
TWIN_DIFF_INPUT = 'x'
TWIN_INPUTS = ['x', 'meta_tokens', 'norm_w', 'w_in', 'conv_w', 'A_log', 'dt_bias', 'pool_mix', 'pool_scale', 'dn_norm_w', 'w_pool_out', 'w_dn_out', 'w_o', 'final_norm_w', 'loss_target', 'm_meta_tokens', 'm_norm_w', 'm_w_in', 'm_conv_w', 'm_A_log', 'm_dt_bias', 'm_pool_mix', 'm_pool_scale', 'm_dn_norm_w', 'm_w_pool_out', 'm_w_dn_out', 'm_w_o', 'm_final_norm_w', 'v_meta_tokens', 'v_norm_w', 'v_w_in', 'v_conv_w', 'v_A_log', 'v_dt_bias', 'v_pool_mix', 'v_pool_scale', 'v_dn_norm_w', 'v_w_pool_out', 'v_w_dn_out', 'v_w_o', 'v_final_norm_w']
TWIN_OUTPUTS = ['loss', 'grad_x', 'grad_meta_tokens', 'grad_norm_w', 'grad_w_in', 'grad_conv_w', 'grad_A_log', 'grad_dt_bias', 'grad_pool_mix', 'grad_pool_scale', 'grad_dn_norm_w', 'grad_w_pool_out', 'grad_w_dn_out', 'grad_w_o', 'grad_final_norm_w', 'delta_meta_tokens', 'delta_norm_w', 'delta_w_in', 'delta_conv_w', 'delta_A_log', 'delta_dt_bias', 'delta_pool_mix', 'delta_pool_scale', 'delta_dn_norm_w', 'delta_w_pool_out', 'delta_w_dn_out', 'delta_w_o', 'delta_final_norm_w', 'new_m_meta_tokens', 'new_m_norm_w', 'new_m_w_in', 'new_m_conv_w', 'new_m_A_log', 'new_m_dt_bias', 'new_m_pool_mix', 'new_m_pool_scale', 'new_m_dn_norm_w', 'new_m_w_pool_out', 'new_m_w_dn_out', 'new_m_w_o', 'new_m_final_norm_w', 'new_v_meta_tokens', 'new_v_norm_w', 'new_v_w_in', 'new_v_conv_w', 'new_v_A_log', 'new_v_dt_bias', 'new_v_pool_mix', 'new_v_pool_scale', 'new_v_dn_norm_w', 'new_v_w_pool_out', 'new_v_w_dn_out', 'new_v_w_o', 'new_v_final_norm_w']
TWIN_LEAF_KINDS = {'loss': 'loss', 'grad_x': 'grad_x', 'grad_meta_tokens': 'grad_w', 'grad_norm_w': 'grad_w', 'grad_w_in': 'grad_w', 'grad_conv_w': 'grad_w', 'grad_A_log': 'grad_w', 'grad_dt_bias': 'grad_w', 'grad_pool_mix': 'grad_w', 'grad_pool_scale': 'grad_w', 'grad_dn_norm_w': 'grad_w', 'grad_w_pool_out': 'grad_w', 'grad_w_dn_out': 'grad_w', 'grad_w_o': 'grad_w', 'grad_final_norm_w': 'grad_w', 'delta_meta_tokens': 'delta_w', 'delta_norm_w': 'delta_w', 'delta_w_in': 'delta_w', 'delta_conv_w': 'delta_w', 'delta_A_log': 'delta_w', 'delta_dt_bias': 'delta_w', 'delta_pool_mix': 'delta_w', 'delta_pool_scale': 'delta_w', 'delta_dn_norm_w': 'delta_w', 'delta_w_pool_out': 'delta_w', 'delta_w_dn_out': 'delta_w', 'delta_w_o': 'delta_w', 'delta_final_norm_w': 'delta_w', 'new_m_meta_tokens': 'new_m', 'new_m_norm_w': 'new_m', 'new_m_w_in': 'new_m', 'new_m_conv_w': 'new_m', 'new_m_A_log': 'new_m', 'new_m_dt_bias': 'new_m', 'new_m_pool_mix': 'new_m', 'new_m_pool_scale': 'new_m', 'new_m_dn_norm_w': 'new_m', 'new_m_w_pool_out': 'new_m', 'new_m_w_dn_out': 'new_m', 'new_m_w_o': 'new_m', 'new_m_final_norm_w': 'new_m', 'new_v_meta_tokens': 'new_v', 'new_v_norm_w': 'new_v', 'new_v_w_in': 'new_v', 'new_v_conv_w': 'new_v', 'new_v_A_log': 'new_v', 'new_v_dt_bias': 'new_v', 'new_v_pool_mix': 'new_v', 'new_v_pool_scale': 'new_v', 'new_v_dn_norm_w': 'new_v', 'new_v_w_pool_out': 'new_v', 'new_v_w_dn_out': 'new_v', 'new_v_w_o': 'new_v', 'new_v_final_norm_w': 'new_v'}


def _forward(args):
    return _fwd_reference(*[args[k] for k in FWD_PARAMS])


def _output_shape():
    out = _jax.eval_shape(lambda: _forward(_fwd_setup_inputs(0)))
    return out.shape, out.dtype

N_MICROBATCH = 1
ADAM_LR = 0.001
ADAM_B1 = 0.9
ADAM_B2 = 0.999
ADAM_EPS = 1e-08
ADAM_WD = 0.01
ADAM_STEP = 10
PER_EXAMPLE_BATCH_AXIS = {'x': 0, 'loss_target': 0}
SHARED_INPUTS = []
_WEIGHT_DTYPES = {'meta_tokens': _jnp.float32, 'norm_w': _jnp.float32, 'w_in': _jnp.float32, 'conv_w': _jnp.float32, 'A_log': _jnp.float32, 'dt_bias': _jnp.float32, 'pool_mix': _jnp.float32, 'pool_scale': _jnp.float32, 'dn_norm_w': _jnp.float32, 'w_pool_out': _jnp.float32, 'w_dn_out': _jnp.float32, 'w_o': _jnp.float32, 'final_norm_w': _jnp.float32}
MOMENT_SCALE = {'meta_tokens': 8.103213e-04, 'norm_w': 4.026005e-02, 'w_in': 1.506255e-02, 'conv_w': 1.392781e-02, 'A_log': 5.446929e-02, 'dt_bias': 5.238225e-02, 'pool_mix': 2.308760e-02, 'pool_scale': 2.384473e-02, 'dn_norm_w': 7.011184e-02, 'w_pool_out': 1.636036e-02, 'w_dn_out': 1.792424e-02, 'w_o': 2.427982e-02, 'final_norm_w': 7.998048e+00}


def _to_microbatches(a, axis):
    t = _jnp.moveaxis(a, axis, 0)
    t = t.reshape((N_MICROBATCH, t.shape[0] // N_MICROBATCH) + t.shape[1:])
    return _jnp.moveaxis(t, 1, axis + 1)


def setup_inputs(seed: int = 0) -> dict:
    inp = _fwd_setup_inputs(seed)
    key = _jax.random.fold_in(_jax.random.key(seed), 7919)
    shape, _ = _output_shape()
    out = dict(inp)
    out["loss_target"] = _jax.random.normal(_jax.random.fold_in(key, 0), shape, _jnp.float32)
    for i, name in enumerate(TWIN_WEIGHTS):
        w = inp[name].astype(_jnp.float32)
        if MOMENT_SCALE is None:
            s = _jnp.sqrt(_jnp.mean(_jnp.square(w)) + 1e-30)
        else:
            s = MOMENT_SCALE[name]
        km, kv = _jax.random.split(_jax.random.fold_in(key, i + 1))
        out[name] = w
        out["m_" + name] = s * _jax.random.normal(km, w.shape, _jnp.float32)
        out["v_" + name] = (s * s) * _jax.random.uniform(kv, w.shape, _jnp.float32, 0.5, 1.5)
    if N_MICROBATCH > 1:
        for name, axis in PER_EXAMPLE_BATCH_AXIS.items():
            out[name] = _to_microbatches(out[name], axis)
    return {'x': out['x'], 'meta_tokens': out['meta_tokens'], 'norm_w': out['norm_w'], 'w_in': out['w_in'], 'conv_w': out['conv_w'], 'A_log': out['A_log'], 'dt_bias': out['dt_bias'], 'pool_mix': out['pool_mix'], 'pool_scale': out['pool_scale'], 'dn_norm_w': out['dn_norm_w'], 'w_pool_out': out['w_pool_out'], 'w_dn_out': out['w_dn_out'], 'w_o': out['w_o'], 'final_norm_w': out['final_norm_w'], 'loss_target': out['loss_target'], 'm_meta_tokens': out['m_meta_tokens'], 'm_norm_w': out['m_norm_w'], 'm_w_in': out['m_w_in'], 'm_conv_w': out['m_conv_w'], 'm_A_log': out['m_A_log'], 'm_dt_bias': out['m_dt_bias'], 'm_pool_mix': out['m_pool_mix'], 'm_pool_scale': out['m_pool_scale'], 'm_dn_norm_w': out['m_dn_norm_w'], 'm_w_pool_out': out['m_w_pool_out'], 'm_w_dn_out': out['m_w_dn_out'], 'm_w_o': out['m_w_o'], 'm_final_norm_w': out['m_final_norm_w'], 'v_meta_tokens': out['v_meta_tokens'], 'v_norm_w': out['v_norm_w'], 'v_w_in': out['v_w_in'], 'v_conv_w': out['v_conv_w'], 'v_A_log': out['v_A_log'], 'v_dt_bias': out['v_dt_bias'], 'v_pool_mix': out['v_pool_mix'], 'v_pool_scale': out['v_pool_scale'], 'v_dn_norm_w': out['v_dn_norm_w'], 'v_w_pool_out': out['v_w_pool_out'], 'v_w_dn_out': out['v_w_dn_out'], 'v_w_o': out['v_w_o'], 'v_final_norm_w': out['v_final_norm_w']}


def _loss(weights, diff, rest, loss_target):
    with _jax.named_scope("forward"):
        args = {**rest, TWIN_DIFF_INPUT: diff, **{k: w.astype(_WEIGHT_DTYPES[k]) for k, w in weights.items()}}
        y = _forward(args)
    with _jax.named_scope("loss_head"):
        err = _jnp.square(y.astype(_jnp.float32) - loss_target)
        return 0.5 * _jnp.sum(_jnp.mean(err, axis=-1)) if err.ndim else 0.5 * err


def _adamw(w, g, m, v):
    m = ADAM_B1 * m + (1.0 - ADAM_B1) * g
    v = ADAM_B2 * v + (1.0 - ADAM_B2) * _jnp.square(g)
    m_hat = m / (1.0 - ADAM_B1 ** ADAM_STEP)
    v_hat = v / (1.0 - ADAM_B2 ** ADAM_STEP)
    delta = -ADAM_LR * (m_hat / (_jnp.sqrt(v_hat) + ADAM_EPS) + ADAM_WD * w)
    return delta, m, v


def reference(x, meta_tokens, norm_w, w_in, conv_w, A_log, dt_bias, pool_mix, pool_scale, dn_norm_w, w_pool_out, w_dn_out, w_o, final_norm_w, loss_target, m_meta_tokens, m_norm_w, m_w_in, m_conv_w, m_A_log, m_dt_bias, m_pool_mix, m_pool_scale, m_dn_norm_w, m_w_pool_out, m_w_dn_out, m_w_o, m_final_norm_w, v_meta_tokens, v_norm_w, v_w_in, v_conv_w, v_A_log, v_dt_bias, v_pool_mix, v_pool_scale, v_dn_norm_w, v_w_pool_out, v_w_dn_out, v_w_o, v_final_norm_w):
    given = dict(x=x, meta_tokens=meta_tokens, norm_w=norm_w, w_in=w_in, conv_w=conv_w, A_log=A_log, dt_bias=dt_bias, pool_mix=pool_mix, pool_scale=pool_scale, dn_norm_w=dn_norm_w, w_pool_out=w_pool_out, w_dn_out=w_dn_out, w_o=w_o, final_norm_w=final_norm_w, loss_target=loss_target, m_meta_tokens=m_meta_tokens, m_norm_w=m_norm_w, m_w_in=m_w_in, m_conv_w=m_conv_w, m_A_log=m_A_log, m_dt_bias=m_dt_bias, m_pool_mix=m_pool_mix, m_pool_scale=m_pool_scale, m_dn_norm_w=m_dn_norm_w, m_w_pool_out=m_w_pool_out, m_w_dn_out=m_w_dn_out, m_w_o=m_w_o, m_final_norm_w=m_final_norm_w, v_meta_tokens=v_meta_tokens, v_norm_w=v_norm_w, v_w_in=v_w_in, v_conv_w=v_conv_w, v_A_log=v_A_log, v_dt_bias=v_dt_bias, v_pool_mix=v_pool_mix, v_pool_scale=v_pool_scale, v_dn_norm_w=v_dn_norm_w, v_w_pool_out=v_w_pool_out, v_w_dn_out=v_w_dn_out, v_w_o=v_w_o, v_final_norm_w=v_final_norm_w)
    weights = {n: given[n] for n in TWIN_WEIGHTS}
    shared = {n: given[n] for n in SHARED_INPUTS}
    per_example = {n: given[n] for n in ['x']}
    grad_fn = _jax.value_and_grad(_loss, argnums=(0, 1))

    def one_microbatch(ex, loss_target):
        ex = dict(ex)
        diff = ex.pop(TWIN_DIFF_INPUT)
        return grad_fn(weights, diff, {**shared, **ex}, loss_target)

    if N_MICROBATCH == 1:
        loss, (grad_w, grad_x) = one_microbatch(per_example, given["loss_target"])
    else:
        def body(carry, xs):
            loss_sum, grad_sum = carry
            l_k, (gw_k, gx_k) = one_microbatch(xs[0], xs[1])
            with _jax.named_scope("update"):
                return (loss_sum + l_k, _jax.tree.map(_jnp.add, grad_sum, gw_k)), gx_k

        init = (_jnp.zeros((), _jnp.float32), _jax.tree.map(_jnp.zeros_like, weights))
        (loss, grad_w), grad_x = _jax.lax.scan(body, init, (per_example, given["loss_target"]))
    with _jax.named_scope("update"):
        delta_w, new_m, new_v = {}, {}, {}
        for n in TWIN_WEIGHTS:
            delta_w[n], new_m[n], new_v[n] = _adamw(weights[n], grad_w[n], given["m_" + n], given["v_" + n])
    return (loss, grad_x, *[grad_w[n] for n in TWIN_WEIGHTS], *[delta_w[n] for n in TWIN_WEIGHTS],
            *[new_m[n] for n in TWIN_WEIGHTS], *[new_v[n] for n in TWIN_WEIGHTS])
```

```python
import functools

import jax
import jax.numpy as jnp
from jax import lax
from jax.experimental import pallas as pl
from jax.experimental.pallas import tpu as pltpu

f32 = jnp.float32
bf16 = jnp.bfloat16
HIGHEST = lax.Precision.HIGHEST
MESH = pl.DeviceIdType.MESH

N_META = 16
CHUNK = 64
FRONT_PAD = (-N_META) % CHUNK
HEAD_DIM = 128
POOL_GROUPS = 4
POOL_WINDOWS = (2, 4, 8, 16)
CONV_WIDTH = 4
NORM_EPS = 1e-6
ADAM_LR, ADAM_B1, ADAM_B2, ADAM_EPS, ADAM_WD, ADAM_STEP = 0.001, 0.9, 0.999, 1e-08, 0.01, 10
LANES = 128
VMEM_LIMIT = 56 * 2**20


def _pc(body, **kw):
    return pl.pallas_call(body, **kw)


def _params(*sem, **kw):
    return pltpu.CompilerParams(dimension_semantics=sem or None, vmem_limit_bytes=VMEM_LIMIT, **kw)


def _dg(a, b, dims, prec=None):
    return lax.dot_general(a, b, (dims, ((), ())), precision=prec, preferred_element_type=f32)


@jax.custom_vjp
def mm_nn(a, b):
    return _dg(a.astype(bf16), b.astype(bf16), ((1,), (0,)))


@jax.custom_vjp
def mm_nt(a, b):
    return _dg(a.astype(bf16), b.astype(bf16), ((1,), (1,)))


@jax.custom_vjp
def mm_tn(a, b):
    return _dg(a.astype(bf16).T, b.astype(bf16), ((1,), (0,)))


mm_nn.defvjp(lambda a, b: (mm_nn(a, b), (a, b)), lambda r, dy: (mm_nt(dy, r[1]), mm_tn(r[0], dy)))
mm_nt.defvjp(lambda a, b: (mm_nt(a, b), (a, b)), lambda r, dy: (mm_nn(dy, r[1]), mm_tn(dy, r[0])))
mm_tn.defvjp(lambda a, b: (mm_tn(a, b), (a, b)), lambda r, dy: (mm_nt(r[1], dy), mm_nn(r[0], dy)))


def mm_f32(a, b):
    return _dg(a, b, ((1,), (0,)), HIGHEST)


@jax.custom_vjp
def tri_inv(l):
    n = l.shape[0]
    eye = (lax.broadcasted_iota(jnp.int32, (n, n), 0) == lax.broadcasted_iota(jnp.int32, (n, n), 1)).astype(f32)
    m = -l
    t = eye + m
    k = 1
    while 2 * k < n:
        m = mm_f32(m, m)
        t = t + mm_f32(t, m)
        k *= 2
    return t


def _tri_inv_fwd(l):
    t = tri_inv(l)
    return t, t


def _tri_inv_bwd(t, dt):
    tt = t.T
    return (-mm_f32(mm_f32(tt, dt), tt),)


tri_inv.defvjp(_tri_inv_fwd, _tri_inv_bwd)


@functools.partial(jax.custom_vjp, nondiff_argnums=(1,))
def shift_rows(x, j):
    n = x.shape[0]
    rows = lax.broadcasted_iota(jnp.int32, x.shape, 0)
    if j >= 0:
        return jnp.where(rows >= j, pltpu.roll(x, j, 0), 0.0)
    return jnp.where(rows < n + j, pltpu.roll(x, n + j, 0), 0.0)


shift_rows.defvjp(lambda x, j: (shift_rows(x, j), None), lambda j, _, dy: (shift_rows(dy, -j),))


def sigmoid(x):
    return 1.0 / (1.0 + jnp.exp(-x))


def silu(x):
    return x * sigmoid(x)


def softplus(x):
    return jnp.maximum(x, 0.0) + jnp.log(1.0 + jnp.exp(-jnp.abs(x)))


def rmsnorm(x, w):
    return x * lax.rsqrt(jnp.mean(x * x, axis=-1, keepdims=True) + NORM_EPS) * w


def l2norm(x):
    return x * lax.rsqrt(jnp.sum(x * x, axis=-1, keepdims=True) + NORM_EPS)


def pool_fn(u, zp, mix, scale, group):
    rows = lax.broadcasted_iota(jnp.int32, u.shape, 0)
    sums = []
    s, w = u, 1
    while w < POOL_WINDOWS[-1]:
        s = s + shift_rows(s, w)
        w *= 2
        sums.append(s)
    total = sums[-1]
    for gi in range(POOL_GROUPS - 2, -1, -1):
        total = jnp.where(group == gi, sums[gi], total)
    window = jnp.left_shift(2, group)
    cnt = jnp.clip(rows - (FRONT_PAD - 1), 1, window).astype(f32)
    pooled = total / cnt - u
    return mm_nn(pooled, mix) * scale * silu(zp)


def conv_silu(x, w):
    k = CONV_WIDTH
    y = x * w[k - 1:k, :]
    for kk in range(k - 1):
        y = y + shift_rows(x, k - 1 - kk) * w[kk:kk + 1, :]
    return silu(y)


def _lane_pick(row, idx):
    lanes = lax.broadcasted_iota(jnp.int32, row.shape, 1)
    return jnp.sum(jnp.where(lanes == idx, row, 0.0), axis=1, keepdims=True)


def dn_pre_fn(qr, kr, vr, ba, cwq, cwk, cwv, a_log, dt_bias, head, n_heads):
    q = l2norm(conv_silu(qr, cwq)) * (HEAD_DIM ** -0.5)
    k = l2norm(conv_silu(kr, cwk))
    v = conv_silu(vr, cwv)
    r = lax.broadcasted_iota(jnp.int32, (LANES, LANES), 0)
    b_b = mm_f32(ba, (r == head).astype(f32))
    a_b = mm_f32(ba, (r == head + n_heads).astype(f32))
    real = lax.broadcasted_iota(jnp.int32, qr.shape, 0) >= FRONT_PAD
    beta_b = jnp.where(real, sigmoid(b_b), 0.0)
    g_b = jnp.where(real, -jnp.exp(_lane_pick(a_log, head)) * softplus(a_b + _lane_pick(dt_bias, head)), 0.0)
    return q, k, v, beta_b, g_b


def dn_chunk_fn(q, k, v, beta_b, g_b, zd, norm_w, state):
    c = CHUNK
    r = lax.broadcasted_iota(jnp.int32, (c, c), 0)
    cc = lax.broadcasted_iota(jnp.int32, (c, c), 1)
    causal = r >= cc
    strict = r > cc
    gcum_b = mm_f32(causal.astype(f32), g_b)
    gc = gcum_b[:, :c]
    decay = jnp.where(causal, jnp.exp(jnp.where(causal, gc - gc.T, 0.0)), 0.0)
    k_beta = k * beta_b
    v_beta = v * beta_b
    tmat = tri_inv(jnp.where(strict, mm_nt(k_beta, k) * decay, 0.0))
    u_c = mm_nn(tmat, v_beta)
    eg = jnp.exp(gcum_b)
    w_c = mm_nn(tmat, k_beta * eg)
    qk = jnp.where(causal, mm_nt(q, k) * decay, 0.0)
    glast = gcum_b[c - 1:c, :]
    k_dec = k * jnp.exp(glast - gcum_b)
    v_new = u_c - mm_nn(w_c, state)
    o = mm_nn(q * eg, state) + mm_nn(qk, v_new)
    new_state = state * jnp.exp(glast) + mm_tn(k_dec, v_new)
    y = rmsnorm(o, norm_w) * silu(zd)
    return y, new_state


def loss_fn(o, w, tgt):
    err = rmsnorm(o, w) - tgt
    return 0.5 * jnp.sum(jnp.mean(err * err, axis=-1))


def matmul(name, a, b, *, ta=False, tb=False, tm, tn, tk=None, extras=(), epi=None, out_dtypes=(f32,)):
    m, k = (a.shape[1], a.shape[0]) if ta else a.shape
    n = b.shape[0] if tb else b.shape[1]
    tm, tn, tk = min(tm, m), min(tn, n), min(tk or k, k)
    assert m % tm == 0 and n % tn == 0 and k % tk == 0, (name, m, n, k, tm, tn, tk)
    nm, nn, nk = m // tm, n // tn, k // tk
    a_spec = pl.BlockSpec((tk, tm), lambda i, j, kk: (kk, i)) if ta else pl.BlockSpec((tm, tk), lambda i, j, kk: (i, kk))
    b_spec = pl.BlockSpec((tn, tk), lambda i, j, kk: (j, kk)) if tb else pl.BlockSpec((tk, tn), lambda i, j, kk: (kk, j))
    ex_specs = []
    for _, off in extras:
        assert off % tn == 0, (name, off, tn)
        ex_specs.append(pl.BlockSpec((tm, tn), functools.partial(lambda i, j, kk, o: (i, o + j), o=off // tn)))
    n_ex, n_out = len(extras), len(out_dtypes)
    dims = ((0 if ta else 1,), (1 if tb else 0,))

    def body(a_ref, b_ref, *rest):
        ex_refs, out_refs = rest[:n_ex], rest[n_ex:n_ex + n_out]

        def finish(acc):
            res = epi(acc, *[r[...] for r in ex_refs]) if epi is not None else (acc,)
            for o_ref, val in zip(out_refs, res):
                o_ref[...] = val.astype(o_ref.dtype)

        p = _dg(a_ref[...], b_ref[...], dims)
        if nk == 1:
            finish(p)
        else:
            acc_ref = rest[-1]
            kk = pl.program_id(2)

            @pl.when(kk == 0)
            def _():
                acc_ref[...] = p

            @pl.when(kk > 0)
            def _():
                acc_ref[...] += p

            @pl.when(kk == nk - 1)
            def _():
                finish(acc_ref[...])

    outs = _pc(
        body, name=name, grid=(nm, nn, nk),
        in_specs=[a_spec, b_spec] + ex_specs,
        out_specs=[pl.BlockSpec((tm, tn), lambda i, j, kk: (i, j))] * n_out,
        out_shape=[jax.ShapeDtypeStruct((m, n), dt) for dt in out_dtypes],
        scratch_shapes=[pltpu.VMEM((tm, tn), f32)] if nk > 1 else [],
        compiler_params=_params("parallel", "parallel", "arbitrary"),
    )(a, b, *[e for e, _ in extras])
    return outs[0] if n_out == 1 else outs


def _row_tile(rows, cols, n_arrays, itemsize=4, budget=24 * 2**20):
    best = None
    for t in range(16, rows + 1, 16):
        if rows % t == 0 and 2 * n_arrays * t * cols * itemsize <= budget:
            best = t
    return best or rows


def cast_bf16(name, x):
    rows, cols = x.shape
    t = _row_tile(rows, cols, 2)

    def body(x_ref, o_ref):
        o_ref[...] = x_ref[...].astype(bf16)

    return _pc(body, name=name, grid=(rows // t,), in_specs=[pl.BlockSpec((t, cols), lambda i: (i, 0))],
               out_specs=pl.BlockSpec((t, cols), lambda i: (i, 0)), out_shape=jax.ShapeDtypeStruct(x.shape, bf16),
               compiler_params=_params("parallel"))(x)


def norm_fwd(hp, norm_w, tm):
    tp, d = hp.shape

    def body(h_ref, w_ref, o_ref):
        o_ref[...] = rmsnorm(h_ref[...], w_ref[...]).astype(bf16)

    return _pc(body, name="norm_fwd", grid=(tp // tm,),
               in_specs=[pl.BlockSpec((tm, d), lambda i: (i, 0)), pl.BlockSpec((1, d), lambda i: (0, 0))],
               out_specs=pl.BlockSpec((tm, d), lambda i: (i, 0)), out_shape=jax.ShapeDtypeStruct((tp, d), bf16),
               compiler_params=_params("parallel"))(hp, norm_w)


def norm_bwd(hp, norm_w, dxn, dout, tm):
    tp, d = hp.shape

    def body(h_ref, w_ref, dxn_ref, dout_ref, dh_ref, dw_ref):
        _, vjp = jax.vjp(rmsnorm, h_ref[...], w_ref[...])
        dh, dw = vjp(dxn_ref[...])
        dh_ref[...] = dh + dout_ref[...]

        @pl.when(pl.program_id(0) == 0)
        def _():
            dw_ref[...] = jnp.zeros_like(dw_ref)

        dw_ref[...] += dw

    row = pl.BlockSpec((tm, d), lambda i: (i, 0))
    vec = pl.BlockSpec((1, d), lambda i: (0, 0))
    return _pc(body, name="norm_bwd", grid=(tp // tm,), in_specs=[row, vec, row, row], out_specs=[row, vec],
               out_shape=[jax.ShapeDtypeStruct((tp, d), f32), jax.ShapeDtypeStruct((1, d), f32)],
               compiler_params=_params("arbitrary"))(hp, norm_w, dxn, dout)


def pool_fwd(proj, mix, scale, pw):
    tp = proj.shape[0]
    g = pw // POOL_GROUPS

    def body(u_ref, z_ref, mix_ref, s_ref, y_ref):
        y_ref[...] = pool_fn(u_ref[...], z_ref[...], mix_ref[0], s_ref[...], pl.program_id(0)).astype(bf16)

    return _pc(body, name="pool_fwd", grid=(POOL_GROUPS,),
               in_specs=[pl.BlockSpec((tp, g), lambda i: (0, i)), pl.BlockSpec((tp, g), lambda i: (0, POOL_GROUPS + i)),
                         pl.BlockSpec((1, g, g), lambda i: (i, 0, 0)), pl.BlockSpec((1, g), lambda i: (0, i))],
               out_specs=pl.BlockSpec((tp, g), lambda i: (0, i)), out_shape=jax.ShapeDtypeStruct((tp, pw), bf16),
               compiler_params=_params("parallel"))(proj, proj, mix, scale)


def pool_bwd(proj, mix, scale, dy, pw):
    tp = proj.shape[0]
    g = pw // POOL_GROUPS

    def body(u_ref, z_ref, mix_ref, s_ref, dy_ref, du_ref, dz_ref, dmix_ref, ds_ref):
        grp = pl.program_id(0)
        _, vjp = jax.vjp(lambda u, z, m, s: pool_fn(u, z, m, s, grp), u_ref[...], z_ref[...], mix_ref[0].astype(f32), s_ref[...])
        du, dz, dmix, ds = vjp(dy_ref[...])
        du_ref[...] = du.astype(bf16)
        dz_ref[...] = dz.astype(bf16)
        dmix_ref[0] = dmix
        ds_ref[...] = ds

    col = pl.BlockSpec((tp, g), lambda i: (0, i))
    return _pc(body, name="pool_bwd", grid=(POOL_GROUPS,),
               in_specs=[col, pl.BlockSpec((tp, g), lambda i: (0, POOL_GROUPS + i)),
                         pl.BlockSpec((1, g, g), lambda i: (i, 0, 0)), pl.BlockSpec((1, g), lambda i: (0, i)), col],
               out_specs=[col, col, pl.BlockSpec((1, g, g), lambda i: (i, 0, 0)), pl.BlockSpec((1, g), lambda i: (0, i))],
               out_shape=[jax.ShapeDtypeStruct((tp, pw), bf16), jax.ShapeDtypeStruct((tp, pw), bf16),
                          jax.ShapeDtypeStruct((POOL_GROUPS, g, g), f32), jax.ShapeDtypeStruct((1, pw), f32)],
               compiler_params=_params("parallel"))(proj, proj, mix, scale, dy)


def _dn_pre_specs(tp, n_heads, q_off):
    hb = lambda off: pl.BlockSpec((tp, HEAD_DIM), functools.partial(lambda h, o: (0, o + h), o=off))
    cw = lambda off: pl.BlockSpec((CONV_WIDTH, HEAD_DIM), functools.partial(lambda h, o: (0, o + h), o=off))
    whole = lambda shape: pl.BlockSpec(shape, lambda h: (0, 0))
    return ([hb(q_off), hb(q_off + n_heads), hb(q_off + 2 * n_heads), whole((tp, LANES)),
             cw(0), cw(n_heads), cw(2 * n_heads), whole((1, LANES)), whole((1, LANES))], hb, cw, whole)


def dn_pre_fwd(proj, ba, conv_w, a_log, dt_bias, n_heads, q_off):
    tp = proj.shape[0]
    in_specs, hb, _, _ = _dn_pre_specs(tp, n_heads, q_off)

    def body(q_ref, k_ref, v_ref, ba_ref, cq_ref, ck_ref, cv_ref, al_ref, dt_ref, *out_refs):
        outs = dn_pre_fn(q_ref[...], k_ref[...], v_ref[...], ba_ref[...], cq_ref[...], ck_ref[...], cv_ref[...],
                         al_ref[...], dt_ref[...], pl.program_id(0), n_heads)
        for o_ref, val in zip(out_refs, outs):
            o_ref[...] = val

    return _pc(body, name="dn_pre_fwd", grid=(n_heads,), in_specs=in_specs, out_specs=[hb(0)] * 5,
               out_shape=[jax.ShapeDtypeStruct((tp, n_heads * HEAD_DIM), f32)] * 5,
               compiler_params=_params("parallel"))(proj, proj, proj, ba, conv_w, conv_w, conv_w, a_log, dt_bias)


def dn_pre_bwd(proj, ba, conv_w, a_log, dt_bias, cots, n_heads, q_off):
    tp = proj.shape[0]
    in_specs, hb, cw, whole = _dn_pre_specs(tp, n_heads, q_off)

    def body(q_ref, k_ref, v_ref, ba_ref, cq_ref, ck_ref, cv_ref, al_ref, dt_ref, dq_ref, dk_ref, dv_ref, db_ref, dg_ref,
             dqr_ref, dkr_ref, dvr_ref, dba_ref, dcq_ref, dck_ref, dcv_ref, dal_ref, ddt_ref):
        head = pl.program_id(0)
        fn = lambda *args: dn_pre_fn(*args, head, n_heads)
        _, vjp = jax.vjp(fn, q_ref[...], k_ref[...], v_ref[...], ba_ref[...], cq_ref[...], ck_ref[...], cv_ref[...],
                         al_ref[...], dt_ref[...])
        dqr, dkr, dvr, dba, dcq, dck, dcv, dal, ddt = vjp((dq_ref[...], dk_ref[...], dv_ref[...], db_ref[...], dg_ref[...]))
        dqr_ref[...] = dqr.astype(bf16)
        dkr_ref[...] = dkr.astype(bf16)
        dvr_ref[...] = dvr.astype(bf16)
        dcq_ref[...] = dcq
        dck_ref[...] = dck
        dcv_ref[...] = dcv

        @pl.when(head == 0)
        def _():
            dba_ref[...] = jnp.zeros_like(dba_ref)
            dal_ref[...] = jnp.zeros_like(dal_ref)
            ddt_ref[...] = jnp.zeros_like(ddt_ref)

        dba_ref[...] += dba
        dal_ref[...] += dal
        ddt_ref[...] += ddt

    w = n_heads * HEAD_DIM
    return _pc(body, name="dn_pre_bwd", grid=(n_heads,), in_specs=in_specs + [hb(0)] * 5,
               out_specs=[hb(0)] * 3 + [whole((tp, LANES)), cw(0), cw(0), cw(0), whole((1, LANES)), whole((1, LANES))],
               out_shape=[jax.ShapeDtypeStruct((tp, w), bf16)] * 3 + [jax.ShapeDtypeStruct((tp, LANES), f32)]
               + [jax.ShapeDtypeStruct((CONV_WIDTH, w), f32)] * 3 + [jax.ShapeDtypeStruct((1, LANES), f32)] * 2,
               compiler_params=_params("arbitrary"))(proj, proj, proj, ba, conv_w, conv_w, conv_w, a_log, dt_bias, *cots)


def dn_scan_fwd(q, k, v, beta_b, g_b, proj, dn_norm_w, n_heads, zd_off):
    tp = q.shape[0]
    n_chunks = tp // CHUNK

    def body(q_ref, k_ref, v_ref, b_ref, g_ref, z_ref, w_ref, y_ref, s_ref, state):
        @pl.when(pl.program_id(1) == 0)
        def _():
            state[...] = jnp.zeros_like(state)

        s_ref[0, 0] = state[...]
        y, new_state = dn_chunk_fn(q_ref[...], k_ref[...], v_ref[...], b_ref[...], g_ref[...], z_ref[...], w_ref[...], state[...])
        y_ref[...] = y.astype(bf16)
        state[...] = new_state

    blk = pl.BlockSpec((CHUNK, HEAD_DIM), lambda h, n: (n, h))
    return _pc(body, name="dn_scan_fwd", grid=(n_heads, n_chunks),
               in_specs=[blk] * 5 + [pl.BlockSpec((CHUNK, HEAD_DIM), lambda h, n: (n, zd_off + h)),
                                     pl.BlockSpec((1, HEAD_DIM), lambda h, n: (0, 0))],
               out_specs=[blk, pl.BlockSpec((1, 1, HEAD_DIM, HEAD_DIM), lambda h, n: (n, h, 0, 0))],
               out_shape=[jax.ShapeDtypeStruct(q.shape, bf16), jax.ShapeDtypeStruct((n_chunks, n_heads, HEAD_DIM, HEAD_DIM), f32)],
               scratch_shapes=[pltpu.VMEM((HEAD_DIM, HEAD_DIM), f32)],
               compiler_params=_params("arbitrary", "arbitrary"))(q, k, v, beta_b, g_b, proj, dn_norm_w)


def dn_scan_bwd(q, k, v, beta_b, g_b, proj, dn_norm_w, states, dy, n_heads, zd_off):
    tp = q.shape[0]
    n_chunks = tp // CHUNK

    def body(q_ref, k_ref, v_ref, b_ref, g_ref, z_ref, w_ref, s_ref, dy_ref,
             dq_ref, dk_ref, dv_ref, db_ref, dg_ref, dz_ref, dw_ref, dstate):
        first = jnp.logical_and(pl.program_id(0) == 0, pl.program_id(1) == 0)

        @pl.when(pl.program_id(1) == 0)
        def _():
            dstate[...] = jnp.zeros_like(dstate)

        @pl.when(first)
        def _():
            dw_ref[...] = jnp.zeros_like(dw_ref)

        _, vjp = jax.vjp(dn_chunk_fn, q_ref[...], k_ref[...], v_ref[...], b_ref[...], g_ref[...], z_ref[...], w_ref[...], s_ref[0, 0])
        dq, dk, dv, db, dg, dz, dw, ds = vjp((dy_ref[...], dstate[...]))
        dq_ref[...] = dq
        dk_ref[...] = dk
        dv_ref[...] = dv
        db_ref[...] = db
        dg_ref[...] = dg
        dz_ref[...] = dz.astype(bf16)
        dw_ref[...] += dw
        dstate[...] = ds

    last = n_chunks - 1
    blk = pl.BlockSpec((CHUNK, HEAD_DIM), lambda h, n: (last - n, h))
    return _pc(body, name="dn_scan_bwd", grid=(n_heads, n_chunks),
               in_specs=[blk] * 5 + [pl.BlockSpec((CHUNK, HEAD_DIM), lambda h, n: (last - n, zd_off + h)),
                                     pl.BlockSpec((1, HEAD_DIM), lambda h, n: (0, 0)),
                                     pl.BlockSpec((1, 1, HEAD_DIM, HEAD_DIM), lambda h, n: (last - n, h, 0, 0)), blk],
               out_specs=[blk] * 6 + [pl.BlockSpec((1, HEAD_DIM), lambda h, n: (0, 0))],
               out_shape=[jax.ShapeDtypeStruct(q.shape, f32)] * 5 + [jax.ShapeDtypeStruct(q.shape, bf16),
                                                                      jax.ShapeDtypeStruct((1, HEAD_DIM), f32)],
               scratch_shapes=[pltpu.VMEM((HEAD_DIM, HEAD_DIM), f32)],
               compiler_params=_params("arbitrary", "arbitrary"))(q, k, v, beta_b, g_b, proj, dn_norm_w, states, dy)


def loss_stage(out, final_w, target):
    tp, d = out.shape
    n_tiles = tp // CHUNK

    def body(o_ref, w_ref, t_ref, loss_ref, do_ref, dob_ref, dw_ref):
        i = pl.program_id(0)

        @pl.when(i == 0)
        def _():
            loss_ref[...] = jnp.zeros_like(loss_ref)
            dw_ref[...] = jnp.zeros_like(dw_ref)

        scored = (i > 0).astype(f32)
        val, (do, dw) = jax.value_and_grad(lambda o, w: scored * loss_fn(o, w, t_ref[...]), argnums=(0, 1))(o_ref[...], w_ref[...])
        loss_ref[...] += jnp.full(loss_ref.shape, val, f32)
        do_ref[...] = do
        dob_ref[...] = do.astype(bf16)
        dw_ref[...] += dw

    row = pl.BlockSpec((CHUNK, d), lambda i: (i, 0))
    vec = pl.BlockSpec((1, d), lambda i: (0, 0))
    return _pc(body, name="loss_stage", grid=(n_tiles,),
               in_specs=[row, vec, pl.BlockSpec((CHUNK, d), lambda i: (jnp.maximum(i - 1, 0), 0))],
               out_specs=[pl.BlockSpec((1, LANES), lambda i: (0, 0)), row, row, vec],
               out_shape=[jax.ShapeDtypeStruct((1, LANES), f32), jax.ShapeDtypeStruct((tp, d), f32),
                          jax.ShapeDtypeStruct((tp, d), bf16), jax.ShapeDtypeStruct((1, d), f32)],
               compiler_params=_params("arbitrary"))(out, final_w, target)


def adamw(name, w, g, m, v):
    rows, cols = w.shape
    t = _row_tile(rows, cols, 7)

    def body(w_ref, g_ref, m_ref, v_ref, d_ref, nm_ref, nv_ref):
        gg = g_ref[...]
        nm = ADAM_B1 * m_ref[...] + (1.0 - ADAM_B1) * gg
        nv = ADAM_B2 * v_ref[...] + (1.0 - ADAM_B2) * jnp.square(gg)
        m_hat = nm / (1.0 - ADAM_B1 ** ADAM_STEP)
        v_hat = nv / (1.0 - ADAM_B2 ** ADAM_STEP)
        d_ref[...] = -ADAM_LR * (m_hat / (jnp.sqrt(v_hat) + ADAM_EPS) + ADAM_WD * w_ref[...])
        nm_ref[...] = nm
        nv_ref[...] = nv

    blk = pl.BlockSpec((t, cols), lambda i: (i, 0))
    return _pc(body, name=name, grid=(rows // t,), in_specs=[blk] * 4, out_specs=[blk] * 3,
               out_shape=[jax.ShapeDtypeStruct(w.shape, f32)] * 3, compiler_params=_params("parallel"))(w, g, m, v)


def add_halves(name, g, recv, half):
    _, s, r, c = g.shape
    t = _row_tile(r, c, 3)

    def body(half_ref, g_ref, r_ref, o_ref):
        o_ref[...] = (g_ref[0] + r_ref[...]).astype(bf16)

    grid_spec = pltpu.PrefetchScalarGridSpec(
        num_scalar_prefetch=1, grid=(s, r // t),
        in_specs=[pl.BlockSpec((1, 1, t, c), lambda i, j, hf: (hf[0], i, j, 0)), pl.BlockSpec((1, t, c), lambda i, j, hf: (i, j, 0))],
        out_specs=pl.BlockSpec((1, t, c), lambda i, j, hf: (i, j, 0)))

    return _pc(body, name=name, grid_spec=grid_spec, out_shape=jax.ShapeDtypeStruct((s, r, c), bf16),
               compiler_params=_params("parallel", "parallel"))(half, g, recv)


def sum_leading(name, x, out_dtype=f32):
    s, r, c = x.shape
    t = _row_tile(r, c, s + 1)

    def body(x_ref, o_ref):
        acc = x_ref[0].astype(f32)
        for i in range(1, s):
            acc = acc + x_ref[i].astype(f32)
        o_ref[...] = acc.astype(out_dtype)

    return _pc(body, name=name, grid=(r // t,), in_specs=[pl.BlockSpec((s, t, c), lambda i: (0, i, 0))],
               out_specs=pl.BlockSpec((t, c), lambda i: (i, 0)), out_shape=jax.ShapeDtypeStruct((r, c), out_dtype),
               compiler_params=_params("parallel"))(x)


def _place():
    x, y, c = lax.axis_index("x"), lax.axis_index("y"), lax.axis_index("c")
    return x, y, c


def _other_chips(x, y):
    return [(1 - x, y), (x, 1 - y), (1 - x, 1 - y)]


_ANY = pl.BlockSpec(memory_space=pl.ANY)


def gather_weights(shards):
    n = len(shards)

    def body(*refs):
        ins, outs = refs[:n], refs[n:2 * n]
        send1, recv1, send2, recv2, local = refs[2 * n:]
        x, y, c = _place()
        chip = 2 * x + y
        sibling = (x, y, 1 - c)
        others = _other_chips(x, y)

        def remote(src, dst, s_sem, r_sem, to):
            return pltpu.make_async_remote_copy(src_ref=src, dst_ref=dst, send_sem=s_sem, recv_sem=r_sem,
                                                device_id=to, device_id_type=MESH)

        own = [pltpu.make_async_copy(ins[a], outs[a].at[chip], local.at[a]) for a in range(n)]
        for cp in own:
            cp.start()
        first = []
        for a in range(n):
            for j, (ox, oy) in enumerate(others):
                first.append(remote(ins[a].at[c], outs[a].at[chip, c], send1.at[a, j], recv1.at[a, j], (ox, oy, c)))
        for cp in first:
            cp.start()
        passed = []
        for a in range(n):
            for j, (ox, oy) in enumerate(others):
                slab = outs[a].at[2 * ox + oy, c]
                remote(slab, slab, send1.at[a, j], recv1.at[a, j], (ox, oy, c)).wait_recv()
                cp = remote(slab, slab, send2.at[a, j], recv2.at[a, j], sibling)
                cp.start()
                passed.append(cp)
        for a in range(n):
            for j, (ox, oy) in enumerate(others):
                slab = outs[a].at[2 * ox + oy, 1 - c]
                remote(slab, slab, send2.at[a, j], recv2.at[a, j], sibling).wait_recv()
        for cp in first + passed:
            cp.wait_send()
        for cp in own:
            cp.wait()

    sems = [pltpu.SemaphoreType.DMA((n, 3))] * 4 + [pltpu.SemaphoreType.DMA((n,))]
    return _pc(body, name="gather_weights", in_specs=[_ANY] * n, out_specs=[_ANY] * n,
               out_shape=[jax.ShapeDtypeStruct((4,) + s.shape, s.dtype) for s in shards],
               scratch_shapes=sems)(*shards)


def swap_with_sibling(name, sends):
    n = len(sends)

    def body(*refs):
        ins, outs = refs[:n], refs[n:2 * n]
        send, recv = refs[2 * n:]
        x, y, c = _place()
        cps = [pltpu.make_async_remote_copy(src_ref=ins[a], dst_ref=outs[a], send_sem=send.at[a], recv_sem=recv.at[a],
                                            device_id=(x, y, 1 - c), device_id_type=MESH) for a in range(n)]
        for cp in cps:
            cp.start()
        for cp in cps:
            cp.wait()

    return _pc(body, name=name, in_specs=[_ANY] * n, out_specs=[_ANY] * n,
               out_shape=[jax.ShapeDtypeStruct(s.shape, s.dtype) for s in sends],
               scratch_shapes=[pltpu.SemaphoreType.DMA((n,))] * 2)(*sends)


def send_grad_halves(grads):
    n = len(grads)

    def body(*refs):
        ins, outs = refs[:n], refs[n:2 * n]
        send, recv = refs[2 * n:]
        x, y, c = _place()
        cps = [pltpu.make_async_remote_copy(src_ref=ins[a].at[1 - c], dst_ref=outs[a], send_sem=send.at[a], recv_sem=recv.at[a],
                                            device_id=(x, y, 1 - c), device_id_type=MESH) for a in range(n)]
        for cp in cps:
            cp.start()
        for cp in cps:
            cp.wait()

    return _pc(body, name="send_grad_halves", in_specs=[_ANY] * n, out_specs=[_ANY] * n,
               out_shape=[jax.ShapeDtypeStruct(g.shape[1:], g.dtype) for g in grads],
               scratch_shapes=[pltpu.SemaphoreType.DMA((n,))] * 2)(*grads)


def scatter_to_chips(parts):
    n = len(parts)

    def body(*refs):
        ins, outs = refs[:n], refs[n:2 * n]
        send, recv, local = refs[2 * n:]
        x, y, c = _place()
        chip = 2 * x + y
        others = _other_chips(x, y)
        own = [pltpu.make_async_copy(ins[a].at[chip], outs[a].at[chip], local.at[a]) for a in range(n)]
        for cp in own:
            cp.start()
        cps = []
        for a in range(n):
            for j, (ox, oy) in enumerate(others):
                cps.append(pltpu.make_async_remote_copy(
                    src_ref=ins[a].at[2 * ox + oy], dst_ref=outs[a].at[chip], send_sem=send.at[a, j], recv_sem=recv.at[a, j],
                    device_id=(ox, oy, c), device_id_type=MESH))
        for cp in cps:
            cp.start()
        for a in range(n):
            for j, (ox, oy) in enumerate(others):
                slab = outs[a].at[2 * ox + oy]
                pltpu.make_async_remote_copy(src_ref=slab, dst_ref=slab, send_sem=send.at[a, j], recv_sem=recv.at[a, j],
                                             device_id=(ox, oy, c), device_id_type=MESH).wait_recv()
        for cp in cps:
            cp.wait_send()
        for cp in own:
            cp.wait()

    return _pc(body, name="scatter_to_chips", in_specs=[_ANY] * n, out_specs=[_ANY] * n,
               out_shape=[jax.ShapeDtypeStruct(p.shape, p.dtype) for p in parts],
               scratch_shapes=[pltpu.SemaphoreType.DMA((n, 3))] * 2 + [pltpu.SemaphoreType.DMA((n,))])(*parts)


def join_halves(halves):
    n = len(halves)

    def body(*refs):
        ins, outs = refs[:n], refs[n:2 * n]
        send, recv, local = refs[2 * n:]
        x, y, c = _place()
        own = [pltpu.make_async_copy(ins[a], outs[a].at[c], local.at[a]) for a in range(n)]
        for cp in own:
            cp.start()
        cps = [pltpu.make_async_remote_copy(src_ref=ins[a], dst_ref=outs[a].at[c], send_sem=send.at[a], recv_sem=recv.at[a],
                                            device_id=(x, y, 1 - c), device_id_type=MESH) for a in range(n)]
        for cp in cps:
            cp.start()
        for a in range(n):
            slab = outs[a].at[1 - c]
            pltpu.make_async_remote_copy(src_ref=slab, dst_ref=slab, send_sem=send.at[a], recv_sem=recv.at[a],
                                         device_id=(x, y, 1 - c), device_id_type=MESH).wait_recv()
        for cp in cps:
            cp.wait_send()
        for cp in own:
            cp.wait()

    return _pc(body, name="join_halves", in_specs=[_ANY] * n, out_specs=[_ANY] * n,
               out_shape=[jax.ShapeDtypeStruct((2,) + h.shape, h.dtype) for h in halves],
               scratch_shapes=[pltpu.SemaphoreType.DMA((n,))] * 2 + [pltpu.SemaphoreType.DMA((n,))])(*halves)


def gather_from_all(pack):
    def body(in_ref, out_ref, send, recv, local):
        x, y, c = _place()
        me = 4 * x + 2 * y + c
        own = pltpu.make_async_copy(in_ref, out_ref.at[me], local)
        own.start()
        flips = [(fx, fy, fc) for fx in (0, 1) for fy in (0, 1) for fc in (0, 1)][1:]
        peers = [(jnp.where(fx, 1 - x, x), jnp.where(fy, 1 - y, y), jnp.where(fc, 1 - c, c)) for fx, fy, fc in flips]
        cps = [pltpu.make_async_remote_copy(src_ref=in_ref, dst_ref=out_ref.at[me], send_sem=send.at[j], recv_sem=recv.at[j],
                                            device_id=p, device_id_type=MESH) for j, p in enumerate(peers)]
        for cp in cps:
            cp.start()
        for j, (px, py, pc) in enumerate(peers):
            slab = out_ref.at[4 * px + 2 * py + pc]
            pltpu.make_async_remote_copy(src_ref=slab, dst_ref=slab, send_sem=send.at[j], recv_sem=recv.at[j],
                                         device_id=(px, py, pc), device_id_type=MESH).wait_recv()
        for cp in cps:
            cp.wait_send()
        own.wait()

    return _pc(body, name="gather_from_all", in_specs=[_ANY], out_specs=_ANY,
               out_shape=jax.ShapeDtypeStruct((8,) + pack.shape, pack.dtype),
               scratch_shapes=[pltpu.SemaphoreType.DMA((7,)), pltpu.SemaphoreType.DMA((7,)), pltpu.SemaphoreType.DMA])(pack)


def _pack(arrays, width):
    rows = []
    for a in arrays:
        flat = a.reshape(-1).astype(f32)
        pad = (-flat.shape[0]) % (8 * width)
        rows.append(jnp.pad(flat, (0, pad)).reshape(-1, width))
    return jnp.concatenate(rows, axis=0)


def _unpack(pack, like, width):
    out, row = [], 0
    for a in like:
        size = 1
        for s in a.shape:
            size *= s
        n_rows = -(-size // (8 * width)) * 8
        out.append(pack[row:row + n_rows].reshape(-1)[:size].reshape(a.shape))
        row += n_rows
    return out


def _halves(a2d):
    r, c = a2d.shape
    return a2d.reshape(2, r // 2, c)


def local_step(x, meta, norm_w, w_main, w_ba, conv_w, a_log, dt_bias, pool_mix, pool_scale, dn_norm_w,
               w_pool_out, w_dn_out, w_o, final_w, target):
    seq, d = x.shape
    pw = pool_scale.shape[1]
    dw = w_dn_out.shape[0]
    n_heads = dw // HEAD_DIM
    tp = FRONT_PAD + N_META + seq
    hp = jnp.concatenate([jnp.zeros((FRONT_PAD, d), f32), meta, x], axis=0)
    tm_big = tp // 2 if tp % 32 == 0 else tp
    tm_norm = max(t for t in range(16, min(tp, 352) + 1, 16) if tp % t == 0)
    tile = min(512, d)
    off_q = 2 * pw
    off_zd = off_q + 3 * dw
    off_gp = off_zd + dw
    off_gd = off_gp + d
    a_log128 = jnp.pad(a_log, ((0, 0), (0, LANES - n_heads)))
    dt128 = jnp.pad(dt_bias, ((0, 0), (0, LANES - n_heads)))

    xn = norm_fwd(hp, norm_w, tm_norm)
    proj = matmul("proj", xn, w_main, tm=tp, tn=tile)
    ba = matmul("proj_ba", xn, w_ba, tm=tp, tn=LANES)
    y_pool = pool_fwd(proj, pool_mix, pool_scale, pw)
    q, k, v, beta_b, g_b = dn_pre_fwd(proj, ba, conv_w, a_log128, dt128, n_heads, off_q // HEAD_DIM)
    y_dn, states = dn_scan_fwd(q, k, v, beta_b, g_b, proj, dn_norm_w, n_heads, off_zd // HEAD_DIM)
    a_mat = matmul("pool_out", y_pool, w_pool_out, tm=tp, tn=tile)

    def merge(acc, a_t, gp_t, gd_t):
        return acc, sigmoid(gp_t) * a_t + sigmoid(gd_t) * acc

    b_mat, merged = matmul("dn_out_merge", y_dn, w_dn_out, tm=tm_big, tn=tile, extras=[(a_mat, 0), (proj, off_gp), (proj, off_gd)],
                           epi=merge, out_dtypes=(f32, bf16))
    out = matmul("out_proj", merged, w_o, tm=tm_big, tn=tile, extras=[(hp, 0)], epi=lambda acc, h_t: (acc + h_t,))
    loss, dout, dout_b, dfinal_w = loss_stage(out, final_w, target)

    def unmerge(dm, a_t, b_t, gp_t, gd_t):
        sp, sd = sigmoid(gp_t), sigmoid(gd_t)
        return dm * sp, dm * sd, dm * a_t * sp * (1.0 - sp), dm * b_t * sd * (1.0 - sd)

    d_a, d_b, d_gp, d_gd = matmul("d_merged", dout_b, w_o, tb=True, tm=tm_big, tn=tile,
                                  extras=[(a_mat, 0), (b_mat, 0), (proj, off_gp), (proj, off_gd)], epi=unmerge,
                                  out_dtypes=(bf16,) * 4)
    g_w_o = matmul("g_w_o", merged, dout_b, ta=True, tm=tile, tn=tile)
    d_y_pool = matmul("d_y_pool", d_a, w_pool_out, tb=True, tm=tp, tn=tile)
    g_w_pool_out = matmul("g_w_pool_out", y_pool, d_a, ta=True, tm=tile, tn=tile)
    d_y_dn = matmul("d_y_dn", d_b, w_dn_out, tb=True, tm=tp, tn=tile)
    g_w_dn_out = matmul("g_w_dn_out", y_dn, d_b, ta=True, tm=tile, tn=tile)
    d_u, d_zp, g_pool_mix, g_pool_scale = pool_bwd(proj, pool_mix, pool_scale, d_y_pool, pw)
    d_q, d_k, d_v, d_beta, d_g, d_zd, g_dn_norm_w = dn_scan_bwd(q, k, v, beta_b, g_b, proj, dn_norm_w, states, d_y_dn,
                                                                 n_heads, off_zd // HEAD_DIM)
    d_qr, d_kr, d_vr, d_ba, g_cq, g_ck, g_cv, g_a_log, g_dt = dn_pre_bwd(
        proj, ba, conv_w, a_log128, dt128, (d_q, d_k, d_v, d_beta, d_g), n_heads, off_q // HEAD_DIM)
    d_proj = jnp.concatenate([d_u, d_zp, d_qr, d_kr, d_vr, d_zd, d_gp, d_gd], axis=1)
    d_ba_b = cast_bf16("cast_d_ba", d_ba)
    dxn_ba = matmul("dxn_ba", d_ba_b, w_ba, tb=True, tm=tp, tn=tile)
    dxn = matmul("dxn", d_proj, w_main, tb=True, tm=tm_big, tn=tile, tk=min(2048, d_proj.shape[1]),
                 extras=[(dxn_ba, 0)], epi=lambda acc, e: (acc + e,))
    g_w_main = matmul("g_w_main", xn, d_proj, ta=True, tm=tile, tn=tile)
    g_w_ba = matmul("g_w_ba", xn, d_ba_b, ta=True, tm=tile, tn=LANES)
    dh, g_norm_w = norm_bwd(hp, norm_w, dxn, dout, tm_norm)
    g_conv = jnp.concatenate([g_cq, g_ck, g_cv], axis=1)
    return (loss, dh, g_norm_w, g_w_main, g_w_ba, g_conv, g_a_log[:, :n_heads], g_dt[:, :n_heads], g_pool_mix, g_pool_scale,
            g_dn_norm_w, g_w_pool_out, g_w_dn_out, g_w_o, dfinal_w)


def kernel(x, meta_tokens, norm_w, w_in, conv_w, A_log, dt_bias, pool_mix, pool_scale, dn_norm_w, w_pool_out, w_dn_out, w_o, final_norm_w, loss_target, m_meta_tokens, m_norm_w, m_w_in, m_conv_w, m_A_log, m_dt_bias, m_pool_mix, m_pool_scale, m_dn_norm_w, m_w_pool_out, m_w_dn_out, m_w_o, m_final_norm_w, v_meta_tokens, v_norm_w, v_w_in, v_conv_w, v_A_log, v_dt_bias, v_pool_mix, v_pool_scale, v_dn_norm_w, v_w_pool_out, v_w_dn_out, v_w_o, v_final_norm_w):
    d = x.shape[-1]
    pw = pool_scale.shape[-1]
    dw = w_dn_out.shape[1] * 4
    n_heads = dw // HEAD_DIM
    gdim = pw // POOL_GROUPS
    chip = 2 * lax.axis_index("x") + lax.axis_index("y")
    core = lax.axis_index("c")

    w_in_s, w_po_s, w_do_s, w_o_s = w_in[0], w_pool_out[0], w_dn_out[0], w_o[0]
    mix_s = pool_mix[0].reshape(POOL_GROUPS * (gdim // 4), gdim)
    small_s = _pack([meta_tokens, conv_w[0]], d)
    small_rows = small_s.shape[0]
    small_s = jnp.pad(small_s, ((0, (-small_rows) % 16), (0, 0)))

    shards = [_halves(cast_bf16("cast_w_in", w_in_s)), _halves(cast_bf16("cast_w_po", w_po_s)),
              _halves(cast_bf16("cast_w_do", w_do_s)), _halves(cast_bf16("cast_w_o", w_o_s)),
              _halves(cast_bf16("cast_mix", mix_s)), _halves(small_s)]
    g_in, g_po, g_do, g_o, g_mix, g_small = gather_weights(shards)
    cat_cols = lambda g: jnp.concatenate([g[j].reshape(-1, g.shape[-1]) for j in range(4)], axis=1)
    w_in_full = cat_cols(g_in)
    w_po_full = cat_cols(g_po)
    w_do_full = g_do.reshape(-1, g_do.shape[-1])
    w_o_full = g_o.reshape(-1, g_o.shape[-1])
    mix_full = g_mix.reshape(4, POOL_GROUPS, gdim // 4, gdim).transpose(1, 0, 2, 3).reshape(POOL_GROUPS, gdim, gdim)
    smalls = [_unpack(g_small[j].reshape(-1, d)[:small_rows], [meta_tokens, conv_w[0]], d) for j in range(4)]
    meta_full = jnp.concatenate([s[0] for s in smalls], axis=1)
    conv_full = jnp.concatenate([s[1] for s in smalls], axis=1)
    n_main = 2 * pw + 4 * dw
    w_main = jnp.concatenate([w_in_full[:, :n_main], w_in_full[:, n_main + 2 * n_heads:]], axis=1)
    w_ba = jnp.pad(w_in_full[:, n_main:n_main + 2 * n_heads], ((0, 0), (0, LANES - 2 * n_heads)))

    (loss, dh, g_norm_w, g_w_main, g_w_ba, g_conv, g_a_log, g_dt, g_mix_full, g_pool_scale, g_dn_norm_w, g_w_po, g_w_do, g_w_o_full,
     g_final_w) = local_step(x[0], meta_full, norm_w, w_main, w_ba, conv_full, A_log, dt_bias, mix_full, pool_scale, dn_norm_w,
                             w_po_full, w_do_full, w_o_full, final_norm_w.reshape(1, d), loss_target[0])
    grad_x = dh[FRONT_PAD + N_META:][None]
    g_meta = dh[FRONT_PAD:FRONT_PAD + N_META]

    g_w_in_full = jnp.concatenate([g_w_main[:, :n_main], g_w_ba[:, :2 * n_heads], g_w_main[:, n_main:]], axis=1)
    col_parts = lambda g: g.reshape(2, g.shape[0] // 2, 4, g.shape[1] // 4).transpose(0, 2, 1, 3)
    row_parts = lambda g: g.reshape(4, 2, g.shape[0] // 8, g.shape[1]).transpose(1, 0, 2, 3)
    parts = [col_parts(g_w_in_full), col_parts(g_w_po), row_parts(g_w_do), row_parts(g_w_o_full)]
    from_sibling = send_grad_halves(parts)
    half = core.astype(jnp.int32).reshape(1)
    names = ["w_in", "w_po", "w_do", "w_o"]
    chip_sums = [add_halves("add_" + nm, p, r, half) for nm, p, r in zip(names, parts, from_sibling)]
    from_chips = scatter_to_chips(chip_sums)
    mine = [sum_leading("sum_" + nm, fc) for nm, fc in zip(names, from_chips)]
    g_in_s, g_po_s, g_do_s, g_o_s = [j.reshape(-1, j.shape[-1]) for j in join_halves(mine)]

    small_like = [loss, g_norm_w, g_a_log, g_dt, g_pool_scale, g_dn_norm_w, g_final_w, g_conv, g_meta, g_mix_full]
    pack = _pack(small_like, d)
    total = sum_leading("sum_small", gather_from_all(pack))
    (loss_t, g_norm_w, g_a_log, g_dt, g_pool_scale, g_dn_norm_w, g_final_w, g_conv, g_meta, g_mix_full) = _unpack(total, small_like, d)
    loss_out = loss_t[0, 0]
    g_conv_s = lax.dynamic_slice_in_dim(g_conv, chip * (g_conv.shape[1] // 4), g_conv.shape[1] // 4, axis=1)
    g_meta_s = lax.dynamic_slice_in_dim(g_meta, chip * (d // 4), d // 4, axis=1)
    g_mix_s = lax.dynamic_slice_in_dim(g_mix_full, chip * (gdim // 4), gdim // 4, axis=1)

    grads = [g_meta_s, g_norm_w, g_in_s[None], g_conv_s[None], g_a_log, g_dt, g_mix_s[None], g_pool_scale, g_dn_norm_w,
             g_po_s[None], g_do_s[None], g_o_s[None], g_final_w.reshape(d)]
    weights = [meta_tokens, norm_w, w_in, conv_w, A_log, dt_bias, pool_mix, pool_scale, dn_norm_w, w_pool_out, w_dn_out, w_o, final_norm_w]
    ms = [m_meta_tokens, m_norm_w, m_w_in, m_conv_w, m_A_log, m_dt_bias, m_pool_mix, m_pool_scale, m_dn_norm_w, m_w_pool_out, m_w_dn_out, m_w_o, m_final_norm_w]
    vs = [v_meta_tokens, v_norm_w, v_w_in, v_conv_w, v_A_log, v_dt_bias, v_pool_mix, v_pool_scale, v_dn_norm_w, v_w_pool_out, v_w_dn_out, v_w_o, v_final_norm_w]
    big = {2: "w_in", 9: "w_po", 10: "w_do", 11: "w_o"}
    deltas, new_ms, new_vs = [None] * 13, [None] * 13, [None] * 13
    for i, nm in big.items():
        shape2 = (-1, weights[i].shape[-1])
        dl, nm_, nv_ = adamw("adamw_" + nm, weights[i].reshape(shape2), grads[i].reshape(shape2), ms[i].reshape(shape2), vs[i].reshape(shape2))
        deltas[i], new_ms[i], new_vs[i] = [t.reshape(weights[i].shape) for t in (dl, nm_, nv_)]
    small_idx = [i for i in range(13) if i not in big]
    packs = [_pack([arrs[i] for i in small_idx], d) for arrs in (weights, grads, ms, vs)]
    outs = adamw("adamw_small", *packs)
    like = [weights[i] for i in small_idx]
    for res, dest in zip(outs, (deltas, new_ms, new_vs)):
        for i, val in zip(small_idx, _unpack(res, like, d)):
            dest[i] = val
    return (loss_out, grad_x, *grads, *deltas, *new_ms, *new_vs)
```

```python
import functools

import jax
import jax.numpy as jnp
from jax import lax
from jax.experimental import pallas as pl
from jax.experimental.pallas import tpu as pltpu

f32 = jnp.float32
bf16 = jnp.bfloat16
HIGHEST = lax.Precision.HIGHEST
MESH = pl.DeviceIdType.MESH

N_META = 16
CHUNK = 64
FRONT_PAD = (-N_META) % CHUNK
HEAD_DIM = 128
POOL_GROUPS = 4
POOL_WINDOWS = (2, 4, 8, 16)
CONV_WIDTH = 4
NORM_EPS = 1e-6
ADAM_LR, ADAM_B1, ADAM_B2, ADAM_EPS, ADAM_WD, ADAM_STEP = 0.001, 0.9, 0.999, 1e-08, 0.01, 10
LANES = 128
VMEM_LIMIT = 56 * 2**20


def _pc(body, **kw):
    return pl.pallas_call(body, **kw)


def _params(*sem, **kw):
    return pltpu.CompilerParams(dimension_semantics=sem or None, vmem_limit_bytes=VMEM_LIMIT, **kw)


def _dg(a, b, dims, prec=None):
    return lax.dot_general(a, b, (dims, ((), ())), precision=prec, preferred_element_type=f32)


@jax.custom_vjp
def mm_nn(a, b):
    return _dg(a.astype(bf16), b.astype(bf16), ((1,), (0,)))


@jax.custom_vjp
def mm_nt(a, b):
    return _dg(a.astype(bf16), b.astype(bf16), ((1,), (1,)))


@jax.custom_vjp
def mm_tn(a, b):
    return _dg(a.astype(bf16).T, b.astype(bf16), ((1,), (0,)))


mm_nn.defvjp(lambda a, b: (mm_nn(a, b), (a, b)), lambda r, dy: (mm_nt(dy, r[1]), mm_tn(r[0], dy)))
mm_nt.defvjp(lambda a, b: (mm_nt(a, b), (a, b)), lambda r, dy: (mm_nn(dy, r[1]), mm_tn(dy, r[0])))
mm_tn.defvjp(lambda a, b: (mm_tn(a, b), (a, b)), lambda r, dy: (mm_nt(r[1], dy), mm_nn(r[0], dy)))


def _split3(x):
    hi = x.astype(bf16)
    r1 = x - hi.astype(f32)
    mid = r1.astype(bf16)
    lo = (r1 - mid.astype(f32)).astype(bf16)
    return hi, mid, lo


@jax.custom_vjp
def mm_sel(sel, x):
    s = sel.astype(bf16)
    d = ((1,), (0,))
    hi, mid, lo = _split3(x)
    return _dg(s, hi, d) + _dg(s, mid, d) + _dg(s, lo, d)


def _mm_sel_bwd(sel, dy):
    s = sel.astype(bf16)
    d = ((0,), (0,))
    hi, mid, lo = _split3(dy)
    return jnp.zeros_like(sel), _dg(s, hi, d) + _dg(s, mid, d) + _dg(s, lo, d)


mm_sel.defvjp(lambda sel, x: (mm_sel(sel, x), sel), _mm_sel_bwd)


@jax.custom_vjp
def mm_pick(x, sel):
    s = sel.astype(bf16)
    d = ((1,), (0,))
    hi, mid, lo = _split3(x)
    return _dg(hi, s, d) + _dg(mid, s, d) + _dg(lo, s, d)


def _mm_pick_bwd(sel, dy):
    s = sel.astype(bf16)
    d = ((1,), (1,))
    hi, mid, lo = _split3(dy)
    return _dg(hi, s, d) + _dg(mid, s, d) + _dg(lo, s, d), jnp.zeros_like(sel)


mm_pick.defvjp(lambda x, sel: (mm_pick(x, sel), sel), _mm_pick_bwd)


def mm3(a, b):
    a_hi = a.astype(bf16)
    a_lo = (a - a_hi.astype(f32)).astype(bf16)
    b_hi = b.astype(bf16)
    b_lo = (b - b_hi.astype(f32)).astype(bf16)
    d = ((1,), (0,))
    return _dg(a_hi, b_hi, d) + _dg(a_hi, b_lo, d) + _dg(a_lo, b_hi, d)


def tri_inv(l):
    n = l.shape[0]
    eye = (lax.broadcasted_iota(jnp.int32, (n, n), 0) == lax.broadcasted_iota(jnp.int32, (n, n), 1)).astype(f32)
    m = -l
    t = eye + m
    k = 1
    while 2 * k < CHUNK:
        m = mm3(m, m)
        t = t + mm3(t, m)
        k *= 2
    return t


@functools.partial(jax.custom_vjp, nondiff_argnums=(1,))
def shift_rows(x, j):
    n = x.shape[0]
    rows = lax.broadcasted_iota(jnp.int32, x.shape, 0)
    if j >= 0:
        return jnp.where(rows >= j, pltpu.roll(x, j, 0), 0.0)
    return jnp.where(rows < n + j, pltpu.roll(x, n + j, 0), 0.0)


shift_rows.defvjp(lambda x, j: (shift_rows(x, j), None), lambda j, _, dy: (shift_rows(dy, -j),))


def sigmoid(x):
    return 1.0 / (1.0 + jnp.exp(-x))


def silu(x):
    return x * sigmoid(x)


def softplus(x):
    return jnp.maximum(x, 0.0) + jnp.log(1.0 + jnp.exp(-jnp.abs(x)))


def rmsnorm(x, w):
    return x * lax.rsqrt(jnp.mean(x * x, axis=-1, keepdims=True) + NORM_EPS) * w


def l2norm(x):
    return x * lax.rsqrt(jnp.sum(x * x, axis=-1, keepdims=True) + NORM_EPS)


def pool_fn(u, zp, mix, scale, group):
    rows = lax.broadcasted_iota(jnp.int32, u.shape, 0)
    sums = []
    s, w = u, 1
    while w < POOL_WINDOWS[-1]:
        s = s + shift_rows(s, w)
        w *= 2
        sums.append(s)
    total = sums[-1]
    for gi in range(POOL_GROUPS - 2, -1, -1):
        total = jnp.where(group == gi, sums[gi], total)
    window = jnp.left_shift(2, group)
    cnt = jnp.clip(rows - (FRONT_PAD - 1), 1, window).astype(f32)
    pooled = total / cnt - u
    return mm_nn(pooled, mix) * scale * silu(zp)


def conv_silu(x, w):
    k = CONV_WIDTH
    y = x * w[k - 1:k, :]
    for kk in range(k - 1):
        y = y + shift_rows(x, k - 1 - kk) * w[kk:kk + 1, :]
    return silu(y)


def _lane_pick(row, idx):
    lanes = lax.broadcasted_iota(jnp.int32, row.shape, 1)
    return jnp.sum(jnp.where(lanes == idx, row, 0.0), axis=1, keepdims=True)


def dn_pre_fn(qr, kr, vr, ba, cwq, cwk, cwv, a_log, dt_bias, head, n_heads):
    q = l2norm(conv_silu(qr, cwq)) * (HEAD_DIM ** -0.5)
    k = l2norm(conv_silu(kr, cwk))
    v = conv_silu(vr, cwv)
    r = lax.broadcasted_iota(jnp.int32, (LANES, LANES), 0)
    b_b = mm_pick(ba, (r == head).astype(f32))
    a_b = mm_pick(ba, (r == head + n_heads).astype(f32))
    real = lax.broadcasted_iota(jnp.int32, qr.shape, 0) >= FRONT_PAD
    beta_b = jnp.where(real, sigmoid(b_b), 0.0)
    g_b = jnp.where(real, -jnp.exp(_lane_pick(a_log, head)) * softplus(a_b + _lane_pick(dt_bias, head)), 0.0)
    return q, k, v, beta_b, g_b


def _chunk_masks(rows):
    r = lax.broadcasted_iota(jnp.int32, (rows, rows), 0)
    c = lax.broadcasted_iota(jnp.int32, (rows, rows), 1)
    same = (r // CHUNK) == (c // CHUNK)
    return same, jnp.logical_and(same, r >= c), jnp.logical_and(same, r > c)


def _lane0(rows):
    return (lax.broadcasted_iota(jnp.int32, (rows, LANES), 1) == 0).astype(bf16)


@jax.custom_vjp
def lane0_as_row(x):
    sel = _lane0(x.shape[0])
    d = ((1,), (1,))
    hi, mid, lo = _split3(x)
    return _dg(sel, hi, d) + _dg(sel, mid, d) + _dg(sel, lo, d)


def _lane0_as_row_bwd(rows, dy):
    sel = _lane0(rows)
    d = ((0,), (0,))
    hi, mid, lo = _split3(dy)
    return (_dg(hi, sel, d) + _dg(mid, sel, d) + _dg(lo, sel, d),)


lane0_as_row.defvjp(lambda x: (lane0_as_row(x), x.shape[0]), _lane0_as_row_bwd)


def gate_fn(g_b):
    rows = g_b.shape[0]
    same, causal, _ = _chunk_masks(rows)
    gcum_b = mm_sel(causal.astype(f32), g_b)
    glast_b = mm_sel(same.astype(f32), g_b)
    g_rows = jnp.broadcast_to(gcum_b[:, :1], (rows, rows))
    decay = jnp.where(causal, jnp.exp(jnp.where(causal, g_rows - lane0_as_row(gcum_b), 0.0)), 0.0)
    return decay, jnp.exp(gcum_b), jnp.exp(glast_b - gcum_b), jnp.exp(glast_b)


def _fold_matrix(rows):
    r = lax.broadcasted_iota(jnp.int32, (rows, LANES), 0)
    c = lax.broadcasted_iota(jnp.int32, (rows, LANES), 1)
    return (r % CHUNK == c).astype(bf16)


@jax.custom_vjp
def fold_chunks(x):
    return _dg(x.astype(bf16), _fold_matrix(x.shape[0]), ((1,), (0,)))


def _fold_chunks_bwd(rows, dy):
    fold = _fold_matrix(rows)
    d = ((1,), (1,))
    hi, mid, lo = _split3(dy)
    return (_dg(hi, fold, d) + _dg(mid, fold, d) + _dg(lo, fold, d),)


fold_chunks.defvjp(lambda x: (fold_chunks(x), x.shape[0]), _fold_chunks_bwd)


def lmat_fn(k, beta_b, decay):
    _, _, strict = _chunk_masks(k.shape[0])
    return jnp.where(strict, mm_nt(k * beta_b, k) * decay, 0.0)


def intra_fn(tmat, q, k, v, beta_b, decay, eg, kfac):
    _, causal, _ = _chunk_masks(q.shape[0])
    k_beta = k * beta_b
    u_c = mm_nn(tmat, v * beta_b)
    w_c = mm_nn(tmat, k_beta * eg)
    qk = jnp.where(causal, mm_nt(q, k) * decay, 0.0)
    return u_c, w_c, q * eg, k * kfac, fold_chunks(qk)


def seq_fn(u_c, w_c, q_dec, k_dec, qk_c, gl, zd, norm_w, state):
    v_new = u_c - mm_nn(w_c, state)
    o = mm_nn(q_dec, state) + mm_nn(qk_c[:, :CHUNK], v_new)
    new_state = state * gl[0:1, :] + mm_tn(k_dec, v_new)
    return rmsnorm(o, norm_w) * silu(zd), new_state


def loss_fn(o, w, tgt):
    err = rmsnorm(o, w) - tgt
    return 0.5 * jnp.sum(jnp.mean(err * err, axis=-1))


def matmul(name, a, b, *, ta=False, tb=False, tm, tn, tk=None, extras=(), epi=None, out_dtypes=(f32,)):
    m, k = (a.shape[1], a.shape[0]) if ta else a.shape
    n = b.shape[0] if tb else b.shape[1]
    tm, tn, tk = min(tm, m), min(tn, n), min(tk or k, k)
    assert m % tm == 0 and n % tn == 0 and k % tk == 0, (name, m, n, k, tm, tn, tk)
    nm, nn, nk = m // tm, n // tn, k // tk
    a_spec = pl.BlockSpec((tk, tm), lambda i, j, kk: (kk, i)) if ta else pl.BlockSpec((tm, tk), lambda i, j, kk: (i, kk))
    b_spec = pl.BlockSpec((tn, tk), lambda i, j, kk: (j, kk)) if tb else pl.BlockSpec((tk, tn), lambda i, j, kk: (kk, j))
    ex_specs = []
    for _, off in extras:
        assert off % tn == 0, (name, off, tn)
        ex_specs.append(pl.BlockSpec((tm, tn), functools.partial(lambda i, j, kk, o: (i, o + j), o=off // tn)))
    n_ex, n_out = len(extras), len(out_dtypes)
    dims = ((0 if ta else 1,), (1 if tb else 0,))

    def body(a_ref, b_ref, *rest):
        ex_refs, out_refs = rest[:n_ex], rest[n_ex:n_ex + n_out]

        def finish(acc):
            res = epi(acc, *[r[...] for r in ex_refs]) if epi is not None else (acc,)
            for o_ref, val in zip(out_refs, res):
                o_ref[...] = val.astype(o_ref.dtype)

        p = _dg(a_ref[...], b_ref[...], dims)
        if nk == 1:
            finish(p)
        else:
            acc_ref = rest[-1]
            kk = pl.program_id(2)

            @pl.when(kk == 0)
            def _():
                acc_ref[...] = p

            @pl.when(kk > 0)
            def _():
                acc_ref[...] += p

            @pl.when(kk == nk - 1)
            def _():
                finish(acc_ref[...])

    outs = _pc(
        body, name=name, grid=(nm, nn, nk),
        in_specs=[a_spec, b_spec] + ex_specs,
        out_specs=[pl.BlockSpec((tm, tn), lambda i, j, kk: (i, j))] * n_out,
        out_shape=[jax.ShapeDtypeStruct((m, n), dt) for dt in out_dtypes],
        scratch_shapes=[pltpu.VMEM((tm, tn), f32)] if nk > 1 else [],
        compiler_params=_params("parallel", "parallel", "arbitrary"),
    )(a, b, *[e for e, _ in extras])
    return outs[0] if n_out == 1 else outs


def _row_tile(rows, cols, n_arrays, itemsize=4, budget=24 * 2**20):
    best = None
    for t in range(16, rows + 1, 16):
        if rows % t == 0 and 2 * n_arrays * t * cols * itemsize <= budget:
            best = t
    return best or rows


def cast_bf16(name, x):
    rows, cols = x.shape
    t = _row_tile(rows, cols, 2)

    def body(x_ref, o_ref):
        o_ref[...] = x_ref[...].astype(bf16)

    return _pc(body, name=name, grid=(rows // t,), in_specs=[pl.BlockSpec((t, cols), lambda i: (i, 0))],
               out_specs=pl.BlockSpec((t, cols), lambda i: (i, 0)), out_shape=jax.ShapeDtypeStruct(x.shape, bf16),
               compiler_params=_params("parallel"))(x)


def norm_fwd(hp, norm_w, tm):
    tp, d = hp.shape

    def body(h_ref, w_ref, o_ref):
        o_ref[...] = rmsnorm(h_ref[...], w_ref[...]).astype(bf16)

    return _pc(body, name="norm_fwd", grid=(tp // tm,),
               in_specs=[pl.BlockSpec((tm, d), lambda i: (i, 0)), pl.BlockSpec((1, d), lambda i: (0, 0))],
               out_specs=pl.BlockSpec((tm, d), lambda i: (i, 0)), out_shape=jax.ShapeDtypeStruct((tp, d), bf16),
               compiler_params=_params("parallel"))(hp, norm_w)


def norm_bwd(hp, norm_w, dxn, dout, tm):
    tp, d = hp.shape

    def body(h_ref, w_ref, dxn_ref, dout_ref, dh_ref, dw_ref):
        _, vjp = jax.vjp(rmsnorm, h_ref[...], w_ref[...])
        dh, dw = vjp(dxn_ref[...])
        dh_ref[...] = dh + dout_ref[...]

        @pl.when(pl.program_id(0) == 0)
        def _():
            dw_ref[...] = jnp.zeros_like(dw_ref)

        dw_ref[...] += dw

    row = pl.BlockSpec((tm, d), lambda i: (i, 0))
    vec = pl.BlockSpec((1, d), lambda i: (0, 0))
    return _pc(body, name="norm_bwd", grid=(tp // tm,), in_specs=[row, vec, row, row], out_specs=[row, vec],
               out_shape=[jax.ShapeDtypeStruct((tp, d), f32), jax.ShapeDtypeStruct((1, d), f32)],
               compiler_params=_params("arbitrary"))(hp, norm_w, dxn, dout)


def pool_fwd(proj, mix, scale, pw):
    tp = proj.shape[0]
    g = pw // POOL_GROUPS

    def body(u_ref, z_ref, mix_ref, s_ref, y_ref):
        y_ref[...] = pool_fn(u_ref[...], z_ref[...], mix_ref[0], s_ref[...], pl.program_id(0)).astype(bf16)

    return _pc(body, name="pool_fwd", grid=(POOL_GROUPS,),
               in_specs=[pl.BlockSpec((tp, g), lambda i: (0, i)), pl.BlockSpec((tp, g), lambda i: (0, POOL_GROUPS + i)),
                         pl.BlockSpec((1, g, g), lambda i: (i, 0, 0)), pl.BlockSpec((1, g), lambda i: (0, i))],
               out_specs=pl.BlockSpec((tp, g), lambda i: (0, i)), out_shape=jax.ShapeDtypeStruct((tp, pw), bf16),
               compiler_params=_params("parallel"))(proj, proj, mix, scale)


def pool_bwd(proj, mix, scale, dy, pw):
    tp = proj.shape[0]
    g = pw // POOL_GROUPS

    def body(u_ref, z_ref, mix_ref, s_ref, dy_ref, du_ref, dz_ref, dmix_ref, ds_ref):
        grp = pl.program_id(0)
        _, vjp = jax.vjp(lambda u, z, m, s: pool_fn(u, z, m, s, grp), u_ref[...], z_ref[...], mix_ref[0].astype(f32), s_ref[...])
        du, dz, dmix, ds = vjp(dy_ref[...])
        du_ref[...] = du.astype(bf16)
        dz_ref[...] = dz.astype(bf16)
        dmix_ref[0] = dmix
        ds_ref[...] = ds

    col = pl.BlockSpec((tp, g), lambda i: (0, i))
    return _pc(body, name="pool_bwd", grid=(POOL_GROUPS,),
               in_specs=[col, pl.BlockSpec((tp, g), lambda i: (0, POOL_GROUPS + i)),
                         pl.BlockSpec((1, g, g), lambda i: (i, 0, 0)), pl.BlockSpec((1, g), lambda i: (0, i)), col],
               out_specs=[col, col, pl.BlockSpec((1, g, g), lambda i: (i, 0, 0)), pl.BlockSpec((1, g), lambda i: (0, i))],
               out_shape=[jax.ShapeDtypeStruct((tp, pw), bf16), jax.ShapeDtypeStruct((tp, pw), bf16),
                          jax.ShapeDtypeStruct((POOL_GROUPS, g, g), f32), jax.ShapeDtypeStruct((1, pw), f32)],
               compiler_params=_params("parallel"))(proj, proj, mix, scale, dy)


def _dn_pre_specs(tp, n_heads, q_off):
    hb = lambda off: pl.BlockSpec((tp, HEAD_DIM), functools.partial(lambda h, o: (0, o + h), o=off))
    cw = lambda off: pl.BlockSpec((CONV_WIDTH, HEAD_DIM), functools.partial(lambda h, o: (0, o + h), o=off))
    whole = lambda shape: pl.BlockSpec(shape, lambda h: (0, 0))
    return ([hb(q_off), hb(q_off + n_heads), hb(q_off + 2 * n_heads), whole((tp, LANES)),
             cw(0), cw(n_heads), cw(2 * n_heads), whole((1, LANES)), whole((1, LANES))], hb, cw, whole)


def dn_pre_fwd(proj, ba, conv_w, a_log, dt_bias, n_heads, q_off):
    tp = proj.shape[0]
    in_specs, hb, _, _ = _dn_pre_specs(tp, n_heads, q_off)

    def body(q_ref, k_ref, v_ref, ba_ref, cq_ref, ck_ref, cv_ref, al_ref, dt_ref, *out_refs):
        outs = dn_pre_fn(q_ref[...], k_ref[...], v_ref[...], ba_ref[...], cq_ref[...], ck_ref[...], cv_ref[...],
                         al_ref[...], dt_ref[...], pl.program_id(0), n_heads)
        for o_ref, val in zip(out_refs, outs):
            o_ref[...] = val

    return _pc(body, name="dn_pre_fwd", grid=(n_heads,), in_specs=in_specs, out_specs=[hb(0)] * 5,
               out_shape=[jax.ShapeDtypeStruct((tp, n_heads * HEAD_DIM), f32)] * 5,
               compiler_params=_params("parallel"))(proj, proj, proj, ba, conv_w, conv_w, conv_w, a_log, dt_bias)


def dn_pre_bwd(proj, ba, conv_w, a_log, dt_bias, cots, n_heads, q_off):
    tp = proj.shape[0]
    in_specs, hb, cw, whole = _dn_pre_specs(tp, n_heads, q_off)

    def body(q_ref, k_ref, v_ref, ba_ref, cq_ref, ck_ref, cv_ref, al_ref, dt_ref, dq_ref, dk_ref, dv_ref, db_ref, dg_ref,
             dqr_ref, dkr_ref, dvr_ref, dba_ref, dcq_ref, dck_ref, dcv_ref, dal_ref, ddt_ref):
        head = pl.program_id(0)
        fn = lambda *args: dn_pre_fn(*args, head, n_heads)
        _, vjp = jax.vjp(fn, q_ref[...], k_ref[...], v_ref[...], ba_ref[...], cq_ref[...], ck_ref[...], cv_ref[...],
                         al_ref[...], dt_ref[...])
        dqr, dkr, dvr, dba, dcq, dck, dcv, dal, ddt = vjp((dq_ref[...], dk_ref[...], dv_ref[...], db_ref[...], dg_ref[...]))
        dqr_ref[...] = dqr.astype(bf16)
        dkr_ref[...] = dkr.astype(bf16)
        dvr_ref[...] = dvr.astype(bf16)
        dcq_ref[...] = dcq
        dck_ref[...] = dck
        dcv_ref[...] = dcv

        @pl.when(head == 0)
        def _():
            dba_ref[...] = jnp.zeros_like(dba_ref)
            dal_ref[...] = jnp.zeros_like(dal_ref)
            ddt_ref[...] = jnp.zeros_like(ddt_ref)

        dba_ref[...] += dba
        dal_ref[...] += dal
        ddt_ref[...] += ddt

    w = n_heads * HEAD_DIM
    return _pc(body, name="dn_pre_bwd", grid=(n_heads,), in_specs=in_specs + [hb(0)] * 5,
               out_specs=[hb(0)] * 3 + [whole((tp, LANES)), cw(0), cw(0), cw(0), whole((1, LANES)), whole((1, LANES))],
               out_shape=[jax.ShapeDtypeStruct((tp, w), bf16)] * 3 + [jax.ShapeDtypeStruct((tp, LANES), f32)]
               + [jax.ShapeDtypeStruct((CONV_WIDTH, w), f32)] * 3 + [jax.ShapeDtypeStruct((1, LANES), f32)] * 2,
               compiler_params=_params("arbitrary"))(proj, proj, proj, ba, conv_w, conv_w, conv_w, a_log, dt_bias, *cots)


def _super_rows(tp):
    n = tp // CHUNK
    return CHUNK * max(j for j in (4, 3, 2, 1) if n % j == 0)


def dn_intra_fwd(q, k, v, beta_b, g_b, n_heads):
    tp = q.shape[0]
    rows = _super_rows(tp)
    ns = tp // rows

    def body(q_ref, k_ref, v_ref, b_ref, g_ref, u_ref, w_ref, qd_ref, kd_ref, qk_ref, gl_ref, t_ref):
        decay, eg, kfac, gl = gate_fn(g_ref[...])
        tmat = tri_inv(lmat_fn(k_ref[...], b_ref[...], decay))
        u_c, w_c, q_dec, k_dec, qk_c = intra_fn(tmat, q_ref[...], k_ref[...], v_ref[...], b_ref[...], decay, eg, kfac)
        u_ref[...] = u_c
        w_ref[...] = w_c
        qd_ref[...] = q_dec
        kd_ref[...] = k_dec
        qk_ref[...] = qk_c
        gl_ref[...] = gl
        t_ref[0, 0] = tmat

    blk = pl.BlockSpec((rows, HEAD_DIM), lambda h, s: (s, h))
    return _pc(body, name="dn_intra_fwd", grid=(n_heads, ns), in_specs=[blk] * 5,
               out_specs=[blk] * 6 + [pl.BlockSpec((1, 1, rows, rows), lambda h, s: (h, s, 0, 0))],
               out_shape=[jax.ShapeDtypeStruct(q.shape, f32)] * 6 + [jax.ShapeDtypeStruct((n_heads, ns, rows, rows), f32)],
               compiler_params=_params("parallel", "parallel"))(q, k, v, beta_b, g_b)


def dn_intra_bwd(q, k, v, beta_b, g_b, tmats, cots, n_heads):
    tp = q.shape[0]
    rows = _super_rows(tp)
    ns = tp // rows

    def body(q_ref, k_ref, v_ref, b_ref, g_ref, t_ref, du_ref, dw_ref, dqd_ref, dkd_ref, dqk_ref, dgl_ref,
             dq_ref, dk_ref, dv_ref, db_ref, dg_ref):
        qv, kv, vv, bv, tmat = q_ref[...], k_ref[...], v_ref[...], b_ref[...], t_ref[0, 0]
        (decay, eg, kfac, _), gate_vjp = jax.vjp(gate_fn, g_ref[...])
        _, intra_vjp = jax.vjp(intra_fn, tmat, qv, kv, vv, bv, decay, eg, kfac)
        dt, dq, dk, dv, db, ddecay, deg, dkfac = intra_vjp((du_ref[...], dw_ref[...], dqd_ref[...], dkd_ref[...], dqk_ref[...]))
        tt = tmat.T
        dl = -mm3(mm3(tt, dt), tt)
        _, lmat_vjp = jax.vjp(lmat_fn, kv, bv, decay)
        dk2, db2, ddecay2 = lmat_vjp(dl)
        (dg,) = gate_vjp((ddecay + ddecay2, deg, dkfac, dgl_ref[...]))
        dq_ref[...] = dq
        dk_ref[...] = dk + dk2
        dv_ref[...] = dv
        db_ref[...] = db + db2
        dg_ref[...] = dg

    blk = pl.BlockSpec((rows, HEAD_DIM), lambda h, s: (s, h))
    return _pc(body, name="dn_intra_bwd", grid=(n_heads, ns),
               in_specs=[blk] * 5 + [pl.BlockSpec((1, 1, rows, rows), lambda h, s: (h, s, 0, 0))] + [blk] * 6,
               out_specs=[blk] * 5, out_shape=[jax.ShapeDtypeStruct(q.shape, f32)] * 5,
               compiler_params=_params("parallel", "parallel"))(q, k, v, beta_b, g_b, tmats, *cots)


def dn_seq_fwd(inter, proj, dn_norm_w, n_heads, zd_off):
    tp, width = inter[0].shape
    n_chunks = tp // CHUNK

    def body(u_ref, w_ref, qd_ref, kd_ref, qk_ref, gl_ref, z_ref, nw_ref, y_ref, s_ref, state):
        @pl.when(pl.program_id(0) == 0)
        def _():
            state[...] = jnp.zeros_like(state)

        for h in range(n_heads):
            sl = slice(h * HEAD_DIM, (h + 1) * HEAD_DIM)
            st = state[h]
            s_ref[0, h] = st
            y, new_state = seq_fn(u_ref[:, sl], w_ref[:, sl], qd_ref[:, sl], kd_ref[:, sl], qk_ref[:, sl], gl_ref[:, sl],
                                  z_ref[:, sl], nw_ref[...], st)
            y_ref[:, sl] = y.astype(bf16)
            state[h] = new_state

    blk = pl.BlockSpec((CHUNK, width), lambda n: (n, 0))
    return _pc(body, name="dn_seq_fwd", grid=(n_chunks,),
               in_specs=[blk] * 6 + [pl.BlockSpec((CHUNK, width), lambda n: (n, zd_off)), pl.BlockSpec((1, HEAD_DIM), lambda n: (0, 0))],
               out_specs=[blk, pl.BlockSpec((1, n_heads, HEAD_DIM, HEAD_DIM), lambda n: (n, 0, 0, 0))],
               out_shape=[jax.ShapeDtypeStruct((tp, width), bf16), jax.ShapeDtypeStruct((n_chunks, n_heads, HEAD_DIM, HEAD_DIM), f32)],
               scratch_shapes=[pltpu.VMEM((n_heads, HEAD_DIM, HEAD_DIM), f32)],
               compiler_params=_params("arbitrary"))(*inter, proj, dn_norm_w)


def dn_seq_bwd(inter, proj, dn_norm_w, states, dy, n_heads, zd_off):
    tp, width = inter[0].shape
    n_chunks = tp // CHUNK
    last = n_chunks - 1

    def body(u_ref, w_ref, qd_ref, kd_ref, qk_ref, gl_ref, z_ref, nw_ref, s_ref, dy_ref,
             du_ref, dw_ref, dqd_ref, dkd_ref, dqk_ref, dgl_ref, dz_ref, dnw_ref, dstate):
        @pl.when(pl.program_id(0) == 0)
        def _():
            dstate[...] = jnp.zeros_like(dstate)
            dnw_ref[...] = jnp.zeros_like(dnw_ref)

        dnw = jnp.zeros((1, HEAD_DIM), f32)
        for h in range(n_heads):
            sl = slice(h * HEAD_DIM, (h + 1) * HEAD_DIM)
            _, vjp = jax.vjp(seq_fn, u_ref[:, sl], w_ref[:, sl], qd_ref[:, sl], kd_ref[:, sl], qk_ref[:, sl], gl_ref[:, sl],
                             z_ref[:, sl], nw_ref[...], s_ref[0, h])
            du, dw, dqd, dkd, dqk, dgl, dz, dn, ds = vjp((dy_ref[:, sl], dstate[h]))
            du_ref[:, sl] = du
            dw_ref[:, sl] = dw
            dqd_ref[:, sl] = dqd
            dkd_ref[:, sl] = dkd
            dqk_ref[:, sl] = dqk
            dgl_ref[:, sl] = dgl
            dz_ref[:, sl] = dz.astype(bf16)
            dnw = dnw + dn
            dstate[h] = ds
        dnw_ref[...] += dnw

    blk = pl.BlockSpec((CHUNK, width), lambda n: (last - n, 0))
    return _pc(body, name="dn_seq_bwd", grid=(n_chunks,),
               in_specs=[blk] * 6 + [pl.BlockSpec((CHUNK, width), lambda n: (last - n, zd_off)), pl.BlockSpec((1, HEAD_DIM), lambda n: (0, 0)),
                         pl.BlockSpec((1, n_heads, HEAD_DIM, HEAD_DIM), lambda n: (last - n, 0, 0, 0)), blk],
               out_specs=[blk] * 7 + [pl.BlockSpec((1, HEAD_DIM), lambda n: (0, 0))],
               out_shape=[jax.ShapeDtypeStruct((tp, width), f32)] * 6 + [jax.ShapeDtypeStruct((tp, width), bf16),
                                                                          jax.ShapeDtypeStruct((1, HEAD_DIM), f32)],
               scratch_shapes=[pltpu.VMEM((n_heads, HEAD_DIM, HEAD_DIM), f32)],
               compiler_params=_params("arbitrary"))(*inter, proj, dn_norm_w, states, dy)


def loss_stage(out, final_w, target):
    tp, d = out.shape
    n_tiles = tp // CHUNK

    def body(o_ref, w_ref, t_ref, loss_ref, do_ref, dob_ref, dw_ref):
        i = pl.program_id(0)

        @pl.when(i == 0)
        def _():
            loss_ref[...] = jnp.zeros_like(loss_ref)
            dw_ref[...] = jnp.zeros_like(dw_ref)

        scored = (i > 0).astype(f32)
        val, (do, dw) = jax.value_and_grad(lambda o, w: scored * loss_fn(o, w, t_ref[...]), argnums=(0, 1))(o_ref[...], w_ref[...])
        loss_ref[...] += jnp.full(loss_ref.shape, val, f32)
        do_ref[...] = do
        dob_ref[...] = do.astype(bf16)
        dw_ref[...] += dw

    row = pl.BlockSpec((CHUNK, d), lambda i: (i, 0))
    vec = pl.BlockSpec((1, d), lambda i: (0, 0))
    return _pc(body, name="loss_stage", grid=(n_tiles,),
               in_specs=[row, vec, pl.BlockSpec((CHUNK, d), lambda i: (jnp.maximum(i - 1, 0), 0))],
               out_specs=[pl.BlockSpec((1, LANES), lambda i: (0, 0)), row, row, vec],
               out_shape=[jax.ShapeDtypeStruct((1, LANES), f32), jax.ShapeDtypeStruct((tp, d), f32),
                          jax.ShapeDtypeStruct((tp, d), bf16), jax.ShapeDtypeStruct((1, d), f32)],
               compiler_params=_params("arbitrary"))(out, final_w, target)


def adamw(name, w, g, m, v):
    rows, cols = w.shape
    t = _row_tile(rows, cols, 7)

    def body(w_ref, g_ref, m_ref, v_ref, d_ref, nm_ref, nv_ref):
        gg = g_ref[...]
        nm = ADAM_B1 * m_ref[...] + (1.0 - ADAM_B1) * gg
        nv = ADAM_B2 * v_ref[...] + (1.0 - ADAM_B2) * jnp.square(gg)
        m_hat = nm / (1.0 - ADAM_B1 ** ADAM_STEP)
        v_hat = nv / (1.0 - ADAM_B2 ** ADAM_STEP)
        d_ref[...] = -ADAM_LR * (m_hat / (jnp.sqrt(v_hat) + ADAM_EPS) + ADAM_WD * w_ref[...])
        nm_ref[...] = nm
        nv_ref[...] = nv

    blk = pl.BlockSpec((t, cols), lambda i: (i, 0))
    return _pc(body, name=name, grid=(rows // t,), in_specs=[blk] * 4, out_specs=[blk] * 3,
               out_shape=[jax.ShapeDtypeStruct(w.shape, f32)] * 3, compiler_params=_params("parallel"))(w, g, m, v)


def add_halves(name, g, recv, half):
    _, s, r, c = g.shape
    t = _row_tile(r, c, 3)

    def body(half_ref, g_ref, r_ref, o_ref):
        o_ref[...] = (g_ref[0].astype(f32) + r_ref[...].astype(f32)).astype(bf16)

    grid_spec = pltpu.PrefetchScalarGridSpec(
        num_scalar_prefetch=1, grid=(s, r // t),
        in_specs=[pl.BlockSpec((1, 1, t, c), lambda i, j, hf: (hf[0], i, j, 0)), pl.BlockSpec((1, t, c), lambda i, j, hf: (i, j, 0))],
        out_specs=pl.BlockSpec((1, t, c), lambda i, j, hf: (i, j, 0)))

    return _pc(body, name=name, grid_spec=grid_spec, out_shape=jax.ShapeDtypeStruct((s, r, c), bf16),
               compiler_params=_params("parallel", "parallel"))(half, g, recv)


def sum_leading(name, x, out_dtype=f32):
    s, r, c = x.shape
    t = _row_tile(r, c, s + 1)

    def body(x_ref, o_ref):
        acc = x_ref[0].astype(f32)
        for i in range(1, s):
            acc = acc + x_ref[i].astype(f32)
        o_ref[...] = acc.astype(out_dtype)

    return _pc(body, name=name, grid=(r // t,), in_specs=[pl.BlockSpec((s, t, c), lambda i: (0, i, 0))],
               out_specs=pl.BlockSpec((t, c), lambda i: (i, 0)), out_shape=jax.ShapeDtypeStruct((r, c), out_dtype),
               compiler_params=_params("parallel"))(x)


def _place():
    x, y, c = lax.axis_index("x"), lax.axis_index("y"), lax.axis_index("c")
    return x, y, c


def _other_chips(x, y):
    return [(1 - x, y), (x, 1 - y), (1 - x, 1 - y)]


_ANY = pl.BlockSpec(memory_space=pl.ANY)


def gather_weights(shards):
    n = len(shards)

    def body(*refs):
        ins, outs = refs[:n], refs[n:2 * n]
        send1, recv1, send2, recv2, local = refs[2 * n:]
        x, y, c = _place()
        chip = 2 * x + y
        sibling = (x, y, 1 - c)
        others = _other_chips(x, y)

        def remote(src, dst, s_sem, r_sem, to):
            return pltpu.make_async_remote_copy(src_ref=src, dst_ref=dst, send_sem=s_sem, recv_sem=r_sem,
                                                device_id=to, device_id_type=MESH)

        own = [pltpu.make_async_copy(ins[a], outs[a].at[chip], local.at[a]) for a in range(n)]
        for cp in own:
            cp.start()
        first = []
        for a in range(n):
            for j, (ox, oy) in enumerate(others):
                first.append(remote(ins[a].at[c], outs[a].at[chip, c], send1.at[a, j], recv1.at[a, j], (ox, oy, c)))
        for cp in first:
            cp.start()
        passed = []
        for a in range(n):
            for j, (ox, oy) in enumerate(others):
                slab = outs[a].at[2 * ox + oy, c]
                remote(slab, slab, send1.at[a, j], recv1.at[a, j], (ox, oy, c)).wait_recv()
                cp = remote(slab, slab, send2.at[a, j], recv2.at[a, j], sibling)
                cp.start()
                passed.append(cp)
        for a in range(n):
            for j, (ox, oy) in enumerate(others):
                slab = outs[a].at[2 * ox + oy, 1 - c]
                remote(slab, slab, send2.at[a, j], recv2.at[a, j], sibling).wait_recv()
        for cp in first + passed:
            cp.wait_send()
        for cp in own:
            cp.wait()

    sems = [pltpu.SemaphoreType.DMA((n, 3))] * 4 + [pltpu.SemaphoreType.DMA((n,))]
    return _pc(body, name="gather_weights", in_specs=[_ANY] * n, out_specs=[_ANY] * n,
               out_shape=[jax.ShapeDtypeStruct((4,) + s.shape, s.dtype) for s in shards],
               scratch_shapes=sems)(*shards)


def swap_with_sibling(name, sends):
    n = len(sends)

    def body(*refs):
        ins, outs = refs[:n], refs[n:2 * n]
        send, recv = refs[2 * n:]
        x, y, c = _place()
        cps = [pltpu.make_async_remote_copy(src_ref=ins[a], dst_ref=outs[a], send_sem=send.at[a], recv_sem=recv.at[a],
                                            device_id=(x, y, 1 - c), device_id_type=MESH) for a in range(n)]
        for cp in cps:
            cp.start()
        for cp in cps:
            cp.wait()

    return _pc(body, name=name, in_specs=[_ANY] * n, out_specs=[_ANY] * n,
               out_shape=[jax.ShapeDtypeStruct(s.shape, s.dtype) for s in sends],
               scratch_shapes=[pltpu.SemaphoreType.DMA((n,))] * 2)(*sends)


def send_grad_halves(grads):
    n = len(grads)

    def body(*refs):
        ins, outs = refs[:n], refs[n:2 * n]
        send, recv = refs[2 * n:]
        x, y, c = _place()
        cps = [pltpu.make_async_remote_copy(src_ref=ins[a].at[1 - c], dst_ref=outs[a], send_sem=send.at[a], recv_sem=recv.at[a],
                                            device_id=(x, y, 1 - c), device_id_type=MESH) for a in range(n)]
        for cp in cps:
            cp.start()
        for cp in cps:
            cp.wait()

    return _pc(body, name="send_grad_halves", in_specs=[_ANY] * n, out_specs=[_ANY] * n,
               out_shape=[jax.ShapeDtypeStruct(g.shape[1:], g.dtype) for g in grads],
               scratch_shapes=[pltpu.SemaphoreType.DMA((n,))] * 2)(*grads)


def scatter_to_chips(parts):
    n = len(parts)

    def body(*refs):
        ins, outs = refs[:n], refs[n:2 * n]
        send, recv, local = refs[2 * n:]
        x, y, c = _place()
        chip = 2 * x + y
        others = _other_chips(x, y)
        own = [pltpu.make_async_copy(ins[a].at[chip], outs[a].at[chip], local.at[a]) for a in range(n)]
        for cp in own:
            cp.start()
        cps = []
        for a in range(n):
            for j, (ox, oy) in enumerate(others):
                cps.append(pltpu.make_async_remote_copy(
                    src_ref=ins[a].at[2 * ox + oy], dst_ref=outs[a].at[chip], send_sem=send.at[a, j], recv_sem=recv.at[a, j],
                    device_id=(ox, oy, c), device_id_type=MESH))
        for cp in cps:
            cp.start()
        for a in range(n):
            for j, (ox, oy) in enumerate(others):
                slab = outs[a].at[2 * ox + oy]
                pltpu.make_async_remote_copy(src_ref=slab, dst_ref=slab, send_sem=send.at[a, j], recv_sem=recv.at[a, j],
                                             device_id=(ox, oy, c), device_id_type=MESH).wait_recv()
        for cp in cps:
            cp.wait_send()
        for cp in own:
            cp.wait()

    return _pc(body, name="scatter_to_chips", in_specs=[_ANY] * n, out_specs=[_ANY] * n,
               out_shape=[jax.ShapeDtypeStruct(p.shape, p.dtype) for p in parts],
               scratch_shapes=[pltpu.SemaphoreType.DMA((n, 3))] * 2 + [pltpu.SemaphoreType.DMA((n,))])(*parts)


def gather_from_all(pack):
    def body(in_ref, out_ref, send, recv, local):
        x, y, c = _place()
        me = 4 * x + 2 * y + c
        own = pltpu.make_async_copy(in_ref, out_ref.at[me], local)
        own.start()
        flips = [(fx, fy, fc) for fx in (0, 1) for fy in (0, 1) for fc in (0, 1)][1:]
        peers = [(jnp.where(fx, 1 - x, x), jnp.where(fy, 1 - y, y), jnp.where(fc, 1 - c, c)) for fx, fy, fc in flips]
        cps = [pltpu.make_async_remote_copy(src_ref=in_ref, dst_ref=out_ref.at[me], send_sem=send.at[j], recv_sem=recv.at[j],
                                            device_id=p, device_id_type=MESH) for j, p in enumerate(peers)]
        for cp in cps:
            cp.start()
        for j, (px, py, pc) in enumerate(peers):
            slab = out_ref.at[4 * px + 2 * py + pc]
            pltpu.make_async_remote_copy(src_ref=slab, dst_ref=slab, send_sem=send.at[j], recv_sem=recv.at[j],
                                         device_id=(px, py, pc), device_id_type=MESH).wait_recv()
        for cp in cps:
            cp.wait_send()
        own.wait()

    return _pc(body, name="gather_from_all", in_specs=[_ANY], out_specs=_ANY,
               out_shape=jax.ShapeDtypeStruct((8,) + pack.shape, pack.dtype),
               scratch_shapes=[pltpu.SemaphoreType.DMA((7,)), pltpu.SemaphoreType.DMA((7,)), pltpu.SemaphoreType.DMA])(pack)


def _pack(arrays, width):
    rows = []
    for a in arrays:
        flat = a.reshape(-1).astype(f32)
        pad = (-flat.shape[0]) % (8 * width)
        rows.append(jnp.pad(flat, (0, pad)).reshape(-1, width))
    return jnp.concatenate(rows, axis=0)


def _unpack(pack, like, width):
    out, row = [], 0
    for a in like:
        size = 1
        for s in a.shape:
            size *= s
        n_rows = -(-size // (8 * width)) * 8
        out.append(pack[row:row + n_rows].reshape(-1)[:size].reshape(a.shape))
        row += n_rows
    return out


def _halves(a2d):
    r, c = a2d.shape
    return a2d.reshape(2, r // 2, c)


def local_step(x, meta, norm_w, w_main, w_ba, conv_w, a_log, dt_bias, pool_mix, pool_scale, dn_norm_w,
               w_pool_out, w_dn_out, w_o, final_w, target):
    seq, d = x.shape
    pw = pool_scale.shape[1]
    dw = w_dn_out.shape[0]
    n_heads = dw // HEAD_DIM
    tp = FRONT_PAD + N_META + seq
    hp = jnp.concatenate([jnp.zeros((FRONT_PAD, d), f32), meta, x], axis=0)
    tm_big = tp // 2 if tp % 32 == 0 else tp
    tm_norm = max(t for t in range(16, min(tp, 352) + 1, 16) if tp % t == 0)
    tile = min(512, d)
    off_q = 2 * pw
    off_zd = off_q + 3 * dw
    off_gp = off_zd + dw
    off_gd = off_gp + d
    a_log128 = jnp.pad(a_log, ((0, 0), (0, LANES - n_heads)))
    dt128 = jnp.pad(dt_bias, ((0, 0), (0, LANES - n_heads)))

    xn = norm_fwd(hp, norm_w, tm_norm)
    proj = matmul("proj", xn, w_main, tm=tp, tn=tile)
    ba = matmul("proj_ba", xn, w_ba, tm=tp, tn=LANES)
    y_pool = pool_fwd(proj, pool_mix, pool_scale, pw)
    q, k, v, beta_b, g_b = dn_pre_fwd(proj, ba, conv_w, a_log128, dt128, n_heads, off_q // HEAD_DIM)
    assert off_zd % dw == 0
    *inter, tmats = dn_intra_fwd(q, k, v, beta_b, g_b, n_heads)
    y_dn, states = dn_seq_fwd(inter, proj, dn_norm_w, n_heads, off_zd // dw)
    a_mat = matmul("pool_out", y_pool, w_pool_out, tm=tp, tn=tile)

    def merge(acc, a_t, gp_t, gd_t):
        return acc, sigmoid(gp_t) * a_t + sigmoid(gd_t) * acc

    b_mat, merged = matmul("dn_out_merge", y_dn, w_dn_out, tm=tm_big, tn=tile, extras=[(a_mat, 0), (proj, off_gp), (proj, off_gd)],
                           epi=merge, out_dtypes=(f32, bf16))
    out = matmul("out_proj", merged, w_o, tm=tm_big, tn=tile, extras=[(hp, 0)], epi=lambda acc, h_t: (acc + h_t,))
    loss, dout, dout_b, dfinal_w = loss_stage(out, final_w, target)

    def unmerge(dm, a_t, b_t, gp_t, gd_t):
        sp, sd = sigmoid(gp_t), sigmoid(gd_t)
        return dm * sp, dm * sd, dm * a_t * sp * (1.0 - sp), dm * b_t * sd * (1.0 - sd)

    d_a, d_b, d_gp, d_gd = matmul("d_merged", dout_b, w_o, tb=True, tm=tm_big, tn=tile,
                                  extras=[(a_mat, 0), (b_mat, 0), (proj, off_gp), (proj, off_gd)], epi=unmerge,
                                  out_dtypes=(bf16,) * 4)
    g_w_o = matmul("g_w_o", merged, dout_b, ta=True, tm=tile, tn=tile, out_dtypes=(bf16,))
    d_y_pool = matmul("d_y_pool", d_a, w_pool_out, tb=True, tm=tp, tn=tile)
    g_w_pool_out = matmul("g_w_pool_out", y_pool, d_a, ta=True, tm=tile, tn=tile, out_dtypes=(bf16,))
    d_y_dn = matmul("d_y_dn", d_b, w_dn_out, tb=True, tm=tp, tn=tile)
    g_w_dn_out = matmul("g_w_dn_out", y_dn, d_b, ta=True, tm=tile, tn=tile, out_dtypes=(bf16,))
    d_u, d_zp, g_pool_mix, g_pool_scale = pool_bwd(proj, pool_mix, pool_scale, d_y_pool, pw)
    *d_inter, d_zd, g_dn_norm_w = dn_seq_bwd(inter, proj, dn_norm_w, states, d_y_dn, n_heads, off_zd // dw)
    d_q, d_k, d_v, d_beta, d_g = dn_intra_bwd(q, k, v, beta_b, g_b, tmats, d_inter, n_heads)
    d_qr, d_kr, d_vr, d_ba, g_cq, g_ck, g_cv, g_a_log, g_dt = dn_pre_bwd(
        proj, ba, conv_w, a_log128, dt128, (d_q, d_k, d_v, d_beta, d_g), n_heads, off_q // HEAD_DIM)
    d_proj = jnp.concatenate([d_u, d_zp, d_qr, d_kr, d_vr, d_zd, d_gp, d_gd], axis=1)
    d_ba_b = cast_bf16("cast_d_ba", d_ba)
    dxn_ba = matmul("dxn_ba", d_ba_b, w_ba, tb=True, tm=tp, tn=tile)
    n_cols = d_proj.shape[1]
    tk_dxn = max(t for t in range(LANES, min(2048, n_cols) + 1, LANES) if n_cols % t == 0)
    dxn = matmul("dxn", d_proj, w_main, tb=True, tm=tm_big, tn=tile, tk=tk_dxn,
                 extras=[(dxn_ba, 0)], epi=lambda acc, e: (acc + e,))
    g_w_main = matmul("g_w_main", xn, d_proj, ta=True, tm=tile, tn=tile, out_dtypes=(bf16,))
    g_w_ba = matmul("g_w_ba", xn, d_ba_b, ta=True, tm=tile, tn=LANES, out_dtypes=(bf16,))
    dh, g_norm_w = norm_bwd(hp, norm_w, dxn, dout, tm_norm)
    g_conv = jnp.concatenate([g_cq, g_ck, g_cv], axis=1)
    return (loss, dh, g_norm_w, g_w_main, g_w_ba, g_conv, g_a_log[:, :n_heads], g_dt[:, :n_heads], g_pool_mix, g_pool_scale,
            g_dn_norm_w, g_w_pool_out, g_w_dn_out, g_w_o, dfinal_w)


def kernel(x, meta_tokens, norm_w, w_in, conv_w, A_log, dt_bias, pool_mix, pool_scale, dn_norm_w, w_pool_out, w_dn_out, w_o, final_norm_w, loss_target, m_meta_tokens, m_norm_w, m_w_in, m_conv_w, m_A_log, m_dt_bias, m_pool_mix, m_pool_scale, m_dn_norm_w, m_w_pool_out, m_w_dn_out, m_w_o, m_final_norm_w, v_meta_tokens, v_norm_w, v_w_in, v_conv_w, v_A_log, v_dt_bias, v_pool_mix, v_pool_scale, v_dn_norm_w, v_w_pool_out, v_w_dn_out, v_w_o, v_final_norm_w):
    d = x.shape[-1]
    pw = pool_scale.shape[-1]
    dw = w_dn_out.shape[1] * 4
    n_heads = dw // HEAD_DIM
    gdim = pw // POOL_GROUPS
    chip = 2 * lax.axis_index("x") + lax.axis_index("y")
    core = lax.axis_index("c")

    w_in_s, w_po_s, w_do_s, w_o_s = w_in[0], w_pool_out[0], w_dn_out[0], w_o[0]
    mix_s = pool_mix[0].reshape(POOL_GROUPS * (gdim // 4), gdim)
    small_s = _pack([meta_tokens, conv_w[0]], d)
    small_rows = small_s.shape[0]
    small_s = jnp.pad(small_s, ((0, (-small_rows) % 16), (0, 0)))

    shards = [_halves(cast_bf16("cast_w_in", w_in_s)), _halves(cast_bf16("cast_w_po", w_po_s)),
              _halves(cast_bf16("cast_w_do", w_do_s)), _halves(cast_bf16("cast_w_o", w_o_s)),
              _halves(cast_bf16("cast_mix", mix_s)), _halves(small_s)]
    g_in, g_po, g_do, g_o, g_mix, g_small = gather_weights(shards)
    cat_cols = lambda g: jnp.concatenate([g[j].reshape(-1, g.shape[-1]) for j in range(4)], axis=1)
    w_in_full = cat_cols(g_in)
    w_po_full = cat_cols(g_po)
    w_do_full = g_do.reshape(-1, g_do.shape[-1])
    w_o_full = g_o.reshape(-1, g_o.shape[-1])
    mix_full = g_mix.reshape(4, POOL_GROUPS, gdim // 4, gdim).transpose(1, 0, 2, 3).reshape(POOL_GROUPS, gdim, gdim)
    smalls = [_unpack(g_small[j].reshape(-1, d)[:small_rows], [meta_tokens, conv_w[0]], d) for j in range(4)]
    meta_full = jnp.concatenate([s[0] for s in smalls], axis=1)
    conv_full = jnp.concatenate([s[1] for s in smalls], axis=1)
    n_main = 2 * pw + 4 * dw
    w_main = jnp.concatenate([w_in_full[:, :n_main], w_in_full[:, n_main + 2 * n_heads:]], axis=1)
    w_ba = jnp.pad(w_in_full[:, n_main:n_main + 2 * n_heads], ((0, 0), (0, LANES - 2 * n_heads)))

    (loss, dh, g_norm_w, g_w_main, g_w_ba, g_conv, g_a_log, g_dt, g_mix_full, g_pool_scale, g_dn_norm_w, g_w_po, g_w_do, g_w_o_full,
     g_final_w) = local_step(x[0], meta_full, norm_w, w_main, w_ba, conv_full, A_log, dt_bias, mix_full, pool_scale, dn_norm_w,
                             w_po_full, w_do_full, w_o_full, final_norm_w.reshape(1, d), loss_target[0])
    grad_x = dh[FRONT_PAD + N_META:][None]
    g_meta = dh[FRONT_PAD:FRONT_PAD + N_META]

    g_w_in_full = jnp.concatenate([g_w_main[:, :n_main], g_w_ba[:, :2 * n_heads], g_w_main[:, n_main:]], axis=1)
    col_parts = lambda g: g.reshape(2, g.shape[0] // 2, 4, g.shape[1] // 4).transpose(0, 2, 1, 3)
    row_parts = lambda g: g.reshape(4, 2, g.shape[0] // 8, g.shape[1]).transpose(1, 0, 2, 3)
    mix_rows = POOL_GROUPS * (gdim // 4)
    mix_parts = (g_mix_full.astype(bf16).reshape(POOL_GROUPS, 4, gdim // 4, gdim).transpose(1, 0, 2, 3)
                 .reshape(4, 2, mix_rows // 2, gdim).transpose(1, 0, 2, 3))
    parts = [col_parts(g_w_in_full), col_parts(g_w_po), row_parts(g_w_do), row_parts(g_w_o_full), mix_parts]
    from_sibling = send_grad_halves(parts)
    half = core.astype(jnp.int32).reshape(1)
    names = ["w_in", "w_po", "w_do", "w_o", "mix"]
    chip_sums = [add_halves("add_" + nm, p, r, half) for nm, p, r in zip(names, parts, from_sibling)]
    from_chips = scatter_to_chips(chip_sums)
    mine = [sum_leading("sum_" + nm, fc) for nm, fc in zip(names, from_chips)]
    theirs = swap_with_sibling("swap_grad_halves", mine)
    both = lambda lo, hi: jnp.where(core == 0, jnp.concatenate([lo, hi], axis=0), jnp.concatenate([hi, lo], axis=0))
    g_in_s, g_po_s, g_do_s, g_o_s, g_mix_s = [both(m_, t_) for m_, t_ in zip(mine, theirs)]
    g_mix_s = g_mix_s.reshape(POOL_GROUPS, gdim // 4, gdim)

    small_like = [loss, g_norm_w, g_a_log, g_dt, g_pool_scale, g_dn_norm_w, g_final_w, g_conv, g_meta]
    pack = _pack(small_like, d)
    total = sum_leading("sum_small", gather_from_all(pack))
    (loss_t, g_norm_w, g_a_log, g_dt, g_pool_scale, g_dn_norm_w, g_final_w, g_conv, g_meta) = _unpack(total, small_like, d)
    loss_out = loss_t[0, 0]
    g_conv_s = lax.dynamic_slice_in_dim(g_conv, chip * (g_conv.shape[1] // 4), g_conv.shape[1] // 4, axis=1)
    g_meta_s = lax.dynamic_slice_in_dim(g_meta, chip * (d // 4), d // 4, axis=1)

    grads = [g_meta_s, g_norm_w, g_in_s[None], g_conv_s[None], g_a_log, g_dt, g_mix_s[None], g_pool_scale, g_dn_norm_w,
             g_po_s[None], g_do_s[None], g_o_s[None], g_final_w.reshape(d)]
    weights = [meta_tokens, norm_w, w_in, conv_w, A_log, dt_bias, pool_mix, pool_scale, dn_norm_w, w_pool_out, w_dn_out, w_o, final_norm_w]
    ms = [m_meta_tokens, m_norm_w, m_w_in, m_conv_w, m_A_log, m_dt_bias, m_pool_mix, m_pool_scale, m_dn_norm_w, m_w_pool_out, m_w_dn_out, m_w_o, m_final_norm_w]
    vs = [v_meta_tokens, v_norm_w, v_w_in, v_conv_w, v_A_log, v_dt_bias, v_pool_mix, v_pool_scale, v_dn_norm_w, v_w_pool_out, v_w_dn_out, v_w_o, v_final_norm_w]
    big = {2: "w_in", 9: "w_po", 10: "w_do", 11: "w_o"}
    deltas, new_ms, new_vs = [None] * 13, [None] * 13, [None] * 13
    for i, nm in big.items():
        shape2 = (-1, weights[i].shape[-1])
        dl, nm_, nv_ = adamw("adamw_" + nm, weights[i].reshape(shape2), grads[i].reshape(shape2), ms[i].reshape(shape2), vs[i].reshape(shape2))
        deltas[i], new_ms[i], new_vs[i] = [t.reshape(weights[i].shape) for t in (dl, nm_, nv_)]
    small_idx = [i for i in range(13) if i not in big]
    packs = [_pack([arrs[i] for i in small_idx], d) for arrs in (weights, grads, ms, vs)]
    outs = adamw("adamw_small", *packs)
    like = [weights[i] for i in small_idx]
    for res, dest in zip(outs, (deltas, new_ms, new_vs)):
        for i, val in zip(small_idx, _unpack(res, like, d)):
            dest[i] = val
    return (loss_out, grad_x, *grads, *deltas, *new_ms, *new_vs)
```

```python
import functools

import jax
import jax.numpy as jnp
from jax import lax
from jax.experimental import pallas as pl
from jax.experimental.pallas import tpu as pltpu

f32 = jnp.float32
bf16 = jnp.bfloat16
HIGHEST = lax.Precision.HIGHEST
MESH = pl.DeviceIdType.MESH

N_META = 16
CHUNK = 64
FRONT_PAD = (-N_META) % CHUNK
HEAD_DIM = 128
POOL_GROUPS = 4
POOL_WINDOWS = (2, 4, 8, 16)
CONV_WIDTH = 4
NORM_EPS = 1e-6
ADAM_LR, ADAM_B1, ADAM_B2, ADAM_EPS, ADAM_WD, ADAM_STEP = 0.001, 0.9, 0.999, 1e-08, 0.01, 10
LANES = 128
VMEM_LIMIT = 56 * 2**20


def _pc(body, **kw):
    return pl.pallas_call(body, **kw)


def _params(*sem, **kw):
    return pltpu.CompilerParams(dimension_semantics=sem or None, vmem_limit_bytes=VMEM_LIMIT, **kw)


def _dg(a, b, dims, prec=None):
    return lax.dot_general(a, b, (dims, ((), ())), precision=prec, preferred_element_type=f32)


@jax.custom_vjp
def mm_nn(a, b):
    return _dg(a.astype(bf16), b.astype(bf16), ((1,), (0,)))


@jax.custom_vjp
def mm_nt(a, b):
    return _dg(a.astype(bf16), b.astype(bf16), ((1,), (1,)))


@jax.custom_vjp
def mm_tn(a, b):
    return _dg(a.astype(bf16).T, b.astype(bf16), ((1,), (0,)))


mm_nn.defvjp(lambda a, b: (mm_nn(a, b), (a, b)), lambda r, dy: (mm_nt(dy, r[1]), mm_tn(r[0], dy)))
mm_nt.defvjp(lambda a, b: (mm_nt(a, b), (a, b)), lambda r, dy: (mm_nn(dy, r[1]), mm_tn(dy, r[0])))
mm_tn.defvjp(lambda a, b: (mm_tn(a, b), (a, b)), lambda r, dy: (mm_nt(r[1], dy), mm_nn(r[0], dy)))


def _split3(x):
    hi = x.astype(bf16)
    r1 = x - hi.astype(f32)
    mid = r1.astype(bf16)
    lo = (r1 - mid.astype(f32)).astype(bf16)
    return hi, mid, lo


@jax.custom_vjp
def mm_sel(sel, x):
    s = sel.astype(bf16)
    d = ((1,), (0,))
    hi, mid, lo = _split3(x)
    return _dg(s, hi, d) + _dg(s, mid, d) + _dg(s, lo, d)


def _mm_sel_bwd(sel, dy):
    s = sel.astype(bf16)
    d = ((0,), (0,))
    hi, mid, lo = _split3(dy)
    return jnp.zeros_like(sel), _dg(s, hi, d) + _dg(s, mid, d) + _dg(s, lo, d)


mm_sel.defvjp(lambda sel, x: (mm_sel(sel, x), sel), _mm_sel_bwd)


@jax.custom_vjp
def mm_pick(x, sel):
    s = sel.astype(bf16)
    d = ((1,), (0,))
    hi, mid, lo = _split3(x)
    return _dg(hi, s, d) + _dg(mid, s, d) + _dg(lo, s, d)


def _mm_pick_bwd(sel, dy):
    s = sel.astype(bf16)
    d = ((1,), (1,))
    hi, mid, lo = _split3(dy)
    return _dg(hi, s, d) + _dg(mid, s, d) + _dg(lo, s, d), jnp.zeros_like(sel)


mm_pick.defvjp(lambda x, sel: (mm_pick(x, sel), sel), _mm_pick_bwd)


def mm3(a, b):
    a_hi = a.astype(bf16)
    a_lo = (a - a_hi.astype(f32)).astype(bf16)
    b_hi = b.astype(bf16)
    b_lo = (b - b_hi.astype(f32)).astype(bf16)
    d = ((1,), (0,))
    return _dg(a_hi, b_hi, d) + _dg(a_hi, b_lo, d) + _dg(a_lo, b_hi, d)


def tri_inv(l):
    n = l.shape[0]
    eye = (lax.broadcasted_iota(jnp.int32, (n, n), 0) == lax.broadcasted_iota(jnp.int32, (n, n), 1)).astype(f32)
    m = -l
    t = eye + m
    k = 1
    while 2 * k < CHUNK:
        m = mm3(m, m)
        t = t + mm3(t, m)
        k *= 2
    return t


@functools.partial(jax.custom_vjp, nondiff_argnums=(1,))
def shift_rows(x, j):
    n = x.shape[0]
    rows = lax.broadcasted_iota(jnp.int32, x.shape, 0)
    if j >= 0:
        return jnp.where(rows >= j, pltpu.roll(x, j, 0), 0.0)
    return jnp.where(rows < n + j, pltpu.roll(x, n + j, 0), 0.0)


shift_rows.defvjp(lambda x, j: (shift_rows(x, j), None), lambda j, _, dy: (shift_rows(dy, -j),))


def sigmoid(x):
    return 1.0 / (1.0 + jnp.exp(-x))


def silu(x):
    return x * sigmoid(x)


def softplus(x):
    return jnp.maximum(x, 0.0) + jnp.log(1.0 + jnp.exp(-jnp.abs(x)))


def rmsnorm(x, w):
    return x * lax.rsqrt(jnp.mean(x * x, axis=-1, keepdims=True) + NORM_EPS) * w


def l2norm(x):
    return x * lax.rsqrt(jnp.sum(x * x, axis=-1, keepdims=True) + NORM_EPS)


def pool_fn(u, zp, mix, scale, group):
    rows = lax.broadcasted_iota(jnp.int32, u.shape, 0)
    sums = []
    s, w = u, 1
    while w < POOL_WINDOWS[-1]:
        s = s + shift_rows(s, w)
        w *= 2
        sums.append(s)
    total = sums[-1]
    for gi in range(POOL_GROUPS - 2, -1, -1):
        total = jnp.where(group == gi, sums[gi], total)
    window = jnp.left_shift(2, group)
    cnt = jnp.clip(rows - (FRONT_PAD - 1), 1, window).astype(f32)
    pooled = total / cnt - u
    return mm_nn(pooled, mix) * scale * silu(zp)


def conv_silu(x, w):
    k = CONV_WIDTH
    y = x * w[k - 1:k, :]
    for kk in range(k - 1):
        y = y + shift_rows(x, k - 1 - kk) * w[kk:kk + 1, :]
    return silu(y)


def _lane_pick(row, idx):
    lanes = lax.broadcasted_iota(jnp.int32, row.shape, 1)
    return jnp.sum(jnp.where(lanes == idx, row, 0.0), axis=1, keepdims=True)


def dn_pre_fn(qr, kr, vr, ba, cwq, cwk, cwv, a_log, dt_bias, head, n_heads):
    q = l2norm(conv_silu(qr, cwq)) * (HEAD_DIM ** -0.5)
    k = l2norm(conv_silu(kr, cwk))
    v = conv_silu(vr, cwv)
    r = lax.broadcasted_iota(jnp.int32, (LANES, LANES), 0)
    b_b = mm_pick(ba, (r == head).astype(f32))
    a_b = mm_pick(ba, (r == head + n_heads).astype(f32))
    real = lax.broadcasted_iota(jnp.int32, qr.shape, 0) >= FRONT_PAD
    beta_b = jnp.where(real, sigmoid(b_b), 0.0)
    g_b = jnp.where(real, -jnp.exp(_lane_pick(a_log, head)) * softplus(a_b + _lane_pick(dt_bias, head)), 0.0)
    return q, k, v, beta_b, g_b


def _chunk_masks(rows):
    r = lax.broadcasted_iota(jnp.int32, (rows, rows), 0)
    c = lax.broadcasted_iota(jnp.int32, (rows, rows), 1)
    same = (r // CHUNK) == (c // CHUNK)
    return same, jnp.logical_and(same, r >= c), jnp.logical_and(same, r > c)


def _lane0(rows):
    return (lax.broadcasted_iota(jnp.int32, (rows, LANES), 1) == 0).astype(bf16)


@jax.custom_vjp
def lane0_as_row(x):
    sel = _lane0(x.shape[0])
    d = ((1,), (1,))
    hi, mid, lo = _split3(x)
    return _dg(sel, hi, d) + _dg(sel, mid, d) + _dg(sel, lo, d)


def _lane0_as_row_bwd(rows, dy):
    sel = _lane0(rows)
    d = ((0,), (0,))
    hi, mid, lo = _split3(dy)
    return (_dg(hi, sel, d) + _dg(mid, sel, d) + _dg(lo, sel, d),)


lane0_as_row.defvjp(lambda x: (lane0_as_row(x), x.shape[0]), _lane0_as_row_bwd)


def gate_fn(g_b):
    rows = g_b.shape[0]
    same, causal, _ = _chunk_masks(rows)
    gcum_b = mm_sel(causal.astype(f32), g_b)
    glast_b = mm_sel(same.astype(f32), g_b)
    g_rows = jnp.broadcast_to(gcum_b[:, :1], (rows, rows))
    decay = jnp.where(causal, jnp.exp(jnp.where(causal, g_rows - lane0_as_row(gcum_b), 0.0)), 0.0)
    return decay, jnp.exp(gcum_b), jnp.exp(glast_b - gcum_b), jnp.exp(glast_b)


def _fold_matrix(rows):
    r = lax.broadcasted_iota(jnp.int32, (rows, LANES), 0)
    c = lax.broadcasted_iota(jnp.int32, (rows, LANES), 1)
    return (r % CHUNK == c).astype(bf16)


@jax.custom_vjp
def fold_chunks(x):
    return _dg(x.astype(bf16), _fold_matrix(x.shape[0]), ((1,), (0,)))


def _fold_chunks_bwd(rows, dy):
    fold = _fold_matrix(rows)
    d = ((1,), (1,))
    hi, mid, lo = _split3(dy)
    return (_dg(hi, fold, d) + _dg(mid, fold, d) + _dg(lo, fold, d),)


fold_chunks.defvjp(lambda x: (fold_chunks(x), x.shape[0]), _fold_chunks_bwd)


def lmat_fn(k, beta_b, decay):
    _, _, strict = _chunk_masks(k.shape[0])
    return jnp.where(strict, mm_nt(k * beta_b, k) * decay, 0.0)


def intra_fn(tmat, q, k, v, beta_b, decay, eg, kfac):
    _, causal, _ = _chunk_masks(q.shape[0])
    k_beta = k * beta_b
    u_c = mm_nn(tmat, v * beta_b)
    w_c = mm_nn(tmat, k_beta * eg)
    qk = jnp.where(causal, mm_nt(q, k) * decay, 0.0)
    return u_c, w_c, q * eg, k * kfac, fold_chunks(qk)


def seq_fn(u_c, w_c, q_dec, k_dec, qk_c, gl, zd, norm_w, state):
    v_new = u_c - mm_nn(w_c, state)
    o = mm_nn(q_dec, state) + mm_nn(qk_c[:, :CHUNK], v_new)
    new_state = state * gl[0:1, :] + mm_tn(k_dec, v_new)
    return rmsnorm(o, norm_w) * silu(zd), new_state


def loss_fn(o, w, tgt):
    err = rmsnorm(o, w) - tgt
    return 0.5 * jnp.sum(jnp.mean(err * err, axis=-1))


def matmul(name, a, b, *, ta=False, tb=False, tm, tn, tk=None, extras=(), epi=None, out_dtypes=(f32,)):
    m, k = (a.shape[1], a.shape[0]) if ta else a.shape
    n = b.shape[0] if tb else b.shape[1]
    tm, tn, tk = min(tm, m), min(tn, n), min(tk or k, k)
    assert m % tm == 0 and n % tn == 0 and k % tk == 0, (name, m, n, k, tm, tn, tk)
    nm, nn, nk = m // tm, n // tn, k // tk
    a_spec = pl.BlockSpec((tk, tm), lambda i, j, kk: (kk, i)) if ta else pl.BlockSpec((tm, tk), lambda i, j, kk: (i, kk))
    b_spec = pl.BlockSpec((tn, tk), lambda i, j, kk: (j, kk)) if tb else pl.BlockSpec((tk, tn), lambda i, j, kk: (kk, j))
    ex_specs = []
    for _, off in extras:
        assert off % tn == 0, (name, off, tn)
        ex_specs.append(pl.BlockSpec((tm, tn), functools.partial(lambda i, j, kk, o: (i, o + j), o=off // tn)))
    n_ex, n_out = len(extras), len(out_dtypes)
    dims = ((0 if ta else 1,), (1 if tb else 0,))

    def body(a_ref, b_ref, *rest):
        ex_refs, out_refs = rest[:n_ex], rest[n_ex:n_ex + n_out]

        def finish(acc):
            res = epi(acc, *[r[...] for r in ex_refs]) if epi is not None else (acc,)
            for o_ref, val in zip(out_refs, res):
                o_ref[...] = val.astype(o_ref.dtype)

        p = _dg(a_ref[...], b_ref[...], dims)
        if nk == 1:
            finish(p)
        else:
            acc_ref = rest[-1]
            kk = pl.program_id(2)

            @pl.when(kk == 0)
            def _():
                acc_ref[...] = p

            @pl.when(kk > 0)
            def _():
                acc_ref[...] += p

            @pl.when(kk == nk - 1)
            def _():
                finish(acc_ref[...])

    outs = _pc(
        body, name=name, grid=(nm, nn, nk),
        in_specs=[a_spec, b_spec] + ex_specs,
        out_specs=[pl.BlockSpec((tm, tn), lambda i, j, kk: (i, j))] * n_out,
        out_shape=[jax.ShapeDtypeStruct((m, n), dt) for dt in out_dtypes],
        scratch_shapes=[pltpu.VMEM((tm, tn), f32)] if nk > 1 else [],
        compiler_params=_params("parallel", "parallel", "arbitrary"),
    )(a, b, *[e for e, _ in extras])
    return outs[0] if n_out == 1 else outs


def _row_tile(rows, cols, n_arrays, itemsize=4, budget=24 * 2**20):
    best = None
    for t in range(16, rows + 1, 16):
        if rows % t == 0 and 2 * n_arrays * t * cols * itemsize <= budget:
            best = t
    return best or rows


def _tile(rows, cols, n_arrays, budget=24 * 2**20):
    if rows % 16 == 0 or cols % LANES != 0:
        return _row_tile(rows, cols, n_arrays, budget=budget), cols
    fits = [t for t in range(LANES, cols + 1, LANES) if cols % t == 0 and 2 * n_arrays * rows * t * 4 <= budget]
    return rows, (max(fits) if fits else LANES)


def cast_bf16(name, x):
    rows, cols = x.shape
    tr, tc = _tile(rows, cols, 2)

    def body(x_ref, o_ref):
        o_ref[...] = x_ref[...].astype(bf16)

    blk = pl.BlockSpec((tr, tc), lambda i, j: (i, j))
    return _pc(body, name=name, grid=(rows // tr, cols // tc), in_specs=[blk], out_specs=blk,
               out_shape=jax.ShapeDtypeStruct(x.shape, bf16), compiler_params=_params("parallel", "parallel"))(x)


def norm_fwd(hp, norm_w, tm):
    tp, d = hp.shape

    def body(h_ref, w_ref, o_ref):
        o_ref[...] = rmsnorm(h_ref[...], w_ref[...]).astype(bf16)

    return _pc(body, name="norm_fwd", grid=(tp // tm,),
               in_specs=[pl.BlockSpec((tm, d), lambda i: (i, 0)), pl.BlockSpec((1, d), lambda i: (0, 0))],
               out_specs=pl.BlockSpec((tm, d), lambda i: (i, 0)), out_shape=jax.ShapeDtypeStruct((tp, d), bf16),
               compiler_params=_params("parallel"))(hp, norm_w)


def norm_bwd(hp, norm_w, dxn, dout, tm):
    tp, d = hp.shape

    def body(h_ref, w_ref, dxn_ref, dout_ref, dh_ref, dw_ref):
        _, vjp = jax.vjp(rmsnorm, h_ref[...], w_ref[...])
        dh, dw = vjp(dxn_ref[...])
        dh_ref[...] = dh + dout_ref[...]

        @pl.when(pl.program_id(0) == 0)
        def _():
            dw_ref[...] = jnp.zeros_like(dw_ref)

        dw_ref[...] += dw

    row = pl.BlockSpec((tm, d), lambda i: (i, 0))
    vec = pl.BlockSpec((1, d), lambda i: (0, 0))
    return _pc(body, name="norm_bwd", grid=(tp // tm,), in_specs=[row, vec, row, row], out_specs=[row, vec],
               out_shape=[jax.ShapeDtypeStruct((tp, d), f32), jax.ShapeDtypeStruct((1, d), f32)],
               compiler_params=_params("arbitrary"))(hp, norm_w, dxn, dout)


def pool_fwd(proj, mix, scale, pw):
    tp = proj.shape[0]
    g = pw // POOL_GROUPS

    def body(u_ref, z_ref, mix_ref, s_ref, y_ref):
        y_ref[...] = pool_fn(u_ref[...], z_ref[...], mix_ref[0], s_ref[...], pl.program_id(0)).astype(bf16)

    return _pc(body, name="pool_fwd", grid=(POOL_GROUPS,),
               in_specs=[pl.BlockSpec((tp, g), lambda i: (0, i)), pl.BlockSpec((tp, g), lambda i: (0, POOL_GROUPS + i)),
                         pl.BlockSpec((1, g, g), lambda i: (i, 0, 0)), pl.BlockSpec((1, g), lambda i: (0, i))],
               out_specs=pl.BlockSpec((tp, g), lambda i: (0, i)), out_shape=jax.ShapeDtypeStruct((tp, pw), bf16),
               compiler_params=_params("parallel"))(proj, proj, mix, scale)


def pool_bwd(proj, mix, scale, dy, pw):
    tp = proj.shape[0]
    g = pw // POOL_GROUPS

    def body(u_ref, z_ref, mix_ref, s_ref, dy_ref, du_ref, dz_ref, dmix_ref, ds_ref):
        grp = pl.program_id(0)
        _, vjp = jax.vjp(lambda u, z, m, s: pool_fn(u, z, m, s, grp), u_ref[...], z_ref[...], mix_ref[0].astype(f32), s_ref[...])
        du, dz, dmix, ds = vjp(dy_ref[...])
        du_ref[...] = du.astype(bf16)
        dz_ref[...] = dz.astype(bf16)
        dmix_ref[0] = dmix
        ds_ref[...] = ds

    col = pl.BlockSpec((tp, g), lambda i: (0, i))
    return _pc(body, name="pool_bwd", grid=(POOL_GROUPS,),
               in_specs=[col, pl.BlockSpec((tp, g), lambda i: (0, POOL_GROUPS + i)),
                         pl.BlockSpec((1, g, g), lambda i: (i, 0, 0)), pl.BlockSpec((1, g), lambda i: (0, i)), col],
               out_specs=[col, col, pl.BlockSpec((1, g, g), lambda i: (i, 0, 0)), pl.BlockSpec((1, g), lambda i: (0, i))],
               out_shape=[jax.ShapeDtypeStruct((tp, pw), bf16), jax.ShapeDtypeStruct((tp, pw), bf16),
                          jax.ShapeDtypeStruct((POOL_GROUPS, g, g), f32), jax.ShapeDtypeStruct((1, pw), f32)],
               compiler_params=_params("parallel"))(proj, proj, mix, scale, dy)


def _dn_pre_specs(tp, n_heads, q_off):
    hb = lambda off: pl.BlockSpec((tp, HEAD_DIM), functools.partial(lambda h, o: (0, o + h), o=off))
    cw = lambda off: pl.BlockSpec((CONV_WIDTH, HEAD_DIM), functools.partial(lambda h, o: (0, o + h), o=off))
    whole = lambda shape: pl.BlockSpec(shape, lambda h: (0, 0))
    return ([hb(q_off), hb(q_off + n_heads), hb(q_off + 2 * n_heads), whole((tp, LANES)),
             cw(0), cw(n_heads), cw(2 * n_heads), whole((1, LANES)), whole((1, LANES))], hb, cw, whole)


def dn_pre_fwd(proj, ba, conv_w, a_log, dt_bias, n_heads, q_off):
    tp = proj.shape[0]
    in_specs, hb, _, _ = _dn_pre_specs(tp, n_heads, q_off)

    def body(q_ref, k_ref, v_ref, ba_ref, cq_ref, ck_ref, cv_ref, al_ref, dt_ref, *out_refs):
        outs = dn_pre_fn(q_ref[...], k_ref[...], v_ref[...], ba_ref[...], cq_ref[...], ck_ref[...], cv_ref[...],
                         al_ref[...], dt_ref[...], pl.program_id(0), n_heads)
        for o_ref, val in zip(out_refs, outs):
            o_ref[...] = val

    return _pc(body, name="dn_pre_fwd", grid=(n_heads,), in_specs=in_specs, out_specs=[hb(0)] * 5,
               out_shape=[jax.ShapeDtypeStruct((tp, n_heads * HEAD_DIM), f32)] * 5,
               compiler_params=_params("parallel"))(proj, proj, proj, ba, conv_w, conv_w, conv_w, a_log, dt_bias)


def dn_pre_bwd(proj, ba, conv_w, a_log, dt_bias, cots, n_heads, q_off):
    tp = proj.shape[0]
    in_specs, hb, cw, whole = _dn_pre_specs(tp, n_heads, q_off)

    def body(q_ref, k_ref, v_ref, ba_ref, cq_ref, ck_ref, cv_ref, al_ref, dt_ref, dq_ref, dk_ref, dv_ref, db_ref, dg_ref,
             dqr_ref, dkr_ref, dvr_ref, dba_ref, dcq_ref, dck_ref, dcv_ref, dal_ref, ddt_ref):
        head = pl.program_id(0)
        fn = lambda *args: dn_pre_fn(*args, head, n_heads)
        _, vjp = jax.vjp(fn, q_ref[...], k_ref[...], v_ref[...], ba_ref[...], cq_ref[...], ck_ref[...], cv_ref[...],
                         al_ref[...], dt_ref[...])
        dqr, dkr, dvr, dba, dcq, dck, dcv, dal, ddt = vjp((dq_ref[...], dk_ref[...], dv_ref[...], db_ref[...], dg_ref[...]))
        dqr_ref[...] = dqr.astype(bf16)
        dkr_ref[...] = dkr.astype(bf16)
        dvr_ref[...] = dvr.astype(bf16)
        dcq_ref[...] = dcq
        dck_ref[...] = dck
        dcv_ref[...] = dcv

        @pl.when(head == 0)
        def _():
            dba_ref[...] = jnp.zeros_like(dba_ref)
            dal_ref[...] = jnp.zeros_like(dal_ref)
            ddt_ref[...] = jnp.zeros_like(ddt_ref)

        dba_ref[...] += dba
        dal_ref[...] += dal
        ddt_ref[...] += ddt

    w = n_heads * HEAD_DIM
    return _pc(body, name="dn_pre_bwd", grid=(n_heads,), in_specs=in_specs + [hb(0)] * 5,
               out_specs=[hb(0)] * 3 + [whole((tp, LANES)), cw(0), cw(0), cw(0), whole((1, LANES)), whole((1, LANES))],
               out_shape=[jax.ShapeDtypeStruct((tp, w), bf16)] * 3 + [jax.ShapeDtypeStruct((tp, LANES), f32)]
               + [jax.ShapeDtypeStruct((CONV_WIDTH, w), f32)] * 3 + [jax.ShapeDtypeStruct((1, LANES), f32)] * 2,
               compiler_params=_params("arbitrary"))(proj, proj, proj, ba, conv_w, conv_w, conv_w, a_log, dt_bias, *cots)


def _super_rows(tp):
    n = tp // CHUNK
    return CHUNK * max(j for j in (4, 3, 2, 1) if n % j == 0)


def dn_intra_fwd(q, k, v, beta_b, g_b, n_heads):
    tp = q.shape[0]
    rows = _super_rows(tp)
    ns = tp // rows

    def body(q_ref, k_ref, v_ref, b_ref, g_ref, u_ref, w_ref, qd_ref, kd_ref, qk_ref, gl_ref, t_ref):
        decay, eg, kfac, gl = gate_fn(g_ref[...])
        tmat = tri_inv(lmat_fn(k_ref[...], b_ref[...], decay))
        u_c, w_c, q_dec, k_dec, qk_c = intra_fn(tmat, q_ref[...], k_ref[...], v_ref[...], b_ref[...], decay, eg, kfac)
        u_ref[...] = u_c
        w_ref[...] = w_c
        qd_ref[...] = q_dec
        kd_ref[...] = k_dec
        qk_ref[...] = qk_c
        gl_ref[...] = gl
        t_ref[0, 0] = tmat

    blk = pl.BlockSpec((rows, HEAD_DIM), lambda h, s: (s, h))
    return _pc(body, name="dn_intra_fwd", grid=(n_heads, ns), in_specs=[blk] * 5,
               out_specs=[blk] * 6 + [pl.BlockSpec((1, 1, rows, rows), lambda h, s: (h, s, 0, 0))],
               out_shape=[jax.ShapeDtypeStruct(q.shape, f32)] * 6 + [jax.ShapeDtypeStruct((n_heads, ns, rows, rows), f32)],
               compiler_params=_params("parallel", "parallel"))(q, k, v, beta_b, g_b)


def dn_intra_bwd(q, k, v, beta_b, g_b, tmats, cots, n_heads):
    tp = q.shape[0]
    rows = _super_rows(tp)
    ns = tp // rows

    def body(q_ref, k_ref, v_ref, b_ref, g_ref, t_ref, du_ref, dw_ref, dqd_ref, dkd_ref, dqk_ref, dgl_ref,
             dq_ref, dk_ref, dv_ref, db_ref, dg_ref):
        qv, kv, vv, bv, tmat = q_ref[...], k_ref[...], v_ref[...], b_ref[...], t_ref[0, 0]
        (decay, eg, kfac, _), gate_vjp = jax.vjp(gate_fn, g_ref[...])
        _, intra_vjp = jax.vjp(intra_fn, tmat, qv, kv, vv, bv, decay, eg, kfac)
        dt, dq, dk, dv, db, ddecay, deg, dkfac = intra_vjp((du_ref[...], dw_ref[...], dqd_ref[...], dkd_ref[...], dqk_ref[...]))
        tt = tmat.T
        dl = -mm3(mm3(tt, dt), tt)
        _, lmat_vjp = jax.vjp(lmat_fn, kv, bv, decay)
        dk2, db2, ddecay2 = lmat_vjp(dl)
        (dg,) = gate_vjp((ddecay + ddecay2, deg, dkfac, dgl_ref[...]))
        dq_ref[...] = dq
        dk_ref[...] = dk + dk2
        dv_ref[...] = dv
        db_ref[...] = db + db2
        dg_ref[...] = dg

    blk = pl.BlockSpec((rows, HEAD_DIM), lambda h, s: (s, h))
    return _pc(body, name="dn_intra_bwd", grid=(n_heads, ns),
               in_specs=[blk] * 5 + [pl.BlockSpec((1, 1, rows, rows), lambda h, s: (h, s, 0, 0))] + [blk] * 6,
               out_specs=[blk] * 5, out_shape=[jax.ShapeDtypeStruct(q.shape, f32)] * 5,
               compiler_params=_params("parallel", "parallel"))(q, k, v, beta_b, g_b, tmats, *cots)


def dn_seq_fwd(inter, proj, dn_norm_w, n_heads, zd_off):
    tp, width = inter[0].shape
    n_chunks = tp // CHUNK

    def body(u_ref, w_ref, qd_ref, kd_ref, qk_ref, gl_ref, z_ref, nw_ref, y_ref, s_ref, state):
        @pl.when(pl.program_id(0) == 0)
        def _():
            state[...] = jnp.zeros_like(state)

        for h in range(n_heads):
            sl = slice(h * HEAD_DIM, (h + 1) * HEAD_DIM)
            st = state[h]
            s_ref[0, h] = st
            y, new_state = seq_fn(u_ref[:, sl], w_ref[:, sl], qd_ref[:, sl], kd_ref[:, sl], qk_ref[:, sl], gl_ref[:, sl],
                                  z_ref[:, sl], nw_ref[...], st)
            y_ref[:, sl] = y.astype(bf16)
            state[h] = new_state

    blk = pl.BlockSpec((CHUNK, width), lambda n: (n, 0))
    return _pc(body, name="dn_seq_fwd", grid=(n_chunks,),
               in_specs=[blk] * 6 + [pl.BlockSpec((CHUNK, width), lambda n: (n, zd_off)), pl.BlockSpec((1, HEAD_DIM), lambda n: (0, 0))],
               out_specs=[blk, pl.BlockSpec((1, n_heads, HEAD_DIM, HEAD_DIM), lambda n: (n, 0, 0, 0))],
               out_shape=[jax.ShapeDtypeStruct((tp, width), bf16), jax.ShapeDtypeStruct((n_chunks, n_heads, HEAD_DIM, HEAD_DIM), f32)],
               scratch_shapes=[pltpu.VMEM((n_heads, HEAD_DIM, HEAD_DIM), f32)],
               compiler_params=_params("arbitrary"))(*inter, proj, dn_norm_w)


def dn_seq_bwd(inter, proj, dn_norm_w, states, dy, n_heads, zd_off):
    tp, width = inter[0].shape
    n_chunks = tp // CHUNK
    last = n_chunks - 1

    def body(u_ref, w_ref, qd_ref, kd_ref, qk_ref, gl_ref, z_ref, nw_ref, s_ref, dy_ref,
             du_ref, dw_ref, dqd_ref, dkd_ref, dqk_ref, dgl_ref, dz_ref, dnw_ref, dstate):
        @pl.when(pl.program_id(0) == 0)
        def _():
            dstate[...] = jnp.zeros_like(dstate)
            dnw_ref[...] = jnp.zeros_like(dnw_ref)

        dnw = jnp.zeros((1, HEAD_DIM), f32)
        for h in range(n_heads):
            sl = slice(h * HEAD_DIM, (h + 1) * HEAD_DIM)
            _, vjp = jax.vjp(seq_fn, u_ref[:, sl], w_ref[:, sl], qd_ref[:, sl], kd_ref[:, sl], qk_ref[:, sl], gl_ref[:, sl],
                             z_ref[:, sl], nw_ref[...], s_ref[0, h])
            du, dw, dqd, dkd, dqk, dgl, dz, dn, ds = vjp((dy_ref[:, sl], dstate[h]))
            du_ref[:, sl] = du
            dw_ref[:, sl] = dw
            dqd_ref[:, sl] = dqd
            dkd_ref[:, sl] = dkd
            dqk_ref[:, sl] = dqk
            dgl_ref[:, sl] = dgl
            dz_ref[:, sl] = dz.astype(bf16)
            dnw = dnw + dn
            dstate[h] = ds
        dnw_ref[...] += dnw

    blk = pl.BlockSpec((CHUNK, width), lambda n: (last - n, 0))
    return _pc(body, name="dn_seq_bwd", grid=(n_chunks,),
               in_specs=[blk] * 6 + [pl.BlockSpec((CHUNK, width), lambda n: (last - n, zd_off)), pl.BlockSpec((1, HEAD_DIM), lambda n: (0, 0)),
                         pl.BlockSpec((1, n_heads, HEAD_DIM, HEAD_DIM), lambda n: (last - n, 0, 0, 0)), blk],
               out_specs=[blk] * 7 + [pl.BlockSpec((1, HEAD_DIM), lambda n: (0, 0))],
               out_shape=[jax.ShapeDtypeStruct((tp, width), f32)] * 6 + [jax.ShapeDtypeStruct((tp, width), bf16),
                                                                          jax.ShapeDtypeStruct((1, HEAD_DIM), f32)],
               scratch_shapes=[pltpu.VMEM((n_heads, HEAD_DIM, HEAD_DIM), f32)],
               compiler_params=_params("arbitrary"))(*inter, proj, dn_norm_w, states, dy)


def loss_stage(out, final_w, target):
    tp, d = out.shape
    n_tiles = tp // CHUNK

    def body(o_ref, w_ref, t_ref, loss_ref, do_ref, dob_ref, dw_ref):
        i = pl.program_id(0)

        @pl.when(i == 0)
        def _():
            loss_ref[...] = jnp.zeros_like(loss_ref)
            dw_ref[...] = jnp.zeros_like(dw_ref)

        scored = (i > 0).astype(f32)
        val, (do, dw) = jax.value_and_grad(lambda o, w: scored * loss_fn(o, w, t_ref[...]), argnums=(0, 1))(o_ref[...], w_ref[...])
        loss_ref[...] += jnp.full(loss_ref.shape, val, f32)
        do_ref[...] = do
        dob_ref[...] = do.astype(bf16)
        dw_ref[...] += dw

    row = pl.BlockSpec((CHUNK, d), lambda i: (i, 0))
    vec = pl.BlockSpec((1, d), lambda i: (0, 0))
    return _pc(body, name="loss_stage", grid=(n_tiles,),
               in_specs=[row, vec, pl.BlockSpec((CHUNK, d), lambda i: (jnp.maximum(i - 1, 0), 0))],
               out_specs=[pl.BlockSpec((1, LANES), lambda i: (0, 0)), row, row, vec],
               out_shape=[jax.ShapeDtypeStruct((1, LANES), f32), jax.ShapeDtypeStruct((tp, d), f32),
                          jax.ShapeDtypeStruct((tp, d), bf16), jax.ShapeDtypeStruct((1, d), f32)],
               compiler_params=_params("arbitrary"))(out, final_w, target)


def _adam_update(w, g, m, v):
    nm = ADAM_B1 * m + (1.0 - ADAM_B1) * g
    nv = ADAM_B2 * v + (1.0 - ADAM_B2) * jnp.square(g)
    m_hat = nm / (1.0 - ADAM_B1 ** ADAM_STEP)
    v_hat = nv / (1.0 - ADAM_B2 ** ADAM_STEP)
    return -ADAM_LR * (m_hat / (jnp.sqrt(v_hat) + ADAM_EPS) + ADAM_WD * w), nm, nv


def adamw(name, w, g, m, v):
    rows, cols = w.shape
    t = _row_tile(rows, cols, 7)

    def body(w_ref, g_ref, m_ref, v_ref, d_ref, nm_ref, nv_ref):
        d_ref[...], nm_ref[...], nv_ref[...] = _adam_update(w_ref[...], g_ref[...], m_ref[...], v_ref[...])

    blk = pl.BlockSpec((t, cols), lambda i: (i, 0))
    return _pc(body, name=name, grid=(rows // t,), in_specs=[blk] * 4, out_specs=[blk] * 3,
               out_shape=[jax.ShapeDtypeStruct(w.shape, f32)] * 3, compiler_params=_params("parallel"))(w, g, m, v)


def adamw_joined(name, w, g_mine, g_theirs, m, v, half, axis):
    rows, cols = w.shape
    hr, hc = g_mine.shape
    tr, tc = _tile(hr, hc, 9)
    nr, nc = hr // tr, hc // tc

    def body(half_ref, w_ref, gm_ref, gt_ref, m_ref, v_ref, g_ref, d_ref, nm_ref, nv_ref):
        pos = pl.program_id(axis) // (nr if axis == 0 else nc)
        g = jnp.where(pos == half_ref[0], gm_ref[...], gt_ref[...])
        delta, nm, nv = _adam_update(w_ref[...], g, m_ref[...], v_ref[...])
        g_ref[...] = g
        d_ref[...] = delta
        nm_ref[...] = nm
        nv_ref[...] = nv

    whole = pl.BlockSpec((tr, tc), lambda i, j, hf: (i, j))
    part = pl.BlockSpec((tr, tc), lambda i, j, hf: (i % nr, j % nc))
    grid_spec = pltpu.PrefetchScalarGridSpec(num_scalar_prefetch=1, grid=(rows // tr, cols // tc),
                                             in_specs=[whole, part, part, whole, whole], out_specs=[whole] * 4)
    return _pc(body, name=name, grid_spec=grid_spec, out_shape=[jax.ShapeDtypeStruct(w.shape, f32)] * 4,
               compiler_params=_params("parallel", "parallel"))(half, w, g_mine, g_theirs, m, v)


def add_halves(name, g, recv, half, kind):
    s, r, c = recv.shape
    tr, tc = _tile(r, c, 3)
    nc = c // tc

    def body(half_ref, g_ref, r_ref, o_ref):
        o_ref[...] = (g_ref[...].reshape(r_ref.shape).astype(f32) + r_ref[...].astype(f32)).astype(bf16)

    if kind == "lead":
        g_spec = pl.BlockSpec((1, 1, tr, tc), lambda i, j, k, hf: (hf[0], i, j, k))
    else:
        g_spec = pl.BlockSpec((1, tr, tc), lambda i, j, k, hf: (i, j, hf[0] * nc + k))
    blk = pl.BlockSpec((1, tr, tc), lambda i, j, k, hf: (i, j, k))
    grid_spec = pltpu.PrefetchScalarGridSpec(num_scalar_prefetch=1, grid=(s, r // tr, nc), in_specs=[g_spec, blk], out_specs=blk)
    return _pc(body, name=name, grid_spec=grid_spec, out_shape=jax.ShapeDtypeStruct((s, r, c), bf16),
               compiler_params=_params("parallel", "parallel", "parallel"))(half, g, recv)


def sum_leading(name, x, out_dtype=f32):
    s, r, c = x.shape
    tr, tc = _tile(r, c, s + 1)

    def body(x_ref, o_ref):
        acc = x_ref[0].astype(f32)
        for i in range(1, s):
            acc = acc + x_ref[i].astype(f32)
        o_ref[...] = acc.astype(out_dtype)

    return _pc(body, name=name, grid=(r // tr, c // tc), in_specs=[pl.BlockSpec((s, tr, tc), lambda i, j: (0, i, j))],
               out_specs=pl.BlockSpec((tr, tc), lambda i, j: (i, j)), out_shape=jax.ShapeDtypeStruct((r, c), out_dtype),
               compiler_params=_params("parallel", "parallel"))(x)


def _place():
    x, y, c = lax.axis_index("x"), lax.axis_index("y"), lax.axis_index("c")
    return x, y, c


def _other_chips(x, y):
    return [(1 - x, y), (x, 1 - y), (1 - x, 1 - y)]


_ANY = pl.BlockSpec(memory_space=pl.ANY)


def _half_view(ref, half, kind, lead=()):
    if kind == "lead":
        return ref.at[(*lead, half)]
    width = ref.shape[-1] // 2
    return ref.at[(*lead, *([slice(None)] * (len(ref.shape) - len(lead) - 1)), pl.ds(half * width, width))]


def gather_weights(shards, kinds):
    n = len(shards)

    def body(*refs):
        ins, outs = refs[:n], refs[n:2 * n]
        send1, recv1, send2, recv2 = refs[2 * n:]
        x, y, c = _place()
        chip = 2 * x + y
        sibling = (x, y, 1 - c)
        others = _other_chips(x, y)

        def remote(src, dst, s_sem, r_sem, to):
            return pltpu.make_async_remote_copy(src_ref=src, dst_ref=dst, send_sem=s_sem, recv_sem=r_sem,
                                                device_id=to, device_id_type=MESH)

        first = []
        for a in range(n):
            for j, (ox, oy) in enumerate(others):
                first.append(remote(_half_view(ins[a], c, kinds[a]), _half_view(outs[a], c, kinds[a], (chip,)),
                                    send1.at[a, j], recv1.at[a, j], (ox, oy, c)))
        for cp in first:
            cp.start()
        passed = []
        for a in range(n):
            for j, (ox, oy) in enumerate(others):
                slab = _half_view(outs[a], c, kinds[a], (2 * ox + oy,))
                remote(slab, slab, send1.at[a, j], recv1.at[a, j], (ox, oy, c)).wait_recv()
                cp = remote(slab, slab, send2.at[a, j], recv2.at[a, j], sibling)
                cp.start()
                passed.append(cp)
        for a in range(n):
            for j, (ox, oy) in enumerate(others):
                slab = _half_view(outs[a], 1 - c, kinds[a], (2 * ox + oy,))
                remote(slab, slab, send2.at[a, j], recv2.at[a, j], sibling).wait_recv()
        for cp in first + passed:
            cp.wait_send()

    sems = [pltpu.SemaphoreType.DMA((n, 3))] * 4
    return _pc(body, name="gather_weights", in_specs=[_ANY] * n, out_specs=[_ANY] * n,
               out_shape=[jax.ShapeDtypeStruct((4,) + s.shape, s.dtype) for s in shards],
               scratch_shapes=sems)(*shards)


def swap_with_sibling(name, sends):
    n = len(sends)

    def body(*refs):
        ins, outs = refs[:n], refs[n:2 * n]
        send, recv = refs[2 * n:]
        x, y, c = _place()
        cps = [pltpu.make_async_remote_copy(src_ref=ins[a], dst_ref=outs[a], send_sem=send.at[a], recv_sem=recv.at[a],
                                            device_id=(x, y, 1 - c), device_id_type=MESH) for a in range(n)]
        for cp in cps:
            cp.start()
        for cp in cps:
            cp.wait()

    return _pc(body, name=name, in_specs=[_ANY] * n, out_specs=[_ANY] * n,
               out_shape=[jax.ShapeDtypeStruct(s.shape, s.dtype) for s in sends],
               scratch_shapes=[pltpu.SemaphoreType.DMA((n,))] * 2)(*sends)


def send_grad_halves(grads, kinds):
    n = len(grads)

    def body(*refs):
        ins, outs = refs[:n], refs[n:2 * n]
        send, recv = refs[2 * n:]
        x, y, c = _place()
        cps = [pltpu.make_async_remote_copy(src_ref=_half_view(ins[a], 1 - c, kinds[a]), dst_ref=outs[a], send_sem=send.at[a],
                                            recv_sem=recv.at[a], device_id=(x, y, 1 - c), device_id_type=MESH) for a in range(n)]
        for cp in cps:
            cp.start()
        for cp in cps:
            cp.wait()

    shape = lambda g, kind: g.shape[1:] if kind == "lead" else g.shape[:-1] + (g.shape[-1] // 2,)
    return _pc(body, name="send_grad_halves", in_specs=[_ANY] * n, out_specs=[_ANY] * n,
               out_shape=[jax.ShapeDtypeStruct(shape(g, k), g.dtype) for g, k in zip(grads, kinds)],
               scratch_shapes=[pltpu.SemaphoreType.DMA((n,))] * 2)(*grads)


def scatter_to_chips(parts):
    n = len(parts)

    def body(*refs):
        ins, outs = refs[:n], refs[n:2 * n]
        send, recv = refs[2 * n:]
        x, y, c = _place()
        chip = 2 * x + y
        others = _other_chips(x, y)
        cps = []
        for a in range(n):
            for j, (ox, oy) in enumerate(others):
                cps.append(pltpu.make_async_remote_copy(
                    src_ref=ins[a].at[2 * ox + oy], dst_ref=outs[a].at[chip], send_sem=send.at[a, j], recv_sem=recv.at[a, j],
                    device_id=(ox, oy, c), device_id_type=MESH))
        for cp in cps:
            cp.start()
        for a in range(n):
            for j, (ox, oy) in enumerate(others):
                slab = outs[a].at[2 * ox + oy]
                pltpu.make_async_remote_copy(src_ref=slab, dst_ref=slab, send_sem=send.at[a, j], recv_sem=recv.at[a, j],
                                             device_id=(ox, oy, c), device_id_type=MESH).wait_recv()
        for cp in cps:
            cp.wait_send()

    return _pc(body, name="scatter_to_chips", in_specs=[_ANY] * n, out_specs=[_ANY] * n,
               out_shape=[jax.ShapeDtypeStruct(p.shape, p.dtype) for p in parts],
               scratch_shapes=[pltpu.SemaphoreType.DMA((n, 3))] * 2)(*parts)


def gather_from_all(pack):
    def body(in_ref, out_ref, send, recv):
        x, y, c = _place()
        me = 4 * x + 2 * y + c
        flips = [(fx, fy, fc) for fx in (0, 1) for fy in (0, 1) for fc in (0, 1)][1:]
        peers = [(jnp.where(fx, 1 - x, x), jnp.where(fy, 1 - y, y), jnp.where(fc, 1 - c, c)) for fx, fy, fc in flips]
        cps = [pltpu.make_async_remote_copy(src_ref=in_ref, dst_ref=out_ref.at[me], send_sem=send.at[j], recv_sem=recv.at[j],
                                            device_id=p, device_id_type=MESH) for j, p in enumerate(peers)]
        for cp in cps:
            cp.start()
        for j, (px, py, pc) in enumerate(peers):
            slab = out_ref.at[4 * px + 2 * py + pc]
            pltpu.make_async_remote_copy(src_ref=slab, dst_ref=slab, send_sem=send.at[j], recv_sem=recv.at[j],
                                         device_id=(px, py, pc), device_id_type=MESH).wait_recv()
        for cp in cps:
            cp.wait_send()

    return _pc(body, name="gather_from_all", in_specs=[_ANY], out_specs=_ANY,
               out_shape=jax.ShapeDtypeStruct((8,) + pack.shape, pack.dtype),
               scratch_shapes=[pltpu.SemaphoreType.DMA((7,)), pltpu.SemaphoreType.DMA((7,))])(pack)


def _pack(arrays, width):
    rows = []
    for a in arrays:
        flat = a.reshape(-1).astype(f32)
        pad = (-flat.shape[0]) % (8 * width)
        rows.append(jnp.pad(flat, (0, pad)).reshape(-1, width))
    return jnp.concatenate(rows, axis=0)


def _unpack(pack, like, width):
    out, row = [], 0
    for a in like:
        size = 1
        for s in a.shape:
            size *= s
        n_rows = -(-size // (8 * width)) * 8
        out.append(pack[row:row + n_rows].reshape(-1)[:size].reshape(a.shape))
        row += n_rows
    return out


def _halves(a2d):
    r, c = a2d.shape
    return a2d.reshape(2, r // 2, c)


def local_step(x, meta, norm_w, w_main, w_ba, conv_w, a_log, dt_bias, pool_mix, pool_scale, dn_norm_w,
               w_pool_out, w_dn_out, w_o, final_w, target):
    seq, d = x.shape
    pw = pool_scale.shape[1]
    dw = w_dn_out.shape[0]
    n_heads = dw // HEAD_DIM
    tp = FRONT_PAD + N_META + seq
    hp = jnp.concatenate([jnp.zeros((FRONT_PAD, d), f32), meta, x], axis=0)
    tm_big = tp // 2 if tp % 32 == 0 else tp
    tm_norm = max(t for t in range(16, min(tp, 352) + 1, 16) if tp % t == 0)
    tile = min(512, d)
    off_q = 2 * pw
    off_zd = off_q + 3 * dw
    off_gp = off_zd + dw
    off_gd = off_gp + d
    a_log128 = jnp.pad(a_log, ((0, 0), (0, LANES - n_heads)))
    dt128 = jnp.pad(dt_bias, ((0, 0), (0, LANES - n_heads)))

    xn = norm_fwd(hp, norm_w, tm_norm)
    proj = matmul("proj", xn, w_main, tb=True, tm=tp, tn=tile)
    ba = matmul("proj_ba", xn, w_ba, tb=True, tm=tp, tn=LANES)
    y_pool = pool_fwd(proj, pool_mix, pool_scale, pw)
    q, k, v, beta_b, g_b = dn_pre_fwd(proj, ba, conv_w, a_log128, dt128, n_heads, off_q // HEAD_DIM)
    assert off_zd % dw == 0
    *inter, tmats = dn_intra_fwd(q, k, v, beta_b, g_b, n_heads)
    y_dn, states = dn_seq_fwd(inter, proj, dn_norm_w, n_heads, off_zd // dw)
    a_mat = matmul("pool_out", y_pool, w_pool_out, tm=tp, tn=tile)

    def merge(acc, a_t, gp_t, gd_t):
        return acc, sigmoid(gp_t) * a_t + sigmoid(gd_t) * acc

    b_mat, merged = matmul("dn_out_merge", y_dn, w_dn_out, tm=tm_big, tn=tile, extras=[(a_mat, 0), (proj, off_gp), (proj, off_gd)],
                           epi=merge, out_dtypes=(f32, bf16))
    out = matmul("out_proj", merged, w_o, tm=tm_big, tn=tile, extras=[(hp, 0)], epi=lambda acc, h_t: (acc + h_t,))
    loss, dout, dout_b, dfinal_w = loss_stage(out, final_w, target)

    def unmerge(dm, a_t, b_t, gp_t, gd_t):
        sp, sd = sigmoid(gp_t), sigmoid(gd_t)
        return dm * sp, dm * sd, dm * a_t * sp * (1.0 - sp), dm * b_t * sd * (1.0 - sd)

    d_a, d_b, d_gp, d_gd = matmul("d_merged", dout_b, w_o, tb=True, tm=tm_big, tn=tile,
                                  extras=[(a_mat, 0), (b_mat, 0), (proj, off_gp), (proj, off_gd)], epi=unmerge,
                                  out_dtypes=(bf16,) * 4)
    g_w_o = matmul("g_w_o", merged, dout_b, ta=True, tm=tile, tn=tile, out_dtypes=(bf16,))
    d_y_pool = matmul("d_y_pool", d_a, w_pool_out, tb=True, tm=tp, tn=tile)
    g_w_pool_out = matmul("g_w_pool_out", y_pool, d_a, ta=True, tm=tile, tn=tile, out_dtypes=(bf16,))
    d_y_dn = matmul("d_y_dn", d_b, w_dn_out, tb=True, tm=tp, tn=tile)
    g_w_dn_out = matmul("g_w_dn_out", y_dn, d_b, ta=True, tm=tile, tn=tile, out_dtypes=(bf16,))
    d_u, d_zp, g_pool_mix, g_pool_scale = pool_bwd(proj, pool_mix, pool_scale, d_y_pool, pw)
    *d_inter, d_zd, g_dn_norm_w = dn_seq_bwd(inter, proj, dn_norm_w, states, d_y_dn, n_heads, off_zd // dw)
    d_q, d_k, d_v, d_beta, d_g = dn_intra_bwd(q, k, v, beta_b, g_b, tmats, d_inter, n_heads)
    d_qr, d_kr, d_vr, d_ba, g_cq, g_ck, g_cv, g_a_log, g_dt = dn_pre_bwd(
        proj, ba, conv_w, a_log128, dt128, (d_q, d_k, d_v, d_beta, d_g), n_heads, off_q // HEAD_DIM)
    d_proj = jnp.concatenate([d_u, d_zp, d_qr, d_kr, d_vr, d_zd, d_gp, d_gd], axis=1)
    d_ba_b = cast_bf16("cast_d_ba", d_ba)
    dxn_ba = matmul("dxn_ba", d_ba_b, w_ba, tm=tp, tn=tile)
    n_cols = d_proj.shape[1]
    tk_dxn = max(t for t in range(LANES, min(2048, n_cols) + 1, LANES) if n_cols % t == 0)
    dxn = matmul("dxn", d_proj, w_main, tm=tm_big, tn=tile, tk=tk_dxn,
                 extras=[(dxn_ba, 0)], epi=lambda acc, e: (acc + e,))
    g_w_main = matmul("g_w_main", d_proj, xn, ta=True, tm=tile, tn=tile, out_dtypes=(bf16,))
    g_w_ba = matmul("g_w_ba", d_ba_b, xn, ta=True, tm=LANES, tn=tile, out_dtypes=(bf16,))
    dh, g_norm_w = norm_bwd(hp, norm_w, dxn, dout, tm_norm)
    g_conv = jnp.concatenate([g_cq, g_ck, g_cv], axis=1)
    return (loss, dh, g_norm_w, g_w_main, g_w_ba, g_conv, g_a_log[:, :n_heads], g_dt[:, :n_heads], g_pool_mix, g_pool_scale,
            g_dn_norm_w, g_w_pool_out, g_w_dn_out, g_w_o, dfinal_w)


def kernel(x, meta_tokens, norm_w, w_in, conv_w, A_log, dt_bias, pool_mix, pool_scale, dn_norm_w, w_pool_out, w_dn_out, w_o, final_norm_w, loss_target, m_meta_tokens, m_norm_w, m_w_in, m_conv_w, m_A_log, m_dt_bias, m_pool_mix, m_pool_scale, m_dn_norm_w, m_w_pool_out, m_w_dn_out, m_w_o, m_final_norm_w, v_meta_tokens, v_norm_w, v_w_in, v_conv_w, v_A_log, v_dt_bias, v_pool_mix, v_pool_scale, v_dn_norm_w, v_w_pool_out, v_w_dn_out, v_w_o, v_final_norm_w):
    d = x.shape[-1]
    pw = pool_scale.shape[-1]
    dw = w_dn_out.shape[1] * 4
    n_heads = dw // HEAD_DIM
    gdim = pw // POOL_GROUPS
    chip = 2 * lax.axis_index("x") + lax.axis_index("y")
    core = lax.axis_index("c")

    half = core.astype(jnp.int32).reshape(1)
    me = 4 * lax.axis_index("x") + 2 * lax.axis_index("y") + core

    w_in_t = w_in[0].T
    mix_s = pool_mix[0].reshape(POOL_GROUPS * (gdim // 4), gdim)
    small_s = _pack([meta_tokens, conv_w[0]], d)
    small_rows = small_s.shape[0]
    small_s = jnp.pad(small_s, ((0, (-small_rows) % 16), (0, 0)))
    sw = w_in_t.shape[0]
    n_main, n_ba = 2 * pw + 4 * dw, 2 * n_heads

    shards = [cast_bf16("cast_w_in", w_in_t), _halves(cast_bf16("cast_w_po", w_pool_out[0])),
              _halves(cast_bf16("cast_w_do", w_dn_out[0])), _halves(cast_bf16("cast_w_o", w_o[0])),
              _halves(cast_bf16("cast_mix", mix_s)), _halves(small_s)]
    kinds = ["cols", "lead", "lead", "lead", "lead", "lead"]
    fill = lambda g, own: lax.dynamic_update_slice(g, own[None], (chip,) + (0,) * own.ndim)
    g_in, g_po, g_do, g_o, g_mix, g_small = [fill(g, own) for g, own in zip(gather_weights(shards, kinds), shards)]

    def shard_rows(lo, hi):
        cut = [(max(lo, j * sw), min(hi, (j + 1) * sw), j) for j in range(4)]
        return [g_in[j, a - j * sw:b - j * sw] for a, b, j in cut if a < b]

    w_main = jnp.concatenate(shard_rows(0, n_main) + shard_rows(n_main + n_ba, 4 * sw), axis=0)
    w_ba = jnp.pad(jnp.concatenate(shard_rows(n_main, n_main + n_ba), axis=0), ((0, LANES - n_ba), (0, 0)))
    cat_cols = lambda g: jnp.concatenate([g[j].reshape(-1, g.shape[-1]) for j in range(4)], axis=1)
    w_po_full = cat_cols(g_po)
    w_do_full = g_do.reshape(-1, g_do.shape[-1])
    w_o_full = g_o.reshape(-1, g_o.shape[-1])
    mix_full = g_mix.reshape(4, POOL_GROUPS, gdim // 4, gdim).transpose(1, 0, 2, 3).reshape(POOL_GROUPS, gdim, gdim)
    smalls = [_unpack(g_small[j].reshape(-1, d)[:small_rows], [meta_tokens, conv_w[0]], d) for j in range(4)]
    meta_full = jnp.concatenate([s[0] for s in smalls], axis=1)
    conv_full = jnp.concatenate([s[1] for s in smalls], axis=1)

    (loss, dh, g_norm_w, g_w_main, g_w_ba, g_conv, g_a_log, g_dt, g_mix_full, g_pool_scale, g_dn_norm_w, g_w_po, g_w_do, g_w_o_full,
     g_final_w) = local_step(x[0], meta_full, norm_w, w_main, w_ba, conv_full, A_log, dt_bias, mix_full, pool_scale, dn_norm_w,
                             w_po_full, w_do_full, w_o_full, final_norm_w.reshape(1, d), loss_target[0])
    grad_x = dh[FRONT_PAD + N_META:][None]
    g_meta = dh[FRONT_PAD:FRONT_PAD + N_META]

    def grad_rows(lo, hi):
        segs = [(0, n_main, g_w_main, 0), (n_main, n_main + n_ba, g_w_ba, 0), (n_main + n_ba, 4 * sw, g_w_main, n_main)]
        cut = [(max(lo, s0), min(hi, s1), s0, arr, off) for s0, s1, arr, off in segs]
        return [arr[a - s0 + off:b - s0 + off] for a, b, s0, arr, off in cut if a < b]

    in_parts = jnp.stack([jnp.concatenate(grad_rows(j * sw, (j + 1) * sw), axis=0) for j in range(4)])
    col_parts = lambda g: g.reshape(2, g.shape[0] // 2, 4, g.shape[1] // 4).transpose(0, 2, 1, 3)
    row_parts = lambda g: g.reshape(4, 2, g.shape[0] // 8, g.shape[1]).transpose(1, 0, 2, 3)
    mix_rows = POOL_GROUPS * (gdim // 4)
    mix_parts = (g_mix_full.astype(bf16).reshape(POOL_GROUPS, 4, gdim // 4, gdim).transpose(1, 0, 2, 3)
                 .reshape(4, 2, mix_rows // 2, gdim).transpose(1, 0, 2, 3))
    parts = [in_parts, col_parts(g_w_po), row_parts(g_w_do), row_parts(g_w_o_full), mix_parts]
    kinds = ["cols", "lead", "lead", "lead", "lead"]
    names = ["w_in", "w_po", "w_do", "w_o", "mix"]
    from_sibling = send_grad_halves(parts, kinds)
    chip_sums = [add_halves("add_" + nm, p, r, half, k) for nm, p, r, k in zip(names, parts, from_sibling, kinds)]
    own_slab = lambda got, sent: lax.dynamic_update_slice(
        got, lax.dynamic_slice(sent, (chip, 0, 0), (1,) + sent.shape[1:]), (chip, 0, 0))
    from_chips = [own_slab(got, sent) for got, sent in zip(scatter_to_chips(chip_sums), chip_sums)]
    mine = [sum_leading("sum_" + nm, fc) for nm, fc in zip(names, from_chips)]
    theirs = swap_with_sibling("swap_grad_halves", mine)

    small_like = [loss, g_norm_w, g_a_log, g_dt, g_pool_scale, g_dn_norm_w, g_final_w, g_conv, g_meta]
    pack = _pack(small_like, d)
    packs = lax.dynamic_update_slice(gather_from_all(pack), pack[None], (me, 0, 0))
    total = sum_leading("sum_small", packs)
    (loss_t, g_norm_w, g_a_log, g_dt, g_pool_scale, g_dn_norm_w, g_final_w, g_conv, g_meta) = _unpack(total, small_like, d)
    loss_out = loss_t[0, 0]
    g_conv_s = lax.dynamic_slice_in_dim(g_conv, chip * (g_conv.shape[1] // 4), g_conv.shape[1] // 4, axis=1)
    g_meta_s = lax.dynamic_slice_in_dim(g_meta, chip * (d // 4), d // 4, axis=1)

    weights = [meta_tokens, norm_w, w_in, conv_w, A_log, dt_bias, pool_mix, pool_scale, dn_norm_w, w_pool_out, w_dn_out, w_o, final_norm_w]
    ms = [m_meta_tokens, m_norm_w, m_w_in, m_conv_w, m_A_log, m_dt_bias, m_pool_mix, m_pool_scale, m_dn_norm_w, m_w_pool_out, m_w_dn_out, m_w_o, m_final_norm_w]
    vs = [v_meta_tokens, v_norm_w, v_w_in, v_conv_w, v_A_log, v_dt_bias, v_pool_mix, v_pool_scale, v_dn_norm_w, v_w_pool_out, v_w_dn_out, v_w_o, v_final_norm_w]
    grads = [g_meta_s, g_norm_w, None, g_conv_s[None], g_a_log, g_dt, None, g_pool_scale, g_dn_norm_w, None, None, None, g_final_w.reshape(d)]
    deltas, new_ms, new_vs = [None] * 13, [None] * 13, [None] * 13
    big = [2, 9, 10, 11, 6]
    for i, nm, g_mine, g_theirs in zip(big, names, mine, theirs):
        if nm == "w_in":
            to2d, back, axis = (lambda t: t[0].T), (lambda t: t.T[None]), 1
        else:
            to2d, back, axis = (lambda t: t.reshape(-1, t.shape[-1])), (lambda t, i=i: t.reshape(weights[i].shape)), 0
        res = adamw_joined("adamw_" + nm, to2d(weights[i]), g_mine, g_theirs, to2d(ms[i]), to2d(vs[i]), half, axis)
        grads[i], deltas[i], new_ms[i], new_vs[i] = [back(t) for t in res]
    small_idx = [i for i in range(13) if i not in big]
    packs = [_pack([arrs[i] for i in small_idx], d) for arrs in (weights, grads, ms, vs)]
    outs = adamw("adamw_small", *packs)
    like = [weights[i] for i in small_idx]
    for res, dest in zip(outs, (deltas, new_ms, new_vs)):
        for i, val in zip(small_idx, _unpack(res, like, d)):
            dest[i] = val
    return (loss_out, grad_x, *grads, *deltas, *new_ms, *new_vs)
```

```python
import functools

import jax
import jax.numpy as jnp
from jax import lax
from jax.experimental import pallas as pl
from jax.experimental.pallas import tpu as pltpu

f32 = jnp.float32
bf16 = jnp.bfloat16
HIGHEST = lax.Precision.HIGHEST
MESH = pl.DeviceIdType.MESH

N_META = 16
CHUNK = 64
FRONT_PAD = (-N_META) % CHUNK
HEAD_DIM = 128
POOL_GROUPS = 4
POOL_WINDOWS = (2, 4, 8, 16)
CONV_WIDTH = 4
NORM_EPS = 1e-6
ADAM_LR, ADAM_B1, ADAM_B2, ADAM_EPS, ADAM_WD, ADAM_STEP = 0.001, 0.9, 0.999, 1e-08, 0.01, 10
LANES = 128
VMEM_LIMIT = 56 * 2**20


def _pc(body, **kw):
    return pl.pallas_call(body, **kw)


def _params(*sem, **kw):
    return pltpu.CompilerParams(dimension_semantics=sem or None, vmem_limit_bytes=VMEM_LIMIT, **kw)


def _dg(a, b, dims, prec=None):
    return lax.dot_general(a, b, (dims, ((), ())), precision=prec, preferred_element_type=f32)


@jax.custom_vjp
def mm_nn(a, b):
    return _dg(a.astype(bf16), b.astype(bf16), ((1,), (0,)))


@jax.custom_vjp
def mm_nt(a, b):
    return _dg(a.astype(bf16), b.astype(bf16), ((1,), (1,)))


@jax.custom_vjp
def mm_tn(a, b):
    return _dg(a.astype(bf16).T, b.astype(bf16), ((1,), (0,)))


mm_nn.defvjp(lambda a, b: (mm_nn(a, b), (a, b)), lambda r, dy: (mm_nt(dy, r[1]), mm_tn(r[0], dy)))
mm_nt.defvjp(lambda a, b: (mm_nt(a, b), (a, b)), lambda r, dy: (mm_nn(dy, r[1]), mm_tn(dy, r[0])))
mm_tn.defvjp(lambda a, b: (mm_tn(a, b), (a, b)), lambda r, dy: (mm_nt(r[1], dy), mm_nn(r[0], dy)))


def _split3(x):
    hi = x.astype(bf16)
    r1 = x - hi.astype(f32)
    mid = r1.astype(bf16)
    lo = (r1 - mid.astype(f32)).astype(bf16)
    return hi, mid, lo


@jax.custom_vjp
def mm_sel(sel, x):
    s = sel.astype(bf16)
    d = ((1,), (0,))
    hi, mid, lo = _split3(x)
    return _dg(s, hi, d) + _dg(s, mid, d) + _dg(s, lo, d)


def _mm_sel_bwd(sel, dy):
    s = sel.astype(bf16)
    d = ((0,), (0,))
    hi, mid, lo = _split3(dy)
    return jnp.zeros_like(sel), _dg(s, hi, d) + _dg(s, mid, d) + _dg(s, lo, d)


mm_sel.defvjp(lambda sel, x: (mm_sel(sel, x), sel), _mm_sel_bwd)


@jax.custom_vjp
def mm_pick(x, sel):
    s = sel.astype(bf16)
    d = ((1,), (0,))
    hi, mid, lo = _split3(x)
    return _dg(hi, s, d) + _dg(mid, s, d) + _dg(lo, s, d)


def _mm_pick_bwd(sel, dy):
    s = sel.astype(bf16)
    d = ((1,), (1,))
    hi, mid, lo = _split3(dy)
    return _dg(hi, s, d) + _dg(mid, s, d) + _dg(lo, s, d), jnp.zeros_like(sel)


mm_pick.defvjp(lambda x, sel: (mm_pick(x, sel), sel), _mm_pick_bwd)


def mm3(a, b):
    a_hi = a.astype(bf16)
    a_lo = (a - a_hi.astype(f32)).astype(bf16)
    b_hi = b.astype(bf16)
    b_lo = (b - b_hi.astype(f32)).astype(bf16)
    d = ((1,), (0,))
    return _dg(a_hi, b_hi, d) + _dg(a_hi, b_lo, d) + _dg(a_lo, b_hi, d)


def tri_inv(ls):
    n = ls[0].shape[0]
    eye = (lax.broadcasted_iota(jnp.int32, (n, n), 0) == lax.broadcasted_iota(jnp.int32, (n, n), 1)).astype(f32)
    ms = [-l for l in ls]
    ts = [eye + m for m in ms]
    k = 1
    while 2 * k < CHUNK:
        ms = [mm3(m, m) for m in ms]
        ts = [t + mm3(t, m) for t, m in zip(ts, ms)]
        k *= 2
    return ts


@functools.partial(jax.custom_vjp, nondiff_argnums=(1,))
def shift_rows(x, j):
    n = x.shape[0]
    rows = lax.broadcasted_iota(jnp.int32, x.shape, 0)
    if j >= 0:
        return jnp.where(rows >= j, pltpu.roll(x, j, 0), 0.0)
    return jnp.where(rows < n + j, pltpu.roll(x, n + j, 0), 0.0)


shift_rows.defvjp(lambda x, j: (shift_rows(x, j), None), lambda j, _, dy: (shift_rows(dy, -j),))


def sigmoid(x):
    return 1.0 / (1.0 + jnp.exp(-x))


def silu(x):
    return x * sigmoid(x)


def softplus(x):
    return jnp.maximum(x, 0.0) + jnp.log(1.0 + jnp.exp(-jnp.abs(x)))


def rmsnorm(x, w):
    return x * lax.rsqrt(jnp.mean(x * x, axis=-1, keepdims=True) + NORM_EPS) * w


def l2norm(x):
    return x * lax.rsqrt(jnp.sum(x * x, axis=-1, keepdims=True) + NORM_EPS)


def pool_fn(u, zp, mix, scale, group):
    rows = lax.broadcasted_iota(jnp.int32, u.shape, 0)
    sums = []
    s, w = u, 1
    while w < POOL_WINDOWS[-1]:
        s = s + shift_rows(s, w)
        w *= 2
        sums.append(s)
    total = sums[-1]
    for gi in range(POOL_GROUPS - 2, -1, -1):
        total = jnp.where(group == gi, sums[gi], total)
    window = jnp.left_shift(2, group)
    cnt = jnp.clip(rows - (FRONT_PAD - 1), 1, window).astype(f32)
    pooled = total / cnt - u
    return mm_nn(pooled, mix) * scale * silu(zp)


def conv_silu(x, w):
    k = CONV_WIDTH
    y = x * w[k - 1:k, :]
    for kk in range(k - 1):
        y = y + shift_rows(x, k - 1 - kk) * w[kk:kk + 1, :]
    return silu(y)


def _lane_pick(row, idx):
    lanes = lax.broadcasted_iota(jnp.int32, row.shape, 1)
    return jnp.sum(jnp.where(lanes == idx, row, 0.0), axis=1, keepdims=True)


def dn_pre_fn(qr, kr, vr, ba, cwq, cwk, cwv, a_log, dt_bias, head, n_heads):
    q = l2norm(conv_silu(qr, cwq)) * (HEAD_DIM ** -0.5)
    k = l2norm(conv_silu(kr, cwk))
    v = conv_silu(vr, cwv)
    r = lax.broadcasted_iota(jnp.int32, (LANES, LANES), 0)
    b_b = mm_pick(ba, (r == head).astype(f32))
    a_b = mm_pick(ba, (r == head + n_heads).astype(f32))
    real = lax.broadcasted_iota(jnp.int32, qr.shape, 0) >= FRONT_PAD
    beta_b = jnp.where(real, sigmoid(b_b), 0.0)
    g_b = jnp.where(real, -jnp.exp(_lane_pick(a_log, head)) * softplus(a_b + _lane_pick(dt_bias, head)), 0.0)
    return q, k, v, beta_b, g_b


def _chunk_masks(rows):
    r = lax.broadcasted_iota(jnp.int32, (rows, rows), 0)
    c = lax.broadcasted_iota(jnp.int32, (rows, rows), 1)
    same = (r // CHUNK) == (c // CHUNK)
    return same, jnp.logical_and(same, r >= c), jnp.logical_and(same, r > c)


def _lane0(rows):
    return (lax.broadcasted_iota(jnp.int32, (rows, LANES), 1) == 0).astype(bf16)


@jax.custom_vjp
def lane0_as_row(x):
    sel = _lane0(x.shape[0])
    d = ((1,), (1,))
    hi, mid, lo = _split3(x)
    return _dg(sel, hi, d) + _dg(sel, mid, d) + _dg(sel, lo, d)


def _lane0_as_row_bwd(rows, dy):
    sel = _lane0(rows)
    d = ((0,), (0,))
    hi, mid, lo = _split3(dy)
    return (_dg(hi, sel, d) + _dg(mid, sel, d) + _dg(lo, sel, d),)


lane0_as_row.defvjp(lambda x: (lane0_as_row(x), x.shape[0]), _lane0_as_row_bwd)


def gate_fn(g_b):
    rows = g_b.shape[0]
    same, causal, _ = _chunk_masks(rows)
    gcum_b = mm_sel(causal.astype(f32), g_b)
    glast_b = mm_sel(same.astype(f32), g_b)
    g_rows = jnp.broadcast_to(gcum_b[:, :1], (rows, rows))
    decay = jnp.where(causal, jnp.exp(jnp.where(causal, g_rows - lane0_as_row(gcum_b), 0.0)), 0.0)
    return decay, jnp.exp(gcum_b), jnp.exp(glast_b - gcum_b), jnp.exp(glast_b)


def _fold_matrix(rows):
    r = lax.broadcasted_iota(jnp.int32, (rows, LANES), 0)
    c = lax.broadcasted_iota(jnp.int32, (rows, LANES), 1)
    return (r % CHUNK == c).astype(bf16)


@jax.custom_vjp
def fold_chunks(x):
    return _dg(x.astype(bf16), _fold_matrix(x.shape[0]), ((1,), (0,)))


def _fold_chunks_bwd(rows, dy):
    fold = _fold_matrix(rows)
    d = ((1,), (1,))
    hi, mid, lo = _split3(dy)
    return (_dg(hi, fold, d) + _dg(mid, fold, d) + _dg(lo, fold, d),)


fold_chunks.defvjp(lambda x: (fold_chunks(x), x.shape[0]), _fold_chunks_bwd)


def lmat_fn(k, beta_b, decay):
    _, _, strict = _chunk_masks(k.shape[0])
    return jnp.where(strict, mm_nt(k * beta_b, k) * decay, 0.0)


def intra_fn(tmat, q, k, v, beta_b, decay, eg, kfac):
    _, causal, _ = _chunk_masks(q.shape[0])
    k_beta = k * beta_b
    u_c = mm_nn(tmat, v * beta_b)
    w_c = mm_nn(tmat, k_beta * eg)
    qk = jnp.where(causal, mm_nt(q, k) * decay, 0.0)
    return u_c, w_c, q * eg, k * kfac, fold_chunks(qk)


def gated_norm(o, norm_w, zd):
    return rmsnorm(o, norm_w) * silu(zd)


def loss_fn(o, w, tgt):
    err = rmsnorm(o, w) - tgt
    return 0.5 * jnp.sum(jnp.mean(err * err, axis=-1))


def matmul(name, a, b, *, ta=False, tb=False, tm, tn, tk=None, extras=(), epi=None, out_dtypes=(f32,)):
    m, k = (a.shape[1], a.shape[0]) if ta else a.shape
    n = b.shape[0] if tb else b.shape[1]
    tm, tn, tk = min(tm, m), min(tn, n), min(tk or k, k)
    assert m % tm == 0 and n % tn == 0 and k % tk == 0, (name, m, n, k, tm, tn, tk)
    nm, nn, nk = m // tm, n // tn, k // tk
    a_spec = pl.BlockSpec((tk, tm), lambda i, j, kk: (kk, i)) if ta else pl.BlockSpec((tm, tk), lambda i, j, kk: (i, kk))
    b_spec = pl.BlockSpec((tn, tk), lambda i, j, kk: (j, kk)) if tb else pl.BlockSpec((tk, tn), lambda i, j, kk: (kk, j))
    ex_specs = []
    for _, off in extras:
        assert off % tn == 0, (name, off, tn)
        ex_specs.append(pl.BlockSpec((tm, tn), functools.partial(lambda i, j, kk, o: (i, o + j), o=off // tn)))
    n_ex, n_out = len(extras), len(out_dtypes)
    dims = ((0 if ta else 1,), (1 if tb else 0,))

    def body(a_ref, b_ref, *rest):
        ex_refs, out_refs = rest[:n_ex], rest[n_ex:n_ex + n_out]

        def finish(acc):
            res = epi(acc, *[r[...] for r in ex_refs]) if epi is not None else (acc,)
            for o_ref, val in zip(out_refs, res):
                o_ref[...] = val.astype(o_ref.dtype)

        p = _dg(a_ref[...], b_ref[...], dims)
        if nk == 1:
            finish(p)
        else:
            acc_ref = rest[-1]
            kk = pl.program_id(2)

            @pl.when(kk == 0)
            def _():
                acc_ref[...] = p

            @pl.when(kk > 0)
            def _():
                acc_ref[...] += p

            @pl.when(kk == nk - 1)
            def _():
                finish(acc_ref[...])

    outs = _pc(
        body, name=name, grid=(nm, nn, nk),
        in_specs=[a_spec, b_spec] + ex_specs,
        out_specs=[pl.BlockSpec((tm, tn), lambda i, j, kk: (i, j))] * n_out,
        out_shape=[jax.ShapeDtypeStruct((m, n), dt) for dt in out_dtypes],
        scratch_shapes=[pltpu.VMEM((tm, tn), f32)] if nk > 1 else [],
        compiler_params=_params("parallel", "parallel", "arbitrary"),
    )(a, b, *[e for e, _ in extras])
    return outs[0] if n_out == 1 else outs


def _row_tile(rows, cols, n_arrays, itemsize=4, budget=24 * 2**20):
    best = None
    for t in range(16, rows + 1, 16):
        if rows % t == 0 and 2 * n_arrays * t * cols * itemsize <= budget:
            best = t
    return best or rows


def _tile(rows, cols, n_arrays, budget=24 * 2**20):
    if rows % 16 == 0 or cols % LANES != 0:
        return _row_tile(rows, cols, n_arrays, budget=budget), cols
    fits = [t for t in range(LANES, cols + 1, LANES) if cols % t == 0 and 2 * n_arrays * rows * t * 4 <= budget]
    return rows, (max(fits) if fits else LANES)


def cast_bf16(name, x):
    rows, cols = x.shape
    tr, tc = _tile(rows, cols, 2)

    def body(x_ref, o_ref):
        o_ref[...] = x_ref[...].astype(bf16)

    blk = pl.BlockSpec((tr, tc), lambda i, j: (i, j))
    return _pc(body, name=name, grid=(rows // tr, cols // tc), in_specs=[blk], out_specs=blk,
               out_shape=jax.ShapeDtypeStruct(x.shape, bf16), compiler_params=_params("parallel", "parallel"))(x)


def norm_fwd(hp, norm_w, tm):
    tp, d = hp.shape

    def body(h_ref, w_ref, o_ref):
        o_ref[...] = rmsnorm(h_ref[...], w_ref[...]).astype(bf16)

    return _pc(body, name="norm_fwd", grid=(tp // tm,),
               in_specs=[pl.BlockSpec((tm, d), lambda i: (i, 0)), pl.BlockSpec((1, d), lambda i: (0, 0))],
               out_specs=pl.BlockSpec((tm, d), lambda i: (i, 0)), out_shape=jax.ShapeDtypeStruct((tp, d), bf16),
               compiler_params=_params("parallel"))(hp, norm_w)


def norm_bwd(hp, norm_w, dxn, dout, tm):
    tp, d = hp.shape

    def body(h_ref, w_ref, dxn_ref, dout_ref, dh_ref, dw_ref):
        _, vjp = jax.vjp(rmsnorm, h_ref[...], w_ref[...])
        dh, dw = vjp(dxn_ref[...])
        dh_ref[...] = dh + dout_ref[...]

        @pl.when(pl.program_id(0) == 0)
        def _():
            dw_ref[...] = jnp.zeros_like(dw_ref)

        dw_ref[...] += dw

    row = pl.BlockSpec((tm, d), lambda i: (i, 0))
    vec = pl.BlockSpec((1, d), lambda i: (0, 0))
    return _pc(body, name="norm_bwd", grid=(tp // tm,), in_specs=[row, vec, row, row], out_specs=[row, vec],
               out_shape=[jax.ShapeDtypeStruct((tp, d), f32), jax.ShapeDtypeStruct((1, d), f32)],
               compiler_params=_params("arbitrary"))(hp, norm_w, dxn, dout)


def pool_fwd(proj, mix, scale, pw):
    tp = proj.shape[0]
    g = pw // POOL_GROUPS

    def body(u_ref, z_ref, mix_ref, s_ref, y_ref):
        y_ref[...] = pool_fn(u_ref[...], z_ref[...], mix_ref[0], s_ref[...], pl.program_id(0)).astype(bf16)

    return _pc(body, name="pool_fwd", grid=(POOL_GROUPS,),
               in_specs=[pl.BlockSpec((tp, g), lambda i: (0, i)), pl.BlockSpec((tp, g), lambda i: (0, POOL_GROUPS + i)),
                         pl.BlockSpec((1, g, g), lambda i: (i, 0, 0)), pl.BlockSpec((1, g), lambda i: (0, i))],
               out_specs=pl.BlockSpec((tp, g), lambda i: (0, i)), out_shape=jax.ShapeDtypeStruct((tp, pw), bf16),
               compiler_params=_params("parallel"))(proj, proj, mix, scale)


def pool_bwd(proj, mix, scale, dy, pw):
    tp = proj.shape[0]
    g = pw // POOL_GROUPS

    def body(u_ref, z_ref, mix_ref, s_ref, dy_ref, du_ref, dz_ref, dmix_ref, ds_ref):
        grp = pl.program_id(0)
        _, vjp = jax.vjp(lambda u, z, m, s: pool_fn(u, z, m, s, grp), u_ref[...], z_ref[...], mix_ref[0].astype(f32), s_ref[...])
        du, dz, dmix, ds = vjp(dy_ref[...])
        du_ref[...] = du.astype(bf16)
        dz_ref[...] = dz.astype(bf16)
        dmix_ref[0] = dmix
        ds_ref[...] = ds

    col = pl.BlockSpec((tp, g), lambda i: (0, i))
    return _pc(body, name="pool_bwd", grid=(POOL_GROUPS,),
               in_specs=[col, pl.BlockSpec((tp, g), lambda i: (0, POOL_GROUPS + i)),
                         pl.BlockSpec((1, g, g), lambda i: (i, 0, 0)), pl.BlockSpec((1, g), lambda i: (0, i)), col],
               out_specs=[col, col, pl.BlockSpec((1, g, g), lambda i: (i, 0, 0)), pl.BlockSpec((1, g), lambda i: (0, i))],
               out_shape=[jax.ShapeDtypeStruct((tp, pw), bf16), jax.ShapeDtypeStruct((tp, pw), bf16),
                          jax.ShapeDtypeStruct((POOL_GROUPS, g, g), f32), jax.ShapeDtypeStruct((1, pw), f32)],
               compiler_params=_params("parallel"))(proj, proj, mix, scale, dy)


def _dn_pre_specs(tp, n_heads, q_off):
    hb = lambda off: pl.BlockSpec((tp, HEAD_DIM), functools.partial(lambda h, o: (0, o + h), o=off))
    cw = lambda off: pl.BlockSpec((CONV_WIDTH, HEAD_DIM), functools.partial(lambda h, o: (0, o + h), o=off))
    whole = lambda shape: pl.BlockSpec(shape, lambda h: (0, 0))
    return ([hb(q_off), hb(q_off + n_heads), hb(q_off + 2 * n_heads), whole((tp, LANES)),
             cw(0), cw(n_heads), cw(2 * n_heads), whole((1, LANES)), whole((1, LANES))], hb, cw, whole)


def dn_pre_fwd(proj, ba, conv_w, a_log, dt_bias, n_heads, q_off):
    tp = proj.shape[0]
    in_specs, hb, _, _ = _dn_pre_specs(tp, n_heads, q_off)

    def body(q_ref, k_ref, v_ref, ba_ref, cq_ref, ck_ref, cv_ref, al_ref, dt_ref, *out_refs):
        outs = dn_pre_fn(q_ref[...], k_ref[...], v_ref[...], ba_ref[...], cq_ref[...], ck_ref[...], cv_ref[...],
                         al_ref[...], dt_ref[...], pl.program_id(0), n_heads)
        for o_ref, val in zip(out_refs, outs):
            o_ref[...] = val

    return _pc(body, name="dn_pre_fwd", grid=(n_heads,), in_specs=in_specs, out_specs=[hb(0)] * 5,
               out_shape=[jax.ShapeDtypeStruct((tp, n_heads * HEAD_DIM), f32)] * 5,
               compiler_params=_params("parallel"))(proj, proj, proj, ba, conv_w, conv_w, conv_w, a_log, dt_bias)


def dn_pre_bwd(proj, ba, conv_w, a_log, dt_bias, cots, n_heads, q_off):
    tp = proj.shape[0]
    in_specs, hb, cw, whole = _dn_pre_specs(tp, n_heads, q_off)

    def body(q_ref, k_ref, v_ref, ba_ref, cq_ref, ck_ref, cv_ref, al_ref, dt_ref, dq_ref, dk_ref, dv_ref, db_ref, dg_ref,
             dqr_ref, dkr_ref, dvr_ref, dba_ref, dcq_ref, dck_ref, dcv_ref, dal_ref, ddt_ref):
        head = pl.program_id(0)
        fn = lambda *args: dn_pre_fn(*args, head, n_heads)
        _, vjp = jax.vjp(fn, q_ref[...], k_ref[...], v_ref[...], ba_ref[...], cq_ref[...], ck_ref[...], cv_ref[...],
                         al_ref[...], dt_ref[...])
        dqr, dkr, dvr, dba, dcq, dck, dcv, dal, ddt = vjp((dq_ref[...], dk_ref[...], dv_ref[...], db_ref[...], dg_ref[...]))
        dqr_ref[...] = dqr.astype(bf16)
        dkr_ref[...] = dkr.astype(bf16)
        dvr_ref[...] = dvr.astype(bf16)
        dcq_ref[...] = dcq
        dck_ref[...] = dck
        dcv_ref[...] = dcv

        @pl.when(head == 0)
        def _():
            dba_ref[...] = jnp.zeros_like(dba_ref)
            dal_ref[...] = jnp.zeros_like(dal_ref)
            ddt_ref[...] = jnp.zeros_like(ddt_ref)

        dba_ref[...] += dba
        dal_ref[...] += dal
        ddt_ref[...] += ddt

    w = n_heads * HEAD_DIM
    return _pc(body, name="dn_pre_bwd", grid=(n_heads,), in_specs=in_specs + [hb(0)] * 5,
               out_specs=[hb(0)] * 3 + [whole((tp, LANES)), cw(0), cw(0), cw(0), whole((1, LANES)), whole((1, LANES))],
               out_shape=[jax.ShapeDtypeStruct((tp, w), bf16)] * 3 + [jax.ShapeDtypeStruct((tp, LANES), f32)]
               + [jax.ShapeDtypeStruct((CONV_WIDTH, w), f32)] * 3 + [jax.ShapeDtypeStruct((1, LANES), f32)] * 2,
               compiler_params=_params("arbitrary"))(proj, proj, proj, ba, conv_w, conv_w, conv_w, a_log, dt_bias, *cots)


def _super_rows(tp):
    n = tp // CHUNK
    return CHUNK * max(j for j in (4, 3, 2, 1) if n % j == 0)


def _heads_per_step(n_heads):
    return max(j for j in (4, 2, 1) if n_heads % j == 0)


def dn_intra_fwd(q, k, v, beta_b, g_b, n_heads):
    tp = q.shape[0]
    rows = _super_rows(tp)
    ns = tp // rows

    hps = _heads_per_step(n_heads)

    def body(q_ref, k_ref, v_ref, b_ref, g_ref, u_ref, w_ref, qd_ref, kd_ref, qk_ref, gl_ref, t_ref):
        lanes = [slice(i * HEAD_DIM, (i + 1) * HEAD_DIM) for i in range(hps)]
        gates = [gate_fn(g_ref[:, sl]) for sl in lanes]
        tmats = tri_inv([lmat_fn(k_ref[:, sl], b_ref[:, sl], gt[0]) for sl, gt in zip(lanes, gates)])
        for i, (sl, (decay, eg, kfac, gl), tmat) in enumerate(zip(lanes, gates, tmats)):
            u_c, w_c, q_dec, k_dec, qk_c = intra_fn(tmat, q_ref[:, sl], k_ref[:, sl], v_ref[:, sl], b_ref[:, sl], decay, eg, kfac)
            u_ref[:, sl] = u_c
            w_ref[:, sl] = w_c
            qd_ref[:, sl] = q_dec
            kd_ref[:, sl] = k_dec
            qk_ref[:, sl] = qk_c
            gl_ref[:, sl] = gl
            t_ref[i, 0] = tmat

    blk = pl.BlockSpec((rows, hps * HEAD_DIM), lambda h, s: (s, h))
    return _pc(body, name="dn_intra_fwd", grid=(n_heads // hps, ns), in_specs=[blk] * 5,
               out_specs=[blk] * 6 + [pl.BlockSpec((hps, 1, rows, rows), lambda h, s: (h, s, 0, 0))],
               out_shape=[jax.ShapeDtypeStruct(q.shape, f32)] * 6 + [jax.ShapeDtypeStruct((n_heads, ns, rows, rows), f32)],
               compiler_params=_params("parallel", "parallel"))(q, k, v, beta_b, g_b)


def dn_intra_bwd(q, k, v, beta_b, g_b, tmats, cots, n_heads):
    tp = q.shape[0]
    rows = _super_rows(tp)
    ns = tp // rows

    hps = _heads_per_step(n_heads)

    def body(q_ref, k_ref, v_ref, b_ref, g_ref, t_ref, du_ref, dw_ref, dqd_ref, dkd_ref, dqk_ref, dgl_ref,
             dq_ref, dk_ref, dv_ref, db_ref, dg_ref):
        lanes = [slice(i * HEAD_DIM, (i + 1) * HEAD_DIM) for i in range(hps)]
        tmats = [t_ref[i, 0] for i in range(hps)]
        gates = [jax.vjp(gate_fn, g_ref[:, sl]) for sl in lanes]
        intra = [jax.vjp(intra_fn, tmat, q_ref[:, sl], k_ref[:, sl], v_ref[:, sl], b_ref[:, sl], gt[0][0], gt[0][1], gt[0][2])[1](
            (du_ref[:, sl], dw_ref[:, sl], dqd_ref[:, sl], dkd_ref[:, sl], dqk_ref[:, sl]))
            for sl, tmat, gt in zip(lanes, tmats, gates)]
        tts = [tmat.T for tmat in tmats]
        dls = [mm3(tt, res[0]) for tt, res in zip(tts, intra)]
        dls = [-mm3(dl, tt) for dl, tt in zip(dls, tts)]
        for sl, gt, res, dl in zip(lanes, gates, intra, dls):
            _, dq, dk, dv, db, ddecay, deg, dkfac = res
            dk2, db2, ddecay2 = jax.vjp(lmat_fn, k_ref[:, sl], b_ref[:, sl], gt[0][0])[1](dl)
            (dg,) = gt[1]((ddecay + ddecay2, deg, dkfac, dgl_ref[:, sl]))
            dq_ref[:, sl] = dq
            dk_ref[:, sl] = dk + dk2
            dv_ref[:, sl] = dv
            db_ref[:, sl] = db + db2
            dg_ref[:, sl] = dg

    blk = pl.BlockSpec((rows, hps * HEAD_DIM), lambda h, s: (s, h))
    return _pc(body, name="dn_intra_bwd", grid=(n_heads // hps, ns),
               in_specs=[blk] * 5 + [pl.BlockSpec((hps, 1, rows, rows), lambda h, s: (h, s, 0, 0))] + [blk] * 6,
               out_specs=[blk] * 5, out_shape=[jax.ShapeDtypeStruct(q.shape, f32)] * 5,
               compiler_params=_params("parallel", "parallel"))(q, k, v, beta_b, g_b, tmats, *cots)


def dn_seq_fwd(inter, proj, dn_norm_w, n_heads, zd_off):
    tp, width = inter[0].shape
    n_chunks = tp // CHUNK

    def body(u_ref, w_ref, qd_ref, kd_ref, qk_ref, gl_ref, z_ref, nw_ref, y_ref, s_ref, state):
        @pl.when(pl.program_id(0) == 0)
        def _():
            state[...] = jnp.zeros_like(state)

        lanes = [slice(h * HEAD_DIM, (h + 1) * HEAD_DIM) for h in range(n_heads)]
        sts = [state[h] for h in range(n_heads)]
        for h, st in enumerate(sts):
            s_ref[0, h] = st
        v_new = [u_ref[:, sl] - mm_nn(w_ref[:, sl], st) for sl, st in zip(lanes, sts)]
        outs = [mm_nn(qd_ref[:, sl], st) + mm_nn(qk_ref[:, sl][:, :CHUNK], vn) for sl, st, vn in zip(lanes, sts, v_new)]
        for h, (sl, st, vn) in enumerate(zip(lanes, sts, v_new)):
            state[h] = st * gl_ref[0:1, sl] + mm_tn(kd_ref[:, sl], vn)
        for sl, o in zip(lanes, outs):
            y_ref[:, sl] = gated_norm(o, nw_ref[...], z_ref[:, sl]).astype(bf16)

    blk = pl.BlockSpec((CHUNK, width), lambda n: (n, 0))
    return _pc(body, name="dn_seq_fwd", grid=(n_chunks,),
               in_specs=[blk] * 6 + [pl.BlockSpec((CHUNK, width), lambda n: (n, zd_off)), pl.BlockSpec((1, HEAD_DIM), lambda n: (0, 0))],
               out_specs=[blk, pl.BlockSpec((1, n_heads, HEAD_DIM, HEAD_DIM), lambda n: (n, 0, 0, 0))],
               out_shape=[jax.ShapeDtypeStruct((tp, width), bf16), jax.ShapeDtypeStruct((n_chunks, n_heads, HEAD_DIM, HEAD_DIM), f32)],
               scratch_shapes=[pltpu.VMEM((n_heads, HEAD_DIM, HEAD_DIM), f32)],
               compiler_params=_params("arbitrary"))(*inter, proj, dn_norm_w)


def dn_seq_bwd(inter, proj, dn_norm_w, states, dy, n_heads, zd_off):
    tp, width = inter[0].shape
    n_chunks = tp // CHUNK
    last = n_chunks - 1

    def body(u_ref, w_ref, qd_ref, kd_ref, qk_ref, gl_ref, z_ref, nw_ref, s_ref, dy_ref,
             du_ref, dw_ref, dqd_ref, dkd_ref, dqk_ref, dgl_ref, dz_ref, dnw_ref, dstate):
        @pl.when(pl.program_id(0) == 0)
        def _():
            dstate[...] = jnp.zeros_like(dstate)
            dnw_ref[...] = jnp.zeros_like(dnw_ref)

        lanes = [slice(h * HEAD_DIM, (h + 1) * HEAD_DIM) for h in range(n_heads)]
        sts = [s_ref[0, h] for h in range(n_heads)]
        dsts = [dstate[h] for h in range(n_heads)]
        v_new = [u_ref[:, sl] - mm_nn(w_ref[:, sl], st) for sl, st in zip(lanes, sts)]
        outs = [mm_nn(qd_ref[:, sl], st) + mm_nn(qk_ref[:, sl][:, :CHUNK], vn) for sl, st, vn in zip(lanes, sts, v_new)]
        dnw = jnp.zeros((1, HEAD_DIM), f32)
        d_outs = []
        for sl, o in zip(lanes, outs):
            do, dn, dz = jax.vjp(gated_norm, o, nw_ref[...], z_ref[:, sl])[1](dy_ref[:, sl])
            dz_ref[:, sl] = dz.astype(bf16)
            dnw = dnw + dn
            d_outs.append(do)
        dnw_ref[...] += dnw
        d_vn = [mm_tn(qk_ref[:, sl][:, :CHUNK], do) + mm_nn(kd_ref[:, sl], ds) for sl, do, ds in zip(lanes, d_outs, dsts)]
        zeros = jnp.zeros((HEAD_DIM - CHUNK, HEAD_DIM), f32)
        rows = lax.broadcasted_iota(jnp.int32, (CHUNK, HEAD_DIM), 0)
        for h, (sl, st, vn, do, ds, dvn) in enumerate(zip(lanes, sts, v_new, d_outs, dsts, d_vn)):
            du_ref[:, sl] = dvn
            dw_ref[:, sl] = -mm_nt(dvn, st)
            dqd_ref[:, sl] = mm_nt(do, st)
            dkd_ref[:, sl] = mm_nt(vn, ds)
            dqk_ref[:, sl] = mm_nt(do, jnp.concatenate([vn, zeros], axis=0))
            dgl_ref[:, sl] = jnp.where(rows == 0, jnp.sum(st * ds, axis=0, keepdims=True), 0.0)
            dstate[h] = ds * gl_ref[0:1, sl] + mm_tn(qd_ref[:, sl], do) - mm_tn(w_ref[:, sl], dvn)

    blk = pl.BlockSpec((CHUNK, width), lambda n: (last - n, 0))
    return _pc(body, name="dn_seq_bwd", grid=(n_chunks,),
               in_specs=[blk] * 6 + [pl.BlockSpec((CHUNK, width), lambda n: (last - n, zd_off)), pl.BlockSpec((1, HEAD_DIM), lambda n: (0, 0)),
                         pl.BlockSpec((1, n_heads, HEAD_DIM, HEAD_DIM), lambda n: (last - n, 0, 0, 0)), blk],
               out_specs=[blk] * 7 + [pl.BlockSpec((1, HEAD_DIM), lambda n: (0, 0))],
               out_shape=[jax.ShapeDtypeStruct((tp, width), f32)] * 6 + [jax.ShapeDtypeStruct((tp, width), bf16),
                                                                          jax.ShapeDtypeStruct((1, HEAD_DIM), f32)],
               scratch_shapes=[pltpu.VMEM((n_heads, HEAD_DIM, HEAD_DIM), f32)],
               compiler_params=_params("arbitrary"))(*inter, proj, dn_norm_w, states, dy)


def loss_stage(out, final_w, target):
    tp, d = out.shape
    n_tiles = tp // CHUNK

    def body(o_ref, w_ref, t_ref, loss_ref, do_ref, dob_ref, dw_ref):
        i = pl.program_id(0)

        @pl.when(i == 0)
        def _():
            loss_ref[...] = jnp.zeros_like(loss_ref)
            dw_ref[...] = jnp.zeros_like(dw_ref)

        scored = (i > 0).astype(f32)
        val, (do, dw) = jax.value_and_grad(lambda o, w: scored * loss_fn(o, w, t_ref[...]), argnums=(0, 1))(o_ref[...], w_ref[...])
        loss_ref[...] += jnp.full(loss_ref.shape, val, f32)
        do_ref[...] = do
        dob_ref[...] = do.astype(bf16)
        dw_ref[...] += dw

    row = pl.BlockSpec((CHUNK, d), lambda i: (i, 0))
    vec = pl.BlockSpec((1, d), lambda i: (0, 0))
    return _pc(body, name="loss_stage", grid=(n_tiles,),
               in_specs=[row, vec, pl.BlockSpec((CHUNK, d), lambda i: (jnp.maximum(i - 1, 0), 0))],
               out_specs=[pl.BlockSpec((1, LANES), lambda i: (0, 0)), row, row, vec],
               out_shape=[jax.ShapeDtypeStruct((1, LANES), f32), jax.ShapeDtypeStruct((tp, d), f32),
                          jax.ShapeDtypeStruct((tp, d), bf16), jax.ShapeDtypeStruct((1, d), f32)],
               compiler_params=_params("arbitrary"))(out, final_w, target)


def _adam_update(w, g, m, v):
    nm = ADAM_B1 * m + (1.0 - ADAM_B1) * g
    nv = ADAM_B2 * v + (1.0 - ADAM_B2) * jnp.square(g)
    m_hat = nm / (1.0 - ADAM_B1 ** ADAM_STEP)
    v_hat = nv / (1.0 - ADAM_B2 ** ADAM_STEP)
    return -ADAM_LR * (m_hat / (jnp.sqrt(v_hat) + ADAM_EPS) + ADAM_WD * w), nm, nv


def adamw(name, w, g, m, v):
    rows, cols = w.shape
    t = _row_tile(rows, cols, 7)

    def body(w_ref, g_ref, m_ref, v_ref, d_ref, nm_ref, nv_ref):
        d_ref[...], nm_ref[...], nv_ref[...] = _adam_update(w_ref[...], g_ref[...], m_ref[...], v_ref[...])

    blk = pl.BlockSpec((t, cols), lambda i: (i, 0))
    return _pc(body, name=name, grid=(rows // t,), in_specs=[blk] * 4, out_specs=[blk] * 3,
               out_shape=[jax.ShapeDtypeStruct(w.shape, f32)] * 3, compiler_params=_params("parallel"))(w, g, m, v)


def adamw_joined(name, w, g_mine, g_theirs, m, v, half, axis):
    rows, cols = w.shape
    hr, hc = g_mine.shape
    tr, tc = _tile(hr, hc, 9)
    nr, nc = hr // tr, hc // tc

    def body(half_ref, w_ref, gm_ref, gt_ref, m_ref, v_ref, g_ref, d_ref, nm_ref, nv_ref):
        pos = pl.program_id(axis) // (nr if axis == 0 else nc)
        g = jnp.where(pos == half_ref[0], gm_ref[...], gt_ref[...])
        delta, nm, nv = _adam_update(w_ref[...], g, m_ref[...], v_ref[...])
        g_ref[...] = g
        d_ref[...] = delta
        nm_ref[...] = nm
        nv_ref[...] = nv

    whole = pl.BlockSpec((tr, tc), lambda i, j, hf: (i, j))
    part = pl.BlockSpec((tr, tc), lambda i, j, hf: (i % nr, j % nc))
    grid_spec = pltpu.PrefetchScalarGridSpec(num_scalar_prefetch=1, grid=(rows // tr, cols // tc),
                                             in_specs=[whole, part, part, whole, whole], out_specs=[whole] * 4)
    return _pc(body, name=name, grid_spec=grid_spec, out_shape=[jax.ShapeDtypeStruct(w.shape, f32)] * 4,
               compiler_params=_params("parallel", "parallel"))(half, w, g_mine, g_theirs, m, v)


def add_halves(name, g, recv, half, kind):
    s, r, c = recv.shape
    tr, tc = _tile(r, c, 3)
    nc = c // tc

    def body(half_ref, g_ref, r_ref, o_ref):
        o_ref[...] = (g_ref[...].reshape(r_ref.shape).astype(f32) + r_ref[...].astype(f32)).astype(bf16)

    if kind == "lead":
        g_spec = pl.BlockSpec((1, 1, tr, tc), lambda i, j, k, hf: (hf[0], i, j, k))
    else:
        g_spec = pl.BlockSpec((1, tr, tc), lambda i, j, k, hf: (i, j, hf[0] * nc + k))
    blk = pl.BlockSpec((1, tr, tc), lambda i, j, k, hf: (i, j, k))
    grid_spec = pltpu.PrefetchScalarGridSpec(num_scalar_prefetch=1, grid=(s, r // tr, nc), in_specs=[g_spec, blk], out_specs=blk)
    return _pc(body, name=name, grid_spec=grid_spec, out_shape=jax.ShapeDtypeStruct((s, r, c), bf16),
               compiler_params=_params("parallel", "parallel", "parallel"))(half, g, recv)


def sum_leading(name, x, out_dtype=f32):
    s, r, c = x.shape
    tr, tc = _tile(r, c, s + 1)

    def body(x_ref, o_ref):
        acc = x_ref[0].astype(f32)
        for i in range(1, s):
            acc = acc + x_ref[i].astype(f32)
        o_ref[...] = acc.astype(out_dtype)

    return _pc(body, name=name, grid=(r // tr, c // tc), in_specs=[pl.BlockSpec((s, tr, tc), lambda i, j: (0, i, j))],
               out_specs=pl.BlockSpec((tr, tc), lambda i, j: (i, j)), out_shape=jax.ShapeDtypeStruct((r, c), out_dtype),
               compiler_params=_params("parallel", "parallel"))(x)


def _place():
    x, y, c = lax.axis_index("x"), lax.axis_index("y"), lax.axis_index("c")
    return x, y, c


def _other_chips(x, y):
    return [(1 - x, y), (x, 1 - y), (1 - x, 1 - y)]


_ANY = pl.BlockSpec(memory_space=pl.ANY)


def _half_view(ref, half, kind, lead=()):
    if kind == "lead":
        return ref.at[(*lead, half)]
    width = ref.shape[-1] // 2
    return ref.at[(*lead, *([slice(None)] * (len(ref.shape) - len(lead) - 1)), pl.ds(half * width, width))]


def gather_weights(shards, kinds):
    n = len(shards)

    def body(*refs):
        ins, outs = refs[:n], refs[n:2 * n]
        send1, recv1, send2, recv2 = refs[2 * n:]
        x, y, c = _place()
        chip = 2 * x + y
        sibling = (x, y, 1 - c)
        others = _other_chips(x, y)

        def remote(src, dst, s_sem, r_sem, to):
            return pltpu.make_async_remote_copy(src_ref=src, dst_ref=dst, send_sem=s_sem, recv_sem=r_sem,
                                                device_id=to, device_id_type=MESH)

        first = []
        for a in range(n):
            for j, (ox, oy) in enumerate(others):
                first.append(remote(_half_view(ins[a], c, kinds[a]), _half_view(outs[a], c, kinds[a], (chip,)),
                                    send1.at[a, j], recv1.at[a, j], (ox, oy, c)))
        for cp in first:
            cp.start()
        passed = []
        for a in range(n):
            for j, (ox, oy) in enumerate(others):
                slab = _half_view(outs[a], c, kinds[a], (2 * ox + oy,))
                remote(slab, slab, send1.at[a, j], recv1.at[a, j], (ox, oy, c)).wait_recv()
                cp = remote(slab, slab, send2.at[a, j], recv2.at[a, j], sibling)
                cp.start()
                passed.append(cp)
        for a in range(n):
            for j, (ox, oy) in enumerate(others):
                slab = _half_view(outs[a], 1 - c, kinds[a], (2 * ox + oy,))
                remote(slab, slab, send2.at[a, j], recv2.at[a, j], sibling).wait_recv()
        for cp in first + passed:
            cp.wait_send()

    sems = [pltpu.SemaphoreType.DMA((n, 3))] * 4
    return _pc(body, name="gather_weights", in_specs=[_ANY] * n, out_specs=[_ANY] * n,
               out_shape=[jax.ShapeDtypeStruct((4,) + s.shape, s.dtype) for s in shards],
               scratch_shapes=sems)(*shards)


def swap_with_sibling(name, sends):
    n = len(sends)

    def body(*refs):
        ins, outs = refs[:n], refs[n:2 * n]
        send, recv = refs[2 * n:]
        x, y, c = _place()
        cps = [pltpu.make_async_remote_copy(src_ref=ins[a], dst_ref=outs[a], send_sem=send.at[a], recv_sem=recv.at[a],
                                            device_id=(x, y, 1 - c), device_id_type=MESH) for a in range(n)]
        for cp in cps:
            cp.start()
        for cp in cps:
            cp.wait()

    return _pc(body, name=name, in_specs=[_ANY] * n, out_specs=[_ANY] * n,
               out_shape=[jax.ShapeDtypeStruct(s.shape, s.dtype) for s in sends],
               scratch_shapes=[pltpu.SemaphoreType.DMA((n,))] * 2)(*sends)


def send_grad_halves(grads, kinds):
    n = len(grads)

    def body(*refs):
        ins, outs = refs[:n], refs[n:2 * n]
        send, recv = refs[2 * n:]
        x, y, c = _place()
        cps = [pltpu.make_async_remote_copy(src_ref=_half_view(ins[a], 1 - c, kinds[a]), dst_ref=outs[a], send_sem=send.at[a],
                                            recv_sem=recv.at[a], device_id=(x, y, 1 - c), device_id_type=MESH) for a in range(n)]
        for cp in cps:
            cp.start()
        for cp in cps:
            cp.wait()

    shape = lambda g, kind: g.shape[1:] if kind == "lead" else g.shape[:-1] + (g.shape[-1] // 2,)
    return _pc(body, name="send_grad_halves", in_specs=[_ANY] * n, out_specs=[_ANY] * n,
               out_shape=[jax.ShapeDtypeStruct(shape(g, k), g.dtype) for g, k in zip(grads, kinds)],
               scratch_shapes=[pltpu.SemaphoreType.DMA((n,))] * 2)(*grads)


def scatter_to_chips(parts):
    n = len(parts)

    def body(*refs):
        ins, outs = refs[:n], refs[n:2 * n]
        send, recv = refs[2 * n:]
        x, y, c = _place()
        chip = 2 * x + y
        others = _other_chips(x, y)
        cps = []
        for a in range(n):
            for j, (ox, oy) in enumerate(others):
                cps.append(pltpu.make_async_remote_copy(
                    src_ref=ins[a].at[2 * ox + oy], dst_ref=outs[a].at[chip], send_sem=send.at[a, j], recv_sem=recv.at[a, j],
                    device_id=(ox, oy, c), device_id_type=MESH))
        for cp in cps:
            cp.start()
        for a in range(n):
            for j, (ox, oy) in enumerate(others):
                slab = outs[a].at[2 * ox + oy]
                pltpu.make_async_remote_copy(src_ref=slab, dst_ref=slab, send_sem=send.at[a, j], recv_sem=recv.at[a, j],
                                             device_id=(ox, oy, c), device_id_type=MESH).wait_recv()
        for cp in cps:
            cp.wait_send()

    return _pc(body, name="scatter_to_chips", in_specs=[_ANY] * n, out_specs=[_ANY] * n,
               out_shape=[jax.ShapeDtypeStruct(p.shape, p.dtype) for p in parts],
               scratch_shapes=[pltpu.SemaphoreType.DMA((n, 3))] * 2)(*parts)


def gather_from_all(pack):
    def body(in_ref, out_ref, send, recv):
        x, y, c = _place()
        me = 4 * x + 2 * y + c
        flips = [(fx, fy, fc) for fx in (0, 1) for fy in (0, 1) for fc in (0, 1)][1:]
        peers = [(jnp.where(fx, 1 - x, x), jnp.where(fy, 1 - y, y), jnp.where(fc, 1 - c, c)) for fx, fy, fc in flips]
        cps = [pltpu.make_async_remote_copy(src_ref=in_ref, dst_ref=out_ref.at[me], send_sem=send.at[j], recv_sem=recv.at[j],
                                            device_id=p, device_id_type=MESH) for j, p in enumerate(peers)]
        for cp in cps:
            cp.start()
        for j, (px, py, pc) in enumerate(peers):
            slab = out_ref.at[4 * px + 2 * py + pc]
            pltpu.make_async_remote_copy(src_ref=slab, dst_ref=slab, send_sem=send.at[j], recv_sem=recv.at[j],
                                         device_id=(px, py, pc), device_id_type=MESH).wait_recv()
        for cp in cps:
            cp.wait_send()

    return _pc(body, name="gather_from_all", in_specs=[_ANY], out_specs=_ANY,
               out_shape=jax.ShapeDtypeStruct((8,) + pack.shape, pack.dtype),
               scratch_shapes=[pltpu.SemaphoreType.DMA((7,)), pltpu.SemaphoreType.DMA((7,))])(pack)


def _pack(arrays, width):
    rows = []
    for a in arrays:
        flat = a.reshape(-1).astype(f32)
        pad = (-flat.shape[0]) % (8 * width)
        rows.append(jnp.pad(flat, (0, pad)).reshape(-1, width))
    return jnp.concatenate(rows, axis=0)


def _unpack(pack, like, width):
    out, row = [], 0
    for a in like:
        size = 1
        for s in a.shape:
            size *= s
        n_rows = -(-size // (8 * width)) * 8
        out.append(pack[row:row + n_rows].reshape(-1)[:size].reshape(a.shape))
        row += n_rows
    return out


def _halves(a2d):
    r, c = a2d.shape
    return a2d.reshape(2, r // 2, c)


def local_step(x, meta, norm_w, w_main, w_ba, conv_w, a_log, dt_bias, pool_mix, pool_scale, dn_norm_w,
               w_pool_out, w_dn_out, w_o, final_w, target):
    seq, d = x.shape
    pw = pool_scale.shape[1]
    dw = w_dn_out.shape[0]
    n_heads = dw // HEAD_DIM
    tp = FRONT_PAD + N_META + seq
    hp = jnp.concatenate([jnp.zeros((FRONT_PAD, d), f32), meta, x], axis=0)
    tm_big = tp // 2 if tp % 32 == 0 else tp
    tm_norm = max(t for t in range(16, min(tp, 352) + 1, 16) if tp % t == 0)
    tile = min(512, d)
    off_q = 2 * pw
    off_zd = off_q + 3 * dw
    off_gp = off_zd + dw
    off_gd = off_gp + d
    a_log128 = jnp.pad(a_log, ((0, 0), (0, LANES - n_heads)))
    dt128 = jnp.pad(dt_bias, ((0, 0), (0, LANES - n_heads)))

    xn = norm_fwd(hp, norm_w, tm_norm)
    proj = matmul("proj", xn, w_main, tb=True, tm=tp, tn=tile)
    ba = matmul("proj_ba", xn, w_ba, tb=True, tm=tp, tn=LANES)
    y_pool = pool_fwd(proj, pool_mix, pool_scale, pw)
    q, k, v, beta_b, g_b = dn_pre_fwd(proj, ba, conv_w, a_log128, dt128, n_heads, off_q // HEAD_DIM)
    assert off_zd % dw == 0
    *inter, tmats = dn_intra_fwd(q, k, v, beta_b, g_b, n_heads)
    y_dn, states = dn_seq_fwd(inter, proj, dn_norm_w, n_heads, off_zd // dw)
    a_mat = matmul("pool_out", y_pool, w_pool_out, tm=tp, tn=tile)

    def merge(acc, a_t, gp_t, gd_t):
        return acc, sigmoid(gp_t) * a_t + sigmoid(gd_t) * acc

    b_mat, merged = matmul("dn_out_merge", y_dn, w_dn_out, tm=tm_big, tn=tile, extras=[(a_mat, 0), (proj, off_gp), (proj, off_gd)],
                           epi=merge, out_dtypes=(f32, bf16))
    out = matmul("out_proj", merged, w_o, tm=tm_big, tn=tile, extras=[(hp, 0)], epi=lambda acc, h_t: (acc + h_t,))
    loss, dout, dout_b, dfinal_w = loss_stage(out, final_w, target)

    def unmerge(dm, a_t, b_t, gp_t, gd_t):
        sp, sd = sigmoid(gp_t), sigmoid(gd_t)
        return dm * sp, dm * sd, dm * a_t * sp * (1.0 - sp), dm * b_t * sd * (1.0 - sd)

    d_a, d_b, d_gp, d_gd = matmul("d_merged", dout_b, w_o, tb=True, tm=tm_big, tn=tile,
                                  extras=[(a_mat, 0), (b_mat, 0), (proj, off_gp), (proj, off_gd)], epi=unmerge,
                                  out_dtypes=(bf16,) * 4)
    g_w_o = matmul("g_w_o", merged, dout_b, ta=True, tm=tile, tn=tile, out_dtypes=(bf16,))
    d_y_pool = matmul("d_y_pool", d_a, w_pool_out, tb=True, tm=tp, tn=tile)
    g_w_pool_out = matmul("g_w_pool_out", y_pool, d_a, ta=True, tm=tile, tn=tile, out_dtypes=(bf16,))
    d_y_dn = matmul("d_y_dn", d_b, w_dn_out, tb=True, tm=tp, tn=tile)
    g_w_dn_out = matmul("g_w_dn_out", y_dn, d_b, ta=True, tm=tile, tn=tile, out_dtypes=(bf16,))
    d_u, d_zp, g_pool_mix, g_pool_scale = pool_bwd(proj, pool_mix, pool_scale, d_y_pool, pw)
    *d_inter, d_zd, g_dn_norm_w = dn_seq_bwd(inter, proj, dn_norm_w, states, d_y_dn, n_heads, off_zd // dw)
    d_q, d_k, d_v, d_beta, d_g = dn_intra_bwd(q, k, v, beta_b, g_b, tmats, d_inter, n_heads)
    d_qr, d_kr, d_vr, d_ba, g_cq, g_ck, g_cv, g_a_log, g_dt = dn_pre_bwd(
        proj, ba, conv_w, a_log128, dt128, (d_q, d_k, d_v, d_beta, d_g), n_heads, off_q // HEAD_DIM)
    d_proj = jnp.concatenate([d_u, d_zp, d_qr, d_kr, d_vr, d_zd, d_gp, d_gd], axis=1)
    d_ba_b = cast_bf16("cast_d_ba", d_ba)
    dxn_ba = matmul("dxn_ba", d_ba_b, w_ba, tm=tp, tn=tile)
    n_cols = d_proj.shape[1]
    tk_dxn = max(t for t in range(LANES, min(2048, n_cols) + 1, LANES) if n_cols % t == 0)
    dxn = matmul("dxn", d_proj, w_main, tm=tm_big, tn=tile, tk=tk_dxn,
                 extras=[(dxn_ba, 0)], epi=lambda acc, e: (acc + e,))
    g_w_main = matmul("g_w_main", d_proj, xn, ta=True, tm=tile, tn=tile, out_dtypes=(bf16,))
    g_w_ba = matmul("g_w_ba", d_ba_b, xn, ta=True, tm=LANES, tn=tile, out_dtypes=(bf16,))
    dh, g_norm_w = norm_bwd(hp, norm_w, dxn, dout, tm_norm)
    g_conv = jnp.concatenate([g_cq, g_ck, g_cv], axis=1)
    return (loss, dh, g_norm_w, g_w_main, g_w_ba, g_conv, g_a_log[:, :n_heads], g_dt[:, :n_heads], g_pool_mix, g_pool_scale,
            g_dn_norm_w, g_w_pool_out, g_w_dn_out, g_w_o, dfinal_w)


def kernel(x, meta_tokens, norm_w, w_in, conv_w, A_log, dt_bias, pool_mix, pool_scale, dn_norm_w, w_pool_out, w_dn_out, w_o, final_norm_w, loss_target, m_meta_tokens, m_norm_w, m_w_in, m_conv_w, m_A_log, m_dt_bias, m_pool_mix, m_pool_scale, m_dn_norm_w, m_w_pool_out, m_w_dn_out, m_w_o, m_final_norm_w, v_meta_tokens, v_norm_w, v_w_in, v_conv_w, v_A_log, v_dt_bias, v_pool_mix, v_pool_scale, v_dn_norm_w, v_w_pool_out, v_w_dn_out, v_w_o, v_final_norm_w):
    d = x.shape[-1]
    pw = pool_scale.shape[-1]
    dw = w_dn_out.shape[1] * 4
    n_heads = dw // HEAD_DIM
    gdim = pw // POOL_GROUPS
    chip = 2 * lax.axis_index("x") + lax.axis_index("y")
    core = lax.axis_index("c")

    half = core.astype(jnp.int32).reshape(1)
    me = 4 * lax.axis_index("x") + 2 * lax.axis_index("y") + core

    w_in_t = w_in[0].T
    mix_s = pool_mix[0].reshape(POOL_GROUPS * (gdim // 4), gdim)
    small_s = _pack([meta_tokens, conv_w[0]], d)
    small_rows = small_s.shape[0]
    small_s = jnp.pad(small_s, ((0, (-small_rows) % 16), (0, 0)))
    sw = w_in_t.shape[0]
    n_main, n_ba = 2 * pw + 4 * dw, 2 * n_heads

    shards = [cast_bf16("cast_w_in", w_in_t), _halves(cast_bf16("cast_w_po", w_pool_out[0])),
              _halves(cast_bf16("cast_w_do", w_dn_out[0])), _halves(cast_bf16("cast_w_o", w_o[0])),
              _halves(cast_bf16("cast_mix", mix_s)), _halves(small_s)]
    kinds = ["cols", "lead", "lead", "lead", "lead", "lead"]
    fill = lambda g, own: lax.dynamic_update_slice(g, own[None], (chip,) + (0,) * own.ndim)
    g_in, g_po, g_do, g_o, g_mix, g_small = [fill(g, own) for g, own in zip(gather_weights(shards, kinds), shards)]

    def shard_rows(lo, hi):
        cut = [(max(lo, j * sw), min(hi, (j + 1) * sw), j) for j in range(4)]
        return [g_in[j, a - j * sw:b - j * sw] for a, b, j in cut if a < b]

    w_main = jnp.concatenate(shard_rows(0, n_main) + shard_rows(n_main + n_ba, 4 * sw), axis=0)
    w_ba = jnp.pad(jnp.concatenate(shard_rows(n_main, n_main + n_ba), axis=0), ((0, LANES - n_ba), (0, 0)))
    cat_cols = lambda g: jnp.concatenate([g[j].reshape(-1, g.shape[-1]) for j in range(4)], axis=1)
    w_po_full = cat_cols(g_po)
    w_do_full = g_do.reshape(-1, g_do.shape[-1])
    w_o_full = g_o.reshape(-1, g_o.shape[-1])
    mix_full = g_mix.reshape(4, POOL_GROUPS, gdim // 4, gdim).transpose(1, 0, 2, 3).reshape(POOL_GROUPS, gdim, gdim)
    smalls = [_unpack(g_small[j].reshape(-1, d)[:small_rows], [meta_tokens, conv_w[0]], d) for j in range(4)]
    meta_full = jnp.concatenate([s[0] for s in smalls], axis=1)
    conv_full = jnp.concatenate([s[1] for s in smalls], axis=1)

    (loss, dh, g_norm_w, g_w_main, g_w_ba, g_conv, g_a_log, g_dt, g_mix_full, g_pool_scale, g_dn_norm_w, g_w_po, g_w_do, g_w_o_full,
     g_final_w) = local_step(x[0], meta_full, norm_w, w_main, w_ba, conv_full, A_log, dt_bias, mix_full, pool_scale, dn_norm_w,
                             w_po_full, w_do_full, w_o_full, final_norm_w.reshape(1, d), loss_target[0])
    grad_x = dh[FRONT_PAD + N_META:][None]
    g_meta = dh[FRONT_PAD:FRONT_PAD + N_META]

    def grad_rows(lo, hi):
        segs = [(0, n_main, g_w_main, 0), (n_main, n_main + n_ba, g_w_ba, 0), (n_main + n_ba, 4 * sw, g_w_main, n_main)]
        cut = [(max(lo, s0), min(hi, s1), s0, arr, off) for s0, s1, arr, off in segs]
        return [arr[a - s0 + off:b - s0 + off] for a, b, s0, arr, off in cut if a < b]

    in_parts = jnp.stack([jnp.concatenate(grad_rows(j * sw, (j + 1) * sw), axis=0) for j in range(4)])
    col_parts = lambda g: g.reshape(2, g.shape[0] // 2, 4, g.shape[1] // 4).transpose(0, 2, 1, 3)
    row_parts = lambda g: g.reshape(4, 2, g.shape[0] // 8, g.shape[1]).transpose(1, 0, 2, 3)
    mix_rows = POOL_GROUPS * (gdim // 4)
    mix_parts = (g_mix_full.astype(bf16).reshape(POOL_GROUPS, 4, gdim // 4, gdim).transpose(1, 0, 2, 3)
                 .reshape(4, 2, mix_rows // 2, gdim).transpose(1, 0, 2, 3))
    parts = [in_parts, col_parts(g_w_po), row_parts(g_w_do), row_parts(g_w_o_full), mix_parts]
    kinds = ["cols", "lead", "lead", "lead", "lead"]
    names = ["w_in", "w_po", "w_do", "w_o", "mix"]
    from_sibling = send_grad_halves(parts, kinds)
    chip_sums = [add_halves("add_" + nm, p, r, half, k) for nm, p, r, k in zip(names, parts, from_sibling, kinds)]
    own_slab = lambda got, sent: lax.dynamic_update_slice(
        got, lax.dynamic_slice(sent, (chip, 0, 0), (1,) + sent.shape[1:]), (chip, 0, 0))
    from_chips = [own_slab(got, sent) for got, sent in zip(scatter_to_chips(chip_sums), chip_sums)]
    mine = [sum_leading("sum_" + nm, fc) for nm, fc in zip(names, from_chips)]
    theirs = swap_with_sibling("swap_grad_halves", mine)

    small_like = [loss, g_norm_w, g_a_log, g_dt, g_pool_scale, g_dn_norm_w, g_final_w, g_conv, g_meta]
    pack = _pack(small_like, d)
    packs = lax.dynamic_update_slice(gather_from_all(pack), pack[None], (me, 0, 0))
    total = sum_leading("sum_small", packs)
    (loss_t, g_norm_w, g_a_log, g_dt, g_pool_scale, g_dn_norm_w, g_final_w, g_conv, g_meta) = _unpack(total, small_like, d)
    loss_out = loss_t[0, 0]
    g_conv_s = lax.dynamic_slice_in_dim(g_conv, chip * (g_conv.shape[1] // 4), g_conv.shape[1] // 4, axis=1)
    g_meta_s = lax.dynamic_slice_in_dim(g_meta, chip * (d // 4), d // 4, axis=1)

    weights = [meta_tokens, norm_w, w_in, conv_w, A_log, dt_bias, pool_mix, pool_scale, dn_norm_w, w_pool_out, w_dn_out, w_o, final_norm_w]
    ms = [m_meta_tokens, m_norm_w, m_w_in, m_conv_w, m_A_log, m_dt_bias, m_pool_mix, m_pool_scale, m_dn_norm_w, m_w_pool_out, m_w_dn_out, m_w_o, m_final_norm_w]
    vs = [v_meta_tokens, v_norm_w, v_w_in, v_conv_w, v_A_log, v_dt_bias, v_pool_mix, v_pool_scale, v_dn_norm_w, v_w_pool_out, v_w_dn_out, v_w_o, v_final_norm_w]
    grads = [g_meta_s, g_norm_w, None, g_conv_s[None], g_a_log, g_dt, None, g_pool_scale, g_dn_norm_w, None, None, None, g_final_w.reshape(d)]
    deltas, new_ms, new_vs = [None] * 13, [None] * 13, [None] * 13
    big = [2, 9, 10, 11, 6]
    for i, nm, g_mine, g_theirs in zip(big, names, mine, theirs):
        if nm == "w_in":
            to2d, back, axis = (lambda t: t[0].T), (lambda t: t.T[None]), 1
        else:
            to2d, back, axis = (lambda t: t.reshape(-1, t.shape[-1])), (lambda t, i=i: t.reshape(weights[i].shape)), 0
        res = adamw_joined("adamw_" + nm, to2d(weights[i]), g_mine, g_theirs, to2d(ms[i]), to2d(vs[i]), half, axis)
        grads[i], deltas[i], new_ms[i], new_vs[i] = [back(t) for t in res]
    small_idx = [i for i in range(13) if i not in big]
    packs = [_pack([arrs[i] for i in small_idx], d) for arrs in (weights, grads, ms, vs)]
    outs = adamw("adamw_small", *packs)
    like = [weights[i] for i in small_idx]
    for res, dest in zip(outs, (deltas, new_ms, new_vs)):
        for i, val in zip(small_idx, _unpack(res, like, d)):
            dest[i] = val
    return (loss_out, grad_x, *grads, *deltas, *new_ms, *new_vs)
```

```python
import functools

import jax
import jax.numpy as jnp
from jax import lax
from jax.experimental import pallas as pl
from jax.experimental.pallas import tpu as pltpu

f32 = jnp.float32
bf16 = jnp.bfloat16
HIGHEST = lax.Precision.HIGHEST
MESH = pl.DeviceIdType.MESH

N_META = 16
CHUNK = 64
FRONT_PAD = (-N_META) % CHUNK
HEAD_DIM = 128
POOL_GROUPS = 4
POOL_WINDOWS = (2, 4, 8, 16)
CONV_WIDTH = 4
NORM_EPS = 1e-6
ADAM_LR, ADAM_B1, ADAM_B2, ADAM_EPS, ADAM_WD, ADAM_STEP = 0.001, 0.9, 0.999, 1e-08, 0.01, 10
LANES = 128
VMEM_LIMIT = 56 * 2**20


def _pc(body, **kw):
    return pl.pallas_call(body, **kw)


def _params(*sem, **kw):
    return pltpu.CompilerParams(dimension_semantics=sem or None, vmem_limit_bytes=VMEM_LIMIT, **kw)


def _dg(a, b, dims, prec=None):
    return lax.dot_general(a, b, (dims, ((), ())), precision=prec, preferred_element_type=f32)


@jax.custom_vjp
def mm_nn(a, b):
    return _dg(a.astype(bf16), b.astype(bf16), ((1,), (0,)))


@jax.custom_vjp
def mm_nt(a, b):
    return _dg(a.astype(bf16), b.astype(bf16), ((1,), (1,)))


@jax.custom_vjp
def mm_tn(a, b):
    return _dg(a.astype(bf16).T, b.astype(bf16), ((1,), (0,)))


mm_nn.defvjp(lambda a, b: (mm_nn(a, b), (a, b)), lambda r, dy: (mm_nt(dy, r[1]), mm_tn(r[0], dy)))
mm_nt.defvjp(lambda a, b: (mm_nt(a, b), (a, b)), lambda r, dy: (mm_nn(dy, r[1]), mm_tn(dy, r[0])))
mm_tn.defvjp(lambda a, b: (mm_tn(a, b), (a, b)), lambda r, dy: (mm_nt(r[1], dy), mm_nn(r[0], dy)))


def _split3(x):
    hi = x.astype(bf16)
    r1 = x - hi.astype(f32)
    mid = r1.astype(bf16)
    lo = (r1 - mid.astype(f32)).astype(bf16)
    return hi, mid, lo


@jax.custom_vjp
def mm_sel(sel, x):
    s = sel.astype(bf16)
    d = ((1,), (0,))
    hi, mid, lo = _split3(x)
    return _dg(s, hi, d) + _dg(s, mid, d) + _dg(s, lo, d)


def _mm_sel_bwd(sel, dy):
    s = sel.astype(bf16)
    d = ((0,), (0,))
    hi, mid, lo = _split3(dy)
    return jnp.zeros_like(sel), _dg(s, hi, d) + _dg(s, mid, d) + _dg(s, lo, d)


mm_sel.defvjp(lambda sel, x: (mm_sel(sel, x), sel), _mm_sel_bwd)


@jax.custom_vjp
def mm_pick(x, sel):
    s = sel.astype(bf16)
    d = ((1,), (0,))
    hi, mid, lo = _split3(x)
    return _dg(hi, s, d) + _dg(mid, s, d) + _dg(lo, s, d)


def _mm_pick_bwd(sel, dy):
    s = sel.astype(bf16)
    d = ((1,), (1,))
    hi, mid, lo = _split3(dy)
    return _dg(hi, s, d) + _dg(mid, s, d) + _dg(lo, s, d), jnp.zeros_like(sel)


mm_pick.defvjp(lambda x, sel: (mm_pick(x, sel), sel), _mm_pick_bwd)


def mm3(a, b):
    a_hi = a.astype(bf16)
    a_lo = (a - a_hi.astype(f32)).astype(bf16)
    b_hi = b.astype(bf16)
    b_lo = (b - b_hi.astype(f32)).astype(bf16)
    d = ((1,), (0,))
    return _dg(a_hi, b_hi, d) + _dg(a_hi, b_lo, d) + _dg(a_lo, b_hi, d)


def tri_inv(ls):
    n = ls[0].shape[0]
    eye = (lax.broadcasted_iota(jnp.int32, (n, n), 0) == lax.broadcasted_iota(jnp.int32, (n, n), 1)).astype(f32)
    ms = [-l for l in ls]
    ts = [eye + m for m in ms]
    k = 1
    while 2 * k < CHUNK:
        ms = [mm3(m, m) for m in ms]
        ts = [t + mm3(t, m) for t, m in zip(ts, ms)]
        k *= 2
    return ts


@functools.partial(jax.custom_vjp, nondiff_argnums=(1,))
def shift_rows(x, j):
    n = x.shape[0]
    rows = lax.broadcasted_iota(jnp.int32, x.shape, 0)
    if j >= 0:
        return jnp.where(rows >= j, pltpu.roll(x, j, 0), 0.0)
    return jnp.where(rows < n + j, pltpu.roll(x, n + j, 0), 0.0)


shift_rows.defvjp(lambda x, j: (shift_rows(x, j), None), lambda j, _, dy: (shift_rows(dy, -j),))


def sigmoid(x):
    return 1.0 / (1.0 + jnp.exp(-x))


def silu(x):
    return x * sigmoid(x)


def softplus(x):
    return jnp.maximum(x, 0.0) + jnp.log(1.0 + jnp.exp(-jnp.abs(x)))


def rmsnorm(x, w):
    return x * lax.rsqrt(jnp.mean(x * x, axis=-1, keepdims=True) + NORM_EPS) * w


def l2norm(x):
    return x * lax.rsqrt(jnp.sum(x * x, axis=-1, keepdims=True) + NORM_EPS)


def pool_fn(u, zp, mix, scale, group):
    rows = lax.broadcasted_iota(jnp.int32, u.shape, 0)
    sums = []
    s, w = u, 1
    while w < POOL_WINDOWS[-1]:
        s = s + shift_rows(s, w)
        w *= 2
        sums.append(s)
    total = sums[-1]
    for gi in range(POOL_GROUPS - 2, -1, -1):
        total = jnp.where(group == gi, sums[gi], total)
    window = jnp.left_shift(2, group)
    cnt = jnp.clip(rows - (FRONT_PAD - 1), 1, window).astype(f32)
    pooled = total / cnt - u
    return mm_nn(pooled, mix) * scale * silu(zp)


def conv_silu(x, w):
    k = CONV_WIDTH
    y = x * w[k - 1:k, :]
    for kk in range(k - 1):
        y = y + shift_rows(x, k - 1 - kk) * w[kk:kk + 1, :]
    return silu(y)


def _lane_pick(row, idx):
    lanes = lax.broadcasted_iota(jnp.int32, row.shape, 1)
    return jnp.sum(jnp.where(lanes == idx, row, 0.0), axis=1, keepdims=True)


def dn_pre_fn(qr, kr, vr, ba, cwq, cwk, cwv, a_log, dt_bias, head, n_heads):
    q = l2norm(conv_silu(qr, cwq)) * (HEAD_DIM ** -0.5)
    k = l2norm(conv_silu(kr, cwk))
    v = conv_silu(vr, cwv)
    r = lax.broadcasted_iota(jnp.int32, (LANES, LANES), 0)
    b_b = mm_pick(ba, (r == head).astype(f32))
    a_b = mm_pick(ba, (r == head + n_heads).astype(f32))
    real = lax.broadcasted_iota(jnp.int32, qr.shape, 0) >= FRONT_PAD
    beta_b = jnp.where(real, sigmoid(b_b), 0.0)
    g_b = jnp.where(real, -jnp.exp(_lane_pick(a_log, head)) * softplus(a_b + _lane_pick(dt_bias, head)), 0.0)
    return q, k, v, beta_b, g_b


def _chunk_masks(rows):
    r = lax.broadcasted_iota(jnp.int32, (rows, rows), 0)
    c = lax.broadcasted_iota(jnp.int32, (rows, rows), 1)
    same = (r // CHUNK) == (c // CHUNK)
    return same, jnp.logical_and(same, r >= c), jnp.logical_and(same, r > c)


def _lane0(rows):
    return (lax.broadcasted_iota(jnp.int32, (rows, LANES), 1) == 0).astype(bf16)


@jax.custom_vjp
def lane0_as_row(x):
    sel = _lane0(x.shape[0])
    d = ((1,), (1,))
    hi, mid, lo = _split3(x)
    return _dg(sel, hi, d) + _dg(sel, mid, d) + _dg(sel, lo, d)


def _lane0_as_row_bwd(rows, dy):
    sel = _lane0(rows)
    d = ((0,), (0,))
    hi, mid, lo = _split3(dy)
    return (_dg(hi, sel, d) + _dg(mid, sel, d) + _dg(lo, sel, d),)


lane0_as_row.defvjp(lambda x: (lane0_as_row(x), x.shape[0]), _lane0_as_row_bwd)


def gate_fn(g_b):
    rows = g_b.shape[0]
    same, causal, _ = _chunk_masks(rows)
    gcum_b = mm_sel(causal.astype(f32), g_b)
    glast_b = mm_sel(same.astype(f32), g_b)
    g_rows = jnp.broadcast_to(gcum_b[:, :1], (rows, rows))
    decay = jnp.where(causal, jnp.exp(jnp.where(causal, g_rows - lane0_as_row(gcum_b), 0.0)), 0.0)
    return decay, jnp.exp(gcum_b), jnp.exp(glast_b - gcum_b), jnp.exp(glast_b)


def _fold_matrix(rows):
    r = lax.broadcasted_iota(jnp.int32, (rows, LANES), 0)
    c = lax.broadcasted_iota(jnp.int32, (rows, LANES), 1)
    return (r % CHUNK == c).astype(bf16)


@jax.custom_vjp
def fold_chunks(x):
    return _dg(x.astype(bf16), _fold_matrix(x.shape[0]), ((1,), (0,)))


def _fold_chunks_bwd(rows, dy):
    fold = _fold_matrix(rows)
    d = ((1,), (1,))
    hi, mid, lo = _split3(dy)
    return (_dg(hi, fold, d) + _dg(mid, fold, d) + _dg(lo, fold, d),)


fold_chunks.defvjp(lambda x: (fold_chunks(x), x.shape[0]), _fold_chunks_bwd)


def lmat_fn(k, beta_b, decay):
    _, _, strict = _chunk_masks(k.shape[0])
    return jnp.where(strict, mm_nt(k * beta_b, k) * decay, 0.0)


def intra_fn(tmat, q, k, v, beta_b, decay, eg, kfac):
    _, causal, _ = _chunk_masks(q.shape[0])
    k_beta = k * beta_b
    u_c = mm_nn(tmat, v * beta_b)
    w_c = mm_nn(tmat, k_beta * eg)
    qk = jnp.where(causal, mm_nt(q, k) * decay, 0.0)
    return u_c, w_c, q * eg, k * kfac, fold_chunks(qk)


def gated_norm(o, norm_w, zd):
    return rmsnorm(o, norm_w) * silu(zd)


def loss_fn(o, w, tgt):
    err = rmsnorm(o, w) - tgt
    return 0.5 * jnp.sum(jnp.mean(err * err, axis=-1))


_ANY = pl.BlockSpec(memory_space=pl.ANY)


class Hosted:
    def __init__(self, arrays, out_shapes, sems, stages):
        self.arrays, self.out_shapes, self.sems, self.stages = list(arrays), list(out_shapes), list(sems), stages


def matmul(name, a, b, *, ta=False, tb=False, tm, tn, tk=None, extras=(), epi=None, out_dtypes=(f32,), hosted=None):
    m, k = (a.shape[1], a.shape[0]) if ta else a.shape
    n = b.shape[0] if tb else b.shape[1]
    tm, tn, tk = min(tm, m), min(tn, n), min(tk or k, k)
    assert m % tm == 0 and n % tn == 0 and k % tk == 0, (name, m, n, k, tm, tn, tk)
    nm, nn, nk = m // tm, n // tn, k // tk
    a_spec = pl.BlockSpec((tk, tm), lambda i, j, kk: (kk, i)) if ta else pl.BlockSpec((tm, tk), lambda i, j, kk: (i, kk))
    b_spec = pl.BlockSpec((tn, tk), lambda i, j, kk: (j, kk)) if tb else pl.BlockSpec((tk, tn), lambda i, j, kk: (kk, j))
    ex_specs = []
    for _, off in extras:
        assert off % tn == 0, (name, off, tn)
        ex_specs.append(pl.BlockSpec((tm, tn), functools.partial(lambda i, j, kk, o: (i, o + j), o=off // tn)))
    n_ex, n_out = len(extras), len(out_dtypes)
    dims = ((0 if ta else 1,), (1 if tb else 0,))
    n_hin = len(hosted.arrays) if hosted else 0
    n_hout = len(hosted.out_shapes) if hosted else 0
    n_sem = len(hosted.sems) if hosted else 0

    def body(a_ref, b_ref, *rest):
        ex_refs, rest = rest[:n_ex], rest[n_ex:]
        hin_refs, rest = rest[:n_hin], rest[n_hin:]
        out_refs, rest = rest[:n_out], rest[n_out:]
        hout_refs, rest = rest[:n_hout], rest[n_hout:]
        sem_refs = rest[len(rest) - n_sem:] if n_sem else ()
        step = (pl.program_id(0) * nn + pl.program_id(1)) * nk + pl.program_id(2)
        stages = hosted.stages(hin_refs, hout_refs, sem_refs) if hosted else []
        for frac, emit in stages:
            if frac < 1.0:
                pl.when(step == int(frac * (nm * nn * nk - 1)))(emit)

        def finish(acc):
            res = epi(acc, *[r[...] for r in ex_refs]) if epi is not None else (acc,)
            for o_ref, val in zip(out_refs, res):
                o_ref[...] = val.astype(o_ref.dtype)

        p = _dg(a_ref[...], b_ref[...], dims)
        if nk == 1:
            finish(p)
        else:
            acc_ref = rest[0]
            kk = pl.program_id(2)

            @pl.when(kk == 0)
            def _():
                acc_ref[...] = p

            @pl.when(kk > 0)
            def _():
                acc_ref[...] += p

            @pl.when(kk == nk - 1)
            def _():
                finish(acc_ref[...])

        for frac, emit in stages:
            if frac >= 1.0:
                pl.when(step == nm * nn * nk - 1)(emit)

    outs = _pc(
        body, name=name, grid=(nm, nn, nk),
        in_specs=[a_spec, b_spec] + ex_specs + [_ANY] * n_hin,
        out_specs=[pl.BlockSpec((tm, tn), lambda i, j, kk: (i, j))] * n_out + [_ANY] * n_hout,
        out_shape=[jax.ShapeDtypeStruct((m, n), dt) for dt in out_dtypes] + (hosted.out_shapes if hosted else []),
        scratch_shapes=([pltpu.VMEM((tm, tn), f32)] if nk > 1 else []) + (hosted.sems if hosted else []),
        compiler_params=_params(*(("arbitrary",) * 3 if hosted else ("parallel", "parallel", "arbitrary"))),
    )(a, b, *[e for e, _ in extras], *(hosted.arrays if hosted else []))
    return outs[0] if len(outs) == 1 else outs


def _row_tile(rows, cols, n_arrays, itemsize=4, budget=24 * 2**20):
    best = None
    for t in range(16, rows + 1, 16):
        if rows % t == 0 and 2 * n_arrays * t * cols * itemsize <= budget:
            best = t
    return best or rows


def _tile(rows, cols, n_arrays, budget=24 * 2**20):
    if rows % 16 == 0 or cols % LANES != 0:
        return _row_tile(rows, cols, n_arrays, budget=budget), cols
    fits = [t for t in range(LANES, cols + 1, LANES) if cols % t == 0 and 2 * n_arrays * rows * t * 4 <= budget]
    return rows, (max(fits) if fits else LANES)


def cast_bf16(name, x):
    rows, cols = x.shape
    tr, tc = _tile(rows, cols, 2)

    def body(x_ref, o_ref):
        o_ref[...] = x_ref[...].astype(bf16)

    blk = pl.BlockSpec((tr, tc), lambda i, j: (i, j))
    return _pc(body, name=name, grid=(rows // tr, cols // tc), in_specs=[blk], out_specs=blk,
               out_shape=jax.ShapeDtypeStruct(x.shape, bf16), compiler_params=_params("parallel", "parallel"))(x)


def norm_fwd(hp, norm_w, tm):
    tp, d = hp.shape

    def body(h_ref, w_ref, o_ref):
        o_ref[...] = rmsnorm(h_ref[...], w_ref[...]).astype(bf16)

    return _pc(body, name="norm_fwd", grid=(tp // tm,),
               in_specs=[pl.BlockSpec((tm, d), lambda i: (i, 0)), pl.BlockSpec((1, d), lambda i: (0, 0))],
               out_specs=pl.BlockSpec((tm, d), lambda i: (i, 0)), out_shape=jax.ShapeDtypeStruct((tp, d), bf16),
               compiler_params=_params("parallel"))(hp, norm_w)


def norm_bwd(hp, norm_w, dxn, dout, tm):
    tp, d = hp.shape

    def body(h_ref, w_ref, dxn_ref, dout_ref, dh_ref, dw_ref):
        _, vjp = jax.vjp(rmsnorm, h_ref[...], w_ref[...])
        dh, dw = vjp(dxn_ref[...])
        dh_ref[...] = dh + dout_ref[...]

        @pl.when(pl.program_id(0) == 0)
        def _():
            dw_ref[...] = jnp.zeros_like(dw_ref)

        dw_ref[...] += dw

    row = pl.BlockSpec((tm, d), lambda i: (i, 0))
    vec = pl.BlockSpec((1, d), lambda i: (0, 0))
    return _pc(body, name="norm_bwd", grid=(tp // tm,), in_specs=[row, vec, row, row], out_specs=[row, vec],
               out_shape=[jax.ShapeDtypeStruct((tp, d), f32), jax.ShapeDtypeStruct((1, d), f32)],
               compiler_params=_params("arbitrary"))(hp, norm_w, dxn, dout)


def pool_fwd(proj, mix, scale, pw):
    tp = proj.shape[0]
    g = pw // POOL_GROUPS

    def body(u_ref, z_ref, mix_ref, s_ref, y_ref):
        y_ref[...] = pool_fn(u_ref[...], z_ref[...], mix_ref[0], s_ref[...], pl.program_id(0)).astype(bf16)

    return _pc(body, name="pool_fwd", grid=(POOL_GROUPS,),
               in_specs=[pl.BlockSpec((tp, g), lambda i: (0, i)), pl.BlockSpec((tp, g), lambda i: (0, POOL_GROUPS + i)),
                         pl.BlockSpec((1, g, g), lambda i: (i, 0, 0)), pl.BlockSpec((1, g), lambda i: (0, i))],
               out_specs=pl.BlockSpec((tp, g), lambda i: (0, i)), out_shape=jax.ShapeDtypeStruct((tp, pw), bf16),
               compiler_params=_params("parallel"))(proj, proj, mix, scale)


def pool_bwd(proj, mix, scale, dy, pw):
    tp = proj.shape[0]
    g = pw // POOL_GROUPS

    def body(u_ref, z_ref, mix_ref, s_ref, dy_ref, du_ref, dz_ref, dmix_ref, ds_ref):
        grp = pl.program_id(0)
        _, vjp = jax.vjp(lambda u, z, m, s: pool_fn(u, z, m, s, grp), u_ref[...], z_ref[...], mix_ref[0].astype(f32), s_ref[...])
        du, dz, dmix, ds = vjp(dy_ref[...])
        du_ref[...] = du.astype(bf16)
        dz_ref[...] = dz.astype(bf16)
        dmix_ref[0] = dmix
        ds_ref[...] = ds

    col = pl.BlockSpec((tp, g), lambda i: (0, i))
    return _pc(body, name="pool_bwd", grid=(POOL_GROUPS,),
               in_specs=[col, pl.BlockSpec((tp, g), lambda i: (0, POOL_GROUPS + i)),
                         pl.BlockSpec((1, g, g), lambda i: (i, 0, 0)), pl.BlockSpec((1, g), lambda i: (0, i)), col],
               out_specs=[col, col, pl.BlockSpec((1, g, g), lambda i: (i, 0, 0)), pl.BlockSpec((1, g), lambda i: (0, i))],
               out_shape=[jax.ShapeDtypeStruct((tp, pw), bf16), jax.ShapeDtypeStruct((tp, pw), bf16),
                          jax.ShapeDtypeStruct((POOL_GROUPS, g, g), f32), jax.ShapeDtypeStruct((1, pw), f32)],
               compiler_params=_params("parallel"))(proj, proj, mix, scale, dy)


def _dn_pre_specs(tp, n_heads, q_off):
    hb = lambda off: pl.BlockSpec((tp, HEAD_DIM), functools.partial(lambda h, o: (0, o + h), o=off))
    cw = lambda off: pl.BlockSpec((CONV_WIDTH, HEAD_DIM), functools.partial(lambda h, o: (0, o + h), o=off))
    whole = lambda shape: pl.BlockSpec(shape, lambda h: (0, 0))
    return ([hb(q_off), hb(q_off + n_heads), hb(q_off + 2 * n_heads), whole((tp, LANES)),
             cw(0), cw(n_heads), cw(2 * n_heads), whole((1, LANES)), whole((1, LANES))], hb, cw, whole)


def dn_pre_fwd(proj, ba, conv_w, a_log, dt_bias, n_heads, q_off):
    tp = proj.shape[0]
    in_specs, hb, _, _ = _dn_pre_specs(tp, n_heads, q_off)

    def body(q_ref, k_ref, v_ref, ba_ref, cq_ref, ck_ref, cv_ref, al_ref, dt_ref, *out_refs):
        outs = dn_pre_fn(q_ref[...], k_ref[...], v_ref[...], ba_ref[...], cq_ref[...], ck_ref[...], cv_ref[...],
                         al_ref[...], dt_ref[...], pl.program_id(0), n_heads)
        for o_ref, val in zip(out_refs, outs):
            o_ref[...] = val

    return _pc(body, name="dn_pre_fwd", grid=(n_heads,), in_specs=in_specs, out_specs=[hb(0)] * 5,
               out_shape=[jax.ShapeDtypeStruct((tp, n_heads * HEAD_DIM), f32)] * 5,
               compiler_params=_params("parallel"))(proj, proj, proj, ba, conv_w, conv_w, conv_w, a_log, dt_bias)


def dn_pre_bwd(proj, ba, conv_w, a_log, dt_bias, cots, n_heads, q_off):
    tp = proj.shape[0]
    in_specs, hb, cw, whole = _dn_pre_specs(tp, n_heads, q_off)

    def body(q_ref, k_ref, v_ref, ba_ref, cq_ref, ck_ref, cv_ref, al_ref, dt_ref, dq_ref, dk_ref, dv_ref, db_ref, dg_ref,
             dqr_ref, dkr_ref, dvr_ref, dba_ref, dcq_ref, dck_ref, dcv_ref, dal_ref, ddt_ref):
        head = pl.program_id(0)
        fn = lambda *args: dn_pre_fn(*args, head, n_heads)
        _, vjp = jax.vjp(fn, q_ref[...], k_ref[...], v_ref[...], ba_ref[...], cq_ref[...], ck_ref[...], cv_ref[...],
                         al_ref[...], dt_ref[...])
        dqr, dkr, dvr, dba, dcq, dck, dcv, dal, ddt = vjp((dq_ref[...], dk_ref[...], dv_ref[...], db_ref[...], dg_ref[...]))
        dqr_ref[...] = dqr.astype(bf16)
        dkr_ref[...] = dkr.astype(bf16)
        dvr_ref[...] = dvr.astype(bf16)
        dcq_ref[...] = dcq
        dck_ref[...] = dck
        dcv_ref[...] = dcv

        @pl.when(head == 0)
        def _():
            dba_ref[...] = jnp.zeros_like(dba_ref)
            dal_ref[...] = jnp.zeros_like(dal_ref)
            ddt_ref[...] = jnp.zeros_like(ddt_ref)

        dba_ref[...] += dba
        dal_ref[...] += dal
        ddt_ref[...] += ddt

    w = n_heads * HEAD_DIM
    return _pc(body, name="dn_pre_bwd", grid=(n_heads,), in_specs=in_specs + [hb(0)] * 5,
               out_specs=[hb(0)] * 3 + [whole((tp, LANES)), cw(0), cw(0), cw(0), whole((1, LANES)), whole((1, LANES))],
               out_shape=[jax.ShapeDtypeStruct((tp, w), bf16)] * 3 + [jax.ShapeDtypeStruct((tp, LANES), f32)]
               + [jax.ShapeDtypeStruct((CONV_WIDTH, w), f32)] * 3 + [jax.ShapeDtypeStruct((1, LANES), f32)] * 2,
               compiler_params=_params("arbitrary"))(proj, proj, proj, ba, conv_w, conv_w, conv_w, a_log, dt_bias, *cots)


def _super_rows(tp):
    n = tp // CHUNK
    return CHUNK * max(j for j in (4, 3, 2, 1) if n % j == 0)


def _heads_per_step(n_heads):
    return max(j for j in (4, 2, 1) if n_heads % j == 0)


def dn_intra_fwd(q, k, v, beta_b, g_b, n_heads):
    tp = q.shape[0]
    rows = _super_rows(tp)
    ns = tp // rows

    hps = _heads_per_step(n_heads)

    def body(q_ref, k_ref, v_ref, b_ref, g_ref, u_ref, w_ref, qd_ref, kd_ref, qk_ref, gl_ref, t_ref):
        lanes = [slice(i * HEAD_DIM, (i + 1) * HEAD_DIM) for i in range(hps)]
        gates = [gate_fn(g_ref[:, sl]) for sl in lanes]
        tmats = tri_inv([lmat_fn(k_ref[:, sl], b_ref[:, sl], gt[0]) for sl, gt in zip(lanes, gates)])
        for i, (sl, (decay, eg, kfac, gl), tmat) in enumerate(zip(lanes, gates, tmats)):
            u_c, w_c, q_dec, k_dec, qk_c = intra_fn(tmat, q_ref[:, sl], k_ref[:, sl], v_ref[:, sl], b_ref[:, sl], decay, eg, kfac)
            u_ref[:, sl] = u_c
            w_ref[:, sl] = w_c
            qd_ref[:, sl] = q_dec
            kd_ref[:, sl] = k_dec
            qk_ref[:, sl] = qk_c
            gl_ref[:, sl] = gl
            t_ref[i, 0] = tmat

    blk = pl.BlockSpec((rows, hps * HEAD_DIM), lambda h, s: (s, h))
    return _pc(body, name="dn_intra_fwd", grid=(n_heads // hps, ns), in_specs=[blk] * 5,
               out_specs=[blk] * 6 + [pl.BlockSpec((hps, 1, rows, rows), lambda h, s: (h, s, 0, 0))],
               out_shape=[jax.ShapeDtypeStruct(q.shape, f32)] * 6 + [jax.ShapeDtypeStruct((n_heads, ns, rows, rows), f32)],
               compiler_params=_params("parallel", "parallel"))(q, k, v, beta_b, g_b)


def dn_intra_bwd(q, k, v, beta_b, g_b, tmats, cots, n_heads):
    tp = q.shape[0]
    rows = _super_rows(tp)
    ns = tp // rows

    hps = _heads_per_step(n_heads)

    def body(q_ref, k_ref, v_ref, b_ref, g_ref, t_ref, du_ref, dw_ref, dqd_ref, dkd_ref, dqk_ref, dgl_ref,
             dq_ref, dk_ref, dv_ref, db_ref, dg_ref):
        lanes = [slice(i * HEAD_DIM, (i + 1) * HEAD_DIM) for i in range(hps)]
        tmats = [t_ref[i, 0] for i in range(hps)]
        gates = [jax.vjp(gate_fn, g_ref[:, sl]) for sl in lanes]
        intra = [jax.vjp(intra_fn, tmat, q_ref[:, sl], k_ref[:, sl], v_ref[:, sl], b_ref[:, sl], gt[0][0], gt[0][1], gt[0][2])[1](
            (du_ref[:, sl], dw_ref[:, sl], dqd_ref[:, sl], dkd_ref[:, sl], dqk_ref[:, sl]))
            for sl, tmat, gt in zip(lanes, tmats, gates)]
        tts = [tmat.T for tmat in tmats]
        dls = [mm3(tt, res[0]) for tt, res in zip(tts, intra)]
        dls = [-mm3(dl, tt) for dl, tt in zip(dls, tts)]
        for sl, gt, res, dl in zip(lanes, gates, intra, dls):
            _, dq, dk, dv, db, ddecay, deg, dkfac = res
            dk2, db2, ddecay2 = jax.vjp(lmat_fn, k_ref[:, sl], b_ref[:, sl], gt[0][0])[1](dl)
            (dg,) = gt[1]((ddecay + ddecay2, deg, dkfac, dgl_ref[:, sl]))
            dq_ref[:, sl] = dq
            dk_ref[:, sl] = dk + dk2
            dv_ref[:, sl] = dv
            db_ref[:, sl] = db + db2
            dg_ref[:, sl] = dg

    blk = pl.BlockSpec((rows, hps * HEAD_DIM), lambda h, s: (s, h))
    return _pc(body, name="dn_intra_bwd", grid=(n_heads // hps, ns),
               in_specs=[blk] * 5 + [pl.BlockSpec((hps, 1, rows, rows), lambda h, s: (h, s, 0, 0))] + [blk] * 6,
               out_specs=[blk] * 5, out_shape=[jax.ShapeDtypeStruct(q.shape, f32)] * 5,
               compiler_params=_params("parallel", "parallel"))(q, k, v, beta_b, g_b, tmats, *cots)


def dn_seq_fwd(inter, proj, dn_norm_w, n_heads, zd_off):
    tp, width = inter[0].shape
    n_chunks = tp // CHUNK

    def body(u_ref, w_ref, qd_ref, kd_ref, qk_ref, gl_ref, z_ref, nw_ref, y_ref, s_ref, state):
        @pl.when(pl.program_id(0) == 0)
        def _():
            state[...] = jnp.zeros_like(state)

        lanes = [slice(h * HEAD_DIM, (h + 1) * HEAD_DIM) for h in range(n_heads)]
        sts = [state[h] for h in range(n_heads)]
        for h, st in enumerate(sts):
            s_ref[0, h] = st
        v_new = [u_ref[:, sl] - mm_nn(w_ref[:, sl], st) for sl, st in zip(lanes, sts)]
        outs = [mm_nn(qd_ref[:, sl], st) + mm_nn(qk_ref[:, sl][:, :CHUNK], vn) for sl, st, vn in zip(lanes, sts, v_new)]
        for h, (sl, st, vn) in enumerate(zip(lanes, sts, v_new)):
            state[h] = st * gl_ref[0:1, sl] + mm_tn(kd_ref[:, sl], vn)
        for sl, o in zip(lanes, outs):
            y_ref[:, sl] = gated_norm(o, nw_ref[...], z_ref[:, sl]).astype(bf16)

    blk = pl.BlockSpec((CHUNK, width), lambda n: (n, 0))
    return _pc(body, name="dn_seq_fwd", grid=(n_chunks,),
               in_specs=[blk] * 6 + [pl.BlockSpec((CHUNK, width), lambda n: (n, zd_off)), pl.BlockSpec((1, HEAD_DIM), lambda n: (0, 0))],
               out_specs=[blk, pl.BlockSpec((1, n_heads, HEAD_DIM, HEAD_DIM), lambda n: (n, 0, 0, 0))],
               out_shape=[jax.ShapeDtypeStruct((tp, width), bf16), jax.ShapeDtypeStruct((n_chunks, n_heads, HEAD_DIM, HEAD_DIM), f32)],
               scratch_shapes=[pltpu.VMEM((n_heads, HEAD_DIM, HEAD_DIM), f32)],
               compiler_params=_params("arbitrary"))(*inter, proj, dn_norm_w)


def dn_seq_bwd(inter, proj, dn_norm_w, states, dy, n_heads, zd_off):
    tp, width = inter[0].shape
    n_chunks = tp // CHUNK
    last = n_chunks - 1

    def body(u_ref, w_ref, qd_ref, kd_ref, qk_ref, gl_ref, z_ref, nw_ref, s_ref, dy_ref,
             du_ref, dw_ref, dqd_ref, dkd_ref, dqk_ref, dgl_ref, dz_ref, dnw_ref, dstate):
        @pl.when(pl.program_id(0) == 0)
        def _():
            dstate[...] = jnp.zeros_like(dstate)
            dnw_ref[...] = jnp.zeros_like(dnw_ref)

        lanes = [slice(h * HEAD_DIM, (h + 1) * HEAD_DIM) for h in range(n_heads)]
        sts = [s_ref[0, h] for h in range(n_heads)]
        dsts = [dstate[h] for h in range(n_heads)]
        v_new = [u_ref[:, sl] - mm_nn(w_ref[:, sl], st) for sl, st in zip(lanes, sts)]
        outs = [mm_nn(qd_ref[:, sl], st) + mm_nn(qk_ref[:, sl][:, :CHUNK], vn) for sl, st, vn in zip(lanes, sts, v_new)]
        dnw = jnp.zeros((1, HEAD_DIM), f32)
        d_outs = []
        for sl, o in zip(lanes, outs):
            do, dn, dz = jax.vjp(gated_norm, o, nw_ref[...], z_ref[:, sl])[1](dy_ref[:, sl])
            dz_ref[:, sl] = dz.astype(bf16)
            dnw = dnw + dn
            d_outs.append(do)
        dnw_ref[...] += dnw
        d_vn = [mm_tn(qk_ref[:, sl][:, :CHUNK], do) + mm_nn(kd_ref[:, sl], ds) for sl, do, ds in zip(lanes, d_outs, dsts)]
        zeros = jnp.zeros((HEAD_DIM - CHUNK, HEAD_DIM), f32)
        rows = lax.broadcasted_iota(jnp.int32, (CHUNK, HEAD_DIM), 0)
        for h, (sl, st, vn, do, ds, dvn) in enumerate(zip(lanes, sts, v_new, d_outs, dsts, d_vn)):
            du_ref[:, sl] = dvn
            dw_ref[:, sl] = -mm_nt(dvn, st)
            dqd_ref[:, sl] = mm_nt(do, st)
            dkd_ref[:, sl] = mm_nt(vn, ds)
            dqk_ref[:, sl] = mm_nt(do, jnp.concatenate([vn, zeros], axis=0))
            dgl_ref[:, sl] = jnp.where(rows == 0, jnp.sum(st * ds, axis=0, keepdims=True), 0.0)
            dstate[h] = ds * gl_ref[0:1, sl] + mm_tn(qd_ref[:, sl], do) - mm_tn(w_ref[:, sl], dvn)

    blk = pl.BlockSpec((CHUNK, width), lambda n: (last - n, 0))
    return _pc(body, name="dn_seq_bwd", grid=(n_chunks,),
               in_specs=[blk] * 6 + [pl.BlockSpec((CHUNK, width), lambda n: (last - n, zd_off)), pl.BlockSpec((1, HEAD_DIM), lambda n: (0, 0)),
                         pl.BlockSpec((1, n_heads, HEAD_DIM, HEAD_DIM), lambda n: (last - n, 0, 0, 0)), blk],
               out_specs=[blk] * 7 + [pl.BlockSpec((1, HEAD_DIM), lambda n: (0, 0))],
               out_shape=[jax.ShapeDtypeStruct((tp, width), f32)] * 6 + [jax.ShapeDtypeStruct((tp, width), bf16),
                                                                          jax.ShapeDtypeStruct((1, HEAD_DIM), f32)],
               scratch_shapes=[pltpu.VMEM((n_heads, HEAD_DIM, HEAD_DIM), f32)],
               compiler_params=_params("arbitrary"))(*inter, proj, dn_norm_w, states, dy)


def loss_stage(out, final_w, target):
    tp, d = out.shape
    n_tiles = tp // CHUNK

    def body(o_ref, w_ref, t_ref, loss_ref, do_ref, dob_ref, dw_ref):
        i = pl.program_id(0)

        @pl.when(i == 0)
        def _():
            loss_ref[...] = jnp.zeros_like(loss_ref)
            dw_ref[...] = jnp.zeros_like(dw_ref)

        scored = (i > 0).astype(f32)
        val, (do, dw) = jax.value_and_grad(lambda o, w: scored * loss_fn(o, w, t_ref[...]), argnums=(0, 1))(o_ref[...], w_ref[...])
        loss_ref[...] += jnp.full(loss_ref.shape, val, f32)
        do_ref[...] = do
        dob_ref[...] = do.astype(bf16)
        dw_ref[...] += dw

    row = pl.BlockSpec((CHUNK, d), lambda i: (i, 0))
    vec = pl.BlockSpec((1, d), lambda i: (0, 0))
    return _pc(body, name="loss_stage", grid=(n_tiles,),
               in_specs=[row, vec, pl.BlockSpec((CHUNK, d), lambda i: (jnp.maximum(i - 1, 0), 0))],
               out_specs=[pl.BlockSpec((1, LANES), lambda i: (0, 0)), row, row, vec],
               out_shape=[jax.ShapeDtypeStruct((1, LANES), f32), jax.ShapeDtypeStruct((tp, d), f32),
                          jax.ShapeDtypeStruct((tp, d), bf16), jax.ShapeDtypeStruct((1, d), f32)],
               compiler_params=_params("arbitrary"))(out, final_w, target)


def _adam_update(w, g, m, v):
    nm = ADAM_B1 * m + (1.0 - ADAM_B1) * g
    nv = ADAM_B2 * v + (1.0 - ADAM_B2) * jnp.square(g)
    m_hat = nm / (1.0 - ADAM_B1 ** ADAM_STEP)
    v_hat = nv / (1.0 - ADAM_B2 ** ADAM_STEP)
    return -ADAM_LR * (m_hat / (jnp.sqrt(v_hat) + ADAM_EPS) + ADAM_WD * w), nm, nv


def adamw(name, w, g, m, v):
    rows, cols = w.shape
    t = _row_tile(rows, cols, 7)

    def body(w_ref, g_ref, m_ref, v_ref, d_ref, nm_ref, nv_ref):
        d_ref[...], nm_ref[...], nv_ref[...] = _adam_update(w_ref[...], g_ref[...], m_ref[...], v_ref[...])

    blk = pl.BlockSpec((t, cols), lambda i: (i, 0))
    return _pc(body, name=name, grid=(rows // t,), in_specs=[blk] * 4, out_specs=[blk] * 3,
               out_shape=[jax.ShapeDtypeStruct(w.shape, f32)] * 3, compiler_params=_params("parallel"))(w, g, m, v)


def adamw_joined(name, w, g_mine, g_theirs, m, v, half, axis):
    rows, cols = w.shape
    hr, hc = g_mine.shape
    tr, tc = _tile(hr, hc, 9)
    nr, nc = hr // tr, hc // tc

    def body(half_ref, w_ref, gm_ref, gt_ref, m_ref, v_ref, g_ref, d_ref, nm_ref, nv_ref):
        pos = pl.program_id(axis) // (nr if axis == 0 else nc)
        g = jnp.where(pos == half_ref[0], gm_ref[...], gt_ref[...])
        delta, nm, nv = _adam_update(w_ref[...], g, m_ref[...], v_ref[...])
        g_ref[...] = g
        d_ref[...] = delta
        nm_ref[...] = nm
        nv_ref[...] = nv

    whole = pl.BlockSpec((tr, tc), lambda i, j, hf: (i, j))
    part = pl.BlockSpec((tr, tc), lambda i, j, hf: (i % nr, j % nc))
    grid_spec = pltpu.PrefetchScalarGridSpec(num_scalar_prefetch=1, grid=(rows // tr, cols // tc),
                                             in_specs=[whole, part, part, whole, whole], out_specs=[whole] * 4)
    return _pc(body, name=name, grid_spec=grid_spec, out_shape=[jax.ShapeDtypeStruct(w.shape, f32)] * 4,
               compiler_params=_params("parallel", "parallel"))(half, w, g_mine, g_theirs, m, v)


def add_halves(name, g, recv, half, kind):
    s, r, c = recv.shape
    tr, tc = _tile(r, c, 3)
    nc = c // tc

    def body(half_ref, g_ref, r_ref, o_ref):
        o_ref[...] = (g_ref[...].reshape(r_ref.shape).astype(f32) + r_ref[...].astype(f32)).astype(bf16)

    if kind == "lead":
        g_spec = pl.BlockSpec((1, 1, tr, tc), lambda i, j, k, hf: (hf[0], i, j, k))
    else:
        g_spec = pl.BlockSpec((1, tr, tc), lambda i, j, k, hf: (i, j, hf[0] * nc + k))
    blk = pl.BlockSpec((1, tr, tc), lambda i, j, k, hf: (i, j, k))
    grid_spec = pltpu.PrefetchScalarGridSpec(num_scalar_prefetch=1, grid=(s, r // tr, nc), in_specs=[g_spec, blk], out_specs=blk)
    return _pc(body, name=name, grid_spec=grid_spec, out_shape=jax.ShapeDtypeStruct((s, r, c), bf16),
               compiler_params=_params("parallel", "parallel", "parallel"))(half, g, recv)


def add_first(name, parts, got, chips):
    _, r, c = parts.shape
    tr, tc = _tile(r, c, 6)

    def body(chips_ref, mine_ref, theirs_ref, got_ref, keep_ref, pass_ref):
        keep_ref[...] = mine_ref[0].astype(f32) + got_ref[0].astype(f32)
        pass_ref[...] = (theirs_ref[0].astype(f32) + got_ref[1].astype(f32)).astype(bf16)

    blk = pl.BlockSpec((tr, tc), lambda i, j, ch: (i, j))
    grid_spec = pltpu.PrefetchScalarGridSpec(
        num_scalar_prefetch=1, grid=(r // tr, c // tc),
        in_specs=[pl.BlockSpec((1, tr, tc), lambda i, j, ch: (ch[0], i, j)), pl.BlockSpec((1, tr, tc), lambda i, j, ch: (ch[1], i, j)),
                  pl.BlockSpec((2, tr, tc), lambda i, j, ch: (0, i, j))],
        out_specs=[blk, blk])
    return _pc(body, name=name, grid_spec=grid_spec, out_shape=[jax.ShapeDtypeStruct((r, c), f32), jax.ShapeDtypeStruct((r, c), bf16)],
               compiler_params=_params("parallel", "parallel"))(chips, parts, parts, got)


def add_second(name, kept, got):
    r, c = kept.shape
    tr, tc = _tile(r, c, 3)

    def body(k_ref, g_ref, o_ref):
        o_ref[...] = k_ref[...] + g_ref[...].astype(f32)

    blk = pl.BlockSpec((tr, tc), lambda i, j: (i, j))
    return _pc(body, name=name, grid=(r // tr, c // tc), in_specs=[blk, blk], out_specs=blk,
               out_shape=jax.ShapeDtypeStruct((r, c), f32), compiler_params=_params("parallel", "parallel"))(kept, got)


def sum_leading(name, x, out_dtype=f32):
    s, r, c = x.shape
    tr, tc = _tile(r, c, s + 1)

    def body(x_ref, o_ref):
        acc = x_ref[0].astype(f32)
        for i in range(1, s):
            acc = acc + x_ref[i].astype(f32)
        o_ref[...] = acc.astype(out_dtype)

    return _pc(body, name=name, grid=(r // tr, c // tc), in_specs=[pl.BlockSpec((s, tr, tc), lambda i, j: (0, i, j))],
               out_specs=pl.BlockSpec((tr, tc), lambda i, j: (i, j)), out_shape=jax.ShapeDtypeStruct((r, c), out_dtype),
               compiler_params=_params("parallel", "parallel"))(x)


def _place():
    x, y, c = lax.axis_index("x"), lax.axis_index("y"), lax.axis_index("c")
    return x, y, c


def _route(x, y, c):
    return (x ^ (1 - c), y ^ c), (x ^ c, y ^ (1 - c)), (1 - x, 1 - y)


def _half_view(ref, half, kind, lead=()):
    if kind == "lead":
        return ref.at[(*lead, half)]
    width = ref.shape[-1] // 2
    return ref.at[(*lead, *([slice(None)] * (len(ref.shape) - len(lead) - 1)), pl.ds(half * width, width))]


def _gather_plan(ins, outs, sems, kinds):
    n = len(ins)
    send1, recv1, send2, recv2 = sems
    x, y, c = _place()
    chip = 2 * x + y
    sibling = (x, y, 1 - c)
    near, far, diag = _route(x, y, c)
    near_id, far_id, diag_id = [2 * cx + cy for cx, cy in (near, far, diag)]

    def remote(src, dst, s_sem, r_sem, to):
        return pltpu.make_async_remote_copy(src_ref=src, dst_ref=dst, send_sem=s_sem, recv_sem=r_sem, device_id=to, device_id_type=MESH)

    def slab(a, chip_id, half):
        return _half_view(outs[a], half, kinds[a], (chip_id,))

    def landed(a, chip_id, sem, frm):
        return remote(slab(a, chip_id, c), slab(a, chip_id, c), send1.at[a, sem], recv1.at[a, sem], (*frm, c))

    def onward(a, chip_id, sem):
        return remote(slab(a, chip_id, c), slab(a, chip_id, c), send2.at[a, sem], recv2.at[a, sem], sibling)

    own = [remote(_half_view(ins[a], c, kinds[a]), slab(a, chip, c), send1.at[a, j], recv1.at[a, j], (*to, c))
           for a in range(n) for j, to in enumerate((near, far))]
    relay = [remote(slab(a, near_id, c), slab(a, near_id, c), send1.at[a, 2], recv1.at[a, 2], (*far, c)) for a in range(n)]
    to_sibling = [[onward(a, cid, sem) for sem, cid in enumerate((near_id, far_id, diag_id))] for a in range(n)]
    from_sibling = [remote(slab(a, cid, 1 - c), slab(a, cid, 1 - c), send2.at[a, sem], recv2.at[a, sem], sibling)
                    for a in range(n) for sem, cid in enumerate((far_id, near_id, diag_id))]

    def start_own():
        for cp in own:
            cp.start()

    def pass_near():
        for a in range(n):
            landed(a, near_id, 0, near).wait_recv()
            relay[a].start()
            to_sibling[a][0].start()

    def pass_far():
        for sem, cid in ((1, far_id), (2, diag_id)):
            for a in range(n):
                landed(a, cid, sem, far).wait_recv()
                to_sibling[a][sem].start()

    def finish():
        for cp in from_sibling:
            cp.wait_recv()
        for cp in own + relay + [cp for row in to_sibling for cp in row]:
            cp.wait_send()

    return [start_own, pass_near, pass_far, finish]


def _gather_shapes(shards):
    return ([jax.ShapeDtypeStruct((4,) + s.shape, s.dtype) for s in shards], [pltpu.SemaphoreType.DMA((len(shards), 3))] * 4)


def gather_weights(shards, kinds):
    n = len(shards)

    def body(*refs):
        for emit in _gather_plan(refs[:n], refs[n:2 * n], refs[2 * n:], kinds):
            emit()

    out_shapes, sems = _gather_shapes(shards)
    return _pc(body, name="gather_weights", in_specs=[_ANY] * n, out_specs=[_ANY] * n, out_shape=out_shapes, scratch_shapes=sems)(*shards)


def hosted_gather(shards, kinds):
    out_shapes, sems = _gather_shapes(shards)
    return Hosted(shards, out_shapes, sems, lambda i, o, s: list(zip((0.0, 0.45, 0.8, 1.0), _gather_plan(i, o, s, kinds))))


def swap_with_sibling(name, sends):
    n = len(sends)

    def body(*refs):
        ins, outs = refs[:n], refs[n:2 * n]
        send, recv = refs[2 * n:]
        x, y, c = _place()
        cps = [pltpu.make_async_remote_copy(src_ref=ins[a], dst_ref=outs[a], send_sem=send.at[a], recv_sem=recv.at[a],
                                            device_id=(x, y, 1 - c), device_id_type=MESH) for a in range(n)]
        for cp in cps:
            cp.start()
        for cp in cps:
            cp.wait()

    return _pc(body, name=name, in_specs=[_ANY] * n, out_specs=[_ANY] * n,
               out_shape=[jax.ShapeDtypeStruct(s.shape, s.dtype) for s in sends],
               scratch_shapes=[pltpu.SemaphoreType.DMA((n,))] * 2)(*sends)


def send_grad_halves(grads, kinds):
    n = len(grads)

    def body(*refs):
        ins, outs = refs[:n], refs[n:2 * n]
        send, recv = refs[2 * n:]
        x, y, c = _place()
        cps = [pltpu.make_async_remote_copy(src_ref=_half_view(ins[a], 1 - c, kinds[a]), dst_ref=outs[a], send_sem=send.at[a],
                                            recv_sem=recv.at[a], device_id=(x, y, 1 - c), device_id_type=MESH) for a in range(n)]
        for cp in cps:
            cp.start()
        for cp in cps:
            cp.wait()

    shape = lambda g, kind: g.shape[1:] if kind == "lead" else g.shape[:-1] + (g.shape[-1] // 2,)
    return _pc(body, name="send_grad_halves", in_specs=[_ANY] * n, out_specs=[_ANY] * n,
               out_shape=[jax.ShapeDtypeStruct(shape(g, k), g.dtype) for g, k in zip(grads, kinds)],
               scratch_shapes=[pltpu.SemaphoreType.DMA((n,))] * 2)(*grads)


def _scatter_first_plan(ins, outs, sems):
    n = len(ins)
    send, recv = sems
    x, y, c = _place()
    near, _, diag = _route(x, y, c)
    cps = [pltpu.make_async_remote_copy(src_ref=ins[a].at[2 * cx + cy], dst_ref=outs[a].at[j], send_sem=send.at[a, j],
                                        recv_sem=recv.at[a, j], device_id=(*near, c), device_id_type=MESH)
           for a in range(n) for j, (cx, cy) in enumerate((near, diag))]

    def start():
        for cp in cps:
            cp.start()

    def wait():
        for cp in cps:
            cp.wait()

    return [start, wait]


def _scatter_first_shapes(parts):
    return ([jax.ShapeDtypeStruct((2,) + p.shape[1:], p.dtype) for p in parts], [pltpu.SemaphoreType.DMA((len(parts), 2))] * 2)


def scatter_first(parts):
    n = len(parts)

    def body(*refs):
        for emit in _scatter_first_plan(refs[:n], refs[n:2 * n], refs[2 * n:]):
            emit()

    out_shapes, sems = _scatter_first_shapes(parts)
    return _pc(body, name="scatter_first", in_specs=[_ANY] * n, out_specs=[_ANY] * n, out_shape=out_shapes, scratch_shapes=sems)(*parts)


def hosted_scatter_first(parts):
    out_shapes, sems = _scatter_first_shapes(parts)
    return Hosted(parts, out_shapes, sems, lambda i, o, s: list(zip((0.0, 1.0), _scatter_first_plan(i, o, s))))


def scatter_second(parts):
    n = len(parts)

    def body(*refs):
        ins, outs = refs[:n], refs[n:2 * n]
        send, recv = refs[2 * n:]
        x, y, c = _place()
        _, far, _ = _route(x, y, c)
        cps = [pltpu.make_async_remote_copy(src_ref=ins[a], dst_ref=outs[a], send_sem=send.at[a], recv_sem=recv.at[a],
                                            device_id=(*far, c), device_id_type=MESH) for a in range(n)]
        for cp in cps:
            cp.start()
        for cp in cps:
            cp.wait()

    return _pc(body, name="scatter_second", in_specs=[_ANY] * n, out_specs=[_ANY] * n,
               out_shape=[jax.ShapeDtypeStruct(p.shape, p.dtype) for p in parts],
               scratch_shapes=[pltpu.SemaphoreType.DMA((n,))] * 2)(*parts)


def gather_from_all(pack):
    def body(in_ref, out_ref, send, recv):
        x, y, c = _place()
        me = 4 * x + 2 * y + c
        flips = [(fx, fy, fc) for fx in (0, 1) for fy in (0, 1) for fc in (0, 1)][1:]
        peers = [(jnp.where(fx, 1 - x, x), jnp.where(fy, 1 - y, y), jnp.where(fc, 1 - c, c)) for fx, fy, fc in flips]
        cps = [pltpu.make_async_remote_copy(src_ref=in_ref, dst_ref=out_ref.at[me], send_sem=send.at[j], recv_sem=recv.at[j],
                                            device_id=p, device_id_type=MESH) for j, p in enumerate(peers)]
        for cp in cps:
            cp.start()
        for j, (px, py, pc) in enumerate(peers):
            slab = out_ref.at[4 * px + 2 * py + pc]
            pltpu.make_async_remote_copy(src_ref=slab, dst_ref=slab, send_sem=send.at[j], recv_sem=recv.at[j],
                                         device_id=(px, py, pc), device_id_type=MESH).wait_recv()
        for cp in cps:
            cp.wait_send()

    return _pc(body, name="gather_from_all", in_specs=[_ANY], out_specs=_ANY,
               out_shape=jax.ShapeDtypeStruct((8,) + pack.shape, pack.dtype),
               scratch_shapes=[pltpu.SemaphoreType.DMA((7,)), pltpu.SemaphoreType.DMA((7,))])(pack)


def _pack(arrays, width):
    rows = []
    for a in arrays:
        flat = a.reshape(-1).astype(f32)
        pad = (-flat.shape[0]) % (8 * width)
        rows.append(jnp.pad(flat, (0, pad)).reshape(-1, width))
    return jnp.concatenate(rows, axis=0)


def _unpack(pack, like, width):
    out, row = [], 0
    for a in like:
        size = 1
        for s in a.shape:
            size *= s
        n_rows = -(-size // (8 * width)) * 8
        out.append(pack[row:row + n_rows].reshape(-1)[:size].reshape(a.shape))
        row += n_rows
    return out


def _halves(a2d):
    r, c = a2d.shape
    return a2d.reshape(2, r // 2, c)


def local_step(x, meta, norm_w, w_main, w_ba, conv_w, a_log, dt_bias, pool_mix, pool_scale, dn_norm_w,
               late_weights, final_w, target, grad_hook=None):
    seq, d = x.shape
    pw = pool_scale.shape[1]
    dw = conv_w.shape[1] // 3
    n_heads = dw // HEAD_DIM
    tp = FRONT_PAD + N_META + seq
    hp = jnp.concatenate([jnp.zeros((FRONT_PAD, d), f32), meta, x], axis=0)
    tm_big = tp // 2 if tp % 32 == 0 else tp
    tm_norm = max(t for t in range(16, min(tp, 352) + 1, 16) if tp % t == 0)
    tile = min(512, d)
    off_q = 2 * pw
    off_zd = off_q + 3 * dw
    off_gp = off_zd + dw
    off_gd = off_gp + d
    a_log128 = jnp.pad(a_log, ((0, 0), (0, LANES - n_heads)))
    dt128 = jnp.pad(dt_bias, ((0, 0), (0, LANES - n_heads)))

    xn = norm_fwd(hp, norm_w, tm_norm)
    if isinstance(late_weights[0], Hosted):
        proj, *fetched = matmul("proj", xn, w_main, tb=True, tm=tp, tn=tile, hosted=late_weights[0])
        w_pool_out, w_dn_out, w_o = late_weights[1](fetched)
    else:
        proj = matmul("proj", xn, w_main, tb=True, tm=tp, tn=tile)
        w_pool_out, w_dn_out, w_o = late_weights
    ba = matmul("proj_ba", xn, w_ba, tb=True, tm=tp, tn=LANES)
    y_pool = pool_fwd(proj, pool_mix, pool_scale, pw)
    q, k, v, beta_b, g_b = dn_pre_fwd(proj, ba, conv_w, a_log128, dt128, n_heads, off_q // HEAD_DIM)
    assert off_zd % dw == 0
    *inter, tmats = dn_intra_fwd(q, k, v, beta_b, g_b, n_heads)
    y_dn, states = dn_seq_fwd(inter, proj, dn_norm_w, n_heads, off_zd // dw)
    a_mat = matmul("pool_out", y_pool, w_pool_out, tm=tp, tn=tile)

    def merge(acc, a_t, gp_t, gd_t):
        return acc, sigmoid(gp_t) * a_t + sigmoid(gd_t) * acc

    b_mat, merged = matmul("dn_out_merge", y_dn, w_dn_out, tm=tm_big, tn=tile, extras=[(a_mat, 0), (proj, off_gp), (proj, off_gd)],
                           epi=merge, out_dtypes=(f32, bf16))
    out = matmul("out_proj", merged, w_o, tm=tm_big, tn=tile, extras=[(hp, 0)], epi=lambda acc, h_t: (acc + h_t,))
    loss, dout, dout_b, dfinal_w = loss_stage(out, final_w, target)

    def unmerge(dm, a_t, b_t, gp_t, gd_t):
        sp, sd = sigmoid(gp_t), sigmoid(gd_t)
        return dm * sp, dm * sd, dm * a_t * sp * (1.0 - sp), dm * b_t * sd * (1.0 - sd)

    d_a, d_b, d_gp, d_gd = matmul("d_merged", dout_b, w_o, tb=True, tm=tm_big, tn=tile,
                                  extras=[(a_mat, 0), (b_mat, 0), (proj, off_gp), (proj, off_gd)], epi=unmerge,
                                  out_dtypes=(bf16,) * 4)
    g_w_o = matmul("g_w_o", merged, dout_b, ta=True, tm=tile, tn=tile, out_dtypes=(bf16,))
    d_y_pool = matmul("d_y_pool", d_a, w_pool_out, tb=True, tm=tp, tn=tile)
    g_w_pool_out = matmul("g_w_pool_out", y_pool, d_a, ta=True, tm=tile, tn=tile, out_dtypes=(bf16,))
    d_y_dn = matmul("d_y_dn", d_b, w_dn_out, tb=True, tm=tp, tn=tile)
    g_w_dn_out = matmul("g_w_dn_out", y_dn, d_b, ta=True, tm=tile, tn=tile, out_dtypes=(bf16,))
    d_u, d_zp, g_pool_mix, g_pool_scale = pool_bwd(proj, pool_mix, pool_scale, d_y_pool, pw)
    *d_inter, d_zd, g_dn_norm_w = dn_seq_bwd(inter, proj, dn_norm_w, states, d_y_dn, n_heads, off_zd // dw)
    d_q, d_k, d_v, d_beta, d_g = dn_intra_bwd(q, k, v, beta_b, g_b, tmats, d_inter, n_heads)
    d_qr, d_kr, d_vr, d_ba, g_cq, g_ck, g_cv, g_a_log, g_dt = dn_pre_bwd(
        proj, ba, conv_w, a_log128, dt128, (d_q, d_k, d_v, d_beta, d_g), n_heads, off_q // HEAD_DIM)
    d_proj = jnp.concatenate([d_u, d_zp, d_qr, d_kr, d_vr, d_zd, d_gp, d_gd], axis=1)
    d_ba_b = cast_bf16("cast_d_ba", d_ba)
    g_w_main = matmul("g_w_main", d_proj, xn, ta=True, tm=tile, tn=tile, out_dtypes=(bf16,))
    g_w_ba = matmul("g_w_ba", d_ba_b, xn, ta=True, tm=LANES, tn=tile, out_dtypes=(bf16,))
    hosted, carried = grad_hook(g_w_main, g_w_ba, g_w_pool_out, g_w_dn_out, g_w_o, g_pool_mix) if grad_hook else (None, None)
    dxn_ba = matmul("dxn_ba", d_ba_b, w_ba, tm=tp, tn=tile)
    n_cols = d_proj.shape[1]
    tk_dxn = max(t for t in range(LANES, min(2048, n_cols) + 1, LANES) if n_cols % t == 0)
    res = matmul("dxn", d_proj, w_main, tm=tm_big, tn=tile, tk=tk_dxn, extras=[(dxn_ba, 0)], epi=lambda acc, e: (acc + e,), hosted=hosted)
    dxn, landed = (res[0], list(res[1:])) if hosted else (res, [])
    dh, g_norm_w = norm_bwd(hp, norm_w, dxn, dout, tm_norm)
    g_conv = jnp.concatenate([g_cq, g_ck, g_cv], axis=1)
    return (loss, dh, g_norm_w, g_w_main, g_w_ba, g_conv, g_a_log[:, :n_heads], g_dt[:, :n_heads], g_pool_mix, g_pool_scale,
            g_dn_norm_w, g_w_pool_out, g_w_dn_out, g_w_o, dfinal_w, (carried, landed))


def kernel(x, meta_tokens, norm_w, w_in, conv_w, A_log, dt_bias, pool_mix, pool_scale, dn_norm_w, w_pool_out, w_dn_out, w_o, final_norm_w, loss_target, m_meta_tokens, m_norm_w, m_w_in, m_conv_w, m_A_log, m_dt_bias, m_pool_mix, m_pool_scale, m_dn_norm_w, m_w_pool_out, m_w_dn_out, m_w_o, m_final_norm_w, v_meta_tokens, v_norm_w, v_w_in, v_conv_w, v_A_log, v_dt_bias, v_pool_mix, v_pool_scale, v_dn_norm_w, v_w_pool_out, v_w_dn_out, v_w_o, v_final_norm_w):
    d = x.shape[-1]
    pw = pool_scale.shape[-1]
    dw = w_dn_out.shape[1] * 4
    n_heads = dw // HEAD_DIM
    gdim = pw // POOL_GROUPS
    chip = 2 * lax.axis_index("x") + lax.axis_index("y")
    core = lax.axis_index("c")

    half = core.astype(jnp.int32).reshape(1)
    me = 4 * lax.axis_index("x") + 2 * lax.axis_index("y") + core

    w_in_t = w_in[0].T
    mix_s = pool_mix[0].reshape(POOL_GROUPS * (gdim // 4), gdim)
    small_s = _pack([meta_tokens, conv_w[0]], d)
    small_rows = small_s.shape[0]
    small_s = jnp.pad(small_s, ((0, (-small_rows) % 16), (0, 0)))
    sw = w_in_t.shape[0]
    n_main, n_ba = 2 * pw + 4 * dw, 2 * n_heads

    early = [cast_bf16("cast_w_in", w_in_t), _halves(cast_bf16("cast_mix", mix_s)), _halves(small_s)]
    late = [_halves(cast_bf16("cast_w_po", w_pool_out[0])), _halves(cast_bf16("cast_w_do", w_dn_out[0])),
            _halves(cast_bf16("cast_w_o", w_o[0]))]
    fill = lambda g, own: lax.dynamic_update_slice(g, own[None], (chip,) + (0,) * own.ndim)
    g_in, g_mix, g_small = [fill(g, own) for g, own in zip(gather_weights(early, ["cols", "lead", "lead"]), early)]

    def shard_rows(lo, hi):
        cut = [(max(lo, j * sw), min(hi, (j + 1) * sw), j) for j in range(4)]
        return [g_in[j, a - j * sw:b - j * sw] for a, b, j in cut if a < b]

    w_main = jnp.concatenate(shard_rows(0, n_main) + shard_rows(n_main + n_ba, 4 * sw), axis=0)
    w_ba = jnp.pad(jnp.concatenate(shard_rows(n_main, n_main + n_ba), axis=0), ((0, LANES - n_ba), (0, 0)))
    cat_cols = lambda g: jnp.concatenate([g[j].reshape(-1, g.shape[-1]) for j in range(4)], axis=1)

    def late_weights(fetched):
        g_po, g_do, g_o = [fill(g, own) for g, own in zip(fetched, late)]
        return cat_cols(g_po), g_do.reshape(-1, g_do.shape[-1]), g_o.reshape(-1, g_o.shape[-1])

    mix_full = g_mix.reshape(4, POOL_GROUPS, gdim // 4, gdim).transpose(1, 0, 2, 3).reshape(POOL_GROUPS, gdim, gdim)
    smalls = [_unpack(g_small[j].reshape(-1, d)[:small_rows], [meta_tokens, conv_w[0]], d) for j in range(4)]
    meta_full = jnp.concatenate([s[0] for s in smalls], axis=1)
    conv_full = jnp.concatenate([s[1] for s in smalls], axis=1)

    names = ["w_in", "w_po", "w_do", "w_o", "mix"]
    kinds = ["cols", "lead", "lead", "lead", "lead"]

    def grads_to_chips(g_w_main, g_w_ba, g_w_po, g_w_do, g_w_o_full, g_mix_full):
        def grad_rows(lo, hi):
            segs = [(0, n_main, g_w_main, 0), (n_main, n_main + n_ba, g_w_ba, 0), (n_main + n_ba, 4 * sw, g_w_main, n_main)]
            cut = [(max(lo, s0), min(hi, s1), s0, arr, off) for s0, s1, arr, off in segs]
            return [arr[a - s0 + off:b - s0 + off] for a, b, s0, arr, off in cut if a < b]

        in_parts = jnp.stack([jnp.concatenate(grad_rows(j * sw, (j + 1) * sw), axis=0) for j in range(4)])
        col_parts = lambda g: g.reshape(2, g.shape[0] // 2, 4, g.shape[1] // 4).transpose(0, 2, 1, 3)
        row_parts = lambda g: g.reshape(4, 2, g.shape[0] // 8, g.shape[1]).transpose(1, 0, 2, 3)
        mix_rows = POOL_GROUPS * (gdim // 4)
        mix_parts = (g_mix_full.astype(bf16).reshape(POOL_GROUPS, 4, gdim // 4, gdim).transpose(1, 0, 2, 3)
                     .reshape(4, 2, mix_rows // 2, gdim).transpose(1, 0, 2, 3))
        parts = [in_parts, col_parts(g_w_po), row_parts(g_w_do), row_parts(g_w_o_full), mix_parts]
        from_sibling = send_grad_halves(parts, kinds)
        chip_sums = [add_halves("add_" + nm, p, r, half, k) for nm, p, r, k in zip(names, parts, from_sibling, kinds)]
        return hosted_scatter_first(chip_sums), chip_sums

    (loss, dh, g_norm_w, _, _, g_conv, g_a_log, g_dt, _, g_pool_scale, g_dn_norm_w, _, _, _, g_final_w,
     (chip_sums, from_near)) = local_step(x[0], meta_full, norm_w, w_main, w_ba, conv_full, A_log, dt_bias, mix_full, pool_scale,
                                          dn_norm_w, (hosted_gather(late, ["lead"] * 3), late_weights), final_norm_w.reshape(1, d),
                                          loss_target[0], grad_hook=grads_to_chips)
    grad_x = dh[FRONT_PAD + N_META:][None]
    g_meta = dh[FRONT_PAD:FRONT_PAD + N_META]

    far_chip = 2 * (lax.axis_index("x") ^ core) + (lax.axis_index("y") ^ (1 - core))
    chips = jnp.stack([chip, far_chip]).astype(jnp.int32)
    kept, passed = zip(*[add_first("add1_" + nm, p, g, chips) for nm, p, g in zip(names, chip_sums, from_near)])
    mine = [add_second("add2_" + nm, k_, g) for nm, k_, g in zip(names, kept, scatter_second(list(passed)))]
    theirs = swap_with_sibling("swap_grad_halves", mine)

    small_like = [loss, g_norm_w, g_a_log, g_dt, g_pool_scale, g_dn_norm_w, g_final_w, g_conv, g_meta]
    pack = _pack(small_like, d)
    packs = lax.dynamic_update_slice(gather_from_all(pack), pack[None], (me, 0, 0))
    total = sum_leading("sum_small", packs)
    (loss_t, g_norm_w, g_a_log, g_dt, g_pool_scale, g_dn_norm_w, g_final_w, g_conv, g_meta) = _unpack(total, small_like, d)
    loss_out = loss_t[0, 0]
    g_conv_s = lax.dynamic_slice_in_dim(g_conv, chip * (g_conv.shape[1] // 4), g_conv.shape[1] // 4, axis=1)
    g_meta_s = lax.dynamic_slice_in_dim(g_meta, chip * (d // 4), d // 4, axis=1)

    weights = [meta_tokens, norm_w, w_in, conv_w, A_log, dt_bias, pool_mix, pool_scale, dn_norm_w, w_pool_out, w_dn_out, w_o, final_norm_w]
    ms = [m_meta_tokens, m_norm_w, m_w_in, m_conv_w, m_A_log, m_dt_bias, m_pool_mix, m_pool_scale, m_dn_norm_w, m_w_pool_out, m_w_dn_out, m_w_o, m_final_norm_w]
    vs = [v_meta_tokens, v_norm_w, v_w_in, v_conv_w, v_A_log, v_dt_bias, v_pool_mix, v_pool_scale, v_dn_norm_w, v_w_pool_out, v_w_dn_out, v_w_o, v_final_norm_w]
    grads = [g_meta_s, g_norm_w, None, g_conv_s[None], g_a_log, g_dt, None, g_pool_scale, g_dn_norm_w, None, None, None, g_final_w.reshape(d)]
    deltas, new_ms, new_vs = [None] * 13, [None] * 13, [None] * 13
    big = [2, 9, 10, 11, 6]
    for i, nm, g_mine, g_theirs in zip(big, names, mine, theirs):
        if nm == "w_in":
            to2d, back, axis = (lambda t: t[0].T), (lambda t: t.T[None]), 1
        else:
            to2d, back, axis = (lambda t: t.reshape(-1, t.shape[-1])), (lambda t, i=i: t.reshape(weights[i].shape)), 0
        res = adamw_joined("adamw_" + nm, to2d(weights[i]), g_mine, g_theirs, to2d(ms[i]), to2d(vs[i]), half, axis)
        grads[i], deltas[i], new_ms[i], new_vs[i] = [back(t) for t in res]
    small_idx = [i for i in range(13) if i not in big]
    packs = [_pack([arrs[i] for i in small_idx], d) for arrs in (weights, grads, ms, vs)]
    outs = adamw("adamw_small", *packs)
    like = [weights[i] for i in small_idx]
    for res, dest in zip(outs, (deltas, new_ms, new_vs)):
        for i, val in zip(small_idx, _unpack(res, like, d)):
            dest[i] = val
    return (loss_out, grad_x, *grads, *deltas, *new_ms, *new_vs)
```

```python
import functools

import jax
import jax.numpy as jnp
from jax import lax
from jax.experimental import pallas as pl
from jax.experimental.pallas import tpu as pltpu

f32 = jnp.float32
bf16 = jnp.bfloat16
HIGHEST = lax.Precision.HIGHEST
MESH = pl.DeviceIdType.MESH

N_META = 16
CHUNK = 64
FRONT_PAD = (-N_META) % CHUNK
HEAD_DIM = 128
POOL_GROUPS = 4
POOL_WINDOWS = (2, 4, 8, 16)
CONV_WIDTH = 4
NORM_EPS = 1e-6
ADAM_LR, ADAM_B1, ADAM_B2, ADAM_EPS, ADAM_WD, ADAM_STEP = 0.001, 0.9, 0.999, 1e-08, 0.01, 10
LANES = 128
VMEM_LIMIT = 56 * 2**20


def _pc(body, **kw):
    return pl.pallas_call(body, **kw)


def _params(*sem, **kw):
    return pltpu.CompilerParams(dimension_semantics=sem or None, vmem_limit_bytes=VMEM_LIMIT, **kw)


def _dg(a, b, dims, prec=None):
    return lax.dot_general(a, b, (dims, ((), ())), precision=prec, preferred_element_type=f32)


@jax.custom_vjp
def mm_nn(a, b):
    return _dg(a.astype(bf16), b.astype(bf16), ((1,), (0,)))


@jax.custom_vjp
def mm_nt(a, b):
    return _dg(a.astype(bf16), b.astype(bf16), ((1,), (1,)))


@jax.custom_vjp
def mm_tn(a, b):
    return _dg(a.astype(bf16).T, b.astype(bf16), ((1,), (0,)))


mm_nn.defvjp(lambda a, b: (mm_nn(a, b), (a, b)), lambda r, dy: (mm_nt(dy, r[1]), mm_tn(r[0], dy)))
mm_nt.defvjp(lambda a, b: (mm_nt(a, b), (a, b)), lambda r, dy: (mm_nn(dy, r[1]), mm_tn(dy, r[0])))
mm_tn.defvjp(lambda a, b: (mm_tn(a, b), (a, b)), lambda r, dy: (mm_nt(r[1], dy), mm_nn(r[0], dy)))


def _split3(x):
    hi = x.astype(bf16)
    r1 = x - hi.astype(f32)
    mid = r1.astype(bf16)
    lo = (r1 - mid.astype(f32)).astype(bf16)
    return hi, mid, lo


@jax.custom_vjp
def mm_sel(sel, x):
    s = sel.astype(bf16)
    d = ((1,), (0,))
    hi, mid, lo = _split3(x)
    return _dg(s, hi, d) + _dg(s, mid, d) + _dg(s, lo, d)


def _mm_sel_bwd(sel, dy):
    s = sel.astype(bf16)
    d = ((0,), (0,))
    hi, mid, lo = _split3(dy)
    return jnp.zeros_like(sel), _dg(s, hi, d) + _dg(s, mid, d) + _dg(s, lo, d)


mm_sel.defvjp(lambda sel, x: (mm_sel(sel, x), sel), _mm_sel_bwd)


@jax.custom_vjp
def mm_pick(x, sel):
    s = sel.astype(bf16)
    d = ((1,), (0,))
    hi, mid, lo = _split3(x)
    return _dg(hi, s, d) + _dg(mid, s, d) + _dg(lo, s, d)


def _mm_pick_bwd(sel, dy):
    s = sel.astype(bf16)
    d = ((1,), (1,))
    hi, mid, lo = _split3(dy)
    return _dg(hi, s, d) + _dg(mid, s, d) + _dg(lo, s, d), jnp.zeros_like(sel)


mm_pick.defvjp(lambda x, sel: (mm_pick(x, sel), sel), _mm_pick_bwd)


def tri_inv(ls):
    n = ls[0].shape[0]
    eye = (lax.broadcasted_iota(jnp.int32, (n, n), 0) == lax.broadcasted_iota(jnp.int32, (n, n), 1)).astype(f32)
    ms = [-l for l in ls]
    ts = [eye + m for m in ms]
    k = 1
    while 2 * k < CHUNK:
        ms = [mm_nn(m, m) for m in ms]
        ts = [t + mm_nn(t, m) for t, m in zip(ts, ms)]
        k *= 2
    return ts


@functools.partial(jax.custom_vjp, nondiff_argnums=(1,))
def shift_rows(x, j):
    n = x.shape[0]
    rows = lax.broadcasted_iota(jnp.int32, x.shape, 0)
    if j >= 0:
        return jnp.where(rows >= j, pltpu.roll(x, j, 0), 0.0)
    return jnp.where(rows < n + j, pltpu.roll(x, n + j, 0), 0.0)


shift_rows.defvjp(lambda x, j: (shift_rows(x, j), None), lambda j, _, dy: (shift_rows(dy, -j),))


def sigmoid(x):
    return 1.0 / (1.0 + jnp.exp(-x))


def silu(x):
    return x * sigmoid(x)


def softplus(x):
    return jnp.maximum(x, 0.0) + jnp.log(1.0 + jnp.exp(-jnp.abs(x)))


def rmsnorm(x, w):
    return x * lax.rsqrt(jnp.mean(x * x, axis=-1, keepdims=True) + NORM_EPS) * w


def l2norm(x):
    return x * lax.rsqrt(jnp.sum(x * x, axis=-1, keepdims=True) + NORM_EPS)


def pool_fn(u, zp, mix, scale, group):
    rows = lax.broadcasted_iota(jnp.int32, u.shape, 0)
    sums = []
    s, w = u, 1
    while w < POOL_WINDOWS[-1]:
        s = s + shift_rows(s, w)
        w *= 2
        sums.append(s)
    total = sums[-1]
    for gi in range(POOL_GROUPS - 2, -1, -1):
        total = jnp.where(group == gi, sums[gi], total)
    window = jnp.left_shift(2, group)
    cnt = jnp.clip(rows - (FRONT_PAD - 1), 1, window).astype(f32)
    pooled = total / cnt - u
    return mm_nn(pooled, mix) * scale * silu(zp)


def conv_silu(x, w):
    k = CONV_WIDTH
    y = x * w[k - 1:k, :]
    for kk in range(k - 1):
        y = y + shift_rows(x, k - 1 - kk) * w[kk:kk + 1, :]
    return silu(y)


def _lane_pick(row, idx):
    lanes = lax.broadcasted_iota(jnp.int32, row.shape, 1)
    return jnp.sum(jnp.where(lanes == idx, row, 0.0), axis=1, keepdims=True)


def dn_pre_fn(qr, kr, vr, ba, cwq, cwk, cwv, a_log, dt_bias, head, n_heads):
    q = l2norm(conv_silu(qr, cwq)) * (HEAD_DIM ** -0.5)
    k = l2norm(conv_silu(kr, cwk))
    v = conv_silu(vr, cwv)
    r = lax.broadcasted_iota(jnp.int32, (LANES, LANES), 0)
    b_b = mm_pick(ba, (r == head).astype(f32))
    a_b = mm_pick(ba, (r == head + n_heads).astype(f32))
    real = lax.broadcasted_iota(jnp.int32, qr.shape, 0) >= FRONT_PAD
    beta_b = jnp.where(real, sigmoid(b_b), 0.0)
    g_b = jnp.where(real, -jnp.exp(_lane_pick(a_log, head)) * softplus(a_b + _lane_pick(dt_bias, head)), 0.0)
    return q, k, v, beta_b, g_b


def _chunk_masks(rows):
    r = lax.broadcasted_iota(jnp.int32, (rows, rows), 0)
    c = lax.broadcasted_iota(jnp.int32, (rows, rows), 1)
    same = (r // CHUNK) == (c // CHUNK)
    return same, jnp.logical_and(same, r >= c), jnp.logical_and(same, r > c)


def _lane0(rows):
    return (lax.broadcasted_iota(jnp.int32, (rows, LANES), 1) == 0).astype(bf16)


@jax.custom_vjp
def lane0_as_row(x):
    sel = _lane0(x.shape[0])
    d = ((1,), (1,))
    hi, mid, lo = _split3(x)
    return _dg(sel, hi, d) + _dg(sel, mid, d) + _dg(sel, lo, d)


def _lane0_as_row_bwd(rows, dy):
    sel = _lane0(rows)
    d = ((0,), (0,))
    hi, mid, lo = _split3(dy)
    return (_dg(hi, sel, d) + _dg(mid, sel, d) + _dg(lo, sel, d),)


lane0_as_row.defvjp(lambda x: (lane0_as_row(x), x.shape[0]), _lane0_as_row_bwd)


def gate_fn(g_b):
    rows = g_b.shape[0]
    same, causal, _ = _chunk_masks(rows)
    gcum_b = mm_sel(causal.astype(f32), g_b)
    glast_b = mm_sel(same.astype(f32), g_b)
    g_rows = jnp.broadcast_to(gcum_b[:, :1], (rows, rows))
    decay = jnp.where(causal, jnp.exp(jnp.where(causal, g_rows - lane0_as_row(gcum_b), 0.0)), 0.0)
    return decay, jnp.exp(gcum_b), jnp.exp(glast_b - gcum_b), jnp.exp(glast_b)


def _fold_matrix(rows):
    r = lax.broadcasted_iota(jnp.int32, (rows, LANES), 0)
    c = lax.broadcasted_iota(jnp.int32, (rows, LANES), 1)
    return (r % CHUNK == c).astype(bf16)


@jax.custom_vjp
def fold_chunks(x):
    return _dg(x.astype(bf16), _fold_matrix(x.shape[0]), ((1,), (0,)))


def _fold_chunks_bwd(rows, dy):
    fold = _fold_matrix(rows)
    d = ((1,), (1,))
    hi, mid, lo = _split3(dy)
    return (_dg(hi, fold, d) + _dg(mid, fold, d) + _dg(lo, fold, d),)


fold_chunks.defvjp(lambda x: (fold_chunks(x), x.shape[0]), _fold_chunks_bwd)


def lmat_fn(k, beta_b, decay):
    _, _, strict = _chunk_masks(k.shape[0])
    return jnp.where(strict, mm_nt(k * beta_b, k) * decay, 0.0)


def intra_fn(tmat, q, k, v, beta_b, decay, eg, kfac):
    _, causal, _ = _chunk_masks(q.shape[0])
    k_beta = k * beta_b
    u_c = mm_nn(tmat, v * beta_b)
    w_c = mm_nn(tmat, k_beta * eg)
    qk = jnp.where(causal, mm_nt(q, k) * decay, 0.0)
    return u_c, w_c, q * eg, k * kfac, fold_chunks(qk)


def gated_norm(o, norm_w, zd):
    return rmsnorm(o, norm_w) * silu(zd)


def loss_fn(o, w, tgt):
    err = rmsnorm(o, w) - tgt
    return 0.5 * jnp.sum(jnp.mean(err * err, axis=-1))


_ANY = pl.BlockSpec(memory_space=pl.ANY)


class Hosted:
    def __init__(self, arrays, out_shapes, sems, stages):
        self.arrays, self.out_shapes, self.sems, self.stages = list(arrays), list(out_shapes), list(sems), stages


def matmul(name, a, b, *, ta=False, tb=False, tm, tn, tk=None, extras=(), epi=None, out_dtypes=(f32,), hosted=None):
    m, k = (a.shape[1], a.shape[0]) if ta else a.shape
    n = b.shape[0] if tb else b.shape[1]
    tm, tn, tk = min(tm, m), min(tn, n), min(tk or k, k)
    assert m % tm == 0 and n % tn == 0 and k % tk == 0, (name, m, n, k, tm, tn, tk)
    nm, nn, nk = m // tm, n // tn, k // tk
    a_spec = pl.BlockSpec((tk, tm), lambda i, j, kk: (kk, i)) if ta else pl.BlockSpec((tm, tk), lambda i, j, kk: (i, kk))
    b_spec = pl.BlockSpec((tn, tk), lambda i, j, kk: (j, kk)) if tb else pl.BlockSpec((tk, tn), lambda i, j, kk: (kk, j))
    ex_specs = []
    for _, off in extras:
        assert off % tn == 0, (name, off, tn)
        ex_specs.append(pl.BlockSpec((tm, tn), functools.partial(lambda i, j, kk, o: (i, o + j), o=off // tn)))
    n_ex, n_out = len(extras), len(out_dtypes)
    dims = ((0 if ta else 1,), (1 if tb else 0,))
    n_hin = len(hosted.arrays) if hosted else 0
    n_hout = len(hosted.out_shapes) if hosted else 0
    n_sem = len(hosted.sems) if hosted else 0
    keep_at = ta and nk == 1 and nn > 1

    def body(a_ref, b_ref, *rest):
        ex_refs, rest = rest[:n_ex], rest[n_ex:]
        hin_refs, rest = rest[:n_hin], rest[n_hin:]
        out_refs, rest = rest[:n_out], rest[n_out:]
        hout_refs, rest = rest[:n_hout], rest[n_hout:]
        sem_refs = rest[len(rest) - n_sem:] if n_sem else ()
        step = (pl.program_id(0) * nn + pl.program_id(1)) * nk + pl.program_id(2)
        stages = hosted.stages(hin_refs, hout_refs, sem_refs) if hosted else []
        for frac, emit in stages:
            if frac < 1.0:
                pl.when(step == int(frac * (nm * nn * nk - 1)))(emit)

        def finish(acc):
            res = epi(acc, *[r[...] for r in ex_refs]) if epi is not None else (acc,)
            for o_ref, val in zip(out_refs, res):
                o_ref[...] = val.astype(o_ref.dtype)

        if keep_at:
            at_ref = rest[0]

            @pl.when(pl.program_id(1) == 0)
            def _():
                at_ref[...] = a_ref[...].T

            p = _dg(at_ref[...], b_ref[...], ((1,), dims[1]))
        else:
            p = _dg(a_ref[...], b_ref[...], dims)
        if nk == 1:
            finish(p)
        else:
            acc_ref = rest[0]
            kk = pl.program_id(2)

            @pl.when(kk == 0)
            def _():
                acc_ref[...] = p

            @pl.when(kk > 0)
            def _():
                acc_ref[...] += p

            @pl.when(kk == nk - 1)
            def _():
                finish(acc_ref[...])

        for frac, emit in stages:
            if frac >= 1.0:
                pl.when(step == nm * nn * nk - 1)(emit)

    outs = _pc(
        body, name=name, grid=(nm, nn, nk),
        in_specs=[a_spec, b_spec] + ex_specs + [_ANY] * n_hin,
        out_specs=[pl.BlockSpec((tm, tn), lambda i, j, kk: (i, j))] * n_out + [_ANY] * n_hout,
        out_shape=[jax.ShapeDtypeStruct((m, n), dt) for dt in out_dtypes] + (hosted.out_shapes if hosted else []),
        scratch_shapes=([pltpu.VMEM((tm, tn), f32)] if nk > 1 else []) + ([pltpu.VMEM((tm, tk), a.dtype)] if keep_at else [])
        + (hosted.sems if hosted else []),
        compiler_params=_params(*(("arbitrary",) * 3 if hosted or keep_at else ("parallel", "parallel", "arbitrary"))),
    )(a, b, *[e for e, _ in extras], *(hosted.arrays if hosted else []))
    return outs[0] if len(outs) == 1 else outs


def _row_tile(rows, cols, n_arrays, itemsize=4, budget=24 * 2**20):
    best = None
    for t in range(16, rows + 1, 16):
        if rows % t == 0 and 2 * n_arrays * t * cols * itemsize <= budget:
            best = t
    return best or rows


def _tile(rows, cols, n_arrays, budget=24 * 2**20):
    if rows % 16 == 0 or cols % LANES != 0:
        return _row_tile(rows, cols, n_arrays, budget=budget), cols
    fits = [t for t in range(LANES, cols + 1, LANES) if cols % t == 0 and 2 * n_arrays * rows * t * 4 <= budget]
    return rows, (max(fits) if fits else LANES)


def cast_bf16(name, x):
    rows, cols = x.shape
    tr, tc = _tile(rows, cols, 2)

    def body(x_ref, o_ref):
        o_ref[...] = x_ref[...].astype(bf16)

    blk = pl.BlockSpec((tr, tc), lambda i, j: (i, j))
    return _pc(body, name=name, grid=(rows // tr, cols // tc), in_specs=[blk], out_specs=blk,
               out_shape=jax.ShapeDtypeStruct(x.shape, bf16), compiler_params=_params("parallel", "parallel"))(x)


def norm_fwd(hp, norm_w, tm):
    tp, d = hp.shape

    def body(h_ref, w_ref, o_ref):
        o_ref[...] = rmsnorm(h_ref[...], w_ref[...]).astype(bf16)

    return _pc(body, name="norm_fwd", grid=(tp // tm,),
               in_specs=[pl.BlockSpec((tm, d), lambda i: (i, 0)), pl.BlockSpec((1, d), lambda i: (0, 0))],
               out_specs=pl.BlockSpec((tm, d), lambda i: (i, 0)), out_shape=jax.ShapeDtypeStruct((tp, d), bf16),
               compiler_params=_params("parallel"))(hp, norm_w)


def norm_bwd(hp, norm_w, dxn, dout, tm):
    tp, d = hp.shape

    def body(h_ref, w_ref, dxn_ref, dout_ref, dh_ref, dw_ref):
        _, vjp = jax.vjp(rmsnorm, h_ref[...], w_ref[...])
        dh, dw = vjp(dxn_ref[...])
        dh_ref[...] = dh + dout_ref[...]

        @pl.when(pl.program_id(0) == 0)
        def _():
            dw_ref[...] = jnp.zeros_like(dw_ref)

        dw_ref[...] += dw

    row = pl.BlockSpec((tm, d), lambda i: (i, 0))
    vec = pl.BlockSpec((1, d), lambda i: (0, 0))
    return _pc(body, name="norm_bwd", grid=(tp // tm,), in_specs=[row, vec, row, row], out_specs=[row, vec],
               out_shape=[jax.ShapeDtypeStruct((tp, d), f32), jax.ShapeDtypeStruct((1, d), f32)],
               compiler_params=_params("arbitrary"))(hp, norm_w, dxn, dout)


def pool_fwd(proj, mix, scale, pw):
    tp = proj.shape[0]
    g = pw // POOL_GROUPS

    def body(u_ref, z_ref, mix_ref, s_ref, y_ref):
        y_ref[...] = pool_fn(u_ref[...], z_ref[...], mix_ref[0], s_ref[...], pl.program_id(0)).astype(bf16)

    return _pc(body, name="pool_fwd", grid=(POOL_GROUPS,),
               in_specs=[pl.BlockSpec((tp, g), lambda i: (0, i)), pl.BlockSpec((tp, g), lambda i: (0, POOL_GROUPS + i)),
                         pl.BlockSpec((1, g, g), lambda i: (i, 0, 0)), pl.BlockSpec((1, g), lambda i: (0, i))],
               out_specs=pl.BlockSpec((tp, g), lambda i: (0, i)), out_shape=jax.ShapeDtypeStruct((tp, pw), bf16),
               compiler_params=_params("parallel"))(proj, proj, mix, scale)


def pool_bwd(proj, mix, scale, dy, pw):
    tp = proj.shape[0]
    g = pw // POOL_GROUPS

    def body(u_ref, z_ref, mix_ref, s_ref, dy_ref, du_ref, dz_ref, dmix_ref, ds_ref):
        grp = pl.program_id(0)
        _, vjp = jax.vjp(lambda u, z, m, s: pool_fn(u, z, m, s, grp), u_ref[...], z_ref[...], mix_ref[0].astype(f32), s_ref[...])
        du, dz, dmix, ds = vjp(dy_ref[...])
        du_ref[...] = du.astype(bf16)
        dz_ref[...] = dz.astype(bf16)
        dmix_ref[0] = dmix
        ds_ref[...] = ds

    col = pl.BlockSpec((tp, g), lambda i: (0, i))
    return _pc(body, name="pool_bwd", grid=(POOL_GROUPS,),
               in_specs=[col, pl.BlockSpec((tp, g), lambda i: (0, POOL_GROUPS + i)),
                         pl.BlockSpec((1, g, g), lambda i: (i, 0, 0)), pl.BlockSpec((1, g), lambda i: (0, i)), col],
               out_specs=[col, col, pl.BlockSpec((1, g, g), lambda i: (i, 0, 0)), pl.BlockSpec((1, g), lambda i: (0, i))],
               out_shape=[jax.ShapeDtypeStruct((tp, pw), bf16), jax.ShapeDtypeStruct((tp, pw), bf16),
                          jax.ShapeDtypeStruct((POOL_GROUPS, g, g), f32), jax.ShapeDtypeStruct((1, pw), f32)],
               compiler_params=_params("parallel"))(proj, proj, mix, scale, dy)


def _dn_pre_specs(tp, n_heads, q_off):
    hb = lambda off: pl.BlockSpec((tp, HEAD_DIM), functools.partial(lambda h, o: (0, o + h), o=off))
    cw = lambda off: pl.BlockSpec((CONV_WIDTH, HEAD_DIM), functools.partial(lambda h, o: (0, o + h), o=off))
    whole = lambda shape: pl.BlockSpec(shape, lambda h: (0, 0))
    return ([hb(q_off), hb(q_off + n_heads), hb(q_off + 2 * n_heads), whole((tp, LANES)),
             cw(0), cw(n_heads), cw(2 * n_heads), whole((1, LANES)), whole((1, LANES))], hb, cw, whole)


def dn_pre_fwd(proj, ba, conv_w, a_log, dt_bias, n_heads, q_off):
    tp = proj.shape[0]
    in_specs, hb, _, _ = _dn_pre_specs(tp, n_heads, q_off)

    def body(q_ref, k_ref, v_ref, ba_ref, cq_ref, ck_ref, cv_ref, al_ref, dt_ref, *out_refs):
        outs = dn_pre_fn(q_ref[...], k_ref[...], v_ref[...], ba_ref[...], cq_ref[...], ck_ref[...], cv_ref[...],
                         al_ref[...], dt_ref[...], pl.program_id(0), n_heads)
        for o_ref, val in zip(out_refs, outs):
            o_ref[...] = val

    return _pc(body, name="dn_pre_fwd", grid=(n_heads,), in_specs=in_specs, out_specs=[hb(0)] * 5,
               out_shape=[jax.ShapeDtypeStruct((tp, n_heads * HEAD_DIM), f32)] * 5,
               compiler_params=_params("parallel"))(proj, proj, proj, ba, conv_w, conv_w, conv_w, a_log, dt_bias)


def dn_pre_bwd(proj, ba, conv_w, a_log, dt_bias, cots, n_heads, q_off):
    tp = proj.shape[0]
    in_specs, hb, cw, whole = _dn_pre_specs(tp, n_heads, q_off)

    def body(q_ref, k_ref, v_ref, ba_ref, cq_ref, ck_ref, cv_ref, al_ref, dt_ref, dq_ref, dk_ref, dv_ref, db_ref, dg_ref,
             dqr_ref, dkr_ref, dvr_ref, dba_ref, dcq_ref, dck_ref, dcv_ref, dal_ref, ddt_ref):
        head = pl.program_id(0)
        fn = lambda *args: dn_pre_fn(*args, head, n_heads)
        _, vjp = jax.vjp(fn, q_ref[...], k_ref[...], v_ref[...], ba_ref[...], cq_ref[...], ck_ref[...], cv_ref[...],
                         al_ref[...], dt_ref[...])
        dqr, dkr, dvr, dba, dcq, dck, dcv, dal, ddt = vjp((dq_ref[...], dk_ref[...], dv_ref[...], db_ref[...], dg_ref[...]))
        dqr_ref[...] = dqr.astype(bf16)
        dkr_ref[...] = dkr.astype(bf16)
        dvr_ref[...] = dvr.astype(bf16)
        dcq_ref[...] = dcq
        dck_ref[...] = dck
        dcv_ref[...] = dcv

        @pl.when(head == 0)
        def _():
            dba_ref[...] = jnp.zeros_like(dba_ref)
            dal_ref[...] = jnp.zeros_like(dal_ref)
            ddt_ref[...] = jnp.zeros_like(ddt_ref)

        dba_ref[...] += dba
        dal_ref[...] += dal
        ddt_ref[...] += ddt

    w = n_heads * HEAD_DIM
    return _pc(body, name="dn_pre_bwd", grid=(n_heads,), in_specs=in_specs + [hb(0)] * 5,
               out_specs=[hb(0)] * 3 + [whole((tp, LANES)), cw(0), cw(0), cw(0), whole((1, LANES)), whole((1, LANES))],
               out_shape=[jax.ShapeDtypeStruct((tp, w), bf16)] * 3 + [jax.ShapeDtypeStruct((tp, LANES), f32)]
               + [jax.ShapeDtypeStruct((CONV_WIDTH, w), f32)] * 3 + [jax.ShapeDtypeStruct((1, LANES), f32)] * 2,
               compiler_params=_params("arbitrary"))(proj, proj, proj, ba, conv_w, conv_w, conv_w, a_log, dt_bias, *cots)


def _super_rows(tp):
    n = tp // CHUNK
    return CHUNK * max(j for j in (4, 3, 2, 1) if n % j == 0)


def _heads_per_step(n_heads):
    return max(j for j in (4, 2, 1) if n_heads % j == 0)


def dn_intra_fwd(q, k, v, beta_b, g_b, n_heads):
    tp = q.shape[0]
    rows = _super_rows(tp)
    ns = tp // rows

    hps = _heads_per_step(n_heads)

    def body(q_ref, k_ref, v_ref, b_ref, g_ref, u_ref, w_ref, qd_ref, kd_ref, qk_ref, gl_ref, t_ref):
        lanes = [slice(i * HEAD_DIM, (i + 1) * HEAD_DIM) for i in range(hps)]
        gates = [gate_fn(g_ref[:, sl]) for sl in lanes]
        tmats = tri_inv([lmat_fn(k_ref[:, sl], b_ref[:, sl], gt[0]) for sl, gt in zip(lanes, gates)])
        for i, (sl, (decay, eg, kfac, gl), tmat) in enumerate(zip(lanes, gates, tmats)):
            u_c, w_c, q_dec, k_dec, qk_c = intra_fn(tmat, q_ref[:, sl], k_ref[:, sl], v_ref[:, sl], b_ref[:, sl], decay, eg, kfac)
            u_ref[:, sl] = u_c
            w_ref[:, sl] = w_c
            qd_ref[:, sl] = q_dec
            kd_ref[:, sl] = k_dec
            qk_ref[:, sl] = qk_c
            gl_ref[:, sl] = gl
            t_ref[i, 0] = tmat

    blk = pl.BlockSpec((rows, hps * HEAD_DIM), lambda h, s: (s, h))
    return _pc(body, name="dn_intra_fwd", grid=(n_heads // hps, ns), in_specs=[blk] * 5,
               out_specs=[blk] * 6 + [pl.BlockSpec((hps, 1, rows, rows), lambda h, s: (h, s, 0, 0))],
               out_shape=[jax.ShapeDtypeStruct(q.shape, f32)] * 6 + [jax.ShapeDtypeStruct((n_heads, ns, rows, rows), f32)],
               compiler_params=_params("parallel", "parallel"))(q, k, v, beta_b, g_b)


def dn_intra_bwd(q, k, v, beta_b, g_b, tmats, cots, n_heads):
    tp = q.shape[0]
    rows = _super_rows(tp)
    ns = tp // rows

    hps = _heads_per_step(n_heads)

    def body(q_ref, k_ref, v_ref, b_ref, g_ref, t_ref, du_ref, dw_ref, dqd_ref, dkd_ref, dqk_ref, dgl_ref,
             dq_ref, dk_ref, dv_ref, db_ref, dg_ref):
        lanes = [slice(i * HEAD_DIM, (i + 1) * HEAD_DIM) for i in range(hps)]
        tmats = [t_ref[i, 0] for i in range(hps)]
        gates = [jax.vjp(gate_fn, g_ref[:, sl]) for sl in lanes]
        intra = [jax.vjp(intra_fn, tmat, q_ref[:, sl], k_ref[:, sl], v_ref[:, sl], b_ref[:, sl], gt[0][0], gt[0][1], gt[0][2])[1](
            (du_ref[:, sl], dw_ref[:, sl], dqd_ref[:, sl], dkd_ref[:, sl], dqk_ref[:, sl]))
            for sl, tmat, gt in zip(lanes, tmats, gates)]
        tts = [tmat.T for tmat in tmats]
        dls = [mm_nn(tt, res[0]) for tt, res in zip(tts, intra)]
        dls = [-mm_nn(dl, tt) for dl, tt in zip(dls, tts)]
        for sl, gt, res, dl in zip(lanes, gates, intra, dls):
            _, dq, dk, dv, db, ddecay, deg, dkfac = res
            dk2, db2, ddecay2 = jax.vjp(lmat_fn, k_ref[:, sl], b_ref[:, sl], gt[0][0])[1](dl)
            (dg,) = gt[1]((ddecay + ddecay2, deg, dkfac, dgl_ref[:, sl]))
            dq_ref[:, sl] = dq
            dk_ref[:, sl] = dk + dk2
            dv_ref[:, sl] = dv
            db_ref[:, sl] = db + db2
            dg_ref[:, sl] = dg

    blk = pl.BlockSpec((rows, hps * HEAD_DIM), lambda h, s: (s, h))
    return _pc(body, name="dn_intra_bwd", grid=(n_heads // hps, ns),
               in_specs=[blk] * 5 + [pl.BlockSpec((hps, 1, rows, rows), lambda h, s: (h, s, 0, 0))] + [blk] * 6,
               out_specs=[blk] * 5, out_shape=[jax.ShapeDtypeStruct(q.shape, f32)] * 5,
               compiler_params=_params("parallel", "parallel"))(q, k, v, beta_b, g_b, tmats, *cots)


def dn_seq_fwd(inter, proj, dn_norm_w, n_heads, zd_off):
    tp, width = inter[0].shape
    n_chunks = tp // CHUNK

    def body(u_ref, w_ref, qd_ref, kd_ref, qk_ref, gl_ref, z_ref, nw_ref, y_ref, s_ref, state):
        @pl.when(pl.program_id(0) == 0)
        def _():
            state[...] = jnp.zeros_like(state)

        lanes = [slice(h * HEAD_DIM, (h + 1) * HEAD_DIM) for h in range(n_heads)]
        sts = [state[h] for h in range(n_heads)]
        for h, st in enumerate(sts):
            s_ref[0, h] = st
        v_new = [u_ref[:, sl] - mm_nn(w_ref[:, sl], st) for sl, st in zip(lanes, sts)]
        outs = [mm_nn(qd_ref[:, sl], st) + mm_nn(qk_ref[:, sl][:, :CHUNK], vn) for sl, st, vn in zip(lanes, sts, v_new)]
        for h, (sl, st, vn) in enumerate(zip(lanes, sts, v_new)):
            state[h] = st * gl_ref[0:1, sl] + mm_tn(kd_ref[:, sl], vn)
        for sl, o in zip(lanes, outs):
            y_ref[:, sl] = gated_norm(o, nw_ref[...], z_ref[:, sl]).astype(bf16)

    blk = pl.BlockSpec((CHUNK, width), lambda n: (n, 0))
    return _pc(body, name="dn_seq_fwd", grid=(n_chunks,),
               in_specs=[blk] * 6 + [pl.BlockSpec((CHUNK, width), lambda n: (n, zd_off)), pl.BlockSpec((1, HEAD_DIM), lambda n: (0, 0))],
               out_specs=[blk, pl.BlockSpec((1, n_heads, HEAD_DIM, HEAD_DIM), lambda n: (n, 0, 0, 0))],
               out_shape=[jax.ShapeDtypeStruct((tp, width), bf16), jax.ShapeDtypeStruct((n_chunks, n_heads, HEAD_DIM, HEAD_DIM), f32)],
               scratch_shapes=[pltpu.VMEM((n_heads, HEAD_DIM, HEAD_DIM), f32)],
               compiler_params=_params("arbitrary"))(*inter, proj, dn_norm_w)


def dn_seq_bwd(inter, proj, dn_norm_w, states, dy, n_heads, zd_off):
    tp, width = inter[0].shape
    n_chunks = tp // CHUNK
    last = n_chunks - 1

    def body(u_ref, w_ref, qd_ref, kd_ref, qk_ref, gl_ref, z_ref, nw_ref, s_ref, dy_ref,
             du_ref, dw_ref, dqd_ref, dkd_ref, dqk_ref, dgl_ref, dz_ref, dnw_ref, dstate):
        @pl.when(pl.program_id(0) == 0)
        def _():
            dstate[...] = jnp.zeros_like(dstate)
            dnw_ref[...] = jnp.zeros_like(dnw_ref)

        lanes = [slice(h * HEAD_DIM, (h + 1) * HEAD_DIM) for h in range(n_heads)]
        sts = [s_ref[0, h] for h in range(n_heads)]
        dsts = [dstate[h] for h in range(n_heads)]
        v_new = [u_ref[:, sl] - mm_nn(w_ref[:, sl], st) for sl, st in zip(lanes, sts)]
        outs = [mm_nn(qd_ref[:, sl], st) + mm_nn(qk_ref[:, sl][:, :CHUNK], vn) for sl, st, vn in zip(lanes, sts, v_new)]
        dnw = jnp.zeros((1, HEAD_DIM), f32)
        d_outs = []
        for sl, o in zip(lanes, outs):
            do, dn, dz = jax.vjp(gated_norm, o, nw_ref[...], z_ref[:, sl])[1](dy_ref[:, sl])
            dz_ref[:, sl] = dz.astype(bf16)
            dnw = dnw + dn
            d_outs.append(do)
        dnw_ref[...] += dnw
        d_vn = [mm_tn(qk_ref[:, sl][:, :CHUNK], do) + mm_nn(kd_ref[:, sl], ds) for sl, do, ds in zip(lanes, d_outs, dsts)]
        zeros = jnp.zeros((HEAD_DIM - CHUNK, HEAD_DIM), f32)
        rows = lax.broadcasted_iota(jnp.int32, (CHUNK, HEAD_DIM), 0)
        for h, (sl, st, vn, do, ds, dvn) in enumerate(zip(lanes, sts, v_new, d_outs, dsts, d_vn)):
            du_ref[:, sl] = dvn
            dw_ref[:, sl] = -mm_nt(dvn, st)
            dqd_ref[:, sl] = mm_nt(do, st)
            dkd_ref[:, sl] = mm_nt(vn, ds)
            dqk_ref[:, sl] = mm_nt(do, jnp.concatenate([vn, zeros], axis=0))
            dgl_ref[:, sl] = jnp.where(rows == 0, jnp.sum(st * ds, axis=0, keepdims=True), 0.0)
            dstate[h] = ds * gl_ref[0:1, sl] + mm_tn(qd_ref[:, sl], do) - mm_tn(w_ref[:, sl], dvn)

    blk = pl.BlockSpec((CHUNK, width), lambda n: (last - n, 0))
    return _pc(body, name="dn_seq_bwd", grid=(n_chunks,),
               in_specs=[blk] * 6 + [pl.BlockSpec((CHUNK, width), lambda n: (last - n, zd_off)), pl.BlockSpec((1, HEAD_DIM), lambda n: (0, 0)),
                         pl.BlockSpec((1, n_heads, HEAD_DIM, HEAD_DIM), lambda n: (last - n, 0, 0, 0)), blk],
               out_specs=[blk] * 7 + [pl.BlockSpec((1, HEAD_DIM), lambda n: (0, 0))],
               out_shape=[jax.ShapeDtypeStruct((tp, width), f32)] * 6 + [jax.ShapeDtypeStruct((tp, width), bf16),
                                                                          jax.ShapeDtypeStruct((1, HEAD_DIM), f32)],
               scratch_shapes=[pltpu.VMEM((n_heads, HEAD_DIM, HEAD_DIM), f32)],
               compiler_params=_params("arbitrary"))(*inter, proj, dn_norm_w, states, dy)


def loss_stage(out, final_w, target):
    tp, d = out.shape
    n_tiles = tp // CHUNK

    def body(o_ref, w_ref, t_ref, loss_ref, do_ref, dob_ref, dw_ref):
        i = pl.program_id(0)

        @pl.when(i == 0)
        def _():
            loss_ref[...] = jnp.zeros_like(loss_ref)
            dw_ref[...] = jnp.zeros_like(dw_ref)

        scored = (i > 0).astype(f32)
        val, (do, dw) = jax.value_and_grad(lambda o, w: scored * loss_fn(o, w, t_ref[...]), argnums=(0, 1))(o_ref[...], w_ref[...])
        loss_ref[...] += jnp.full(loss_ref.shape, val, f32)
        do_ref[...] = do
        dob_ref[...] = do.astype(bf16)
        dw_ref[...] += dw

    row = pl.BlockSpec((CHUNK, d), lambda i: (i, 0))
    vec = pl.BlockSpec((1, d), lambda i: (0, 0))
    return _pc(body, name="loss_stage", grid=(n_tiles,),
               in_specs=[row, vec, pl.BlockSpec((CHUNK, d), lambda i: (jnp.maximum(i - 1, 0), 0))],
               out_specs=[pl.BlockSpec((1, LANES), lambda i: (0, 0)), row, row, vec],
               out_shape=[jax.ShapeDtypeStruct((1, LANES), f32), jax.ShapeDtypeStruct((tp, d), f32),
                          jax.ShapeDtypeStruct((tp, d), bf16), jax.ShapeDtypeStruct((1, d), f32)],
               compiler_params=_params("arbitrary"))(out, final_w, target)


def _adam_update(w, g, m, v):
    nm = ADAM_B1 * m + (1.0 - ADAM_B1) * g
    nv = ADAM_B2 * v + (1.0 - ADAM_B2) * jnp.square(g)
    m_hat = nm / (1.0 - ADAM_B1 ** ADAM_STEP)
    v_hat = nv / (1.0 - ADAM_B2 ** ADAM_STEP)
    return -ADAM_LR * (m_hat / (jnp.sqrt(v_hat) + ADAM_EPS) + ADAM_WD * w), nm, nv


def adamw(name, w, g, m, v):
    rows, cols = w.shape
    t = _row_tile(rows, cols, 7)

    def body(w_ref, g_ref, m_ref, v_ref, d_ref, nm_ref, nv_ref):
        d_ref[...], nm_ref[...], nv_ref[...] = _adam_update(w_ref[...], g_ref[...], m_ref[...], v_ref[...])

    blk = pl.BlockSpec((t, cols), lambda i: (i, 0))
    return _pc(body, name=name, grid=(rows // t,), in_specs=[blk] * 4, out_specs=[blk] * 3,
               out_shape=[jax.ShapeDtypeStruct(w.shape, f32)] * 3, compiler_params=_params("parallel"))(w, g, m, v)


def adamw_joined(name, w, g_mine, g_theirs, m, v, half, axis):
    rows, cols = w.shape
    hr, hc = g_mine.shape
    tr, tc = _tile(hr, hc, 9)
    nr, nc = hr // tr, hc // tc

    def body(half_ref, w_ref, gm_ref, gt_ref, m_ref, v_ref, g_ref, d_ref, nm_ref, nv_ref):
        pos = pl.program_id(axis) // (nr if axis == 0 else nc)
        g = jnp.where(pos == half_ref[0], gm_ref[...], gt_ref[...])
        delta, nm, nv = _adam_update(w_ref[...], g, m_ref[...], v_ref[...])
        g_ref[...] = g
        d_ref[...] = delta
        nm_ref[...] = nm
        nv_ref[...] = nv

    whole = pl.BlockSpec((tr, tc), lambda i, j, hf: (i, j))
    part = pl.BlockSpec((tr, tc), lambda i, j, hf: (i % nr, j % nc))
    grid_spec = pltpu.PrefetchScalarGridSpec(num_scalar_prefetch=1, grid=(rows // tr, cols // tc),
                                             in_specs=[whole, part, part, whole, whole], out_specs=[whole] * 4)
    return _pc(body, name=name, grid_spec=grid_spec, out_shape=[jax.ShapeDtypeStruct(w.shape, f32)] * 4,
               compiler_params=_params("parallel", "parallel"))(half, w, g_mine, g_theirs, m, v)


def add_halves(name, g, recv, half, kind):
    s, r, c = recv.shape
    tr, tc = _tile(r, c, 3)
    nc = c // tc

    def body(half_ref, g_ref, r_ref, o_ref):
        o_ref[...] = (g_ref[...].reshape(r_ref.shape).astype(f32) + r_ref[...].astype(f32)).astype(bf16)

    if kind == "lead":
        g_spec = pl.BlockSpec((1, 1, tr, tc), lambda i, j, k, hf: (hf[0], i, j, k))
    else:
        g_spec = pl.BlockSpec((1, tr, tc), lambda i, j, k, hf: (i, j, hf[0] * nc + k))
    blk = pl.BlockSpec((1, tr, tc), lambda i, j, k, hf: (i, j, k))
    grid_spec = pltpu.PrefetchScalarGridSpec(num_scalar_prefetch=1, grid=(s, r // tr, nc), in_specs=[g_spec, blk], out_specs=blk)
    return _pc(body, name=name, grid_spec=grid_spec, out_shape=jax.ShapeDtypeStruct((s, r, c), bf16),
               compiler_params=_params("parallel", "parallel", "parallel"))(half, g, recv)


def add_first(name, parts, got, chips):
    _, r, c = parts.shape
    tr, tc = _tile(r, c, 6)

    def body(chips_ref, mine_ref, theirs_ref, got_ref, keep_ref, pass_ref):
        keep_ref[...] = mine_ref[0].astype(f32) + got_ref[0].astype(f32)
        pass_ref[...] = (theirs_ref[0].astype(f32) + got_ref[1].astype(f32)).astype(bf16)

    blk = pl.BlockSpec((tr, tc), lambda i, j, ch: (i, j))
    grid_spec = pltpu.PrefetchScalarGridSpec(
        num_scalar_prefetch=1, grid=(r // tr, c // tc),
        in_specs=[pl.BlockSpec((1, tr, tc), lambda i, j, ch: (ch[0], i, j)), pl.BlockSpec((1, tr, tc), lambda i, j, ch: (ch[1], i, j)),
                  pl.BlockSpec((2, tr, tc), lambda i, j, ch: (0, i, j))],
        out_specs=[blk, blk])
    return _pc(body, name=name, grid_spec=grid_spec, out_shape=[jax.ShapeDtypeStruct((r, c), f32), jax.ShapeDtypeStruct((r, c), bf16)],
               compiler_params=_params("parallel", "parallel"))(chips, parts, parts, got)


def add_second(name, kept, got):
    r, c = kept.shape
    tr, tc = _tile(r, c, 3)

    def body(k_ref, g_ref, o_ref):
        o_ref[...] = k_ref[...] + g_ref[...].astype(f32)

    blk = pl.BlockSpec((tr, tc), lambda i, j: (i, j))
    return _pc(body, name=name, grid=(r // tr, c // tc), in_specs=[blk, blk], out_specs=blk,
               out_shape=jax.ShapeDtypeStruct((r, c), f32), compiler_params=_params("parallel", "parallel"))(kept, got)


def sum_leading(name, x, out_dtype=f32):
    s, r, c = x.shape
    tr, tc = _tile(r, c, s + 1)

    def body(x_ref, o_ref):
        acc = x_ref[0].astype(f32)
        for i in range(1, s):
            acc = acc + x_ref[i].astype(f32)
        o_ref[...] = acc.astype(out_dtype)

    return _pc(body, name=name, grid=(r // tr, c // tc), in_specs=[pl.BlockSpec((s, tr, tc), lambda i, j: (0, i, j))],
               out_specs=pl.BlockSpec((tr, tc), lambda i, j: (i, j)), out_shape=jax.ShapeDtypeStruct((r, c), out_dtype),
               compiler_params=_params("parallel", "parallel"))(x)


def _place():
    x, y, c = lax.axis_index("x"), lax.axis_index("y"), lax.axis_index("c")
    return x, y, c


def _route(x, y, c):
    return (x ^ (1 - c), y ^ c), (x ^ c, y ^ (1 - c)), (1 - x, 1 - y)


def _half_view(ref, half, kind, lead=()):
    if kind == "lead":
        return ref.at[(*lead, half)]
    width = ref.shape[-1] // 2
    return ref.at[(*lead, *([slice(None)] * (len(ref.shape) - len(lead) - 1)), pl.ds(half * width, width))]


def _gather_plan(ins, outs, sems, kinds):
    n = len(ins)
    send1, recv1, send2, recv2 = sems
    x, y, c = _place()
    chip = 2 * x + y
    sibling = (x, y, 1 - c)
    near, far, diag = _route(x, y, c)
    near_id, far_id, diag_id = [2 * cx + cy for cx, cy in (near, far, diag)]

    def remote(src, dst, s_sem, r_sem, to):
        return pltpu.make_async_remote_copy(src_ref=src, dst_ref=dst, send_sem=s_sem, recv_sem=r_sem, device_id=to, device_id_type=MESH)

    def slab(a, chip_id, half):
        return _half_view(outs[a], half, kinds[a], (chip_id,))

    def landed(a, chip_id, sem, frm):
        return remote(slab(a, chip_id, c), slab(a, chip_id, c), send1.at[a, sem], recv1.at[a, sem], (*frm, c))

    def onward(a, chip_id, sem):
        return remote(slab(a, chip_id, c), slab(a, chip_id, c), send2.at[a, sem], recv2.at[a, sem], sibling)

    own = [remote(_half_view(ins[a], c, kinds[a]), slab(a, chip, c), send1.at[a, j], recv1.at[a, j], (*to, c))
           for a in range(n) for j, to in enumerate((near, far))]
    relay = [remote(slab(a, near_id, c), slab(a, near_id, c), send1.at[a, 2], recv1.at[a, 2], (*far, c)) for a in range(n)]
    to_sibling = [[onward(a, cid, sem) for sem, cid in enumerate((near_id, far_id, diag_id))] for a in range(n)]
    from_sibling = [remote(slab(a, cid, 1 - c), slab(a, cid, 1 - c), send2.at[a, sem], recv2.at[a, sem], sibling)
                    for a in range(n) for sem, cid in enumerate((far_id, near_id, diag_id))]

    def start_own():
        for cp in own:
            cp.start()

    def pass_near():
        for a in range(n):
            landed(a, near_id, 0, near).wait_recv()
            relay[a].start()
            to_sibling[a][0].start()

    def pass_far():
        for sem, cid in ((1, far_id), (2, diag_id)):
            for a in range(n):
                landed(a, cid, sem, far).wait_recv()
                to_sibling[a][sem].start()

    def finish():
        for cp in from_sibling:
            cp.wait_recv()
        for cp in own + relay + [cp for row in to_sibling for cp in row]:
            cp.wait_send()

    return [start_own, pass_near, pass_far, finish]


def _gather_shapes(shards):
    return ([jax.ShapeDtypeStruct((4,) + s.shape, s.dtype) for s in shards], [pltpu.SemaphoreType.DMA((len(shards), 3))] * 4)


def gather_weights(shards, kinds):
    n = len(shards)

    def body(*refs):
        for emit in _gather_plan(refs[:n], refs[n:2 * n], refs[2 * n:], kinds):
            emit()

    out_shapes, sems = _gather_shapes(shards)
    return _pc(body, name="gather_weights", in_specs=[_ANY] * n, out_specs=[_ANY] * n, out_shape=out_shapes, scratch_shapes=sems)(*shards)


def hosted_gather(shards, kinds):
    out_shapes, sems = _gather_shapes(shards)
    return Hosted(shards, out_shapes, sems, lambda i, o, s: list(zip((0.0, 0.45, 0.8, 1.0), _gather_plan(i, o, s, kinds))))


def swap_with_sibling(name, sends):
    n = len(sends)

    def body(*refs):
        ins, outs = refs[:n], refs[n:2 * n]
        send, recv = refs[2 * n:]
        x, y, c = _place()
        cps = [pltpu.make_async_remote_copy(src_ref=ins[a], dst_ref=outs[a], send_sem=send.at[a], recv_sem=recv.at[a],
                                            device_id=(x, y, 1 - c), device_id_type=MESH) for a in range(n)]
        for cp in cps:
            cp.start()
        for cp in cps:
            cp.wait()

    return _pc(body, name=name, in_specs=[_ANY] * n, out_specs=[_ANY] * n,
               out_shape=[jax.ShapeDtypeStruct(s.shape, s.dtype) for s in sends],
               scratch_shapes=[pltpu.SemaphoreType.DMA((n,))] * 2)(*sends)


def send_grad_halves(grads, kinds):
    n = len(grads)

    def body(*refs):
        ins, outs = refs[:n], refs[n:2 * n]
        send, recv = refs[2 * n:]
        x, y, c = _place()
        cps = [pltpu.make_async_remote_copy(src_ref=_half_view(ins[a], 1 - c, kinds[a]), dst_ref=outs[a], send_sem=send.at[a],
                                            recv_sem=recv.at[a], device_id=(x, y, 1 - c), device_id_type=MESH) for a in range(n)]
        for cp in cps:
            cp.start()
        for cp in cps:
            cp.wait()

    shape = lambda g, kind: g.shape[1:] if kind == "lead" else g.shape[:-1] + (g.shape[-1] // 2,)
    return _pc(body, name="send_grad_halves", in_specs=[_ANY] * n, out_specs=[_ANY] * n,
               out_shape=[jax.ShapeDtypeStruct(shape(g, k), g.dtype) for g, k in zip(grads, kinds)],
               scratch_shapes=[pltpu.SemaphoreType.DMA((n,))] * 2)(*grads)


def _scatter_first_plan(ins, outs, sems):
    n = len(ins)
    send, recv = sems
    x, y, c = _place()
    near, _, diag = _route(x, y, c)
    cps = [pltpu.make_async_remote_copy(src_ref=ins[a].at[2 * cx + cy], dst_ref=outs[a].at[j], send_sem=send.at[a, j],
                                        recv_sem=recv.at[a, j], device_id=(*near, c), device_id_type=MESH)
           for a in range(n) for j, (cx, cy) in enumerate((near, diag))]

    def start():
        for cp in cps:
            cp.start()

    def wait():
        for cp in cps:
            cp.wait()

    return [start, wait]


def _scatter_first_shapes(parts):
    return ([jax.ShapeDtypeStruct((2,) + p.shape[1:], p.dtype) for p in parts], [pltpu.SemaphoreType.DMA((len(parts), 2))] * 2)


def scatter_first(parts):
    n = len(parts)

    def body(*refs):
        for emit in _scatter_first_plan(refs[:n], refs[n:2 * n], refs[2 * n:]):
            emit()

    out_shapes, sems = _scatter_first_shapes(parts)
    return _pc(body, name="scatter_first", in_specs=[_ANY] * n, out_specs=[_ANY] * n, out_shape=out_shapes, scratch_shapes=sems)(*parts)


def hosted_scatter_first(parts):
    out_shapes, sems = _scatter_first_shapes(parts)
    return Hosted(parts, out_shapes, sems, lambda i, o, s: list(zip((0.0, 1.0), _scatter_first_plan(i, o, s))))


def scatter_second_and_small(parts, pack):
    n = len(parts)

    def body(*refs):
        ins, pack_ref, outs, packs_ref = refs[:n], refs[n], refs[n + 1:2 * n + 1], refs[2 * n + 1]
        send, recv, pack_send, pack_recv = refs[2 * n + 2:]
        x, y, c = _place()
        me = 4 * x + 2 * y + c
        _, far, _ = _route(x, y, c)
        cps = [pltpu.make_async_remote_copy(src_ref=ins[a], dst_ref=outs[a], send_sem=send.at[a], recv_sem=recv.at[a],
                                            device_id=(*far, c), device_id_type=MESH) for a in range(n)]
        flips = [(fx, fy, fc) for fx in (0, 1) for fy in (0, 1) for fc in (0, 1)][1:]
        peers = [(x ^ fx, y ^ fy, c ^ fc) for fx, fy, fc in flips]
        pack_cps = [pltpu.make_async_remote_copy(src_ref=pack_ref, dst_ref=packs_ref.at[me], send_sem=pack_send.at[j],
                                                 recv_sem=pack_recv.at[j], device_id=p, device_id_type=MESH) for j, p in enumerate(peers)]
        for cp in cps + pack_cps:
            cp.start()
        for cp in cps:
            cp.wait()
        for j, (px, py, pc) in enumerate(peers):
            slab = packs_ref.at[4 * px + 2 * py + pc]
            pltpu.make_async_remote_copy(src_ref=slab, dst_ref=slab, send_sem=pack_send.at[j], recv_sem=pack_recv.at[j],
                                         device_id=(px, py, pc), device_id_type=MESH).wait_recv()
        for cp in pack_cps:
            cp.wait_send()

    outs = _pc(body, name="scatter_second_and_small", in_specs=[_ANY] * (n + 1), out_specs=[_ANY] * (n + 1),
               out_shape=[jax.ShapeDtypeStruct(p.shape, p.dtype) for p in parts] + [jax.ShapeDtypeStruct((8,) + pack.shape, pack.dtype)],
               scratch_shapes=[pltpu.SemaphoreType.DMA((n,))] * 2 + [pltpu.SemaphoreType.DMA((7,))] * 2)(*parts, pack)
    return outs[:n], outs[n]


def _pack(arrays, width):
    rows = []
    for a in arrays:
        flat = a.reshape(-1).astype(f32)
        pad = (-flat.shape[0]) % (8 * width)
        rows.append(jnp.pad(flat, (0, pad)).reshape(-1, width))
    return jnp.concatenate(rows, axis=0)


def _unpack(pack, like, width):
    out, row = [], 0
    for a in like:
        size = 1
        for s in a.shape:
            size *= s
        n_rows = -(-size // (8 * width)) * 8
        out.append(pack[row:row + n_rows].reshape(-1)[:size].reshape(a.shape))
        row += n_rows
    return out


def _halves(a2d):
    r, c = a2d.shape
    return a2d.reshape(2, r // 2, c)


def local_step(x, meta, norm_w, w_main, w_ba, conv_w, a_log, dt_bias, pool_mix, pool_scale, dn_norm_w,
               late_weights, final_w, target, grad_hook=None):
    seq, d = x.shape
    pw = pool_scale.shape[1]
    dw = conv_w.shape[1] // 3
    n_heads = dw // HEAD_DIM
    tp = FRONT_PAD + N_META + seq
    hp = jnp.concatenate([jnp.zeros((FRONT_PAD, d), f32), meta, x], axis=0)
    tm_big = tp // 2 if tp % 32 == 0 else tp
    tm_norm = max(t for t in range(16, min(tp, 352) + 1, 16) if tp % t == 0)
    tile = min(512, d)
    off_q = 2 * pw
    off_zd = off_q + 3 * dw
    off_gp = off_zd + dw
    off_gd = off_gp + d
    a_log128 = jnp.pad(a_log, ((0, 0), (0, LANES - n_heads)))
    dt128 = jnp.pad(dt_bias, ((0, 0), (0, LANES - n_heads)))

    xn = norm_fwd(hp, norm_w, tm_norm)
    if isinstance(late_weights[0], Hosted):
        proj, *fetched = matmul("proj", xn, w_main, tb=True, tm=tp, tn=tile, hosted=late_weights[0])
        w_pool_out, w_dn_out, w_o = late_weights[1](fetched)
    else:
        proj = matmul("proj", xn, w_main, tb=True, tm=tp, tn=tile)
        w_pool_out, w_dn_out, w_o = late_weights
    ba = matmul("proj_ba", xn, w_ba, tb=True, tm=tp, tn=LANES)
    y_pool = pool_fwd(proj, pool_mix, pool_scale, pw)
    q, k, v, beta_b, g_b = dn_pre_fwd(proj, ba, conv_w, a_log128, dt128, n_heads, off_q // HEAD_DIM)
    assert off_zd % dw == 0
    *inter, tmats = dn_intra_fwd(q, k, v, beta_b, g_b, n_heads)
    y_dn, states = dn_seq_fwd(inter, proj, dn_norm_w, n_heads, off_zd // dw)
    a_mat = matmul("pool_out", y_pool, w_pool_out, tm=tp, tn=tile)

    def merge(acc, a_t, gp_t, gd_t):
        return acc, sigmoid(gp_t) * a_t + sigmoid(gd_t) * acc

    b_mat, merged = matmul("dn_out_merge", y_dn, w_dn_out, tm=tm_big, tn=tile, extras=[(a_mat, 0), (proj, off_gp), (proj, off_gd)],
                           epi=merge, out_dtypes=(f32, bf16))
    out = matmul("out_proj", merged, w_o, tm=tm_big, tn=tile, extras=[(hp, 0)], epi=lambda acc, h_t: (acc + h_t,))
    loss, dout, dout_b, dfinal_w = loss_stage(out, final_w, target)

    def unmerge(dm, a_t, b_t, gp_t, gd_t):
        sp, sd = sigmoid(gp_t), sigmoid(gd_t)
        return dm * sp, dm * sd, dm * a_t * sp * (1.0 - sp), dm * b_t * sd * (1.0 - sd)

    d_a, d_b, d_gp, d_gd = matmul("d_merged", dout_b, w_o, tb=True, tm=tm_big, tn=tile,
                                  extras=[(a_mat, 0), (b_mat, 0), (proj, off_gp), (proj, off_gd)], epi=unmerge,
                                  out_dtypes=(bf16,) * 4)
    g_w_o = matmul("g_w_o", merged, dout_b, ta=True, tm=tile, tn=tile, out_dtypes=(bf16,))
    d_y_pool = matmul("d_y_pool", d_a, w_pool_out, tb=True, tm=tp, tn=tile)
    g_w_pool_out = matmul("g_w_pool_out", y_pool, d_a, ta=True, tm=tile, tn=tile, out_dtypes=(bf16,))
    d_y_dn = matmul("d_y_dn", d_b, w_dn_out, tb=True, tm=tp, tn=tile)
    g_w_dn_out = matmul("g_w_dn_out", y_dn, d_b, ta=True, tm=tile, tn=tile, out_dtypes=(bf16,))
    d_u, d_zp, g_pool_mix, g_pool_scale = pool_bwd(proj, pool_mix, pool_scale, d_y_pool, pw)
    *d_inter, d_zd, g_dn_norm_w = dn_seq_bwd(inter, proj, dn_norm_w, states, d_y_dn, n_heads, off_zd // dw)
    d_q, d_k, d_v, d_beta, d_g = dn_intra_bwd(q, k, v, beta_b, g_b, tmats, d_inter, n_heads)
    d_qr, d_kr, d_vr, d_ba, g_cq, g_ck, g_cv, g_a_log, g_dt = dn_pre_bwd(
        proj, ba, conv_w, a_log128, dt128, (d_q, d_k, d_v, d_beta, d_g), n_heads, off_q // HEAD_DIM)
    d_proj = jnp.concatenate([d_u, d_zp, d_qr, d_kr, d_vr, d_zd, d_gp, d_gd], axis=1)
    d_ba_b = cast_bf16("cast_d_ba", d_ba)
    g_w_main = matmul("g_w_main", d_proj, xn, ta=True, tm=tile, tn=tile, out_dtypes=(bf16,))
    g_w_ba = matmul("g_w_ba", d_ba_b, xn, ta=True, tm=LANES, tn=tile, out_dtypes=(bf16,))
    hosted, carried = grad_hook(g_w_main, g_w_ba, g_w_pool_out, g_w_dn_out, g_w_o, g_pool_mix) if grad_hook else (None, None)
    dxn_ba = matmul("dxn_ba", d_ba_b, w_ba, tm=tp, tn=tile)
    n_cols = d_proj.shape[1]
    tk_dxn = max(t for t in range(LANES, min(3584, n_cols) + 1, LANES) if n_cols % t == 0)
    res = matmul("dxn", d_proj, w_main, tm=tm_big, tn=tile, tk=tk_dxn, extras=[(dxn_ba, 0)], epi=lambda acc, e: (acc + e,), hosted=hosted)
    dxn, landed = (res[0], list(res[1:])) if hosted else (res, [])
    dh, g_norm_w = norm_bwd(hp, norm_w, dxn, dout, tm_norm)
    g_conv = jnp.concatenate([g_cq, g_ck, g_cv], axis=1)
    return (loss, dh, g_norm_w, g_w_main, g_w_ba, g_conv, g_a_log[:, :n_heads], g_dt[:, :n_heads], g_pool_mix, g_pool_scale,
            g_dn_norm_w, g_w_pool_out, g_w_dn_out, g_w_o, dfinal_w, (carried, landed))


def kernel(x, meta_tokens, norm_w, w_in, conv_w, A_log, dt_bias, pool_mix, pool_scale, dn_norm_w, w_pool_out, w_dn_out, w_o, final_norm_w, loss_target, m_meta_tokens, m_norm_w, m_w_in, m_conv_w, m_A_log, m_dt_bias, m_pool_mix, m_pool_scale, m_dn_norm_w, m_w_pool_out, m_w_dn_out, m_w_o, m_final_norm_w, v_meta_tokens, v_norm_w, v_w_in, v_conv_w, v_A_log, v_dt_bias, v_pool_mix, v_pool_scale, v_dn_norm_w, v_w_pool_out, v_w_dn_out, v_w_o, v_final_norm_w):
    d = x.shape[-1]
    pw = pool_scale.shape[-1]
    dw = w_dn_out.shape[1] * 4
    n_heads = dw // HEAD_DIM
    gdim = pw // POOL_GROUPS
    chip = 2 * lax.axis_index("x") + lax.axis_index("y")
    core = lax.axis_index("c")

    half = core.astype(jnp.int32).reshape(1)
    me = 4 * lax.axis_index("x") + 2 * lax.axis_index("y") + core

    w_in_t = w_in[0].T
    mix_s = pool_mix[0].reshape(POOL_GROUPS * (gdim // 4), gdim)
    small_s = _pack([meta_tokens, conv_w[0]], d)
    small_rows = small_s.shape[0]
    small_s = jnp.pad(small_s, ((0, (-small_rows) % 16), (0, 0)))
    sw = w_in_t.shape[0]
    n_main, n_ba = 2 * pw + 4 * dw, 2 * n_heads

    early = [cast_bf16("cast_w_in", w_in_t), _halves(cast_bf16("cast_mix", mix_s)), _halves(small_s)]
    late = [_halves(cast_bf16("cast_w_po", w_pool_out[0])), _halves(cast_bf16("cast_w_do", w_dn_out[0])),
            _halves(cast_bf16("cast_w_o", w_o[0]))]
    fill = lambda g, own: lax.dynamic_update_slice(g, own[None], (chip,) + (0,) * own.ndim)
    g_in, g_mix, g_small = [fill(g, own) for g, own in zip(gather_weights(early, ["cols", "lead", "lead"]), early)]

    def shard_rows(lo, hi):
        cut = [(max(lo, j * sw), min(hi, (j + 1) * sw), j) for j in range(4)]
        return [g_in[j, a - j * sw:b - j * sw] for a, b, j in cut if a < b]

    w_main = jnp.concatenate(shard_rows(0, n_main) + shard_rows(n_main + n_ba, 4 * sw), axis=0)
    w_ba = jnp.pad(jnp.concatenate(shard_rows(n_main, n_main + n_ba), axis=0), ((0, LANES - n_ba), (0, 0)))
    cat_cols = lambda g: jnp.concatenate([g[j].reshape(-1, g.shape[-1]) for j in range(4)], axis=1)

    def late_weights(fetched):
        g_po, g_do, g_o = [fill(g, own) for g, own in zip(fetched, late)]
        return cat_cols(g_po), g_do.reshape(-1, g_do.shape[-1]), g_o.reshape(-1, g_o.shape[-1])

    mix_full = g_mix.reshape(4, POOL_GROUPS, gdim // 4, gdim).transpose(1, 0, 2, 3).reshape(POOL_GROUPS, gdim, gdim)
    smalls = [_unpack(g_small[j].reshape(-1, d)[:small_rows], [meta_tokens, conv_w[0]], d) for j in range(4)]
    meta_full = jnp.concatenate([s[0] for s in smalls], axis=1)
    conv_full = jnp.concatenate([s[1] for s in smalls], axis=1)

    names = ["w_in", "w_po", "w_do", "w_o", "mix"]
    kinds = ["cols", "lead", "lead", "lead", "lead"]

    def grads_to_chips(g_w_main, g_w_ba, g_w_po, g_w_do, g_w_o_full, g_mix_full):
        def grad_rows(lo, hi):
            segs = [(0, n_main, g_w_main, 0), (n_main, n_main + n_ba, g_w_ba, 0), (n_main + n_ba, 4 * sw, g_w_main, n_main)]
            cut = [(max(lo, s0), min(hi, s1), s0, arr, off) for s0, s1, arr, off in segs]
            return [arr[a - s0 + off:b - s0 + off] for a, b, s0, arr, off in cut if a < b]

        in_parts = jnp.stack([jnp.concatenate(grad_rows(j * sw, (j + 1) * sw), axis=0) for j in range(4)])
        col_parts = lambda g: g.reshape(2, g.shape[0] // 2, 4, g.shape[1] // 4).transpose(0, 2, 1, 3)
        row_parts = lambda g: g.reshape(4, 2, g.shape[0] // 8, g.shape[1]).transpose(1, 0, 2, 3)
        mix_rows = POOL_GROUPS * (gdim // 4)
        mix_parts = (g_mix_full.astype(bf16).reshape(POOL_GROUPS, 4, gdim // 4, gdim).transpose(1, 0, 2, 3)
                     .reshape(4, 2, mix_rows // 2, gdim).transpose(1, 0, 2, 3))
        parts = [in_parts, col_parts(g_w_po), row_parts(g_w_do), row_parts(g_w_o_full), mix_parts]
        from_sibling = send_grad_halves(parts, kinds)
        chip_sums = [add_halves("add_" + nm, p, r, half, k) for nm, p, r, k in zip(names, parts, from_sibling, kinds)]
        return hosted_scatter_first(chip_sums), chip_sums

    (loss, dh, g_norm_w, _, _, g_conv, g_a_log, g_dt, _, g_pool_scale, g_dn_norm_w, _, _, _, g_final_w,
     (chip_sums, from_near)) = local_step(x[0], meta_full, norm_w, w_main, w_ba, conv_full, A_log, dt_bias, mix_full, pool_scale,
                                          dn_norm_w, (hosted_gather(late, ["lead"] * 3), late_weights), final_norm_w.reshape(1, d),
                                          loss_target[0], grad_hook=grads_to_chips)
    grad_x = dh[FRONT_PAD + N_META:][None]
    g_meta = dh[FRONT_PAD:FRONT_PAD + N_META]

    far_chip = 2 * (lax.axis_index("x") ^ core) + (lax.axis_index("y") ^ (1 - core))
    chips = jnp.stack([chip, far_chip]).astype(jnp.int32)
    kept, passed = zip(*[add_first("add1_" + nm, p, g, chips) for nm, p, g in zip(names, chip_sums, from_near)])
    small_like = [loss, g_norm_w, g_a_log, g_dt, g_pool_scale, g_dn_norm_w, g_final_w, g_conv, g_meta]
    pack = _pack(small_like, d)
    from_far, packs = scatter_second_and_small(list(passed), pack)
    mine = [add_second("add2_" + nm, k_, g) for nm, k_, g in zip(names, kept, from_far)]
    theirs = swap_with_sibling("swap_grad_halves", mine)
    total = sum_leading("sum_small", lax.dynamic_update_slice(packs, pack[None], (me, 0, 0)))
    (loss_t, g_norm_w, g_a_log, g_dt, g_pool_scale, g_dn_norm_w, g_final_w, g_conv, g_meta) = _unpack(total, small_like, d)
    loss_out = loss_t[0, 0]
    g_conv_s = lax.dynamic_slice_in_dim(g_conv, chip * (g_conv.shape[1] // 4), g_conv.shape[1] // 4, axis=1)
    g_meta_s = lax.dynamic_slice_in_dim(g_meta, chip * (d // 4), d // 4, axis=1)

    weights = [meta_tokens, norm_w, w_in, conv_w, A_log, dt_bias, pool_mix, pool_scale, dn_norm_w, w_pool_out, w_dn_out, w_o, final_norm_w]
    ms = [m_meta_tokens, m_norm_w, m_w_in, m_conv_w, m_A_log, m_dt_bias, m_pool_mix, m_pool_scale, m_dn_norm_w, m_w_pool_out, m_w_dn_out, m_w_o, m_final_norm_w]
    vs = [v_meta_tokens, v_norm_w, v_w_in, v_conv_w, v_A_log, v_dt_bias, v_pool_mix, v_pool_scale, v_dn_norm_w, v_w_pool_out, v_w_dn_out, v_w_o, v_final_norm_w]
    grads = [g_meta_s, g_norm_w, None, g_conv_s[None], g_a_log, g_dt, None, g_pool_scale, g_dn_norm_w, None, None, None, g_final_w.reshape(d)]
    deltas, new_ms, new_vs = [None] * 13, [None] * 13, [None] * 13
    big = [2, 9, 10, 11, 6]
    for i, nm, g_mine, g_theirs in zip(big, names, mine, theirs):
        if nm == "w_in":
            to2d, back, axis = (lambda t: t[0].T), (lambda t: t.T[None]), 1
        else:
            to2d, back, axis = (lambda t: t.reshape(-1, t.shape[-1])), (lambda t, i=i: t.reshape(weights[i].shape)), 0
        res = adamw_joined("adamw_" + nm, to2d(weights[i]), g_mine, g_theirs, to2d(ms[i]), to2d(vs[i]), half, axis)
        grads[i], deltas[i], new_ms[i], new_vs[i] = [back(t) for t in res]
    small_idx = [i for i in range(13) if i not in big]
    packs = [_pack([arrs[i] for i in small_idx], d) for arrs in (weights, grads, ms, vs)]
    outs = adamw("adamw_small", *packs)
    like = [weights[i] for i in small_idx]
    for res, dest in zip(outs, (deltas, new_ms, new_vs)):
        for i, val in zip(small_idx, _unpack(res, like, d)):
            dest[i] = val
    return (loss_out, grad_x, *grads, *deltas, *new_ms, *new_vs)
```

```python
import functools

import jax
import jax.numpy as jnp
from jax import lax
from jax.experimental import pallas as pl
from jax.experimental.pallas import tpu as pltpu

f32 = jnp.float32
bf16 = jnp.bfloat16
HIGHEST = lax.Precision.HIGHEST
MESH = pl.DeviceIdType.MESH

N_META = 16
CHUNK = 64
FRONT_PAD = (-N_META) % CHUNK
HEAD_DIM = 128
POOL_GROUPS = 4
POOL_WINDOWS = (2, 4, 8, 16)
CONV_WIDTH = 4
NORM_EPS = 1e-6
ADAM_LR, ADAM_B1, ADAM_B2, ADAM_EPS, ADAM_WD, ADAM_STEP = 0.001, 0.9, 0.999, 1e-08, 0.01, 10
LANES = 128
VMEM_LIMIT = 56 * 2**20


def _pc(body, **kw):
    return pl.pallas_call(body, **kw)


def _params(*sem, **kw):
    return pltpu.CompilerParams(dimension_semantics=sem or None, vmem_limit_bytes=VMEM_LIMIT, **kw)


def _dg(a, b, dims, prec=None):
    return lax.dot_general(a, b, (dims, ((), ())), precision=prec, preferred_element_type=f32)


@jax.custom_vjp
def mm_nn(a, b):
    return _dg(a.astype(bf16), b.astype(bf16), ((1,), (0,)))


@jax.custom_vjp
def mm_nt(a, b):
    return _dg(a.astype(bf16), b.astype(bf16), ((1,), (1,)))


@jax.custom_vjp
def mm_tn(a, b):
    return _dg(a.astype(bf16).T, b.astype(bf16), ((1,), (0,)))


mm_nn.defvjp(lambda a, b: (mm_nn(a, b), (a, b)), lambda r, dy: (mm_nt(dy, r[1]), mm_tn(r[0], dy)))
mm_nt.defvjp(lambda a, b: (mm_nt(a, b), (a, b)), lambda r, dy: (mm_nn(dy, r[1]), mm_tn(dy, r[0])))
mm_tn.defvjp(lambda a, b: (mm_tn(a, b), (a, b)), lambda r, dy: (mm_nt(r[1], dy), mm_nn(r[0], dy)))


def _split3(x):
    hi = x.astype(bf16)
    r1 = x - hi.astype(f32)
    mid = r1.astype(bf16)
    lo = (r1 - mid.astype(f32)).astype(bf16)
    return hi, mid, lo


@jax.custom_vjp
def mm_sel(sel, x):
    s = sel.astype(bf16)
    d = ((1,), (0,))
    hi, mid, lo = _split3(x)
    return _dg(s, hi, d) + _dg(s, mid, d) + _dg(s, lo, d)


def _mm_sel_bwd(sel, dy):
    s = sel.astype(bf16)
    d = ((0,), (0,))
    hi, mid, lo = _split3(dy)
    return jnp.zeros_like(sel), _dg(s, hi, d) + _dg(s, mid, d) + _dg(s, lo, d)


mm_sel.defvjp(lambda sel, x: (mm_sel(sel, x), sel), _mm_sel_bwd)


@jax.custom_vjp
def mm_pick(x, sel):
    s = sel.astype(bf16)
    d = ((1,), (0,))
    hi, mid, lo = _split3(x)
    return _dg(hi, s, d) + _dg(mid, s, d) + _dg(lo, s, d)


def _mm_pick_bwd(sel, dy):
    s = sel.astype(bf16)
    d = ((1,), (1,))
    hi, mid, lo = _split3(dy)
    return _dg(hi, s, d) + _dg(mid, s, d) + _dg(lo, s, d), jnp.zeros_like(sel)


mm_pick.defvjp(lambda x, sel: (mm_pick(x, sel), sel), _mm_pick_bwd)


def tri_inv(ls):
    n = ls[0].shape[0]
    eye = (lax.broadcasted_iota(jnp.int32, (n, n), 0) == lax.broadcasted_iota(jnp.int32, (n, n), 1)).astype(f32)
    ms = [-l for l in ls]
    ts = [eye + m for m in ms]
    k = 1
    while 2 * k < CHUNK:
        ms = [mm_nn(m, m) for m in ms]
        ts = [t + mm_nn(t, m) for t, m in zip(ts, ms)]
        k *= 2
    return ts


@functools.partial(jax.custom_vjp, nondiff_argnums=(1,))
def shift_rows(x, j):
    n = x.shape[0]
    rows = lax.broadcasted_iota(jnp.int32, x.shape, 0)
    if j >= 0:
        return jnp.where(rows >= j, pltpu.roll(x, j, 0), 0.0)
    return jnp.where(rows < n + j, pltpu.roll(x, n + j, 0), 0.0)


shift_rows.defvjp(lambda x, j: (shift_rows(x, j), None), lambda j, _, dy: (shift_rows(dy, -j),))


def sigmoid(x):
    return 1.0 / (1.0 + jnp.exp(-x))


def silu(x):
    return x * sigmoid(x)


def softplus(x):
    return jnp.maximum(x, 0.0) + jnp.log(1.0 + jnp.exp(-jnp.abs(x)))


def rmsnorm(x, w):
    return x * lax.rsqrt(jnp.mean(x * x, axis=-1, keepdims=True) + NORM_EPS) * w


def l2norm(x):
    return x * lax.rsqrt(jnp.sum(x * x, axis=-1, keepdims=True) + NORM_EPS)


def pool_fn(u, zp, mix, scale, group):
    rows = lax.broadcasted_iota(jnp.int32, u.shape, 0)
    sums = []
    s, w = u, 1
    while w < POOL_WINDOWS[-1]:
        s = s + shift_rows(s, w)
        w *= 2
        sums.append(s)
    total = sums[-1]
    for gi in range(POOL_GROUPS - 2, -1, -1):
        total = jnp.where(group == gi, sums[gi], total)
    window = jnp.left_shift(2, group)
    cnt = jnp.clip(rows - (FRONT_PAD - 1), 1, window).astype(f32)
    pooled = total / cnt - u
    return mm_nn(pooled, mix) * scale * silu(zp)


def conv_silu(x, w):
    k = CONV_WIDTH
    y = x * w[k - 1:k, :]
    for kk in range(k - 1):
        y = y + shift_rows(x, k - 1 - kk) * w[kk:kk + 1, :]
    return silu(y)


def _lane_pick(row, idx):
    lanes = lax.broadcasted_iota(jnp.int32, row.shape, 1)
    return jnp.sum(jnp.where(lanes == idx, row, 0.0), axis=1, keepdims=True)


def dn_pre_fn(qr, kr, vr, ba, cwq, cwk, cwv, a_log, dt_bias, head, n_heads):
    q = l2norm(conv_silu(qr, cwq)) * (HEAD_DIM ** -0.5)
    k = l2norm(conv_silu(kr, cwk))
    v = conv_silu(vr, cwv)
    r = lax.broadcasted_iota(jnp.int32, (LANES, LANES), 0)
    b_b = mm_pick(ba, (r == head).astype(f32))
    a_b = mm_pick(ba, (r == head + n_heads).astype(f32))
    real = lax.broadcasted_iota(jnp.int32, qr.shape, 0) >= FRONT_PAD
    beta_b = jnp.where(real, sigmoid(b_b), 0.0)
    g_b = jnp.where(real, -jnp.exp(_lane_pick(a_log, head)) * softplus(a_b + _lane_pick(dt_bias, head)), 0.0)
    return q, k, v, beta_b, g_b


def _chunk_masks(rows):
    r = lax.broadcasted_iota(jnp.int32, (rows, rows), 0)
    c = lax.broadcasted_iota(jnp.int32, (rows, rows), 1)
    same = (r // CHUNK) == (c // CHUNK)
    return same, jnp.logical_and(same, r >= c), jnp.logical_and(same, r > c)


def _lane0(rows):
    return (lax.broadcasted_iota(jnp.int32, (rows, LANES), 1) == 0).astype(bf16)


@jax.custom_vjp
def lane0_as_row(x):
    sel = _lane0(x.shape[0])
    d = ((1,), (1,))
    hi, mid, lo = _split3(x)
    return _dg(sel, hi, d) + _dg(sel, mid, d) + _dg(sel, lo, d)


def _lane0_as_row_bwd(rows, dy):
    sel = _lane0(rows)
    d = ((0,), (0,))
    hi, mid, lo = _split3(dy)
    return (_dg(hi, sel, d) + _dg(mid, sel, d) + _dg(lo, sel, d),)


lane0_as_row.defvjp(lambda x: (lane0_as_row(x), x.shape[0]), _lane0_as_row_bwd)


def gate_fn(g_b):
    rows = g_b.shape[0]
    same, causal, _ = _chunk_masks(rows)
    gcum_b = mm_sel(causal.astype(f32), g_b)
    glast_b = mm_sel(same.astype(f32), g_b)
    g_rows = jnp.broadcast_to(gcum_b[:, :1], (rows, rows))
    decay = jnp.where(causal, jnp.exp(jnp.where(causal, g_rows - lane0_as_row(gcum_b), 0.0)), 0.0)
    return decay, jnp.exp(gcum_b), jnp.exp(glast_b - gcum_b), jnp.exp(glast_b)


def _fold_matrix(rows):
    r = lax.broadcasted_iota(jnp.int32, (rows, LANES), 0)
    c = lax.broadcasted_iota(jnp.int32, (rows, LANES), 1)
    return (r % CHUNK == c).astype(bf16)


@jax.custom_vjp
def fold_chunks(x):
    return _dg(x.astype(bf16), _fold_matrix(x.shape[0]), ((1,), (0,)))


def _fold_chunks_bwd(rows, dy):
    fold = _fold_matrix(rows)
    d = ((1,), (1,))
    hi, mid, lo = _split3(dy)
    return (_dg(hi, fold, d) + _dg(mid, fold, d) + _dg(lo, fold, d),)


fold_chunks.defvjp(lambda x: (fold_chunks(x), x.shape[0]), _fold_chunks_bwd)


def lmat_fn(k, beta_b, decay):
    _, _, strict = _chunk_masks(k.shape[0])
    return jnp.where(strict, mm_nt(k * beta_b, k) * decay, 0.0)


def intra_fn(tmat, q, k, v, beta_b, decay, eg, kfac):
    _, causal, _ = _chunk_masks(q.shape[0])
    k_beta = k * beta_b
    u_c = mm_nn(tmat, v * beta_b)
    w_c = mm_nn(tmat, k_beta * eg)
    qk = jnp.where(causal, mm_nt(q, k) * decay, 0.0)
    return u_c, w_c, q * eg, k * kfac, fold_chunks(qk)


def gated_norm(o, norm_w, zd):
    return rmsnorm(o, norm_w) * silu(zd)


def loss_fn(o, w, tgt):
    err = rmsnorm(o, w) - tgt
    return 0.5 * jnp.sum(jnp.mean(err * err, axis=-1))


_ANY = pl.BlockSpec(memory_space=pl.ANY)


class Hosted:
    def __init__(self, arrays, out_shapes, sems, stages):
        self.arrays, self.out_shapes, self.sems, self.stages = list(arrays), list(out_shapes), list(sems), stages

    def __add__(self, other):
        ni, no, ns = len(self.arrays), len(self.out_shapes), len(self.sems)
        stages = lambda i, o, s: self.stages(i[:ni], o[:no], s[:ns]) + other.stages(i[ni:], o[no:], s[ns:])
        return Hosted(self.arrays + other.arrays, self.out_shapes + other.out_shapes, self.sems + other.sems, stages)


def matmul(name, a, b, *, ta=False, tb=False, tm, tn, tk=None, extras=(), epi=None, out_dtypes=(f32,), hosted=None):
    m, k = (a.shape[1], a.shape[0]) if ta else a.shape
    n = b.shape[0] if tb else b.shape[1]
    tm, tn, tk = min(tm, m), min(tn, n), min(tk or k, k)
    assert m % tm == 0 and n % tn == 0 and k % tk == 0, (name, m, n, k, tm, tn, tk)
    nm, nn, nk = m // tm, n // tn, k // tk
    a_spec = pl.BlockSpec((tk, tm), lambda i, j, kk: (kk, i)) if ta else pl.BlockSpec((tm, tk), lambda i, j, kk: (i, kk))
    b_spec = pl.BlockSpec((tn, tk), lambda i, j, kk: (j, kk)) if tb else pl.BlockSpec((tk, tn), lambda i, j, kk: (kk, j))
    ex_specs = []
    for _, off in extras:
        assert off % tn == 0, (name, off, tn)
        ex_specs.append(pl.BlockSpec((tm, tn), functools.partial(lambda i, j, kk, o: (i, o + j), o=off // tn)))
    n_ex, n_out = len(extras), len(out_dtypes)
    dims = ((0 if ta else 1,), (1 if tb else 0,))
    n_hin = len(hosted.arrays) if hosted else 0
    n_hout = len(hosted.out_shapes) if hosted else 0
    n_sem = len(hosted.sems) if hosted else 0

    def body(a_ref, b_ref, *rest):
        ex_refs, rest = rest[:n_ex], rest[n_ex:]
        hin_refs, rest = rest[:n_hin], rest[n_hin:]
        out_refs, rest = rest[:n_out], rest[n_out:]
        hout_refs, rest = rest[:n_hout], rest[n_hout:]
        sem_refs = rest[len(rest) - n_sem:] if n_sem else ()
        step = (pl.program_id(0) * nn + pl.program_id(1)) * nk + pl.program_id(2)
        stages = hosted.stages(hin_refs, hout_refs, sem_refs) if hosted else []
        for frac, emit in stages:
            if frac < 1.0:
                pl.when(step == int(frac * (nm * nn * nk - 1)))(emit)

        def finish(acc):
            res = epi(acc, *[r[...] for r in ex_refs]) if epi is not None else (acc,)
            for o_ref, val in zip(out_refs, res):
                o_ref[...] = val.astype(o_ref.dtype)

        p = _dg(a_ref[...], b_ref[...], dims)
        if nk == 1:
            finish(p)
        else:
            acc_ref = rest[0]
            kk = pl.program_id(2)

            @pl.when(kk == 0)
            def _():
                acc_ref[...] = p

            @pl.when(kk > 0)
            def _():
                acc_ref[...] += p

            @pl.when(kk == nk - 1)
            def _():
                finish(acc_ref[...])

        for frac, emit in stages:
            if frac >= 1.0:
                pl.when(step == nm * nn * nk - 1)(emit)

    outs = _pc(
        body, name=name, grid=(nm, nn, nk),
        in_specs=[a_spec, b_spec] + ex_specs + [_ANY] * n_hin,
        out_specs=[pl.BlockSpec((tm, tn), lambda i, j, kk: (i, j))] * n_out + [_ANY] * n_hout,
        out_shape=[jax.ShapeDtypeStruct((m, n), dt) for dt in out_dtypes] + (hosted.out_shapes if hosted else []),
        scratch_shapes=([pltpu.VMEM((tm, tn), f32)] if nk > 1 else []) + (hosted.sems if hosted else []),
        compiler_params=_params(*(("arbitrary",) * 3 if hosted else ("parallel", "parallel", "arbitrary"))),
    )(a, b, *[e for e, _ in extras], *(hosted.arrays if hosted else []))
    return outs[0] if len(outs) == 1 else outs


def _row_tile(rows, cols, n_arrays, itemsize=4, budget=24 * 2**20):
    best = None
    for t in range(16, rows + 1, 16):
        if rows % t == 0 and 2 * n_arrays * t * cols * itemsize <= budget:
            best = t
    return best or rows


def _tile(rows, cols, n_arrays, budget=24 * 2**20):
    if rows % 16 == 0 or cols % LANES != 0:
        return _row_tile(rows, cols, n_arrays, budget=budget), cols
    fits = [t for t in range(LANES, cols + 1, LANES) if cols % t == 0 and 2 * n_arrays * rows * t * 4 <= budget]
    return rows, (max(fits) if fits else LANES)


def cast_bf16(name, x):
    rows, cols = x.shape
    tr, tc = _tile(rows, cols, 2)

    def body(x_ref, o_ref):
        o_ref[...] = x_ref[...].astype(bf16)

    blk = pl.BlockSpec((tr, tc), lambda i, j: (i, j))
    return _pc(body, name=name, grid=(rows // tr, cols // tc), in_specs=[blk], out_specs=blk,
               out_shape=jax.ShapeDtypeStruct(x.shape, bf16), compiler_params=_params("parallel", "parallel"))(x)


def norm_fwd(hp, norm_w, tm):
    tp, d = hp.shape

    def body(h_ref, w_ref, o_ref):
        o_ref[...] = rmsnorm(h_ref[...], w_ref[...]).astype(bf16)

    return _pc(body, name="norm_fwd", grid=(tp // tm,),
               in_specs=[pl.BlockSpec((tm, d), lambda i: (i, 0)), pl.BlockSpec((1, d), lambda i: (0, 0))],
               out_specs=pl.BlockSpec((tm, d), lambda i: (i, 0)), out_shape=jax.ShapeDtypeStruct((tp, d), bf16),
               compiler_params=_params("parallel"))(hp, norm_w)


def norm_bwd(hp, norm_w, dxn, dout, tm):
    tp, d = hp.shape

    def body(h_ref, w_ref, dxn_ref, dout_ref, dh_ref, dw_ref):
        _, vjp = jax.vjp(rmsnorm, h_ref[...], w_ref[...])
        dh, dw = vjp(dxn_ref[...])
        dh_ref[...] = dh + dout_ref[...]

        @pl.when(pl.program_id(0) == 0)
        def _():
            dw_ref[...] = jnp.zeros_like(dw_ref)

        dw_ref[...] += dw

    row = pl.BlockSpec((tm, d), lambda i: (i, 0))
    vec = pl.BlockSpec((1, d), lambda i: (0, 0))
    return _pc(body, name="norm_bwd", grid=(tp // tm,), in_specs=[row, vec, row, row], out_specs=[row, vec],
               out_shape=[jax.ShapeDtypeStruct((tp, d), f32), jax.ShapeDtypeStruct((1, d), f32)],
               compiler_params=_params("arbitrary"))(hp, norm_w, dxn, dout)


def pool_fwd(proj, mix, scale, pw):
    tp = proj.shape[0]
    g = pw // POOL_GROUPS

    def body(u_ref, z_ref, mix_ref, s_ref, y_ref):
        y_ref[...] = pool_fn(u_ref[...], z_ref[...], mix_ref[0], s_ref[...], pl.program_id(0)).astype(bf16)

    return _pc(body, name="pool_fwd", grid=(POOL_GROUPS,),
               in_specs=[pl.BlockSpec((tp, g), lambda i: (0, i)), pl.BlockSpec((tp, g), lambda i: (0, POOL_GROUPS + i)),
                         pl.BlockSpec((1, g, g), lambda i: (i, 0, 0)), pl.BlockSpec((1, g), lambda i: (0, i))],
               out_specs=pl.BlockSpec((tp, g), lambda i: (0, i)), out_shape=jax.ShapeDtypeStruct((tp, pw), bf16),
               compiler_params=_params("parallel"))(proj, proj, mix, scale)


def pool_bwd(proj, mix, scale, dy, pw):
    tp = proj.shape[0]
    g = pw // POOL_GROUPS

    def body(u_ref, z_ref, mix_ref, s_ref, dy_ref, du_ref, dz_ref, dmix_ref, ds_ref):
        grp = pl.program_id(0)
        _, vjp = jax.vjp(lambda u, z, m, s: pool_fn(u, z, m, s, grp), u_ref[...], z_ref[...], mix_ref[0].astype(f32), s_ref[...])
        du, dz, dmix, ds = vjp(dy_ref[...])
        du_ref[...] = du.astype(bf16)
        dz_ref[...] = dz.astype(bf16)
        dmix_ref[0] = dmix
        ds_ref[...] = ds

    col = pl.BlockSpec((tp, g), lambda i: (0, i))
    return _pc(body, name="pool_bwd", grid=(POOL_GROUPS,),
               in_specs=[col, pl.BlockSpec((tp, g), lambda i: (0, POOL_GROUPS + i)),
                         pl.BlockSpec((1, g, g), lambda i: (i, 0, 0)), pl.BlockSpec((1, g), lambda i: (0, i)), col],
               out_specs=[col, col, pl.BlockSpec((1, g, g), lambda i: (i, 0, 0)), pl.BlockSpec((1, g), lambda i: (0, i))],
               out_shape=[jax.ShapeDtypeStruct((tp, pw), bf16), jax.ShapeDtypeStruct((tp, pw), bf16),
                          jax.ShapeDtypeStruct((POOL_GROUPS, g, g), f32), jax.ShapeDtypeStruct((1, pw), f32)],
               compiler_params=_params("parallel"))(proj, proj, mix, scale, dy)


def _dn_pre_specs(tp, n_heads, q_off):
    hb = lambda off: pl.BlockSpec((tp, HEAD_DIM), functools.partial(lambda h, o: (0, o + h), o=off))
    cw = lambda off: pl.BlockSpec((CONV_WIDTH, HEAD_DIM), functools.partial(lambda h, o: (0, o + h), o=off))
    whole = lambda shape: pl.BlockSpec(shape, lambda h: (0, 0))
    return ([hb(q_off), hb(q_off + n_heads), hb(q_off + 2 * n_heads), whole((tp, LANES)),
             cw(0), cw(n_heads), cw(2 * n_heads), whole((1, LANES)), whole((1, LANES))], hb, cw, whole)


def dn_pre_fwd(proj, ba, conv_w, a_log, dt_bias, n_heads, q_off):
    tp = proj.shape[0]
    in_specs, hb, _, _ = _dn_pre_specs(tp, n_heads, q_off)

    def body(q_ref, k_ref, v_ref, ba_ref, cq_ref, ck_ref, cv_ref, al_ref, dt_ref, *out_refs):
        outs = dn_pre_fn(q_ref[...], k_ref[...], v_ref[...], ba_ref[...], cq_ref[...], ck_ref[...], cv_ref[...],
                         al_ref[...], dt_ref[...], pl.program_id(0), n_heads)
        for o_ref, val in zip(out_refs, outs):
            o_ref[...] = val

    return _pc(body, name="dn_pre_fwd", grid=(n_heads,), in_specs=in_specs, out_specs=[hb(0)] * 5,
               out_shape=[jax.ShapeDtypeStruct((tp, n_heads * HEAD_DIM), f32)] * 5,
               compiler_params=_params("parallel"))(proj, proj, proj, ba, conv_w, conv_w, conv_w, a_log, dt_bias)


def dn_pre_bwd(proj, ba, conv_w, a_log, dt_bias, cots, n_heads, q_off):
    tp = proj.shape[0]
    in_specs, hb, cw, whole = _dn_pre_specs(tp, n_heads, q_off)

    def body(q_ref, k_ref, v_ref, ba_ref, cq_ref, ck_ref, cv_ref, al_ref, dt_ref, dq_ref, dk_ref, dv_ref, db_ref, dg_ref,
             dqr_ref, dkr_ref, dvr_ref, dba_ref, dcq_ref, dck_ref, dcv_ref, dal_ref, ddt_ref):
        head = pl.program_id(0)
        fn = lambda *args: dn_pre_fn(*args, head, n_heads)
        _, vjp = jax.vjp(fn, q_ref[...], k_ref[...], v_ref[...], ba_ref[...], cq_ref[...], ck_ref[...], cv_ref[...],
                         al_ref[...], dt_ref[...])
        dqr, dkr, dvr, dba, dcq, dck, dcv, dal, ddt = vjp((dq_ref[...], dk_ref[...], dv_ref[...], db_ref[...], dg_ref[...]))
        dqr_ref[...] = dqr.astype(bf16)
        dkr_ref[...] = dkr.astype(bf16)
        dvr_ref[...] = dvr.astype(bf16)
        dcq_ref[...] = dcq
        dck_ref[...] = dck
        dcv_ref[...] = dcv

        @pl.when(head == 0)
        def _():
            dba_ref[...] = jnp.zeros_like(dba_ref)
            dal_ref[...] = jnp.zeros_like(dal_ref)
            ddt_ref[...] = jnp.zeros_like(ddt_ref)

        dba_ref[...] += dba
        dal_ref[...] += dal
        ddt_ref[...] += ddt

    w = n_heads * HEAD_DIM
    return _pc(body, name="dn_pre_bwd", grid=(n_heads,), in_specs=in_specs + [hb(0)] * 5,
               out_specs=[hb(0)] * 3 + [whole((tp, LANES)), cw(0), cw(0), cw(0), whole((1, LANES)), whole((1, LANES))],
               out_shape=[jax.ShapeDtypeStruct((tp, w), bf16)] * 3 + [jax.ShapeDtypeStruct((tp, LANES), f32)]
               + [jax.ShapeDtypeStruct((CONV_WIDTH, w), f32)] * 3 + [jax.ShapeDtypeStruct((1, LANES), f32)] * 2,
               compiler_params=_params("arbitrary"))(proj, proj, proj, ba, conv_w, conv_w, conv_w, a_log, dt_bias, *cots)


def _super_rows(tp):
    n = tp // CHUNK
    return CHUNK * max(j for j in (4, 3, 2, 1) if n % j == 0)


def _heads_per_step(n_heads):
    return max(j for j in (4, 2, 1) if n_heads % j == 0)


def dn_intra_fwd(q, k, v, beta_b, g_b, n_heads):
    tp = q.shape[0]
    rows = _super_rows(tp)
    ns = tp // rows

    hps = _heads_per_step(n_heads)

    def body(q_ref, k_ref, v_ref, b_ref, g_ref, u_ref, w_ref, qd_ref, kd_ref, qk_ref, gl_ref, t_ref):
        lanes = [slice(i * HEAD_DIM, (i + 1) * HEAD_DIM) for i in range(hps)]
        gates = [gate_fn(g_ref[:, sl]) for sl in lanes]
        tmats = tri_inv([lmat_fn(k_ref[:, sl], b_ref[:, sl], gt[0]) for sl, gt in zip(lanes, gates)])
        for i, (sl, (decay, eg, kfac, gl), tmat) in enumerate(zip(lanes, gates, tmats)):
            u_c, w_c, q_dec, k_dec, qk_c = intra_fn(tmat, q_ref[:, sl], k_ref[:, sl], v_ref[:, sl], b_ref[:, sl], decay, eg, kfac)
            u_ref[:, sl] = u_c
            w_ref[:, sl] = w_c
            qd_ref[:, sl] = q_dec
            kd_ref[:, sl] = k_dec
            qk_ref[:, sl] = qk_c
            gl_ref[:, sl] = gl
            t_ref[i, 0] = tmat

    blk = pl.BlockSpec((rows, hps * HEAD_DIM), lambda h, s: (s, h))
    return _pc(body, name="dn_intra_fwd", grid=(n_heads // hps, ns), in_specs=[blk] * 5,
               out_specs=[blk] * 6 + [pl.BlockSpec((hps, 1, rows, rows), lambda h, s: (h, s, 0, 0))],
               out_shape=[jax.ShapeDtypeStruct(q.shape, f32)] * 6 + [jax.ShapeDtypeStruct((n_heads, ns, rows, rows), f32)],
               compiler_params=_params("parallel", "parallel"))(q, k, v, beta_b, g_b)


def dn_intra_bwd(q, k, v, beta_b, g_b, tmats, cots, n_heads):
    tp = q.shape[0]
    rows = _super_rows(tp)
    ns = tp // rows

    hps = _heads_per_step(n_heads)

    def body(q_ref, k_ref, v_ref, b_ref, g_ref, t_ref, du_ref, dw_ref, dqd_ref, dkd_ref, dqk_ref, dgl_ref,
             dq_ref, dk_ref, dv_ref, db_ref, dg_ref):
        lanes = [slice(i * HEAD_DIM, (i + 1) * HEAD_DIM) for i in range(hps)]
        tmats = [t_ref[i, 0] for i in range(hps)]
        gates = [jax.vjp(gate_fn, g_ref[:, sl]) for sl in lanes]
        intra = [jax.vjp(intra_fn, tmat, q_ref[:, sl], k_ref[:, sl], v_ref[:, sl], b_ref[:, sl], gt[0][0], gt[0][1], gt[0][2])[1](
            (du_ref[:, sl], dw_ref[:, sl], dqd_ref[:, sl], dkd_ref[:, sl], dqk_ref[:, sl]))
            for sl, tmat, gt in zip(lanes, tmats, gates)]
        tts = [tmat.T for tmat in tmats]
        dls = [mm_nn(tt, res[0]) for tt, res in zip(tts, intra)]
        dls = [-mm_nn(dl, tt) for dl, tt in zip(dls, tts)]
        for sl, gt, res, dl in zip(lanes, gates, intra, dls):
            _, dq, dk, dv, db, ddecay, deg, dkfac = res
            dk2, db2, ddecay2 = jax.vjp(lmat_fn, k_ref[:, sl], b_ref[:, sl], gt[0][0])[1](dl)
            (dg,) = gt[1]((ddecay + ddecay2, deg, dkfac, dgl_ref[:, sl]))
            dq_ref[:, sl] = dq
            dk_ref[:, sl] = dk + dk2
            dv_ref[:, sl] = dv
            db_ref[:, sl] = db + db2
            dg_ref[:, sl] = dg

    blk = pl.BlockSpec((rows, hps * HEAD_DIM), lambda h, s: (s, h))
    return _pc(body, name="dn_intra_bwd", grid=(n_heads // hps, ns),
               in_specs=[blk] * 5 + [pl.BlockSpec((hps, 1, rows, rows), lambda h, s: (h, s, 0, 0))] + [blk] * 6,
               out_specs=[blk] * 5, out_shape=[jax.ShapeDtypeStruct(q.shape, f32)] * 5,
               compiler_params=_params("parallel", "parallel"))(q, k, v, beta_b, g_b, tmats, *cots)


def dn_seq_fwd(inter, proj, dn_norm_w, n_heads, zd_off):
    tp, width = inter[0].shape
    n_chunks = tp // CHUNK

    def body(u_ref, w_ref, qd_ref, kd_ref, qk_ref, gl_ref, z_ref, nw_ref, y_ref, s_ref, state):
        @pl.when(pl.program_id(0) == 0)
        def _():
            state[...] = jnp.zeros_like(state)

        lanes = [slice(h * HEAD_DIM, (h + 1) * HEAD_DIM) for h in range(n_heads)]
        sts = [state[h] for h in range(n_heads)]
        for h, st in enumerate(sts):
            s_ref[0, h] = st
        v_new = [u_ref[:, sl] - mm_nn(w_ref[:, sl], st) for sl, st in zip(lanes, sts)]
        outs = [mm_nn(qd_ref[:, sl], st) + mm_nn(qk_ref[:, sl][:, :CHUNK], vn) for sl, st, vn in zip(lanes, sts, v_new)]
        for h, (sl, st, vn) in enumerate(zip(lanes, sts, v_new)):
            state[h] = st * gl_ref[0:1, sl] + mm_tn(kd_ref[:, sl], vn)
        for sl, o in zip(lanes, outs):
            y_ref[:, sl] = gated_norm(o, nw_ref[...], z_ref[:, sl]).astype(bf16)

    blk = pl.BlockSpec((CHUNK, width), lambda n: (n, 0))
    return _pc(body, name="dn_seq_fwd", grid=(n_chunks,),
               in_specs=[blk] * 6 + [pl.BlockSpec((CHUNK, width), lambda n: (n, zd_off)), pl.BlockSpec((1, HEAD_DIM), lambda n: (0, 0))],
               out_specs=[blk, pl.BlockSpec((1, n_heads, HEAD_DIM, HEAD_DIM), lambda n: (n, 0, 0, 0))],
               out_shape=[jax.ShapeDtypeStruct((tp, width), bf16), jax.ShapeDtypeStruct((n_chunks, n_heads, HEAD_DIM, HEAD_DIM), f32)],
               scratch_shapes=[pltpu.VMEM((n_heads, HEAD_DIM, HEAD_DIM), f32)],
               compiler_params=_params("arbitrary"))(*inter, proj, dn_norm_w)


def dn_seq_bwd(inter, proj, dn_norm_w, states, dy, n_heads, zd_off):
    tp, width = inter[0].shape
    n_chunks = tp // CHUNK
    last = n_chunks - 1

    def body(u_ref, w_ref, qd_ref, kd_ref, qk_ref, gl_ref, z_ref, nw_ref, s_ref, dy_ref,
             du_ref, dw_ref, dqd_ref, dkd_ref, dqk_ref, dgl_ref, dz_ref, dnw_ref, dstate):
        @pl.when(pl.program_id(0) == 0)
        def _():
            dstate[...] = jnp.zeros_like(dstate)
            dnw_ref[...] = jnp.zeros_like(dnw_ref)

        lanes = [slice(h * HEAD_DIM, (h + 1) * HEAD_DIM) for h in range(n_heads)]
        sts = [s_ref[0, h] for h in range(n_heads)]
        dsts = [dstate[h] for h in range(n_heads)]
        v_new = [u_ref[:, sl] - mm_nn(w_ref[:, sl], st) for sl, st in zip(lanes, sts)]
        outs = [mm_nn(qd_ref[:, sl], st) + mm_nn(qk_ref[:, sl][:, :CHUNK], vn) for sl, st, vn in zip(lanes, sts, v_new)]
        dnw = jnp.zeros((1, HEAD_DIM), f32)
        d_outs = []
        for sl, o in zip(lanes, outs):
            do, dn, dz = jax.vjp(gated_norm, o, nw_ref[...], z_ref[:, sl])[1](dy_ref[:, sl])
            dz_ref[:, sl] = dz.astype(bf16)
            dnw = dnw + dn
            d_outs.append(do)
        dnw_ref[...] += dnw
        d_vn = [mm_tn(qk_ref[:, sl][:, :CHUNK], do) + mm_nn(kd_ref[:, sl], ds) for sl, do, ds in zip(lanes, d_outs, dsts)]
        zeros = jnp.zeros((HEAD_DIM - CHUNK, HEAD_DIM), f32)
        rows = lax.broadcasted_iota(jnp.int32, (CHUNK, HEAD_DIM), 0)
        for h, (sl, st, vn, do, ds, dvn) in enumerate(zip(lanes, sts, v_new, d_outs, dsts, d_vn)):
            du_ref[:, sl] = dvn
            dw_ref[:, sl] = -mm_nt(dvn, st)
            dqd_ref[:, sl] = mm_nt(do, st)
            dkd_ref[:, sl] = mm_nt(vn, ds)
            dqk_ref[:, sl] = mm_nt(do, jnp.concatenate([vn, zeros], axis=0))
            dgl_ref[:, sl] = jnp.where(rows == 0, jnp.sum(st * ds, axis=0, keepdims=True), 0.0)
            dstate[h] = ds * gl_ref[0:1, sl] + mm_tn(qd_ref[:, sl], do) - mm_tn(w_ref[:, sl], dvn)

    blk = pl.BlockSpec((CHUNK, width), lambda n: (last - n, 0))
    return _pc(body, name="dn_seq_bwd", grid=(n_chunks,),
               in_specs=[blk] * 6 + [pl.BlockSpec((CHUNK, width), lambda n: (last - n, zd_off)), pl.BlockSpec((1, HEAD_DIM), lambda n: (0, 0)),
                         pl.BlockSpec((1, n_heads, HEAD_DIM, HEAD_DIM), lambda n: (last - n, 0, 0, 0)), blk],
               out_specs=[blk] * 7 + [pl.BlockSpec((1, HEAD_DIM), lambda n: (0, 0))],
               out_shape=[jax.ShapeDtypeStruct((tp, width), f32)] * 6 + [jax.ShapeDtypeStruct((tp, width), bf16),
                                                                          jax.ShapeDtypeStruct((1, HEAD_DIM), f32)],
               scratch_shapes=[pltpu.VMEM((n_heads, HEAD_DIM, HEAD_DIM), f32)],
               compiler_params=_params("arbitrary"))(*inter, proj, dn_norm_w, states, dy)


def loss_stage(out, final_w, target):
    tp, d = out.shape
    n_tiles = tp // CHUNK

    def body(o_ref, w_ref, t_ref, loss_ref, do_ref, dob_ref, dw_ref):
        i = pl.program_id(0)

        @pl.when(i == 0)
        def _():
            loss_ref[...] = jnp.zeros_like(loss_ref)
            dw_ref[...] = jnp.zeros_like(dw_ref)

        scored = (i > 0).astype(f32)
        val, (do, dw) = jax.value_and_grad(lambda o, w: scored * loss_fn(o, w, t_ref[...]), argnums=(0, 1))(o_ref[...], w_ref[...])
        loss_ref[...] += jnp.full(loss_ref.shape, val, f32)
        do_ref[...] = do
        dob_ref[...] = do.astype(bf16)
        dw_ref[...] += dw

    row = pl.BlockSpec((CHUNK, d), lambda i: (i, 0))
    vec = pl.BlockSpec((1, d), lambda i: (0, 0))
    return _pc(body, name="loss_stage", grid=(n_tiles,),
               in_specs=[row, vec, pl.BlockSpec((CHUNK, d), lambda i: (jnp.maximum(i - 1, 0), 0))],
               out_specs=[pl.BlockSpec((1, LANES), lambda i: (0, 0)), row, row, vec],
               out_shape=[jax.ShapeDtypeStruct((1, LANES), f32), jax.ShapeDtypeStruct((tp, d), f32),
                          jax.ShapeDtypeStruct((tp, d), bf16), jax.ShapeDtypeStruct((1, d), f32)],
               compiler_params=_params("arbitrary"))(out, final_w, target)


def _adam_update(w, g, m, v):
    nm = ADAM_B1 * m + (1.0 - ADAM_B1) * g
    nv = ADAM_B2 * v + (1.0 - ADAM_B2) * jnp.square(g)
    m_hat = nm / (1.0 - ADAM_B1 ** ADAM_STEP)
    v_hat = nv / (1.0 - ADAM_B2 ** ADAM_STEP)
    return -ADAM_LR * (m_hat / (jnp.sqrt(v_hat) + ADAM_EPS) + ADAM_WD * w), nm, nv


def adamw(name, w, g, m, v):
    rows, cols = w.shape
    t = _row_tile(rows, cols, 7)

    def body(w_ref, g_ref, m_ref, v_ref, d_ref, nm_ref, nv_ref):
        d_ref[...], nm_ref[...], nv_ref[...] = _adam_update(w_ref[...], g_ref[...], m_ref[...], v_ref[...])

    blk = pl.BlockSpec((t, cols), lambda i: (i, 0))
    return _pc(body, name=name, grid=(rows // t,), in_specs=[blk] * 4, out_specs=[blk] * 3,
               out_shape=[jax.ShapeDtypeStruct(w.shape, f32)] * 3, compiler_params=_params("parallel"))(w, g, m, v)


def adamw_joined(name, w, g_mine, g_theirs, m, v, half, axis):
    rows, cols = w.shape
    hr, hc = g_mine.shape
    tr, tc = _tile(hr, hc, 9)
    nr, nc = hr // tr, hc // tc

    def body(half_ref, w_ref, gm_ref, gt_ref, m_ref, v_ref, g_ref, d_ref, nm_ref, nv_ref):
        pos = pl.program_id(axis) // (nr if axis == 0 else nc)
        g = jnp.where(pos == half_ref[0], gm_ref[...], gt_ref[...])
        delta, nm, nv = _adam_update(w_ref[...], g, m_ref[...], v_ref[...])
        g_ref[...] = g
        d_ref[...] = delta
        nm_ref[...] = nm
        nv_ref[...] = nv

    whole = pl.BlockSpec((tr, tc), lambda i, j, hf: (i, j))
    part = pl.BlockSpec((tr, tc), lambda i, j, hf: (i % nr, j % nc))
    grid_spec = pltpu.PrefetchScalarGridSpec(num_scalar_prefetch=1, grid=(rows // tr, cols // tc),
                                             in_specs=[whole, part, part, whole, whole], out_specs=[whole] * 4)
    return _pc(body, name=name, grid_spec=grid_spec, out_shape=[jax.ShapeDtypeStruct(w.shape, f32)] * 4,
               compiler_params=_params("parallel", "parallel"))(half, w, g_mine, g_theirs, m, v)


def add_halves(name, g, recv, half, kind):
    s, r, c = recv.shape
    tr, tc = _tile(r, c, 3)
    nc = c // tc

    def body(half_ref, g_ref, r_ref, o_ref):
        o_ref[...] = (g_ref[...].reshape(r_ref.shape).astype(f32) + r_ref[...].astype(f32)).astype(bf16)

    if kind == "lead":
        g_spec = pl.BlockSpec((1, 1, tr, tc), lambda i, j, k, hf: (hf[0], i, j, k))
    else:
        g_spec = pl.BlockSpec((1, tr, tc), lambda i, j, k, hf: (i, j, hf[0] * nc + k))
    blk = pl.BlockSpec((1, tr, tc), lambda i, j, k, hf: (i, j, k))
    grid_spec = pltpu.PrefetchScalarGridSpec(num_scalar_prefetch=1, grid=(s, r // tr, nc), in_specs=[g_spec, blk], out_specs=blk)
    return _pc(body, name=name, grid_spec=grid_spec, out_shape=jax.ShapeDtypeStruct((s, r, c), bf16),
               compiler_params=_params("parallel", "parallel", "parallel"))(half, g, recv)


def add_first(name, parts, got, chips):
    _, r, c = parts.shape
    tr, tc = _tile(r, c, 6)

    def body(chips_ref, mine_ref, theirs_ref, got_ref, keep_ref, pass_ref):
        keep_ref[...] = mine_ref[0].astype(f32) + got_ref[0].astype(f32)
        pass_ref[...] = (theirs_ref[0].astype(f32) + got_ref[1].astype(f32)).astype(bf16)

    blk = pl.BlockSpec((tr, tc), lambda i, j, ch: (i, j))
    grid_spec = pltpu.PrefetchScalarGridSpec(
        num_scalar_prefetch=1, grid=(r // tr, c // tc),
        in_specs=[pl.BlockSpec((1, tr, tc), lambda i, j, ch: (ch[0], i, j)), pl.BlockSpec((1, tr, tc), lambda i, j, ch: (ch[1], i, j)),
                  pl.BlockSpec((2, tr, tc), lambda i, j, ch: (0, i, j))],
        out_specs=[blk, blk])
    return _pc(body, name=name, grid_spec=grid_spec, out_shape=[jax.ShapeDtypeStruct((r, c), f32), jax.ShapeDtypeStruct((r, c), bf16)],
               compiler_params=_params("parallel", "parallel"))(chips, parts, parts, got)


def add_second(name, kept, got):
    r, c = kept.shape
    tr, tc = _tile(r, c, 3)

    def body(k_ref, g_ref, o_ref):
        o_ref[...] = k_ref[...] + g_ref[...].astype(f32)

    blk = pl.BlockSpec((tr, tc), lambda i, j: (i, j))
    return _pc(body, name=name, grid=(r // tr, c // tc), in_specs=[blk, blk], out_specs=blk,
               out_shape=jax.ShapeDtypeStruct((r, c), f32), compiler_params=_params("parallel", "parallel"))(kept, got)


def sum_leading(name, x, out_dtype=f32):
    s, r, c = x.shape
    tr, tc = _tile(r, c, s + 1)

    def body(x_ref, o_ref):
        acc = x_ref[0].astype(f32)
        for i in range(1, s):
            acc = acc + x_ref[i].astype(f32)
        o_ref[...] = acc.astype(out_dtype)

    return _pc(body, name=name, grid=(r // tr, c // tc), in_specs=[pl.BlockSpec((s, tr, tc), lambda i, j: (0, i, j))],
               out_specs=pl.BlockSpec((tr, tc), lambda i, j: (i, j)), out_shape=jax.ShapeDtypeStruct((r, c), out_dtype),
               compiler_params=_params("parallel", "parallel"))(x)


def _place():
    x, y, c = lax.axis_index("x"), lax.axis_index("y"), lax.axis_index("c")
    return x, y, c


def _route(x, y, c):
    return (x ^ (1 - c), y ^ c), (x ^ c, y ^ (1 - c)), (1 - x, 1 - y)


def _half_view(ref, half, kind, lead=()):
    if kind == "lead":
        return ref.at[(*lead, half)]
    width = ref.shape[-1] // 2
    return ref.at[(*lead, *([slice(None)] * (len(ref.shape) - len(lead) - 1)), pl.ds(half * width, width))]


def _gather_plan(ins, outs, sems, kinds):
    n = len(ins)
    send1, recv1, send2, recv2 = sems
    x, y, c = _place()
    chip = 2 * x + y
    sibling = (x, y, 1 - c)
    near, far, diag = _route(x, y, c)
    near_id, far_id, diag_id = [2 * cx + cy for cx, cy in (near, far, diag)]

    def remote(src, dst, s_sem, r_sem, to):
        return pltpu.make_async_remote_copy(src_ref=src, dst_ref=dst, send_sem=s_sem, recv_sem=r_sem, device_id=to, device_id_type=MESH)

    def slab(a, chip_id, half):
        return _half_view(outs[a], half, kinds[a], (chip_id,))

    def landed(a, chip_id, sem, frm):
        return remote(slab(a, chip_id, c), slab(a, chip_id, c), send1.at[a, sem], recv1.at[a, sem], (*frm, c))

    def onward(a, chip_id, sem):
        return remote(slab(a, chip_id, c), slab(a, chip_id, c), send2.at[a, sem], recv2.at[a, sem], sibling)

    own = [remote(_half_view(ins[a], c, kinds[a]), slab(a, chip, c), send1.at[a, j], recv1.at[a, j], (*to, c))
           for a in range(n) for j, to in enumerate((near, far))]
    relay = [remote(slab(a, near_id, c), slab(a, near_id, c), send1.at[a, 2], recv1.at[a, 2], (*far, c)) for a in range(n)]
    to_sibling = [[onward(a, cid, sem) for sem, cid in enumerate((near_id, far_id, diag_id))] for a in range(n)]
    from_sibling = [remote(slab(a, cid, 1 - c), slab(a, cid, 1 - c), send2.at[a, sem], recv2.at[a, sem], sibling)
                    for a in range(n) for sem, cid in enumerate((far_id, near_id, diag_id))]

    def start_own():
        for cp in own:
            cp.start()

    def pass_near():
        for a in range(n):
            landed(a, near_id, 0, near).wait_recv()
            relay[a].start()
            to_sibling[a][0].start()

    def pass_far():
        for sem, cid in ((1, far_id), (2, diag_id)):
            for a in range(n):
                landed(a, cid, sem, far).wait_recv()
                to_sibling[a][sem].start()

    def finish():
        for cp in from_sibling:
            cp.wait_recv()
        for cp in own + relay + [cp for row in to_sibling for cp in row]:
            cp.wait_send()

    return [start_own, pass_near, pass_far, finish]


def _gather_shapes(shards):
    return ([jax.ShapeDtypeStruct((4,) + s.shape, s.dtype) for s in shards], [pltpu.SemaphoreType.DMA((len(shards), 3))] * 4)


def gather_weights(shards, kinds):
    n = len(shards)

    def body(*refs):
        for emit in _gather_plan(refs[:n], refs[n:2 * n], refs[2 * n:], kinds):
            emit()

    out_shapes, sems = _gather_shapes(shards)
    return _pc(body, name="gather_weights", in_specs=[_ANY] * n, out_specs=[_ANY] * n, out_shape=out_shapes, scratch_shapes=sems)(*shards)


def hosted_gather(shards, kinds):
    out_shapes, sems = _gather_shapes(shards)
    return Hosted(shards, out_shapes, sems, lambda i, o, s: list(zip((0.0, 0.45, 0.8, 1.0), _gather_plan(i, o, s, kinds))))


def swap_with_sibling(name, sends):
    n = len(sends)

    def body(*refs):
        ins, outs = refs[:n], refs[n:2 * n]
        send, recv = refs[2 * n:]
        x, y, c = _place()
        cps = [pltpu.make_async_remote_copy(src_ref=ins[a], dst_ref=outs[a], send_sem=send.at[a], recv_sem=recv.at[a],
                                            device_id=(x, y, 1 - c), device_id_type=MESH) for a in range(n)]
        for cp in cps:
            cp.start()
        for cp in cps:
            cp.wait()

    return _pc(body, name=name, in_specs=[_ANY] * n, out_specs=[_ANY] * n,
               out_shape=[jax.ShapeDtypeStruct(s.shape, s.dtype) for s in sends],
               scratch_shapes=[pltpu.SemaphoreType.DMA((n,))] * 2)(*sends)


def send_grad_halves(name, grads, kinds):
    n = len(grads)

    def body(*refs):
        ins, outs = refs[:n], refs[n:2 * n]
        send, recv = refs[2 * n:]
        x, y, c = _place()
        cps = [pltpu.make_async_remote_copy(src_ref=_half_view(ins[a], 1 - c, kinds[a]), dst_ref=outs[a], send_sem=send.at[a],
                                            recv_sem=recv.at[a], device_id=(x, y, 1 - c), device_id_type=MESH) for a in range(n)]
        for cp in cps:
            cp.start()
        for cp in cps:
            cp.wait()

    shape = lambda g, kind: g.shape[1:] if kind == "lead" else g.shape[:-1] + (g.shape[-1] // 2,)
    return _pc(body, name=name, in_specs=[_ANY] * n, out_specs=[_ANY] * n,
               out_shape=[jax.ShapeDtypeStruct(shape(g, k), g.dtype) for g, k in zip(grads, kinds)],
               scratch_shapes=[pltpu.SemaphoreType.DMA((n,))] * 2)(*grads)


def _scatter_first_plan(ins, outs, sems):
    n = len(ins)
    send, recv = sems
    x, y, c = _place()
    near, _, diag = _route(x, y, c)
    cps = [pltpu.make_async_remote_copy(src_ref=ins[a].at[2 * cx + cy], dst_ref=outs[a].at[j], send_sem=send.at[a, j],
                                        recv_sem=recv.at[a, j], device_id=(*near, c), device_id_type=MESH)
           for a in range(n) for j, (cx, cy) in enumerate((near, diag))]

    def start():
        for cp in cps:
            cp.start()

    def wait():
        for cp in cps:
            cp.wait()

    return [start, wait]


def _scatter_first_shapes(parts):
    return ([jax.ShapeDtypeStruct((2,) + p.shape[1:], p.dtype) for p in parts], [pltpu.SemaphoreType.DMA((len(parts), 2))] * 2)


def scatter_first(parts):
    n = len(parts)

    def body(*refs):
        for emit in _scatter_first_plan(refs[:n], refs[n:2 * n], refs[2 * n:]):
            emit()

    out_shapes, sems = _scatter_first_shapes(parts)
    return _pc(body, name="scatter_first", in_specs=[_ANY] * n, out_specs=[_ANY] * n, out_shape=out_shapes, scratch_shapes=sems)(*parts)


def hosted_scatter_first(parts):
    out_shapes, sems = _scatter_first_shapes(parts)
    return Hosted(parts, out_shapes, sems, lambda i, o, s: list(zip((0.0, 1.0), _scatter_first_plan(i, o, s))))


def _far_swap_plan(ins, outs, sems):
    send, recv = sems
    x, y, c = _place()
    _, far, _ = _route(x, y, c)
    cps = [pltpu.make_async_remote_copy(src_ref=ins[a], dst_ref=outs[a], send_sem=send.at[a], recv_sem=recv.at[a],
                                        device_id=(*far, c), device_id_type=MESH) for a in range(len(ins))]

    def start():
        for cp in cps:
            cp.start()

    def wait():
        for cp in cps:
            cp.wait()

    return [start, wait]


def hosted_scatter_second(parts):
    return Hosted(parts, [jax.ShapeDtypeStruct(p.shape, p.dtype) for p in parts], [pltpu.SemaphoreType.DMA((len(parts),))] * 2,
                  lambda i, o, s: list(zip((0.0, 1.0), _far_swap_plan(i, o, s))))


def scatter_second_and_small(parts, pack):
    n = len(parts)

    def body(*refs):
        ins, pack_ref, outs, packs_ref = refs[:n], refs[n], refs[n + 1:2 * n + 1], refs[2 * n + 1]
        send, recv, pack_send, pack_recv = refs[2 * n + 2:]
        x, y, c = _place()
        me = 4 * x + 2 * y + c
        _, far, _ = _route(x, y, c)
        cps = [pltpu.make_async_remote_copy(src_ref=ins[a], dst_ref=outs[a], send_sem=send.at[a], recv_sem=recv.at[a],
                                            device_id=(*far, c), device_id_type=MESH) for a in range(n)]
        flips = [(fx, fy, fc) for fx in (0, 1) for fy in (0, 1) for fc in (0, 1)][1:]
        peers = [(x ^ fx, y ^ fy, c ^ fc) for fx, fy, fc in flips]
        pack_cps = [pltpu.make_async_remote_copy(src_ref=pack_ref, dst_ref=packs_ref.at[me], send_sem=pack_send.at[j],
                                                 recv_sem=pack_recv.at[j], device_id=p, device_id_type=MESH) for j, p in enumerate(peers)]
        for cp in cps + pack_cps:
            cp.start()
        for cp in cps:
            cp.wait()
        for j, (px, py, pc) in enumerate(peers):
            slab = packs_ref.at[4 * px + 2 * py + pc]
            pltpu.make_async_remote_copy(src_ref=slab, dst_ref=slab, send_sem=pack_send.at[j], recv_sem=pack_recv.at[j],
                                         device_id=(px, py, pc), device_id_type=MESH).wait_recv()
        for cp in pack_cps:
            cp.wait_send()

    outs = _pc(body, name="scatter_second_and_small", in_specs=[_ANY] * (n + 1), out_specs=[_ANY] * (n + 1),
               out_shape=[jax.ShapeDtypeStruct(p.shape, p.dtype) for p in parts] + [jax.ShapeDtypeStruct((8,) + pack.shape, pack.dtype)],
               scratch_shapes=[pltpu.SemaphoreType.DMA((n,))] * 2 + [pltpu.SemaphoreType.DMA((7,))] * 2)(*parts, pack)
    return outs[:n], outs[n]


def _pack(arrays, width):
    flat = jnp.concatenate([a.reshape(-1).astype(f32) for a in arrays])
    return jnp.pad(flat, (0, (-flat.shape[0]) % (8 * width))).reshape(-1, width)


def _unpack(pack, like, width):
    flat, out, at = pack.reshape(-1), [], 0
    for a in like:
        size = 1
        for s in a.shape:
            size *= s
        out.append(flat[at:at + size].reshape(a.shape))
        at += size
    return out


def _halves(a2d):
    r, c = a2d.shape
    return a2d.reshape(2, r // 2, c)


def local_step(x, meta, norm_w, w_main, w_ba, conv_w, a_log, dt_bias, pool_mix, pool_scale, dn_norm_w,
               late_weights, final_w, target, grad_hooks=None):
    seq, d = x.shape
    pw = pool_scale.shape[1]
    dw = conv_w.shape[1] // 3
    n_heads = dw // HEAD_DIM
    tp = FRONT_PAD + N_META + seq
    hp = jnp.concatenate([jnp.zeros((FRONT_PAD, d), f32), meta, x], axis=0)
    tm_big = tp // 2 if tp % 32 == 0 else tp
    tm_norm = max(t for t in range(16, min(tp, 352) + 1, 16) if tp % t == 0)
    tile = min(512, d)
    off_q = 2 * pw
    off_zd = off_q + 3 * dw
    off_gp = off_zd + dw
    off_gd = off_gp + d
    a_log128 = jnp.pad(a_log, ((0, 0), (0, LANES - n_heads)))
    dt128 = jnp.pad(dt_bias, ((0, 0), (0, LANES - n_heads)))

    xn = norm_fwd(hp, norm_w, tm_norm)
    if isinstance(late_weights[0], Hosted):
        proj, *fetched = matmul("proj", xn, w_main, tb=True, tm=tp, tn=tile, hosted=late_weights[0])
        w_pool_out, w_dn_out, w_o = late_weights[1](fetched)
    else:
        proj = matmul("proj", xn, w_main, tb=True, tm=tp, tn=tile)
        w_pool_out, w_dn_out, w_o = late_weights
    ba = matmul("proj_ba", xn, w_ba, tb=True, tm=tp, tn=LANES)
    y_pool = pool_fwd(proj, pool_mix, pool_scale, pw)
    q, k, v, beta_b, g_b = dn_pre_fwd(proj, ba, conv_w, a_log128, dt128, n_heads, off_q // HEAD_DIM)
    assert off_zd % dw == 0
    *inter, tmats = dn_intra_fwd(q, k, v, beta_b, g_b, n_heads)
    y_dn, states = dn_seq_fwd(inter, proj, dn_norm_w, n_heads, off_zd // dw)
    a_mat = matmul("pool_out", y_pool, w_pool_out, tm=tp, tn=tile)

    def merge(acc, a_t, gp_t, gd_t):
        return acc, sigmoid(gp_t) * a_t + sigmoid(gd_t) * acc

    b_mat, merged = matmul("dn_out_merge", y_dn, w_dn_out, tm=tm_big, tn=tile, extras=[(a_mat, 0), (proj, off_gp), (proj, off_gd)],
                           epi=merge, out_dtypes=(f32, bf16))
    out = matmul("out_proj", merged, w_o, tm=tm_big, tn=tile, extras=[(hp, 0)], epi=lambda acc, h_t: (acc + h_t,))
    loss, dout, dout_b, dfinal_w = loss_stage(out, final_w, target)

    def unmerge(dm, a_t, b_t, gp_t, gd_t):
        sp, sd = sigmoid(gp_t), sigmoid(gd_t)
        return dm * sp, dm * sd, dm * a_t * sp * (1.0 - sp), dm * b_t * sd * (1.0 - sd)

    d_a, d_b, d_gp, d_gd = matmul("d_merged", dout_b, w_o, tb=True, tm=tm_big, tn=tile,
                                  extras=[(a_mat, 0), (b_mat, 0), (proj, off_gp), (proj, off_gd)], epi=unmerge,
                                  out_dtypes=(bf16,) * 4)
    g_w_o = matmul("g_w_o", merged, dout_b, ta=True, tm=tile, tn=tile, out_dtypes=(bf16,))
    d_y_pool = matmul("d_y_pool", d_a, w_pool_out, tb=True, tm=tp, tn=tile)
    g_w_pool_out = matmul("g_w_pool_out", y_pool, d_a, ta=True, tm=tile, tn=tile, out_dtypes=(bf16,))
    d_y_dn = matmul("d_y_dn", d_b, w_dn_out, tb=True, tm=tp, tn=tile)
    g_w_dn_out = matmul("g_w_dn_out", y_dn, d_b, ta=True, tm=tile, tn=tile, out_dtypes=(bf16,))
    d_u, d_zp, g_pool_mix, g_pool_scale = pool_bwd(proj, pool_mix, pool_scale, d_y_pool, pw)
    *d_inter, d_zd, g_dn_norm_w = dn_seq_bwd(inter, proj, dn_norm_w, states, d_y_dn, n_heads, off_zd // dw)
    d_q, d_k, d_v, d_beta, d_g = dn_intra_bwd(q, k, v, beta_b, g_b, tmats, d_inter, n_heads)
    d_qr, d_kr, d_vr, d_ba, g_cq, g_ck, g_cv, g_a_log, g_dt = dn_pre_bwd(
        proj, ba, conv_w, a_log128, dt128, (d_q, d_k, d_v, d_beta, d_g), n_heads, off_q // HEAD_DIM)
    d_proj = jnp.concatenate([d_u, d_zp, d_qr, d_kr, d_vr, d_zd, d_gp, d_gd], axis=1)
    d_ba_b = cast_bf16("cast_d_ba", d_ba)
    early = grad_hooks[0](g_w_pool_out, g_w_dn_out, g_w_o, g_pool_mix) if grad_hooks else None
    res = matmul("g_w_main", d_proj, xn, ta=True, tm=tile, tn=tile, out_dtypes=(bf16,), hosted=early)
    g_w_main, early_landed = (res[0], list(res[1:])) if early else (res, [])
    g_w_ba = matmul("g_w_ba", d_ba_b, xn, ta=True, tm=LANES, tn=tile, out_dtypes=(bf16,))
    hosted, carried = grad_hooks[1](g_w_main, g_w_ba, early_landed) if grad_hooks else (None, None)
    dxn_ba = matmul("dxn_ba", d_ba_b, w_ba, tm=tp, tn=tile)
    n_cols = d_proj.shape[1]
    tk_dxn = max(t for t in range(LANES, min(3584, n_cols) + 1, LANES) if n_cols % t == 0)
    res = matmul("dxn", d_proj, w_main, tm=tm_big, tn=tile, tk=tk_dxn, extras=[(dxn_ba, 0)], epi=lambda acc, e: (acc + e,), hosted=hosted)
    dxn, landed = (res[0], list(res[1:])) if hosted else (res, [])
    dh, g_norm_w = norm_bwd(hp, norm_w, dxn, dout, tm_norm)
    g_conv = jnp.concatenate([g_cq, g_ck, g_cv], axis=1)
    return (loss, dh, g_norm_w, g_w_main, g_w_ba, g_conv, g_a_log[:, :n_heads], g_dt[:, :n_heads], g_pool_mix, g_pool_scale,
            g_dn_norm_w, g_w_pool_out, g_w_dn_out, g_w_o, dfinal_w, (carried, landed))


def kernel(x, meta_tokens, norm_w, w_in, conv_w, A_log, dt_bias, pool_mix, pool_scale, dn_norm_w, w_pool_out, w_dn_out, w_o, final_norm_w, loss_target, m_meta_tokens, m_norm_w, m_w_in, m_conv_w, m_A_log, m_dt_bias, m_pool_mix, m_pool_scale, m_dn_norm_w, m_w_pool_out, m_w_dn_out, m_w_o, m_final_norm_w, v_meta_tokens, v_norm_w, v_w_in, v_conv_w, v_A_log, v_dt_bias, v_pool_mix, v_pool_scale, v_dn_norm_w, v_w_pool_out, v_w_dn_out, v_w_o, v_final_norm_w):
    d = x.shape[-1]
    pw = pool_scale.shape[-1]
    dw = w_dn_out.shape[1] * 4
    n_heads = dw // HEAD_DIM
    gdim = pw // POOL_GROUPS
    chip = 2 * lax.axis_index("x") + lax.axis_index("y")
    core = lax.axis_index("c")

    half = core.astype(jnp.int32).reshape(1)
    me = 4 * lax.axis_index("x") + 2 * lax.axis_index("y") + core

    w_in_t = w_in[0].T
    mix_s = pool_mix[0].reshape(POOL_GROUPS * (gdim // 4), gdim)
    small_s = _pack([meta_tokens, conv_w[0]], d)
    small_rows = small_s.shape[0]
    small_s = jnp.pad(small_s, ((0, (-small_rows) % 16), (0, 0)))
    sw = w_in_t.shape[0]
    n_main, n_ba = 2 * pw + 4 * dw, 2 * n_heads

    early = [cast_bf16("cast_w_in", w_in_t), _halves(cast_bf16("cast_mix", mix_s)), _halves(small_s)]
    late = [_halves(cast_bf16("cast_w_po", w_pool_out[0])), _halves(cast_bf16("cast_w_do", w_dn_out[0])),
            _halves(cast_bf16("cast_w_o", w_o[0]))]
    fill = lambda g, own: lax.dynamic_update_slice(g, own[None], (chip,) + (0,) * own.ndim)
    g_in, g_mix, g_small = [fill(g, own) for g, own in zip(gather_weights(early, ["cols", "lead", "lead"]), early)]

    def shard_rows(lo, hi):
        cut = [(max(lo, j * sw), min(hi, (j + 1) * sw), j) for j in range(4)]
        return [g_in[j, a - j * sw:b - j * sw] for a, b, j in cut if a < b]

    w_main = jnp.concatenate(shard_rows(0, n_main) + shard_rows(n_main + n_ba, 4 * sw), axis=0)
    w_ba = jnp.pad(jnp.concatenate(shard_rows(n_main, n_main + n_ba), axis=0), ((0, LANES - n_ba), (0, 0)))
    cat_cols = lambda g: jnp.concatenate([g[j].reshape(-1, g.shape[-1]) for j in range(4)], axis=1)

    def late_weights(fetched):
        g_po, g_do, g_o = [fill(g, own) for g, own in zip(fetched, late)]
        return cat_cols(g_po), g_do.reshape(-1, g_do.shape[-1]), g_o.reshape(-1, g_o.shape[-1])

    mix_full = g_mix.reshape(4, POOL_GROUPS, gdim // 4, gdim).transpose(1, 0, 2, 3).reshape(POOL_GROUPS, gdim, gdim)
    smalls = [_unpack(g_small[j].reshape(-1, d)[:small_rows], [meta_tokens, conv_w[0]], d) for j in range(4)]
    meta_full = jnp.concatenate([s[0] for s in smalls], axis=1)
    conv_full = jnp.concatenate([s[1] for s in smalls], axis=1)

    names = ["w_in", "w_po", "w_do", "w_o", "mix"]
    kinds = ["cols", "lead", "lead", "lead", "lead"]

    far_chip = 2 * (lax.axis_index("x") ^ core) + (lax.axis_index("y") ^ (1 - core))
    chips = jnp.stack([chip, far_chip]).astype(jnp.int32)
    col_parts = lambda g: g.reshape(2, g.shape[0] // 2, 4, g.shape[1] // 4).transpose(0, 2, 1, 3)
    row_parts = lambda g: g.reshape(4, 2, g.shape[0] // 8, g.shape[1]).transpose(1, 0, 2, 3)

    def pair_sums(tag, group, parts):
        group_kinds = [kinds[names.index(nm)] for nm in group]
        from_sibling = send_grad_halves("send_grad_halves_" + tag, parts, group_kinds)
        return [add_halves("add_" + nm, p, r, half, k) for nm, p, r, k in zip(group, parts, from_sibling, group_kinds)]

    later = names[1:]
    state = {}

    def later_grads(g_w_po, g_w_do, g_w_o_full, g_mix_full):
        mix_rows = POOL_GROUPS * (gdim // 4)
        mix_parts = (g_mix_full.astype(bf16).reshape(POOL_GROUPS, 4, gdim // 4, gdim).transpose(1, 0, 2, 3)
                     .reshape(4, 2, mix_rows // 2, gdim).transpose(1, 0, 2, 3))
        state["later_sums"] = pair_sums("later", later, [col_parts(g_w_po), row_parts(g_w_do), row_parts(g_w_o_full), mix_parts])
        return hosted_scatter_first(state["later_sums"])

    def input_grads(g_w_main, g_w_ba, later_from_near):
        def grad_rows(lo, hi):
            segs = [(0, n_main, g_w_main, 0), (n_main, n_main + n_ba, g_w_ba, 0), (n_main + n_ba, 4 * sw, g_w_main, n_main)]
            cut = [(max(lo, s0), min(hi, s1), s0, arr, off) for s0, s1, arr, off in segs]
            return [arr[a - s0 + off:b - s0 + off] for a, b, s0, arr, off in cut if a < b]

        kept, passed = zip(*[add_first("add1_" + nm, p, g, chips) for nm, p, g in zip(later, state["later_sums"], later_from_near)])
        state["later_kept"] = list(kept)
        in_parts = jnp.stack([jnp.concatenate(grad_rows(j * sw, (j + 1) * sw), axis=0) for j in range(4)])
        state["in_sums"] = pair_sums("input", names[:1], [in_parts])
        return hosted_scatter_first(state["in_sums"]) + hosted_scatter_second(list(passed)), None

    (loss, dh, g_norm_w, _, _, g_conv, g_a_log, g_dt, _, g_pool_scale, g_dn_norm_w, _, _, _, g_final_w,
     (_, landed)) = local_step(x[0], meta_full, norm_w, w_main, w_ba, conv_full, A_log, dt_bias, mix_full, pool_scale,
                               dn_norm_w, (hosted_gather(late, ["lead"] * 3), late_weights), final_norm_w.reshape(1, d),
                               loss_target[0], grad_hooks=(later_grads, input_grads))
    grad_x = dh[FRONT_PAD + N_META:][None]
    g_meta = dh[FRONT_PAD:FRONT_PAD + N_META]

    in_from_near, later_from_far = landed[0], landed[1:]
    in_kept, in_passed = add_first("add1_w_in", state["in_sums"][0], in_from_near, chips)
    small_like = [loss, g_norm_w, g_a_log, g_dt, g_pool_scale, g_dn_norm_w, g_final_w, g_conv, g_meta]
    pack = _pack(small_like, d)
    (in_from_far,), packs = scatter_second_and_small([in_passed], pack)
    mine = [add_second("add2_" + nm, k_, g) for nm, k_, g in zip(names, [in_kept] + state["later_kept"], [in_from_far] + list(later_from_far))]
    theirs = swap_with_sibling("swap_grad_halves", mine)
    total = sum_leading("sum_small", lax.dynamic_update_slice(packs, pack[None], (me, 0, 0)))
    (loss_t, g_norm_w, g_a_log, g_dt, g_pool_scale, g_dn_norm_w, g_final_w, g_conv, g_meta) = _unpack(total, small_like, d)
    loss_out = loss_t[0, 0]
    g_conv_s = lax.dynamic_slice_in_dim(g_conv, chip * (g_conv.shape[1] // 4), g_conv.shape[1] // 4, axis=1)
    g_meta_s = lax.dynamic_slice_in_dim(g_meta, chip * (d // 4), d // 4, axis=1)

    weights = [meta_tokens, norm_w, w_in, conv_w, A_log, dt_bias, pool_mix, pool_scale, dn_norm_w, w_pool_out, w_dn_out, w_o, final_norm_w]
    ms = [m_meta_tokens, m_norm_w, m_w_in, m_conv_w, m_A_log, m_dt_bias, m_pool_mix, m_pool_scale, m_dn_norm_w, m_w_pool_out, m_w_dn_out, m_w_o, m_final_norm_w]
    vs = [v_meta_tokens, v_norm_w, v_w_in, v_conv_w, v_A_log, v_dt_bias, v_pool_mix, v_pool_scale, v_dn_norm_w, v_w_pool_out, v_w_dn_out, v_w_o, v_final_norm_w]
    grads = [g_meta_s, g_norm_w, None, g_conv_s[None], g_a_log, g_dt, None, g_pool_scale, g_dn_norm_w, None, None, None, g_final_w.reshape(d)]
    deltas, new_ms, new_vs = [None] * 13, [None] * 13, [None] * 13
    big = [2, 9, 10, 11, 6]
    for i, nm, g_mine, g_theirs in zip(big, names, mine, theirs):
        if nm == "w_in":
            to2d, back, axis = (lambda t: t[0].T), (lambda t: t.T[None]), 1
        else:
            to2d, back, axis = (lambda t: t.reshape(-1, t.shape[-1])), (lambda t, i=i: t.reshape(weights[i].shape)), 0
        res = adamw_joined("adamw_" + nm, to2d(weights[i]), g_mine, g_theirs, to2d(ms[i]), to2d(vs[i]), half, axis)
        grads[i], deltas[i], new_ms[i], new_vs[i] = [back(t) for t in res]
    small_idx = [i for i in range(13) if i not in big]
    packs = [_pack([arrs[i] for i in small_idx], d) for arrs in (weights, grads, ms, vs)]
    outs = adamw("adamw_small", *packs)
    like = [weights[i] for i in small_idx]
    for res, dest in zip(outs, (deltas, new_ms, new_vs)):
        for i, val in zip(small_idx, _unpack(res, like, d)):
            dest[i] = val
    return (loss_out, grad_x, *grads, *deltas, *new_ms, *new_vs)
```

```python
import functools

import jax
import jax.numpy as jnp
from jax import lax
from jax.experimental import pallas as pl
from jax.experimental.pallas import tpu as pltpu

f32 = jnp.float32
bf16 = jnp.bfloat16
MESH = pl.DeviceIdType.MESH

N_META = 16
CHUNK = 64
FRONT_PAD = (-N_META) % CHUNK
HEAD_DIM = 128
POOL_GROUPS = 4
POOL_WINDOWS = (2, 4, 8, 16)
CONV_WIDTH = 4
NORM_EPS = 1e-6
ADAM_LR, ADAM_B1, ADAM_B2, ADAM_EPS, ADAM_WD, ADAM_STEP = 0.001, 0.9, 0.999, 1e-08, 0.01, 10
LANES = 128
V7X_VMEM_BYTES = 64 * 2**20
VMEM_LIMIT = V7X_VMEM_BYTES - 8 * 2**20


def _pc(body, **kw):
    return pl.pallas_call(body, **kw)


def _params(*sem, **kw):
    return pltpu.CompilerParams(dimension_semantics=sem or None, vmem_limit_bytes=VMEM_LIMIT, **kw)


def _dg(a, b, dims):
    return lax.dot_general(a, b, (dims, ((), ())), preferred_element_type=f32)


@jax.custom_vjp
def mm_nn(a, b):
    return _dg(a.astype(bf16), b.astype(bf16), ((1,), (0,)))


@jax.custom_vjp
def mm_nt(a, b):
    return _dg(a.astype(bf16), b.astype(bf16), ((1,), (1,)))


@jax.custom_vjp
def mm_tn(a, b):
    return _dg(a.astype(bf16).T, b.astype(bf16), ((1,), (0,)))


mm_nn.defvjp(lambda a, b: (mm_nn(a, b), (a, b)), lambda r, dy: (mm_nt(dy, r[1]), mm_tn(r[0], dy)))
mm_nt.defvjp(lambda a, b: (mm_nt(a, b), (a, b)), lambda r, dy: (mm_nn(dy, r[1]), mm_tn(dy, r[0])))
mm_tn.defvjp(lambda a, b: (mm_tn(a, b), (a, b)), lambda r, dy: (mm_nt(r[1], dy), mm_nn(r[0], dy)))


def _split3(x):
    hi = x.astype(bf16)
    r1 = x - hi.astype(f32)
    mid = r1.astype(bf16)
    lo = (r1 - mid.astype(f32)).astype(bf16)
    return hi, mid, lo


@jax.custom_vjp
def mm_sel(sel, x):
    s = sel.astype(bf16)
    d = ((1,), (0,))
    hi, mid, lo = _split3(x)
    return _dg(s, hi, d) + _dg(s, mid, d) + _dg(s, lo, d)


def _mm_sel_bwd(sel, dy):
    s = sel.astype(bf16)
    d = ((0,), (0,))
    hi, mid, lo = _split3(dy)
    return jnp.zeros_like(sel), _dg(s, hi, d) + _dg(s, mid, d) + _dg(s, lo, d)


mm_sel.defvjp(lambda sel, x: (mm_sel(sel, x), sel), _mm_sel_bwd)


@jax.custom_vjp
def mm_pick(x, sel):
    s = sel.astype(bf16)
    d = ((1,), (0,))
    hi, mid, lo = _split3(x)
    return _dg(hi, s, d) + _dg(mid, s, d) + _dg(lo, s, d)


def _mm_pick_bwd(sel, dy):
    s = sel.astype(bf16)
    d = ((1,), (1,))
    hi, mid, lo = _split3(dy)
    return _dg(hi, s, d) + _dg(mid, s, d) + _dg(lo, s, d), jnp.zeros_like(sel)


mm_pick.defvjp(lambda x, sel: (mm_pick(x, sel), sel), _mm_pick_bwd)


def tri_inv(ls):
    n = ls[0].shape[0]
    eye = (lax.broadcasted_iota(jnp.int32, (n, n), 0) == lax.broadcasted_iota(jnp.int32, (n, n), 1)).astype(f32)
    ms = [-l for l in ls]
    ts = [eye + m for m in ms]
    k = 1
    while 2 * k < CHUNK:
        ms = [mm_nn(m, m) for m in ms]
        ts = [t + mm_nn(t, m) for t, m in zip(ts, ms)]
        k *= 2
    return ts


@functools.partial(jax.custom_vjp, nondiff_argnums=(1,))
def shift_rows(x, j):
    n = x.shape[0]
    rows = lax.broadcasted_iota(jnp.int32, x.shape, 0)
    if j >= 0:
        return jnp.where(rows >= j, pltpu.roll(x, j, 0), 0.0)
    return jnp.where(rows < n + j, pltpu.roll(x, n + j, 0), 0.0)


shift_rows.defvjp(lambda x, j: (shift_rows(x, j), None), lambda j, _, dy: (shift_rows(dy, -j),))


def sigmoid(x):
    return 1.0 / (1.0 + jnp.exp(-x))


def silu(x):
    return x * sigmoid(x)


def softplus(x):
    return jnp.maximum(x, 0.0) + jnp.log(1.0 + jnp.exp(-jnp.abs(x)))


def rmsnorm(x, w):
    return x * lax.rsqrt(jnp.mean(x * x, axis=-1, keepdims=True) + NORM_EPS) * w


def l2norm(x):
    return x * lax.rsqrt(jnp.sum(x * x, axis=-1, keepdims=True) + NORM_EPS)


def pool_fn(u, zp, mix, scale, group):
    rows = lax.broadcasted_iota(jnp.int32, u.shape, 0)
    sums = []
    s, w = u, 1
    while w < POOL_WINDOWS[-1]:
        s = s + shift_rows(s, w)
        w *= 2
        sums.append(s)
    total = sums[-1]
    for gi in range(POOL_GROUPS - 2, -1, -1):
        total = jnp.where(group == gi, sums[gi], total)
    window = jnp.left_shift(2, group)
    cnt = jnp.clip(rows - (FRONT_PAD - 1), 1, window).astype(f32)
    pooled = total / cnt - u
    return mm_nn(pooled, mix) * scale * silu(zp)


PRE_HALO = 8


def conv_silu(x, w):
    k = CONV_WIDTH
    y = x * w[k - 1:k, :]
    for kk in range(k - 1):
        y = y + shift_rows(x, k - 1 - kk) * w[kk:kk + 1, :]
    return silu(y[PRE_HALO:])


def _lane_pick(row, idx):
    lanes = lax.broadcasted_iota(jnp.int32, row.shape, 1)
    return jnp.sum(jnp.where(lanes == idx, row, 0.0), axis=1, keepdims=True)


def dn_pre_fn(qr, kr, vr, ba, cwq, cwk, cwv, a_log, dt_bias, head, n_heads, row0):
    q = l2norm(conv_silu(qr, cwq)) * (HEAD_DIM ** -0.5)
    k = l2norm(conv_silu(kr, cwk))
    v = conv_silu(vr, cwv)
    r = lax.broadcasted_iota(jnp.int32, (LANES, LANES), 0)
    b_b = mm_pick(ba, (r == head).astype(f32))
    a_b = mm_pick(ba, (r == head + n_heads).astype(f32))
    real = lax.broadcasted_iota(jnp.int32, ba.shape, 0) + row0 >= FRONT_PAD
    beta_b = jnp.where(real, sigmoid(b_b), 0.0)
    g_b = jnp.where(real, -jnp.exp(_lane_pick(a_log, head)) * softplus(a_b + _lane_pick(dt_bias, head)), 0.0)
    return q, k, v, beta_b, g_b


def _chunk_masks(rows):
    r = lax.broadcasted_iota(jnp.int32, (rows, rows), 0)
    c = lax.broadcasted_iota(jnp.int32, (rows, rows), 1)
    same = (r // CHUNK) == (c // CHUNK)
    return same, jnp.logical_and(same, r >= c), jnp.logical_and(same, r > c)


def _lane0(rows):
    return (lax.broadcasted_iota(jnp.int32, (rows, LANES), 1) == 0).astype(bf16)


@jax.custom_vjp
def lane0_as_row(x):
    sel = _lane0(x.shape[0])
    d = ((1,), (1,))
    hi, mid, lo = _split3(x)
    return _dg(sel, hi, d) + _dg(sel, mid, d) + _dg(sel, lo, d)


def _lane0_as_row_bwd(rows, dy):
    sel = _lane0(rows)
    d = ((0,), (0,))
    hi, mid, lo = _split3(dy)
    return (_dg(hi, sel, d) + _dg(mid, sel, d) + _dg(lo, sel, d),)


lane0_as_row.defvjp(lambda x: (lane0_as_row(x), x.shape[0]), _lane0_as_row_bwd)


def gate_fn(g_b):
    rows = g_b.shape[0]
    same, causal, _ = _chunk_masks(rows)
    gcum_b = mm_sel(causal.astype(f32), g_b)
    glast_b = mm_sel(same.astype(f32), g_b)
    g_rows = jnp.broadcast_to(gcum_b[:, :1], (rows, rows))
    decay = jnp.where(causal, jnp.exp(jnp.where(causal, g_rows - lane0_as_row(gcum_b), 0.0)), 0.0)
    return decay, jnp.exp(gcum_b), jnp.exp(glast_b - gcum_b), jnp.exp(glast_b)


def _fold_matrix(rows):
    r = lax.broadcasted_iota(jnp.int32, (rows, LANES), 0)
    c = lax.broadcasted_iota(jnp.int32, (rows, LANES), 1)
    return (r % CHUNK == c).astype(bf16)


@jax.custom_vjp
def fold_chunks(x):
    return _dg(x.astype(bf16), _fold_matrix(x.shape[0]), ((1,), (0,)))


def _fold_chunks_bwd(rows, dy):
    fold = _fold_matrix(rows)
    d = ((1,), (1,))
    hi, mid, lo = _split3(dy)
    return (_dg(hi, fold, d) + _dg(mid, fold, d) + _dg(lo, fold, d),)


fold_chunks.defvjp(lambda x: (fold_chunks(x), x.shape[0]), _fold_chunks_bwd)


def lmat_fn(k, beta_b, decay):
    _, _, strict = _chunk_masks(k.shape[0])
    return jnp.where(strict, mm_nt(k * beta_b, k) * decay, 0.0)


def intra_fn(tmat, q, k, v, beta_b, decay, eg, kfac):
    _, causal, _ = _chunk_masks(q.shape[0])
    k_beta = k * beta_b
    u_c = mm_nn(tmat, v * beta_b)
    w_c = mm_nn(tmat, k_beta * eg)
    qk = jnp.where(causal, mm_nt(q, k) * decay, 0.0)
    return u_c, w_c, q * eg, k * kfac, fold_chunks(qk)


def gated_norm(o, norm_w, zd):
    return rmsnorm(o, norm_w) * silu(zd)


def loss_fn(o, w, tgt):
    err = rmsnorm(o, w) - tgt
    return 0.5 * jnp.sum(jnp.mean(err * err, axis=-1))


_ANY = pl.BlockSpec(memory_space=pl.ANY)


class Hosted:
    def __init__(self, arrays, out_shapes, sems, stages):
        self.arrays, self.out_shapes, self.sems, self.stages = list(arrays), list(out_shapes), list(sems), stages

    def __add__(self, other):
        ni, no, ns = len(self.arrays), len(self.out_shapes), len(self.sems)
        stages = lambda i, o, s: self.stages(i[:ni], o[:no], s[:ns]) + other.stages(i[ni:], o[no:], s[ns:])
        return Hosted(self.arrays + other.arrays, self.out_shapes + other.out_shapes, self.sems + other.sems, stages)


def matmul(name, a, b, *, ta=False, tb=False, tm, tn, tk=None, extras=(), epi=None, out_dtypes=(f32,), hosted=None):
    m, k = (a.shape[1], a.shape[0]) if ta else a.shape
    n = b.shape[0] if tb else b.shape[1]
    tm, tn, tk = min(tm, m), min(tn, n), min(tk or k, k)
    assert m % tm == 0 and n % tn == 0 and k % tk == 0, (name, m, n, k, tm, tn, tk)
    nm, nn, nk = m // tm, n // tn, k // tk
    a_spec = pl.BlockSpec((tk, tm), lambda i, j, kk: (kk, i)) if ta else pl.BlockSpec((tm, tk), lambda i, j, kk: (i, kk))
    b_spec = pl.BlockSpec((tn, tk), lambda i, j, kk: (j, kk)) if tb else pl.BlockSpec((tk, tn), lambda i, j, kk: (kk, j))
    ex_specs = []
    for _, off in extras:
        assert off % tn == 0, (name, off, tn)
        ex_specs.append(pl.BlockSpec((tm, tn), functools.partial(lambda i, j, kk, o: (i, o + j), o=off // tn)))
    n_ex, n_out = len(extras), len(out_dtypes)
    dims = ((0 if ta else 1,), (1 if tb else 0,))
    n_hin = len(hosted.arrays) if hosted else 0
    n_hout = len(hosted.out_shapes) if hosted else 0
    n_sem = len(hosted.sems) if hosted else 0

    def body(a_ref, b_ref, *rest):
        ex_refs, rest = rest[:n_ex], rest[n_ex:]
        hin_refs, rest = rest[:n_hin], rest[n_hin:]
        out_refs, rest = rest[:n_out], rest[n_out:]
        hout_refs, rest = rest[:n_hout], rest[n_hout:]
        sem_refs = rest[len(rest) - n_sem:] if n_sem else ()
        step = (pl.program_id(0) * nn + pl.program_id(1)) * nk + pl.program_id(2)
        stages = hosted.stages(hin_refs, hout_refs, sem_refs) if hosted else []
        for frac, emit in stages:
            if frac < 1.0:
                pl.when(step == int(frac * (nm * nn * nk - 1)))(emit)

        def finish(acc):
            res = epi(acc, *[r[...] for r in ex_refs]) if epi is not None else (acc,)
            for o_ref, val in zip(out_refs, res):
                o_ref[...] = val.astype(o_ref.dtype)

        p = _dg(a_ref[...], b_ref[...], dims)
        if nk == 1:
            finish(p)
        else:
            acc_ref = rest[0]
            kk = pl.program_id(2)

            @pl.when(kk == 0)
            def _():
                acc_ref[...] = p

            @pl.when(kk > 0)
            def _():
                acc_ref[...] += p

            @pl.when(kk == nk - 1)
            def _():
                finish(acc_ref[...])

        for frac, emit in stages:
            if frac >= 1.0:
                pl.when(step == nm * nn * nk - 1)(emit)

    outs = _pc(
        body, name=name, grid=(nm, nn, nk),
        in_specs=[a_spec, b_spec] + ex_specs + [_ANY] * n_hin,
        out_specs=[pl.BlockSpec((tm, tn), lambda i, j, kk: (i, j))] * n_out + [_ANY] * n_hout,
        out_shape=[jax.ShapeDtypeStruct((m, n), dt) for dt in out_dtypes] + (hosted.out_shapes if hosted else []),
        scratch_shapes=([pltpu.VMEM((tm, tn), f32)] if nk > 1 else []) + (hosted.sems if hosted else []),
        compiler_params=_params(*(("arbitrary",) * 3 if hosted else ("parallel", "parallel", "arbitrary"))),
    )(a, b, *[e for e, _ in extras], *(hosted.arrays if hosted else []))
    return outs[0] if len(outs) == 1 else outs


def _row_tile(rows, cols, n_arrays, itemsize=4, budget=24 * 2**20):
    best = None
    for t in range(16, rows + 1, 16):
        if rows % t == 0 and 2 * n_arrays * t * cols * itemsize <= budget:
            best = t
    return best or rows


def _tile(rows, cols, n_arrays, budget=24 * 2**20):
    if rows % 16 == 0 or cols % LANES != 0:
        return _row_tile(rows, cols, n_arrays, budget=budget), cols
    fits = [t for t in range(LANES, cols + 1, LANES) if cols % t == 0 and 2 * n_arrays * rows * t * 4 <= budget]
    return rows, (max(fits) if fits else LANES)


def cast_bf16(name, x):
    rows, cols = x.shape
    tr, tc = _tile(rows, cols, 2)

    def body(x_ref, o_ref):
        o_ref[...] = x_ref[...].astype(bf16)

    blk = pl.BlockSpec((tr, tc), lambda i, j: (i, j))
    return _pc(body, name=name, grid=(rows // tr, cols // tc), in_specs=[blk], out_specs=blk,
               out_shape=jax.ShapeDtypeStruct(x.shape, bf16), compiler_params=_params("parallel", "parallel"))(x)


def norm_fwd(hp, norm_w, tm):
    tp, d = hp.shape

    def body(h_ref, w_ref, o_ref):
        o_ref[...] = rmsnorm(h_ref[...], w_ref[...]).astype(bf16)

    return _pc(body, name="norm_fwd", grid=(tp // tm,),
               in_specs=[pl.BlockSpec((tm, d), lambda i: (i, 0)), pl.BlockSpec((1, d), lambda i: (0, 0))],
               out_specs=pl.BlockSpec((tm, d), lambda i: (i, 0)), out_shape=jax.ShapeDtypeStruct((tp, d), bf16),
               compiler_params=_params("parallel"))(hp, norm_w)


def norm_bwd(hp, norm_w, dxn, dout, tm):
    tp, d = hp.shape

    def body(h_ref, w_ref, dxn_ref, dout_ref, dh_ref, dw_ref):
        _, vjp = jax.vjp(rmsnorm, h_ref[...], w_ref[...])
        dh, dw = vjp(dxn_ref[...])
        dh_ref[...] = dh + dout_ref[...]

        @pl.when(pl.program_id(0) == 0)
        def _():
            dw_ref[...] = jnp.zeros_like(dw_ref)

        dw_ref[...] += dw

    row = pl.BlockSpec((tm, d), lambda i: (i, 0))
    vec = pl.BlockSpec((1, d), lambda i: (0, 0))
    return _pc(body, name="norm_bwd", grid=(tp // tm,), in_specs=[row, vec, row, row], out_specs=[row, vec],
               out_shape=[jax.ShapeDtypeStruct((tp, d), f32), jax.ShapeDtypeStruct((1, d), f32)],
               compiler_params=_params("arbitrary"))(hp, norm_w, dxn, dout)


def pool_fwd(proj, mix, scale, pw):
    tp = proj.shape[0]
    g = pw // POOL_GROUPS

    def body(u_ref, z_ref, mix_ref, s_ref, y_ref):
        y_ref[...] = pool_fn(u_ref[...], z_ref[...], mix_ref[0], s_ref[...], pl.program_id(0)).astype(bf16)

    return _pc(body, name="pool_fwd", grid=(POOL_GROUPS,),
               in_specs=[pl.BlockSpec((tp, g), lambda i: (0, i)), pl.BlockSpec((tp, g), lambda i: (0, POOL_GROUPS + i)),
                         pl.BlockSpec((1, g, g), lambda i: (i, 0, 0)), pl.BlockSpec((1, g), lambda i: (0, i))],
               out_specs=pl.BlockSpec((tp, g), lambda i: (0, i)), out_shape=jax.ShapeDtypeStruct((tp, pw), bf16),
               compiler_params=_params("parallel"))(proj, proj, mix, scale)


def pool_bwd(proj, mix, scale, dy, pw):
    tp = proj.shape[0]
    g = pw // POOL_GROUPS

    def body(u_ref, z_ref, mix_ref, s_ref, dy_ref, du_ref, dz_ref, dmix_ref, ds_ref):
        grp = pl.program_id(0)
        _, vjp = jax.vjp(lambda u, z, m, s: pool_fn(u, z, m, s, grp), u_ref[...], z_ref[...], mix_ref[0].astype(f32), s_ref[...])
        du, dz, dmix, ds = vjp(dy_ref[...])
        du_ref[...] = du.astype(bf16)
        dz_ref[...] = dz.astype(bf16)
        dmix_ref[0] = dmix
        ds_ref[...] = ds

    col = pl.BlockSpec((tp, g), lambda i: (0, i))
    return _pc(body, name="pool_bwd", grid=(POOL_GROUPS,),
               in_specs=[col, pl.BlockSpec((tp, g), lambda i: (0, POOL_GROUPS + i)),
                         pl.BlockSpec((1, g, g), lambda i: (i, 0, 0)), pl.BlockSpec((1, g), lambda i: (0, i)), col],
               out_specs=[col, col, pl.BlockSpec((1, g, g), lambda i: (i, 0, 0)), pl.BlockSpec((1, g), lambda i: (0, i))],
               out_shape=[jax.ShapeDtypeStruct((tp, pw), bf16), jax.ShapeDtypeStruct((tp, pw), bf16),
                          jax.ShapeDtypeStruct((POOL_GROUPS, g, g), f32), jax.ShapeDtypeStruct((1, pw), f32)],
               compiler_params=_params("parallel"))(proj, proj, mix, scale, dy)


def _dn_pre_specs(tp, n_heads, q_off):
    hb = lambda off: pl.BlockSpec((tp, HEAD_DIM), functools.partial(lambda h, o: (0, o + h), o=off))
    cw = lambda off: pl.BlockSpec((CONV_WIDTH, HEAD_DIM), functools.partial(lambda h, o: (0, o + h), o=off))
    whole = lambda shape: pl.BlockSpec(shape, lambda h: (0, 0))
    return ([hb(q_off), hb(q_off + n_heads), hb(q_off + 2 * n_heads), whole((tp, LANES)),
             cw(0), cw(n_heads), cw(2 * n_heads), whole((1, LANES)), whole((1, LANES))], hb, cw, whole)


def _pre_rows(tp):
    return max(t for t in range(16, min(tp, 192) + 1, 16) if tp % t == 0)


def _with_history(ref, r0, rows):
    if r0 == 0:
        return jnp.concatenate([jnp.zeros((PRE_HALO, ref.shape[1]), f32), ref[0:rows, :]], axis=0)
    return ref[r0 - PRE_HALO:r0 + rows, :]


def dn_pre_fwd(proj, ba, conv_w, a_log, dt_bias, n_heads, q_off):
    tp = proj.shape[0]
    rows = _pre_rows(tp)
    in_specs, hb, _, _ = _dn_pre_specs(tp, n_heads, q_off)

    def body(q_ref, k_ref, v_ref, ba_ref, cq_ref, ck_ref, cv_ref, al_ref, dt_ref, *out_refs):
        for r0 in range(0, tp, rows):
            outs = dn_pre_fn(_with_history(q_ref, r0, rows), _with_history(k_ref, r0, rows), _with_history(v_ref, r0, rows),
                             ba_ref[r0:r0 + rows, :], cq_ref[...], ck_ref[...], cv_ref[...], al_ref[...], dt_ref[...],
                             pl.program_id(0), n_heads, r0)
            for o_ref, val in zip(out_refs, outs):
                o_ref[r0:r0 + rows, :] = val

    return _pc(body, name="dn_pre_fwd", grid=(n_heads,), in_specs=in_specs, out_specs=[hb(0)] * 5,
               out_shape=[jax.ShapeDtypeStruct((tp, n_heads * HEAD_DIM), f32)] * 5,
               compiler_params=_params("parallel"))(proj, proj, proj, ba, conv_w, conv_w, conv_w, a_log, dt_bias)


def dn_pre_bwd(proj, ba, conv_w, a_log, dt_bias, cots, n_heads, q_off):
    tp = proj.shape[0]
    rows = _pre_rows(tp)
    in_specs, hb, cw, whole = _dn_pre_specs(tp, n_heads, q_off)

    def body(q_ref, k_ref, v_ref, ba_ref, cq_ref, ck_ref, cv_ref, al_ref, dt_ref, dq_ref, dk_ref, dv_ref, db_ref, dg_ref,
             dqr_ref, dkr_ref, dvr_ref, dba_ref, dcq_ref, dck_ref, dcv_ref, dal_ref, ddt_ref):
        head = pl.program_id(0)

        @pl.when(head == 0)
        def _():
            dba_ref[...] = jnp.zeros_like(dba_ref)
            dal_ref[...] = jnp.zeros_like(dal_ref)
            ddt_ref[...] = jnp.zeros_like(ddt_ref)

        owed = [jnp.zeros((PRE_HALO, HEAD_DIM), f32)] * 3
        d_conv = [jnp.zeros((CONV_WIDTH, HEAD_DIM), f32)] * 3
        d_al, d_dt = jnp.zeros((1, LANES), f32), jnp.zeros((1, LANES), f32)
        for r0 in reversed(range(0, tp, rows)):
            fn = lambda *args, r0=r0: dn_pre_fn(*args, head, n_heads, r0)
            _, vjp = jax.vjp(fn, _with_history(q_ref, r0, rows), _with_history(k_ref, r0, rows), _with_history(v_ref, r0, rows),
                             ba_ref[r0:r0 + rows, :], cq_ref[...], ck_ref[...], cv_ref[...], al_ref[...], dt_ref[...])
            tile = slice(r0, r0 + rows)
            *d_raw, dba, dcq, dck, dcv, dal, ddt = vjp((dq_ref[tile, :], dk_ref[tile, :], dv_ref[tile, :], db_ref[tile, :], dg_ref[tile, :]))
            for i, (out_ref, d) in enumerate(zip((dqr_ref, dkr_ref, dvr_ref), d_raw)):
                out_ref[tile, :] = jnp.concatenate([d[PRE_HALO:rows], d[rows:] + owed[i]], axis=0).astype(bf16)
                owed[i] = d[:PRE_HALO]
            dba_ref[tile, :] += dba
            d_conv = [acc + d for acc, d in zip(d_conv, (dcq, dck, dcv))]
            d_al, d_dt = d_al + dal, d_dt + ddt
        dcq_ref[...], dck_ref[...], dcv_ref[...] = d_conv
        dal_ref[...] += d_al
        ddt_ref[...] += d_dt

    w = n_heads * HEAD_DIM
    return _pc(body, name="dn_pre_bwd", grid=(n_heads,), in_specs=in_specs + [hb(0)] * 5,
               out_specs=[hb(0)] * 3 + [whole((tp, LANES)), cw(0), cw(0), cw(0), whole((1, LANES)), whole((1, LANES))],
               out_shape=[jax.ShapeDtypeStruct((tp, w), bf16)] * 3 + [jax.ShapeDtypeStruct((tp, LANES), f32)]
               + [jax.ShapeDtypeStruct((CONV_WIDTH, w), f32)] * 3 + [jax.ShapeDtypeStruct((1, LANES), f32)] * 2,
               compiler_params=_params("arbitrary"))(proj, proj, proj, ba, conv_w, conv_w, conv_w, a_log, dt_bias, *cots)


def _super_rows(tp):
    n = tp // CHUNK
    return CHUNK * max(j for j in (4, 3, 2, 1) if n % j == 0)


def _heads_per_step(n_heads):
    return max(j for j in (4, 2, 1) if n_heads % j == 0)


def dn_intra_fwd(q, k, v, beta_b, g_b, n_heads):
    tp = q.shape[0]
    rows = _super_rows(tp)
    ns = tp // rows

    hps = _heads_per_step(n_heads)

    def body(q_ref, k_ref, v_ref, b_ref, g_ref, u_ref, w_ref, qd_ref, kd_ref, qk_ref, gl_ref, t_ref):
        lanes = [slice(i * HEAD_DIM, (i + 1) * HEAD_DIM) for i in range(hps)]
        gates = [gate_fn(g_ref[:, sl]) for sl in lanes]
        tmats = tri_inv([lmat_fn(k_ref[:, sl], b_ref[:, sl], gt[0]) for sl, gt in zip(lanes, gates)])
        for i, (sl, (decay, eg, kfac, gl), tmat) in enumerate(zip(lanes, gates, tmats)):
            u_c, w_c, q_dec, k_dec, qk_c = intra_fn(tmat, q_ref[:, sl], k_ref[:, sl], v_ref[:, sl], b_ref[:, sl], decay, eg, kfac)
            u_ref[:, sl] = u_c
            w_ref[:, sl] = w_c
            qd_ref[:, sl] = q_dec
            kd_ref[:, sl] = k_dec
            qk_ref[:, sl] = qk_c
            gl_ref[:, sl] = gl
            t_ref[i, 0] = tmat

    blk = pl.BlockSpec((rows, hps * HEAD_DIM), lambda h, s: (s, h))
    return _pc(body, name="dn_intra_fwd", grid=(n_heads // hps, ns), in_specs=[blk] * 5,
               out_specs=[blk] * 6 + [pl.BlockSpec((hps, 1, rows, rows), lambda h, s: (h, s, 0, 0))],
               out_shape=[jax.ShapeDtypeStruct(q.shape, f32)] * 6 + [jax.ShapeDtypeStruct((n_heads, ns, rows, rows), f32)],
               compiler_params=_params("parallel", "parallel"))(q, k, v, beta_b, g_b)


def dn_intra_bwd(q, k, v, beta_b, g_b, tmats, cots, n_heads):
    tp = q.shape[0]
    rows = _super_rows(tp)
    ns = tp // rows

    hps = _heads_per_step(n_heads)

    def body(q_ref, k_ref, v_ref, b_ref, g_ref, t_ref, du_ref, dw_ref, dqd_ref, dkd_ref, dqk_ref, dgl_ref,
             dq_ref, dk_ref, dv_ref, db_ref, dg_ref):
        lanes = [slice(i * HEAD_DIM, (i + 1) * HEAD_DIM) for i in range(hps)]
        tmats = [t_ref[i, 0] for i in range(hps)]
        gates = [jax.vjp(gate_fn, g_ref[:, sl]) for sl in lanes]
        intra = [jax.vjp(intra_fn, tmat, q_ref[:, sl], k_ref[:, sl], v_ref[:, sl], b_ref[:, sl], gt[0][0], gt[0][1], gt[0][2])[1](
            (du_ref[:, sl], dw_ref[:, sl], dqd_ref[:, sl], dkd_ref[:, sl], dqk_ref[:, sl]))
            for sl, tmat, gt in zip(lanes, tmats, gates)]
        tts = [tmat.T for tmat in tmats]
        dls = [mm_nn(tt, res[0]) for tt, res in zip(tts, intra)]
        dls = [-mm_nn(dl, tt) for dl, tt in zip(dls, tts)]
        for sl, gt, res, dl in zip(lanes, gates, intra, dls):
            _, dq, dk, dv, db, ddecay, deg, dkfac = res
            dk2, db2, ddecay2 = jax.vjp(lmat_fn, k_ref[:, sl], b_ref[:, sl], gt[0][0])[1](dl)
            (dg,) = gt[1]((ddecay + ddecay2, deg, dkfac, dgl_ref[:, sl]))
            dq_ref[:, sl] = dq
            dk_ref[:, sl] = dk + dk2
            dv_ref[:, sl] = dv
            db_ref[:, sl] = db + db2
            dg_ref[:, sl] = dg

    blk = pl.BlockSpec((rows, hps * HEAD_DIM), lambda h, s: (s, h))
    return _pc(body, name="dn_intra_bwd", grid=(n_heads // hps, ns),
               in_specs=[blk] * 5 + [pl.BlockSpec((hps, 1, rows, rows), lambda h, s: (h, s, 0, 0))] + [blk] * 6,
               out_specs=[blk] * 5, out_shape=[jax.ShapeDtypeStruct(q.shape, f32)] * 5,
               compiler_params=_params("parallel", "parallel"))(q, k, v, beta_b, g_b, tmats, *cots)


def dn_seq_fwd(inter, proj, dn_norm_w, n_heads, zd_off):
    tp, width = inter[0].shape
    n_chunks = tp // CHUNK

    def body(u_ref, w_ref, qd_ref, kd_ref, qk_ref, gl_ref, z_ref, nw_ref, y_ref, s_ref, state):
        @pl.when(pl.program_id(0) == 0)
        def _():
            state[...] = jnp.zeros_like(state)

        lanes = [slice(h * HEAD_DIM, (h + 1) * HEAD_DIM) for h in range(n_heads)]
        sts = [state[h] for h in range(n_heads)]
        for h, st in enumerate(sts):
            s_ref[0, h] = st
        v_new = [u_ref[:, sl] - mm_nn(w_ref[:, sl], st) for sl, st in zip(lanes, sts)]
        outs = [mm_nn(qd_ref[:, sl], st) + mm_nn(qk_ref[:, sl][:, :CHUNK], vn) for sl, st, vn in zip(lanes, sts, v_new)]
        for h, (sl, st, vn) in enumerate(zip(lanes, sts, v_new)):
            state[h] = st * gl_ref[0:1, sl] + mm_tn(kd_ref[:, sl], vn)
        for sl, o in zip(lanes, outs):
            y_ref[:, sl] = gated_norm(o, nw_ref[...], z_ref[:, sl]).astype(bf16)

    blk = pl.BlockSpec((CHUNK, width), lambda n: (n, 0))
    return _pc(body, name="dn_seq_fwd", grid=(n_chunks,),
               in_specs=[blk] * 6 + [pl.BlockSpec((CHUNK, width), lambda n: (n, zd_off)), pl.BlockSpec((1, HEAD_DIM), lambda n: (0, 0))],
               out_specs=[blk, pl.BlockSpec((1, n_heads, HEAD_DIM, HEAD_DIM), lambda n: (n, 0, 0, 0))],
               out_shape=[jax.ShapeDtypeStruct((tp, width), bf16), jax.ShapeDtypeStruct((n_chunks, n_heads, HEAD_DIM, HEAD_DIM), f32)],
               scratch_shapes=[pltpu.VMEM((n_heads, HEAD_DIM, HEAD_DIM), f32)],
               compiler_params=_params("arbitrary"))(*inter, proj, dn_norm_w)


def dn_seq_bwd(inter, proj, dn_norm_w, states, dy, n_heads, zd_off):
    tp, width = inter[0].shape
    n_chunks = tp // CHUNK
    last = n_chunks - 1

    def body(u_ref, w_ref, qd_ref, kd_ref, qk_ref, gl_ref, z_ref, nw_ref, s_ref, dy_ref,
             du_ref, dw_ref, dqd_ref, dkd_ref, dqk_ref, dgl_ref, dz_ref, dnw_ref, dstate):
        @pl.when(pl.program_id(0) == 0)
        def _():
            dstate[...] = jnp.zeros_like(dstate)
            dnw_ref[...] = jnp.zeros_like(dnw_ref)

        lanes = [slice(h * HEAD_DIM, (h + 1) * HEAD_DIM) for h in range(n_heads)]
        sts = [s_ref[0, h] for h in range(n_heads)]
        dsts = [dstate[h] for h in range(n_heads)]
        v_new = [u_ref[:, sl] - mm_nn(w_ref[:, sl], st) for sl, st in zip(lanes, sts)]
        outs = [mm_nn(qd_ref[:, sl], st) + mm_nn(qk_ref[:, sl][:, :CHUNK], vn) for sl, st, vn in zip(lanes, sts, v_new)]
        dnw = jnp.zeros((1, HEAD_DIM), f32)
        d_outs = []
        for sl, o in zip(lanes, outs):
            do, dn, dz = jax.vjp(gated_norm, o, nw_ref[...], z_ref[:, sl])[1](dy_ref[:, sl])
            dz_ref[:, sl] = dz.astype(bf16)
            dnw = dnw + dn
            d_outs.append(do)
        dnw_ref[...] += dnw
        d_vn = [mm_tn(qk_ref[:, sl][:, :CHUNK], do) + mm_nn(kd_ref[:, sl], ds) for sl, do, ds in zip(lanes, d_outs, dsts)]
        zeros = jnp.zeros((HEAD_DIM - CHUNK, HEAD_DIM), f32)
        rows = lax.broadcasted_iota(jnp.int32, (CHUNK, HEAD_DIM), 0)
        for h, (sl, st, vn, do, ds, dvn) in enumerate(zip(lanes, sts, v_new, d_outs, dsts, d_vn)):
            du_ref[:, sl] = dvn
            dw_ref[:, sl] = -mm_nt(dvn, st)
            dqd_ref[:, sl] = mm_nt(do, st)
            dkd_ref[:, sl] = mm_nt(vn, ds)
            dqk_ref[:, sl] = mm_nt(do, jnp.concatenate([vn, zeros], axis=0))
            dgl_ref[:, sl] = jnp.where(rows == 0, jnp.sum(st * ds, axis=0, keepdims=True), 0.0)
            dstate[h] = ds * gl_ref[0:1, sl] + mm_tn(qd_ref[:, sl], do) - mm_tn(w_ref[:, sl], dvn)

    blk = pl.BlockSpec((CHUNK, width), lambda n: (last - n, 0))
    return _pc(body, name="dn_seq_bwd", grid=(n_chunks,),
               in_specs=[blk] * 6 + [pl.BlockSpec((CHUNK, width), lambda n: (last - n, zd_off)), pl.BlockSpec((1, HEAD_DIM), lambda n: (0, 0)),
                         pl.BlockSpec((1, n_heads, HEAD_DIM, HEAD_DIM), lambda n: (last - n, 0, 0, 0)), blk],
               out_specs=[blk] * 7 + [pl.BlockSpec((1, HEAD_DIM), lambda n: (0, 0))],
               out_shape=[jax.ShapeDtypeStruct((tp, width), f32)] * 6 + [jax.ShapeDtypeStruct((tp, width), bf16),
                                                                          jax.ShapeDtypeStruct((1, HEAD_DIM), f32)],
               scratch_shapes=[pltpu.VMEM((n_heads, HEAD_DIM, HEAD_DIM), f32)],
               compiler_params=_params("arbitrary"))(*inter, proj, dn_norm_w, states, dy)


def loss_stage(out, final_w, target):
    tp, d = out.shape
    n_tiles = tp // CHUNK

    def body(o_ref, w_ref, t_ref, loss_ref, do_ref, dob_ref, dw_ref):
        i = pl.program_id(0)

        @pl.when(i == 0)
        def _():
            loss_ref[...] = jnp.zeros_like(loss_ref)
            dw_ref[...] = jnp.zeros_like(dw_ref)

        scored = (i > 0).astype(f32)
        val, (do, dw) = jax.value_and_grad(lambda o, w: scored * loss_fn(o, w, t_ref[...]), argnums=(0, 1))(o_ref[...], w_ref[...])
        loss_ref[...] += jnp.full(loss_ref.shape, val, f32)
        do_ref[...] = do
        dob_ref[...] = do.astype(bf16)
        dw_ref[...] += dw

    row = pl.BlockSpec((CHUNK, d), lambda i: (i, 0))
    vec = pl.BlockSpec((1, d), lambda i: (0, 0))
    return _pc(body, name="loss_stage", grid=(n_tiles,),
               in_specs=[row, vec, pl.BlockSpec((CHUNK, d), lambda i: (jnp.maximum(i - 1, 0), 0))],
               out_specs=[pl.BlockSpec((1, LANES), lambda i: (0, 0)), row, row, vec],
               out_shape=[jax.ShapeDtypeStruct((1, LANES), f32), jax.ShapeDtypeStruct((tp, d), f32),
                          jax.ShapeDtypeStruct((tp, d), bf16), jax.ShapeDtypeStruct((1, d), f32)],
               compiler_params=_params("arbitrary"))(out, final_w, target)


def _adam_update(w, g, m, v):
    nm = ADAM_B1 * m + (1.0 - ADAM_B1) * g
    nv = ADAM_B2 * v + (1.0 - ADAM_B2) * jnp.square(g)
    m_hat = nm / (1.0 - ADAM_B1 ** ADAM_STEP)
    v_hat = nv / (1.0 - ADAM_B2 ** ADAM_STEP)
    return -ADAM_LR * (m_hat / (jnp.sqrt(v_hat) + ADAM_EPS) + ADAM_WD * w), nm, nv


def adamw(name, w, g, m, v):
    rows, cols = w.shape
    t = _row_tile(rows, cols, 7)

    def body(w_ref, g_ref, m_ref, v_ref, d_ref, nm_ref, nv_ref):
        d_ref[...], nm_ref[...], nv_ref[...] = _adam_update(w_ref[...], g_ref[...], m_ref[...], v_ref[...])

    blk = pl.BlockSpec((t, cols), lambda i: (i, 0))
    return _pc(body, name=name, grid=(rows // t,), in_specs=[blk] * 4, out_specs=[blk] * 3,
               out_shape=[jax.ShapeDtypeStruct(w.shape, f32)] * 3, compiler_params=_params("parallel"))(w, g, m, v)


def adamw_joined(name, w, g_mine, g_theirs, m, v, half, axis):
    rows, cols = w.shape
    hr, hc = g_mine.shape
    tr, tc = _tile(hr, hc, 9)
    nr, nc = hr // tr, hc // tc

    def body(half_ref, w_ref, gm_ref, gt_ref, m_ref, v_ref, g_ref, d_ref, nm_ref, nv_ref):
        pos = pl.program_id(axis) // (nr if axis == 0 else nc)
        g = jnp.where(pos == half_ref[0], gm_ref[...], gt_ref[...])
        delta, nm, nv = _adam_update(w_ref[...], g, m_ref[...], v_ref[...])
        g_ref[...] = g
        d_ref[...] = delta
        nm_ref[...] = nm
        nv_ref[...] = nv

    whole = pl.BlockSpec((tr, tc), lambda i, j, hf: (i, j))
    part = pl.BlockSpec((tr, tc), lambda i, j, hf: (i % nr, j % nc))
    grid_spec = pltpu.PrefetchScalarGridSpec(num_scalar_prefetch=1, grid=(rows // tr, cols // tc),
                                             in_specs=[whole, part, part, whole, whole], out_specs=[whole] * 4)
    return _pc(body, name=name, grid_spec=grid_spec, out_shape=[jax.ShapeDtypeStruct(w.shape, f32)] * 4,
               compiler_params=_params("parallel", "parallel"))(half, w, g_mine, g_theirs, m, v)


def add_halves(name, g, recv, half, kind):
    s, r, c = recv.shape
    tr, tc = _tile(r, c, 3)
    nc = c // tc

    def body(half_ref, g_ref, r_ref, o_ref):
        o_ref[...] = (g_ref[...].reshape(r_ref.shape).astype(f32) + r_ref[...].astype(f32)).astype(bf16)

    if kind == "lead":
        g_spec = pl.BlockSpec((1, 1, tr, tc), lambda i, j, k, hf: (hf[0], i, j, k))
    else:
        g_spec = pl.BlockSpec((1, tr, tc), lambda i, j, k, hf: (i, j, hf[0] * nc + k))
    blk = pl.BlockSpec((1, tr, tc), lambda i, j, k, hf: (i, j, k))
    grid_spec = pltpu.PrefetchScalarGridSpec(num_scalar_prefetch=1, grid=(s, r // tr, nc), in_specs=[g_spec, blk], out_specs=blk)
    return _pc(body, name=name, grid_spec=grid_spec, out_shape=jax.ShapeDtypeStruct((s, r, c), bf16),
               compiler_params=_params("parallel", "parallel", "parallel"))(half, g, recv)


def add_first(name, parts, got, chips):
    _, r, c = parts.shape
    tr, tc = _tile(r, c, 6)

    def body(chips_ref, mine_ref, theirs_ref, got_ref, keep_ref, pass_ref):
        keep_ref[...] = mine_ref[0].astype(f32) + got_ref[0].astype(f32)
        pass_ref[...] = (theirs_ref[0].astype(f32) + got_ref[1].astype(f32)).astype(bf16)

    blk = pl.BlockSpec((tr, tc), lambda i, j, ch: (i, j))
    grid_spec = pltpu.PrefetchScalarGridSpec(
        num_scalar_prefetch=1, grid=(r // tr, c // tc),
        in_specs=[pl.BlockSpec((1, tr, tc), lambda i, j, ch: (ch[0], i, j)), pl.BlockSpec((1, tr, tc), lambda i, j, ch: (ch[1], i, j)),
                  pl.BlockSpec((2, tr, tc), lambda i, j, ch: (0, i, j))],
        out_specs=[blk, blk])
    return _pc(body, name=name, grid_spec=grid_spec, out_shape=[jax.ShapeDtypeStruct((r, c), f32), jax.ShapeDtypeStruct((r, c), bf16)],
               compiler_params=_params("parallel", "parallel"))(chips, parts, parts, got)


def add_second(name, kept, got):
    r, c = kept.shape
    tr, tc = _tile(r, c, 3)

    def body(k_ref, g_ref, o_ref):
        o_ref[...] = k_ref[...] + g_ref[...].astype(f32)

    blk = pl.BlockSpec((tr, tc), lambda i, j: (i, j))
    return _pc(body, name=name, grid=(r // tr, c // tc), in_specs=[blk, blk], out_specs=blk,
               out_shape=jax.ShapeDtypeStruct((r, c), f32), compiler_params=_params("parallel", "parallel"))(kept, got)


def sum_leading(name, x, out_dtype=f32):
    s, r, c = x.shape
    tr, tc = _tile(r, c, s + 1)

    def body(x_ref, o_ref):
        acc = x_ref[0].astype(f32)
        for i in range(1, s):
            acc = acc + x_ref[i].astype(f32)
        o_ref[...] = acc.astype(out_dtype)

    return _pc(body, name=name, grid=(r // tr, c // tc), in_specs=[pl.BlockSpec((s, tr, tc), lambda i, j: (0, i, j))],
               out_specs=pl.BlockSpec((tr, tc), lambda i, j: (i, j)), out_shape=jax.ShapeDtypeStruct((r, c), out_dtype),
               compiler_params=_params("parallel", "parallel"))(x)


def _place():
    x, y, c = lax.axis_index("x"), lax.axis_index("y"), lax.axis_index("c")
    return x, y, c


def _route(x, y, c):
    return (x ^ (1 - c), y ^ c), (x ^ c, y ^ (1 - c)), (1 - x, 1 - y)


def _half_view(ref, half, kind, lead=()):
    if kind == "lead":
        return ref.at[(*lead, half)]
    width = ref.shape[-1] // 2
    return ref.at[(*lead, *([slice(None)] * (len(ref.shape) - len(lead) - 1)), pl.ds(half * width, width))]


def _gather_plan(ins, outs, sems, kinds):
    n = len(ins)
    send1, recv1, send2, recv2 = sems
    x, y, c = _place()
    chip = 2 * x + y
    sibling = (x, y, 1 - c)
    near, far, diag = _route(x, y, c)
    near_id, far_id, diag_id = [2 * cx + cy for cx, cy in (near, far, diag)]

    def remote(src, dst, s_sem, r_sem, to):
        return pltpu.make_async_remote_copy(src_ref=src, dst_ref=dst, send_sem=s_sem, recv_sem=r_sem, device_id=to, device_id_type=MESH)

    def slab(a, chip_id, half):
        return _half_view(outs[a], half, kinds[a], (chip_id,))

    def landed(a, chip_id, sem, frm):
        return remote(slab(a, chip_id, c), slab(a, chip_id, c), send1.at[a, sem], recv1.at[a, sem], (*frm, c))

    def onward(a, chip_id, sem):
        return remote(slab(a, chip_id, c), slab(a, chip_id, c), send2.at[a, sem], recv2.at[a, sem], sibling)

    own = [remote(_half_view(ins[a], c, kinds[a]), slab(a, chip, c), send1.at[a, j], recv1.at[a, j], (*to, c))
           for a in range(n) for j, to in enumerate((near, far))]
    relay = [remote(slab(a, near_id, c), slab(a, near_id, c), send1.at[a, 2], recv1.at[a, 2], (*far, c)) for a in range(n)]
    to_sibling = [[onward(a, cid, sem) for sem, cid in enumerate((near_id, far_id, diag_id))] for a in range(n)]
    from_sibling = [remote(slab(a, cid, 1 - c), slab(a, cid, 1 - c), send2.at[a, sem], recv2.at[a, sem], sibling)
                    for a in range(n) for sem, cid in enumerate((far_id, near_id, diag_id))]

    def start_own():
        for cp in own:
            cp.start()

    def pass_near():
        for a in range(n):
            landed(a, near_id, 0, near).wait_recv()
            relay[a].start()
            to_sibling[a][0].start()

    def pass_far():
        for sem, cid in ((1, far_id), (2, diag_id)):
            for a in range(n):
                landed(a, cid, sem, far).wait_recv()
                to_sibling[a][sem].start()

    def finish():
        for cp in from_sibling:
            cp.wait_recv()
        for cp in own + relay + [cp for row in to_sibling for cp in row]:
            cp.wait_send()

    return [start_own, pass_near, pass_far, finish]


def _gather_shapes(shards):
    return ([jax.ShapeDtypeStruct((4,) + s.shape, s.dtype) for s in shards], [pltpu.SemaphoreType.DMA((len(shards), 3))] * 4)


def gather_weights(shards, kinds):
    n = len(shards)

    def body(*refs):
        for emit in _gather_plan(refs[:n], refs[n:2 * n], refs[2 * n:], kinds):
            emit()

    out_shapes, sems = _gather_shapes(shards)
    return _pc(body, name="gather_weights", in_specs=[_ANY] * n, out_specs=[_ANY] * n, out_shape=out_shapes, scratch_shapes=sems)(*shards)


def hosted_gather(shards, kinds):
    out_shapes, sems = _gather_shapes(shards)
    return Hosted(shards, out_shapes, sems, lambda i, o, s: list(zip((0.0, 0.45, 0.8, 1.0), _gather_plan(i, o, s, kinds))))


def swap_with_sibling(name, sends):
    n = len(sends)

    def body(*refs):
        ins, outs = refs[:n], refs[n:2 * n]
        send, recv = refs[2 * n:]
        x, y, c = _place()
        cps = [pltpu.make_async_remote_copy(src_ref=ins[a], dst_ref=outs[a], send_sem=send.at[a], recv_sem=recv.at[a],
                                            device_id=(x, y, 1 - c), device_id_type=MESH) for a in range(n)]
        for cp in cps:
            cp.start()
        for cp in cps:
            cp.wait()

    return _pc(body, name=name, in_specs=[_ANY] * n, out_specs=[_ANY] * n,
               out_shape=[jax.ShapeDtypeStruct(s.shape, s.dtype) for s in sends],
               scratch_shapes=[pltpu.SemaphoreType.DMA((n,))] * 2)(*sends)


def send_grad_halves(name, grads, kinds):
    n = len(grads)

    def body(*refs):
        ins, outs = refs[:n], refs[n:2 * n]
        send, recv = refs[2 * n:]
        x, y, c = _place()
        cps = [pltpu.make_async_remote_copy(src_ref=_half_view(ins[a], 1 - c, kinds[a]), dst_ref=outs[a], send_sem=send.at[a],
                                            recv_sem=recv.at[a], device_id=(x, y, 1 - c), device_id_type=MESH) for a in range(n)]
        for cp in cps:
            cp.start()
        for cp in cps:
            cp.wait()

    shape = lambda g, kind: g.shape[1:] if kind == "lead" else g.shape[:-1] + (g.shape[-1] // 2,)
    return _pc(body, name=name, in_specs=[_ANY] * n, out_specs=[_ANY] * n,
               out_shape=[jax.ShapeDtypeStruct(shape(g, k), g.dtype) for g, k in zip(grads, kinds)],
               scratch_shapes=[pltpu.SemaphoreType.DMA((n,))] * 2)(*grads)


def _scatter_first_plan(ins, outs, sems):
    n = len(ins)
    send, recv = sems
    x, y, c = _place()
    near, _, diag = _route(x, y, c)
    cps = [pltpu.make_async_remote_copy(src_ref=ins[a].at[2 * cx + cy], dst_ref=outs[a].at[j], send_sem=send.at[a, j],
                                        recv_sem=recv.at[a, j], device_id=(*near, c), device_id_type=MESH)
           for a in range(n) for j, (cx, cy) in enumerate((near, diag))]

    def start():
        for cp in cps:
            cp.start()

    def wait():
        for cp in cps:
            cp.wait()

    return [start, wait]


def _scatter_first_shapes(parts):
    return ([jax.ShapeDtypeStruct((2,) + p.shape[1:], p.dtype) for p in parts], [pltpu.SemaphoreType.DMA((len(parts), 2))] * 2)


def hosted_scatter_first(parts):
    out_shapes, sems = _scatter_first_shapes(parts)
    return Hosted(parts, out_shapes, sems, lambda i, o, s: list(zip((0.0, 1.0), _scatter_first_plan(i, o, s))))


def _far_swap_plan(ins, outs, sems):
    send, recv = sems
    x, y, c = _place()
    _, far, _ = _route(x, y, c)
    cps = [pltpu.make_async_remote_copy(src_ref=ins[a], dst_ref=outs[a], send_sem=send.at[a], recv_sem=recv.at[a],
                                        device_id=(*far, c), device_id_type=MESH) for a in range(len(ins))]

    def start():
        for cp in cps:
            cp.start()

    def wait():
        for cp in cps:
            cp.wait()

    return [start, wait]


def hosted_scatter_second(parts):
    return Hosted(parts, [jax.ShapeDtypeStruct(p.shape, p.dtype) for p in parts], [pltpu.SemaphoreType.DMA((len(parts),))] * 2,
                  lambda i, o, s: list(zip((0.0, 1.0), _far_swap_plan(i, o, s))))


def scatter_second_and_small(parts, pack):
    n = len(parts)

    def body(*refs):
        ins, pack_ref, outs, packs_ref = refs[:n], refs[n], refs[n + 1:2 * n + 1], refs[2 * n + 1]
        send, recv, pack_send, pack_recv = refs[2 * n + 2:]
        x, y, c = _place()
        me = 4 * x + 2 * y + c
        _, far, _ = _route(x, y, c)
        cps = [pltpu.make_async_remote_copy(src_ref=ins[a], dst_ref=outs[a], send_sem=send.at[a], recv_sem=recv.at[a],
                                            device_id=(*far, c), device_id_type=MESH) for a in range(n)]
        flips = [(fx, fy, fc) for fx in (0, 1) for fy in (0, 1) for fc in (0, 1)][1:]
        peers = [(x ^ fx, y ^ fy, c ^ fc) for fx, fy, fc in flips]
        pack_cps = [pltpu.make_async_remote_copy(src_ref=pack_ref, dst_ref=packs_ref.at[me], send_sem=pack_send.at[j],
                                                 recv_sem=pack_recv.at[j], device_id=p, device_id_type=MESH) for j, p in enumerate(peers)]
        for cp in cps + pack_cps:
            cp.start()
        for cp in cps:
            cp.wait()
        for j, (px, py, pc) in enumerate(peers):
            slab = packs_ref.at[4 * px + 2 * py + pc]
            pltpu.make_async_remote_copy(src_ref=slab, dst_ref=slab, send_sem=pack_send.at[j], recv_sem=pack_recv.at[j],
                                         device_id=(px, py, pc), device_id_type=MESH).wait_recv()
        for cp in pack_cps:
            cp.wait_send()

    outs = _pc(body, name="scatter_second_and_small", in_specs=[_ANY] * (n + 1), out_specs=[_ANY] * (n + 1),
               out_shape=[jax.ShapeDtypeStruct(p.shape, p.dtype) for p in parts] + [jax.ShapeDtypeStruct((8,) + pack.shape, pack.dtype)],
               scratch_shapes=[pltpu.SemaphoreType.DMA((n,))] * 2 + [pltpu.SemaphoreType.DMA((7,))] * 2)(*parts, pack)
    return outs[:n], outs[n]


def _pack(arrays, width):
    flat = jnp.concatenate([a.reshape(-1).astype(f32) for a in arrays])
    return jnp.pad(flat, (0, (-flat.shape[0]) % (8 * width))).reshape(-1, width)


def _unpack(pack, like, width):
    flat, out, at = pack.reshape(-1), [], 0
    for a in like:
        size = 1
        for s in a.shape:
            size *= s
        out.append(flat[at:at + size].reshape(a.shape))
        at += size
    return out


def _halves(a2d):
    r, c = a2d.shape
    return a2d.reshape(2, r // 2, c)


def local_step(x, meta, norm_w, w_main, w_ba, conv_w, a_log, dt_bias, pool_mix, pool_scale, dn_norm_w,
               late_weights, final_w, target, grad_hooks=None):
    seq, d = x.shape
    pw = pool_scale.shape[1]
    dw = conv_w.shape[1] // 3
    n_heads = dw // HEAD_DIM
    tp = FRONT_PAD + N_META + seq
    hp = jnp.concatenate([jnp.zeros((FRONT_PAD, d), f32), meta, x], axis=0)
    tm_big = tp // 2 if tp % 32 == 0 else tp
    tm_norm = max(t for t in range(16, min(tp, 352) + 1, 16) if tp % t == 0)
    tile = min(512, d)
    off_q = 2 * pw
    off_zd = off_q + 3 * dw
    off_gp = off_zd + dw
    off_gd = off_gp + d
    a_log128 = jnp.pad(a_log, ((0, 0), (0, LANES - n_heads)))
    dt128 = jnp.pad(dt_bias, ((0, 0), (0, LANES - n_heads)))

    xn = norm_fwd(hp, norm_w, tm_norm)
    if isinstance(late_weights[0], Hosted):
        proj, *fetched = matmul("proj", xn, w_main, tb=True, tm=tp, tn=tile, hosted=late_weights[0])
        w_pool_out, w_dn_out, w_o = late_weights[1](fetched)
    else:
        proj = matmul("proj", xn, w_main, tb=True, tm=tp, tn=tile)
        w_pool_out, w_dn_out, w_o = late_weights
    ba = matmul("proj_ba", xn, w_ba, tb=True, tm=tp, tn=LANES)
    y_pool = pool_fwd(proj, pool_mix, pool_scale, pw)
    q, k, v, beta_b, g_b = dn_pre_fwd(proj, ba, conv_w, a_log128, dt128, n_heads, off_q // HEAD_DIM)
    assert off_zd % dw == 0
    *inter, tmats = dn_intra_fwd(q, k, v, beta_b, g_b, n_heads)
    y_dn, states = dn_seq_fwd(inter, proj, dn_norm_w, n_heads, off_zd // dw)
    a_mat = matmul("pool_out", y_pool, w_pool_out, tm=tp, tn=tile)

    def merge(acc, a_t, gp_t, gd_t):
        return acc, sigmoid(gp_t) * a_t + sigmoid(gd_t) * acc

    b_mat, merged = matmul("dn_out_merge", y_dn, w_dn_out, tm=tm_big, tn=tile, extras=[(a_mat, 0), (proj, off_gp), (proj, off_gd)],
                           epi=merge, out_dtypes=(f32, bf16))
    out = matmul("out_proj", merged, w_o, tm=tm_big, tn=tile, extras=[(hp, 0)], epi=lambda acc, h_t: (acc + h_t,))
    loss, dout, dout_b, dfinal_w = loss_stage(out, final_w, target)

    def unmerge(dm, a_t, b_t, gp_t, gd_t):
        sp, sd = sigmoid(gp_t), sigmoid(gd_t)
        return dm * sp, dm * sd, dm * a_t * sp * (1.0 - sp), dm * b_t * sd * (1.0 - sd)

    d_a, d_b, d_gp, d_gd = matmul("d_merged", dout_b, w_o, tb=True, tm=tm_big, tn=tile,
                                  extras=[(a_mat, 0), (b_mat, 0), (proj, off_gp), (proj, off_gd)], epi=unmerge,
                                  out_dtypes=(bf16,) * 4)
    g_w_o = matmul("g_w_o", merged, dout_b, ta=True, tm=tile, tn=tile, out_dtypes=(bf16,))
    d_y_pool = matmul("d_y_pool", d_a, w_pool_out, tb=True, tm=tp, tn=tile)
    g_w_pool_out = matmul("g_w_pool_out", y_pool, d_a, ta=True, tm=tile, tn=tile, out_dtypes=(bf16,))
    d_y_dn = matmul("d_y_dn", d_b, w_dn_out, tb=True, tm=tp, tn=tile)
    g_w_dn_out = matmul("g_w_dn_out", y_dn, d_b, ta=True, tm=tile, tn=tile, out_dtypes=(bf16,))
    d_u, d_zp, g_pool_mix, g_pool_scale = pool_bwd(proj, pool_mix, pool_scale, d_y_pool, pw)
    *d_inter, d_zd, g_dn_norm_w = dn_seq_bwd(inter, proj, dn_norm_w, states, d_y_dn, n_heads, off_zd // dw)
    d_q, d_k, d_v, d_beta, d_g = dn_intra_bwd(q, k, v, beta_b, g_b, tmats, d_inter, n_heads)
    d_qr, d_kr, d_vr, d_ba, g_cq, g_ck, g_cv, g_a_log, g_dt = dn_pre_bwd(
        proj, ba, conv_w, a_log128, dt128, (d_q, d_k, d_v, d_beta, d_g), n_heads, off_q // HEAD_DIM)
    d_proj = jnp.concatenate([d_u, d_zp, d_qr, d_kr, d_vr, d_zd, d_gp, d_gd], axis=1)
    d_ba_b = cast_bf16("cast_d_ba", d_ba)
    early = grad_hooks[0](g_w_pool_out, g_w_dn_out, g_w_o, g_pool_mix) if grad_hooks else None
    res = matmul("g_w_main", d_proj, xn, ta=True, tm=tile, tn=tile, out_dtypes=(bf16,), hosted=early)
    g_w_main, early_landed = (res[0], list(res[1:])) if early else (res, [])
    g_w_ba = matmul("g_w_ba", d_ba_b, xn, ta=True, tm=LANES, tn=tile, out_dtypes=(bf16,))
    hosted, carried = grad_hooks[1](g_w_main, g_w_ba, early_landed) if grad_hooks else (None, None)
    dxn_ba = matmul("dxn_ba", d_ba_b, w_ba, tm=tp, tn=tile)
    n_cols = d_proj.shape[1]
    tk_dxn = max(t for t in range(LANES, min(3584, n_cols) + 1, LANES) if n_cols % t == 0)
    res = matmul("dxn", d_proj, w_main, tm=tm_big, tn=tile, tk=tk_dxn, extras=[(dxn_ba, 0)], epi=lambda acc, e: (acc + e,), hosted=hosted)
    dxn, landed = (res[0], list(res[1:])) if hosted else (res, [])
    dh, g_norm_w = norm_bwd(hp, norm_w, dxn, dout, tm_norm)
    g_conv = jnp.concatenate([g_cq, g_ck, g_cv], axis=1)
    return (loss, dh, g_norm_w, g_w_main, g_w_ba, g_conv, g_a_log[:, :n_heads], g_dt[:, :n_heads], g_pool_mix, g_pool_scale,
            g_dn_norm_w, g_w_pool_out, g_w_dn_out, g_w_o, dfinal_w, (carried, landed))


def kernel(x, meta_tokens, norm_w, w_in, conv_w, A_log, dt_bias, pool_mix, pool_scale, dn_norm_w, w_pool_out, w_dn_out, w_o, final_norm_w, loss_target, m_meta_tokens, m_norm_w, m_w_in, m_conv_w, m_A_log, m_dt_bias, m_pool_mix, m_pool_scale, m_dn_norm_w, m_w_pool_out, m_w_dn_out, m_w_o, m_final_norm_w, v_meta_tokens, v_norm_w, v_w_in, v_conv_w, v_A_log, v_dt_bias, v_pool_mix, v_pool_scale, v_dn_norm_w, v_w_pool_out, v_w_dn_out, v_w_o, v_final_norm_w):
    d = x.shape[-1]
    pw = pool_scale.shape[-1]
    dw = w_dn_out.shape[1] * 4
    n_heads = dw // HEAD_DIM
    gdim = pw // POOL_GROUPS
    chip = 2 * lax.axis_index("x") + lax.axis_index("y")
    core = lax.axis_index("c")

    half = core.astype(jnp.int32).reshape(1)
    me = 4 * lax.axis_index("x") + 2 * lax.axis_index("y") + core

    w_in_t = w_in[0].T
    mix_s = pool_mix[0].reshape(POOL_GROUPS * (gdim // 4), gdim)
    small_s = _pack([meta_tokens, conv_w[0]], d)
    small_rows = small_s.shape[0]
    small_s = jnp.pad(small_s, ((0, (-small_rows) % 16), (0, 0)))
    sw = w_in_t.shape[0]
    n_main, n_ba = 2 * pw + 4 * dw, 2 * n_heads

    early = [cast_bf16("cast_w_in", w_in_t), _halves(cast_bf16("cast_mix", mix_s)), _halves(small_s)]
    late = [_halves(cast_bf16("cast_w_po", w_pool_out[0])), _halves(cast_bf16("cast_w_do", w_dn_out[0])),
            _halves(cast_bf16("cast_w_o", w_o[0]))]
    fill = lambda g, own: lax.dynamic_update_slice(g, own[None], (chip,) + (0,) * own.ndim)
    g_in, g_mix, g_small = [fill(g, own) for g, own in zip(gather_weights(early, ["cols", "lead", "lead"]), early)]

    def shard_rows(lo, hi):
        cut = [(max(lo, j * sw), min(hi, (j + 1) * sw), j) for j in range(4)]
        return [g_in[j, a - j * sw:b - j * sw] for a, b, j in cut if a < b]

    w_main = jnp.concatenate(shard_rows(0, n_main) + shard_rows(n_main + n_ba, 4 * sw), axis=0)
    w_ba = jnp.pad(jnp.concatenate(shard_rows(n_main, n_main + n_ba), axis=0), ((0, LANES - n_ba), (0, 0)))
    cat_cols = lambda g: jnp.concatenate([g[j].reshape(-1, g.shape[-1]) for j in range(4)], axis=1)

    def late_weights(fetched):
        g_po, g_do, g_o = [fill(g, own) for g, own in zip(fetched, late)]
        return cat_cols(g_po), g_do.reshape(-1, g_do.shape[-1]), g_o.reshape(-1, g_o.shape[-1])

    mix_full = g_mix.reshape(4, POOL_GROUPS, gdim // 4, gdim).transpose(1, 0, 2, 3).reshape(POOL_GROUPS, gdim, gdim)
    smalls = [_unpack(g_small[j].reshape(-1, d)[:small_rows], [meta_tokens, conv_w[0]], d) for j in range(4)]
    meta_full = jnp.concatenate([s[0] for s in smalls], axis=1)
    conv_full = jnp.concatenate([s[1] for s in smalls], axis=1)

    names = ["w_in", "w_po", "w_do", "w_o", "mix"]
    kinds = ["cols", "lead", "lead", "lead", "lead"]

    far_chip = 2 * (lax.axis_index("x") ^ core) + (lax.axis_index("y") ^ (1 - core))
    chips = jnp.stack([chip, far_chip]).astype(jnp.int32)
    col_parts = lambda g: g.reshape(2, g.shape[0] // 2, 4, g.shape[1] // 4).transpose(0, 2, 1, 3)
    row_parts = lambda g: g.reshape(4, 2, g.shape[0] // 8, g.shape[1]).transpose(1, 0, 2, 3)

    def pair_sums(tag, group, parts):
        group_kinds = [kinds[names.index(nm)] for nm in group]
        from_sibling = send_grad_halves("send_grad_halves_" + tag, parts, group_kinds)
        return [add_halves("add_" + nm, p, r, half, k) for nm, p, r, k in zip(group, parts, from_sibling, group_kinds)]

    later = names[1:]
    state = {}

    def later_grads(g_w_po, g_w_do, g_w_o_full, g_mix_full):
        mix_rows = POOL_GROUPS * (gdim // 4)
        mix_parts = (g_mix_full.astype(bf16).reshape(POOL_GROUPS, 4, gdim // 4, gdim).transpose(1, 0, 2, 3)
                     .reshape(4, 2, mix_rows // 2, gdim).transpose(1, 0, 2, 3))
        state["later_sums"] = pair_sums("later", later, [col_parts(g_w_po), row_parts(g_w_do), row_parts(g_w_o_full), mix_parts])
        return hosted_scatter_first(state["later_sums"])

    def input_grads(g_w_main, g_w_ba, later_from_near):
        def grad_rows(lo, hi):
            segs = [(0, n_main, g_w_main, 0), (n_main, n_main + n_ba, g_w_ba, 0), (n_main + n_ba, 4 * sw, g_w_main, n_main)]
            cut = [(max(lo, s0), min(hi, s1), s0, arr, off) for s0, s1, arr, off in segs]
            return [arr[a - s0 + off:b - s0 + off] for a, b, s0, arr, off in cut if a < b]

        kept, passed = zip(*[add_first("add1_" + nm, p, g, chips) for nm, p, g in zip(later, state["later_sums"], later_from_near)])
        state["later_kept"] = list(kept)
        in_parts = jnp.stack([jnp.concatenate(grad_rows(j * sw, (j + 1) * sw), axis=0) for j in range(4)])
        state["in_sums"] = pair_sums("input", names[:1], [in_parts])
        return hosted_scatter_first(state["in_sums"]) + hosted_scatter_second(list(passed)), None

    (loss, dh, g_norm_w, _, _, g_conv, g_a_log, g_dt, _, g_pool_scale, g_dn_norm_w, _, _, _, g_final_w,
     (_, landed)) = local_step(x[0], meta_full, norm_w, w_main, w_ba, conv_full, A_log, dt_bias, mix_full, pool_scale,
                               dn_norm_w, (hosted_gather(late, ["lead"] * 3), late_weights), final_norm_w.reshape(1, d),
                               loss_target[0], grad_hooks=(later_grads, input_grads))
    grad_x = dh[FRONT_PAD + N_META:][None]
    g_meta = dh[FRONT_PAD:FRONT_PAD + N_META]

    in_from_near, later_from_far = landed[0], landed[1:]
    in_kept, in_passed = add_first("add1_w_in", state["in_sums"][0], in_from_near, chips)
    small_like = [loss, g_norm_w, g_a_log, g_dt, g_pool_scale, g_dn_norm_w, g_final_w, g_conv, g_meta]
    pack = _pack(small_like, d)
    (in_from_far,), packs = scatter_second_and_small([in_passed], pack)
    mine = [add_second("add2_" + nm, k_, g) for nm, k_, g in zip(names, [in_kept] + state["later_kept"], [in_from_far] + list(later_from_far))]
    theirs = swap_with_sibling("swap_grad_halves", mine)
    total = sum_leading("sum_small", lax.dynamic_update_slice(packs, pack[None], (me, 0, 0)))
    (loss_t, g_norm_w, g_a_log, g_dt, g_pool_scale, g_dn_norm_w, g_final_w, g_conv, g_meta) = _unpack(total, small_like, d)
    loss_out = loss_t[0, 0]
    g_conv_s = lax.dynamic_slice_in_dim(g_conv, chip * (g_conv.shape[1] // 4), g_conv.shape[1] // 4, axis=1)
    g_meta_s = lax.dynamic_slice_in_dim(g_meta, chip * (d // 4), d // 4, axis=1)

    weights = [meta_tokens, norm_w, w_in, conv_w, A_log, dt_bias, pool_mix, pool_scale, dn_norm_w, w_pool_out, w_dn_out, w_o, final_norm_w]
    ms = [m_meta_tokens, m_norm_w, m_w_in, m_conv_w, m_A_log, m_dt_bias, m_pool_mix, m_pool_scale, m_dn_norm_w, m_w_pool_out, m_w_dn_out, m_w_o, m_final_norm_w]
    vs = [v_meta_tokens, v_norm_w, v_w_in, v_conv_w, v_A_log, v_dt_bias, v_pool_mix, v_pool_scale, v_dn_norm_w, v_w_pool_out, v_w_dn_out, v_w_o, v_final_norm_w]
    grads = [g_meta_s, g_norm_w, None, g_conv_s[None], g_a_log, g_dt, None, g_pool_scale, g_dn_norm_w, None, None, None, g_final_w.reshape(d)]
    deltas, new_ms, new_vs = [None] * 13, [None] * 13, [None] * 13
    big = [2, 9, 10, 11, 6]
    for i, nm, g_mine, g_theirs in zip(big, names, mine, theirs):
        if nm == "w_in":
            to2d, back, axis = (lambda t: t[0].T), (lambda t: t.T[None]), 1
        else:
            to2d, back, axis = (lambda t: t.reshape(-1, t.shape[-1])), (lambda t, i=i: t.reshape(weights[i].shape)), 0
        res = adamw_joined("adamw_" + nm, to2d(weights[i]), g_mine, g_theirs, to2d(ms[i]), to2d(vs[i]), half, axis)
        grads[i], deltas[i], new_ms[i], new_vs[i] = [back(t) for t in res]
    small_idx = [i for i in range(13) if i not in big]
    packs = [_pack([arrs[i] for i in small_idx], d) for arrs in (weights, grads, ms, vs)]
    outs = adamw("adamw_small", *packs)
    like = [weights[i] for i in small_idx]
    for res, dest in zip(outs, (deltas, new_ms, new_vs)):
        for i, val in zip(small_idx, _unpack(res, like, d)):
            dest[i] = val
    return (loss_out, grad_x, *grads, *deltas, *new_ms, *new_vs)
```

```python
import functools

import jax
import jax.numpy as jnp
from jax import lax
from jax.experimental import pallas as pl
from jax.experimental.pallas import tpu as pltpu

f32 = jnp.float32
bf16 = jnp.bfloat16
MESH = pl.DeviceIdType.MESH

N_META = 16
CHUNK = 64
FRONT_PAD = (-N_META) % CHUNK
HEAD_DIM = 128
POOL_GROUPS = 4
POOL_WINDOWS = (2, 4, 8, 16)
CONV_WIDTH = 4
NORM_EPS = 1e-6
ADAM_LR, ADAM_B1, ADAM_B2, ADAM_EPS, ADAM_WD, ADAM_STEP = 0.001, 0.9, 0.999, 1e-08, 0.01, 10
LANES = 128
V7X_VMEM_BYTES = 64 * 2**20
VMEM_LIMIT = V7X_VMEM_BYTES - 8 * 2**20


def _pc(body, **kw):
    return pl.pallas_call(body, **kw)


def _params(*sem, **kw):
    return pltpu.CompilerParams(dimension_semantics=sem or None, vmem_limit_bytes=VMEM_LIMIT, **kw)


def _dg(a, b, dims):
    return lax.dot_general(a, b, (dims, ((), ())), preferred_element_type=f32)


@jax.custom_vjp
def mm_nn(a, b):
    return _dg(a.astype(bf16), b.astype(bf16), ((1,), (0,)))


@jax.custom_vjp
def mm_nt(a, b):
    return _dg(a.astype(bf16), b.astype(bf16), ((1,), (1,)))


@jax.custom_vjp
def mm_tn(a, b):
    return _dg(a.astype(bf16).T, b.astype(bf16), ((1,), (0,)))


mm_nn.defvjp(lambda a, b: (mm_nn(a, b), (a, b)), lambda r, dy: (mm_nt(dy, r[1]), mm_tn(r[0], dy)))
mm_nt.defvjp(lambda a, b: (mm_nt(a, b), (a, b)), lambda r, dy: (mm_nn(dy, r[1]), mm_tn(dy, r[0])))
mm_tn.defvjp(lambda a, b: (mm_tn(a, b), (a, b)), lambda r, dy: (mm_nt(r[1], dy), mm_nn(r[0], dy)))


def _split3(x):
    hi = x.astype(bf16)
    r1 = x - hi.astype(f32)
    mid = r1.astype(bf16)
    lo = (r1 - mid.astype(f32)).astype(bf16)
    return hi, mid, lo


@jax.custom_vjp
def mm_sel(sel, x):
    s = sel.astype(bf16)
    d = ((1,), (0,))
    hi, mid, lo = _split3(x)
    return _dg(s, hi, d) + _dg(s, mid, d) + _dg(s, lo, d)


def _mm_sel_bwd(sel, dy):
    s = sel.astype(bf16)
    d = ((0,), (0,))
    hi, mid, lo = _split3(dy)
    return jnp.zeros_like(sel), _dg(s, hi, d) + _dg(s, mid, d) + _dg(s, lo, d)


mm_sel.defvjp(lambda sel, x: (mm_sel(sel, x), sel), _mm_sel_bwd)


@jax.custom_vjp
def mm_pick(x, sel):
    s = sel.astype(bf16)
    d = ((1,), (0,))
    hi, mid, lo = _split3(x)
    return _dg(hi, s, d) + _dg(mid, s, d) + _dg(lo, s, d)


def _mm_pick_bwd(sel, dy):
    s = sel.astype(bf16)
    d = ((1,), (1,))
    hi, mid, lo = _split3(dy)
    return _dg(hi, s, d) + _dg(mid, s, d) + _dg(lo, s, d), jnp.zeros_like(sel)


mm_pick.defvjp(lambda x, sel: (mm_pick(x, sel), sel), _mm_pick_bwd)


def tri_inv(ls):
    n = ls[0].shape[0]
    eye = (lax.broadcasted_iota(jnp.int32, (n, n), 0) == lax.broadcasted_iota(jnp.int32, (n, n), 1)).astype(f32)
    ms = [-l for l in ls]
    ts = [eye + m for m in ms]
    k = 1
    while 2 * k < CHUNK:
        ms = [mm_nn(m, m) for m in ms]
        ts = [t + mm_nn(t, m) for t, m in zip(ts, ms)]
        k *= 2
    return ts


@functools.partial(jax.custom_vjp, nondiff_argnums=(1,))
def shift_rows(x, j):
    n = x.shape[0]
    rows = lax.broadcasted_iota(jnp.int32, x.shape, 0)
    if j >= 0:
        return jnp.where(rows >= j, pltpu.roll(x, j, 0), 0.0)
    return jnp.where(rows < n + j, pltpu.roll(x, n + j, 0), 0.0)


shift_rows.defvjp(lambda x, j: (shift_rows(x, j), None), lambda j, _, dy: (shift_rows(dy, -j),))


def sigmoid(x):
    return 1.0 / (1.0 + jnp.exp(-x))


def silu(x):
    return x * sigmoid(x)


def softplus(x):
    return jnp.maximum(x, 0.0) + jnp.log(1.0 + jnp.exp(-jnp.abs(x)))


def rmsnorm(x, w):
    return x * lax.rsqrt(jnp.mean(x * x, axis=-1, keepdims=True) + NORM_EPS) * w


def l2norm(x):
    return x * lax.rsqrt(jnp.sum(x * x, axis=-1, keepdims=True) + NORM_EPS)


def pool_fn(u, zp, mix, scale, group):
    rows = lax.broadcasted_iota(jnp.int32, u.shape, 0)
    sums = []
    s, w = u, 1
    while w < POOL_WINDOWS[-1]:
        s = s + shift_rows(s, w)
        w *= 2
        sums.append(s)
    total = sums[-1]
    for gi in range(POOL_GROUPS - 2, -1, -1):
        total = jnp.where(group == gi, sums[gi], total)
    window = jnp.left_shift(2, group)
    cnt = jnp.clip(rows - (FRONT_PAD - 1), 1, window).astype(f32)
    pooled = total / cnt - u
    return mm_nn(pooled, mix) * scale * silu(zp)


PRE_HALO = 8


def conv_silu(x, w):
    k = CONV_WIDTH
    y = x * w[k - 1:k, :]
    for kk in range(k - 1):
        y = y + shift_rows(x, k - 1 - kk) * w[kk:kk + 1, :]
    return silu(y[PRE_HALO:])


def _lane_pick(row, idx):
    lanes = lax.broadcasted_iota(jnp.int32, row.shape, 1)
    return jnp.sum(jnp.where(lanes == idx, row, 0.0), axis=1, keepdims=True)


def dn_pre_fn(qr, kr, vr, ba, cwq, cwk, cwv, a_log, dt_bias, head, n_heads, row0):
    q = l2norm(conv_silu(qr, cwq)) * (HEAD_DIM ** -0.5)
    k = l2norm(conv_silu(kr, cwk))
    v = conv_silu(vr, cwv)
    r = lax.broadcasted_iota(jnp.int32, (LANES, LANES), 0)
    b_b = mm_pick(ba, (r == head).astype(f32))
    a_b = mm_pick(ba, (r == head + n_heads).astype(f32))
    real = lax.broadcasted_iota(jnp.int32, ba.shape, 0) + row0 >= FRONT_PAD
    beta_b = jnp.where(real, sigmoid(b_b), 0.0)
    g_b = jnp.where(real, -jnp.exp(_lane_pick(a_log, head)) * softplus(a_b + _lane_pick(dt_bias, head)), 0.0)
    return q, k, v, beta_b, g_b


def _chunk_masks(rows):
    r = lax.broadcasted_iota(jnp.int32, (rows, rows), 0)
    c = lax.broadcasted_iota(jnp.int32, (rows, rows), 1)
    same = (r // CHUNK) == (c // CHUNK)
    return same, jnp.logical_and(same, r >= c), jnp.logical_and(same, r > c)


def _lane0(rows):
    return (lax.broadcasted_iota(jnp.int32, (rows, LANES), 1) == 0).astype(bf16)


@jax.custom_vjp
def lane0_as_row(x):
    sel = _lane0(x.shape[0])
    d = ((1,), (1,))
    hi, mid, lo = _split3(x)
    return _dg(sel, hi, d) + _dg(sel, mid, d) + _dg(sel, lo, d)


def _lane0_as_row_bwd(rows, dy):
    sel = _lane0(rows)
    d = ((0,), (0,))
    hi, mid, lo = _split3(dy)
    return (_dg(hi, sel, d) + _dg(mid, sel, d) + _dg(lo, sel, d),)


lane0_as_row.defvjp(lambda x: (lane0_as_row(x), x.shape[0]), _lane0_as_row_bwd)


def gate_fn(g_b):
    rows = g_b.shape[0]
    same, causal, _ = _chunk_masks(rows)
    gcum_b = mm_sel(causal.astype(f32), g_b)
    glast_b = mm_sel(same.astype(f32), g_b)
    g_rows = jnp.broadcast_to(gcum_b[:, :1], (rows, rows))
    decay = jnp.where(causal, jnp.exp(jnp.where(causal, g_rows - lane0_as_row(gcum_b), 0.0)), 0.0)
    return decay, jnp.exp(gcum_b), jnp.exp(glast_b - gcum_b), jnp.exp(glast_b)


def _fold_matrix(rows):
    r = lax.broadcasted_iota(jnp.int32, (rows, LANES), 0)
    c = lax.broadcasted_iota(jnp.int32, (rows, LANES), 1)
    return (r % CHUNK == c).astype(bf16)


@jax.custom_vjp
def fold_chunks(x):
    return _dg(x.astype(bf16), _fold_matrix(x.shape[0]), ((1,), (0,)))


def _fold_chunks_bwd(rows, dy):
    fold = _fold_matrix(rows)
    d = ((1,), (1,))
    hi, mid, lo = _split3(dy)
    return (_dg(hi, fold, d) + _dg(mid, fold, d) + _dg(lo, fold, d),)


fold_chunks.defvjp(lambda x: (fold_chunks(x), x.shape[0]), _fold_chunks_bwd)


def lmat_fn(k, beta_b, decay):
    _, _, strict = _chunk_masks(k.shape[0])
    return jnp.where(strict, mm_nt(k * beta_b, k) * decay, 0.0)


def intra_fn(tmat, q, k, v, beta_b, decay, eg, kfac):
    _, causal, _ = _chunk_masks(q.shape[0])
    k_beta = k * beta_b
    u_c = mm_nn(tmat, v * beta_b)
    w_c = mm_nn(tmat, k_beta * eg)
    qk = jnp.where(causal, mm_nt(q, k) * decay, 0.0)
    return u_c, w_c, q * eg, k * kfac, fold_chunks(qk)


def gated_norm(o, norm_w, zd):
    return rmsnorm(o, norm_w) * silu(zd)


def loss_fn(o, w, tgt):
    err = rmsnorm(o, w) - tgt
    return 0.5 * jnp.sum(jnp.mean(err * err, axis=-1))


_ANY = pl.BlockSpec(memory_space=pl.ANY)


class Hosted:
    def __init__(self, arrays, out_shapes, sems, stages):
        self.arrays, self.out_shapes, self.sems, self.stages = list(arrays), list(out_shapes), list(sems), stages

    def __add__(self, other):
        ni, no, ns = len(self.arrays), len(self.out_shapes), len(self.sems)
        stages = lambda i, o, s: self.stages(i[:ni], o[:no], s[:ns]) + other.stages(i[ni:], o[no:], s[ns:])
        return Hosted(self.arrays + other.arrays, self.out_shapes + other.out_shapes, self.sems + other.sems, stages)


def matmul(name, a, b, *, ta=False, tb=False, tm, tn, tk=None, extras=(), epi=None, out_dtypes=(f32,), hosted=None):
    m, k = (a.shape[1], a.shape[0]) if ta else a.shape
    n = b.shape[0] if tb else b.shape[1]
    tm, tn, tk = min(tm, m), min(tn, n), min(tk or k, k)
    assert m % tm == 0 and n % tn == 0 and k % tk == 0, (name, m, n, k, tm, tn, tk)
    nm, nn, nk = m // tm, n // tn, k // tk
    a_spec = pl.BlockSpec((tk, tm), lambda i, j, kk: (kk, i)) if ta else pl.BlockSpec((tm, tk), lambda i, j, kk: (i, kk))
    b_spec = pl.BlockSpec((tn, tk), lambda i, j, kk: (j, kk)) if tb else pl.BlockSpec((tk, tn), lambda i, j, kk: (kk, j))
    ex_specs = []
    for _, off in extras:
        assert off % tn == 0, (name, off, tn)
        ex_specs.append(pl.BlockSpec((tm, tn), functools.partial(lambda i, j, kk, o: (i, o + j), o=off // tn)))
    n_ex, n_out = len(extras), len(out_dtypes)
    dims = ((0 if ta else 1,), (1 if tb else 0,))
    n_hin = len(hosted.arrays) if hosted else 0
    n_hout = len(hosted.out_shapes) if hosted else 0
    n_sem = len(hosted.sems) if hosted else 0

    def body(a_ref, b_ref, *rest):
        ex_refs, rest = rest[:n_ex], rest[n_ex:]
        hin_refs, rest = rest[:n_hin], rest[n_hin:]
        out_refs, rest = rest[:n_out], rest[n_out:]
        hout_refs, rest = rest[:n_hout], rest[n_hout:]
        sem_refs = rest[len(rest) - n_sem:] if n_sem else ()
        step = (pl.program_id(0) * nn + pl.program_id(1)) * nk + pl.program_id(2)
        stages = hosted.stages(hin_refs, hout_refs, sem_refs) if hosted else []
        for frac, emit in stages:
            if frac < 1.0:
                pl.when(step == int(frac * (nm * nn * nk - 1)))(emit)

        def finish(acc):
            res = epi(acc, *[r[...] for r in ex_refs]) if epi is not None else (acc,)
            for o_ref, val in zip(out_refs, res):
                o_ref[...] = val.astype(o_ref.dtype)

        p = _dg(a_ref[...], b_ref[...], dims)
        if nk == 1:
            finish(p)
        else:
            acc_ref = rest[0]
            kk = pl.program_id(2)

            @pl.when(kk == 0)
            def _():
                acc_ref[...] = p

            @pl.when(kk > 0)
            def _():
                acc_ref[...] += p

            @pl.when(kk == nk - 1)
            def _():
                finish(acc_ref[...])

        for frac, emit in stages:
            if frac >= 1.0:
                pl.when(step == nm * nn * nk - 1)(emit)

    outs = _pc(
        body, name=name, grid=(nm, nn, nk),
        in_specs=[a_spec, b_spec] + ex_specs + [_ANY] * n_hin,
        out_specs=[pl.BlockSpec((tm, tn), lambda i, j, kk: (i, j))] * n_out + [_ANY] * n_hout,
        out_shape=[jax.ShapeDtypeStruct((m, n), dt) for dt in out_dtypes] + (hosted.out_shapes if hosted else []),
        scratch_shapes=([pltpu.VMEM((tm, tn), f32)] if nk > 1 else []) + (hosted.sems if hosted else []),
        compiler_params=_params(*(("arbitrary",) * 3 if hosted else ("parallel", "parallel", "arbitrary"))),
    )(a, b, *[e for e, _ in extras], *(hosted.arrays if hosted else []))
    return outs[0] if len(outs) == 1 else outs


def _row_tile(rows, cols, n_arrays, itemsize=4, budget=24 * 2**20):
    best = None
    for t in range(16, rows + 1, 16):
        if rows % t == 0 and 2 * n_arrays * t * cols * itemsize <= budget:
            best = t
    return best or rows


def _tile(rows, cols, n_arrays, budget=24 * 2**20):
    if rows % 16 == 0 or cols % LANES != 0:
        return _row_tile(rows, cols, n_arrays, budget=budget), cols
    fits = [t for t in range(LANES, cols + 1, LANES) if cols % t == 0 and 2 * n_arrays * rows * t * 4 <= budget]
    return rows, (max(fits) if fits else LANES)


def cast_bf16(name, x):
    rows, cols = x.shape
    tr, tc = _tile(rows, cols, 2)

    def body(x_ref, o_ref):
        o_ref[...] = x_ref[...].astype(bf16)

    blk = pl.BlockSpec((tr, tc), lambda i, j: (i, j))
    return _pc(body, name=name, grid=(rows // tr, cols // tc), in_specs=[blk], out_specs=blk,
               out_shape=jax.ShapeDtypeStruct(x.shape, bf16), compiler_params=_params("parallel", "parallel"))(x)


def norm_fwd(hp, norm_w, tm):
    tp, d = hp.shape

    def body(h_ref, w_ref, o_ref):
        o_ref[...] = rmsnorm(h_ref[...], w_ref[...]).astype(bf16)

    return _pc(body, name="norm_fwd", grid=(tp // tm,),
               in_specs=[pl.BlockSpec((tm, d), lambda i: (i, 0)), pl.BlockSpec((1, d), lambda i: (0, 0))],
               out_specs=pl.BlockSpec((tm, d), lambda i: (i, 0)), out_shape=jax.ShapeDtypeStruct((tp, d), bf16),
               compiler_params=_params("parallel"))(hp, norm_w)


def norm_bwd(hp, norm_w, dxn, dout, tm, hosted=None):
    tp, d = hp.shape
    steps = tp // tm
    n_hin = len(hosted.arrays) if hosted else 0
    n_hout = len(hosted.out_shapes) if hosted else 0

    def body(h_ref, w_ref, dxn_ref, dout_ref, *rest):
        hin_refs, (dh_ref, dw_ref), rest = rest[:n_hin], rest[n_hin:n_hin + 2], rest[n_hin + 2:]
        stages = hosted.stages(hin_refs, rest[:n_hout], rest[n_hout:]) if hosted else []
        for frac, emit in stages:
            if frac < 1.0:
                pl.when(pl.program_id(0) == int(frac * (steps - 1)))(emit)
        _, vjp = jax.vjp(rmsnorm, h_ref[...], w_ref[...])
        dh, dw = vjp(dxn_ref[...])
        dh_ref[...] = dh + dout_ref[...]

        @pl.when(pl.program_id(0) == 0)
        def _():
            dw_ref[...] = jnp.zeros_like(dw_ref)

        dw_ref[...] += dw
        for frac, emit in stages:
            if frac >= 1.0:
                pl.when(pl.program_id(0) == steps - 1)(emit)

    row = pl.BlockSpec((tm, d), lambda i: (i, 0))
    vec = pl.BlockSpec((1, d), lambda i: (0, 0))
    outs = _pc(body, name="norm_bwd", grid=(steps,), in_specs=[row, vec, row, row] + [_ANY] * n_hin,
               out_specs=[row, vec] + [_ANY] * n_hout,
               out_shape=[jax.ShapeDtypeStruct((tp, d), f32), jax.ShapeDtypeStruct((1, d), f32)] + (hosted.out_shapes if hosted else []),
               scratch_shapes=hosted.sems if hosted else [],
               compiler_params=_params("arbitrary"))(hp, norm_w, dxn, dout, *(hosted.arrays if hosted else []))
    return outs[0], outs[1], list(outs[2:])


def pool_fwd(proj, mix, scale, pw):
    tp = proj.shape[0]
    g = pw // POOL_GROUPS

    def body(u_ref, z_ref, mix_ref, s_ref, y_ref):
        y_ref[...] = pool_fn(u_ref[...], z_ref[...], mix_ref[0], s_ref[...], pl.program_id(0)).astype(bf16)

    return _pc(body, name="pool_fwd", grid=(POOL_GROUPS,),
               in_specs=[pl.BlockSpec((tp, g), lambda i: (0, i)), pl.BlockSpec((tp, g), lambda i: (0, POOL_GROUPS + i)),
                         pl.BlockSpec((1, g, g), lambda i: (i, 0, 0)), pl.BlockSpec((1, g), lambda i: (0, i))],
               out_specs=pl.BlockSpec((tp, g), lambda i: (0, i)), out_shape=jax.ShapeDtypeStruct((tp, pw), bf16),
               compiler_params=_params("parallel"))(proj, proj, mix, scale)


def pool_bwd(proj, mix, scale, dy, pw):
    tp = proj.shape[0]
    g = pw // POOL_GROUPS

    def body(u_ref, z_ref, mix_ref, s_ref, dy_ref, du_ref, dz_ref, dmix_ref, ds_ref):
        grp = pl.program_id(0)
        _, vjp = jax.vjp(lambda u, z, m, s: pool_fn(u, z, m, s, grp), u_ref[...], z_ref[...], mix_ref[0].astype(f32), s_ref[...])
        du, dz, dmix, ds = vjp(dy_ref[...])
        du_ref[...] = du.astype(bf16)
        dz_ref[...] = dz.astype(bf16)
        dmix_ref[0] = dmix
        ds_ref[...] = ds

    col = pl.BlockSpec((tp, g), lambda i: (0, i))
    return _pc(body, name="pool_bwd", grid=(POOL_GROUPS,),
               in_specs=[col, pl.BlockSpec((tp, g), lambda i: (0, POOL_GROUPS + i)),
                         pl.BlockSpec((1, g, g), lambda i: (i, 0, 0)), pl.BlockSpec((1, g), lambda i: (0, i)), col],
               out_specs=[col, col, pl.BlockSpec((1, g, g), lambda i: (i, 0, 0)), pl.BlockSpec((1, g), lambda i: (0, i))],
               out_shape=[jax.ShapeDtypeStruct((tp, pw), bf16), jax.ShapeDtypeStruct((tp, pw), bf16),
                          jax.ShapeDtypeStruct((POOL_GROUPS, g, g), f32), jax.ShapeDtypeStruct((1, pw), f32)],
               compiler_params=_params("parallel"))(proj, proj, mix, scale, dy)


def _dn_pre_specs(tp, n_heads, q_off):
    hb = lambda off: pl.BlockSpec((tp, HEAD_DIM), functools.partial(lambda h, o: (0, o + h), o=off))
    cw = lambda off: pl.BlockSpec((CONV_WIDTH, HEAD_DIM), functools.partial(lambda h, o: (0, o + h), o=off))
    whole = lambda shape: pl.BlockSpec(shape, lambda h: (0, 0))
    return ([hb(q_off), hb(q_off + n_heads), hb(q_off + 2 * n_heads), whole((tp, LANES)),
             cw(0), cw(n_heads), cw(2 * n_heads), whole((1, LANES)), whole((1, LANES))], hb, cw, whole)


def _pre_rows(tp):
    return max(t for t in range(16, min(tp, 192) + 1, 16) if tp % t == 0)


def _with_history(ref, r0, rows):
    if r0 == 0:
        return jnp.concatenate([jnp.zeros((PRE_HALO, ref.shape[1]), f32), ref[0:rows, :]], axis=0)
    return ref[r0 - PRE_HALO:r0 + rows, :]


def dn_pre_fwd(proj, ba, conv_w, a_log, dt_bias, n_heads, q_off):
    tp = proj.shape[0]
    rows = _pre_rows(tp)
    in_specs, hb, _, _ = _dn_pre_specs(tp, n_heads, q_off)

    def body(q_ref, k_ref, v_ref, ba_ref, cq_ref, ck_ref, cv_ref, al_ref, dt_ref, *out_refs):
        for r0 in range(0, tp, rows):
            outs = dn_pre_fn(_with_history(q_ref, r0, rows), _with_history(k_ref, r0, rows), _with_history(v_ref, r0, rows),
                             ba_ref[r0:r0 + rows, :], cq_ref[...], ck_ref[...], cv_ref[...], al_ref[...], dt_ref[...],
                             pl.program_id(0), n_heads, r0)
            for o_ref, val in zip(out_refs, outs):
                o_ref[r0:r0 + rows, :] = val

    return _pc(body, name="dn_pre_fwd", grid=(n_heads,), in_specs=in_specs, out_specs=[hb(0)] * 5,
               out_shape=[jax.ShapeDtypeStruct((tp, n_heads * HEAD_DIM), f32)] * 5,
               compiler_params=_params("parallel"))(proj, proj, proj, ba, conv_w, conv_w, conv_w, a_log, dt_bias)


def dn_pre_bwd(proj, ba, conv_w, a_log, dt_bias, cots, n_heads, q_off):
    tp = proj.shape[0]
    rows = _pre_rows(tp)
    in_specs, hb, cw, whole = _dn_pre_specs(tp, n_heads, q_off)

    def body(q_ref, k_ref, v_ref, ba_ref, cq_ref, ck_ref, cv_ref, al_ref, dt_ref, dq_ref, dk_ref, dv_ref, db_ref, dg_ref,
             dqr_ref, dkr_ref, dvr_ref, dba_ref, dcq_ref, dck_ref, dcv_ref, dal_ref, ddt_ref):
        head = pl.program_id(0)

        @pl.when(head == 0)
        def _():
            dba_ref[...] = jnp.zeros_like(dba_ref)
            dal_ref[...] = jnp.zeros_like(dal_ref)
            ddt_ref[...] = jnp.zeros_like(ddt_ref)

        owed = [jnp.zeros((PRE_HALO, HEAD_DIM), f32)] * 3
        d_conv = [jnp.zeros((CONV_WIDTH, HEAD_DIM), f32)] * 3
        d_al, d_dt = jnp.zeros((1, LANES), f32), jnp.zeros((1, LANES), f32)
        for r0 in reversed(range(0, tp, rows)):
            fn = lambda *args, r0=r0: dn_pre_fn(*args, head, n_heads, r0)
            _, vjp = jax.vjp(fn, _with_history(q_ref, r0, rows), _with_history(k_ref, r0, rows), _with_history(v_ref, r0, rows),
                             ba_ref[r0:r0 + rows, :], cq_ref[...], ck_ref[...], cv_ref[...], al_ref[...], dt_ref[...])
            tile = slice(r0, r0 + rows)
            *d_raw, dba, dcq, dck, dcv, dal, ddt = vjp((dq_ref[tile, :], dk_ref[tile, :], dv_ref[tile, :], db_ref[tile, :], dg_ref[tile, :]))
            for i, (out_ref, d) in enumerate(zip((dqr_ref, dkr_ref, dvr_ref), d_raw)):
                out_ref[tile, :] = jnp.concatenate([d[PRE_HALO:rows], d[rows:] + owed[i]], axis=0).astype(bf16)
                owed[i] = d[:PRE_HALO]
            dba_ref[tile, :] += dba
            d_conv = [acc + d for acc, d in zip(d_conv, (dcq, dck, dcv))]
            d_al, d_dt = d_al + dal, d_dt + ddt
        dcq_ref[...], dck_ref[...], dcv_ref[...] = d_conv
        dal_ref[...] += d_al
        ddt_ref[...] += d_dt

    w = n_heads * HEAD_DIM
    return _pc(body, name="dn_pre_bwd", grid=(n_heads,), in_specs=in_specs + [hb(0)] * 5,
               out_specs=[hb(0)] * 3 + [whole((tp, LANES)), cw(0), cw(0), cw(0), whole((1, LANES)), whole((1, LANES))],
               out_shape=[jax.ShapeDtypeStruct((tp, w), bf16)] * 3 + [jax.ShapeDtypeStruct((tp, LANES), f32)]
               + [jax.ShapeDtypeStruct((CONV_WIDTH, w), f32)] * 3 + [jax.ShapeDtypeStruct((1, LANES), f32)] * 2,
               compiler_params=_params("arbitrary"))(proj, proj, proj, ba, conv_w, conv_w, conv_w, a_log, dt_bias, *cots)


def _super_rows(tp):
    n = tp // CHUNK
    return CHUNK * max(j for j in (4, 3, 2, 1) if n % j == 0)


def _heads_per_step(n_heads):
    return max(j for j in (4, 2, 1) if n_heads % j == 0)


def dn_intra_fwd(q, k, v, beta_b, g_b, n_heads):
    tp = q.shape[0]
    rows = _super_rows(tp)
    ns = tp // rows

    hps = _heads_per_step(n_heads)

    def body(q_ref, k_ref, v_ref, b_ref, g_ref, u_ref, w_ref, qd_ref, kd_ref, qk_ref, gl_ref, t_ref):
        lanes = [slice(i * HEAD_DIM, (i + 1) * HEAD_DIM) for i in range(hps)]
        gates = [gate_fn(g_ref[:, sl]) for sl in lanes]
        tmats = tri_inv([lmat_fn(k_ref[:, sl], b_ref[:, sl], gt[0]) for sl, gt in zip(lanes, gates)])
        for i, (sl, (decay, eg, kfac, gl), tmat) in enumerate(zip(lanes, gates, tmats)):
            u_c, w_c, q_dec, k_dec, qk_c = intra_fn(tmat, q_ref[:, sl], k_ref[:, sl], v_ref[:, sl], b_ref[:, sl], decay, eg, kfac)
            u_ref[:, sl] = u_c
            w_ref[:, sl] = w_c
            qd_ref[:, sl] = q_dec
            kd_ref[:, sl] = k_dec
            qk_ref[:, sl] = qk_c
            gl_ref[:, sl] = gl
            t_ref[i, 0] = tmat

    blk = pl.BlockSpec((rows, hps * HEAD_DIM), lambda h, s: (s, h))
    return _pc(body, name="dn_intra_fwd", grid=(n_heads // hps, ns), in_specs=[blk] * 5,
               out_specs=[blk] * 6 + [pl.BlockSpec((hps, 1, rows, rows), lambda h, s: (h, s, 0, 0))],
               out_shape=[jax.ShapeDtypeStruct(q.shape, f32)] * 6 + [jax.ShapeDtypeStruct((n_heads, ns, rows, rows), f32)],
               compiler_params=_params("parallel", "parallel"))(q, k, v, beta_b, g_b)


def dn_intra_bwd(q, k, v, beta_b, g_b, tmats, cots, n_heads):
    tp = q.shape[0]
    rows = _super_rows(tp)
    ns = tp // rows

    hps = _heads_per_step(n_heads)

    def body(q_ref, k_ref, v_ref, b_ref, g_ref, t_ref, du_ref, dw_ref, dqd_ref, dkd_ref, dqk_ref, dgl_ref,
             dq_ref, dk_ref, dv_ref, db_ref, dg_ref):
        lanes = [slice(i * HEAD_DIM, (i + 1) * HEAD_DIM) for i in range(hps)]
        tmats = [t_ref[i, 0] for i in range(hps)]
        gates = [jax.vjp(gate_fn, g_ref[:, sl]) for sl in lanes]
        intra = [jax.vjp(intra_fn, tmat, q_ref[:, sl], k_ref[:, sl], v_ref[:, sl], b_ref[:, sl], gt[0][0], gt[0][1], gt[0][2])[1](
            (du_ref[:, sl], dw_ref[:, sl], dqd_ref[:, sl], dkd_ref[:, sl], dqk_ref[:, sl]))
            for sl, tmat, gt in zip(lanes, tmats, gates)]
        tts = [tmat.T for tmat in tmats]
        dls = [mm_nn(tt, res[0]) for tt, res in zip(tts, intra)]
        dls = [-mm_nn(dl, tt) for dl, tt in zip(dls, tts)]
        for sl, gt, res, dl in zip(lanes, gates, intra, dls):
            _, dq, dk, dv, db, ddecay, deg, dkfac = res
            dk2, db2, ddecay2 = jax.vjp(lmat_fn, k_ref[:, sl], b_ref[:, sl], gt[0][0])[1](dl)
            (dg,) = gt[1]((ddecay + ddecay2, deg, dkfac, dgl_ref[:, sl]))
            dq_ref[:, sl] = dq
            dk_ref[:, sl] = dk + dk2
            dv_ref[:, sl] = dv
            db_ref[:, sl] = db + db2
            dg_ref[:, sl] = dg

    blk = pl.BlockSpec((rows, hps * HEAD_DIM), lambda h, s: (s, h))
    return _pc(body, name="dn_intra_bwd", grid=(n_heads // hps, ns),
               in_specs=[blk] * 5 + [pl.BlockSpec((hps, 1, rows, rows), lambda h, s: (h, s, 0, 0))] + [blk] * 6,
               out_specs=[blk] * 5, out_shape=[jax.ShapeDtypeStruct(q.shape, f32)] * 5,
               compiler_params=_params("parallel", "parallel"))(q, k, v, beta_b, g_b, tmats, *cots)


def dn_seq_fwd(inter, proj, dn_norm_w, n_heads, zd_off):
    tp, width = inter[0].shape
    n_chunks = tp // CHUNK

    def body(u_ref, w_ref, qd_ref, kd_ref, qk_ref, gl_ref, z_ref, nw_ref, y_ref, s_ref, state):
        @pl.when(pl.program_id(0) == 0)
        def _():
            state[...] = jnp.zeros_like(state)

        lanes = [slice(h * HEAD_DIM, (h + 1) * HEAD_DIM) for h in range(n_heads)]
        sts = [state[h] for h in range(n_heads)]
        for h, st in enumerate(sts):
            s_ref[0, h] = st
        v_new = [u_ref[:, sl] - mm_nn(w_ref[:, sl], st) for sl, st in zip(lanes, sts)]
        outs = [mm_nn(qd_ref[:, sl], st) + mm_nn(qk_ref[:, sl][:, :CHUNK], vn) for sl, st, vn in zip(lanes, sts, v_new)]
        for h, (sl, st, vn) in enumerate(zip(lanes, sts, v_new)):
            state[h] = st * gl_ref[0:1, sl] + mm_tn(kd_ref[:, sl], vn)
        for sl, o in zip(lanes, outs):
            y_ref[:, sl] = gated_norm(o, nw_ref[...], z_ref[:, sl]).astype(bf16)

    blk = pl.BlockSpec((CHUNK, width), lambda n: (n, 0))
    return _pc(body, name="dn_seq_fwd", grid=(n_chunks,),
               in_specs=[blk] * 6 + [pl.BlockSpec((CHUNK, width), lambda n: (n, zd_off)), pl.BlockSpec((1, HEAD_DIM), lambda n: (0, 0))],
               out_specs=[blk, pl.BlockSpec((1, n_heads, HEAD_DIM, HEAD_DIM), lambda n: (n, 0, 0, 0))],
               out_shape=[jax.ShapeDtypeStruct((tp, width), bf16), jax.ShapeDtypeStruct((n_chunks, n_heads, HEAD_DIM, HEAD_DIM), f32)],
               scratch_shapes=[pltpu.VMEM((n_heads, HEAD_DIM, HEAD_DIM), f32)],
               compiler_params=_params("arbitrary"))(*inter, proj, dn_norm_w)


def dn_seq_bwd(inter, proj, dn_norm_w, states, dy, n_heads, zd_off):
    tp, width = inter[0].shape
    n_chunks = tp // CHUNK
    last = n_chunks - 1

    def body(u_ref, w_ref, qd_ref, kd_ref, qk_ref, gl_ref, z_ref, nw_ref, s_ref, dy_ref,
             du_ref, dw_ref, dqd_ref, dkd_ref, dqk_ref, dgl_ref, dz_ref, dnw_ref, dstate):
        @pl.when(pl.program_id(0) == 0)
        def _():
            dstate[...] = jnp.zeros_like(dstate)
            dnw_ref[...] = jnp.zeros_like(dnw_ref)

        lanes = [slice(h * HEAD_DIM, (h + 1) * HEAD_DIM) for h in range(n_heads)]
        sts = [s_ref[0, h] for h in range(n_heads)]
        dsts = [dstate[h] for h in range(n_heads)]
        v_new = [u_ref[:, sl] - mm_nn(w_ref[:, sl], st) for sl, st in zip(lanes, sts)]
        outs = [mm_nn(qd_ref[:, sl], st) + mm_nn(qk_ref[:, sl][:, :CHUNK], vn) for sl, st, vn in zip(lanes, sts, v_new)]
        dnw = jnp.zeros((1, HEAD_DIM), f32)
        d_outs = []
        for sl, o in zip(lanes, outs):
            do, dn, dz = jax.vjp(gated_norm, o, nw_ref[...], z_ref[:, sl])[1](dy_ref[:, sl])
            dz_ref[:, sl] = dz.astype(bf16)
            dnw = dnw + dn
            d_outs.append(do)
        dnw_ref[...] += dnw
        d_vn = [mm_tn(qk_ref[:, sl][:, :CHUNK], do) + mm_nn(kd_ref[:, sl], ds) for sl, do, ds in zip(lanes, d_outs, dsts)]
        zeros = jnp.zeros((HEAD_DIM - CHUNK, HEAD_DIM), f32)
        rows = lax.broadcasted_iota(jnp.int32, (CHUNK, HEAD_DIM), 0)
        for h, (sl, st, vn, do, ds, dvn) in enumerate(zip(lanes, sts, v_new, d_outs, dsts, d_vn)):
            du_ref[:, sl] = dvn
            dw_ref[:, sl] = -mm_nt(dvn, st)
            dqd_ref[:, sl] = mm_nt(do, st)
            dkd_ref[:, sl] = mm_nt(vn, ds)
            dqk_ref[:, sl] = mm_nt(do, jnp.concatenate([vn, zeros], axis=0))
            dgl_ref[:, sl] = jnp.where(rows == 0, jnp.sum(st * ds, axis=0, keepdims=True), 0.0)
            dstate[h] = ds * gl_ref[0:1, sl] + mm_tn(qd_ref[:, sl], do) - mm_tn(w_ref[:, sl], dvn)

    blk = pl.BlockSpec((CHUNK, width), lambda n: (last - n, 0))
    return _pc(body, name="dn_seq_bwd", grid=(n_chunks,),
               in_specs=[blk] * 6 + [pl.BlockSpec((CHUNK, width), lambda n: (last - n, zd_off)), pl.BlockSpec((1, HEAD_DIM), lambda n: (0, 0)),
                         pl.BlockSpec((1, n_heads, HEAD_DIM, HEAD_DIM), lambda n: (last - n, 0, 0, 0)), blk],
               out_specs=[blk] * 7 + [pl.BlockSpec((1, HEAD_DIM), lambda n: (0, 0))],
               out_shape=[jax.ShapeDtypeStruct((tp, width), f32)] * 6 + [jax.ShapeDtypeStruct((tp, width), bf16),
                                                                          jax.ShapeDtypeStruct((1, HEAD_DIM), f32)],
               scratch_shapes=[pltpu.VMEM((n_heads, HEAD_DIM, HEAD_DIM), f32)],
               compiler_params=_params("arbitrary"))(*inter, proj, dn_norm_w, states, dy)


def loss_stage(out, final_w, target):
    tp, d = out.shape
    n_tiles = tp // CHUNK

    def body(o_ref, w_ref, t_ref, loss_ref, do_ref, dob_ref, dw_ref):
        i = pl.program_id(0)

        @pl.when(i == 0)
        def _():
            loss_ref[...] = jnp.zeros_like(loss_ref)
            dw_ref[...] = jnp.zeros_like(dw_ref)

        scored = (i > 0).astype(f32)
        val, (do, dw) = jax.value_and_grad(lambda o, w: scored * loss_fn(o, w, t_ref[...]), argnums=(0, 1))(o_ref[...], w_ref[...])
        loss_ref[...] += jnp.full(loss_ref.shape, val, f32)
        do_ref[...] = do
        dob_ref[...] = do.astype(bf16)
        dw_ref[...] += dw

    row = pl.BlockSpec((CHUNK, d), lambda i: (i, 0))
    vec = pl.BlockSpec((1, d), lambda i: (0, 0))
    return _pc(body, name="loss_stage", grid=(n_tiles,),
               in_specs=[row, vec, pl.BlockSpec((CHUNK, d), lambda i: (jnp.maximum(i - 1, 0), 0))],
               out_specs=[pl.BlockSpec((1, LANES), lambda i: (0, 0)), row, row, vec],
               out_shape=[jax.ShapeDtypeStruct((1, LANES), f32), jax.ShapeDtypeStruct((tp, d), f32),
                          jax.ShapeDtypeStruct((tp, d), bf16), jax.ShapeDtypeStruct((1, d), f32)],
               compiler_params=_params("arbitrary"))(out, final_w, target)


def _adam_update(w, g, m, v):
    nm = ADAM_B1 * m + (1.0 - ADAM_B1) * g
    nv = ADAM_B2 * v + (1.0 - ADAM_B2) * jnp.square(g)
    m_hat = nm / (1.0 - ADAM_B1 ** ADAM_STEP)
    v_hat = nv / (1.0 - ADAM_B2 ** ADAM_STEP)
    return -ADAM_LR * (m_hat / (jnp.sqrt(v_hat) + ADAM_EPS) + ADAM_WD * w), nm, nv


def adamw(name, w, g, m, v):
    rows, cols = w.shape
    t = _row_tile(rows, cols, 7)

    def body(w_ref, g_ref, m_ref, v_ref, d_ref, nm_ref, nv_ref):
        d_ref[...], nm_ref[...], nv_ref[...] = _adam_update(w_ref[...], g_ref[...], m_ref[...], v_ref[...])

    blk = pl.BlockSpec((t, cols), lambda i: (i, 0))
    return _pc(body, name=name, grid=(rows // t,), in_specs=[blk] * 4, out_specs=[blk] * 3,
               out_shape=[jax.ShapeDtypeStruct(w.shape, f32)] * 3, compiler_params=_params("parallel"))(w, g, m, v)


def adamw_joined(name, w, g_mine, g_theirs, m, v, half, axis):
    rows, cols = w.shape
    hr, hc = g_mine.shape
    tr, tc = _tile(hr, hc, 9)
    nr, nc = hr // tr, hc // tc

    def body(half_ref, w_ref, gm_ref, gt_ref, m_ref, v_ref, g_ref, d_ref, nm_ref, nv_ref):
        pos = pl.program_id(axis) // (nr if axis == 0 else nc)
        g = jnp.where(pos == half_ref[0], gm_ref[...], gt_ref[...])
        delta, nm, nv = _adam_update(w_ref[...], g, m_ref[...], v_ref[...])
        g_ref[...] = g
        d_ref[...] = delta
        nm_ref[...] = nm
        nv_ref[...] = nv

    whole = pl.BlockSpec((tr, tc), lambda i, j, hf: (i, j))
    part = pl.BlockSpec((tr, tc), lambda i, j, hf: (i % nr, j % nc))
    grid_spec = pltpu.PrefetchScalarGridSpec(num_scalar_prefetch=1, grid=(rows // tr, cols // tc),
                                             in_specs=[whole, part, part, whole, whole], out_specs=[whole] * 4)
    return _pc(body, name=name, grid_spec=grid_spec, out_shape=[jax.ShapeDtypeStruct(w.shape, f32)] * 4,
               compiler_params=_params("parallel", "parallel"))(half, w, g_mine, g_theirs, m, v)


def add_halves(name, g, recv, half, kind):
    s, r, c = recv.shape
    tr, tc = _tile(r, c, 3)
    nc = c // tc

    def body(half_ref, g_ref, r_ref, o_ref):
        o_ref[...] = (g_ref[...].reshape(r_ref.shape).astype(f32) + r_ref[...].astype(f32)).astype(bf16)

    if kind == "lead":
        g_spec = pl.BlockSpec((1, 1, tr, tc), lambda i, j, k, hf: (hf[0], i, j, k))
    else:
        g_spec = pl.BlockSpec((1, tr, tc), lambda i, j, k, hf: (i, j, hf[0] * nc + k))
    blk = pl.BlockSpec((1, tr, tc), lambda i, j, k, hf: (i, j, k))
    grid_spec = pltpu.PrefetchScalarGridSpec(num_scalar_prefetch=1, grid=(s, r // tr, nc), in_specs=[g_spec, blk], out_specs=blk)
    return _pc(body, name=name, grid_spec=grid_spec, out_shape=jax.ShapeDtypeStruct((s, r, c), bf16),
               compiler_params=_params("parallel", "parallel", "parallel"))(half, g, recv)


def add_first(name, parts, got, chips):
    _, r, c = parts.shape
    tr, tc = _tile(r, c, 6)

    def body(chips_ref, mine_ref, theirs_ref, got_ref, keep_ref, pass_ref):
        keep_ref[...] = mine_ref[0].astype(f32) + got_ref[0].astype(f32)
        pass_ref[...] = (theirs_ref[0].astype(f32) + got_ref[1].astype(f32)).astype(bf16)

    blk = pl.BlockSpec((tr, tc), lambda i, j, ch: (i, j))
    grid_spec = pltpu.PrefetchScalarGridSpec(
        num_scalar_prefetch=1, grid=(r // tr, c // tc),
        in_specs=[pl.BlockSpec((1, tr, tc), lambda i, j, ch: (ch[0], i, j)), pl.BlockSpec((1, tr, tc), lambda i, j, ch: (ch[1], i, j)),
                  pl.BlockSpec((2, tr, tc), lambda i, j, ch: (0, i, j))],
        out_specs=[blk, blk])
    return _pc(body, name=name, grid_spec=grid_spec, out_shape=[jax.ShapeDtypeStruct((r, c), f32), jax.ShapeDtypeStruct((r, c), bf16)],
               compiler_params=_params("parallel", "parallel"))(chips, parts, parts, got)


def add_second(name, kept, got):
    r, c = kept.shape
    tr, tc = _tile(r, c, 3)

    def body(k_ref, g_ref, o_ref):
        o_ref[...] = k_ref[...] + g_ref[...].astype(f32)

    blk = pl.BlockSpec((tr, tc), lambda i, j: (i, j))
    return _pc(body, name=name, grid=(r // tr, c // tc), in_specs=[blk, blk], out_specs=blk,
               out_shape=jax.ShapeDtypeStruct((r, c), f32), compiler_params=_params("parallel", "parallel"))(kept, got)


def sum_leading(name, x, out_dtype=f32):
    s, r, c = x.shape
    tr, tc = _tile(r, c, s + 1)

    def body(x_ref, o_ref):
        acc = x_ref[0].astype(f32)
        for i in range(1, s):
            acc = acc + x_ref[i].astype(f32)
        o_ref[...] = acc.astype(out_dtype)

    return _pc(body, name=name, grid=(r // tr, c // tc), in_specs=[pl.BlockSpec((s, tr, tc), lambda i, j: (0, i, j))],
               out_specs=pl.BlockSpec((tr, tc), lambda i, j: (i, j)), out_shape=jax.ShapeDtypeStruct((r, c), out_dtype),
               compiler_params=_params("parallel", "parallel"))(x)


def _place():
    x, y, c = lax.axis_index("x"), lax.axis_index("y"), lax.axis_index("c")
    return x, y, c


def _route(x, y, c):
    return (x ^ (1 - c), y ^ c), (x ^ c, y ^ (1 - c)), (1 - x, 1 - y)


def _half_view(ref, half, kind, lead=()):
    if kind == "lead":
        return ref.at[(*lead, half)]
    width = ref.shape[-1] // 2
    return ref.at[(*lead, *([slice(None)] * (len(ref.shape) - len(lead) - 1)), pl.ds(half * width, width))]


def _gather_plan(ins, outs, sems, kinds):
    n = len(ins)
    send1, recv1, send2, recv2 = sems
    x, y, c = _place()
    chip = 2 * x + y
    sibling = (x, y, 1 - c)
    near, far, diag = _route(x, y, c)
    near_id, far_id, diag_id = [2 * cx + cy for cx, cy in (near, far, diag)]

    def remote(src, dst, s_sem, r_sem, to):
        return pltpu.make_async_remote_copy(src_ref=src, dst_ref=dst, send_sem=s_sem, recv_sem=r_sem, device_id=to, device_id_type=MESH)

    def slab(a, chip_id, half):
        return _half_view(outs[a], half, kinds[a], (chip_id,))

    def landed(a, chip_id, sem, frm):
        return remote(slab(a, chip_id, c), slab(a, chip_id, c), send1.at[a, sem], recv1.at[a, sem], (*frm, c))

    def onward(a, chip_id, sem):
        return remote(slab(a, chip_id, c), slab(a, chip_id, c), send2.at[a, sem], recv2.at[a, sem], sibling)

    own = [remote(_half_view(ins[a], c, kinds[a]), slab(a, chip, c), send1.at[a, j], recv1.at[a, j], (*to, c))
           for a in range(n) for j, to in enumerate((near, far))]
    relay = [remote(slab(a, near_id, c), slab(a, near_id, c), send1.at[a, 2], recv1.at[a, 2], (*far, c)) for a in range(n)]
    to_sibling = [[onward(a, cid, sem) for sem, cid in enumerate((near_id, far_id, diag_id))] for a in range(n)]
    from_sibling = [remote(slab(a, cid, 1 - c), slab(a, cid, 1 - c), send2.at[a, sem], recv2.at[a, sem], sibling)
                    for a in range(n) for sem, cid in enumerate((far_id, near_id, diag_id))]

    def start_own():
        for cp in own:
            cp.start()

    def pass_near():
        for a in range(n):
            landed(a, near_id, 0, near).wait_recv()
            relay[a].start()
            to_sibling[a][0].start()

    def pass_far():
        for sem, cid in ((1, far_id), (2, diag_id)):
            for a in range(n):
                landed(a, cid, sem, far).wait_recv()
                to_sibling[a][sem].start()

    def finish():
        for cp in from_sibling:
            cp.wait_recv()
        for cp in own + relay + [cp for row in to_sibling for cp in row]:
            cp.wait_send()

    return [start_own, pass_near, pass_far, finish]


def _gather_shapes(shards):
    return ([jax.ShapeDtypeStruct((4,) + s.shape, s.dtype) for s in shards], [pltpu.SemaphoreType.DMA((len(shards), 3))] * 4)


def gather_weights(shards, kinds):
    n = len(shards)

    def body(*refs):
        for emit in _gather_plan(refs[:n], refs[n:2 * n], refs[2 * n:], kinds):
            emit()

    out_shapes, sems = _gather_shapes(shards)
    return _pc(body, name="gather_weights", in_specs=[_ANY] * n, out_specs=[_ANY] * n, out_shape=out_shapes, scratch_shapes=sems)(*shards)


def hosted_gather(shards, kinds):
    out_shapes, sems = _gather_shapes(shards)
    return Hosted(shards, out_shapes, sems, lambda i, o, s: list(zip((0.0, 0.45, 0.8, 1.0), _gather_plan(i, o, s, kinds))))


def swap_with_sibling(name, sends):
    n = len(sends)

    def body(*refs):
        ins, outs = refs[:n], refs[n:2 * n]
        send, recv = refs[2 * n:]
        x, y, c = _place()
        cps = [pltpu.make_async_remote_copy(src_ref=ins[a], dst_ref=outs[a], send_sem=send.at[a], recv_sem=recv.at[a],
                                            device_id=(x, y, 1 - c), device_id_type=MESH) for a in range(n)]
        for cp in cps:
            cp.start()
        for cp in cps:
            cp.wait()

    return _pc(body, name=name, in_specs=[_ANY] * n, out_specs=[_ANY] * n,
               out_shape=[jax.ShapeDtypeStruct(s.shape, s.dtype) for s in sends],
               scratch_shapes=[pltpu.SemaphoreType.DMA((n,))] * 2)(*sends)


def send_grad_halves(name, grads, kinds):
    n = len(grads)

    def body(*refs):
        ins, outs = refs[:n], refs[n:2 * n]
        send, recv = refs[2 * n:]
        x, y, c = _place()
        cps = [pltpu.make_async_remote_copy(src_ref=_half_view(ins[a], 1 - c, kinds[a]), dst_ref=outs[a], send_sem=send.at[a],
                                            recv_sem=recv.at[a], device_id=(x, y, 1 - c), device_id_type=MESH) for a in range(n)]
        for cp in cps:
            cp.start()
        for cp in cps:
            cp.wait()

    shape = lambda g, kind: g.shape[1:] if kind == "lead" else g.shape[:-1] + (g.shape[-1] // 2,)
    return _pc(body, name=name, in_specs=[_ANY] * n, out_specs=[_ANY] * n,
               out_shape=[jax.ShapeDtypeStruct(shape(g, k), g.dtype) for g, k in zip(grads, kinds)],
               scratch_shapes=[pltpu.SemaphoreType.DMA((n,))] * 2)(*grads)


def _scatter_first_plan(ins, outs, sems):
    n = len(ins)
    send, recv = sems
    x, y, c = _place()
    near, _, diag = _route(x, y, c)
    cps = [pltpu.make_async_remote_copy(src_ref=ins[a].at[2 * cx + cy], dst_ref=outs[a].at[j], send_sem=send.at[a, j],
                                        recv_sem=recv.at[a, j], device_id=(*near, c), device_id_type=MESH)
           for a in range(n) for j, (cx, cy) in enumerate((near, diag))]

    def start():
        for cp in cps:
            cp.start()

    def wait():
        for cp in cps:
            cp.wait()

    return [start, wait]


def _scatter_first_shapes(parts):
    return ([jax.ShapeDtypeStruct((2,) + p.shape[1:], p.dtype) for p in parts], [pltpu.SemaphoreType.DMA((len(parts), 2))] * 2)


def hosted_scatter_first(parts):
    out_shapes, sems = _scatter_first_shapes(parts)
    return Hosted(parts, out_shapes, sems, lambda i, o, s: list(zip((0.0, 1.0), _scatter_first_plan(i, o, s))))


def _far_swap_plan(ins, outs, sems):
    send, recv = sems
    x, y, c = _place()
    _, far, _ = _route(x, y, c)
    cps = [pltpu.make_async_remote_copy(src_ref=ins[a], dst_ref=outs[a], send_sem=send.at[a], recv_sem=recv.at[a],
                                        device_id=(*far, c), device_id_type=MESH) for a in range(len(ins))]

    def start():
        for cp in cps:
            cp.start()

    def wait():
        for cp in cps:
            cp.wait()

    return [start, wait]


def hosted_scatter_second(parts):
    return Hosted(parts, [jax.ShapeDtypeStruct(p.shape, p.dtype) for p in parts], [pltpu.SemaphoreType.DMA((len(parts),))] * 2,
                  lambda i, o, s: list(zip((0.0, 1.0), _far_swap_plan(i, o, s))))


def gather_small(pack):
    def body(pack_ref, packs_ref, send, recv):
        x, y, c = _place()
        me = 4 * x + 2 * y + c
        flips = [(fx, fy, fc) for fx in (0, 1) for fy in (0, 1) for fc in (0, 1)][1:]
        peers = [(x ^ fx, y ^ fy, c ^ fc) for fx, fy, fc in flips]
        cps = [pltpu.make_async_remote_copy(src_ref=pack_ref, dst_ref=packs_ref.at[me], send_sem=send.at[j], recv_sem=recv.at[j],
                                            device_id=p, device_id_type=MESH) for j, p in enumerate(peers)]
        for cp in cps:
            cp.start()
        for j, (px, py, pc) in enumerate(peers):
            slab = packs_ref.at[4 * px + 2 * py + pc]
            pltpu.make_async_remote_copy(src_ref=slab, dst_ref=slab, send_sem=send.at[j], recv_sem=recv.at[j],
                                         device_id=(px, py, pc), device_id_type=MESH).wait_recv()
        for cp in cps:
            cp.wait_send()

    return _pc(body, name="gather_small", in_specs=[_ANY], out_specs=_ANY, out_shape=jax.ShapeDtypeStruct((8,) + pack.shape, pack.dtype),
               scratch_shapes=[pltpu.SemaphoreType.DMA((7,))] * 2)(pack)


def _pack(arrays, width):
    flat = jnp.concatenate([a.reshape(-1).astype(f32) for a in arrays])
    return jnp.pad(flat, (0, (-flat.shape[0]) % (8 * width))).reshape(-1, width)


def _unpack(pack, like, width):
    flat, out, at = pack.reshape(-1), [], 0
    for a in like:
        size = 1
        for s in a.shape:
            size *= s
        out.append(flat[at:at + size].reshape(a.shape))
        at += size
    return out


def _halves(a2d):
    r, c = a2d.shape
    return a2d.reshape(2, r // 2, c)


def local_step(x, meta, norm_w, w_main, w_ba, conv_w, a_log, dt_bias, pool_mix, pool_scale, dn_norm_w,
               late_weights, final_w, target, grad_hooks=None):
    seq, d = x.shape
    pw = pool_scale.shape[1]
    dw = conv_w.shape[1] // 3
    n_heads = dw // HEAD_DIM
    tp = FRONT_PAD + N_META + seq
    hp = jnp.concatenate([jnp.zeros((FRONT_PAD, d), f32), meta, x], axis=0)
    tm_big = tp // 2 if tp % 32 == 0 else tp
    tm_norm = max(t for t in range(16, min(tp, 352) + 1, 16) if tp % t == 0)
    tile = min(512, d)
    off_q = 2 * pw
    off_zd = off_q + 3 * dw
    off_gp = off_zd + dw
    off_gd = off_gp + d
    a_log128 = jnp.pad(a_log, ((0, 0), (0, LANES - n_heads)))
    dt128 = jnp.pad(dt_bias, ((0, 0), (0, LANES - n_heads)))

    xn = norm_fwd(hp, norm_w, tm_norm)
    if isinstance(late_weights[0], Hosted):
        proj, *fetched = matmul("proj", xn, w_main, tb=True, tm=tp, tn=tile, hosted=late_weights[0])
        w_pool_out, w_dn_out, w_o = late_weights[1](fetched)
    else:
        proj = matmul("proj", xn, w_main, tb=True, tm=tp, tn=tile)
        w_pool_out, w_dn_out, w_o = late_weights
    ba = matmul("proj_ba", xn, w_ba, tb=True, tm=tp, tn=LANES)
    y_pool = pool_fwd(proj, pool_mix, pool_scale, pw)
    q, k, v, beta_b, g_b = dn_pre_fwd(proj, ba, conv_w, a_log128, dt128, n_heads, off_q // HEAD_DIM)
    assert off_zd % dw == 0
    *inter, tmats = dn_intra_fwd(q, k, v, beta_b, g_b, n_heads)
    y_dn, states = dn_seq_fwd(inter, proj, dn_norm_w, n_heads, off_zd // dw)
    a_mat = matmul("pool_out", y_pool, w_pool_out, tm=tp, tn=tile)

    def merge(acc, a_t, gp_t, gd_t):
        return acc, sigmoid(gp_t) * a_t + sigmoid(gd_t) * acc

    b_mat, merged = matmul("dn_out_merge", y_dn, w_dn_out, tm=tm_big, tn=tile, extras=[(a_mat, 0), (proj, off_gp), (proj, off_gd)],
                           epi=merge, out_dtypes=(f32, bf16))
    out = matmul("out_proj", merged, w_o, tm=tm_big, tn=tile, extras=[(hp, 0)], epi=lambda acc, h_t: (acc + h_t,))
    loss, dout, dout_b, dfinal_w = loss_stage(out, final_w, target)

    def unmerge(dm, a_t, b_t, gp_t, gd_t):
        sp, sd = sigmoid(gp_t), sigmoid(gd_t)
        return dm * sp, dm * sd, dm * a_t * sp * (1.0 - sp), dm * b_t * sd * (1.0 - sd)

    d_a, d_b, d_gp, d_gd = matmul("d_merged", dout_b, w_o, tb=True, tm=tm_big, tn=tile,
                                  extras=[(a_mat, 0), (b_mat, 0), (proj, off_gp), (proj, off_gd)], epi=unmerge,
                                  out_dtypes=(bf16,) * 4)
    g_w_o = matmul("g_w_o", merged, dout_b, ta=True, tm=tile, tn=tile, out_dtypes=(bf16,))
    d_y_pool = matmul("d_y_pool", d_a, w_pool_out, tb=True, tm=tp, tn=tile)
    g_w_pool_out = matmul("g_w_pool_out", y_pool, d_a, ta=True, tm=tile, tn=tile, out_dtypes=(bf16,))
    d_y_dn = matmul("d_y_dn", d_b, w_dn_out, tb=True, tm=tp, tn=tile)
    g_w_dn_out = matmul("g_w_dn_out", y_dn, d_b, ta=True, tm=tile, tn=tile, out_dtypes=(bf16,))
    d_u, d_zp, g_pool_mix, g_pool_scale = pool_bwd(proj, pool_mix, pool_scale, d_y_pool, pw)
    *d_inter, d_zd, g_dn_norm_w = dn_seq_bwd(inter, proj, dn_norm_w, states, d_y_dn, n_heads, off_zd // dw)
    d_q, d_k, d_v, d_beta, d_g = dn_intra_bwd(q, k, v, beta_b, g_b, tmats, d_inter, n_heads)
    d_qr, d_kr, d_vr, d_ba, g_cq, g_ck, g_cv, g_a_log, g_dt = dn_pre_bwd(
        proj, ba, conv_w, a_log128, dt128, (d_q, d_k, d_v, d_beta, d_g), n_heads, off_q // HEAD_DIM)
    d_proj = jnp.concatenate([d_u, d_zp, d_qr, d_kr, d_vr, d_zd, d_gp, d_gd], axis=1)
    d_ba_b = cast_bf16("cast_d_ba", d_ba)
    early = grad_hooks[0](g_w_pool_out, g_w_dn_out, g_w_o, g_pool_mix) if grad_hooks else None
    res = matmul("g_w_main", d_proj, xn, ta=True, tm=tile, tn=d, out_dtypes=(bf16,), hosted=early)
    g_w_main, early_landed = (res[0], list(res[1:])) if early else (res, [])
    g_w_ba = matmul("g_w_ba", d_ba_b, xn, ta=True, tm=LANES, tn=tile, out_dtypes=(bf16,))
    hosted = grad_hooks[1](g_w_main, g_w_ba, early_landed) if grad_hooks else None
    dxn_ba = matmul("dxn_ba", d_ba_b, w_ba, tm=tp, tn=tile)
    n_cols = d_proj.shape[1]
    tk_dxn = max(t for t in range(LANES, min(2048, n_cols) + 1, LANES) if n_cols % t == 0)
    res = matmul("dxn", d_proj, w_main, tm=tp, tn=tile, tk=tk_dxn, extras=[(dxn_ba, 0)], epi=lambda acc, e: (acc + e,), hosted=hosted)
    dxn, landed = (res[0], list(res[1:])) if hosted else (res, [])
    dh, g_norm_w, last_landed = norm_bwd(hp, norm_w, dxn, dout, tm_norm, hosted=grad_hooks[2](landed) if grad_hooks else None)
    g_conv = jnp.concatenate([g_cq, g_ck, g_cv], axis=1)
    return (loss, dh, g_norm_w, g_w_main, g_w_ba, g_conv, g_a_log[:, :n_heads], g_dt[:, :n_heads], g_pool_mix, g_pool_scale,
            g_dn_norm_w, g_w_pool_out, g_w_dn_out, g_w_o, dfinal_w, last_landed)


def kernel(x, meta_tokens, norm_w, w_in, conv_w, A_log, dt_bias, pool_mix, pool_scale, dn_norm_w, w_pool_out, w_dn_out, w_o, final_norm_w, loss_target, m_meta_tokens, m_norm_w, m_w_in, m_conv_w, m_A_log, m_dt_bias, m_pool_mix, m_pool_scale, m_dn_norm_w, m_w_pool_out, m_w_dn_out, m_w_o, m_final_norm_w, v_meta_tokens, v_norm_w, v_w_in, v_conv_w, v_A_log, v_dt_bias, v_pool_mix, v_pool_scale, v_dn_norm_w, v_w_pool_out, v_w_dn_out, v_w_o, v_final_norm_w):
    d = x.shape[-1]
    pw = pool_scale.shape[-1]
    dw = w_dn_out.shape[1] * 4
    n_heads = dw // HEAD_DIM
    gdim = pw // POOL_GROUPS
    chip = 2 * lax.axis_index("x") + lax.axis_index("y")
    core = lax.axis_index("c")

    half = core.astype(jnp.int32).reshape(1)
    me = 4 * lax.axis_index("x") + 2 * lax.axis_index("y") + core

    w_in_t = w_in[0].T
    mix_s = pool_mix[0].reshape(POOL_GROUPS * (gdim // 4), gdim)
    small_s = _pack([meta_tokens, conv_w[0]], d)
    small_rows = small_s.shape[0]
    small_s = jnp.pad(small_s, ((0, (-small_rows) % 16), (0, 0)))
    sw = w_in_t.shape[0]
    n_main, n_ba = 2 * pw + 4 * dw, 2 * n_heads

    early = [cast_bf16("cast_w_in", w_in_t), _halves(cast_bf16("cast_mix", mix_s)), _halves(small_s)]
    late = [_halves(cast_bf16("cast_w_po", w_pool_out[0])), _halves(cast_bf16("cast_w_do", w_dn_out[0])),
            _halves(cast_bf16("cast_w_o", w_o[0]))]
    fill = lambda g, own: lax.dynamic_update_slice(g, own[None], (chip,) + (0,) * own.ndim)
    g_in, g_mix, g_small = [fill(g, own) for g, own in zip(gather_weights(early, ["cols", "lead", "lead"]), early)]

    def shard_rows(lo, hi):
        cut = [(max(lo, j * sw), min(hi, (j + 1) * sw), j) for j in range(4)]
        return [g_in[j, a - j * sw:b - j * sw] for a, b, j in cut if a < b]

    w_main = jnp.concatenate(shard_rows(0, n_main) + shard_rows(n_main + n_ba, 4 * sw), axis=0)
    w_ba = jnp.pad(jnp.concatenate(shard_rows(n_main, n_main + n_ba), axis=0), ((0, LANES - n_ba), (0, 0)))
    cat_cols = lambda g: jnp.concatenate([g[j].reshape(-1, g.shape[-1]) for j in range(4)], axis=1)

    def late_weights(fetched):
        g_po, g_do, g_o = [fill(g, own) for g, own in zip(fetched, late)]
        return cat_cols(g_po), g_do.reshape(-1, g_do.shape[-1]), g_o.reshape(-1, g_o.shape[-1])

    mix_full = g_mix.reshape(4, POOL_GROUPS, gdim // 4, gdim).transpose(1, 0, 2, 3).reshape(POOL_GROUPS, gdim, gdim)
    smalls = [_unpack(g_small[j].reshape(-1, d)[:small_rows], [meta_tokens, conv_w[0]], d) for j in range(4)]
    meta_full = jnp.concatenate([s[0] for s in smalls], axis=1)
    conv_full = jnp.concatenate([s[1] for s in smalls], axis=1)

    names = ["w_in", "w_po", "w_do", "w_o", "mix"]
    kinds = ["cols", "lead", "lead", "lead", "lead"]

    far_chip = 2 * (lax.axis_index("x") ^ core) + (lax.axis_index("y") ^ (1 - core))
    chips = jnp.stack([chip, far_chip]).astype(jnp.int32)
    col_parts = lambda g: g.reshape(2, g.shape[0] // 2, 4, g.shape[1] // 4).transpose(0, 2, 1, 3)
    row_parts = lambda g: g.reshape(4, 2, g.shape[0] // 8, g.shape[1]).transpose(1, 0, 2, 3)

    def pair_sums(tag, group, parts):
        group_kinds = [kinds[names.index(nm)] for nm in group]
        from_sibling = send_grad_halves("send_grad_halves_" + tag, parts, group_kinds)
        return [add_halves("add_" + nm, p, r, half, k) for nm, p, r, k in zip(group, parts, from_sibling, group_kinds)]

    later = names[1:]
    state = {}

    def later_grads(g_w_po, g_w_do, g_w_o_full, g_mix_full):
        mix_rows = POOL_GROUPS * (gdim // 4)
        mix_parts = (g_mix_full.astype(bf16).reshape(POOL_GROUPS, 4, gdim // 4, gdim).transpose(1, 0, 2, 3)
                     .reshape(4, 2, mix_rows // 2, gdim).transpose(1, 0, 2, 3))
        state["later_sums"] = pair_sums("later", later, [col_parts(g_w_po), row_parts(g_w_do), row_parts(g_w_o_full), mix_parts])
        return hosted_scatter_first(state["later_sums"])

    def input_grads(g_w_main, g_w_ba, later_from_near):
        def grad_rows(lo, hi):
            segs = [(0, n_main, g_w_main, 0), (n_main, n_main + n_ba, g_w_ba, 0), (n_main + n_ba, 4 * sw, g_w_main, n_main)]
            cut = [(max(lo, s0), min(hi, s1), s0, arr, off) for s0, s1, arr, off in segs]
            return [arr[a - s0 + off:b - s0 + off] for a, b, s0, arr, off in cut if a < b]

        kept, passed = zip(*[add_first("add1_" + nm, p, g, chips) for nm, p, g in zip(later, state["later_sums"], later_from_near)])
        state["later_kept"] = list(kept)
        in_parts = jnp.stack([jnp.concatenate(grad_rows(j * sw, (j + 1) * sw), axis=0) for j in range(4)])
        state["in_sums"] = pair_sums("input", names[:1], [in_parts])
        return hosted_scatter_first(state["in_sums"]) + hosted_scatter_second(list(passed))

    def input_second(landed):
        state["later_from_far"] = landed[1:]
        state["in_kept"], in_passed = add_first("add1_w_in", state["in_sums"][0], landed[0], chips)
        return hosted_scatter_second([in_passed])

    (loss, dh, g_norm_w, _, _, g_conv, g_a_log, g_dt, _, g_pool_scale, g_dn_norm_w, _, _, _, g_final_w,
     (in_from_far,)) = local_step(x[0], meta_full, norm_w, w_main, w_ba, conv_full, A_log, dt_bias, mix_full, pool_scale,
                                  dn_norm_w, (hosted_gather(late, ["lead"] * 3), late_weights), final_norm_w.reshape(1, d),
                                  loss_target[0], grad_hooks=(later_grads, input_grads, input_second))
    grad_x = dh[FRONT_PAD + N_META:][None]
    g_meta = dh[FRONT_PAD:FRONT_PAD + N_META]

    mine = [add_second("add2_" + nm, k_, g) for nm, k_, g in
            zip(names, [state["in_kept"]] + state["later_kept"], [in_from_far] + list(state["later_from_far"]))]
    small_like = [loss, g_norm_w, g_a_log, g_dt, g_pool_scale, g_dn_norm_w, g_final_w, g_conv, g_meta]
    pack = _pack(small_like, d)
    packs = gather_small(pack)
    theirs = swap_with_sibling("swap_grad_halves", mine)
    total = sum_leading("sum_small", lax.dynamic_update_slice(packs, pack[None], (me, 0, 0)))
    (loss_t, g_norm_w, g_a_log, g_dt, g_pool_scale, g_dn_norm_w, g_final_w, g_conv, g_meta) = _unpack(total, small_like, d)
    loss_out = loss_t[0, 0]
    g_conv_s = lax.dynamic_slice_in_dim(g_conv, chip * (g_conv.shape[1] // 4), g_conv.shape[1] // 4, axis=1)
    g_meta_s = lax.dynamic_slice_in_dim(g_meta, chip * (d // 4), d // 4, axis=1)

    weights = [meta_tokens, norm_w, w_in, conv_w, A_log, dt_bias, pool_mix, pool_scale, dn_norm_w, w_pool_out, w_dn_out, w_o, final_norm_w]
    ms = [m_meta_tokens, m_norm_w, m_w_in, m_conv_w, m_A_log, m_dt_bias, m_pool_mix, m_pool_scale, m_dn_norm_w, m_w_pool_out, m_w_dn_out, m_w_o, m_final_norm_w]
    vs = [v_meta_tokens, v_norm_w, v_w_in, v_conv_w, v_A_log, v_dt_bias, v_pool_mix, v_pool_scale, v_dn_norm_w, v_w_pool_out, v_w_dn_out, v_w_o, v_final_norm_w]
    grads = [g_meta_s, g_norm_w, None, g_conv_s[None], g_a_log, g_dt, None, g_pool_scale, g_dn_norm_w, None, None, None, g_final_w.reshape(d)]
    deltas, new_ms, new_vs = [None] * 13, [None] * 13, [None] * 13
    big = [2, 9, 10, 11, 6]
    for i, nm, g_mine, g_theirs in zip(big, names, mine, theirs):
        if nm == "w_in":
            to2d, back, axis = (lambda t: t[0].T), (lambda t: t.T[None]), 1
        else:
            to2d, back, axis = (lambda t: t.reshape(-1, t.shape[-1])), (lambda t, i=i: t.reshape(weights[i].shape)), 0
        res = adamw_joined("adamw_" + nm, to2d(weights[i]), g_mine, g_theirs, to2d(ms[i]), to2d(vs[i]), half, axis)
        grads[i], deltas[i], new_ms[i], new_vs[i] = [back(t) for t in res]
    small_idx = [i for i in range(13) if i not in big]
    packs = [_pack([arrs[i] for i in small_idx], d) for arrs in (weights, grads, ms, vs)]
    outs = adamw("adamw_small", *packs)
    like = [weights[i] for i in small_idx]
    for res, dest in zip(outs, (deltas, new_ms, new_vs)):
        for i, val in zip(small_idx, _unpack(res, like, d)):
            dest[i] = val
    return (loss_out, grad_x, *grads, *deltas, *new_ms, *new_vs)
```

```python
import functools

import jax
import jax.numpy as jnp
from jax import lax
from jax.experimental import pallas as pl
from jax.experimental.pallas import tpu as pltpu

f32 = jnp.float32
bf16 = jnp.bfloat16
MESH = pl.DeviceIdType.MESH

N_META = 16
CHUNK = 64
FRONT_PAD = (-N_META) % CHUNK
HEAD_DIM = 128
POOL_GROUPS = 4
POOL_WINDOWS = (2, 4, 8, 16)
CONV_WIDTH = 4
NORM_EPS = 1e-6
ADAM_LR, ADAM_B1, ADAM_B2, ADAM_EPS, ADAM_WD, ADAM_STEP = 0.001, 0.9, 0.999, 1e-08, 0.01, 10
LANES = 128
V7X_VMEM_BYTES = 64 * 2**20
VMEM_LIMIT = V7X_VMEM_BYTES - 8 * 2**20


def _pc(body, **kw):
    return pl.pallas_call(body, **kw)


def _params(*sem, **kw):
    return pltpu.CompilerParams(dimension_semantics=sem or None, vmem_limit_bytes=VMEM_LIMIT, **kw)


def _dg(a, b, dims):
    return lax.dot_general(a, b, (dims, ((), ())), preferred_element_type=f32)


@jax.custom_vjp
def mm_nn(a, b):
    return _dg(a.astype(bf16), b.astype(bf16), ((1,), (0,)))


@jax.custom_vjp
def mm_nt(a, b):
    return _dg(a.astype(bf16), b.astype(bf16), ((1,), (1,)))


@jax.custom_vjp
def mm_tn(a, b):
    return _dg(a.astype(bf16).T, b.astype(bf16), ((1,), (0,)))


mm_nn.defvjp(lambda a, b: (mm_nn(a, b), (a, b)), lambda r, dy: (mm_nt(dy, r[1]), mm_tn(r[0], dy)))
mm_nt.defvjp(lambda a, b: (mm_nt(a, b), (a, b)), lambda r, dy: (mm_nn(dy, r[1]), mm_tn(dy, r[0])))
mm_tn.defvjp(lambda a, b: (mm_tn(a, b), (a, b)), lambda r, dy: (mm_nt(r[1], dy), mm_nn(r[0], dy)))


def _split3(x):
    hi = x.astype(bf16)
    r1 = x - hi.astype(f32)
    mid = r1.astype(bf16)
    lo = (r1 - mid.astype(f32)).astype(bf16)
    return hi, mid, lo


@jax.custom_vjp
def mm_sel(sel, x):
    s = sel.astype(bf16)
    d = ((1,), (0,))
    hi, mid, lo = _split3(x)
    return _dg(s, hi, d) + _dg(s, mid, d) + _dg(s, lo, d)


def _mm_sel_bwd(sel, dy):
    s = sel.astype(bf16)
    d = ((0,), (0,))
    hi, mid, lo = _split3(dy)
    return jnp.zeros_like(sel), _dg(s, hi, d) + _dg(s, mid, d) + _dg(s, lo, d)


mm_sel.defvjp(lambda sel, x: (mm_sel(sel, x), sel), _mm_sel_bwd)


@jax.custom_vjp
def mm_pick(x, sel):
    s = sel.astype(bf16)
    d = ((1,), (0,))
    hi, mid, lo = _split3(x)
    return _dg(hi, s, d) + _dg(mid, s, d) + _dg(lo, s, d)


def _mm_pick_bwd(sel, dy):
    s = sel.astype(bf16)
    d = ((1,), (1,))
    hi, mid, lo = _split3(dy)
    return _dg(hi, s, d) + _dg(mid, s, d) + _dg(lo, s, d), jnp.zeros_like(sel)


mm_pick.defvjp(lambda x, sel: (mm_pick(x, sel), sel), _mm_pick_bwd)


def tri_inv(ls):
    n = ls[0].shape[0]
    eye = (lax.broadcasted_iota(jnp.int32, (n, n), 0) == lax.broadcasted_iota(jnp.int32, (n, n), 1)).astype(f32)
    ms = [-l for l in ls]
    ts = [eye + m for m in ms]
    k = 1
    while 2 * k < CHUNK:
        ms = [mm_nn(m, m) for m in ms]
        ts = [t + mm_nn(t, m) for t, m in zip(ts, ms)]
        k *= 2
    return ts


@functools.partial(jax.custom_vjp, nondiff_argnums=(1,))
def shift_rows(x, j):
    n = x.shape[0]
    rows = lax.broadcasted_iota(jnp.int32, x.shape, 0)
    if j >= 0:
        return jnp.where(rows >= j, pltpu.roll(x, j, 0), 0.0)
    return jnp.where(rows < n + j, pltpu.roll(x, n + j, 0), 0.0)


shift_rows.defvjp(lambda x, j: (shift_rows(x, j), None), lambda j, _, dy: (shift_rows(dy, -j),))


def sigmoid(x):
    return 1.0 / (1.0 + jnp.exp(-x))


def silu(x):
    return x * sigmoid(x)


def softplus(x):
    return jnp.maximum(x, 0.0) + jnp.log(1.0 + jnp.exp(-jnp.abs(x)))


def rmsnorm(x, w):
    return x * lax.rsqrt(jnp.mean(x * x, axis=-1, keepdims=True) + NORM_EPS) * w


def l2norm(x):
    return x * lax.rsqrt(jnp.sum(x * x, axis=-1, keepdims=True) + NORM_EPS)


def pool_fn(u, zp, mix, scale, group):
    rows = lax.broadcasted_iota(jnp.int32, u.shape, 0)
    sums = []
    s, w = u, 1
    while w < POOL_WINDOWS[-1]:
        s = s + shift_rows(s, w)
        w *= 2
        sums.append(s)
    total = sums[-1]
    for gi in range(POOL_GROUPS - 2, -1, -1):
        total = jnp.where(group == gi, sums[gi], total)
    window = jnp.left_shift(2, group)
    cnt = jnp.clip(rows - (FRONT_PAD - 1), 1, window).astype(f32)
    pooled = total / cnt - u
    return mm_nn(pooled, mix) * scale * silu(zp)


PRE_HALO = 8


def conv_silu(x, w):
    k = CONV_WIDTH
    y = x * w[k - 1:k, :]
    for kk in range(k - 1):
        y = y + shift_rows(x, k - 1 - kk) * w[kk:kk + 1, :]
    return silu(y[PRE_HALO:])


def _lane_pick(row, idx):
    lanes = lax.broadcasted_iota(jnp.int32, row.shape, 1)
    return jnp.sum(jnp.where(lanes == idx, row, 0.0), axis=1, keepdims=True)


def dn_pre_fn(qr, kr, vr, ba, cwq, cwk, cwv, a_log, dt_bias, head, n_heads, row0):
    q = l2norm(conv_silu(qr, cwq)) * (HEAD_DIM ** -0.5)
    k = l2norm(conv_silu(kr, cwk))
    v = conv_silu(vr, cwv)
    r = lax.broadcasted_iota(jnp.int32, (LANES, LANES), 0)
    b_b = mm_pick(ba, (r == head).astype(f32))
    a_b = mm_pick(ba, (r == head + n_heads).astype(f32))
    real = lax.broadcasted_iota(jnp.int32, ba.shape, 0) + row0 >= FRONT_PAD
    beta_b = jnp.where(real, sigmoid(b_b), 0.0)
    g_b = jnp.where(real, -jnp.exp(_lane_pick(a_log, head)) * softplus(a_b + _lane_pick(dt_bias, head)), 0.0)
    return q, k, v, beta_b, g_b


def _chunk_masks(rows):
    r = lax.broadcasted_iota(jnp.int32, (rows, rows), 0)
    c = lax.broadcasted_iota(jnp.int32, (rows, rows), 1)
    same = (r // CHUNK) == (c // CHUNK)
    return same, jnp.logical_and(same, r >= c), jnp.logical_and(same, r > c)


def _lane0(rows):
    return (lax.broadcasted_iota(jnp.int32, (rows, LANES), 1) == 0).astype(bf16)


@jax.custom_vjp
def lane0_as_row(x):
    sel = _lane0(x.shape[0])
    d = ((1,), (1,))
    hi, mid, lo = _split3(x)
    return _dg(sel, hi, d) + _dg(sel, mid, d) + _dg(sel, lo, d)


def _lane0_as_row_bwd(rows, dy):
    sel = _lane0(rows)
    d = ((0,), (0,))
    hi, mid, lo = _split3(dy)
    return (_dg(hi, sel, d) + _dg(mid, sel, d) + _dg(lo, sel, d),)


lane0_as_row.defvjp(lambda x: (lane0_as_row(x), x.shape[0]), _lane0_as_row_bwd)


def gate_fn(g_b):
    rows = g_b.shape[0]
    same, causal, _ = _chunk_masks(rows)
    gcum_b = mm_sel(causal.astype(f32), g_b)
    glast_b = mm_sel(same.astype(f32), g_b)
    g_rows = jnp.broadcast_to(gcum_b[:, :1], (rows, rows))
    decay = jnp.where(causal, jnp.exp(jnp.where(causal, g_rows - lane0_as_row(gcum_b), 0.0)), 0.0)
    return decay, jnp.exp(gcum_b), jnp.exp(glast_b - gcum_b), jnp.exp(glast_b)


def _fold_matrix(rows):
    r = lax.broadcasted_iota(jnp.int32, (rows, LANES), 0)
    c = lax.broadcasted_iota(jnp.int32, (rows, LANES), 1)
    return (r % CHUNK == c).astype(bf16)


@jax.custom_vjp
def fold_chunks(x):
    return _dg(x.astype(bf16), _fold_matrix(x.shape[0]), ((1,), (0,)))


def _fold_chunks_bwd(rows, dy):
    fold = _fold_matrix(rows)
    d = ((1,), (1,))
    hi, mid, lo = _split3(dy)
    return (_dg(hi, fold, d) + _dg(mid, fold, d) + _dg(lo, fold, d),)


fold_chunks.defvjp(lambda x: (fold_chunks(x), x.shape[0]), _fold_chunks_bwd)


def lmat_fn(k, beta_b, decay):
    _, _, strict = _chunk_masks(k.shape[0])
    return jnp.where(strict, mm_nt(k * beta_b, k) * decay, 0.0)


def intra_fn(tmat, q, k, v, beta_b, decay, eg, kfac):
    _, causal, _ = _chunk_masks(q.shape[0])
    k_beta = k * beta_b
    u_c = mm_nn(tmat, v * beta_b)
    w_c = mm_nn(tmat, k_beta * eg)
    qk = jnp.where(causal, mm_nt(q, k) * decay, 0.0)
    return u_c, w_c, q * eg, k * kfac, fold_chunks(qk)


def gated_norm(o, norm_w, zd):
    return rmsnorm(o, norm_w) * silu(zd)


def loss_fn(o, w, tgt):
    err = rmsnorm(o, w) - tgt
    return 0.5 * jnp.sum(jnp.mean(err * err, axis=-1))


_ANY = pl.BlockSpec(memory_space=pl.ANY)


class Hosted:
    def __init__(self, arrays, out_shapes, sems, stages):
        self.arrays, self.out_shapes, self.sems, self.stages = list(arrays), list(out_shapes), list(sems), stages

    def __add__(self, other):
        ni, no, ns = len(self.arrays), len(self.out_shapes), len(self.sems)
        stages = lambda i, o, s: self.stages(i[:ni], o[:no], s[:ns]) + other.stages(i[ni:], o[no:], s[ns:])
        return Hosted(self.arrays + other.arrays, self.out_shapes + other.out_shapes, self.sems + other.sems, stages)


def matmul(name, a, b, *, ta=False, tb=False, tm, tn, tk=None, extras=(), epi=None, out_dtypes=(f32,), hosted=None):
    m, k = (a.shape[1], a.shape[0]) if ta else a.shape
    n = b.shape[0] if tb else b.shape[1]
    tm, tn, tk = min(tm, m), min(tn, n), min(tk or k, k)
    assert m % tm == 0 and n % tn == 0 and k % tk == 0, (name, m, n, k, tm, tn, tk)
    nm, nn, nk = m // tm, n // tn, k // tk
    a_spec = pl.BlockSpec((tk, tm), lambda i, j, kk: (kk, i)) if ta else pl.BlockSpec((tm, tk), lambda i, j, kk: (i, kk))
    b_spec = pl.BlockSpec((tn, tk), lambda i, j, kk: (j, kk)) if tb else pl.BlockSpec((tk, tn), lambda i, j, kk: (kk, j))
    ex_specs = []
    for _, off in extras:
        assert off % tn == 0, (name, off, tn)
        ex_specs.append(pl.BlockSpec((tm, tn), functools.partial(lambda i, j, kk, o: (i, o + j), o=off // tn)))
    n_ex, n_out = len(extras), len(out_dtypes)
    dims = ((0 if ta else 1,), (1 if tb else 0,))
    n_hin = len(hosted.arrays) if hosted else 0
    n_hout = len(hosted.out_shapes) if hosted else 0
    n_sem = len(hosted.sems) if hosted else 0

    def body(a_ref, b_ref, *rest):
        ex_refs, rest = rest[:n_ex], rest[n_ex:]
        hin_refs, rest = rest[:n_hin], rest[n_hin:]
        out_refs, rest = rest[:n_out], rest[n_out:]
        hout_refs, rest = rest[:n_hout], rest[n_hout:]
        sem_refs = rest[len(rest) - n_sem:] if n_sem else ()
        step = (pl.program_id(0) * nn + pl.program_id(1)) * nk + pl.program_id(2)
        stages = hosted.stages(hin_refs, hout_refs, sem_refs) if hosted else []
        for frac, emit in stages:
            if frac < 1.0:
                pl.when(step == int(frac * (nm * nn * nk - 1)))(emit)

        def finish(acc):
            res = epi(acc, *[r[...] for r in ex_refs]) if epi is not None else (acc,)
            for o_ref, val in zip(out_refs, res):
                o_ref[...] = val.astype(o_ref.dtype)

        p = _dg(a_ref[...], b_ref[...], dims)
        if nk == 1:
            finish(p)
        else:
            acc_ref = rest[0]
            kk = pl.program_id(2)

            @pl.when(kk == 0)
            def _():
                acc_ref[...] = p

            @pl.when(kk > 0)
            def _():
                acc_ref[...] += p

            @pl.when(kk == nk - 1)
            def _():
                finish(acc_ref[...])

        for frac, emit in stages:
            if frac >= 1.0:
                pl.when(step == nm * nn * nk - 1)(emit)

    outs = _pc(
        body, name=name, grid=(nm, nn, nk),
        in_specs=[a_spec, b_spec] + ex_specs + [_ANY] * n_hin,
        out_specs=[pl.BlockSpec((tm, tn), lambda i, j, kk: (i, j))] * n_out + [_ANY] * n_hout,
        out_shape=[jax.ShapeDtypeStruct((m, n), dt) for dt in out_dtypes] + (hosted.out_shapes if hosted else []),
        scratch_shapes=([pltpu.VMEM((tm, tn), f32)] if nk > 1 else []) + (hosted.sems if hosted else []),
        compiler_params=_params(*(("arbitrary",) * 3 if hosted else ("parallel", "parallel", "arbitrary"))),
    )(a, b, *[e for e, _ in extras], *(hosted.arrays if hosted else []))
    return outs[0] if len(outs) == 1 else outs


def _row_tile(rows, cols, n_arrays, itemsize=4, budget=24 * 2**20):
    best = None
    for t in range(16, rows + 1, 16):
        if rows % t == 0 and 2 * n_arrays * t * cols * itemsize <= budget:
            best = t
    return best or rows


def _tile(rows, cols, n_arrays, budget=24 * 2**20):
    if rows % 16 == 0 or cols % LANES != 0:
        return _row_tile(rows, cols, n_arrays, budget=budget), cols
    fits = [t for t in range(LANES, cols + 1, LANES) if cols % t == 0 and 2 * n_arrays * rows * t * 4 <= budget]
    return rows, (max(fits) if fits else LANES)


def cast_bf16(name, x):
    rows, cols = x.shape
    tr, tc = _tile(rows, cols, 2)

    def body(x_ref, o_ref):
        o_ref[...] = x_ref[...].astype(bf16)

    blk = pl.BlockSpec((tr, tc), lambda i, j: (i, j))
    return _pc(body, name=name, grid=(rows // tr, cols // tc), in_specs=[blk], out_specs=blk,
               out_shape=jax.ShapeDtypeStruct(x.shape, bf16), compiler_params=_params("parallel", "parallel"))(x)


def norm_fwd(hp, norm_w, tm):
    tp, d = hp.shape

    def body(h_ref, w_ref, o_ref):
        o_ref[...] = rmsnorm(h_ref[...], w_ref[...]).astype(bf16)

    return _pc(body, name="norm_fwd", grid=(tp // tm,),
               in_specs=[pl.BlockSpec((tm, d), lambda i: (i, 0)), pl.BlockSpec((1, d), lambda i: (0, 0))],
               out_specs=pl.BlockSpec((tm, d), lambda i: (i, 0)), out_shape=jax.ShapeDtypeStruct((tp, d), bf16),
               compiler_params=_params("parallel"))(hp, norm_w)


def norm_bwd(hp, norm_w, dxn, dout, tm, hosted=None):
    tp, d = hp.shape
    steps = tp // tm
    n_hin = len(hosted.arrays) if hosted else 0
    n_hout = len(hosted.out_shapes) if hosted else 0

    def body(h_ref, w_ref, dxn_ref, dout_ref, *rest):
        hin_refs, (dh_ref, dw_ref), rest = rest[:n_hin], rest[n_hin:n_hin + 2], rest[n_hin + 2:]
        stages = hosted.stages(hin_refs, rest[:n_hout], rest[n_hout:]) if hosted else []
        for frac, emit in stages:
            if frac < 1.0:
                pl.when(pl.program_id(0) == int(frac * (steps - 1)))(emit)
        _, vjp = jax.vjp(rmsnorm, h_ref[...], w_ref[...])
        dh, dw = vjp(dxn_ref[...])
        dh_ref[...] = dh + dout_ref[...]

        @pl.when(pl.program_id(0) == 0)
        def _():
            dw_ref[...] = jnp.zeros_like(dw_ref)

        dw_ref[...] += dw
        for frac, emit in stages:
            if frac >= 1.0:
                pl.when(pl.program_id(0) == steps - 1)(emit)

    row = pl.BlockSpec((tm, d), lambda i: (i, 0))
    vec = pl.BlockSpec((1, d), lambda i: (0, 0))
    outs = _pc(body, name="norm_bwd", grid=(steps,), in_specs=[row, vec, row, row] + [_ANY] * n_hin,
               out_specs=[row, vec] + [_ANY] * n_hout,
               out_shape=[jax.ShapeDtypeStruct((tp, d), f32), jax.ShapeDtypeStruct((1, d), f32)] + (hosted.out_shapes if hosted else []),
               scratch_shapes=hosted.sems if hosted else [],
               compiler_params=_params("arbitrary"))(hp, norm_w, dxn, dout, *(hosted.arrays if hosted else []))
    return outs[0], outs[1], list(outs[2:])


def pool_fwd(proj, mix, scale, pw):
    tp = proj.shape[0]
    g = pw // POOL_GROUPS

    def body(u_ref, z_ref, mix_ref, s_ref, y_ref):
        y_ref[...] = pool_fn(u_ref[...], z_ref[...], mix_ref[0], s_ref[...], pl.program_id(0)).astype(bf16)

    return _pc(body, name="pool_fwd", grid=(POOL_GROUPS,),
               in_specs=[pl.BlockSpec((tp, g), lambda i: (0, i)), pl.BlockSpec((tp, g), lambda i: (0, POOL_GROUPS + i)),
                         pl.BlockSpec((1, g, g), lambda i: (i, 0, 0)), pl.BlockSpec((1, g), lambda i: (0, i))],
               out_specs=pl.BlockSpec((tp, g), lambda i: (0, i)), out_shape=jax.ShapeDtypeStruct((tp, pw), bf16),
               compiler_params=_params("parallel"))(proj, proj, mix, scale)


def pool_bwd(proj, mix, scale, dy, pw):
    tp = proj.shape[0]
    g = pw // POOL_GROUPS

    def body(u_ref, z_ref, mix_ref, s_ref, dy_ref, du_ref, dz_ref, dmix_ref, ds_ref):
        grp = pl.program_id(0)
        _, vjp = jax.vjp(lambda u, z, m, s: pool_fn(u, z, m, s, grp), u_ref[...], z_ref[...], mix_ref[0].astype(f32), s_ref[...])
        du, dz, dmix, ds = vjp(dy_ref[...])
        du_ref[...] = du.astype(bf16)
        dz_ref[...] = dz.astype(bf16)
        dmix_ref[0] = dmix
        ds_ref[...] = ds

    col = pl.BlockSpec((tp, g), lambda i: (0, i))
    return _pc(body, name="pool_bwd", grid=(POOL_GROUPS,),
               in_specs=[col, pl.BlockSpec((tp, g), lambda i: (0, POOL_GROUPS + i)),
                         pl.BlockSpec((1, g, g), lambda i: (i, 0, 0)), pl.BlockSpec((1, g), lambda i: (0, i)), col],
               out_specs=[col, col, pl.BlockSpec((1, g, g), lambda i: (i, 0, 0)), pl.BlockSpec((1, g), lambda i: (0, i))],
               out_shape=[jax.ShapeDtypeStruct((tp, pw), bf16), jax.ShapeDtypeStruct((tp, pw), bf16),
                          jax.ShapeDtypeStruct((POOL_GROUPS, g, g), f32), jax.ShapeDtypeStruct((1, pw), f32)],
               compiler_params=_params("parallel"))(proj, proj, mix, scale, dy)


def _dn_pre_specs(tp, n_heads, q_off):
    hb = lambda off: pl.BlockSpec((tp, HEAD_DIM), functools.partial(lambda h, o: (0, o + h), o=off))
    cw = lambda off: pl.BlockSpec((CONV_WIDTH, HEAD_DIM), functools.partial(lambda h, o: (0, o + h), o=off))
    whole = lambda shape: pl.BlockSpec(shape, lambda h: (0, 0))
    return ([hb(q_off), hb(q_off + n_heads), hb(q_off + 2 * n_heads), whole((tp, LANES)),
             cw(0), cw(n_heads), cw(2 * n_heads), whole((1, LANES)), whole((1, LANES))], hb, cw, whole)


def _pre_rows(tp):
    return max(t for t in range(16, min(tp, 192) + 1, 16) if tp % t == 0)


def _with_history(ref, r0, rows):
    if r0 == 0:
        return jnp.concatenate([jnp.zeros((PRE_HALO, ref.shape[1]), f32), ref[0:rows, :]], axis=0)
    return ref[r0 - PRE_HALO:r0 + rows, :]


def dn_pre_fwd(proj, ba, conv_w, a_log, dt_bias, n_heads, q_off):
    tp = proj.shape[0]
    rows = _pre_rows(tp)
    in_specs, hb, _, _ = _dn_pre_specs(tp, n_heads, q_off)

    def body(q_ref, k_ref, v_ref, ba_ref, cq_ref, ck_ref, cv_ref, al_ref, dt_ref, *out_refs):
        for r0 in range(0, tp, rows):
            outs = dn_pre_fn(_with_history(q_ref, r0, rows), _with_history(k_ref, r0, rows), _with_history(v_ref, r0, rows),
                             ba_ref[r0:r0 + rows, :], cq_ref[...], ck_ref[...], cv_ref[...], al_ref[...], dt_ref[...],
                             pl.program_id(0), n_heads, r0)
            for o_ref, val in zip(out_refs, outs):
                o_ref[r0:r0 + rows, :] = val

    return _pc(body, name="dn_pre_fwd", grid=(n_heads,), in_specs=in_specs, out_specs=[hb(0)] * 5,
               out_shape=[jax.ShapeDtypeStruct((tp, n_heads * HEAD_DIM), f32)] * 5,
               compiler_params=_params("parallel"))(proj, proj, proj, ba, conv_w, conv_w, conv_w, a_log, dt_bias)


def dn_pre_bwd(proj, ba, conv_w, a_log, dt_bias, cots, n_heads, q_off):
    tp = proj.shape[0]
    rows = _pre_rows(tp)
    in_specs, hb, cw, whole = _dn_pre_specs(tp, n_heads, q_off)

    def body(q_ref, k_ref, v_ref, ba_ref, cq_ref, ck_ref, cv_ref, al_ref, dt_ref, dq_ref, dk_ref, dv_ref, db_ref, dg_ref,
             dqr_ref, dkr_ref, dvr_ref, dba_ref, dcq_ref, dck_ref, dcv_ref, dal_ref, ddt_ref):
        head = pl.program_id(0)

        @pl.when(head == 0)
        def _():
            dba_ref[...] = jnp.zeros_like(dba_ref)
            dal_ref[...] = jnp.zeros_like(dal_ref)
            ddt_ref[...] = jnp.zeros_like(ddt_ref)

        owed = [jnp.zeros((PRE_HALO, HEAD_DIM), f32)] * 3
        d_conv = [jnp.zeros((CONV_WIDTH, HEAD_DIM), f32)] * 3
        d_al, d_dt = jnp.zeros((1, LANES), f32), jnp.zeros((1, LANES), f32)
        for r0 in reversed(range(0, tp, rows)):
            fn = lambda *args, r0=r0: dn_pre_fn(*args, head, n_heads, r0)
            _, vjp = jax.vjp(fn, _with_history(q_ref, r0, rows), _with_history(k_ref, r0, rows), _with_history(v_ref, r0, rows),
                             ba_ref[r0:r0 + rows, :], cq_ref[...], ck_ref[...], cv_ref[...], al_ref[...], dt_ref[...])
            tile = slice(r0, r0 + rows)
            *d_raw, dba, dcq, dck, dcv, dal, ddt = vjp((dq_ref[tile, :], dk_ref[tile, :], dv_ref[tile, :], db_ref[tile, :], dg_ref[tile, :]))
            for i, (out_ref, d) in enumerate(zip((dqr_ref, dkr_ref, dvr_ref), d_raw)):
                out_ref[tile, :] = jnp.concatenate([d[PRE_HALO:rows], d[rows:] + owed[i]], axis=0).astype(bf16)
                owed[i] = d[:PRE_HALO]
            dba_ref[tile, :] += dba
            d_conv = [acc + d for acc, d in zip(d_conv, (dcq, dck, dcv))]
            d_al, d_dt = d_al + dal, d_dt + ddt
        dcq_ref[...], dck_ref[...], dcv_ref[...] = d_conv
        dal_ref[...] += d_al
        ddt_ref[...] += d_dt

    w = n_heads * HEAD_DIM
    return _pc(body, name="dn_pre_bwd", grid=(n_heads,), in_specs=in_specs + [hb(0)] * 5,
               out_specs=[hb(0)] * 3 + [whole((tp, LANES)), cw(0), cw(0), cw(0), whole((1, LANES)), whole((1, LANES))],
               out_shape=[jax.ShapeDtypeStruct((tp, w), bf16)] * 3 + [jax.ShapeDtypeStruct((tp, LANES), f32)]
               + [jax.ShapeDtypeStruct((CONV_WIDTH, w), f32)] * 3 + [jax.ShapeDtypeStruct((1, LANES), f32)] * 2,
               compiler_params=_params("arbitrary"))(proj, proj, proj, ba, conv_w, conv_w, conv_w, a_log, dt_bias, *cots)


def _super_rows(tp):
    n = tp // CHUNK
    return CHUNK * max(j for j in (4, 3, 2, 1) if n % j == 0)


def _heads_per_step(n_heads):
    return max(j for j in (4, 2, 1) if n_heads % j == 0)


def dn_intra_fwd(q, k, v, beta_b, g_b, n_heads):
    tp = q.shape[0]
    rows = _super_rows(tp)
    ns = tp // rows

    hps = _heads_per_step(n_heads)

    def body(q_ref, k_ref, v_ref, b_ref, g_ref, u_ref, w_ref, qd_ref, kd_ref, qk_ref, gl_ref, t_ref):
        lanes = [slice(i * HEAD_DIM, (i + 1) * HEAD_DIM) for i in range(hps)]
        gates = [gate_fn(g_ref[:, sl]) for sl in lanes]
        tmats = tri_inv([lmat_fn(k_ref[:, sl], b_ref[:, sl], gt[0]) for sl, gt in zip(lanes, gates)])
        for i, (sl, (decay, eg, kfac, gl), tmat) in enumerate(zip(lanes, gates, tmats)):
            u_c, w_c, q_dec, k_dec, qk_c = intra_fn(tmat, q_ref[:, sl], k_ref[:, sl], v_ref[:, sl], b_ref[:, sl], decay, eg, kfac)
            u_ref[:, sl] = u_c
            w_ref[:, sl] = w_c
            qd_ref[:, sl] = q_dec
            kd_ref[:, sl] = k_dec
            qk_ref[:, sl] = qk_c
            gl_ref[:, sl] = gl
            t_ref[i, 0] = tmat

    blk = pl.BlockSpec((rows, hps * HEAD_DIM), lambda h, s: (s, h))
    return _pc(body, name="dn_intra_fwd", grid=(n_heads // hps, ns), in_specs=[blk] * 5,
               out_specs=[blk] * 6 + [pl.BlockSpec((hps, 1, rows, rows), lambda h, s: (h, s, 0, 0))],
               out_shape=[jax.ShapeDtypeStruct(q.shape, f32)] * 6 + [jax.ShapeDtypeStruct((n_heads, ns, rows, rows), f32)],
               compiler_params=_params("parallel", "parallel"))(q, k, v, beta_b, g_b)


def dn_intra_bwd(q, k, v, beta_b, g_b, tmats, cots, n_heads):
    tp = q.shape[0]
    rows = _super_rows(tp)
    ns = tp // rows

    hps = _heads_per_step(n_heads)

    def body(q_ref, k_ref, v_ref, b_ref, g_ref, t_ref, du_ref, dw_ref, dqd_ref, dkd_ref, dqk_ref, dgl_ref,
             dq_ref, dk_ref, dv_ref, db_ref, dg_ref):
        lanes = [slice(i * HEAD_DIM, (i + 1) * HEAD_DIM) for i in range(hps)]
        tmats = [t_ref[i, 0] for i in range(hps)]
        gates = [jax.vjp(gate_fn, g_ref[:, sl]) for sl in lanes]
        intra = [jax.vjp(intra_fn, tmat, q_ref[:, sl], k_ref[:, sl], v_ref[:, sl], b_ref[:, sl], gt[0][0], gt[0][1], gt[0][2])[1](
            (du_ref[:, sl], dw_ref[:, sl], dqd_ref[:, sl], dkd_ref[:, sl], dqk_ref[:, sl]))
            for sl, tmat, gt in zip(lanes, tmats, gates)]
        tts = [tmat.T for tmat in tmats]
        dls = [mm_nn(tt, res[0]) for tt, res in zip(tts, intra)]
        dls = [-mm_nn(dl, tt) for dl, tt in zip(dls, tts)]
        for sl, gt, res, dl in zip(lanes, gates, intra, dls):
            _, dq, dk, dv, db, ddecay, deg, dkfac = res
            dk2, db2, ddecay2 = jax.vjp(lmat_fn, k_ref[:, sl], b_ref[:, sl], gt[0][0])[1](dl)
            (dg,) = gt[1]((ddecay + ddecay2, deg, dkfac, dgl_ref[:, sl]))
            dq_ref[:, sl] = dq
            dk_ref[:, sl] = dk + dk2
            dv_ref[:, sl] = dv
            db_ref[:, sl] = db + db2
            dg_ref[:, sl] = dg

    blk = pl.BlockSpec((rows, hps * HEAD_DIM), lambda h, s: (s, h))
    return _pc(body, name="dn_intra_bwd", grid=(n_heads // hps, ns),
               in_specs=[blk] * 5 + [pl.BlockSpec((hps, 1, rows, rows), lambda h, s: (h, s, 0, 0))] + [blk] * 6,
               out_specs=[blk] * 5, out_shape=[jax.ShapeDtypeStruct(q.shape, f32)] * 5,
               compiler_params=_params("parallel", "parallel"))(q, k, v, beta_b, g_b, tmats, *cots)


def dn_seq_fwd(inter, proj, dn_norm_w, n_heads, zd_off):
    tp, width = inter[0].shape
    n_chunks = tp // CHUNK

    def body(u_ref, w_ref, qd_ref, kd_ref, qk_ref, gl_ref, z_ref, nw_ref, y_ref, s_ref, state):
        @pl.when(pl.program_id(0) == 0)
        def _():
            state[...] = jnp.zeros_like(state)

        lanes = [slice(h * HEAD_DIM, (h + 1) * HEAD_DIM) for h in range(n_heads)]
        sts = [state[h] for h in range(n_heads)]
        for h, st in enumerate(sts):
            s_ref[0, h] = st
        v_new = [u_ref[:, sl] - mm_nn(w_ref[:, sl], st) for sl, st in zip(lanes, sts)]
        outs = [mm_nn(qd_ref[:, sl], st) + mm_nn(qk_ref[:, sl][:, :CHUNK], vn) for sl, st, vn in zip(lanes, sts, v_new)]
        for h, (sl, st, vn) in enumerate(zip(lanes, sts, v_new)):
            state[h] = st * gl_ref[0:1, sl] + mm_tn(kd_ref[:, sl], vn)
        for sl, o in zip(lanes, outs):
            y_ref[:, sl] = gated_norm(o, nw_ref[...], z_ref[:, sl]).astype(bf16)

    blk = pl.BlockSpec((CHUNK, width), lambda n: (n, 0))
    return _pc(body, name="dn_seq_fwd", grid=(n_chunks,),
               in_specs=[blk] * 6 + [pl.BlockSpec((CHUNK, width), lambda n: (n, zd_off)), pl.BlockSpec((1, HEAD_DIM), lambda n: (0, 0))],
               out_specs=[blk, pl.BlockSpec((1, n_heads, HEAD_DIM, HEAD_DIM), lambda n: (n, 0, 0, 0))],
               out_shape=[jax.ShapeDtypeStruct((tp, width), bf16), jax.ShapeDtypeStruct((n_chunks, n_heads, HEAD_DIM, HEAD_DIM), f32)],
               scratch_shapes=[pltpu.VMEM((n_heads, HEAD_DIM, HEAD_DIM), f32)],
               compiler_params=_params("arbitrary"))(*inter, proj, dn_norm_w)


def dn_seq_bwd(inter, proj, dn_norm_w, states, dy, n_heads, zd_off):
    tp, width = inter[0].shape
    n_chunks = tp // CHUNK
    last = n_chunks - 1

    def body(u_ref, w_ref, qd_ref, kd_ref, qk_ref, gl_ref, z_ref, nw_ref, s_ref, dy_ref,
             du_ref, dw_ref, dqd_ref, dkd_ref, dqk_ref, dgl_ref, dz_ref, dnw_ref, dstate):
        @pl.when(pl.program_id(0) == 0)
        def _():
            dstate[...] = jnp.zeros_like(dstate)
            dnw_ref[...] = jnp.zeros_like(dnw_ref)

        lanes = [slice(h * HEAD_DIM, (h + 1) * HEAD_DIM) for h in range(n_heads)]
        sts = [s_ref[0, h] for h in range(n_heads)]
        dsts = [dstate[h] for h in range(n_heads)]
        v_new = [u_ref[:, sl] - mm_nn(w_ref[:, sl], st) for sl, st in zip(lanes, sts)]
        outs = [mm_nn(qd_ref[:, sl], st) + mm_nn(qk_ref[:, sl][:, :CHUNK], vn) for sl, st, vn in zip(lanes, sts, v_new)]
        dnw = jnp.zeros((1, HEAD_DIM), f32)
        d_outs = []
        for sl, o in zip(lanes, outs):
            do, dn, dz = jax.vjp(gated_norm, o, nw_ref[...], z_ref[:, sl])[1](dy_ref[:, sl])
            dz_ref[:, sl] = dz.astype(bf16)
            dnw = dnw + dn
            d_outs.append(do)
        dnw_ref[...] += dnw
        d_vn = [mm_tn(qk_ref[:, sl][:, :CHUNK], do) + mm_nn(kd_ref[:, sl], ds) for sl, do, ds in zip(lanes, d_outs, dsts)]
        zeros = jnp.zeros((HEAD_DIM - CHUNK, HEAD_DIM), f32)
        rows = lax.broadcasted_iota(jnp.int32, (CHUNK, HEAD_DIM), 0)
        for h, (sl, st, vn, do, ds, dvn) in enumerate(zip(lanes, sts, v_new, d_outs, dsts, d_vn)):
            du_ref[:, sl] = dvn
            dw_ref[:, sl] = -mm_nt(dvn, st)
            dqd_ref[:, sl] = mm_nt(do, st)
            dkd_ref[:, sl] = mm_nt(vn, ds)
            dqk_ref[:, sl] = mm_nt(do, jnp.concatenate([vn, zeros], axis=0))
            dgl_ref[:, sl] = jnp.where(rows == 0, jnp.sum(st * ds, axis=0, keepdims=True), 0.0)
            dstate[h] = ds * gl_ref[0:1, sl] + mm_tn(qd_ref[:, sl], do) - mm_tn(w_ref[:, sl], dvn)

    blk = pl.BlockSpec((CHUNK, width), lambda n: (last - n, 0))
    return _pc(body, name="dn_seq_bwd", grid=(n_chunks,),
               in_specs=[blk] * 6 + [pl.BlockSpec((CHUNK, width), lambda n: (last - n, zd_off)), pl.BlockSpec((1, HEAD_DIM), lambda n: (0, 0)),
                         pl.BlockSpec((1, n_heads, HEAD_DIM, HEAD_DIM), lambda n: (last - n, 0, 0, 0)), blk],
               out_specs=[blk] * 7 + [pl.BlockSpec((1, HEAD_DIM), lambda n: (0, 0))],
               out_shape=[jax.ShapeDtypeStruct((tp, width), f32)] * 6 + [jax.ShapeDtypeStruct((tp, width), bf16),
                                                                          jax.ShapeDtypeStruct((1, HEAD_DIM), f32)],
               scratch_shapes=[pltpu.VMEM((n_heads, HEAD_DIM, HEAD_DIM), f32)],
               compiler_params=_params("arbitrary"))(*inter, proj, dn_norm_w, states, dy)


def loss_stage(out, final_w, target):
    tp, d = out.shape
    n_tiles = tp // CHUNK

    def body(o_ref, w_ref, t_ref, loss_ref, do_ref, dob_ref, dw_ref):
        i = pl.program_id(0)

        @pl.when(i == 0)
        def _():
            loss_ref[...] = jnp.zeros_like(loss_ref)
            dw_ref[...] = jnp.zeros_like(dw_ref)

        scored = (i > 0).astype(f32)
        val, (do, dw) = jax.value_and_grad(lambda o, w: scored * loss_fn(o, w, t_ref[...]), argnums=(0, 1))(o_ref[...], w_ref[...])
        loss_ref[...] += jnp.full(loss_ref.shape, val, f32)
        do_ref[...] = do
        dob_ref[...] = do.astype(bf16)
        dw_ref[...] += dw

    row = pl.BlockSpec((CHUNK, d), lambda i: (i, 0))
    vec = pl.BlockSpec((1, d), lambda i: (0, 0))
    return _pc(body, name="loss_stage", grid=(n_tiles,),
               in_specs=[row, vec, pl.BlockSpec((CHUNK, d), lambda i: (jnp.maximum(i - 1, 0), 0))],
               out_specs=[pl.BlockSpec((1, LANES), lambda i: (0, 0)), row, row, vec],
               out_shape=[jax.ShapeDtypeStruct((1, LANES), f32), jax.ShapeDtypeStruct((tp, d), f32),
                          jax.ShapeDtypeStruct((tp, d), bf16), jax.ShapeDtypeStruct((1, d), f32)],
               compiler_params=_params("arbitrary"))(out, final_w, target)


def _adam_update(w, g, m, v):
    nm = ADAM_B1 * m + (1.0 - ADAM_B1) * g
    nv = ADAM_B2 * v + (1.0 - ADAM_B2) * jnp.square(g)
    m_hat = nm / (1.0 - ADAM_B1 ** ADAM_STEP)
    v_hat = nv / (1.0 - ADAM_B2 ** ADAM_STEP)
    return -ADAM_LR * (m_hat / (jnp.sqrt(v_hat) + ADAM_EPS) + ADAM_WD * w), nm, nv


def adamw(name, w, g, m, v):
    rows, cols = w.shape
    t = _row_tile(rows, cols, 7)

    def body(w_ref, g_ref, m_ref, v_ref, d_ref, nm_ref, nv_ref):
        d_ref[...], nm_ref[...], nv_ref[...] = _adam_update(w_ref[...], g_ref[...], m_ref[...], v_ref[...])

    blk = pl.BlockSpec((t, cols), lambda i: (i, 0))
    return _pc(body, name=name, grid=(rows // t,), in_specs=[blk] * 4, out_specs=[blk] * 3,
               out_shape=[jax.ShapeDtypeStruct(w.shape, f32)] * 3, compiler_params=_params("parallel"))(w, g, m, v)


def adamw_joined(name, w, g_mine, g_theirs, m, v, half, axis):
    rows, cols = w.shape
    hr, hc = g_mine.shape
    tr, tc = _tile(hr, hc, 9)
    nr, nc = hr // tr, hc // tc

    def body(half_ref, w_ref, gm_ref, gt_ref, m_ref, v_ref, g_ref, d_ref, nm_ref, nv_ref):
        pos = pl.program_id(axis) // (nr if axis == 0 else nc)
        g = jnp.where(pos == half_ref[0], gm_ref[...], gt_ref[...])
        delta, nm, nv = _adam_update(w_ref[...], g, m_ref[...], v_ref[...])
        g_ref[...] = g
        d_ref[...] = delta
        nm_ref[...] = nm
        nv_ref[...] = nv

    whole = pl.BlockSpec((tr, tc), lambda i, j, hf: (i, j))
    part = pl.BlockSpec((tr, tc), lambda i, j, hf: (i % nr, j % nc))
    grid_spec = pltpu.PrefetchScalarGridSpec(num_scalar_prefetch=1, grid=(rows // tr, cols // tc),
                                             in_specs=[whole, part, part, whole, whole], out_specs=[whole] * 4)
    return _pc(body, name=name, grid_spec=grid_spec, out_shape=[jax.ShapeDtypeStruct(w.shape, f32)] * 4,
               compiler_params=_params("parallel", "parallel"))(half, w, g_mine, g_theirs, m, v)


def add_halves(name, g, recv, half, kind):
    s, r, c = recv.shape
    tr, tc = _tile(r, c, 3)
    nc = c // tc

    def body(half_ref, g_ref, r_ref, o_ref):
        o_ref[...] = (g_ref[...].reshape(r_ref.shape).astype(f32) + r_ref[...].astype(f32)).astype(bf16)

    if kind == "lead":
        g_spec = pl.BlockSpec((1, 1, tr, tc), lambda i, j, k, hf: (hf[0], i, j, k))
    else:
        g_spec = pl.BlockSpec((1, tr, tc), lambda i, j, k, hf: (i, j, hf[0] * nc + k))
    blk = pl.BlockSpec((1, tr, tc), lambda i, j, k, hf: (i, j, k))
    grid_spec = pltpu.PrefetchScalarGridSpec(num_scalar_prefetch=1, grid=(s, r // tr, nc), in_specs=[g_spec, blk], out_specs=blk)
    return _pc(body, name=name, grid_spec=grid_spec, out_shape=jax.ShapeDtypeStruct((s, r, c), bf16),
               compiler_params=_params("parallel", "parallel", "parallel"))(half, g, recv)


def add_first(name, parts, got, chips):
    _, r, c = parts.shape
    tr, tc = _tile(r, c, 6)

    def body(chips_ref, mine_ref, theirs_ref, got_ref, keep_ref, pass_ref):
        keep_ref[...] = mine_ref[0].astype(f32) + got_ref[0].astype(f32)
        pass_ref[...] = (theirs_ref[0].astype(f32) + got_ref[1].astype(f32)).astype(bf16)

    blk = pl.BlockSpec((tr, tc), lambda i, j, ch: (i, j))
    grid_spec = pltpu.PrefetchScalarGridSpec(
        num_scalar_prefetch=1, grid=(r // tr, c // tc),
        in_specs=[pl.BlockSpec((1, tr, tc), lambda i, j, ch: (ch[0], i, j)), pl.BlockSpec((1, tr, tc), lambda i, j, ch: (ch[1], i, j)),
                  pl.BlockSpec((2, tr, tc), lambda i, j, ch: (0, i, j))],
        out_specs=[blk, blk])
    return _pc(body, name=name, grid_spec=grid_spec, out_shape=[jax.ShapeDtypeStruct((r, c), f32), jax.ShapeDtypeStruct((r, c), bf16)],
               compiler_params=_params("parallel", "parallel"))(chips, parts, parts, got)


def add_second(name, kept, got):
    r, c = kept.shape
    tr, tc = _tile(r, c, 3)

    def body(k_ref, g_ref, o_ref):
        o_ref[...] = k_ref[...] + g_ref[...].astype(f32)

    blk = pl.BlockSpec((tr, tc), lambda i, j: (i, j))
    return _pc(body, name=name, grid=(r // tr, c // tc), in_specs=[blk, blk], out_specs=blk,
               out_shape=jax.ShapeDtypeStruct((r, c), f32), compiler_params=_params("parallel", "parallel"))(kept, got)


def sum_leading(name, x, out_dtype=f32):
    s, r, c = x.shape
    tr, tc = _tile(r, c, s + 1)

    def body(x_ref, o_ref):
        acc = x_ref[0].astype(f32)
        for i in range(1, s):
            acc = acc + x_ref[i].astype(f32)
        o_ref[...] = acc.astype(out_dtype)

    return _pc(body, name=name, grid=(r // tr, c // tc), in_specs=[pl.BlockSpec((s, tr, tc), lambda i, j: (0, i, j))],
               out_specs=pl.BlockSpec((tr, tc), lambda i, j: (i, j)), out_shape=jax.ShapeDtypeStruct((r, c), out_dtype),
               compiler_params=_params("parallel", "parallel"))(x)


def _place():
    x, y, c = lax.axis_index("x"), lax.axis_index("y"), lax.axis_index("c")
    return x, y, c


def _route(x, y, c):
    return (x ^ (1 - c), y ^ c), (x ^ c, y ^ (1 - c)), (1 - x, 1 - y)


def _half_view(ref, half, kind, lead=()):
    if kind == "lead":
        return ref.at[(*lead, half)]
    width = ref.shape[-1] // 2
    return ref.at[(*lead, *([slice(None)] * (len(ref.shape) - len(lead) - 1)), pl.ds(half * width, width))]


def _gather_plan(ins, outs, sems, kinds):
    n = len(ins)
    send1, recv1, send2, recv2 = sems
    x, y, c = _place()
    chip = 2 * x + y
    sibling = (x, y, 1 - c)
    near, far, diag = _route(x, y, c)
    near_id, far_id, diag_id = [2 * cx + cy for cx, cy in (near, far, diag)]

    def remote(src, dst, s_sem, r_sem, to):
        return pltpu.make_async_remote_copy(src_ref=src, dst_ref=dst, send_sem=s_sem, recv_sem=r_sem, device_id=to, device_id_type=MESH)

    def slab(a, chip_id, half):
        return _half_view(outs[a], half, kinds[a], (chip_id,))

    def landed(a, chip_id, sem, frm):
        return remote(slab(a, chip_id, c), slab(a, chip_id, c), send1.at[a, sem], recv1.at[a, sem], (*frm, c))

    def onward(a, chip_id, sem):
        return remote(slab(a, chip_id, c), slab(a, chip_id, c), send2.at[a, sem], recv2.at[a, sem], sibling)

    own = [remote(_half_view(ins[a], c, kinds[a]), slab(a, chip, c), send1.at[a, j], recv1.at[a, j], (*to, c))
           for a in range(n) for j, to in enumerate((near, far))]
    relay = [remote(slab(a, near_id, c), slab(a, near_id, c), send1.at[a, 2], recv1.at[a, 2], (*far, c)) for a in range(n)]
    to_sibling = [[onward(a, cid, sem) for sem, cid in enumerate((near_id, far_id, diag_id))] for a in range(n)]
    from_sibling = [remote(slab(a, cid, 1 - c), slab(a, cid, 1 - c), send2.at[a, sem], recv2.at[a, sem], sibling)
                    for a in range(n) for sem, cid in enumerate((far_id, near_id, diag_id))]

    def start_own():
        for cp in own:
            cp.start()

    def pass_near():
        for a in range(n):
            landed(a, near_id, 0, near).wait_recv()
            relay[a].start()
            to_sibling[a][0].start()

    def pass_far():
        for sem, cid in ((1, far_id), (2, diag_id)):
            for a in range(n):
                landed(a, cid, sem, far).wait_recv()
                to_sibling[a][sem].start()

    def finish():
        for cp in from_sibling:
            cp.wait_recv()
        for cp in own + relay + [cp for row in to_sibling for cp in row]:
            cp.wait_send()

    return [start_own, pass_near, pass_far, finish]


def _gather_shapes(shards):
    return ([jax.ShapeDtypeStruct((4,) + s.shape, s.dtype) for s in shards], [pltpu.SemaphoreType.DMA((len(shards), 3))] * 4)


def gather_weights(shards, kinds):
    n = len(shards)

    def body(*refs):
        for emit in _gather_plan(refs[:n], refs[n:2 * n], refs[2 * n:], kinds):
            emit()

    out_shapes, sems = _gather_shapes(shards)
    return _pc(body, name="gather_weights", in_specs=[_ANY] * n, out_specs=[_ANY] * n, out_shape=out_shapes, scratch_shapes=sems)(*shards)


def hosted_gather(shards, kinds):
    out_shapes, sems = _gather_shapes(shards)
    return Hosted(shards, out_shapes, sems, lambda i, o, s: list(zip((0.0, 0.45, 0.8, 1.0), _gather_plan(i, o, s, kinds))))


def swap_with_sibling(name, sends):
    n = len(sends)

    def body(*refs):
        ins, outs = refs[:n], refs[n:2 * n]
        send, recv = refs[2 * n:]
        x, y, c = _place()
        cps = [pltpu.make_async_remote_copy(src_ref=ins[a], dst_ref=outs[a], send_sem=send.at[a], recv_sem=recv.at[a],
                                            device_id=(x, y, 1 - c), device_id_type=MESH) for a in range(n)]
        for cp in cps:
            cp.start()
        for cp in cps:
            cp.wait()

    return _pc(body, name=name, in_specs=[_ANY] * n, out_specs=[_ANY] * n,
               out_shape=[jax.ShapeDtypeStruct(s.shape, s.dtype) for s in sends],
               scratch_shapes=[pltpu.SemaphoreType.DMA((n,))] * 2)(*sends)


def send_grad_halves(name, grads, kinds):
    n = len(grads)

    def body(*refs):
        ins, outs = refs[:n], refs[n:2 * n]
        send, recv = refs[2 * n:]
        x, y, c = _place()
        cps = [pltpu.make_async_remote_copy(src_ref=_half_view(ins[a], 1 - c, kinds[a]), dst_ref=outs[a], send_sem=send.at[a],
                                            recv_sem=recv.at[a], device_id=(x, y, 1 - c), device_id_type=MESH) for a in range(n)]
        for cp in cps:
            cp.start()
        for cp in cps:
            cp.wait()

    shape = lambda g, kind: g.shape[1:] if kind == "lead" else g.shape[:-1] + (g.shape[-1] // 2,)
    return _pc(body, name=name, in_specs=[_ANY] * n, out_specs=[_ANY] * n,
               out_shape=[jax.ShapeDtypeStruct(shape(g, k), g.dtype) for g, k in zip(grads, kinds)],
               scratch_shapes=[pltpu.SemaphoreType.DMA((n,))] * 2)(*grads)


def _scatter_first_plan(ins, outs, sems):
    n = len(ins)
    send, recv = sems
    x, y, c = _place()
    near, _, diag = _route(x, y, c)
    cps = [pltpu.make_async_remote_copy(src_ref=ins[a].at[2 * cx + cy], dst_ref=outs[a].at[j], send_sem=send.at[a, j],
                                        recv_sem=recv.at[a, j], device_id=(*near, c), device_id_type=MESH)
           for a in range(n) for j, (cx, cy) in enumerate((near, diag))]

    def start():
        for cp in cps:
            cp.start()

    def wait():
        for cp in cps:
            cp.wait()

    return [start, wait]


def _scatter_first_shapes(parts):
    return ([jax.ShapeDtypeStruct((2,) + p.shape[1:], p.dtype) for p in parts], [pltpu.SemaphoreType.DMA((len(parts), 2))] * 2)


def hosted_scatter_first(parts):
    out_shapes, sems = _scatter_first_shapes(parts)
    return Hosted(parts, out_shapes, sems, lambda i, o, s: list(zip((0.0, 1.0), _scatter_first_plan(i, o, s))))


def _far_swap_plan(ins, outs, sems):
    send, recv = sems
    x, y, c = _place()
    _, far, _ = _route(x, y, c)
    cps = [pltpu.make_async_remote_copy(src_ref=ins[a], dst_ref=outs[a], send_sem=send.at[a], recv_sem=recv.at[a],
                                        device_id=(*far, c), device_id_type=MESH) for a in range(len(ins))]

    def start():
        for cp in cps:
            cp.start()

    def wait():
        for cp in cps:
            cp.wait()

    return [start, wait]


def hosted_scatter_second(parts):
    return Hosted(parts, [jax.ShapeDtypeStruct(p.shape, p.dtype) for p in parts], [pltpu.SemaphoreType.DMA((len(parts),))] * 2,
                  lambda i, o, s: list(zip((0.0, 1.0), _far_swap_plan(i, o, s))))


def gather_small(pack):
    def body(pack_ref, packs_ref, send, recv):
        x, y, c = _place()
        me = 4 * x + 2 * y + c
        flips = [(fx, fy, fc) for fx in (0, 1) for fy in (0, 1) for fc in (0, 1)][1:]
        peers = [(x ^ fx, y ^ fy, c ^ fc) for fx, fy, fc in flips]
        cps = [pltpu.make_async_remote_copy(src_ref=pack_ref, dst_ref=packs_ref.at[me], send_sem=send.at[j], recv_sem=recv.at[j],
                                            device_id=p, device_id_type=MESH) for j, p in enumerate(peers)]
        for cp in cps:
            cp.start()
        for j, (px, py, pc) in enumerate(peers):
            slab = packs_ref.at[4 * px + 2 * py + pc]
            pltpu.make_async_remote_copy(src_ref=slab, dst_ref=slab, send_sem=send.at[j], recv_sem=recv.at[j],
                                         device_id=(px, py, pc), device_id_type=MESH).wait_recv()
        for cp in cps:
            cp.wait_send()

    return _pc(body, name="gather_small", in_specs=[_ANY], out_specs=_ANY, out_shape=jax.ShapeDtypeStruct((8,) + pack.shape, pack.dtype),
               scratch_shapes=[pltpu.SemaphoreType.DMA((7,))] * 2)(pack)


def _pack(arrays, width):
    flat = jnp.concatenate([a.reshape(-1).astype(f32) for a in arrays])
    return jnp.pad(flat, (0, (-flat.shape[0]) % (8 * width))).reshape(-1, width)


def _unpack(pack, like, width):
    flat, out, at = pack.reshape(-1), [], 0
    for a in like:
        size = 1
        for s in a.shape:
            size *= s
        out.append(flat[at:at + size].reshape(a.shape))
        at += size
    return out


def _halves(a2d):
    r, c = a2d.shape
    return a2d.reshape(2, r // 2, c)


def local_step(x, meta, norm_w, w_main, w_ba, conv_w, a_log, dt_bias, pool_mix, pool_scale, dn_norm_w,
               late_weights, final_w, target, grad_hooks=None):
    seq, d = x.shape
    pw = pool_scale.shape[1]
    dw = conv_w.shape[1] // 3
    n_heads = dw // HEAD_DIM
    tp = FRONT_PAD + N_META + seq
    hp = jnp.concatenate([jnp.zeros((FRONT_PAD, d), f32), meta, x], axis=0)
    tm_big = tp // 2 if tp % 32 == 0 else tp
    tm_norm = max(t for t in range(16, min(tp, 352) + 1, 16) if tp % t == 0)
    tile = min(512, d)
    tn_proj = 2 * tile if w_main.shape[0] % (2 * tile) == 0 else tile
    off_q = 2 * pw
    off_zd = off_q + 3 * dw
    off_gp = off_zd + dw
    off_gd = off_gp + d
    a_log128 = jnp.pad(a_log, ((0, 0), (0, LANES - n_heads)))
    dt128 = jnp.pad(dt_bias, ((0, 0), (0, LANES - n_heads)))

    xn = norm_fwd(hp, norm_w, tm_norm)
    if isinstance(late_weights[0], Hosted):
        proj, *fetched = matmul("proj", xn, w_main, tb=True, tm=tp, tn=tn_proj, hosted=late_weights[0])
        w_pool_out, w_dn_out, w_o = late_weights[1](fetched)
    else:
        proj = matmul("proj", xn, w_main, tb=True, tm=tp, tn=tn_proj)
        w_pool_out, w_dn_out, w_o = late_weights
    ba = matmul("proj_ba", xn, w_ba, tb=True, tm=tp, tn=LANES)
    y_pool = pool_fwd(proj, pool_mix, pool_scale, pw)
    q, k, v, beta_b, g_b = dn_pre_fwd(proj, ba, conv_w, a_log128, dt128, n_heads, off_q // HEAD_DIM)
    assert off_zd % dw == 0
    *inter, tmats = dn_intra_fwd(q, k, v, beta_b, g_b, n_heads)
    y_dn, states = dn_seq_fwd(inter, proj, dn_norm_w, n_heads, off_zd // dw)
    a_mat = matmul("pool_out", y_pool, w_pool_out, tm=tp, tn=tile)

    def merge(acc, a_t, gp_t, gd_t):
        return acc, sigmoid(gp_t) * a_t + sigmoid(gd_t) * acc

    b_mat, merged = matmul("dn_out_merge", y_dn, w_dn_out, tm=tm_big, tn=tile, extras=[(a_mat, 0), (proj, off_gp), (proj, off_gd)],
                           epi=merge, out_dtypes=(f32, bf16))
    out = matmul("out_proj", merged, w_o, tm=tm_big, tn=tile, extras=[(hp, 0)], epi=lambda acc, h_t: (acc + h_t,))
    loss, dout, dout_b, dfinal_w = loss_stage(out, final_w, target)

    def unmerge(dm, a_t, b_t, gp_t, gd_t):
        sp, sd = sigmoid(gp_t), sigmoid(gd_t)
        return dm * sp, dm * sd, dm * a_t * sp * (1.0 - sp), dm * b_t * sd * (1.0 - sd)

    d_a, d_b, d_gp, d_gd = matmul("d_merged", dout_b, w_o, tb=True, tm=tm_big, tn=tile,
                                  extras=[(a_mat, 0), (b_mat, 0), (proj, off_gp), (proj, off_gd)], epi=unmerge,
                                  out_dtypes=(bf16,) * 4)
    g_w_o = matmul("g_w_o", merged, dout_b, ta=True, tm=tile, tn=d, out_dtypes=(bf16,))
    d_y_pool = matmul("d_y_pool", d_a, w_pool_out, tb=True, tm=tp, tn=tile)
    g_w_pool_out = matmul("g_w_pool_out", y_pool, d_a, ta=True, tm=tile, tn=d, out_dtypes=(bf16,))
    d_y_dn = matmul("d_y_dn", d_b, w_dn_out, tb=True, tm=tp, tn=tile)
    g_w_dn_out = matmul("g_w_dn_out", y_dn, d_b, ta=True, tm=tile, tn=d, out_dtypes=(bf16,))
    d_u, d_zp, g_pool_mix, g_pool_scale = pool_bwd(proj, pool_mix, pool_scale, d_y_pool, pw)
    *d_inter, d_zd, g_dn_norm_w = dn_seq_bwd(inter, proj, dn_norm_w, states, d_y_dn, n_heads, off_zd // dw)
    d_q, d_k, d_v, d_beta, d_g = dn_intra_bwd(q, k, v, beta_b, g_b, tmats, d_inter, n_heads)
    d_qr, d_kr, d_vr, d_ba, g_cq, g_ck, g_cv, g_a_log, g_dt = dn_pre_bwd(
        proj, ba, conv_w, a_log128, dt128, (d_q, d_k, d_v, d_beta, d_g), n_heads, off_q // HEAD_DIM)
    d_proj = jnp.concatenate([d_u, d_zp, d_qr, d_kr, d_vr, d_zd, d_gp, d_gd], axis=1)
    d_ba_b = cast_bf16("cast_d_ba", d_ba)
    early = grad_hooks[0](g_w_pool_out, g_w_dn_out, g_w_o, g_pool_mix) if grad_hooks else None
    res = matmul("g_w_main", d_proj, xn, ta=True, tm=tile, tn=d, out_dtypes=(bf16,), hosted=early)
    g_w_main, early_landed = (res[0], list(res[1:])) if early else (res, [])
    g_w_ba = matmul("g_w_ba", d_ba_b, xn, ta=True, tm=LANES, tn=tile, out_dtypes=(bf16,))
    hosted = grad_hooks[1](g_w_main, g_w_ba, early_landed) if grad_hooks else None
    dxn_ba = matmul("dxn_ba", d_ba_b, w_ba, tm=tp, tn=tile)
    n_cols = d_proj.shape[1]
    tk_dxn = max(t for t in range(LANES, min(2048, n_cols) + 1, LANES) if n_cols % t == 0)
    res = matmul("dxn", d_proj, w_main, tm=tp, tn=tile, tk=tk_dxn, extras=[(dxn_ba, 0)], epi=lambda acc, e: (acc + e,), hosted=hosted)
    dxn, landed = (res[0], list(res[1:])) if hosted else (res, [])
    dh, g_norm_w, last_landed = norm_bwd(hp, norm_w, dxn, dout, tm_norm, hosted=grad_hooks[2](landed) if grad_hooks else None)
    g_conv = jnp.concatenate([g_cq, g_ck, g_cv], axis=1)
    return (loss, dh, g_norm_w, g_w_main, g_w_ba, g_conv, g_a_log[:, :n_heads], g_dt[:, :n_heads], g_pool_mix, g_pool_scale,
            g_dn_norm_w, g_w_pool_out, g_w_dn_out, g_w_o, dfinal_w, last_landed)


def kernel(x, meta_tokens, norm_w, w_in, conv_w, A_log, dt_bias, pool_mix, pool_scale, dn_norm_w, w_pool_out, w_dn_out, w_o, final_norm_w, loss_target, m_meta_tokens, m_norm_w, m_w_in, m_conv_w, m_A_log, m_dt_bias, m_pool_mix, m_pool_scale, m_dn_norm_w, m_w_pool_out, m_w_dn_out, m_w_o, m_final_norm_w, v_meta_tokens, v_norm_w, v_w_in, v_conv_w, v_A_log, v_dt_bias, v_pool_mix, v_pool_scale, v_dn_norm_w, v_w_pool_out, v_w_dn_out, v_w_o, v_final_norm_w):
    d = x.shape[-1]
    pw = pool_scale.shape[-1]
    dw = w_dn_out.shape[1] * 4
    n_heads = dw // HEAD_DIM
    gdim = pw // POOL_GROUPS
    chip = 2 * lax.axis_index("x") + lax.axis_index("y")
    core = lax.axis_index("c")

    half = core.astype(jnp.int32).reshape(1)
    me = 4 * lax.axis_index("x") + 2 * lax.axis_index("y") + core

    w_in_t = w_in[0].T
    mix_s = pool_mix[0].reshape(POOL_GROUPS * (gdim // 4), gdim)
    small_s = _pack([meta_tokens, conv_w[0]], d)
    small_rows = small_s.shape[0]
    small_s = jnp.pad(small_s, ((0, (-small_rows) % 16), (0, 0)))
    sw = w_in_t.shape[0]
    n_main, n_ba = 2 * pw + 4 * dw, 2 * n_heads

    early = [cast_bf16("cast_w_in", w_in_t), _halves(cast_bf16("cast_mix", mix_s)), _halves(small_s)]
    late = [_halves(cast_bf16("cast_w_po", w_pool_out[0])), _halves(cast_bf16("cast_w_do", w_dn_out[0])),
            _halves(cast_bf16("cast_w_o", w_o[0]))]
    fill = lambda g, own: lax.dynamic_update_slice(g, own[None], (chip,) + (0,) * own.ndim)
    g_in, g_mix, g_small = [fill(g, own) for g, own in zip(gather_weights(early, ["cols", "lead", "lead"]), early)]

    def shard_rows(lo, hi):
        cut = [(max(lo, j * sw), min(hi, (j + 1) * sw), j) for j in range(4)]
        return [g_in[j, a - j * sw:b - j * sw] for a, b, j in cut if a < b]

    w_main = jnp.concatenate(shard_rows(0, n_main) + shard_rows(n_main + n_ba, 4 * sw), axis=0)
    w_ba = jnp.pad(jnp.concatenate(shard_rows(n_main, n_main + n_ba), axis=0), ((0, LANES - n_ba), (0, 0)))
    cat_cols = lambda g: jnp.concatenate([g[j].reshape(-1, g.shape[-1]) for j in range(4)], axis=1)

    def late_weights(fetched):
        g_po, g_do, g_o = [fill(g, own) for g, own in zip(fetched, late)]
        return cat_cols(g_po), g_do.reshape(-1, g_do.shape[-1]), g_o.reshape(-1, g_o.shape[-1])

    mix_full = g_mix.reshape(4, POOL_GROUPS, gdim // 4, gdim).transpose(1, 0, 2, 3).reshape(POOL_GROUPS, gdim, gdim)
    smalls = [_unpack(g_small[j].reshape(-1, d)[:small_rows], [meta_tokens, conv_w[0]], d) for j in range(4)]
    meta_full = jnp.concatenate([s[0] for s in smalls], axis=1)
    conv_full = jnp.concatenate([s[1] for s in smalls], axis=1)

    names = ["w_in", "w_po", "w_do", "w_o", "mix"]
    kinds = ["cols", "lead", "lead", "lead", "lead"]

    far_chip = 2 * (lax.axis_index("x") ^ core) + (lax.axis_index("y") ^ (1 - core))
    chips = jnp.stack([chip, far_chip]).astype(jnp.int32)
    col_parts = lambda g: g.reshape(2, g.shape[0] // 2, 4, g.shape[1] // 4).transpose(0, 2, 1, 3)
    row_parts = lambda g: g.reshape(4, 2, g.shape[0] // 8, g.shape[1]).transpose(1, 0, 2, 3)

    def pair_sums(tag, group, parts):
        group_kinds = [kinds[names.index(nm)] for nm in group]
        from_sibling = send_grad_halves("send_grad_halves_" + tag, parts, group_kinds)
        return [add_halves("add_" + nm, p, r, half, k) for nm, p, r, k in zip(group, parts, from_sibling, group_kinds)]

    later = names[1:]
    state = {}

    def later_grads(g_w_po, g_w_do, g_w_o_full, g_mix_full):
        mix_rows = POOL_GROUPS * (gdim // 4)
        mix_parts = (g_mix_full.astype(bf16).reshape(POOL_GROUPS, 4, gdim // 4, gdim).transpose(1, 0, 2, 3)
                     .reshape(4, 2, mix_rows // 2, gdim).transpose(1, 0, 2, 3))
        state["later_sums"] = pair_sums("later", later, [col_parts(g_w_po), row_parts(g_w_do), row_parts(g_w_o_full), mix_parts])
        return hosted_scatter_first(state["later_sums"])

    def input_grads(g_w_main, g_w_ba, later_from_near):
        def grad_rows(lo, hi):
            segs = [(0, n_main, g_w_main, 0), (n_main, n_main + n_ba, g_w_ba, 0), (n_main + n_ba, 4 * sw, g_w_main, n_main)]
            cut = [(max(lo, s0), min(hi, s1), s0, arr, off) for s0, s1, arr, off in segs]
            return [arr[a - s0 + off:b - s0 + off] for a, b, s0, arr, off in cut if a < b]

        kept, passed = zip(*[add_first("add1_" + nm, p, g, chips) for nm, p, g in zip(later, state["later_sums"], later_from_near)])
        state["later_kept"] = list(kept)
        in_parts = jnp.stack([jnp.concatenate(grad_rows(j * sw, (j + 1) * sw), axis=0) for j in range(4)])
        state["in_sums"] = pair_sums("input", names[:1], [in_parts])
        return hosted_scatter_first(state["in_sums"]) + hosted_scatter_second(list(passed))

    def input_second(landed):
        state["later_from_far"] = landed[1:]
        state["in_kept"], in_passed = add_first("add1_w_in", state["in_sums"][0], landed[0], chips)
        return hosted_scatter_second([in_passed])

    (loss, dh, g_norm_w, _, _, g_conv, g_a_log, g_dt, _, g_pool_scale, g_dn_norm_w, _, _, _, g_final_w,
     (in_from_far,)) = local_step(x[0], meta_full, norm_w, w_main, w_ba, conv_full, A_log, dt_bias, mix_full, pool_scale,
                                  dn_norm_w, (hosted_gather(late, ["lead"] * 3), late_weights), final_norm_w.reshape(1, d),
                                  loss_target[0], grad_hooks=(later_grads, input_grads, input_second))
    grad_x = dh[FRONT_PAD + N_META:][None]
    g_meta = dh[FRONT_PAD:FRONT_PAD + N_META]

    mine = [add_second("add2_" + nm, k_, g) for nm, k_, g in
            zip(names, [state["in_kept"]] + state["later_kept"], [in_from_far] + list(state["later_from_far"]))]
    small_like = [loss, g_norm_w, g_a_log, g_dt, g_pool_scale, g_dn_norm_w, g_final_w, g_conv, g_meta]
    pack = _pack(small_like, d)
    packs = gather_small(pack)
    theirs = swap_with_sibling("swap_grad_halves", mine)
    total = sum_leading("sum_small", lax.dynamic_update_slice(packs, pack[None], (me, 0, 0)))
    (loss_t, g_norm_w, g_a_log, g_dt, g_pool_scale, g_dn_norm_w, g_final_w, g_conv, g_meta) = _unpack(total, small_like, d)
    loss_out = loss_t[0, 0]
    g_conv_s = lax.dynamic_slice_in_dim(g_conv, chip * (g_conv.shape[1] // 4), g_conv.shape[1] // 4, axis=1)
    g_meta_s = lax.dynamic_slice_in_dim(g_meta, chip * (d // 4), d // 4, axis=1)

    weights = [meta_tokens, norm_w, w_in, conv_w, A_log, dt_bias, pool_mix, pool_scale, dn_norm_w, w_pool_out, w_dn_out, w_o, final_norm_w]
    ms = [m_meta_tokens, m_norm_w, m_w_in, m_conv_w, m_A_log, m_dt_bias, m_pool_mix, m_pool_scale, m_dn_norm_w, m_w_pool_out, m_w_dn_out, m_w_o, m_final_norm_w]
    vs = [v_meta_tokens, v_norm_w, v_w_in, v_conv_w, v_A_log, v_dt_bias, v_pool_mix, v_pool_scale, v_dn_norm_w, v_w_pool_out, v_w_dn_out, v_w_o, v_final_norm_w]
    grads = [g_meta_s, g_norm_w, None, g_conv_s[None], g_a_log, g_dt, None, g_pool_scale, g_dn_norm_w, None, None, None, g_final_w.reshape(d)]
    deltas, new_ms, new_vs = [None] * 13, [None] * 13, [None] * 13
    big = [2, 9, 10, 11, 6]
    for i, nm, g_mine, g_theirs in zip(big, names, mine, theirs):
        if nm == "w_in":
            to2d, back, axis = (lambda t: t[0].T), (lambda t: t.T[None]), 1
        else:
            to2d, back, axis = (lambda t: t.reshape(-1, t.shape[-1])), (lambda t, i=i: t.reshape(weights[i].shape)), 0
        res = adamw_joined("adamw_" + nm, to2d(weights[i]), g_mine, g_theirs, to2d(ms[i]), to2d(vs[i]), half, axis)
        grads[i], deltas[i], new_ms[i], new_vs[i] = [back(t) for t in res]
    small_idx = [i for i in range(13) if i not in big]
    packs = [_pack([arrs[i] for i in small_idx], d) for arrs in (weights, grads, ms, vs)]
    outs = adamw("adamw_small", *packs)
    like = [weights[i] for i in small_idx]
    for res, dest in zip(outs, (deltas, new_ms, new_vs)):
        for i, val in zip(small_idx, _unpack(res, like, d)):
            dest[i] = val
    return (loss_out, grad_x, *grads, *deltas, *new_ms, *new_vs)
```

```python
import functools

import jax
import jax.numpy as jnp
from jax import lax
from jax.experimental import pallas as pl
from jax.experimental.pallas import tpu as pltpu

f32 = jnp.float32
bf16 = jnp.bfloat16
MESH = pl.DeviceIdType.MESH

N_META = 16
CHUNK = 64
FRONT_PAD = (-N_META) % CHUNK
HEAD_DIM = 128
POOL_GROUPS = 4
POOL_WINDOWS = (2, 4, 8, 16)
CONV_WIDTH = 4
NORM_EPS = 1e-6
ADAM_LR, ADAM_B1, ADAM_B2, ADAM_EPS, ADAM_WD, ADAM_STEP = 0.001, 0.9, 0.999, 1e-08, 0.01, 10
LANES = 128
V7X_VMEM_BYTES = 64 * 2**20
VMEM_LIMIT = V7X_VMEM_BYTES - 8 * 2**20


def _pc(body, **kw):
    return pl.pallas_call(body, **kw)


def _params(*sem, **kw):
    return pltpu.CompilerParams(dimension_semantics=sem or None, vmem_limit_bytes=VMEM_LIMIT, **kw)


def _dg(a, b, dims):
    return lax.dot_general(a, b, (dims, ((), ())), preferred_element_type=f32)


@jax.custom_vjp
def mm_nn(a, b):
    return _dg(a.astype(bf16), b.astype(bf16), ((1,), (0,)))


@jax.custom_vjp
def mm_nt(a, b):
    return _dg(a.astype(bf16), b.astype(bf16), ((1,), (1,)))


@jax.custom_vjp
def mm_tn(a, b):
    return _dg(a.astype(bf16).T, b.astype(bf16), ((1,), (0,)))


mm_nn.defvjp(lambda a, b: (mm_nn(a, b), (a, b)), lambda r, dy: (mm_nt(dy, r[1]), mm_tn(r[0], dy)))
mm_nt.defvjp(lambda a, b: (mm_nt(a, b), (a, b)), lambda r, dy: (mm_nn(dy, r[1]), mm_tn(dy, r[0])))
mm_tn.defvjp(lambda a, b: (mm_tn(a, b), (a, b)), lambda r, dy: (mm_nt(r[1], dy), mm_nn(r[0], dy)))


def _split3(x):
    hi = x.astype(bf16)
    r1 = x - hi.astype(f32)
    mid = r1.astype(bf16)
    lo = (r1 - mid.astype(f32)).astype(bf16)
    return hi, mid, lo


def _split2(x):
    hi = x.astype(bf16)
    return hi, (x - hi.astype(f32)).astype(bf16)


@jax.custom_vjp
def mm_sel(sel, x):
    s = sel.astype(bf16)
    d = ((1,), (0,))
    hi, lo = _split2(x)
    return _dg(s, hi, d) + _dg(s, lo, d)


def _mm_sel_bwd(sel, dy):
    s = sel.astype(bf16)
    d = ((0,), (0,))
    hi, lo = _split2(dy)
    return jnp.zeros_like(sel), _dg(s, hi, d) + _dg(s, lo, d)


mm_sel.defvjp(lambda sel, x: (mm_sel(sel, x), sel), _mm_sel_bwd)


@jax.custom_vjp
def mm_pick(x, sel):
    s = sel.astype(bf16)
    d = ((1,), (0,))
    hi, mid, lo = _split3(x)
    return _dg(hi, s, d) + _dg(mid, s, d) + _dg(lo, s, d)


def _mm_pick_bwd(sel, dy):
    s = sel.astype(bf16)
    d = ((1,), (1,))
    hi, mid, lo = _split3(dy)
    return _dg(hi, s, d) + _dg(mid, s, d) + _dg(lo, s, d), jnp.zeros_like(sel)


mm_pick.defvjp(lambda x, sel: (mm_pick(x, sel), sel), _mm_pick_bwd)


def tri_inv(ls):
    n = ls[0].shape[0]
    eye = (lax.broadcasted_iota(jnp.int32, (n, n), 0) == lax.broadcasted_iota(jnp.int32, (n, n), 1)).astype(f32)
    ms = [-l for l in ls]
    ts = [eye + m for m in ms]
    k = 1
    while 2 * k < CHUNK:
        ms = [mm_nn(m, m) for m in ms]
        ts = [t + mm_nn(t, m) for t, m in zip(ts, ms)]
        k *= 2
    return ts


@functools.partial(jax.custom_vjp, nondiff_argnums=(1,))
def shift_rows(x, j):
    n = x.shape[0]
    rows = lax.broadcasted_iota(jnp.int32, x.shape, 0)
    if j >= 0:
        return jnp.where(rows >= j, pltpu.roll(x, j, 0), 0.0)
    return jnp.where(rows < n + j, pltpu.roll(x, n + j, 0), 0.0)


shift_rows.defvjp(lambda x, j: (shift_rows(x, j), None), lambda j, _, dy: (shift_rows(dy, -j),))


def sigmoid(x):
    return 1.0 / (1.0 + jnp.exp(-x))


def silu(x):
    return x * sigmoid(x)


def softplus(x):
    return jnp.maximum(x, 0.0) + jnp.log(1.0 + jnp.exp(-jnp.abs(x)))


def rmsnorm(x, w):
    return x * lax.rsqrt(jnp.mean(x * x, axis=-1, keepdims=True) + NORM_EPS) * w


def l2norm(x):
    return x * lax.rsqrt(jnp.sum(x * x, axis=-1, keepdims=True) + NORM_EPS)


def pool_fn(u, zp, mix, scale, group):
    rows = lax.broadcasted_iota(jnp.int32, u.shape, 0)
    sums = []
    s, w = u, 1
    while w < POOL_WINDOWS[-1]:
        s = s + shift_rows(s, w)
        w *= 2
        sums.append(s)
    total = sums[-1]
    for gi in range(POOL_GROUPS - 2, -1, -1):
        total = jnp.where(group == gi, sums[gi], total)
    window = jnp.left_shift(2, group)
    cnt = jnp.clip(rows - (FRONT_PAD - 1), 1, window).astype(f32)
    pooled = total / cnt - u
    return mm_nn(pooled, mix) * scale * silu(zp)


PRE_HALO = 8


def conv_silu(x, w):
    k = CONV_WIDTH
    y = x * w[k - 1:k, :]
    for kk in range(k - 1):
        y = y + shift_rows(x, k - 1 - kk) * w[kk:kk + 1, :]
    return silu(y[PRE_HALO:])


def _lane_pick(row, idx):
    lanes = lax.broadcasted_iota(jnp.int32, row.shape, 1)
    return jnp.sum(jnp.where(lanes == idx, row, 0.0), axis=1, keepdims=True)


def dn_pre_fn(qr, kr, vr, ba, cwq, cwk, cwv, a_log, dt_bias, head, n_heads, row0):
    q = l2norm(conv_silu(qr, cwq)) * (HEAD_DIM ** -0.5)
    k = l2norm(conv_silu(kr, cwk))
    v = conv_silu(vr, cwv)
    r = lax.broadcasted_iota(jnp.int32, (LANES, LANES), 0)
    b_b = mm_pick(ba, (r == head).astype(f32))
    a_b = mm_pick(ba, (r == head + n_heads).astype(f32))
    real = lax.broadcasted_iota(jnp.int32, ba.shape, 0) + row0 >= FRONT_PAD
    beta_b = jnp.where(real, sigmoid(b_b), 0.0)
    g_b = jnp.where(real, -jnp.exp(_lane_pick(a_log, head)) * softplus(a_b + _lane_pick(dt_bias, head)), 0.0)
    return q, k, v, beta_b, g_b


def _chunk_masks(rows):
    r = lax.broadcasted_iota(jnp.int32, (rows, rows), 0)
    c = lax.broadcasted_iota(jnp.int32, (rows, rows), 1)
    same = (r // CHUNK) == (c // CHUNK)
    return same, jnp.logical_and(same, r >= c), jnp.logical_and(same, r > c)


def _lane0(rows):
    return (lax.broadcasted_iota(jnp.int32, (rows, LANES), 1) == 0).astype(bf16)


@jax.custom_vjp
def lane0_as_row(x):
    sel = _lane0(x.shape[0])
    d = ((1,), (1,))
    hi, lo = _split2(x)
    return _dg(sel, hi, d) + _dg(sel, lo, d)


def _lane0_as_row_bwd(rows, dy):
    sel = _lane0(rows)
    d = ((0,), (0,))
    hi, lo = _split2(dy)
    return (_dg(hi, sel, d) + _dg(lo, sel, d),)


lane0_as_row.defvjp(lambda x: (lane0_as_row(x), x.shape[0]), _lane0_as_row_bwd)


def gate_fn(g_b):
    rows = g_b.shape[0]
    same, causal, _ = _chunk_masks(rows)
    gcum_b = mm_sel(causal.astype(f32), g_b)
    glast_b = mm_sel(same.astype(f32), g_b)
    g_rows = jnp.broadcast_to(gcum_b[:, :1], (rows, rows))
    decay = jnp.where(causal, jnp.exp(jnp.where(causal, g_rows - lane0_as_row(gcum_b), 0.0)), 0.0)
    return decay, jnp.exp(gcum_b), jnp.exp(glast_b - gcum_b), jnp.exp(glast_b)


def _fold_matrix(rows):
    r = lax.broadcasted_iota(jnp.int32, (rows, LANES), 0)
    c = lax.broadcasted_iota(jnp.int32, (rows, LANES), 1)
    return (r % CHUNK == c).astype(bf16)


@jax.custom_vjp
def fold_chunks(x):
    return _dg(x.astype(bf16), _fold_matrix(x.shape[0]), ((1,), (0,)))


def _fold_chunks_bwd(rows, dy):
    fold = _fold_matrix(rows)
    d = ((1,), (1,))
    hi, mid, lo = _split3(dy)
    return (_dg(hi, fold, d) + _dg(mid, fold, d) + _dg(lo, fold, d),)


fold_chunks.defvjp(lambda x: (fold_chunks(x), x.shape[0]), _fold_chunks_bwd)


def lmat_fn(k, beta_b, decay):
    _, _, strict = _chunk_masks(k.shape[0])
    return jnp.where(strict, mm_nt(k * beta_b, k) * decay, 0.0)


def intra_fn(tmat, q, k, v, beta_b, decay, eg, kfac):
    _, causal, _ = _chunk_masks(q.shape[0])
    k_beta = k * beta_b
    u_c = mm_nn(tmat, v * beta_b)
    w_c = mm_nn(tmat, k_beta * eg)
    qk = jnp.where(causal, mm_nt(q, k) * decay, 0.0)
    return u_c, w_c, q * eg, k * kfac, fold_chunks(qk)


def gated_norm(o, norm_w, zd):
    return rmsnorm(o, norm_w) * silu(zd)


def loss_fn(o, w, tgt):
    err = rmsnorm(o, w) - tgt
    return 0.5 * jnp.sum(jnp.mean(err * err, axis=-1))


_ANY = pl.BlockSpec(memory_space=pl.ANY)


class Hosted:
    def __init__(self, arrays, out_shapes, sems, stages):
        self.arrays, self.out_shapes, self.sems, self.stages = list(arrays), list(out_shapes), list(sems), stages

    def __add__(self, other):
        ni, no, ns = len(self.arrays), len(self.out_shapes), len(self.sems)
        stages = lambda i, o, s: self.stages(i[:ni], o[:no], s[:ns]) + other.stages(i[ni:], o[no:], s[ns:])
        return Hosted(self.arrays + other.arrays, self.out_shapes + other.out_shapes, self.sems + other.sems, stages)


def matmul(name, a, b, *, ta=False, tb=False, tm, tn, tk=None, extras=(), epi=None, out_dtypes=(f32,), hosted=None):
    m, k = (a.shape[1], a.shape[0]) if ta else a.shape
    n = b.shape[0] if tb else b.shape[1]
    tm, tn, tk = min(tm, m), min(tn, n), min(tk or k, k)
    assert m % tm == 0 and n % tn == 0 and k % tk == 0, (name, m, n, k, tm, tn, tk)
    nm, nn, nk = m // tm, n // tn, k // tk
    a_spec = pl.BlockSpec((tk, tm), lambda i, j, kk: (kk, i)) if ta else pl.BlockSpec((tm, tk), lambda i, j, kk: (i, kk))
    b_spec = pl.BlockSpec((tn, tk), lambda i, j, kk: (j, kk)) if tb else pl.BlockSpec((tk, tn), lambda i, j, kk: (kk, j))
    ex_specs = []
    for _, off in extras:
        assert off % tn == 0, (name, off, tn)
        ex_specs.append(pl.BlockSpec((tm, tn), functools.partial(lambda i, j, kk, o: (i, o + j), o=off // tn)))
    n_ex, n_out = len(extras), len(out_dtypes)
    dims = ((0 if ta else 1,), (1 if tb else 0,))
    n_hin = len(hosted.arrays) if hosted else 0
    n_hout = len(hosted.out_shapes) if hosted else 0
    n_sem = len(hosted.sems) if hosted else 0

    def body(a_ref, b_ref, *rest):
        ex_refs, rest = rest[:n_ex], rest[n_ex:]
        hin_refs, rest = rest[:n_hin], rest[n_hin:]
        out_refs, rest = rest[:n_out], rest[n_out:]
        hout_refs, rest = rest[:n_hout], rest[n_hout:]
        sem_refs = rest[len(rest) - n_sem:] if n_sem else ()
        step = (pl.program_id(0) * nn + pl.program_id(1)) * nk + pl.program_id(2)
        stages = hosted.stages(hin_refs, hout_refs, sem_refs) if hosted else []
        for frac, emit in stages:
            if frac < 1.0:
                pl.when(step == int(frac * (nm * nn * nk - 1)))(emit)

        def finish(acc):
            res = epi(acc, *[r[...] for r in ex_refs]) if epi is not None else (acc,)
            for o_ref, val in zip(out_refs, res):
                o_ref[...] = val.astype(o_ref.dtype)

        p = _dg(a_ref[...], b_ref[...], dims)
        if nk == 1:
            finish(p)
        else:
            acc_ref = rest[0]
            kk = pl.program_id(2)

            @pl.when(kk == 0)
            def _():
                acc_ref[...] = p

            @pl.when(kk > 0)
            def _():
                acc_ref[...] += p

            @pl.when(kk == nk - 1)
            def _():
                finish(acc_ref[...])

        for frac, emit in stages:
            if frac >= 1.0:
                pl.when(step == nm * nn * nk - 1)(emit)

    outs = _pc(
        body, name=name, grid=(nm, nn, nk),
        in_specs=[a_spec, b_spec] + ex_specs + [_ANY] * n_hin,
        out_specs=[pl.BlockSpec((tm, tn), lambda i, j, kk: (i, j))] * n_out + [_ANY] * n_hout,
        out_shape=[jax.ShapeDtypeStruct((m, n), dt) for dt in out_dtypes] + (hosted.out_shapes if hosted else []),
        scratch_shapes=([pltpu.VMEM((tm, tn), f32)] if nk > 1 else []) + (hosted.sems if hosted else []),
        compiler_params=_params(*(("arbitrary",) * 3 if hosted else ("parallel", "parallel", "arbitrary"))),
    )(a, b, *[e for e, _ in extras], *(hosted.arrays if hosted else []))
    return outs[0] if len(outs) == 1 else outs


def _row_tile(rows, cols, n_arrays, itemsize=4, budget=24 * 2**20):
    best = None
    for t in range(16, rows + 1, 16):
        if rows % t == 0 and 2 * n_arrays * t * cols * itemsize <= budget:
            best = t
    return best or rows


def _tile(rows, cols, n_arrays, budget=24 * 2**20):
    if rows % 16 == 0 or cols % LANES != 0:
        return _row_tile(rows, cols, n_arrays, budget=budget), cols
    fits = [t for t in range(LANES, cols + 1, LANES) if cols % t == 0 and 2 * n_arrays * rows * t * 4 <= budget]
    return rows, (max(fits) if fits else LANES)


def cast_bf16(name, x):
    rows, cols = x.shape
    tr, tc = _tile(rows, cols, 2)

    def body(x_ref, o_ref):
        o_ref[...] = x_ref[...].astype(bf16)

    blk = pl.BlockSpec((tr, tc), lambda i, j: (i, j))
    return _pc(body, name=name, grid=(rows // tr, cols // tc), in_specs=[blk], out_specs=blk,
               out_shape=jax.ShapeDtypeStruct(x.shape, bf16), compiler_params=_params("parallel", "parallel"))(x)


def norm_fwd(hp, norm_w, tm):
    tp, d = hp.shape

    def body(h_ref, w_ref, o_ref):
        o_ref[...] = rmsnorm(h_ref[...], w_ref[...]).astype(bf16)

    return _pc(body, name="norm_fwd", grid=(tp // tm,),
               in_specs=[pl.BlockSpec((tm, d), lambda i: (i, 0)), pl.BlockSpec((1, d), lambda i: (0, 0))],
               out_specs=pl.BlockSpec((tm, d), lambda i: (i, 0)), out_shape=jax.ShapeDtypeStruct((tp, d), bf16),
               compiler_params=_params("parallel"))(hp, norm_w)


def norm_bwd(hp, norm_w, dxn, dout, tm, hosted=None):
    tp, d = hp.shape
    steps = tp // tm
    n_hin = len(hosted.arrays) if hosted else 0
    n_hout = len(hosted.out_shapes) if hosted else 0

    def body(h_ref, w_ref, dxn_ref, dout_ref, *rest):
        hin_refs, (dh_ref, dw_ref), rest = rest[:n_hin], rest[n_hin:n_hin + 2], rest[n_hin + 2:]
        stages = hosted.stages(hin_refs, rest[:n_hout], rest[n_hout:]) if hosted else []
        for frac, emit in stages:
            if frac < 1.0:
                pl.when(pl.program_id(0) == int(frac * (steps - 1)))(emit)
        _, vjp = jax.vjp(rmsnorm, h_ref[...], w_ref[...])
        dh, dw = vjp(dxn_ref[...])
        dh_ref[...] = dh + dout_ref[...]

        @pl.when(pl.program_id(0) == 0)
        def _():
            dw_ref[...] = jnp.zeros_like(dw_ref)

        dw_ref[...] += dw
        for frac, emit in stages:
            if frac >= 1.0:
                pl.when(pl.program_id(0) == steps - 1)(emit)

    row = pl.BlockSpec((tm, d), lambda i: (i, 0))
    vec = pl.BlockSpec((1, d), lambda i: (0, 0))
    outs = _pc(body, name="norm_bwd", grid=(steps,), in_specs=[row, vec, row, row] + [_ANY] * n_hin,
               out_specs=[row, vec] + [_ANY] * n_hout,
               out_shape=[jax.ShapeDtypeStruct((tp, d), f32), jax.ShapeDtypeStruct((1, d), f32)] + (hosted.out_shapes if hosted else []),
               scratch_shapes=hosted.sems if hosted else [],
               compiler_params=_params("arbitrary"))(hp, norm_w, dxn, dout, *(hosted.arrays if hosted else []))
    return outs[0], outs[1], list(outs[2:])


def pool_fwd(proj, mix, scale, pw):
    tp = proj.shape[0]
    g = pw // POOL_GROUPS

    def body(u_ref, z_ref, mix_ref, s_ref, y_ref):
        y_ref[...] = pool_fn(u_ref[...], z_ref[...], mix_ref[0], s_ref[...], pl.program_id(0)).astype(bf16)

    return _pc(body, name="pool_fwd", grid=(POOL_GROUPS,),
               in_specs=[pl.BlockSpec((tp, g), lambda i: (0, i)), pl.BlockSpec((tp, g), lambda i: (0, POOL_GROUPS + i)),
                         pl.BlockSpec((1, g, g), lambda i: (i, 0, 0)), pl.BlockSpec((1, g), lambda i: (0, i))],
               out_specs=pl.BlockSpec((tp, g), lambda i: (0, i)), out_shape=jax.ShapeDtypeStruct((tp, pw), bf16),
               compiler_params=_params("parallel"))(proj, proj, mix, scale)


def pool_bwd(proj, mix, scale, dy, pw):
    tp = proj.shape[0]
    g = pw // POOL_GROUPS

    def body(u_ref, z_ref, mix_ref, s_ref, dy_ref, du_ref, dz_ref, dmix_ref, ds_ref):
        grp = pl.program_id(0)
        _, vjp = jax.vjp(lambda u, z, m, s: pool_fn(u, z, m, s, grp), u_ref[...], z_ref[...], mix_ref[0].astype(f32), s_ref[...])
        du, dz, dmix, ds = vjp(dy_ref[...])
        du_ref[...] = du.astype(bf16)
        dz_ref[...] = dz.astype(bf16)
        dmix_ref[0] = dmix
        ds_ref[...] = ds

    col = pl.BlockSpec((tp, g), lambda i: (0, i))
    return _pc(body, name="pool_bwd", grid=(POOL_GROUPS,),
               in_specs=[col, pl.BlockSpec((tp, g), lambda i: (0, POOL_GROUPS + i)),
                         pl.BlockSpec((1, g, g), lambda i: (i, 0, 0)), pl.BlockSpec((1, g), lambda i: (0, i)), col],
               out_specs=[col, col, pl.BlockSpec((1, g, g), lambda i: (i, 0, 0)), pl.BlockSpec((1, g), lambda i: (0, i))],
               out_shape=[jax.ShapeDtypeStruct((tp, pw), bf16), jax.ShapeDtypeStruct((tp, pw), bf16),
                          jax.ShapeDtypeStruct((POOL_GROUPS, g, g), f32), jax.ShapeDtypeStruct((1, pw), f32)],
               compiler_params=_params("parallel"))(proj, proj, mix, scale, dy)


def _dn_pre_specs(tp, n_heads, q_off):
    hb = lambda off: pl.BlockSpec((tp, HEAD_DIM), functools.partial(lambda h, o: (0, o + h), o=off))
    cw = lambda off: pl.BlockSpec((CONV_WIDTH, HEAD_DIM), functools.partial(lambda h, o: (0, o + h), o=off))
    whole = lambda shape: pl.BlockSpec(shape, lambda h: (0, 0))
    return ([hb(q_off), hb(q_off + n_heads), hb(q_off + 2 * n_heads), whole((tp, LANES)),
             cw(0), cw(n_heads), cw(2 * n_heads), whole((1, LANES)), whole((1, LANES))], hb, cw, whole)


def _pre_rows(tp):
    return max(t for t in range(16, min(tp, 192) + 1, 16) if tp % t == 0)


def _with_history(ref, r0, rows):
    if r0 == 0:
        return jnp.concatenate([jnp.zeros((PRE_HALO, ref.shape[1]), f32), ref[0:rows, :]], axis=0)
    return ref[r0 - PRE_HALO:r0 + rows, :]


def dn_pre_fwd(proj, ba, conv_w, a_log, dt_bias, n_heads, q_off):
    tp = proj.shape[0]
    rows = _pre_rows(tp)
    in_specs, hb, _, _ = _dn_pre_specs(tp, n_heads, q_off)

    def body(q_ref, k_ref, v_ref, ba_ref, cq_ref, ck_ref, cv_ref, al_ref, dt_ref, *out_refs):
        for r0 in range(0, tp, rows):
            outs = dn_pre_fn(_with_history(q_ref, r0, rows), _with_history(k_ref, r0, rows), _with_history(v_ref, r0, rows),
                             ba_ref[r0:r0 + rows, :], cq_ref[...], ck_ref[...], cv_ref[...], al_ref[...], dt_ref[...],
                             pl.program_id(0), n_heads, r0)
            for o_ref, val in zip(out_refs, outs):
                o_ref[r0:r0 + rows, :] = val

    return _pc(body, name="dn_pre_fwd", grid=(n_heads,), in_specs=in_specs, out_specs=[hb(0)] * 5,
               out_shape=[jax.ShapeDtypeStruct((tp, n_heads * HEAD_DIM), f32)] * 5,
               compiler_params=_params("parallel"))(proj, proj, proj, ba, conv_w, conv_w, conv_w, a_log, dt_bias)


def dn_pre_bwd(proj, ba, conv_w, a_log, dt_bias, cots, n_heads, q_off):
    tp = proj.shape[0]
    rows = _pre_rows(tp)
    in_specs, hb, cw, whole = _dn_pre_specs(tp, n_heads, q_off)

    def body(q_ref, k_ref, v_ref, ba_ref, cq_ref, ck_ref, cv_ref, al_ref, dt_ref, dq_ref, dk_ref, dv_ref, db_ref, dg_ref,
             dqr_ref, dkr_ref, dvr_ref, dba_ref, dcq_ref, dck_ref, dcv_ref, dal_ref, ddt_ref):
        head = pl.program_id(0)

        @pl.when(head == 0)
        def _():
            dba_ref[...] = jnp.zeros_like(dba_ref)
            dal_ref[...] = jnp.zeros_like(dal_ref)
            ddt_ref[...] = jnp.zeros_like(ddt_ref)

        owed = [jnp.zeros((PRE_HALO, HEAD_DIM), f32)] * 3
        d_conv = [jnp.zeros((CONV_WIDTH, HEAD_DIM), f32)] * 3
        d_al, d_dt = jnp.zeros((1, LANES), f32), jnp.zeros((1, LANES), f32)
        for r0 in reversed(range(0, tp, rows)):
            fn = lambda *args, r0=r0: dn_pre_fn(*args, head, n_heads, r0)
            _, vjp = jax.vjp(fn, _with_history(q_ref, r0, rows), _with_history(k_ref, r0, rows), _with_history(v_ref, r0, rows),
                             ba_ref[r0:r0 + rows, :], cq_ref[...], ck_ref[...], cv_ref[...], al_ref[...], dt_ref[...])
            tile = slice(r0, r0 + rows)
            *d_raw, dba, dcq, dck, dcv, dal, ddt = vjp((dq_ref[tile, :], dk_ref[tile, :], dv_ref[tile, :], db_ref[tile, :], dg_ref[tile, :]))
            for i, (out_ref, d) in enumerate(zip((dqr_ref, dkr_ref, dvr_ref), d_raw)):
                out_ref[tile, :] = jnp.concatenate([d[PRE_HALO:rows], d[rows:] + owed[i]], axis=0).astype(bf16)
                owed[i] = d[:PRE_HALO]
            dba_ref[tile, :] += dba
            d_conv = [acc + d for acc, d in zip(d_conv, (dcq, dck, dcv))]
            d_al, d_dt = d_al + dal, d_dt + ddt
        dcq_ref[...], dck_ref[...], dcv_ref[...] = d_conv
        dal_ref[...] += d_al
        ddt_ref[...] += d_dt

    w = n_heads * HEAD_DIM
    return _pc(body, name="dn_pre_bwd", grid=(n_heads,), in_specs=in_specs + [hb(0)] * 5,
               out_specs=[hb(0)] * 3 + [whole((tp, LANES)), cw(0), cw(0), cw(0), whole((1, LANES)), whole((1, LANES))],
               out_shape=[jax.ShapeDtypeStruct((tp, w), bf16)] * 3 + [jax.ShapeDtypeStruct((tp, LANES), f32)]
               + [jax.ShapeDtypeStruct((CONV_WIDTH, w), f32)] * 3 + [jax.ShapeDtypeStruct((1, LANES), f32)] * 2,
               compiler_params=_params("arbitrary"))(proj, proj, proj, ba, conv_w, conv_w, conv_w, a_log, dt_bias, *cots)


def _super_rows(tp):
    n = tp // CHUNK
    return CHUNK * max(j for j in (4, 3, 2, 1) if n % j == 0)


def _heads_per_step(n_heads):
    return max(j for j in (4, 2, 1) if n_heads % j == 0)


def dn_intra_fwd(q, k, v, beta_b, g_b, n_heads):
    tp = q.shape[0]
    rows = _super_rows(tp)
    ns = tp // rows

    hps = _heads_per_step(n_heads)

    def body(q_ref, k_ref, v_ref, b_ref, g_ref, u_ref, w_ref, qd_ref, kd_ref, qk_ref, gl_ref, t_ref):
        lanes = [slice(i * HEAD_DIM, (i + 1) * HEAD_DIM) for i in range(hps)]
        gates = [gate_fn(g_ref[:, sl]) for sl in lanes]
        tmats = tri_inv([lmat_fn(k_ref[:, sl], b_ref[:, sl], gt[0]) for sl, gt in zip(lanes, gates)])
        for i, (sl, (decay, eg, kfac, gl), tmat) in enumerate(zip(lanes, gates, tmats)):
            u_c, w_c, q_dec, k_dec, qk_c = intra_fn(tmat, q_ref[:, sl], k_ref[:, sl], v_ref[:, sl], b_ref[:, sl], decay, eg, kfac)
            u_ref[:, sl] = u_c
            w_ref[:, sl] = w_c
            qd_ref[:, sl] = q_dec
            kd_ref[:, sl] = k_dec
            qk_ref[:, sl] = qk_c
            gl_ref[:, sl] = gl
            t_ref[i, 0] = tmat

    blk = pl.BlockSpec((rows, hps * HEAD_DIM), lambda h, s: (s, h))
    return _pc(body, name="dn_intra_fwd", grid=(n_heads // hps, ns), in_specs=[blk] * 5,
               out_specs=[blk] * 6 + [pl.BlockSpec((hps, 1, rows, rows), lambda h, s: (h, s, 0, 0))],
               out_shape=[jax.ShapeDtypeStruct(q.shape, f32)] * 6 + [jax.ShapeDtypeStruct((n_heads, ns, rows, rows), f32)],
               compiler_params=_params("parallel", "parallel"))(q, k, v, beta_b, g_b)


def dn_intra_bwd(q, k, v, beta_b, g_b, tmats, cots, n_heads):
    tp = q.shape[0]
    rows = _super_rows(tp)
    ns = tp // rows

    hps = _heads_per_step(n_heads)

    def body(q_ref, k_ref, v_ref, b_ref, g_ref, t_ref, du_ref, dw_ref, dqd_ref, dkd_ref, dqk_ref, dgl_ref,
             dq_ref, dk_ref, dv_ref, db_ref, dg_ref):
        lanes = [slice(i * HEAD_DIM, (i + 1) * HEAD_DIM) for i in range(hps)]
        tmats = [t_ref[i, 0] for i in range(hps)]
        gates = [jax.vjp(gate_fn, g_ref[:, sl]) for sl in lanes]
        intra = [jax.vjp(intra_fn, tmat, q_ref[:, sl], k_ref[:, sl], v_ref[:, sl], b_ref[:, sl], gt[0][0], gt[0][1], gt[0][2])[1](
            (du_ref[:, sl], dw_ref[:, sl], dqd_ref[:, sl], dkd_ref[:, sl], dqk_ref[:, sl]))
            for sl, tmat, gt in zip(lanes, tmats, gates)]
        tts = [tmat.T for tmat in tmats]
        dls = [mm_nn(tt, res[0]) for tt, res in zip(tts, intra)]
        dls = [-mm_nn(dl, tt) for dl, tt in zip(dls, tts)]
        for sl, gt, res, dl in zip(lanes, gates, intra, dls):
            _, dq, dk, dv, db, ddecay, deg, dkfac = res
            dk2, db2, ddecay2 = jax.vjp(lmat_fn, k_ref[:, sl], b_ref[:, sl], gt[0][0])[1](dl)
            (dg,) = gt[1]((ddecay + ddecay2, deg, dkfac, dgl_ref[:, sl]))
            dq_ref[:, sl] = dq
            dk_ref[:, sl] = dk + dk2
            dv_ref[:, sl] = dv
            db_ref[:, sl] = db + db2
            dg_ref[:, sl] = dg

    blk = pl.BlockSpec((rows, hps * HEAD_DIM), lambda h, s: (s, h))
    return _pc(body, name="dn_intra_bwd", grid=(n_heads // hps, ns),
               in_specs=[blk] * 5 + [pl.BlockSpec((hps, 1, rows, rows), lambda h, s: (h, s, 0, 0))] + [blk] * 6,
               out_specs=[blk] * 5, out_shape=[jax.ShapeDtypeStruct(q.shape, f32)] * 5,
               compiler_params=_params("parallel", "parallel"))(q, k, v, beta_b, g_b, tmats, *cots)


def dn_seq_fwd(inter, proj, dn_norm_w, n_heads, zd_off):
    tp, width = inter[0].shape
    n_chunks = tp // CHUNK

    def body(u_ref, w_ref, qd_ref, kd_ref, qk_ref, gl_ref, z_ref, nw_ref, y_ref, s_ref, state):
        @pl.when(pl.program_id(0) == 0)
        def _():
            state[...] = jnp.zeros_like(state)

        lanes = [slice(h * HEAD_DIM, (h + 1) * HEAD_DIM) for h in range(n_heads)]
        sts = [state[h] for h in range(n_heads)]
        for h, st in enumerate(sts):
            s_ref[0, h] = st
        v_new = [u_ref[:, sl] - mm_nn(w_ref[:, sl], st) for sl, st in zip(lanes, sts)]
        outs = [mm_nn(qd_ref[:, sl], st) + mm_nn(qk_ref[:, sl][:, :CHUNK], vn) for sl, st, vn in zip(lanes, sts, v_new)]
        for h, (sl, st, vn) in enumerate(zip(lanes, sts, v_new)):
            state[h] = st * gl_ref[0:1, sl] + mm_tn(kd_ref[:, sl], vn)
        for sl, o in zip(lanes, outs):
            y_ref[:, sl] = gated_norm(o, nw_ref[...], z_ref[:, sl]).astype(bf16)

    blk = pl.BlockSpec((CHUNK, width), lambda n: (n, 0))
    return _pc(body, name="dn_seq_fwd", grid=(n_chunks,),
               in_specs=[blk] * 6 + [pl.BlockSpec((CHUNK, width), lambda n: (n, zd_off)), pl.BlockSpec((1, HEAD_DIM), lambda n: (0, 0))],
               out_specs=[blk, pl.BlockSpec((1, n_heads, HEAD_DIM, HEAD_DIM), lambda n: (n, 0, 0, 0))],
               out_shape=[jax.ShapeDtypeStruct((tp, width), bf16), jax.ShapeDtypeStruct((n_chunks, n_heads, HEAD_DIM, HEAD_DIM), f32)],
               scratch_shapes=[pltpu.VMEM((n_heads, HEAD_DIM, HEAD_DIM), f32)],
               compiler_params=_params("arbitrary"))(*inter, proj, dn_norm_w)


def dn_seq_bwd(inter, proj, dn_norm_w, states, dy, n_heads, zd_off):
    tp, width = inter[0].shape
    n_chunks = tp // CHUNK
    last = n_chunks - 1

    def body(u_ref, w_ref, qd_ref, kd_ref, qk_ref, gl_ref, z_ref, nw_ref, s_ref, dy_ref,
             du_ref, dw_ref, dqd_ref, dkd_ref, dqk_ref, dgl_ref, dz_ref, dnw_ref, dstate):
        @pl.when(pl.program_id(0) == 0)
        def _():
            dstate[...] = jnp.zeros_like(dstate)
            dnw_ref[...] = jnp.zeros_like(dnw_ref)

        lanes = [slice(h * HEAD_DIM, (h + 1) * HEAD_DIM) for h in range(n_heads)]
        sts = [s_ref[0, h] for h in range(n_heads)]
        dsts = [dstate[h] for h in range(n_heads)]
        v_new = [u_ref[:, sl] - mm_nn(w_ref[:, sl], st) for sl, st in zip(lanes, sts)]
        outs = [mm_nn(qd_ref[:, sl], st) + mm_nn(qk_ref[:, sl][:, :CHUNK], vn) for sl, st, vn in zip(lanes, sts, v_new)]
        dnw = jnp.zeros((1, HEAD_DIM), f32)
        d_outs = []
        for sl, o in zip(lanes, outs):
            do, dn, dz = jax.vjp(gated_norm, o, nw_ref[...], z_ref[:, sl])[1](dy_ref[:, sl])
            dz_ref[:, sl] = dz.astype(bf16)
            dnw = dnw + dn
            d_outs.append(do)
        dnw_ref[...] += dnw
        d_vn = [mm_tn(qk_ref[:, sl][:, :CHUNK], do) + mm_nn(kd_ref[:, sl], ds) for sl, do, ds in zip(lanes, d_outs, dsts)]
        zeros = jnp.zeros((HEAD_DIM - CHUNK, HEAD_DIM), f32)
        rows = lax.broadcasted_iota(jnp.int32, (CHUNK, HEAD_DIM), 0)
        for h, (sl, st, vn, do, ds, dvn) in enumerate(zip(lanes, sts, v_new, d_outs, dsts, d_vn)):
            du_ref[:, sl] = dvn
            dw_ref[:, sl] = -mm_nt(dvn, st)
            dqd_ref[:, sl] = mm_nt(do, st)
            dkd_ref[:, sl] = mm_nt(vn, ds)
            dqk_ref[:, sl] = mm_nt(do, jnp.concatenate([vn, zeros], axis=0))
            dgl_ref[:, sl] = jnp.where(rows == 0, jnp.sum(st * ds, axis=0, keepdims=True), 0.0)
            dstate[h] = ds * gl_ref[0:1, sl] + mm_tn(qd_ref[:, sl], do) - mm_tn(w_ref[:, sl], dvn)

    blk = pl.BlockSpec((CHUNK, width), lambda n: (last - n, 0))
    return _pc(body, name="dn_seq_bwd", grid=(n_chunks,),
               in_specs=[blk] * 6 + [pl.BlockSpec((CHUNK, width), lambda n: (last - n, zd_off)), pl.BlockSpec((1, HEAD_DIM), lambda n: (0, 0)),
                         pl.BlockSpec((1, n_heads, HEAD_DIM, HEAD_DIM), lambda n: (last - n, 0, 0, 0)), blk],
               out_specs=[blk] * 7 + [pl.BlockSpec((1, HEAD_DIM), lambda n: (0, 0))],
               out_shape=[jax.ShapeDtypeStruct((tp, width), f32)] * 6 + [jax.ShapeDtypeStruct((tp, width), bf16),
                                                                          jax.ShapeDtypeStruct((1, HEAD_DIM), f32)],
               scratch_shapes=[pltpu.VMEM((n_heads, HEAD_DIM, HEAD_DIM), f32)],
               compiler_params=_params("arbitrary"))(*inter, proj, dn_norm_w, states, dy)


def loss_stage(out, final_w, target):
    tp, d = out.shape
    n_tiles = tp // CHUNK

    def body(o_ref, w_ref, t_ref, loss_ref, do_ref, dob_ref, dw_ref):
        i = pl.program_id(0)

        @pl.when(i == 0)
        def _():
            loss_ref[...] = jnp.zeros_like(loss_ref)
            dw_ref[...] = jnp.zeros_like(dw_ref)

        scored = (i > 0).astype(f32)
        val, (do, dw) = jax.value_and_grad(lambda o, w: scored * loss_fn(o, w, t_ref[...]), argnums=(0, 1))(o_ref[...], w_ref[...])
        loss_ref[...] += jnp.full(loss_ref.shape, val, f32)
        do_ref[...] = do
        dob_ref[...] = do.astype(bf16)
        dw_ref[...] += dw

    row = pl.BlockSpec((CHUNK, d), lambda i: (i, 0))
    vec = pl.BlockSpec((1, d), lambda i: (0, 0))
    return _pc(body, name="loss_stage", grid=(n_tiles,),
               in_specs=[row, vec, pl.BlockSpec((CHUNK, d), lambda i: (jnp.maximum(i - 1, 0), 0))],
               out_specs=[pl.BlockSpec((1, LANES), lambda i: (0, 0)), row, row, vec],
               out_shape=[jax.ShapeDtypeStruct((1, LANES), f32), jax.ShapeDtypeStruct((tp, d), f32),
                          jax.ShapeDtypeStruct((tp, d), bf16), jax.ShapeDtypeStruct((1, d), f32)],
               compiler_params=_params("arbitrary"))(out, final_w, target)


def _adam_update(w, g, m, v):
    nm = ADAM_B1 * m + (1.0 - ADAM_B1) * g
    nv = ADAM_B2 * v + (1.0 - ADAM_B2) * jnp.square(g)
    m_hat = nm / (1.0 - ADAM_B1 ** ADAM_STEP)
    v_hat = nv / (1.0 - ADAM_B2 ** ADAM_STEP)
    return -ADAM_LR * (m_hat / (jnp.sqrt(v_hat) + ADAM_EPS) + ADAM_WD * w), nm, nv


def adamw(name, w, g, m, v):
    rows, cols = w.shape
    t = _row_tile(rows, cols, 7)

    def body(w_ref, g_ref, m_ref, v_ref, d_ref, nm_ref, nv_ref):
        d_ref[...], nm_ref[...], nv_ref[...] = _adam_update(w_ref[...], g_ref[...], m_ref[...], v_ref[...])

    blk = pl.BlockSpec((t, cols), lambda i: (i, 0))
    return _pc(body, name=name, grid=(rows // t,), in_specs=[blk] * 4, out_specs=[blk] * 3,
               out_shape=[jax.ShapeDtypeStruct(w.shape, f32)] * 3, compiler_params=_params("parallel"))(w, g, m, v)


def adamw_joined(name, w, g_mine, g_theirs, m, v, half, axis):
    rows, cols = w.shape
    hr, hc = g_mine.shape
    tr, tc = _tile(hr, hc, 9)
    nr, nc = hr // tr, hc // tc

    def body(half_ref, w_ref, gm_ref, gt_ref, m_ref, v_ref, g_ref, d_ref, nm_ref, nv_ref):
        pos = pl.program_id(axis) // (nr if axis == 0 else nc)
        g = jnp.where(pos == half_ref[0], gm_ref[...], gt_ref[...])
        delta, nm, nv = _adam_update(w_ref[...], g, m_ref[...], v_ref[...])
        g_ref[...] = g
        d_ref[...] = delta
        nm_ref[...] = nm
        nv_ref[...] = nv

    whole = pl.BlockSpec((tr, tc), lambda i, j, hf: (i, j))
    part = pl.BlockSpec((tr, tc), lambda i, j, hf: (i % nr, j % nc))
    grid_spec = pltpu.PrefetchScalarGridSpec(num_scalar_prefetch=1, grid=(rows // tr, cols // tc),
                                             in_specs=[whole, part, part, whole, whole], out_specs=[whole] * 4)
    return _pc(body, name=name, grid_spec=grid_spec, out_shape=[jax.ShapeDtypeStruct(w.shape, f32)] * 4,
               compiler_params=_params("parallel", "parallel"))(half, w, g_mine, g_theirs, m, v)


def add_halves(name, g, recv, half, kind):
    s, r, c = recv.shape
    tr, tc = _tile(r, c, 3)
    nc = c // tc

    def body(half_ref, g_ref, r_ref, o_ref):
        o_ref[...] = (g_ref[...].reshape(r_ref.shape).astype(f32) + r_ref[...].astype(f32)).astype(bf16)

    if kind == "lead":
        g_spec = pl.BlockSpec((1, 1, tr, tc), lambda i, j, k, hf: (hf[0], i, j, k))
    else:
        g_spec = pl.BlockSpec((1, tr, tc), lambda i, j, k, hf: (i, j, hf[0] * nc + k))
    blk = pl.BlockSpec((1, tr, tc), lambda i, j, k, hf: (i, j, k))
    grid_spec = pltpu.PrefetchScalarGridSpec(num_scalar_prefetch=1, grid=(s, r // tr, nc), in_specs=[g_spec, blk], out_specs=blk)
    return _pc(body, name=name, grid_spec=grid_spec, out_shape=jax.ShapeDtypeStruct((s, r, c), bf16),
               compiler_params=_params("parallel", "parallel", "parallel"))(half, g, recv)


def add_first(name, parts, got, chips):
    _, r, c = parts.shape
    tr, tc = _tile(r, c, 6)

    def body(chips_ref, mine_ref, theirs_ref, got_ref, keep_ref, pass_ref):
        keep_ref[...] = mine_ref[0].astype(f32) + got_ref[0].astype(f32)
        pass_ref[...] = (theirs_ref[0].astype(f32) + got_ref[1].astype(f32)).astype(bf16)

    blk = pl.BlockSpec((tr, tc), lambda i, j, ch: (i, j))
    grid_spec = pltpu.PrefetchScalarGridSpec(
        num_scalar_prefetch=1, grid=(r // tr, c // tc),
        in_specs=[pl.BlockSpec((1, tr, tc), lambda i, j, ch: (ch[0], i, j)), pl.BlockSpec((1, tr, tc), lambda i, j, ch: (ch[1], i, j)),
                  pl.BlockSpec((2, tr, tc), lambda i, j, ch: (0, i, j))],
        out_specs=[blk, blk])
    return _pc(body, name=name, grid_spec=grid_spec, out_shape=[jax.ShapeDtypeStruct((r, c), f32), jax.ShapeDtypeStruct((r, c), bf16)],
               compiler_params=_params("parallel", "parallel"))(chips, parts, parts, got)


def add_second(name, kept, got):
    r, c = kept.shape
    tr, tc = _tile(r, c, 3)

    def body(k_ref, g_ref, o_ref):
        o_ref[...] = k_ref[...] + g_ref[...].astype(f32)

    blk = pl.BlockSpec((tr, tc), lambda i, j: (i, j))
    return _pc(body, name=name, grid=(r // tr, c // tc), in_specs=[blk, blk], out_specs=blk,
               out_shape=jax.ShapeDtypeStruct((r, c), f32), compiler_params=_params("parallel", "parallel"))(kept, got)


def sum_leading(name, x, out_dtype=f32):
    s, r, c = x.shape
    tr, tc = _tile(r, c, s + 1)

    def body(x_ref, o_ref):
        acc = x_ref[0].astype(f32)
        for i in range(1, s):
            acc = acc + x_ref[i].astype(f32)
        o_ref[...] = acc.astype(out_dtype)

    return _pc(body, name=name, grid=(r // tr, c // tc), in_specs=[pl.BlockSpec((s, tr, tc), lambda i, j: (0, i, j))],
               out_specs=pl.BlockSpec((tr, tc), lambda i, j: (i, j)), out_shape=jax.ShapeDtypeStruct((r, c), out_dtype),
               compiler_params=_params("parallel", "parallel"))(x)


def _place():
    x, y, c = lax.axis_index("x"), lax.axis_index("y"), lax.axis_index("c")
    return x, y, c


def _route(x, y, c):
    return (x ^ (1 - c), y ^ c), (x ^ c, y ^ (1 - c)), (1 - x, 1 - y)


def _half_view(ref, half, kind, lead=()):
    if kind == "lead":
        return ref.at[(*lead, half)]
    width = ref.shape[-1] // 2
    return ref.at[(*lead, *([slice(None)] * (len(ref.shape) - len(lead) - 1)), pl.ds(half * width, width))]


def _gather_plan(ins, outs, sems, kinds):
    n = len(ins)
    send1, recv1, send2, recv2 = sems
    x, y, c = _place()
    chip = 2 * x + y
    sibling = (x, y, 1 - c)
    near, far, diag = _route(x, y, c)
    near_id, far_id, diag_id = [2 * cx + cy for cx, cy in (near, far, diag)]

    def remote(src, dst, s_sem, r_sem, to):
        return pltpu.make_async_remote_copy(src_ref=src, dst_ref=dst, send_sem=s_sem, recv_sem=r_sem, device_id=to, device_id_type=MESH)

    def slab(a, chip_id, half):
        return _half_view(outs[a], half, kinds[a], (chip_id,))

    def landed(a, chip_id, sem, frm):
        return remote(slab(a, chip_id, c), slab(a, chip_id, c), send1.at[a, sem], recv1.at[a, sem], (*frm, c))

    def onward(a, chip_id, sem):
        return remote(slab(a, chip_id, c), slab(a, chip_id, c), send2.at[a, sem], recv2.at[a, sem], sibling)

    own = [remote(_half_view(ins[a], c, kinds[a]), slab(a, chip, c), send1.at[a, j], recv1.at[a, j], (*to, c))
           for a in range(n) for j, to in enumerate((near, far))]
    relay = [remote(slab(a, near_id, c), slab(a, near_id, c), send1.at[a, 2], recv1.at[a, 2], (*far, c)) for a in range(n)]
    to_sibling = [[onward(a, cid, sem) for sem, cid in enumerate((near_id, far_id, diag_id))] for a in range(n)]
    from_sibling = [remote(slab(a, cid, 1 - c), slab(a, cid, 1 - c), send2.at[a, sem], recv2.at[a, sem], sibling)
                    for a in range(n) for sem, cid in enumerate((far_id, near_id, diag_id))]

    def start_own():
        for cp in own:
            cp.start()

    def pass_near():
        for a in range(n):
            landed(a, near_id, 0, near).wait_recv()
            relay[a].start()
            to_sibling[a][0].start()

    def pass_far():
        for sem, cid in ((1, far_id), (2, diag_id)):
            for a in range(n):
                landed(a, cid, sem, far).wait_recv()
                to_sibling[a][sem].start()

    def finish():
        for cp in from_sibling:
            cp.wait_recv()
        for cp in own + relay + [cp for row in to_sibling for cp in row]:
            cp.wait_send()

    return [start_own, pass_near, pass_far, finish]


def _gather_shapes(shards):
    return ([jax.ShapeDtypeStruct((4,) + s.shape, s.dtype) for s in shards], [pltpu.SemaphoreType.DMA((len(shards), 3))] * 4)


def gather_weights(shards, kinds):
    n = len(shards)

    def body(*refs):
        for emit in _gather_plan(refs[:n], refs[n:2 * n], refs[2 * n:], kinds):
            emit()

    out_shapes, sems = _gather_shapes(shards)
    return _pc(body, name="gather_weights", in_specs=[_ANY] * n, out_specs=[_ANY] * n, out_shape=out_shapes, scratch_shapes=sems)(*shards)


def hosted_gather(shards, kinds):
    out_shapes, sems = _gather_shapes(shards)
    return Hosted(shards, out_shapes, sems, lambda i, o, s: list(zip((0.0, 0.45, 0.8, 1.0), _gather_plan(i, o, s, kinds))))


def swap_with_sibling(name, sends):
    n = len(sends)

    def body(*refs):
        ins, outs = refs[:n], refs[n:2 * n]
        send, recv = refs[2 * n:]
        x, y, c = _place()
        cps = [pltpu.make_async_remote_copy(src_ref=ins[a], dst_ref=outs[a], send_sem=send.at[a], recv_sem=recv.at[a],
                                            device_id=(x, y, 1 - c), device_id_type=MESH) for a in range(n)]
        for cp in cps:
            cp.start()
        for cp in cps:
            cp.wait()

    return _pc(body, name=name, in_specs=[_ANY] * n, out_specs=[_ANY] * n,
               out_shape=[jax.ShapeDtypeStruct(s.shape, s.dtype) for s in sends],
               scratch_shapes=[pltpu.SemaphoreType.DMA((n,))] * 2)(*sends)


def send_grad_halves(name, grads, kinds):
    n = len(grads)

    def body(*refs):
        ins, outs = refs[:n], refs[n:2 * n]
        send, recv = refs[2 * n:]
        x, y, c = _place()
        cps = [pltpu.make_async_remote_copy(src_ref=_half_view(ins[a], 1 - c, kinds[a]), dst_ref=outs[a], send_sem=send.at[a],
                                            recv_sem=recv.at[a], device_id=(x, y, 1 - c), device_id_type=MESH) for a in range(n)]
        for cp in cps:
            cp.start()
        for cp in cps:
            cp.wait()

    shape = lambda g, kind: g.shape[1:] if kind == "lead" else g.shape[:-1] + (g.shape[-1] // 2,)
    return _pc(body, name=name, in_specs=[_ANY] * n, out_specs=[_ANY] * n,
               out_shape=[jax.ShapeDtypeStruct(shape(g, k), g.dtype) for g, k in zip(grads, kinds)],
               scratch_shapes=[pltpu.SemaphoreType.DMA((n,))] * 2)(*grads)


def _scatter_first_plan(ins, outs, sems):
    n = len(ins)
    send, recv = sems
    x, y, c = _place()
    near, _, diag = _route(x, y, c)
    cps = [pltpu.make_async_remote_copy(src_ref=ins[a].at[2 * cx + cy], dst_ref=outs[a].at[j], send_sem=send.at[a, j],
                                        recv_sem=recv.at[a, j], device_id=(*near, c), device_id_type=MESH)
           for a in range(n) for j, (cx, cy) in enumerate((near, diag))]

    def start():
        for cp in cps:
            cp.start()

    def wait():
        for cp in cps:
            cp.wait()

    return [start, wait]


def _scatter_first_shapes(parts):
    return ([jax.ShapeDtypeStruct((2,) + p.shape[1:], p.dtype) for p in parts], [pltpu.SemaphoreType.DMA((len(parts), 2))] * 2)


def hosted_scatter_first(parts):
    out_shapes, sems = _scatter_first_shapes(parts)
    return Hosted(parts, out_shapes, sems, lambda i, o, s: list(zip((0.0, 1.0), _scatter_first_plan(i, o, s))))


def _far_swap_plan(ins, outs, sems):
    send, recv = sems
    x, y, c = _place()
    _, far, _ = _route(x, y, c)
    cps = [pltpu.make_async_remote_copy(src_ref=ins[a], dst_ref=outs[a], send_sem=send.at[a], recv_sem=recv.at[a],
                                        device_id=(*far, c), device_id_type=MESH) for a in range(len(ins))]

    def start():
        for cp in cps:
            cp.start()

    def wait():
        for cp in cps:
            cp.wait()

    return [start, wait]


def hosted_scatter_second(parts):
    return Hosted(parts, [jax.ShapeDtypeStruct(p.shape, p.dtype) for p in parts], [pltpu.SemaphoreType.DMA((len(parts),))] * 2,
                  lambda i, o, s: list(zip((0.0, 1.0), _far_swap_plan(i, o, s))))


def gather_small(pack):
    def body(pack_ref, packs_ref, send, recv):
        x, y, c = _place()
        me = 4 * x + 2 * y + c
        flips = [(fx, fy, fc) for fx in (0, 1) for fy in (0, 1) for fc in (0, 1)][1:]
        peers = [(x ^ fx, y ^ fy, c ^ fc) for fx, fy, fc in flips]
        cps = [pltpu.make_async_remote_copy(src_ref=pack_ref, dst_ref=packs_ref.at[me], send_sem=send.at[j], recv_sem=recv.at[j],
                                            device_id=p, device_id_type=MESH) for j, p in enumerate(peers)]
        for cp in cps:
            cp.start()
        for j, (px, py, pc) in enumerate(peers):
            slab = packs_ref.at[4 * px + 2 * py + pc]
            pltpu.make_async_remote_copy(src_ref=slab, dst_ref=slab, send_sem=send.at[j], recv_sem=recv.at[j],
                                         device_id=(px, py, pc), device_id_type=MESH).wait_recv()
        for cp in cps:
            cp.wait_send()

    return _pc(body, name="gather_small", in_specs=[_ANY], out_specs=_ANY, out_shape=jax.ShapeDtypeStruct((8,) + pack.shape, pack.dtype),
               scratch_shapes=[pltpu.SemaphoreType.DMA((7,))] * 2)(pack)


def _pack(arrays, width):
    flat = jnp.concatenate([a.reshape(-1).astype(f32) for a in arrays])
    return jnp.pad(flat, (0, (-flat.shape[0]) % (8 * width))).reshape(-1, width)


def _unpack(pack, like, width):
    flat, out, at = pack.reshape(-1), [], 0
    for a in like:
        size = 1
        for s in a.shape:
            size *= s
        out.append(flat[at:at + size].reshape(a.shape))
        at += size
    return out


def _halves(a2d):
    r, c = a2d.shape
    return a2d.reshape(2, r // 2, c)


def local_step(x, meta, norm_w, w_main, w_ba, conv_w, a_log, dt_bias, pool_mix, pool_scale, dn_norm_w,
               late_weights, final_w, target, grad_hooks=None):
    seq, d = x.shape
    pw = pool_scale.shape[1]
    dw = conv_w.shape[1] // 3
    n_heads = dw // HEAD_DIM
    tp = FRONT_PAD + N_META + seq
    hp = jnp.concatenate([jnp.zeros((FRONT_PAD, d), f32), meta, x], axis=0)
    tm_big = tp // 2 if tp % 32 == 0 else tp
    tm_norm = max(t for t in range(16, min(tp, 352) + 1, 16) if tp % t == 0)
    tile = min(512, d)
    off_q = 2 * pw
    off_zd = off_q + 3 * dw
    off_gp = off_zd + dw
    off_gd = off_gp + d
    a_log128 = jnp.pad(a_log, ((0, 0), (0, LANES - n_heads)))
    dt128 = jnp.pad(dt_bias, ((0, 0), (0, LANES - n_heads)))

    xn = norm_fwd(hp, norm_w, tm_norm)
    if isinstance(late_weights[0], Hosted):
        proj, *fetched = matmul("proj", xn, w_main, tb=True, tm=tp, tn=tile, hosted=late_weights[0])
        w_pool_out, w_dn_out, w_o = late_weights[1](fetched)
    else:
        proj = matmul("proj", xn, w_main, tb=True, tm=tp, tn=tile)
        w_pool_out, w_dn_out, w_o = late_weights
    ba = matmul("proj_ba", xn, w_ba, tb=True, tm=tp, tn=LANES)
    y_pool = pool_fwd(proj, pool_mix, pool_scale, pw)
    q, k, v, beta_b, g_b = dn_pre_fwd(proj, ba, conv_w, a_log128, dt128, n_heads, off_q // HEAD_DIM)
    assert off_zd % dw == 0
    *inter, tmats = dn_intra_fwd(q, k, v, beta_b, g_b, n_heads)
    y_dn, states = dn_seq_fwd(inter, proj, dn_norm_w, n_heads, off_zd // dw)
    a_mat = matmul("pool_out", y_pool, w_pool_out, tm=tp, tn=tile)

    def merge(acc, a_t, gp_t, gd_t):
        return acc, sigmoid(gp_t) * a_t + sigmoid(gd_t) * acc

    b_mat, merged = matmul("dn_out_merge", y_dn, w_dn_out, tm=tm_big, tn=tile, extras=[(a_mat, 0), (proj, off_gp), (proj, off_gd)],
                           epi=merge, out_dtypes=(f32, bf16))
    out = matmul("out_proj", merged, w_o, tm=tm_big, tn=tile, extras=[(hp, 0)], epi=lambda acc, h_t: (acc + h_t,))
    loss, dout, dout_b, dfinal_w = loss_stage(out, final_w, target)

    def unmerge(dm, a_t, b_t, gp_t, gd_t):
        sp, sd = sigmoid(gp_t), sigmoid(gd_t)
        return dm * sp, dm * sd, dm * a_t * sp * (1.0 - sp), dm * b_t * sd * (1.0 - sd)

    d_a, d_b, d_gp, d_gd = matmul("d_merged", dout_b, w_o, tb=True, tm=tm_big, tn=tile,
                                  extras=[(a_mat, 0), (b_mat, 0), (proj, off_gp), (proj, off_gd)], epi=unmerge,
                                  out_dtypes=(bf16,) * 4)
    g_w_o = matmul("g_w_o", merged, dout_b, ta=True, tm=tile, tn=d, out_dtypes=(bf16,))
    d_y_pool = matmul("d_y_pool", d_a, w_pool_out, tb=True, tm=tp, tn=tile)
    g_w_pool_out = matmul("g_w_pool_out", y_pool, d_a, ta=True, tm=tile, tn=d, out_dtypes=(bf16,))
    d_y_dn = matmul("d_y_dn", d_b, w_dn_out, tb=True, tm=tp, tn=tile)
    g_w_dn_out = matmul("g_w_dn_out", y_dn, d_b, ta=True, tm=tile, tn=d, out_dtypes=(bf16,))
    d_u, d_zp, g_pool_mix, g_pool_scale = pool_bwd(proj, pool_mix, pool_scale, d_y_pool, pw)
    *d_inter, d_zd, g_dn_norm_w = dn_seq_bwd(inter, proj, dn_norm_w, states, d_y_dn, n_heads, off_zd // dw)
    d_q, d_k, d_v, d_beta, d_g = dn_intra_bwd(q, k, v, beta_b, g_b, tmats, d_inter, n_heads)
    d_qr, d_kr, d_vr, d_ba, g_cq, g_ck, g_cv, g_a_log, g_dt = dn_pre_bwd(
        proj, ba, conv_w, a_log128, dt128, (d_q, d_k, d_v, d_beta, d_g), n_heads, off_q // HEAD_DIM)
    d_proj = jnp.concatenate([d_u, d_zp, d_qr, d_kr, d_vr, d_zd, d_gp, d_gd], axis=1)
    d_ba_b = cast_bf16("cast_d_ba", d_ba)
    early = grad_hooks[0](g_w_pool_out, g_w_dn_out, g_w_o, g_pool_mix) if grad_hooks else None
    res = matmul("g_w_main", d_proj, xn, ta=True, tm=tile, tn=d, out_dtypes=(bf16,), hosted=early)
    g_w_main, early_landed = (res[0], list(res[1:])) if early else (res, [])
    g_w_ba = matmul("g_w_ba", d_ba_b, xn, ta=True, tm=LANES, tn=tile, out_dtypes=(bf16,))
    hosted = grad_hooks[1](g_w_main, g_w_ba, early_landed) if grad_hooks else None
    dxn_ba = matmul("dxn_ba", d_ba_b, w_ba, tm=tp, tn=tile)
    n_cols = d_proj.shape[1]
    tk_dxn = max(t for t in range(LANES, min(2048, n_cols) + 1, LANES) if n_cols % t == 0)
    res = matmul("dxn", d_proj, w_main, tm=tp, tn=tile, tk=tk_dxn, extras=[(dxn_ba, 0)], epi=lambda acc, e: (acc + e,), hosted=hosted)
    dxn, landed = (res[0], list(res[1:])) if hosted else (res, [])
    dh, g_norm_w, last_landed = norm_bwd(hp, norm_w, dxn, dout, tm_norm, hosted=grad_hooks[2](landed) if grad_hooks else None)
    g_conv = jnp.concatenate([g_cq, g_ck, g_cv], axis=1)
    return (loss, dh, g_norm_w, g_w_main, g_w_ba, g_conv, g_a_log[:, :n_heads], g_dt[:, :n_heads], g_pool_mix, g_pool_scale,
            g_dn_norm_w, g_w_pool_out, g_w_dn_out, g_w_o, dfinal_w, last_landed)


def kernel(x, meta_tokens, norm_w, w_in, conv_w, A_log, dt_bias, pool_mix, pool_scale, dn_norm_w, w_pool_out, w_dn_out, w_o, final_norm_w, loss_target, m_meta_tokens, m_norm_w, m_w_in, m_conv_w, m_A_log, m_dt_bias, m_pool_mix, m_pool_scale, m_dn_norm_w, m_w_pool_out, m_w_dn_out, m_w_o, m_final_norm_w, v_meta_tokens, v_norm_w, v_w_in, v_conv_w, v_A_log, v_dt_bias, v_pool_mix, v_pool_scale, v_dn_norm_w, v_w_pool_out, v_w_dn_out, v_w_o, v_final_norm_w):
    d = x.shape[-1]
    pw = pool_scale.shape[-1]
    dw = w_dn_out.shape[1] * 4
    n_heads = dw // HEAD_DIM
    gdim = pw // POOL_GROUPS
    chip = 2 * lax.axis_index("x") + lax.axis_index("y")
    core = lax.axis_index("c")

    half = core.astype(jnp.int32).reshape(1)
    me = 4 * lax.axis_index("x") + 2 * lax.axis_index("y") + core

    w_in_t = w_in[0].T
    mix_s = pool_mix[0].reshape(POOL_GROUPS * (gdim // 4), gdim)
    small_s = _pack([meta_tokens, conv_w[0]], d)
    small_rows = small_s.shape[0]
    small_s = jnp.pad(small_s, ((0, (-small_rows) % 16), (0, 0)))
    sw = w_in_t.shape[0]
    n_main, n_ba = 2 * pw + 4 * dw, 2 * n_heads

    early = [cast_bf16("cast_w_in", w_in_t), _halves(cast_bf16("cast_mix", mix_s)), _halves(small_s)]
    late = [_halves(cast_bf16("cast_w_po", w_pool_out[0])), _halves(cast_bf16("cast_w_do", w_dn_out[0])),
            _halves(cast_bf16("cast_w_o", w_o[0]))]
    fill = lambda g, own: lax.dynamic_update_slice(g, own[None], (chip,) + (0,) * own.ndim)
    g_in, g_mix, g_small = [fill(g, own) for g, own in zip(gather_weights(early, ["cols", "lead", "lead"]), early)]

    def shard_rows(lo, hi):
        cut = [(max(lo, j * sw), min(hi, (j + 1) * sw), j) for j in range(4)]
        return [g_in[j, a - j * sw:b - j * sw] for a, b, j in cut if a < b]

    w_main = jnp.concatenate(shard_rows(0, n_main) + shard_rows(n_main + n_ba, 4 * sw), axis=0)
    w_ba = jnp.pad(jnp.concatenate(shard_rows(n_main, n_main + n_ba), axis=0), ((0, LANES - n_ba), (0, 0)))
    cat_cols = lambda g: jnp.concatenate([g[j].reshape(-1, g.shape[-1]) for j in range(4)], axis=1)

    def late_weights(fetched):
        g_po, g_do, g_o = [fill(g, own) for g, own in zip(fetched, late)]
        return cat_cols(g_po), g_do.reshape(-1, g_do.shape[-1]), g_o.reshape(-1, g_o.shape[-1])

    mix_full = g_mix.reshape(4, POOL_GROUPS, gdim // 4, gdim).transpose(1, 0, 2, 3).reshape(POOL_GROUPS, gdim, gdim)
    smalls = [_unpack(g_small[j].reshape(-1, d)[:small_rows], [meta_tokens, conv_w[0]], d) for j in range(4)]
    meta_full = jnp.concatenate([s[0] for s in smalls], axis=1)
    conv_full = jnp.concatenate([s[1] for s in smalls], axis=1)

    names = ["w_in", "w_po", "w_do", "w_o", "mix"]
    kinds = ["cols", "lead", "lead", "lead", "lead"]

    far_chip = 2 * (lax.axis_index("x") ^ core) + (lax.axis_index("y") ^ (1 - core))
    chips = jnp.stack([chip, far_chip]).astype(jnp.int32)
    col_parts = lambda g: g.reshape(2, g.shape[0] // 2, 4, g.shape[1] // 4).transpose(0, 2, 1, 3)
    row_parts = lambda g: g.reshape(4, 2, g.shape[0] // 8, g.shape[1]).transpose(1, 0, 2, 3)

    def pair_sums(tag, group, parts):
        group_kinds = [kinds[names.index(nm)] for nm in group]
        from_sibling = send_grad_halves("send_grad_halves_" + tag, parts, group_kinds)
        return [add_halves("add_" + nm, p, r, half, k) for nm, p, r, k in zip(group, parts, from_sibling, group_kinds)]

    later = names[1:]
    state = {}

    def later_grads(g_w_po, g_w_do, g_w_o_full, g_mix_full):
        mix_rows = POOL_GROUPS * (gdim // 4)
        mix_parts = (g_mix_full.astype(bf16).reshape(POOL_GROUPS, 4, gdim // 4, gdim).transpose(1, 0, 2, 3)
                     .reshape(4, 2, mix_rows // 2, gdim).transpose(1, 0, 2, 3))
        state["later_sums"] = pair_sums("later", later, [col_parts(g_w_po), row_parts(g_w_do), row_parts(g_w_o_full), mix_parts])
        return hosted_scatter_first(state["later_sums"])

    def input_grads(g_w_main, g_w_ba, later_from_near):
        def grad_rows(lo, hi):
            segs = [(0, n_main, g_w_main, 0), (n_main, n_main + n_ba, g_w_ba, 0), (n_main + n_ba, 4 * sw, g_w_main, n_main)]
            cut = [(max(lo, s0), min(hi, s1), s0, arr, off) for s0, s1, arr, off in segs]
            return [arr[a - s0 + off:b - s0 + off] for a, b, s0, arr, off in cut if a < b]

        kept, passed = zip(*[add_first("add1_" + nm, p, g, chips) for nm, p, g in zip(later, state["later_sums"], later_from_near)])
        state["later_kept"] = list(kept)
        in_parts = jnp.stack([jnp.concatenate(grad_rows(j * sw, (j + 1) * sw), axis=0) for j in range(4)])
        state["in_sums"] = pair_sums("input", names[:1], [in_parts])
        return hosted_scatter_first(state["in_sums"]) + hosted_scatter_second(list(passed))

    def input_second(landed):
        state["later_from_far"] = landed[1:]
        state["in_kept"], in_passed = add_first("add1_w_in", state["in_sums"][0], landed[0], chips)
        return hosted_scatter_second([in_passed])

    (loss, dh, g_norm_w, _, _, g_conv, g_a_log, g_dt, _, g_pool_scale, g_dn_norm_w, _, _, _, g_final_w,
     (in_from_far,)) = local_step(x[0], meta_full, norm_w, w_main, w_ba, conv_full, A_log, dt_bias, mix_full, pool_scale,
                                  dn_norm_w, (hosted_gather(late, ["lead"] * 3), late_weights), final_norm_w.reshape(1, d),
                                  loss_target[0], grad_hooks=(later_grads, input_grads, input_second))
    grad_x = dh[FRONT_PAD + N_META:][None]
    g_meta = dh[FRONT_PAD:FRONT_PAD + N_META]

    mine = [add_second("add2_" + nm, k_, g) for nm, k_, g in
            zip(names, [state["in_kept"]] + state["later_kept"], [in_from_far] + list(state["later_from_far"]))]
    small_like = [loss, g_norm_w, g_a_log, g_dt, g_pool_scale, g_dn_norm_w, g_final_w, g_conv, g_meta]
    pack = _pack(small_like, d)
    packs = gather_small(pack)
    theirs = swap_with_sibling("swap_grad_halves", mine)
    total = sum_leading("sum_small", lax.dynamic_update_slice(packs, pack[None], (me, 0, 0)))
    (loss_t, g_norm_w, g_a_log, g_dt, g_pool_scale, g_dn_norm_w, g_final_w, g_conv, g_meta) = _unpack(total, small_like, d)
    loss_out = loss_t[0, 0]
    g_conv_s = lax.dynamic_slice_in_dim(g_conv, chip * (g_conv.shape[1] // 4), g_conv.shape[1] // 4, axis=1)
    g_meta_s = lax.dynamic_slice_in_dim(g_meta, chip * (d // 4), d // 4, axis=1)

    weights = [meta_tokens, norm_w, w_in, conv_w, A_log, dt_bias, pool_mix, pool_scale, dn_norm_w, w_pool_out, w_dn_out, w_o, final_norm_w]
    ms = [m_meta_tokens, m_norm_w, m_w_in, m_conv_w, m_A_log, m_dt_bias, m_pool_mix, m_pool_scale, m_dn_norm_w, m_w_pool_out, m_w_dn_out, m_w_o, m_final_norm_w]
    vs = [v_meta_tokens, v_norm_w, v_w_in, v_conv_w, v_A_log, v_dt_bias, v_pool_mix, v_pool_scale, v_dn_norm_w, v_w_pool_out, v_w_dn_out, v_w_o, v_final_norm_w]
    grads = [g_meta_s, g_norm_w, None, g_conv_s[None], g_a_log, g_dt, None, g_pool_scale, g_dn_norm_w, None, None, None, g_final_w.reshape(d)]
    deltas, new_ms, new_vs = [None] * 13, [None] * 13, [None] * 13
    big = [2, 9, 10, 11, 6]
    for i, nm, g_mine, g_theirs in zip(big, names, mine, theirs):
        if nm == "w_in":
            to2d, back, axis = (lambda t: t[0].T), (lambda t: t.T[None]), 1
        else:
            to2d, back, axis = (lambda t: t.reshape(-1, t.shape[-1])), (lambda t, i=i: t.reshape(weights[i].shape)), 0
        res = adamw_joined("adamw_" + nm, to2d(weights[i]), g_mine, g_theirs, to2d(ms[i]), to2d(vs[i]), half, axis)
        grads[i], deltas[i], new_ms[i], new_vs[i] = [back(t) for t in res]
    small_idx = [i for i in range(13) if i not in big]
    packs = [_pack([arrs[i] for i in small_idx], d) for arrs in (weights, grads, ms, vs)]
    outs = adamw("adamw_small", *packs)
    like = [weights[i] for i in small_idx]
    for res, dest in zip(outs, (deltas, new_ms, new_vs)):
        for i, val in zip(small_idx, _unpack(res, like, d)):
            dest[i] = val
    return (loss_out, grad_x, *grads, *deltas, *new_ms, *new_vs)
```

```python
import functools

import jax
import jax.numpy as jnp
from jax import lax
from jax.experimental import pallas as pl
from jax.experimental.pallas import tpu as pltpu

f32 = jnp.float32
bf16 = jnp.bfloat16
MESH = pl.DeviceIdType.MESH

N_META = 16
CHUNK = 64
FRONT_PAD = (-N_META) % CHUNK
HEAD_DIM = 128
POOL_GROUPS = 4
POOL_WINDOWS = (2, 4, 8, 16)
CONV_WIDTH = 4
NORM_EPS = 1e-6
ADAM_LR, ADAM_B1, ADAM_B2, ADAM_EPS, ADAM_WD, ADAM_STEP = 0.001, 0.9, 0.999, 1e-08, 0.01, 10
LANES = 128
V7X_VMEM_BYTES = 64 * 2**20
VMEM_LIMIT = V7X_VMEM_BYTES - 8 * 2**20


def _pc(body, **kw):
    return pl.pallas_call(body, **kw)


def _params(*sem, **kw):
    return pltpu.CompilerParams(dimension_semantics=sem or None, vmem_limit_bytes=VMEM_LIMIT, **kw)


def _dg(a, b, dims):
    return lax.dot_general(a, b, (dims, ((), ())), preferred_element_type=f32)


@jax.custom_vjp
def mm_nn(a, b):
    return _dg(a.astype(bf16), b.astype(bf16), ((1,), (0,)))


@jax.custom_vjp
def mm_nt(a, b):
    return _dg(a.astype(bf16), b.astype(bf16), ((1,), (1,)))


@jax.custom_vjp
def mm_tn(a, b):
    return _dg(a.astype(bf16).T, b.astype(bf16), ((1,), (0,)))


mm_nn.defvjp(lambda a, b: (mm_nn(a, b), (a, b)), lambda r, dy: (mm_nt(dy, r[1]), mm_tn(r[0], dy)))
mm_nt.defvjp(lambda a, b: (mm_nt(a, b), (a, b)), lambda r, dy: (mm_nn(dy, r[1]), mm_tn(dy, r[0])))
mm_tn.defvjp(lambda a, b: (mm_tn(a, b), (a, b)), lambda r, dy: (mm_nt(r[1], dy), mm_nn(r[0], dy)))


def _split3(x):
    hi = x.astype(bf16)
    r1 = x - hi.astype(f32)
    mid = r1.astype(bf16)
    lo = (r1 - mid.astype(f32)).astype(bf16)
    return hi, mid, lo


def _split2(x):
    hi = x.astype(bf16)
    return hi, (x - hi.astype(f32)).astype(bf16)


@jax.custom_vjp
def mm_sel(sel, x):
    s = sel.astype(bf16)
    d = ((1,), (0,))
    hi, lo = _split2(x)
    return _dg(s, hi, d) + _dg(s, lo, d)


def _mm_sel_bwd(sel, dy):
    s = sel.astype(bf16)
    d = ((0,), (0,))
    hi, lo = _split2(dy)
    return jnp.zeros_like(sel), _dg(s, hi, d) + _dg(s, lo, d)


mm_sel.defvjp(lambda sel, x: (mm_sel(sel, x), sel), _mm_sel_bwd)


@jax.custom_vjp
def mm_pick(x, sel):
    s = sel.astype(bf16)
    d = ((1,), (0,))
    hi, mid, lo = _split3(x)
    return _dg(hi, s, d) + _dg(mid, s, d) + _dg(lo, s, d)


def _mm_pick_bwd(sel, dy):
    s = sel.astype(bf16)
    d = ((1,), (1,))
    hi, mid, lo = _split3(dy)
    return _dg(hi, s, d) + _dg(mid, s, d) + _dg(lo, s, d), jnp.zeros_like(sel)


mm_pick.defvjp(lambda x, sel: (mm_pick(x, sel), sel), _mm_pick_bwd)


def tri_inv(ls):
    n = ls[0].shape[0]
    eye = (lax.broadcasted_iota(jnp.int32, (n, n), 0) == lax.broadcasted_iota(jnp.int32, (n, n), 1)).astype(f32)
    ms = [-l for l in ls]
    ts = [eye + m for m in ms]
    k = 1
    while 2 * k < CHUNK:
        ms = [mm_nn(m, m) for m in ms]
        ts = [t + mm_nn(t, m) for t, m in zip(ts, ms)]
        k *= 2
    return ts


@functools.partial(jax.custom_vjp, nondiff_argnums=(1,))
def shift_rows(x, j):
    n = x.shape[0]
    rows = lax.broadcasted_iota(jnp.int32, x.shape, 0)
    if j >= 0:
        return jnp.where(rows >= j, pltpu.roll(x, j, 0), 0.0)
    return jnp.where(rows < n + j, pltpu.roll(x, n + j, 0), 0.0)


shift_rows.defvjp(lambda x, j: (shift_rows(x, j), None), lambda j, _, dy: (shift_rows(dy, -j),))


def sigmoid(x):
    return 1.0 / (1.0 + jnp.exp(-x))


def silu(x):
    return x * sigmoid(x)


def softplus(x):
    return jnp.maximum(x, 0.0) + jnp.log(1.0 + jnp.exp(-jnp.abs(x)))


def rmsnorm(x, w):
    return x * lax.rsqrt(jnp.mean(x * x, axis=-1, keepdims=True) + NORM_EPS) * w


def l2norm(x):
    return x * lax.rsqrt(jnp.sum(x * x, axis=-1, keepdims=True) + NORM_EPS)


def pool_fn(u, zp, mix, scale, group):
    rows = lax.broadcasted_iota(jnp.int32, u.shape, 0)
    sums = []
    s, w = u, 1
    while w < POOL_WINDOWS[-1]:
        s = s + shift_rows(s, w)
        w *= 2
        sums.append(s)
    total = sums[-1]
    for gi in range(POOL_GROUPS - 2, -1, -1):
        total = jnp.where(group == gi, sums[gi], total)
    window = jnp.left_shift(2, group)
    cnt = jnp.clip(rows - (FRONT_PAD - 1), 1, window).astype(f32)
    pooled = total / cnt - u
    return mm_nn(pooled, mix) * scale * silu(zp)


PRE_HALO = 8


def conv_silu(x, w):
    k = CONV_WIDTH
    y = x * w[k - 1:k, :]
    for kk in range(k - 1):
        y = y + shift_rows(x, k - 1 - kk) * w[kk:kk + 1, :]
    return silu(y[PRE_HALO:])


def _lane_pick(row, idx):
    lanes = lax.broadcasted_iota(jnp.int32, row.shape, 1)
    return jnp.sum(jnp.where(lanes == idx, row, 0.0), axis=1, keepdims=True)


def dn_pre_fn(qr, kr, vr, ba, cwq, cwk, cwv, a_log, dt_bias, head, n_heads, row0):
    q = l2norm(conv_silu(qr, cwq)) * (HEAD_DIM ** -0.5)
    k = l2norm(conv_silu(kr, cwk))
    v = conv_silu(vr, cwv)
    r = lax.broadcasted_iota(jnp.int32, (LANES, LANES), 0)
    b_b = mm_pick(ba, (r == head).astype(f32))
    a_b = mm_pick(ba, (r == head + n_heads).astype(f32))
    real = lax.broadcasted_iota(jnp.int32, ba.shape, 0) + row0 >= FRONT_PAD
    beta_b = jnp.where(real, sigmoid(b_b), 0.0)
    g_b = jnp.where(real, -jnp.exp(_lane_pick(a_log, head)) * softplus(a_b + _lane_pick(dt_bias, head)), 0.0)
    return q, k, v, beta_b, g_b


def _chunk_masks(rows):
    r = lax.broadcasted_iota(jnp.int32, (rows, rows), 0)
    c = lax.broadcasted_iota(jnp.int32, (rows, rows), 1)
    same = (r // CHUNK) == (c // CHUNK)
    return same, jnp.logical_and(same, r >= c), jnp.logical_and(same, r > c)


def _lane0(rows):
    return (lax.broadcasted_iota(jnp.int32, (rows, LANES), 1) == 0).astype(bf16)


@jax.custom_vjp
def lane0_as_row(x):
    sel = _lane0(x.shape[0])
    d = ((1,), (1,))
    hi, lo = _split2(x)
    return _dg(sel, hi, d) + _dg(sel, lo, d)


def _lane0_as_row_bwd(rows, dy):
    sel = _lane0(rows)
    d = ((0,), (0,))
    hi, lo = _split2(dy)
    return (_dg(hi, sel, d) + _dg(lo, sel, d),)


lane0_as_row.defvjp(lambda x: (lane0_as_row(x), x.shape[0]), _lane0_as_row_bwd)


def gate_fn(g_b):
    rows = g_b.shape[0]
    same, causal, _ = _chunk_masks(rows)
    gcum_b = mm_sel(causal.astype(f32), g_b)
    glast_b = mm_sel(same.astype(f32), g_b)
    g_rows = jnp.broadcast_to(gcum_b[:, :1], (rows, rows))
    decay = jnp.where(causal, jnp.exp(jnp.where(causal, g_rows - lane0_as_row(gcum_b), 0.0)), 0.0)
    return decay, jnp.exp(gcum_b), jnp.exp(glast_b - gcum_b), jnp.exp(glast_b)


def _fold_matrix(rows):
    r = lax.broadcasted_iota(jnp.int32, (rows, LANES), 0)
    c = lax.broadcasted_iota(jnp.int32, (rows, LANES), 1)
    return (r % CHUNK == c).astype(bf16)


@jax.custom_vjp
def fold_chunks(x):
    return _dg(x.astype(bf16), _fold_matrix(x.shape[0]), ((1,), (0,)))


def _fold_chunks_bwd(rows, dy):
    fold = _fold_matrix(rows)
    d = ((1,), (1,))
    hi, mid, lo = _split3(dy)
    return (_dg(hi, fold, d) + _dg(mid, fold, d) + _dg(lo, fold, d),)


fold_chunks.defvjp(lambda x: (fold_chunks(x), x.shape[0]), _fold_chunks_bwd)


def lmat_fn(k, beta_b, decay):
    _, _, strict = _chunk_masks(k.shape[0])
    return jnp.where(strict, mm_nt(k * beta_b, k) * decay, 0.0)


def intra_fn(tmat, q, k, v, beta_b, decay, eg, kfac):
    _, causal, _ = _chunk_masks(q.shape[0])
    k_beta = k * beta_b
    u_c = mm_nn(tmat, v * beta_b)
    w_c = mm_nn(tmat, k_beta * eg)
    qk = jnp.where(causal, mm_nt(q, k) * decay, 0.0)
    return u_c, w_c, q * eg, k * kfac, fold_chunks(qk)


def gated_norm(o, norm_w, zd):
    return rmsnorm(o, norm_w) * silu(zd)


def loss_fn(o, w, tgt):
    err = rmsnorm(o, w) - tgt
    return 0.5 * jnp.sum(jnp.mean(err * err, axis=-1))


_ANY = pl.BlockSpec(memory_space=pl.ANY)


class Hosted:
    def __init__(self, arrays, out_shapes, sems, stages):
        self.arrays, self.out_shapes, self.sems, self.stages = list(arrays), list(out_shapes), list(sems), stages

    def __add__(self, other):
        ni, no, ns = len(self.arrays), len(self.out_shapes), len(self.sems)
        stages = lambda i, o, s: self.stages(i[:ni], o[:no], s[:ns]) + other.stages(i[ni:], o[no:], s[ns:])
        return Hosted(self.arrays + other.arrays, self.out_shapes + other.out_shapes, self.sems + other.sems, stages)


def matmul(name, a, b, *, ta=False, tb=False, tm, tn, tk=None, extras=(), epi=None, out_dtypes=(f32,), hosted=None):
    m, k = (a.shape[1], a.shape[0]) if ta else a.shape
    n = b.shape[0] if tb else b.shape[1]
    tm, tn, tk = min(tm, m), min(tn, n), min(tk or k, k)
    assert m % tm == 0 and n % tn == 0 and k % tk == 0, (name, m, n, k, tm, tn, tk)
    nm, nn, nk = m // tm, n // tn, k // tk
    a_spec = pl.BlockSpec((tk, tm), lambda i, j, kk: (kk, i)) if ta else pl.BlockSpec((tm, tk), lambda i, j, kk: (i, kk))
    b_spec = pl.BlockSpec((tn, tk), lambda i, j, kk: (j, kk)) if tb else pl.BlockSpec((tk, tn), lambda i, j, kk: (kk, j))
    ex_specs = []
    for _, off in extras:
        assert off % tn == 0, (name, off, tn)
        ex_specs.append(pl.BlockSpec((tm, tn), functools.partial(lambda i, j, kk, o: (i, o + j), o=off // tn)))
    n_ex, n_out = len(extras), len(out_dtypes)
    dims = ((0 if ta else 1,), (1 if tb else 0,))
    n_hin = len(hosted.arrays) if hosted else 0
    n_hout = len(hosted.out_shapes) if hosted else 0
    n_sem = len(hosted.sems) if hosted else 0

    def body(a_ref, b_ref, *rest):
        ex_refs, rest = rest[:n_ex], rest[n_ex:]
        hin_refs, rest = rest[:n_hin], rest[n_hin:]
        out_refs, rest = rest[:n_out], rest[n_out:]
        hout_refs, rest = rest[:n_hout], rest[n_hout:]
        sem_refs = rest[len(rest) - n_sem:] if n_sem else ()
        step = (pl.program_id(0) * nn + pl.program_id(1)) * nk + pl.program_id(2)
        stages = hosted.stages(hin_refs, hout_refs, sem_refs) if hosted else []
        for frac, emit in stages:
            if frac < 1.0:
                pl.when(step == int(frac * (nm * nn * nk - 1)))(emit)

        def finish(acc):
            res = epi(acc, *[r[...] for r in ex_refs]) if epi is not None else (acc,)
            for o_ref, val in zip(out_refs, res):
                o_ref[...] = val.astype(o_ref.dtype)

        p = _dg(a_ref[...], b_ref[...], dims)
        if nk == 1:
            finish(p)
        else:
            acc_ref = rest[0]
            kk = pl.program_id(2)

            @pl.when(kk == 0)
            def _():
                acc_ref[...] = p

            @pl.when(kk > 0)
            def _():
                acc_ref[...] += p

            @pl.when(kk == nk - 1)
            def _():
                finish(acc_ref[...])

        for frac, emit in stages:
            if frac >= 1.0:
                pl.when(step == nm * nn * nk - 1)(emit)

    outs = _pc(
        body, name=name, grid=(nm, nn, nk),
        in_specs=[a_spec, b_spec] + ex_specs + [_ANY] * n_hin,
        out_specs=[pl.BlockSpec((tm, tn), lambda i, j, kk: (i, j))] * n_out + [_ANY] * n_hout,
        out_shape=[jax.ShapeDtypeStruct((m, n), dt) for dt in out_dtypes] + (hosted.out_shapes if hosted else []),
        scratch_shapes=([pltpu.VMEM((tm, tn), f32)] if nk > 1 else []) + (hosted.sems if hosted else []),
        compiler_params=_params(*(("arbitrary",) * 3 if hosted else ("parallel", "parallel", "arbitrary"))),
    )(a, b, *[e for e, _ in extras], *(hosted.arrays if hosted else []))
    return outs[0] if len(outs) == 1 else outs


def _row_tile(rows, cols, n_arrays, itemsize=4, budget=24 * 2**20):
    best = None
    for t in range(16, rows + 1, 16):
        if rows % t == 0 and 2 * n_arrays * t * cols * itemsize <= budget:
            best = t
    return best or rows


def _tile(rows, cols, n_arrays, budget=24 * 2**20):
    if rows % 16 == 0 or cols % LANES != 0:
        return _row_tile(rows, cols, n_arrays, budget=budget), cols
    fits = [t for t in range(LANES, cols + 1, LANES) if cols % t == 0 and 2 * n_arrays * rows * t * 4 <= budget]
    return rows, (max(fits) if fits else LANES)


def cast_bf16(name, x):
    rows, cols = x.shape
    tr, tc = _tile(rows, cols, 2)

    def body(x_ref, o_ref):
        o_ref[...] = x_ref[...].astype(bf16)

    blk = pl.BlockSpec((tr, tc), lambda i, j: (i, j))
    return _pc(body, name=name, grid=(rows // tr, cols // tc), in_specs=[blk], out_specs=blk,
               out_shape=jax.ShapeDtypeStruct(x.shape, bf16), compiler_params=_params("parallel", "parallel"))(x)


def norm_fwd(hp, norm_w, tm):
    tp, d = hp.shape

    def body(h_ref, w_ref, o_ref):
        o_ref[...] = rmsnorm(h_ref[...], w_ref[...]).astype(bf16)

    return _pc(body, name="norm_fwd", grid=(tp // tm,),
               in_specs=[pl.BlockSpec((tm, d), lambda i: (i, 0)), pl.BlockSpec((1, d), lambda i: (0, 0))],
               out_specs=pl.BlockSpec((tm, d), lambda i: (i, 0)), out_shape=jax.ShapeDtypeStruct((tp, d), bf16),
               compiler_params=_params("parallel"))(hp, norm_w)


def norm_bwd(hp, norm_w, dxn, dout, tm, hosted=None):
    tp, d = hp.shape
    steps = tp // tm
    n_hin = len(hosted.arrays) if hosted else 0
    n_hout = len(hosted.out_shapes) if hosted else 0

    def body(h_ref, w_ref, dxn_ref, dout_ref, *rest):
        hin_refs, (dh_ref, dw_ref), rest = rest[:n_hin], rest[n_hin:n_hin + 2], rest[n_hin + 2:]
        stages = hosted.stages(hin_refs, rest[:n_hout], rest[n_hout:]) if hosted else []
        for frac, emit in stages:
            if frac < 1.0:
                pl.when(pl.program_id(0) == int(frac * (steps - 1)))(emit)
        _, vjp = jax.vjp(rmsnorm, h_ref[...], w_ref[...])
        dh, dw = vjp(dxn_ref[...])
        dh_ref[...] = dh + dout_ref[...]

        @pl.when(pl.program_id(0) == 0)
        def _():
            dw_ref[...] = jnp.zeros_like(dw_ref)

        dw_ref[...] += dw
        for frac, emit in stages:
            if frac >= 1.0:
                pl.when(pl.program_id(0) == steps - 1)(emit)

    row = pl.BlockSpec((tm, d), lambda i: (i, 0))
    vec = pl.BlockSpec((1, d), lambda i: (0, 0))
    outs = _pc(body, name="norm_bwd", grid=(steps,), in_specs=[row, vec, row, row] + [_ANY] * n_hin,
               out_specs=[row, vec] + [_ANY] * n_hout,
               out_shape=[jax.ShapeDtypeStruct((tp, d), f32), jax.ShapeDtypeStruct((1, d), f32)] + (hosted.out_shapes if hosted else []),
               scratch_shapes=hosted.sems if hosted else [],
               compiler_params=_params("arbitrary"))(hp, norm_w, dxn, dout, *(hosted.arrays if hosted else []))
    return outs[0], outs[1], list(outs[2:])


def pool_fwd(proj, mix, scale, pw):
    tp = proj.shape[0]
    g = pw // POOL_GROUPS

    def body(u_ref, z_ref, mix_ref, s_ref, y_ref):
        y_ref[...] = pool_fn(u_ref[...], z_ref[...], mix_ref[0], s_ref[...], pl.program_id(0)).astype(bf16)

    return _pc(body, name="pool_fwd", grid=(POOL_GROUPS,),
               in_specs=[pl.BlockSpec((tp, g), lambda i: (0, i)), pl.BlockSpec((tp, g), lambda i: (0, POOL_GROUPS + i)),
                         pl.BlockSpec((1, g, g), lambda i: (i, 0, 0)), pl.BlockSpec((1, g), lambda i: (0, i))],
               out_specs=pl.BlockSpec((tp, g), lambda i: (0, i)), out_shape=jax.ShapeDtypeStruct((tp, pw), bf16),
               compiler_params=_params("parallel"))(proj, proj, mix, scale)


def pool_bwd(proj, mix, scale, dy, pw):
    tp = proj.shape[0]
    g = pw // POOL_GROUPS

    def body(u_ref, z_ref, mix_ref, s_ref, dy_ref, du_ref, dz_ref, dmix_ref, ds_ref):
        grp = pl.program_id(0)
        _, vjp = jax.vjp(lambda u, z, m, s: pool_fn(u, z, m, s, grp), u_ref[...], z_ref[...], mix_ref[0].astype(f32), s_ref[...])
        du, dz, dmix, ds = vjp(dy_ref[...])
        du_ref[...] = du.astype(bf16)
        dz_ref[...] = dz.astype(bf16)
        dmix_ref[0] = dmix
        ds_ref[...] = ds

    col = pl.BlockSpec((tp, g), lambda i: (0, i))
    return _pc(body, name="pool_bwd", grid=(POOL_GROUPS,),
               in_specs=[col, pl.BlockSpec((tp, g), lambda i: (0, POOL_GROUPS + i)),
                         pl.BlockSpec((1, g, g), lambda i: (i, 0, 0)), pl.BlockSpec((1, g), lambda i: (0, i)), col],
               out_specs=[col, col, pl.BlockSpec((1, g, g), lambda i: (i, 0, 0)), pl.BlockSpec((1, g), lambda i: (0, i))],
               out_shape=[jax.ShapeDtypeStruct((tp, pw), bf16), jax.ShapeDtypeStruct((tp, pw), bf16),
                          jax.ShapeDtypeStruct((POOL_GROUPS, g, g), f32), jax.ShapeDtypeStruct((1, pw), f32)],
               compiler_params=_params("parallel"))(proj, proj, mix, scale, dy)


def _dn_pre_specs(tp, n_heads, q_off):
    hb = lambda off: pl.BlockSpec((tp, HEAD_DIM), functools.partial(lambda h, o: (0, o + h), o=off))
    cw = lambda off: pl.BlockSpec((CONV_WIDTH, HEAD_DIM), functools.partial(lambda h, o: (0, o + h), o=off))
    whole = lambda shape: pl.BlockSpec(shape, lambda h: (0, 0))
    return ([hb(q_off), hb(q_off + n_heads), hb(q_off + 2 * n_heads), whole((tp, LANES)),
             cw(0), cw(n_heads), cw(2 * n_heads), whole((1, LANES)), whole((1, LANES))], hb, cw, whole)


def _pre_rows(tp):
    return max(t for t in range(16, min(tp, 192) + 1, 16) if tp % t == 0)


def _with_history(ref, r0, rows):
    if r0 == 0:
        return jnp.concatenate([jnp.zeros((PRE_HALO, ref.shape[1]), f32), ref[0:rows, :]], axis=0)
    return ref[r0 - PRE_HALO:r0 + rows, :]


def dn_pre_fwd(proj, ba, conv_w, a_log, dt_bias, n_heads, q_off):
    tp = proj.shape[0]
    rows = _pre_rows(tp)
    in_specs, hb, _, _ = _dn_pre_specs(tp, n_heads, q_off)

    def body(q_ref, k_ref, v_ref, ba_ref, cq_ref, ck_ref, cv_ref, al_ref, dt_ref, *out_refs):
        for r0 in range(0, tp, rows):
            outs = dn_pre_fn(_with_history(q_ref, r0, rows), _with_history(k_ref, r0, rows), _with_history(v_ref, r0, rows),
                             ba_ref[r0:r0 + rows, :], cq_ref[...], ck_ref[...], cv_ref[...], al_ref[...], dt_ref[...],
                             pl.program_id(0), n_heads, r0)
            for o_ref, val in zip(out_refs, outs):
                o_ref[r0:r0 + rows, :] = val

    return _pc(body, name="dn_pre_fwd", grid=(n_heads,), in_specs=in_specs, out_specs=[hb(0)] * 5,
               out_shape=[jax.ShapeDtypeStruct((tp, n_heads * HEAD_DIM), f32)] * 5,
               compiler_params=_params("parallel"))(proj, proj, proj, ba, conv_w, conv_w, conv_w, a_log, dt_bias)


def dn_pre_bwd(proj, ba, conv_w, a_log, dt_bias, cots, n_heads, q_off):
    tp = proj.shape[0]
    rows = _pre_rows(tp)
    in_specs, hb, cw, whole = _dn_pre_specs(tp, n_heads, q_off)

    def body(q_ref, k_ref, v_ref, ba_ref, cq_ref, ck_ref, cv_ref, al_ref, dt_ref, dq_ref, dk_ref, dv_ref, db_ref, dg_ref,
             dqr_ref, dkr_ref, dvr_ref, dba_ref, dcq_ref, dck_ref, dcv_ref, dal_ref, ddt_ref):
        head = pl.program_id(0)

        @pl.when(head == 0)
        def _():
            dba_ref[...] = jnp.zeros_like(dba_ref)
            dal_ref[...] = jnp.zeros_like(dal_ref)
            ddt_ref[...] = jnp.zeros_like(ddt_ref)

        owed = [jnp.zeros((PRE_HALO, HEAD_DIM), f32)] * 3
        d_conv = [jnp.zeros((CONV_WIDTH, HEAD_DIM), f32)] * 3
        d_al, d_dt = jnp.zeros((1, LANES), f32), jnp.zeros((1, LANES), f32)
        for r0 in reversed(range(0, tp, rows)):
            fn = lambda *args, r0=r0: dn_pre_fn(*args, head, n_heads, r0)
            _, vjp = jax.vjp(fn, _with_history(q_ref, r0, rows), _with_history(k_ref, r0, rows), _with_history(v_ref, r0, rows),
                             ba_ref[r0:r0 + rows, :], cq_ref[...], ck_ref[...], cv_ref[...], al_ref[...], dt_ref[...])
            tile = slice(r0, r0 + rows)
            *d_raw, dba, dcq, dck, dcv, dal, ddt = vjp((dq_ref[tile, :], dk_ref[tile, :], dv_ref[tile, :], db_ref[tile, :], dg_ref[tile, :]))
            for i, (out_ref, d) in enumerate(zip((dqr_ref, dkr_ref, dvr_ref), d_raw)):
                out_ref[tile, :] = jnp.concatenate([d[PRE_HALO:rows], d[rows:] + owed[i]], axis=0).astype(bf16)
                owed[i] = d[:PRE_HALO]
            dba_ref[tile, :] += dba
            d_conv = [acc + d for acc, d in zip(d_conv, (dcq, dck, dcv))]
            d_al, d_dt = d_al + dal, d_dt + ddt
        dcq_ref[...], dck_ref[...], dcv_ref[...] = d_conv
        dal_ref[...] += d_al
        ddt_ref[...] += d_dt

    w = n_heads * HEAD_DIM
    return _pc(body, name="dn_pre_bwd", grid=(n_heads,), in_specs=in_specs + [hb(0)] * 5,
               out_specs=[hb(0)] * 3 + [whole((tp, LANES)), cw(0), cw(0), cw(0), whole((1, LANES)), whole((1, LANES))],
               out_shape=[jax.ShapeDtypeStruct((tp, w), bf16)] * 3 + [jax.ShapeDtypeStruct((tp, LANES), f32)]
               + [jax.ShapeDtypeStruct((CONV_WIDTH, w), f32)] * 3 + [jax.ShapeDtypeStruct((1, LANES), f32)] * 2,
               compiler_params=_params("arbitrary"))(proj, proj, proj, ba, conv_w, conv_w, conv_w, a_log, dt_bias, *cots)


def _super_rows(tp):
    n = tp // CHUNK
    return CHUNK * max(j for j in (4, 3, 2, 1) if n % j == 0)


def _heads_per_step(n_heads):
    return max(j for j in (8, 4, 2, 1) if n_heads % j == 0)


def dn_intra_fwd(q, k, v, beta_b, g_b, n_heads):
    tp = q.shape[0]
    rows = _super_rows(tp)
    ns = tp // rows

    hps = _heads_per_step(n_heads)

    def body(q_ref, k_ref, v_ref, b_ref, g_ref, u_ref, w_ref, qd_ref, kd_ref, qk_ref, gl_ref, t_ref):
        lanes = [slice(i * HEAD_DIM, (i + 1) * HEAD_DIM) for i in range(hps)]
        gates = [gate_fn(g_ref[:, sl]) for sl in lanes]
        tmats = tri_inv([lmat_fn(k_ref[:, sl], b_ref[:, sl], gt[0]) for sl, gt in zip(lanes, gates)])
        for i, (sl, (decay, eg, kfac, gl), tmat) in enumerate(zip(lanes, gates, tmats)):
            u_c, w_c, q_dec, k_dec, qk_c = intra_fn(tmat, q_ref[:, sl], k_ref[:, sl], v_ref[:, sl], b_ref[:, sl], decay, eg, kfac)
            u_ref[:, sl] = u_c
            w_ref[:, sl] = w_c
            qd_ref[:, sl] = q_dec
            kd_ref[:, sl] = k_dec
            qk_ref[:, sl] = qk_c
            gl_ref[:, sl] = gl
            t_ref[i, 0] = tmat

    blk = pl.BlockSpec((rows, hps * HEAD_DIM), lambda h, s: (s, h))
    return _pc(body, name="dn_intra_fwd", grid=(n_heads // hps, ns), in_specs=[blk] * 5,
               out_specs=[blk] * 6 + [pl.BlockSpec((hps, 1, rows, rows), lambda h, s: (h, s, 0, 0))],
               out_shape=[jax.ShapeDtypeStruct(q.shape, f32)] * 6 + [jax.ShapeDtypeStruct((n_heads, ns, rows, rows), f32)],
               compiler_params=_params("parallel", "parallel"))(q, k, v, beta_b, g_b)


def dn_intra_bwd(q, k, v, beta_b, g_b, tmats, cots, n_heads):
    tp = q.shape[0]
    rows = _super_rows(tp)
    ns = tp // rows

    hps = _heads_per_step(n_heads)

    def body(q_ref, k_ref, v_ref, b_ref, g_ref, t_ref, du_ref, dw_ref, dqd_ref, dkd_ref, dqk_ref, dgl_ref,
             dq_ref, dk_ref, dv_ref, db_ref, dg_ref):
        lanes = [slice(i * HEAD_DIM, (i + 1) * HEAD_DIM) for i in range(hps)]
        tmats = [t_ref[i, 0] for i in range(hps)]
        gates = [jax.vjp(gate_fn, g_ref[:, sl]) for sl in lanes]
        intra = [jax.vjp(intra_fn, tmat, q_ref[:, sl], k_ref[:, sl], v_ref[:, sl], b_ref[:, sl], gt[0][0], gt[0][1], gt[0][2])[1](
            (du_ref[:, sl], dw_ref[:, sl], dqd_ref[:, sl], dkd_ref[:, sl], dqk_ref[:, sl]))
            for sl, tmat, gt in zip(lanes, tmats, gates)]
        tts = [tmat.T for tmat in tmats]
        dls = [mm_nn(tt, res[0]) for tt, res in zip(tts, intra)]
        dls = [-mm_nn(dl, tt) for dl, tt in zip(dls, tts)]
        for sl, gt, res, dl in zip(lanes, gates, intra, dls):
            _, dq, dk, dv, db, ddecay, deg, dkfac = res
            dk2, db2, ddecay2 = jax.vjp(lmat_fn, k_ref[:, sl], b_ref[:, sl], gt[0][0])[1](dl)
            (dg,) = gt[1]((ddecay + ddecay2, deg, dkfac, dgl_ref[:, sl]))
            dq_ref[:, sl] = dq
            dk_ref[:, sl] = dk + dk2
            dv_ref[:, sl] = dv
            db_ref[:, sl] = db + db2
            dg_ref[:, sl] = dg

    blk = pl.BlockSpec((rows, hps * HEAD_DIM), lambda h, s: (s, h))
    return _pc(body, name="dn_intra_bwd", grid=(n_heads // hps, ns),
               in_specs=[blk] * 5 + [pl.BlockSpec((hps, 1, rows, rows), lambda h, s: (h, s, 0, 0))] + [blk] * 6,
               out_specs=[blk] * 5, out_shape=[jax.ShapeDtypeStruct(q.shape, f32)] * 5,
               compiler_params=_params("parallel", "parallel"))(q, k, v, beta_b, g_b, tmats, *cots)


def dn_seq_fwd(inter, proj, dn_norm_w, n_heads, zd_off):
    tp, width = inter[0].shape
    n_chunks = tp // CHUNK

    def body(u_ref, w_ref, qd_ref, kd_ref, qk_ref, gl_ref, z_ref, nw_ref, y_ref, s_ref, state):
        @pl.when(pl.program_id(0) == 0)
        def _():
            state[...] = jnp.zeros_like(state)

        lanes = [slice(h * HEAD_DIM, (h + 1) * HEAD_DIM) for h in range(n_heads)]
        sts = [state[h] for h in range(n_heads)]
        for h, st in enumerate(sts):
            s_ref[0, h] = st
        v_new = [u_ref[:, sl] - mm_nn(w_ref[:, sl], st) for sl, st in zip(lanes, sts)]
        outs = [mm_nn(qd_ref[:, sl], st) + mm_nn(qk_ref[:, sl][:, :CHUNK], vn) for sl, st, vn in zip(lanes, sts, v_new)]
        for h, (sl, st, vn) in enumerate(zip(lanes, sts, v_new)):
            state[h] = st * gl_ref[0:1, sl] + mm_tn(kd_ref[:, sl], vn)
        for sl, o in zip(lanes, outs):
            y_ref[:, sl] = gated_norm(o, nw_ref[...], z_ref[:, sl]).astype(bf16)

    blk = pl.BlockSpec((CHUNK, width), lambda n: (n, 0))
    return _pc(body, name="dn_seq_fwd", grid=(n_chunks,),
               in_specs=[blk] * 6 + [pl.BlockSpec((CHUNK, width), lambda n: (n, zd_off)), pl.BlockSpec((1, HEAD_DIM), lambda n: (0, 0))],
               out_specs=[blk, pl.BlockSpec((1, n_heads, HEAD_DIM, HEAD_DIM), lambda n: (n, 0, 0, 0))],
               out_shape=[jax.ShapeDtypeStruct((tp, width), bf16), jax.ShapeDtypeStruct((n_chunks, n_heads, HEAD_DIM, HEAD_DIM), f32)],
               scratch_shapes=[pltpu.VMEM((n_heads, HEAD_DIM, HEAD_DIM), f32)],
               compiler_params=_params("arbitrary"))(*inter, proj, dn_norm_w)


def dn_seq_bwd(inter, proj, dn_norm_w, states, dy, n_heads, zd_off):
    tp, width = inter[0].shape
    n_chunks = tp // CHUNK
    last = n_chunks - 1

    def body(u_ref, w_ref, qd_ref, kd_ref, qk_ref, gl_ref, z_ref, nw_ref, s_ref, dy_ref,
             du_ref, dw_ref, dqd_ref, dkd_ref, dqk_ref, dgl_ref, dz_ref, dnw_ref, dstate):
        @pl.when(pl.program_id(0) == 0)
        def _():
            dstate[...] = jnp.zeros_like(dstate)
            dnw_ref[...] = jnp.zeros_like(dnw_ref)

        lanes = [slice(h * HEAD_DIM, (h + 1) * HEAD_DIM) for h in range(n_heads)]
        sts = [s_ref[0, h] for h in range(n_heads)]
        dsts = [dstate[h] for h in range(n_heads)]
        v_new = [u_ref[:, sl] - mm_nn(w_ref[:, sl], st) for sl, st in zip(lanes, sts)]
        outs = [mm_nn(qd_ref[:, sl], st) + mm_nn(qk_ref[:, sl][:, :CHUNK], vn) for sl, st, vn in zip(lanes, sts, v_new)]
        dnw = jnp.zeros((1, HEAD_DIM), f32)
        d_outs = []
        for sl, o in zip(lanes, outs):
            do, dn, dz = jax.vjp(gated_norm, o, nw_ref[...], z_ref[:, sl])[1](dy_ref[:, sl])
            dz_ref[:, sl] = dz.astype(bf16)
            dnw = dnw + dn
            d_outs.append(do)
        dnw_ref[...] += dnw
        d_vn = [mm_tn(qk_ref[:, sl][:, :CHUNK], do) + mm_nn(kd_ref[:, sl], ds) for sl, do, ds in zip(lanes, d_outs, dsts)]
        zeros = jnp.zeros((HEAD_DIM - CHUNK, HEAD_DIM), f32)
        rows = lax.broadcasted_iota(jnp.int32, (CHUNK, HEAD_DIM), 0)
        for h, (sl, st, vn, do, ds, dvn) in enumerate(zip(lanes, sts, v_new, d_outs, dsts, d_vn)):
            du_ref[:, sl] = dvn
            dw_ref[:, sl] = -mm_nt(dvn, st)
            dqd_ref[:, sl] = mm_nt(do, st)
            dkd_ref[:, sl] = mm_nt(vn, ds)
            dqk_ref[:, sl] = mm_nt(do, jnp.concatenate([vn, zeros], axis=0))
            dgl_ref[:, sl] = jnp.where(rows == 0, jnp.sum(st * ds, axis=0, keepdims=True), 0.0)
            dstate[h] = ds * gl_ref[0:1, sl] + mm_tn(qd_ref[:, sl], do) - mm_tn(w_ref[:, sl], dvn)

    blk = pl.BlockSpec((CHUNK, width), lambda n: (last - n, 0))
    return _pc(body, name="dn_seq_bwd", grid=(n_chunks,),
               in_specs=[blk] * 6 + [pl.BlockSpec((CHUNK, width), lambda n: (last - n, zd_off)), pl.BlockSpec((1, HEAD_DIM), lambda n: (0, 0)),
                         pl.BlockSpec((1, n_heads, HEAD_DIM, HEAD_DIM), lambda n: (last - n, 0, 0, 0)), blk],
               out_specs=[blk] * 7 + [pl.BlockSpec((1, HEAD_DIM), lambda n: (0, 0))],
               out_shape=[jax.ShapeDtypeStruct((tp, width), f32)] * 6 + [jax.ShapeDtypeStruct((tp, width), bf16),
                                                                          jax.ShapeDtypeStruct((1, HEAD_DIM), f32)],
               scratch_shapes=[pltpu.VMEM((n_heads, HEAD_DIM, HEAD_DIM), f32)],
               compiler_params=_params("arbitrary"))(*inter, proj, dn_norm_w, states, dy)


def loss_stage(out, final_w, target):
    tp, d = out.shape
    n_tiles = tp // CHUNK

    def body(o_ref, w_ref, t_ref, loss_ref, do_ref, dob_ref, dw_ref):
        i = pl.program_id(0)

        @pl.when(i == 0)
        def _():
            loss_ref[...] = jnp.zeros_like(loss_ref)
            dw_ref[...] = jnp.zeros_like(dw_ref)

        scored = (i > 0).astype(f32)
        val, (do, dw) = jax.value_and_grad(lambda o, w: scored * loss_fn(o, w, t_ref[...]), argnums=(0, 1))(o_ref[...], w_ref[...])
        loss_ref[...] += jnp.full(loss_ref.shape, val, f32)
        do_ref[...] = do
        dob_ref[...] = do.astype(bf16)
        dw_ref[...] += dw

    row = pl.BlockSpec((CHUNK, d), lambda i: (i, 0))
    vec = pl.BlockSpec((1, d), lambda i: (0, 0))
    return _pc(body, name="loss_stage", grid=(n_tiles,),
               in_specs=[row, vec, pl.BlockSpec((CHUNK, d), lambda i: (jnp.maximum(i - 1, 0), 0))],
               out_specs=[pl.BlockSpec((1, LANES), lambda i: (0, 0)), row, row, vec],
               out_shape=[jax.ShapeDtypeStruct((1, LANES), f32), jax.ShapeDtypeStruct((tp, d), f32),
                          jax.ShapeDtypeStruct((tp, d), bf16), jax.ShapeDtypeStruct((1, d), f32)],
               compiler_params=_params("arbitrary"))(out, final_w, target)


def _adam_update(w, g, m, v):
    nm = ADAM_B1 * m + (1.0 - ADAM_B1) * g
    nv = ADAM_B2 * v + (1.0 - ADAM_B2) * jnp.square(g)
    m_hat = nm / (1.0 - ADAM_B1 ** ADAM_STEP)
    v_hat = nv / (1.0 - ADAM_B2 ** ADAM_STEP)
    return -ADAM_LR * (m_hat / (jnp.sqrt(v_hat) + ADAM_EPS) + ADAM_WD * w), nm, nv


def adamw(name, w, g, m, v):
    rows, cols = w.shape
    t = _row_tile(rows, cols, 7)

    def body(w_ref, g_ref, m_ref, v_ref, d_ref, nm_ref, nv_ref):
        d_ref[...], nm_ref[...], nv_ref[...] = _adam_update(w_ref[...], g_ref[...], m_ref[...], v_ref[...])

    blk = pl.BlockSpec((t, cols), lambda i: (i, 0))
    return _pc(body, name=name, grid=(rows // t,), in_specs=[blk] * 4, out_specs=[blk] * 3,
               out_shape=[jax.ShapeDtypeStruct(w.shape, f32)] * 3, compiler_params=_params("parallel"))(w, g, m, v)


def adamw_joined(name, w, g_mine, g_theirs, m, v, half, axis):
    rows, cols = w.shape
    hr, hc = g_mine.shape
    tr, tc = _tile(hr, hc, 9)
    nr, nc = hr // tr, hc // tc

    def body(half_ref, w_ref, gm_ref, gt_ref, m_ref, v_ref, g_ref, d_ref, nm_ref, nv_ref):
        pos = pl.program_id(axis) // (nr if axis == 0 else nc)
        g = jnp.where(pos == half_ref[0], gm_ref[...], gt_ref[...])
        delta, nm, nv = _adam_update(w_ref[...], g, m_ref[...], v_ref[...])
        g_ref[...] = g
        d_ref[...] = delta
        nm_ref[...] = nm
        nv_ref[...] = nv

    whole = pl.BlockSpec((tr, tc), lambda i, j, hf: (i, j))
    part = pl.BlockSpec((tr, tc), lambda i, j, hf: (i % nr, j % nc))
    grid_spec = pltpu.PrefetchScalarGridSpec(num_scalar_prefetch=1, grid=(rows // tr, cols // tc),
                                             in_specs=[whole, part, part, whole, whole], out_specs=[whole] * 4)
    return _pc(body, name=name, grid_spec=grid_spec, out_shape=[jax.ShapeDtypeStruct(w.shape, f32)] * 4,
               compiler_params=_params("parallel", "parallel"))(half, w, g_mine, g_theirs, m, v)


def add_halves(name, g, recv, half, kind):
    s, r, c = recv.shape
    tr, tc = _tile(r, c, 3)
    nc = c // tc

    def body(half_ref, g_ref, r_ref, o_ref):
        o_ref[...] = (g_ref[...].reshape(r_ref.shape).astype(f32) + r_ref[...].astype(f32)).astype(bf16)

    if kind == "lead":
        g_spec = pl.BlockSpec((1, 1, tr, tc), lambda i, j, k, hf: (hf[0], i, j, k))
    else:
        g_spec = pl.BlockSpec((1, tr, tc), lambda i, j, k, hf: (i, j, hf[0] * nc + k))
    blk = pl.BlockSpec((1, tr, tc), lambda i, j, k, hf: (i, j, k))
    grid_spec = pltpu.PrefetchScalarGridSpec(num_scalar_prefetch=1, grid=(s, r // tr, nc), in_specs=[g_spec, blk], out_specs=blk)
    return _pc(body, name=name, grid_spec=grid_spec, out_shape=jax.ShapeDtypeStruct((s, r, c), bf16),
               compiler_params=_params("parallel", "parallel", "parallel"))(half, g, recv)


def add_first(name, parts, got, chips):
    _, r, c = parts.shape
    tr, tc = _tile(r, c, 6)

    def body(chips_ref, mine_ref, theirs_ref, got_ref, keep_ref, pass_ref):
        keep_ref[...] = mine_ref[0].astype(f32) + got_ref[0].astype(f32)
        pass_ref[...] = (theirs_ref[0].astype(f32) + got_ref[1].astype(f32)).astype(bf16)

    blk = pl.BlockSpec((tr, tc), lambda i, j, ch: (i, j))
    grid_spec = pltpu.PrefetchScalarGridSpec(
        num_scalar_prefetch=1, grid=(r // tr, c // tc),
        in_specs=[pl.BlockSpec((1, tr, tc), lambda i, j, ch: (ch[0], i, j)), pl.BlockSpec((1, tr, tc), lambda i, j, ch: (ch[1], i, j)),
                  pl.BlockSpec((2, tr, tc), lambda i, j, ch: (0, i, j))],
        out_specs=[blk, blk])
    return _pc(body, name=name, grid_spec=grid_spec, out_shape=[jax.ShapeDtypeStruct((r, c), f32), jax.ShapeDtypeStruct((r, c), bf16)],
               compiler_params=_params("parallel", "parallel"))(chips, parts, parts, got)


def add_second(name, kept, got):
    r, c = kept.shape
    tr, tc = _tile(r, c, 3)

    def body(k_ref, g_ref, o_ref):
        o_ref[...] = k_ref[...] + g_ref[...].astype(f32)

    blk = pl.BlockSpec((tr, tc), lambda i, j: (i, j))
    return _pc(body, name=name, grid=(r // tr, c // tc), in_specs=[blk, blk], out_specs=blk,
               out_shape=jax.ShapeDtypeStruct((r, c), f32), compiler_params=_params("parallel", "parallel"))(kept, got)


def sum_leading(name, x, out_dtype=f32):
    s, r, c = x.shape
    tr, tc = _tile(r, c, s + 1)

    def body(x_ref, o_ref):
        acc = x_ref[0].astype(f32)
        for i in range(1, s):
            acc = acc + x_ref[i].astype(f32)
        o_ref[...] = acc.astype(out_dtype)

    return _pc(body, name=name, grid=(r // tr, c // tc), in_specs=[pl.BlockSpec((s, tr, tc), lambda i, j: (0, i, j))],
               out_specs=pl.BlockSpec((tr, tc), lambda i, j: (i, j)), out_shape=jax.ShapeDtypeStruct((r, c), out_dtype),
               compiler_params=_params("parallel", "parallel"))(x)


def _place():
    x, y, c = lax.axis_index("x"), lax.axis_index("y"), lax.axis_index("c")
    return x, y, c


def _route(x, y, c):
    return (x ^ (1 - c), y ^ c), (x ^ c, y ^ (1 - c)), (1 - x, 1 - y)


def _half_view(ref, half, kind, lead=()):
    if kind == "lead":
        return ref.at[(*lead, half)]
    width = ref.shape[-1] // 2
    return ref.at[(*lead, *([slice(None)] * (len(ref.shape) - len(lead) - 1)), pl.ds(half * width, width))]


def _gather_plan(ins, outs, sems, kinds):
    n = len(ins)
    send1, recv1, send2, recv2 = sems
    x, y, c = _place()
    chip = 2 * x + y
    sibling = (x, y, 1 - c)
    near, far, diag = _route(x, y, c)
    near_id, far_id, diag_id = [2 * cx + cy for cx, cy in (near, far, diag)]

    def remote(src, dst, s_sem, r_sem, to):
        return pltpu.make_async_remote_copy(src_ref=src, dst_ref=dst, send_sem=s_sem, recv_sem=r_sem, device_id=to, device_id_type=MESH)

    def slab(a, chip_id, half):
        return _half_view(outs[a], half, kinds[a], (chip_id,))

    def landed(a, chip_id, sem, frm):
        return remote(slab(a, chip_id, c), slab(a, chip_id, c), send1.at[a, sem], recv1.at[a, sem], (*frm, c))

    def onward(a, chip_id, sem):
        return remote(slab(a, chip_id, c), slab(a, chip_id, c), send2.at[a, sem], recv2.at[a, sem], sibling)

    own = [remote(_half_view(ins[a], c, kinds[a]), slab(a, chip, c), send1.at[a, j], recv1.at[a, j], (*to, c))
           for a in range(n) for j, to in enumerate((near, far))]
    relay = [remote(slab(a, near_id, c), slab(a, near_id, c), send1.at[a, 2], recv1.at[a, 2], (*far, c)) for a in range(n)]
    to_sibling = [[onward(a, cid, sem) for sem, cid in enumerate((near_id, far_id, diag_id))] for a in range(n)]
    from_sibling = [remote(slab(a, cid, 1 - c), slab(a, cid, 1 - c), send2.at[a, sem], recv2.at[a, sem], sibling)
                    for a in range(n) for sem, cid in enumerate((far_id, near_id, diag_id))]

    def start_own():
        for cp in own:
            cp.start()

    def pass_near():
        for a in range(n):
            landed(a, near_id, 0, near).wait_recv()
            relay[a].start()
            to_sibling[a][0].start()

    def pass_far():
        for sem, cid in ((1, far_id), (2, diag_id)):
            for a in range(n):
                landed(a, cid, sem, far).wait_recv()
                to_sibling[a][sem].start()

    def finish():
        for cp in from_sibling:
            cp.wait_recv()
        for cp in own + relay + [cp for row in to_sibling for cp in row]:
            cp.wait_send()

    return [start_own, pass_near, pass_far, finish]


def _gather_shapes(shards):
    return ([jax.ShapeDtypeStruct((4,) + s.shape, s.dtype) for s in shards], [pltpu.SemaphoreType.DMA((len(shards), 3))] * 4)


def gather_weights(shards, kinds):
    n = len(shards)

    def body(*refs):
        for emit in _gather_plan(refs[:n], refs[n:2 * n], refs[2 * n:], kinds):
            emit()

    out_shapes, sems = _gather_shapes(shards)
    return _pc(body, name="gather_weights", in_specs=[_ANY] * n, out_specs=[_ANY] * n, out_shape=out_shapes, scratch_shapes=sems)(*shards)


def hosted_gather(shards, kinds):
    out_shapes, sems = _gather_shapes(shards)
    return Hosted(shards, out_shapes, sems, lambda i, o, s: list(zip((0.0, 0.45, 0.8, 1.0), _gather_plan(i, o, s, kinds))))


def swap_with_sibling(name, sends):
    n = len(sends)

    def body(*refs):
        ins, outs = refs[:n], refs[n:2 * n]
        send, recv = refs[2 * n:]
        x, y, c = _place()
        cps = [pltpu.make_async_remote_copy(src_ref=ins[a], dst_ref=outs[a], send_sem=send.at[a], recv_sem=recv.at[a],
                                            device_id=(x, y, 1 - c), device_id_type=MESH) for a in range(n)]
        for cp in cps:
            cp.start()
        for cp in cps:
            cp.wait()

    return _pc(body, name=name, in_specs=[_ANY] * n, out_specs=[_ANY] * n,
               out_shape=[jax.ShapeDtypeStruct(s.shape, s.dtype) for s in sends],
               scratch_shapes=[pltpu.SemaphoreType.DMA((n,))] * 2)(*sends)


def send_grad_halves(name, grads, kinds):
    n = len(grads)

    def body(*refs):
        ins, outs = refs[:n], refs[n:2 * n]
        send, recv = refs[2 * n:]
        x, y, c = _place()
        cps = [pltpu.make_async_remote_copy(src_ref=_half_view(ins[a], 1 - c, kinds[a]), dst_ref=outs[a], send_sem=send.at[a],
                                            recv_sem=recv.at[a], device_id=(x, y, 1 - c), device_id_type=MESH) for a in range(n)]
        for cp in cps:
            cp.start()
        for cp in cps:
            cp.wait()

    shape = lambda g, kind: g.shape[1:] if kind == "lead" else g.shape[:-1] + (g.shape[-1] // 2,)
    return _pc(body, name=name, in_specs=[_ANY] * n, out_specs=[_ANY] * n,
               out_shape=[jax.ShapeDtypeStruct(shape(g, k), g.dtype) for g, k in zip(grads, kinds)],
               scratch_shapes=[pltpu.SemaphoreType.DMA((n,))] * 2)(*grads)


def _scatter_first_plan(ins, outs, sems):
    n = len(ins)
    send, recv = sems
    x, y, c = _place()
    near, _, diag = _route(x, y, c)
    cps = [pltpu.make_async_remote_copy(src_ref=ins[a].at[2 * cx + cy], dst_ref=outs[a].at[j], send_sem=send.at[a, j],
                                        recv_sem=recv.at[a, j], device_id=(*near, c), device_id_type=MESH)
           for a in range(n) for j, (cx, cy) in enumerate((near, diag))]

    def start():
        for cp in cps:
            cp.start()

    def wait():
        for cp in cps:
            cp.wait()

    return [start, wait]


def _scatter_first_shapes(parts):
    return ([jax.ShapeDtypeStruct((2,) + p.shape[1:], p.dtype) for p in parts], [pltpu.SemaphoreType.DMA((len(parts), 2))] * 2)


def hosted_scatter_first(parts):
    out_shapes, sems = _scatter_first_shapes(parts)
    return Hosted(parts, out_shapes, sems, lambda i, o, s: list(zip((0.0, 1.0), _scatter_first_plan(i, o, s))))


def _far_swap_plan(ins, outs, sems):
    send, recv = sems
    x, y, c = _place()
    _, far, _ = _route(x, y, c)
    cps = [pltpu.make_async_remote_copy(src_ref=ins[a], dst_ref=outs[a], send_sem=send.at[a], recv_sem=recv.at[a],
                                        device_id=(*far, c), device_id_type=MESH) for a in range(len(ins))]

    def start():
        for cp in cps:
            cp.start()

    def wait():
        for cp in cps:
            cp.wait()

    return [start, wait]


def hosted_scatter_second(parts):
    return Hosted(parts, [jax.ShapeDtypeStruct(p.shape, p.dtype) for p in parts], [pltpu.SemaphoreType.DMA((len(parts),))] * 2,
                  lambda i, o, s: list(zip((0.0, 1.0), _far_swap_plan(i, o, s))))


def gather_small(pack):
    def body(pack_ref, packs_ref, send, recv):
        x, y, c = _place()
        me = 4 * x + 2 * y + c
        flips = [(fx, fy, fc) for fx in (0, 1) for fy in (0, 1) for fc in (0, 1)][1:]
        peers = [(x ^ fx, y ^ fy, c ^ fc) for fx, fy, fc in flips]
        cps = [pltpu.make_async_remote_copy(src_ref=pack_ref, dst_ref=packs_ref.at[me], send_sem=send.at[j], recv_sem=recv.at[j],
                                            device_id=p, device_id_type=MESH) for j, p in enumerate(peers)]
        for cp in cps:
            cp.start()
        for j, (px, py, pc) in enumerate(peers):
            slab = packs_ref.at[4 * px + 2 * py + pc]
            pltpu.make_async_remote_copy(src_ref=slab, dst_ref=slab, send_sem=send.at[j], recv_sem=recv.at[j],
                                         device_id=(px, py, pc), device_id_type=MESH).wait_recv()
        for cp in cps:
            cp.wait_send()

    return _pc(body, name="gather_small", in_specs=[_ANY], out_specs=_ANY, out_shape=jax.ShapeDtypeStruct((8,) + pack.shape, pack.dtype),
               scratch_shapes=[pltpu.SemaphoreType.DMA((7,))] * 2)(pack)


def _pack(arrays, width):
    flat = jnp.concatenate([a.reshape(-1).astype(f32) for a in arrays])
    return jnp.pad(flat, (0, (-flat.shape[0]) % (8 * width))).reshape(-1, width)


def _unpack(pack, like, width):
    flat, out, at = pack.reshape(-1), [], 0
    for a in like:
        size = 1
        for s in a.shape:
            size *= s
        out.append(flat[at:at + size].reshape(a.shape))
        at += size
    return out


def _halves(a2d):
    r, c = a2d.shape
    return a2d.reshape(2, r // 2, c)


def local_step(x, meta, norm_w, w_main, w_ba, conv_w, a_log, dt_bias, pool_mix, pool_scale, dn_norm_w,
               late_weights, final_w, target, grad_hooks=None):
    seq, d = x.shape
    pw = pool_scale.shape[1]
    dw = conv_w.shape[1] // 3
    n_heads = dw // HEAD_DIM
    tp = FRONT_PAD + N_META + seq
    hp = jnp.concatenate([jnp.zeros((FRONT_PAD, d), f32), meta, x], axis=0)
    tm_big = tp // 2 if tp % 32 == 0 else tp
    tm_norm = max(t for t in range(16, min(tp, 352) + 1, 16) if tp % t == 0)
    tile = min(512, d)
    off_q = 2 * pw
    off_zd = off_q + 3 * dw
    off_gp = off_zd + dw
    off_gd = off_gp + d
    a_log128 = jnp.pad(a_log, ((0, 0), (0, LANES - n_heads)))
    dt128 = jnp.pad(dt_bias, ((0, 0), (0, LANES - n_heads)))

    xn = norm_fwd(hp, norm_w, tm_norm)
    if isinstance(late_weights[0], Hosted):
        proj, *fetched = matmul("proj", xn, w_main, tb=True, tm=tp, tn=tile, hosted=late_weights[0])
        w_pool_out, w_dn_out, w_o = late_weights[1](fetched)
    else:
        proj = matmul("proj", xn, w_main, tb=True, tm=tp, tn=tile)
        w_pool_out, w_dn_out, w_o = late_weights
    ba = matmul("proj_ba", xn, w_ba, tb=True, tm=tp, tn=LANES)
    y_pool = pool_fwd(proj, pool_mix, pool_scale, pw)
    q, k, v, beta_b, g_b = dn_pre_fwd(proj, ba, conv_w, a_log128, dt128, n_heads, off_q // HEAD_DIM)
    assert off_zd % dw == 0
    *inter, tmats = dn_intra_fwd(q, k, v, beta_b, g_b, n_heads)
    y_dn, states = dn_seq_fwd(inter, proj, dn_norm_w, n_heads, off_zd // dw)
    a_mat = matmul("pool_out", y_pool, w_pool_out, tm=tp, tn=tile)

    def merge(acc, a_t, gp_t, gd_t):
        return acc, sigmoid(gp_t) * a_t + sigmoid(gd_t) * acc

    b_mat, merged = matmul("dn_out_merge", y_dn, w_dn_out, tm=tm_big, tn=tile, extras=[(a_mat, 0), (proj, off_gp), (proj, off_gd)],
                           epi=merge, out_dtypes=(f32, bf16))
    out = matmul("out_proj", merged, w_o, tm=tm_big, tn=tile, extras=[(hp, 0)], epi=lambda acc, h_t: (acc + h_t,))
    loss, dout, dout_b, dfinal_w = loss_stage(out, final_w, target)

    def unmerge(dm, a_t, b_t, gp_t, gd_t):
        sp, sd = sigmoid(gp_t), sigmoid(gd_t)
        return dm * sp, dm * sd, dm * a_t * sp * (1.0 - sp), dm * b_t * sd * (1.0 - sd)

    d_a, d_b, d_gp, d_gd = matmul("d_merged", dout_b, w_o, tb=True, tm=tm_big, tn=tile,
                                  extras=[(a_mat, 0), (b_mat, 0), (proj, off_gp), (proj, off_gd)], epi=unmerge,
                                  out_dtypes=(bf16,) * 4)
    g_w_o = matmul("g_w_o", merged, dout_b, ta=True, tm=tile, tn=d, out_dtypes=(bf16,))
    d_y_pool = matmul("d_y_pool", d_a, w_pool_out, tb=True, tm=tp, tn=tile)
    g_w_pool_out = matmul("g_w_pool_out", y_pool, d_a, ta=True, tm=tile, tn=d, out_dtypes=(bf16,))
    d_y_dn = matmul("d_y_dn", d_b, w_dn_out, tb=True, tm=tp, tn=tile)
    g_w_dn_out = matmul("g_w_dn_out", y_dn, d_b, ta=True, tm=tile, tn=d, out_dtypes=(bf16,))
    d_u, d_zp, g_pool_mix, g_pool_scale = pool_bwd(proj, pool_mix, pool_scale, d_y_pool, pw)
    *d_inter, d_zd, g_dn_norm_w = dn_seq_bwd(inter, proj, dn_norm_w, states, d_y_dn, n_heads, off_zd // dw)
    d_q, d_k, d_v, d_beta, d_g = dn_intra_bwd(q, k, v, beta_b, g_b, tmats, d_inter, n_heads)
    d_qr, d_kr, d_vr, d_ba, g_cq, g_ck, g_cv, g_a_log, g_dt = dn_pre_bwd(
        proj, ba, conv_w, a_log128, dt128, (d_q, d_k, d_v, d_beta, d_g), n_heads, off_q // HEAD_DIM)
    d_proj = jnp.concatenate([d_u, d_zp, d_qr, d_kr, d_vr, d_zd, d_gp, d_gd], axis=1)
    d_ba_b = cast_bf16("cast_d_ba", d_ba)
    early = grad_hooks[0](g_w_pool_out, g_w_dn_out, g_w_o, g_pool_mix) if grad_hooks else None
    res = matmul("g_w_main", d_proj, xn, ta=True, tm=tile, tn=d, out_dtypes=(bf16,), hosted=early)
    g_w_main, early_landed = (res[0], list(res[1:])) if early else (res, [])
    g_w_ba = matmul("g_w_ba", d_ba_b, xn, ta=True, tm=LANES, tn=tile, out_dtypes=(bf16,))
    hosted = grad_hooks[1](g_w_main, g_w_ba, early_landed) if grad_hooks else None
    dxn_ba = matmul("dxn_ba", d_ba_b, w_ba, tm=tp, tn=tile)
    n_cols = d_proj.shape[1]
    tk_dxn = max(t for t in range(LANES, min(2048, n_cols) + 1, LANES) if n_cols % t == 0)
    res = matmul("dxn", d_proj, w_main, tm=tp, tn=tile, tk=tk_dxn, extras=[(dxn_ba, 0)], epi=lambda acc, e: (acc + e,), hosted=hosted)
    dxn, landed = (res[0], list(res[1:])) if hosted else (res, [])
    dh, g_norm_w, last_landed = norm_bwd(hp, norm_w, dxn, dout, tm_norm, hosted=grad_hooks[2](landed) if grad_hooks else None)
    g_conv = jnp.concatenate([g_cq, g_ck, g_cv], axis=1)
    return (loss, dh, g_norm_w, g_w_main, g_w_ba, g_conv, g_a_log[:, :n_heads], g_dt[:, :n_heads], g_pool_mix, g_pool_scale,
            g_dn_norm_w, g_w_pool_out, g_w_dn_out, g_w_o, dfinal_w, last_landed)


def kernel(x, meta_tokens, norm_w, w_in, conv_w, A_log, dt_bias, pool_mix, pool_scale, dn_norm_w, w_pool_out, w_dn_out, w_o, final_norm_w, loss_target, m_meta_tokens, m_norm_w, m_w_in, m_conv_w, m_A_log, m_dt_bias, m_pool_mix, m_pool_scale, m_dn_norm_w, m_w_pool_out, m_w_dn_out, m_w_o, m_final_norm_w, v_meta_tokens, v_norm_w, v_w_in, v_conv_w, v_A_log, v_dt_bias, v_pool_mix, v_pool_scale, v_dn_norm_w, v_w_pool_out, v_w_dn_out, v_w_o, v_final_norm_w):
    d = x.shape[-1]
    pw = pool_scale.shape[-1]
    dw = w_dn_out.shape[1] * 4
    n_heads = dw // HEAD_DIM
    gdim = pw // POOL_GROUPS
    chip = 2 * lax.axis_index("x") + lax.axis_index("y")
    core = lax.axis_index("c")

    half = core.astype(jnp.int32).reshape(1)
    me = 4 * lax.axis_index("x") + 2 * lax.axis_index("y") + core

    w_in_t = w_in[0].T
    mix_s = pool_mix[0].reshape(POOL_GROUPS * (gdim // 4), gdim)
    small_s = _pack([meta_tokens, conv_w[0]], d)
    small_rows = small_s.shape[0]
    small_s = jnp.pad(small_s, ((0, (-small_rows) % 16), (0, 0)))
    sw = w_in_t.shape[0]
    n_main, n_ba = 2 * pw + 4 * dw, 2 * n_heads

    early = [cast_bf16("cast_w_in", w_in_t), _halves(cast_bf16("cast_mix", mix_s)), _halves(small_s)]
    late = [_halves(cast_bf16("cast_w_po", w_pool_out[0])), _halves(cast_bf16("cast_w_do", w_dn_out[0])),
            _halves(cast_bf16("cast_w_o", w_o[0]))]
    fill = lambda g, own: lax.dynamic_update_slice(g, own[None], (chip,) + (0,) * own.ndim)
    g_in, g_mix, g_small = [fill(g, own) for g, own in zip(gather_weights(early, ["cols", "lead", "lead"]), early)]

    def shard_rows(lo, hi):
        cut = [(max(lo, j * sw), min(hi, (j + 1) * sw), j) for j in range(4)]
        return [g_in[j, a - j * sw:b - j * sw] for a, b, j in cut if a < b]

    w_main = jnp.concatenate(shard_rows(0, n_main) + shard_rows(n_main + n_ba, 4 * sw), axis=0)
    w_ba = jnp.pad(jnp.concatenate(shard_rows(n_main, n_main + n_ba), axis=0), ((0, LANES - n_ba), (0, 0)))
    cat_cols = lambda g: jnp.concatenate([g[j].reshape(-1, g.shape[-1]) for j in range(4)], axis=1)

    def late_weights(fetched):
        g_po, g_do, g_o = [fill(g, own) for g, own in zip(fetched, late)]
        return cat_cols(g_po), g_do.reshape(-1, g_do.shape[-1]), g_o.reshape(-1, g_o.shape[-1])

    mix_full = g_mix.reshape(4, POOL_GROUPS, gdim // 4, gdim).transpose(1, 0, 2, 3).reshape(POOL_GROUPS, gdim, gdim)
    smalls = [_unpack(g_small[j].reshape(-1, d)[:small_rows], [meta_tokens, conv_w[0]], d) for j in range(4)]
    meta_full = jnp.concatenate([s[0] for s in smalls], axis=1)
    conv_full = jnp.concatenate([s[1] for s in smalls], axis=1)

    names = ["w_in", "w_po", "w_do", "w_o", "mix"]
    kinds = ["cols", "lead", "lead", "lead", "lead"]

    far_chip = 2 * (lax.axis_index("x") ^ core) + (lax.axis_index("y") ^ (1 - core))
    chips = jnp.stack([chip, far_chip]).astype(jnp.int32)
    col_parts = lambda g: g.reshape(2, g.shape[0] // 2, 4, g.shape[1] // 4).transpose(0, 2, 1, 3)
    row_parts = lambda g: g.reshape(4, 2, g.shape[0] // 8, g.shape[1]).transpose(1, 0, 2, 3)

    def pair_sums(tag, group, parts):
        group_kinds = [kinds[names.index(nm)] for nm in group]
        from_sibling = send_grad_halves("send_grad_halves_" + tag, parts, group_kinds)
        return [add_halves("add_" + nm, p, r, half, k) for nm, p, r, k in zip(group, parts, from_sibling, group_kinds)]

    later = names[1:]
    state = {}

    def later_grads(g_w_po, g_w_do, g_w_o_full, g_mix_full):
        mix_rows = POOL_GROUPS * (gdim // 4)
        mix_parts = (g_mix_full.astype(bf16).reshape(POOL_GROUPS, 4, gdim // 4, gdim).transpose(1, 0, 2, 3)
                     .reshape(4, 2, mix_rows // 2, gdim).transpose(1, 0, 2, 3))
        state["later_sums"] = pair_sums("later", later, [col_parts(g_w_po), row_parts(g_w_do), row_parts(g_w_o_full), mix_parts])
        return hosted_scatter_first(state["later_sums"])

    def input_grads(g_w_main, g_w_ba, later_from_near):
        def grad_rows(lo, hi):
            segs = [(0, n_main, g_w_main, 0), (n_main, n_main + n_ba, g_w_ba, 0), (n_main + n_ba, 4 * sw, g_w_main, n_main)]
            cut = [(max(lo, s0), min(hi, s1), s0, arr, off) for s0, s1, arr, off in segs]
            return [arr[a - s0 + off:b - s0 + off] for a, b, s0, arr, off in cut if a < b]

        kept, passed = zip(*[add_first("add1_" + nm, p, g, chips) for nm, p, g in zip(later, state["later_sums"], later_from_near)])
        state["later_kept"] = list(kept)
        in_parts = jnp.stack([jnp.concatenate(grad_rows(j * sw, (j + 1) * sw), axis=0) for j in range(4)])
        state["in_sums"] = pair_sums("input", names[:1], [in_parts])
        return hosted_scatter_first(state["in_sums"]) + hosted_scatter_second(list(passed))

    def input_second(landed):
        state["later_from_far"] = landed[1:]
        state["in_kept"], in_passed = add_first("add1_w_in", state["in_sums"][0], landed[0], chips)
        return hosted_scatter_second([in_passed])

    (loss, dh, g_norm_w, _, _, g_conv, g_a_log, g_dt, _, g_pool_scale, g_dn_norm_w, _, _, _, g_final_w,
     (in_from_far,)) = local_step(x[0], meta_full, norm_w, w_main, w_ba, conv_full, A_log, dt_bias, mix_full, pool_scale,
                                  dn_norm_w, (hosted_gather(late, ["lead"] * 3), late_weights), final_norm_w.reshape(1, d),
                                  loss_target[0], grad_hooks=(later_grads, input_grads, input_second))
    grad_x = dh[FRONT_PAD + N_META:][None]
    g_meta = dh[FRONT_PAD:FRONT_PAD + N_META]

    mine = [add_second("add2_" + nm, k_, g) for nm, k_, g in
            zip(names, [state["in_kept"]] + state["later_kept"], [in_from_far] + list(state["later_from_far"]))]
    small_like = [loss, g_norm_w, g_a_log, g_dt, g_pool_scale, g_dn_norm_w, g_final_w, g_conv, g_meta]
    pack = _pack(small_like, d)
    packs = gather_small(pack)
    theirs = swap_with_sibling("swap_grad_halves", mine)
    total = sum_leading("sum_small", lax.dynamic_update_slice(packs, pack[None], (me, 0, 0)))
    (loss_t, g_norm_w, g_a_log, g_dt, g_pool_scale, g_dn_norm_w, g_final_w, g_conv, g_meta) = _unpack(total, small_like, d)
    loss_out = loss_t[0, 0]
    g_conv_s = lax.dynamic_slice_in_dim(g_conv, chip * (g_conv.shape[1] // 4), g_conv.shape[1] // 4, axis=1)
    g_meta_s = lax.dynamic_slice_in_dim(g_meta, chip * (d // 4), d // 4, axis=1)

    weights = [meta_tokens, norm_w, w_in, conv_w, A_log, dt_bias, pool_mix, pool_scale, dn_norm_w, w_pool_out, w_dn_out, w_o, final_norm_w]
    ms = [m_meta_tokens, m_norm_w, m_w_in, m_conv_w, m_A_log, m_dt_bias, m_pool_mix, m_pool_scale, m_dn_norm_w, m_w_pool_out, m_w_dn_out, m_w_o, m_final_norm_w]
    vs = [v_meta_tokens, v_norm_w, v_w_in, v_conv_w, v_A_log, v_dt_bias, v_pool_mix, v_pool_scale, v_dn_norm_w, v_w_pool_out, v_w_dn_out, v_w_o, v_final_norm_w]
    grads = [g_meta_s, g_norm_w, None, g_conv_s[None], g_a_log, g_dt, None, g_pool_scale, g_dn_norm_w, None, None, None, g_final_w.reshape(d)]
    deltas, new_ms, new_vs = [None] * 13, [None] * 13, [None] * 13
    big = [2, 9, 10, 11, 6]
    for i, nm, g_mine, g_theirs in zip(big, names, mine, theirs):
        if nm == "w_in":
            to2d, back, axis = (lambda t: t[0].T), (lambda t: t.T[None]), 1
        else:
            to2d, back, axis = (lambda t: t.reshape(-1, t.shape[-1])), (lambda t, i=i: t.reshape(weights[i].shape)), 0
        res = adamw_joined("adamw_" + nm, to2d(weights[i]), g_mine, g_theirs, to2d(ms[i]), to2d(vs[i]), half, axis)
        grads[i], deltas[i], new_ms[i], new_vs[i] = [back(t) for t in res]
    small_idx = [i for i in range(13) if i not in big]
    packs = [_pack([arrs[i] for i in small_idx], d) for arrs in (weights, grads, ms, vs)]
    outs = adamw("adamw_small", *packs)
    like = [weights[i] for i in small_idx]
    for res, dest in zip(outs, (deltas, new_ms, new_vs)):
        for i, val in zip(small_idx, _unpack(res, like, d)):
            dest[i] = val
    return (loss_out, grad_x, *grads, *deltas, *new_ms, *new_vs)
```

```python
import functools

import jax
import jax.numpy as jnp
from jax import lax
from jax.experimental import pallas as pl
from jax.experimental.pallas import tpu as pltpu

f32 = jnp.float32
bf16 = jnp.bfloat16
MESH = pl.DeviceIdType.MESH

N_META = 16
CHUNK = 64
FRONT_PAD = (-N_META) % CHUNK
HEAD_DIM = 128
POOL_GROUPS = 4
POOL_WINDOWS = (2, 4, 8, 16)
CONV_WIDTH = 4
NORM_EPS = 1e-6
ADAM_LR, ADAM_B1, ADAM_B2, ADAM_EPS, ADAM_WD, ADAM_STEP = 0.001, 0.9, 0.999, 1e-08, 0.01, 10
LANES = 128
V7X_VMEM_BYTES = 64 * 2**20
VMEM_LIMIT = V7X_VMEM_BYTES - 8 * 2**20


def _pc(body, **kw):
    return pl.pallas_call(body, **kw)


def _params(*sem, **kw):
    return pltpu.CompilerParams(dimension_semantics=sem or None, vmem_limit_bytes=VMEM_LIMIT, **kw)


def _dg(a, b, dims):
    return lax.dot_general(a, b, (dims, ((), ())), preferred_element_type=f32)


@jax.custom_vjp
def mm_nn(a, b):
    return _dg(a.astype(bf16), b.astype(bf16), ((1,), (0,)))


@jax.custom_vjp
def mm_nt(a, b):
    return _dg(a.astype(bf16), b.astype(bf16), ((1,), (1,)))


@jax.custom_vjp
def mm_tn(a, b):
    return _dg(a.astype(bf16).T, b.astype(bf16), ((1,), (0,)))


mm_nn.defvjp(lambda a, b: (mm_nn(a, b), (a, b)), lambda r, dy: (mm_nt(dy, r[1]), mm_tn(r[0], dy)))
mm_nt.defvjp(lambda a, b: (mm_nt(a, b), (a, b)), lambda r, dy: (mm_nn(dy, r[1]), mm_tn(dy, r[0])))
mm_tn.defvjp(lambda a, b: (mm_tn(a, b), (a, b)), lambda r, dy: (mm_nt(r[1], dy), mm_nn(r[0], dy)))


def _split3(x):
    hi = x.astype(bf16)
    r1 = x - hi.astype(f32)
    mid = r1.astype(bf16)
    lo = (r1 - mid.astype(f32)).astype(bf16)
    return hi, mid, lo


def _split2(x):
    hi = x.astype(bf16)
    return hi, (x - hi.astype(f32)).astype(bf16)


@jax.custom_vjp
def mm_sel(sel, x):
    s = sel.astype(bf16)
    d = ((1,), (0,))
    hi, lo = _split2(x)
    return _dg(s, hi, d) + _dg(s, lo, d)


def _mm_sel_bwd(sel, dy):
    s = sel.astype(bf16)
    d = ((0,), (0,))
    hi, lo = _split2(dy)
    return jnp.zeros_like(sel), _dg(s, hi, d) + _dg(s, lo, d)


mm_sel.defvjp(lambda sel, x: (mm_sel(sel, x), sel), _mm_sel_bwd)


@jax.custom_vjp
def mm_pick(x, sel):
    s = sel.astype(bf16)
    d = ((1,), (0,))
    hi, mid, lo = _split3(x)
    return _dg(hi, s, d) + _dg(mid, s, d) + _dg(lo, s, d)


def _mm_pick_bwd(sel, dy):
    s = sel.astype(bf16)
    d = ((1,), (1,))
    hi, mid, lo = _split3(dy)
    return _dg(hi, s, d) + _dg(mid, s, d) + _dg(lo, s, d), jnp.zeros_like(sel)


mm_pick.defvjp(lambda x, sel: (mm_pick(x, sel), sel), _mm_pick_bwd)


def tri_inv(ls):
    n = ls[0].shape[0]
    eye = (lax.broadcasted_iota(jnp.int32, (n, n), 0) == lax.broadcasted_iota(jnp.int32, (n, n), 1)).astype(f32)
    ms = [-l for l in ls]
    ts = [eye + m for m in ms]
    k = 1
    while 2 * k < CHUNK:
        ms = [mm_nn(m, m) for m in ms]
        ts = [t + mm_nn(t, m) for t, m in zip(ts, ms)]
        k *= 2
    return ts


@functools.partial(jax.custom_vjp, nondiff_argnums=(1,))
def shift_rows(x, j):
    n = x.shape[0]
    rows = lax.broadcasted_iota(jnp.int32, x.shape, 0)
    if j >= 0:
        return jnp.where(rows >= j, pltpu.roll(x, j, 0), 0.0)
    return jnp.where(rows < n + j, pltpu.roll(x, n + j, 0), 0.0)


shift_rows.defvjp(lambda x, j: (shift_rows(x, j), None), lambda j, _, dy: (shift_rows(dy, -j),))


def sigmoid(x):
    return 1.0 / (1.0 + jnp.exp(-x))


def silu(x):
    return x * sigmoid(x)


def softplus(x):
    return jnp.maximum(x, 0.0) + jnp.log(1.0 + jnp.exp(-jnp.abs(x)))


def rmsnorm(x, w):
    return x * lax.rsqrt(jnp.mean(x * x, axis=-1, keepdims=True) + NORM_EPS) * w


def l2norm(x):
    return x * lax.rsqrt(jnp.sum(x * x, axis=-1, keepdims=True) + NORM_EPS)


def pool_fn(u, zp, mix, scale, group):
    rows = lax.broadcasted_iota(jnp.int32, u.shape, 0)
    sums = []
    s, w = u, 1
    while w < POOL_WINDOWS[-1]:
        s = s + shift_rows(s, w)
        w *= 2
        sums.append(s)
    total = sums[-1]
    for gi in range(POOL_GROUPS - 2, -1, -1):
        total = jnp.where(group == gi, sums[gi], total)
    window = jnp.left_shift(2, group)
    cnt = jnp.clip(rows - (FRONT_PAD - 1), 1, window).astype(f32)
    pooled = total / cnt - u
    return mm_nn(pooled, mix) * scale * silu(zp)


PRE_HALO = 8


def conv_silu(x, w):
    k = CONV_WIDTH
    y = x * w[k - 1:k, :]
    for kk in range(k - 1):
        y = y + shift_rows(x, k - 1 - kk) * w[kk:kk + 1, :]
    return silu(y[PRE_HALO:])


def _lane_pick(row, idx):
    lanes = lax.broadcasted_iota(jnp.int32, row.shape, 1)
    return jnp.sum(jnp.where(lanes == idx, row, 0.0), axis=1, keepdims=True)


def dn_pre_fn(qr, kr, vr, ba, cwq, cwk, cwv, a_log, dt_bias, head, n_heads, row0):
    q = l2norm(conv_silu(qr, cwq)) * (HEAD_DIM ** -0.5)
    k = l2norm(conv_silu(kr, cwk))
    v = conv_silu(vr, cwv)
    r = lax.broadcasted_iota(jnp.int32, (LANES, LANES), 0)
    b_b = mm_pick(ba, (r == head).astype(f32))
    a_b = mm_pick(ba, (r == head + n_heads).astype(f32))
    real = lax.broadcasted_iota(jnp.int32, ba.shape, 0) + row0 >= FRONT_PAD
    beta_b = jnp.where(real, sigmoid(b_b), 0.0)
    g_b = jnp.where(real, -jnp.exp(_lane_pick(a_log, head)) * softplus(a_b + _lane_pick(dt_bias, head)), 0.0)
    return q, k, v, beta_b, g_b


def _chunk_masks(rows):
    r = lax.broadcasted_iota(jnp.int32, (rows, rows), 0)
    c = lax.broadcasted_iota(jnp.int32, (rows, rows), 1)
    same = (r // CHUNK) == (c // CHUNK)
    return same, jnp.logical_and(same, r >= c), jnp.logical_and(same, r > c)


def _lane0(rows):
    return (lax.broadcasted_iota(jnp.int32, (rows, LANES), 1) == 0).astype(bf16)


@jax.custom_vjp
def lane0_as_row(x):
    sel = _lane0(x.shape[0])
    d = ((1,), (1,))
    hi, lo = _split2(x)
    return _dg(sel, hi, d) + _dg(sel, lo, d)


def _lane0_as_row_bwd(rows, dy):
    sel = _lane0(rows)
    d = ((0,), (0,))
    hi, lo = _split2(dy)
    return (_dg(hi, sel, d) + _dg(lo, sel, d),)


lane0_as_row.defvjp(lambda x: (lane0_as_row(x), x.shape[0]), _lane0_as_row_bwd)


def gate_fn(g_b):
    rows = g_b.shape[0]
    same, causal, _ = _chunk_masks(rows)
    gcum_b = mm_sel(causal.astype(f32), g_b)
    glast_b = mm_sel(same.astype(f32), g_b)
    g_rows = jnp.broadcast_to(gcum_b[:, :1], (rows, rows))
    decay = jnp.where(causal, jnp.exp(jnp.where(causal, g_rows - lane0_as_row(gcum_b), 0.0)), 0.0)
    return decay, jnp.exp(gcum_b), jnp.exp(glast_b - gcum_b), jnp.exp(glast_b)


def _fold_matrix(rows):
    r = lax.broadcasted_iota(jnp.int32, (rows, LANES), 0)
    c = lax.broadcasted_iota(jnp.int32, (rows, LANES), 1)
    return (r % CHUNK == c).astype(bf16)


@jax.custom_vjp
def fold_chunks(x):
    return _dg(x.astype(bf16), _fold_matrix(x.shape[0]), ((1,), (0,)))


def _fold_chunks_bwd(rows, dy):
    fold = _fold_matrix(rows)
    d = ((1,), (1,))
    hi, mid, lo = _split3(dy)
    return (_dg(hi, fold, d) + _dg(mid, fold, d) + _dg(lo, fold, d),)


fold_chunks.defvjp(lambda x: (fold_chunks(x), x.shape[0]), _fold_chunks_bwd)


def lmat_fn(k, beta_b, decay):
    _, _, strict = _chunk_masks(k.shape[0])
    return jnp.where(strict, mm_nt(k * beta_b, k) * decay, 0.0)


def intra_fn(tmat, q, k, v, beta_b, decay, eg, kfac):
    _, causal, _ = _chunk_masks(q.shape[0])
    k_beta = k * beta_b
    u_c = mm_nn(tmat, v * beta_b)
    w_c = mm_nn(tmat, k_beta * eg)
    qk = jnp.where(causal, mm_nt(q, k) * decay, 0.0)
    return u_c, w_c, q * eg, k * kfac, fold_chunks(qk)


def gated_norm(o, norm_w, zd):
    return rmsnorm(o, norm_w) * silu(zd)


def loss_fn(o, w, tgt):
    err = rmsnorm(o, w) - tgt
    return 0.5 * jnp.sum(jnp.mean(err * err, axis=-1))


_ANY = pl.BlockSpec(memory_space=pl.ANY)


class Hosted:
    def __init__(self, arrays, out_shapes, sems, stages):
        self.arrays, self.out_shapes, self.sems, self.stages = list(arrays), list(out_shapes), list(sems), stages

    def __add__(self, other):
        ni, no, ns = len(self.arrays), len(self.out_shapes), len(self.sems)
        stages = lambda i, o, s: self.stages(i[:ni], o[:no], s[:ns]) + other.stages(i[ni:], o[no:], s[ns:])
        return Hosted(self.arrays + other.arrays, self.out_shapes + other.out_shapes, self.sems + other.sems, stages)


def matmul(name, a, b, *, ta=False, tb=False, tm, tn, tk=None, extras=(), epi=None, out_dtypes=(f32,), hosted=None):
    m, k = (a.shape[1], a.shape[0]) if ta else a.shape
    n = b.shape[0] if tb else b.shape[1]
    tm, tn, tk = min(tm, m), min(tn, n), min(tk or k, k)
    assert m % tm == 0 and n % tn == 0 and k % tk == 0, (name, m, n, k, tm, tn, tk)
    nm, nn, nk = m // tm, n // tn, k // tk
    a_spec = pl.BlockSpec((tk, tm), lambda i, j, kk: (kk, i)) if ta else pl.BlockSpec((tm, tk), lambda i, j, kk: (i, kk))
    b_spec = pl.BlockSpec((tn, tk), lambda i, j, kk: (j, kk)) if tb else pl.BlockSpec((tk, tn), lambda i, j, kk: (kk, j))
    ex_specs = []
    for _, off in extras:
        assert off % tn == 0, (name, off, tn)
        ex_specs.append(pl.BlockSpec((tm, tn), functools.partial(lambda i, j, kk, o: (i, o + j), o=off // tn)))
    n_ex, n_out = len(extras), len(out_dtypes)
    dims = ((0 if ta else 1,), (1 if tb else 0,))
    n_hin = len(hosted.arrays) if hosted else 0
    n_hout = len(hosted.out_shapes) if hosted else 0
    n_sem = len(hosted.sems) if hosted else 0

    def body(a_ref, b_ref, *rest):
        ex_refs, rest = rest[:n_ex], rest[n_ex:]
        hin_refs, rest = rest[:n_hin], rest[n_hin:]
        out_refs, rest = rest[:n_out], rest[n_out:]
        hout_refs, rest = rest[:n_hout], rest[n_hout:]
        sem_refs = rest[len(rest) - n_sem:] if n_sem else ()
        step = (pl.program_id(0) * nn + pl.program_id(1)) * nk + pl.program_id(2)
        stages = hosted.stages(hin_refs, hout_refs, sem_refs) if hosted else []
        for frac, emit in stages:
            if frac < 1.0:
                pl.when(step == int(frac * (nm * nn * nk - 1)))(emit)

        def finish(acc):
            res = epi(acc, *[r[...] for r in ex_refs]) if epi is not None else (acc,)
            for o_ref, val in zip(out_refs, res):
                o_ref[...] = val.astype(o_ref.dtype)

        p = _dg(a_ref[...], b_ref[...], dims)
        if nk == 1:
            finish(p)
        else:
            acc_ref = rest[0]
            kk = pl.program_id(2)

            @pl.when(kk == 0)
            def _():
                acc_ref[...] = p

            @pl.when(kk > 0)
            def _():
                acc_ref[...] += p

            @pl.when(kk == nk - 1)
            def _():
                finish(acc_ref[...])

        for frac, emit in stages:
            if frac >= 1.0:
                pl.when(step == nm * nn * nk - 1)(emit)

    outs = _pc(
        body, name=name, grid=(nm, nn, nk),
        in_specs=[a_spec, b_spec] + ex_specs + [_ANY] * n_hin,
        out_specs=[pl.BlockSpec((tm, tn), lambda i, j, kk: (i, j))] * n_out + [_ANY] * n_hout,
        out_shape=[jax.ShapeDtypeStruct((m, n), dt) for dt in out_dtypes] + (hosted.out_shapes if hosted else []),
        scratch_shapes=([pltpu.VMEM((tm, tn), f32)] if nk > 1 else []) + (hosted.sems if hosted else []),
        compiler_params=_params(*(("arbitrary",) * 3 if hosted else ("parallel", "parallel", "arbitrary"))),
    )(a, b, *[e for e, _ in extras], *(hosted.arrays if hosted else []))
    return outs[0] if len(outs) == 1 else outs


def _row_tile(rows, cols, n_arrays, itemsize=4, budget=24 * 2**20):
    best = None
    for t in range(16, rows + 1, 16):
        if rows % t == 0 and 2 * n_arrays * t * cols * itemsize <= budget:
            best = t
    return best or rows


def _tile(rows, cols, n_arrays, budget=24 * 2**20):
    if rows % 16 == 0 or cols % LANES != 0:
        return _row_tile(rows, cols, n_arrays, budget=budget), cols
    fits = [t for t in range(LANES, cols + 1, LANES) if cols % t == 0 and 2 * n_arrays * rows * t * 4 <= budget]
    return rows, (max(fits) if fits else LANES)


def cast_bf16(name, x):
    rows, cols = x.shape
    tr, tc = _tile(rows, cols, 2)

    def body(x_ref, o_ref):
        o_ref[...] = x_ref[...].astype(bf16)

    blk = pl.BlockSpec((tr, tc), lambda i, j: (i, j))
    return _pc(body, name=name, grid=(rows // tr, cols // tc), in_specs=[blk], out_specs=blk,
               out_shape=jax.ShapeDtypeStruct(x.shape, bf16), compiler_params=_params("parallel", "parallel"))(x)


def norm_fwd(hp, norm_w, tm):
    tp, d = hp.shape

    def body(h_ref, w_ref, o_ref):
        o_ref[...] = rmsnorm(h_ref[...], w_ref[...]).astype(bf16)

    return _pc(body, name="norm_fwd", grid=(tp // tm,),
               in_specs=[pl.BlockSpec((tm, d), lambda i: (i, 0)), pl.BlockSpec((1, d), lambda i: (0, 0))],
               out_specs=pl.BlockSpec((tm, d), lambda i: (i, 0)), out_shape=jax.ShapeDtypeStruct((tp, d), bf16),
               compiler_params=_params("parallel"))(hp, norm_w)


def norm_bwd(hp, norm_w, dxn, dout, tm, hosted=None):
    tp, d = hp.shape
    steps = tp // tm
    n_hin = len(hosted.arrays) if hosted else 0
    n_hout = len(hosted.out_shapes) if hosted else 0

    def body(h_ref, w_ref, dxn_ref, dout_ref, *rest):
        hin_refs, (dh_ref, dw_ref), rest = rest[:n_hin], rest[n_hin:n_hin + 2], rest[n_hin + 2:]
        stages = hosted.stages(hin_refs, rest[:n_hout], rest[n_hout:]) if hosted else []
        for frac, emit in stages:
            if frac < 1.0:
                pl.when(pl.program_id(0) == int(frac * (steps - 1)))(emit)
        _, vjp = jax.vjp(rmsnorm, h_ref[...], w_ref[...])
        dh, dw = vjp(dxn_ref[...])
        dh_ref[...] = dh + dout_ref[...]

        @pl.when(pl.program_id(0) == 0)
        def _():
            dw_ref[...] = jnp.zeros_like(dw_ref)

        dw_ref[...] += dw
        for frac, emit in stages:
            if frac >= 1.0:
                pl.when(pl.program_id(0) == steps - 1)(emit)

    row = pl.BlockSpec((tm, d), lambda i: (i, 0))
    vec = pl.BlockSpec((1, d), lambda i: (0, 0))
    outs = _pc(body, name="norm_bwd", grid=(steps,), in_specs=[row, vec, row, row] + [_ANY] * n_hin,
               out_specs=[row, vec] + [_ANY] * n_hout,
               out_shape=[jax.ShapeDtypeStruct((tp, d), f32), jax.ShapeDtypeStruct((1, d), f32)] + (hosted.out_shapes if hosted else []),
               scratch_shapes=hosted.sems if hosted else [],
               compiler_params=_params("arbitrary"))(hp, norm_w, dxn, dout, *(hosted.arrays if hosted else []))
    return outs[0], outs[1], list(outs[2:])


def pool_fwd(proj, mix, scale, pw):
    tp = proj.shape[0]
    g = pw // POOL_GROUPS

    def body(u_ref, z_ref, mix_ref, s_ref, y_ref):
        y_ref[...] = pool_fn(u_ref[...], z_ref[...], mix_ref[0], s_ref[...], pl.program_id(0)).astype(bf16)

    return _pc(body, name="pool_fwd", grid=(POOL_GROUPS,),
               in_specs=[pl.BlockSpec((tp, g), lambda i: (0, i)), pl.BlockSpec((tp, g), lambda i: (0, POOL_GROUPS + i)),
                         pl.BlockSpec((1, g, g), lambda i: (i, 0, 0)), pl.BlockSpec((1, g), lambda i: (0, i))],
               out_specs=pl.BlockSpec((tp, g), lambda i: (0, i)), out_shape=jax.ShapeDtypeStruct((tp, pw), bf16),
               compiler_params=_params("parallel"))(proj, proj, mix, scale)


def pool_bwd(proj, mix, scale, dy, pw):
    tp = proj.shape[0]
    g = pw // POOL_GROUPS

    def body(u_ref, z_ref, mix_ref, s_ref, dy_ref, du_ref, dz_ref, dmix_ref, ds_ref):
        grp = pl.program_id(0)
        _, vjp = jax.vjp(lambda u, z, m, s: pool_fn(u, z, m, s, grp), u_ref[...], z_ref[...], mix_ref[0].astype(f32), s_ref[...])
        du, dz, dmix, ds = vjp(dy_ref[...])
        du_ref[...] = du.astype(bf16)
        dz_ref[...] = dz.astype(bf16)
        dmix_ref[0] = dmix
        ds_ref[...] = ds

    col = pl.BlockSpec((tp, g), lambda i: (0, i))
    return _pc(body, name="pool_bwd", grid=(POOL_GROUPS,),
               in_specs=[col, pl.BlockSpec((tp, g), lambda i: (0, POOL_GROUPS + i)),
                         pl.BlockSpec((1, g, g), lambda i: (i, 0, 0)), pl.BlockSpec((1, g), lambda i: (0, i)), col],
               out_specs=[col, col, pl.BlockSpec((1, g, g), lambda i: (i, 0, 0)), pl.BlockSpec((1, g), lambda i: (0, i))],
               out_shape=[jax.ShapeDtypeStruct((tp, pw), bf16), jax.ShapeDtypeStruct((tp, pw), bf16),
                          jax.ShapeDtypeStruct((POOL_GROUPS, g, g), f32), jax.ShapeDtypeStruct((1, pw), f32)],
               compiler_params=_params("parallel"))(proj, proj, mix, scale, dy)


def _dn_pre_specs(tp, n_heads, q_off):
    hb = lambda off: pl.BlockSpec((tp, HEAD_DIM), functools.partial(lambda h, o: (0, o + h), o=off))
    cw = lambda off: pl.BlockSpec((CONV_WIDTH, HEAD_DIM), functools.partial(lambda h, o: (0, o + h), o=off))
    whole = lambda shape: pl.BlockSpec(shape, lambda h: (0, 0))
    return ([hb(q_off), hb(q_off + n_heads), hb(q_off + 2 * n_heads), whole((tp, LANES)),
             cw(0), cw(n_heads), cw(2 * n_heads), whole((1, LANES)), whole((1, LANES))], hb, cw, whole)


def _pre_rows(tp):
    return max(t for t in range(16, min(tp, 192) + 1, 16) if tp % t == 0)


def _with_history(ref, r0, rows):
    if r0 == 0:
        return jnp.concatenate([jnp.zeros((PRE_HALO, ref.shape[1]), f32), ref[0:rows, :]], axis=0)
    return ref[r0 - PRE_HALO:r0 + rows, :]


def dn_pre_fwd(proj, ba, conv_w, a_log, dt_bias, n_heads, q_off):
    tp = proj.shape[0]
    rows = _pre_rows(tp)
    in_specs, hb, _, _ = _dn_pre_specs(tp, n_heads, q_off)

    def body(q_ref, k_ref, v_ref, ba_ref, cq_ref, ck_ref, cv_ref, al_ref, dt_ref, *out_refs):
        for r0 in range(0, tp, rows):
            outs = dn_pre_fn(_with_history(q_ref, r0, rows), _with_history(k_ref, r0, rows), _with_history(v_ref, r0, rows),
                             ba_ref[r0:r0 + rows, :], cq_ref[...], ck_ref[...], cv_ref[...], al_ref[...], dt_ref[...],
                             pl.program_id(0), n_heads, r0)
            for o_ref, val in zip(out_refs, outs):
                o_ref[r0:r0 + rows, :] = val

    return _pc(body, name="dn_pre_fwd", grid=(n_heads,), in_specs=in_specs, out_specs=[hb(0)] * 5,
               out_shape=[jax.ShapeDtypeStruct((tp, n_heads * HEAD_DIM), f32)] * 5,
               compiler_params=_params("parallel"))(proj, proj, proj, ba, conv_w, conv_w, conv_w, a_log, dt_bias)


def dn_pre_bwd(proj, ba, conv_w, a_log, dt_bias, cots, n_heads, q_off):
    tp = proj.shape[0]
    rows = _pre_rows(tp)
    in_specs, hb, cw, whole = _dn_pre_specs(tp, n_heads, q_off)

    def body(q_ref, k_ref, v_ref, ba_ref, cq_ref, ck_ref, cv_ref, al_ref, dt_ref, dq_ref, dk_ref, dv_ref, db_ref, dg_ref,
             dqr_ref, dkr_ref, dvr_ref, dba_ref, dcq_ref, dck_ref, dcv_ref, dal_ref, ddt_ref):
        head = pl.program_id(0)

        @pl.when(head == 0)
        def _():
            dba_ref[...] = jnp.zeros_like(dba_ref)
            dal_ref[...] = jnp.zeros_like(dal_ref)
            ddt_ref[...] = jnp.zeros_like(ddt_ref)

        owed = [jnp.zeros((PRE_HALO, HEAD_DIM), f32)] * 3
        d_conv = [jnp.zeros((CONV_WIDTH, HEAD_DIM), f32)] * 3
        d_al, d_dt = jnp.zeros((1, LANES), f32), jnp.zeros((1, LANES), f32)
        for r0 in reversed(range(0, tp, rows)):
            fn = lambda *args, r0=r0: dn_pre_fn(*args, head, n_heads, r0)
            _, vjp = jax.vjp(fn, _with_history(q_ref, r0, rows), _with_history(k_ref, r0, rows), _with_history(v_ref, r0, rows),
                             ba_ref[r0:r0 + rows, :], cq_ref[...], ck_ref[...], cv_ref[...], al_ref[...], dt_ref[...])
            tile = slice(r0, r0 + rows)
            *d_raw, dba, dcq, dck, dcv, dal, ddt = vjp((dq_ref[tile, :], dk_ref[tile, :], dv_ref[tile, :], db_ref[tile, :], dg_ref[tile, :]))
            for i, (out_ref, d) in enumerate(zip((dqr_ref, dkr_ref, dvr_ref), d_raw)):
                out_ref[tile, :] = jnp.concatenate([d[PRE_HALO:rows], d[rows:] + owed[i]], axis=0).astype(bf16)
                owed[i] = d[:PRE_HALO]
            dba_ref[tile, :] += dba
            d_conv = [acc + d for acc, d in zip(d_conv, (dcq, dck, dcv))]
            d_al, d_dt = d_al + dal, d_dt + ddt
        dcq_ref[...], dck_ref[...], dcv_ref[...] = d_conv
        dal_ref[...] += d_al
        ddt_ref[...] += d_dt

    w = n_heads * HEAD_DIM
    return _pc(body, name="dn_pre_bwd", grid=(n_heads,), in_specs=in_specs + [hb(0)] * 5,
               out_specs=[hb(0)] * 3 + [whole((tp, LANES)), cw(0), cw(0), cw(0), whole((1, LANES)), whole((1, LANES))],
               out_shape=[jax.ShapeDtypeStruct((tp, w), bf16)] * 3 + [jax.ShapeDtypeStruct((tp, LANES), f32)]
               + [jax.ShapeDtypeStruct((CONV_WIDTH, w), f32)] * 3 + [jax.ShapeDtypeStruct((1, LANES), f32)] * 2,
               compiler_params=_params("arbitrary"))(proj, proj, proj, ba, conv_w, conv_w, conv_w, a_log, dt_bias, *cots)


def _super_rows(tp):
    n = tp // CHUNK
    return CHUNK * max(j for j in (4, 3, 2, 1) if n % j == 0)


def _heads_per_step(n_heads):
    return max(j for j in (8, 4, 2, 1) if n_heads % j == 0)


def dn_intra_fwd(q, k, v, beta_b, g_b, n_heads):
    tp = q.shape[0]
    rows = _super_rows(tp)
    ns = tp // rows

    hps = _heads_per_step(n_heads)

    def body(q_ref, k_ref, v_ref, b_ref, g_ref, u_ref, w_ref, qd_ref, kd_ref, qk_ref, gl_ref, t_ref):
        lanes = [slice(i * HEAD_DIM, (i + 1) * HEAD_DIM) for i in range(hps)]
        gates = [gate_fn(g_ref[:, sl]) for sl in lanes]
        tmats = tri_inv([lmat_fn(k_ref[:, sl], b_ref[:, sl], gt[0]) for sl, gt in zip(lanes, gates)])
        for i, (sl, (decay, eg, kfac, gl), tmat) in enumerate(zip(lanes, gates, tmats)):
            u_c, w_c, q_dec, k_dec, qk_c = intra_fn(tmat, q_ref[:, sl], k_ref[:, sl], v_ref[:, sl], b_ref[:, sl], decay, eg, kfac)
            u_ref[:, sl] = u_c
            w_ref[:, sl] = w_c.astype(bf16)
            qd_ref[:, sl] = q_dec.astype(bf16)
            kd_ref[:, sl] = k_dec.astype(bf16)
            qk_ref[:, sl] = qk_c.astype(bf16)
            gl_ref[:, sl] = gl
            t_ref[i, 0] = tmat

    blk = pl.BlockSpec((rows, hps * HEAD_DIM), lambda h, s: (s, h))
    return _pc(body, name="dn_intra_fwd", grid=(n_heads // hps, ns), in_specs=[blk] * 5,
               out_specs=[blk] * 6 + [pl.BlockSpec((hps, 1, rows, rows), lambda h, s: (h, s, 0, 0))],
               out_shape=[jax.ShapeDtypeStruct(q.shape, dt) for dt in (f32, bf16, bf16, bf16, bf16, f32)]
               + [jax.ShapeDtypeStruct((n_heads, ns, rows, rows), f32)],
               compiler_params=_params("parallel", "parallel"))(q, k, v, beta_b, g_b)


def dn_intra_bwd(q, k, v, beta_b, g_b, tmats, cots, n_heads):
    tp = q.shape[0]
    rows = _super_rows(tp)
    ns = tp // rows

    hps = _heads_per_step(n_heads)

    def body(q_ref, k_ref, v_ref, b_ref, g_ref, t_ref, du_ref, dw_ref, dqd_ref, dkd_ref, dqk_ref, dgl_ref,
             dq_ref, dk_ref, dv_ref, db_ref, dg_ref):
        lanes = [slice(i * HEAD_DIM, (i + 1) * HEAD_DIM) for i in range(hps)]
        tmats = [t_ref[i, 0] for i in range(hps)]
        gates = [jax.vjp(gate_fn, g_ref[:, sl]) for sl in lanes]
        intra = [jax.vjp(intra_fn, tmat, q_ref[:, sl], k_ref[:, sl], v_ref[:, sl], b_ref[:, sl], gt[0][0], gt[0][1], gt[0][2])[1](
            (du_ref[:, sl], dw_ref[:, sl], dqd_ref[:, sl], dkd_ref[:, sl], dqk_ref[:, sl]))
            for sl, tmat, gt in zip(lanes, tmats, gates)]
        tts = [tmat.T for tmat in tmats]
        dls = [mm_nn(tt, res[0]) for tt, res in zip(tts, intra)]
        dls = [-mm_nn(dl, tt) for dl, tt in zip(dls, tts)]
        for sl, gt, res, dl in zip(lanes, gates, intra, dls):
            _, dq, dk, dv, db, ddecay, deg, dkfac = res
            dk2, db2, ddecay2 = jax.vjp(lmat_fn, k_ref[:, sl], b_ref[:, sl], gt[0][0])[1](dl)
            (dg,) = gt[1]((ddecay + ddecay2, deg, dkfac, dgl_ref[:, sl]))
            dq_ref[:, sl] = dq
            dk_ref[:, sl] = dk + dk2
            dv_ref[:, sl] = dv
            db_ref[:, sl] = db + db2
            dg_ref[:, sl] = dg

    blk = pl.BlockSpec((rows, hps * HEAD_DIM), lambda h, s: (s, h))
    return _pc(body, name="dn_intra_bwd", grid=(n_heads // hps, ns),
               in_specs=[blk] * 5 + [pl.BlockSpec((hps, 1, rows, rows), lambda h, s: (h, s, 0, 0))] + [blk] * 6,
               out_specs=[blk] * 5, out_shape=[jax.ShapeDtypeStruct(q.shape, f32)] * 5,
               compiler_params=_params("parallel", "parallel"))(q, k, v, beta_b, g_b, tmats, *cots)


def dn_seq_fwd(inter, proj, dn_norm_w, n_heads, zd_off):
    tp, width = inter[0].shape
    n_chunks = tp // CHUNK

    def body(u_ref, w_ref, qd_ref, kd_ref, qk_ref, gl_ref, z_ref, nw_ref, y_ref, s_ref, state):
        @pl.when(pl.program_id(0) == 0)
        def _():
            state[...] = jnp.zeros_like(state)

        lanes = [slice(h * HEAD_DIM, (h + 1) * HEAD_DIM) for h in range(n_heads)]
        sts = [state[h] for h in range(n_heads)]
        for h, st in enumerate(sts):
            s_ref[0, h] = st
        v_new = [u_ref[:, sl] - mm_nn(w_ref[:, sl], st) for sl, st in zip(lanes, sts)]
        outs = [mm_nn(qd_ref[:, sl], st) + mm_nn(qk_ref[:, sl][:, :CHUNK], vn) for sl, st, vn in zip(lanes, sts, v_new)]
        for h, (sl, st, vn) in enumerate(zip(lanes, sts, v_new)):
            state[h] = st * gl_ref[0:1, sl] + mm_tn(kd_ref[:, sl], vn)
        for sl, o in zip(lanes, outs):
            y_ref[:, sl] = gated_norm(o, nw_ref[...], z_ref[:, sl]).astype(bf16)

    blk = pl.BlockSpec((CHUNK, width), lambda n: (n, 0))
    return _pc(body, name="dn_seq_fwd", grid=(n_chunks,),
               in_specs=[blk] * 6 + [pl.BlockSpec((CHUNK, width), lambda n: (n, zd_off)), pl.BlockSpec((1, HEAD_DIM), lambda n: (0, 0))],
               out_specs=[blk, pl.BlockSpec((1, n_heads, HEAD_DIM, HEAD_DIM), lambda n: (n, 0, 0, 0))],
               out_shape=[jax.ShapeDtypeStruct((tp, width), bf16), jax.ShapeDtypeStruct((n_chunks, n_heads, HEAD_DIM, HEAD_DIM), f32)],
               scratch_shapes=[pltpu.VMEM((n_heads, HEAD_DIM, HEAD_DIM), f32)],
               compiler_params=_params("arbitrary"))(*inter, proj, dn_norm_w)


def dn_seq_bwd(inter, proj, dn_norm_w, states, dy, n_heads, zd_off):
    tp, width = inter[0].shape
    n_chunks = tp // CHUNK
    last = n_chunks - 1

    def body(u_ref, w_ref, qd_ref, kd_ref, qk_ref, gl_ref, z_ref, nw_ref, s_ref, dy_ref,
             du_ref, dw_ref, dqd_ref, dkd_ref, dqk_ref, dgl_ref, dz_ref, dnw_ref, dstate):
        @pl.when(pl.program_id(0) == 0)
        def _():
            dstate[...] = jnp.zeros_like(dstate)
            dnw_ref[...] = jnp.zeros_like(dnw_ref)

        lanes = [slice(h * HEAD_DIM, (h + 1) * HEAD_DIM) for h in range(n_heads)]
        sts = [s_ref[0, h] for h in range(n_heads)]
        dsts = [dstate[h] for h in range(n_heads)]
        v_new = [u_ref[:, sl] - mm_nn(w_ref[:, sl], st) for sl, st in zip(lanes, sts)]
        outs = [mm_nn(qd_ref[:, sl], st) + mm_nn(qk_ref[:, sl][:, :CHUNK], vn) for sl, st, vn in zip(lanes, sts, v_new)]
        dnw = jnp.zeros((1, HEAD_DIM), f32)
        d_outs = []
        for sl, o in zip(lanes, outs):
            do, dn, dz = jax.vjp(gated_norm, o, nw_ref[...], z_ref[:, sl])[1](dy_ref[:, sl])
            dz_ref[:, sl] = dz.astype(bf16)
            dnw = dnw + dn
            d_outs.append(do)
        dnw_ref[...] += dnw
        d_vn = [mm_tn(qk_ref[:, sl][:, :CHUNK], do) + mm_nn(kd_ref[:, sl], ds) for sl, do, ds in zip(lanes, d_outs, dsts)]
        zeros = jnp.zeros((HEAD_DIM - CHUNK, HEAD_DIM), f32)
        rows = lax.broadcasted_iota(jnp.int32, (CHUNK, HEAD_DIM), 0)
        for h, (sl, st, vn, do, ds, dvn) in enumerate(zip(lanes, sts, v_new, d_outs, dsts, d_vn)):
            du_ref[:, sl] = dvn
            dw_ref[:, sl] = -mm_nt(dvn, st)
            dqd_ref[:, sl] = mm_nt(do, st)
            dkd_ref[:, sl] = mm_nt(vn, ds)
            dqk_ref[:, sl] = mm_nt(do, jnp.concatenate([vn, zeros], axis=0))
            dgl_ref[:, sl] = jnp.where(rows == 0, jnp.sum(st * ds, axis=0, keepdims=True), 0.0)
            dstate[h] = ds * gl_ref[0:1, sl] + mm_tn(qd_ref[:, sl], do) - mm_tn(w_ref[:, sl], dvn)

    blk = pl.BlockSpec((CHUNK, width), lambda n: (last - n, 0))
    return _pc(body, name="dn_seq_bwd", grid=(n_chunks,),
               in_specs=[blk] * 6 + [pl.BlockSpec((CHUNK, width), lambda n: (last - n, zd_off)), pl.BlockSpec((1, HEAD_DIM), lambda n: (0, 0)),
                         pl.BlockSpec((1, n_heads, HEAD_DIM, HEAD_DIM), lambda n: (last - n, 0, 0, 0)), blk],
               out_specs=[blk] * 7 + [pl.BlockSpec((1, HEAD_DIM), lambda n: (0, 0))],
               out_shape=[jax.ShapeDtypeStruct((tp, width), f32)] * 6 + [jax.ShapeDtypeStruct((tp, width), bf16),
                                                                          jax.ShapeDtypeStruct((1, HEAD_DIM), f32)],
               scratch_shapes=[pltpu.VMEM((n_heads, HEAD_DIM, HEAD_DIM), f32)],
               compiler_params=_params("arbitrary"))(*inter, proj, dn_norm_w, states, dy)


def loss_stage(out, final_w, target):
    tp, d = out.shape
    n_tiles = tp // CHUNK

    def body(o_ref, w_ref, t_ref, loss_ref, do_ref, dob_ref, dw_ref):
        i = pl.program_id(0)

        @pl.when(i == 0)
        def _():
            loss_ref[...] = jnp.zeros_like(loss_ref)
            dw_ref[...] = jnp.zeros_like(dw_ref)

        scored = (i > 0).astype(f32)
        val, (do, dw) = jax.value_and_grad(lambda o, w: scored * loss_fn(o, w, t_ref[...]), argnums=(0, 1))(o_ref[...], w_ref[...])
        loss_ref[...] += jnp.full(loss_ref.shape, val, f32)
        do_ref[...] = do
        dob_ref[...] = do.astype(bf16)
        dw_ref[...] += dw

    row = pl.BlockSpec((CHUNK, d), lambda i: (i, 0))
    vec = pl.BlockSpec((1, d), lambda i: (0, 0))
    return _pc(body, name="loss_stage", grid=(n_tiles,),
               in_specs=[row, vec, pl.BlockSpec((CHUNK, d), lambda i: (jnp.maximum(i - 1, 0), 0))],
               out_specs=[pl.BlockSpec((1, LANES), lambda i: (0, 0)), row, row, vec],
               out_shape=[jax.ShapeDtypeStruct((1, LANES), f32), jax.ShapeDtypeStruct((tp, d), f32),
                          jax.ShapeDtypeStruct((tp, d), bf16), jax.ShapeDtypeStruct((1, d), f32)],
               compiler_params=_params("arbitrary"))(out, final_w, target)


def _adam_update(w, g, m, v):
    nm = ADAM_B1 * m + (1.0 - ADAM_B1) * g
    nv = ADAM_B2 * v + (1.0 - ADAM_B2) * jnp.square(g)
    m_hat = nm / (1.0 - ADAM_B1 ** ADAM_STEP)
    v_hat = nv / (1.0 - ADAM_B2 ** ADAM_STEP)
    return -ADAM_LR * (m_hat / (jnp.sqrt(v_hat) + ADAM_EPS) + ADAM_WD * w), nm, nv


def adamw(name, w, g, m, v):
    rows, cols = w.shape
    t = _row_tile(rows, cols, 7)

    def body(w_ref, g_ref, m_ref, v_ref, d_ref, nm_ref, nv_ref):
        d_ref[...], nm_ref[...], nv_ref[...] = _adam_update(w_ref[...], g_ref[...], m_ref[...], v_ref[...])

    blk = pl.BlockSpec((t, cols), lambda i: (i, 0))
    return _pc(body, name=name, grid=(rows // t,), in_specs=[blk] * 4, out_specs=[blk] * 3,
               out_shape=[jax.ShapeDtypeStruct(w.shape, f32)] * 3, compiler_params=_params("parallel"))(w, g, m, v)


def adamw_joined(name, w, g_mine, g_theirs, m, v, half, axis):
    rows, cols = w.shape
    hr, hc = g_mine.shape
    tr, tc = _tile(hr, hc, 9)
    nr, nc = hr // tr, hc // tc

    def body(half_ref, w_ref, gm_ref, gt_ref, m_ref, v_ref, g_ref, d_ref, nm_ref, nv_ref):
        pos = pl.program_id(axis) // (nr if axis == 0 else nc)
        g = jnp.where(pos == half_ref[0], gm_ref[...], gt_ref[...])
        delta, nm, nv = _adam_update(w_ref[...], g, m_ref[...], v_ref[...])
        g_ref[...] = g
        d_ref[...] = delta
        nm_ref[...] = nm
        nv_ref[...] = nv

    whole = pl.BlockSpec((tr, tc), lambda i, j, hf: (i, j))
    part = pl.BlockSpec((tr, tc), lambda i, j, hf: (i % nr, j % nc))
    grid_spec = pltpu.PrefetchScalarGridSpec(num_scalar_prefetch=1, grid=(rows // tr, cols // tc),
                                             in_specs=[whole, part, part, whole, whole], out_specs=[whole] * 4)
    return _pc(body, name=name, grid_spec=grid_spec, out_shape=[jax.ShapeDtypeStruct(w.shape, f32)] * 4,
               compiler_params=_params("parallel", "parallel"))(half, w, g_mine, g_theirs, m, v)


def add_halves(name, g, recv, half, kind):
    s, r, c = recv.shape
    tr, tc = _tile(r, c, 3)
    nc = c // tc

    def body(half_ref, g_ref, r_ref, o_ref):
        o_ref[...] = (g_ref[...].reshape(r_ref.shape).astype(f32) + r_ref[...].astype(f32)).astype(bf16)

    if kind == "lead":
        g_spec = pl.BlockSpec((1, 1, tr, tc), lambda i, j, k, hf: (hf[0], i, j, k))
    else:
        g_spec = pl.BlockSpec((1, tr, tc), lambda i, j, k, hf: (i, j, hf[0] * nc + k))
    blk = pl.BlockSpec((1, tr, tc), lambda i, j, k, hf: (i, j, k))
    grid_spec = pltpu.PrefetchScalarGridSpec(num_scalar_prefetch=1, grid=(s, r // tr, nc), in_specs=[g_spec, blk], out_specs=blk)
    return _pc(body, name=name, grid_spec=grid_spec, out_shape=jax.ShapeDtypeStruct((s, r, c), bf16),
               compiler_params=_params("parallel", "parallel", "parallel"))(half, g, recv)


def add_first(name, parts, got, chips):
    _, r, c = parts.shape
    tr, tc = _tile(r, c, 6)

    def body(chips_ref, mine_ref, theirs_ref, got_ref, keep_ref, pass_ref):
        keep_ref[...] = mine_ref[0].astype(f32) + got_ref[0].astype(f32)
        pass_ref[...] = (theirs_ref[0].astype(f32) + got_ref[1].astype(f32)).astype(bf16)

    blk = pl.BlockSpec((tr, tc), lambda i, j, ch: (i, j))
    grid_spec = pltpu.PrefetchScalarGridSpec(
        num_scalar_prefetch=1, grid=(r // tr, c // tc),
        in_specs=[pl.BlockSpec((1, tr, tc), lambda i, j, ch: (ch[0], i, j)), pl.BlockSpec((1, tr, tc), lambda i, j, ch: (ch[1], i, j)),
                  pl.BlockSpec((2, tr, tc), lambda i, j, ch: (0, i, j))],
        out_specs=[blk, blk])
    return _pc(body, name=name, grid_spec=grid_spec, out_shape=[jax.ShapeDtypeStruct((r, c), f32), jax.ShapeDtypeStruct((r, c), bf16)],
               compiler_params=_params("parallel", "parallel"))(chips, parts, parts, got)


def add_second(name, kept, got):
    r, c = kept.shape
    tr, tc = _tile(r, c, 3)

    def body(k_ref, g_ref, o_ref):
        o_ref[...] = k_ref[...] + g_ref[...].astype(f32)

    blk = pl.BlockSpec((tr, tc), lambda i, j: (i, j))
    return _pc(body, name=name, grid=(r // tr, c // tc), in_specs=[blk, blk], out_specs=blk,
               out_shape=jax.ShapeDtypeStruct((r, c), f32), compiler_params=_params("parallel", "parallel"))(kept, got)


def sum_leading(name, x, out_dtype=f32):
    s, r, c = x.shape
    tr, tc = _tile(r, c, s + 1)

    def body(x_ref, o_ref):
        acc = x_ref[0].astype(f32)
        for i in range(1, s):
            acc = acc + x_ref[i].astype(f32)
        o_ref[...] = acc.astype(out_dtype)

    return _pc(body, name=name, grid=(r // tr, c // tc), in_specs=[pl.BlockSpec((s, tr, tc), lambda i, j: (0, i, j))],
               out_specs=pl.BlockSpec((tr, tc), lambda i, j: (i, j)), out_shape=jax.ShapeDtypeStruct((r, c), out_dtype),
               compiler_params=_params("parallel", "parallel"))(x)


def _place():
    x, y, c = lax.axis_index("x"), lax.axis_index("y"), lax.axis_index("c")
    return x, y, c


def _route(x, y, c):
    return (x ^ (1 - c), y ^ c), (x ^ c, y ^ (1 - c)), (1 - x, 1 - y)


def _half_view(ref, half, kind, lead=()):
    if kind == "lead":
        return ref.at[(*lead, half)]
    width = ref.shape[-1] // 2
    return ref.at[(*lead, *([slice(None)] * (len(ref.shape) - len(lead) - 1)), pl.ds(half * width, width))]


def _gather_plan(ins, outs, sems, kinds):
    n = len(ins)
    send1, recv1, send2, recv2 = sems
    x, y, c = _place()
    chip = 2 * x + y
    sibling = (x, y, 1 - c)
    near, far, diag = _route(x, y, c)
    near_id, far_id, diag_id = [2 * cx + cy for cx, cy in (near, far, diag)]

    def remote(src, dst, s_sem, r_sem, to):
        return pltpu.make_async_remote_copy(src_ref=src, dst_ref=dst, send_sem=s_sem, recv_sem=r_sem, device_id=to, device_id_type=MESH)

    def slab(a, chip_id, half):
        return _half_view(outs[a], half, kinds[a], (chip_id,))

    def landed(a, chip_id, sem, frm):
        return remote(slab(a, chip_id, c), slab(a, chip_id, c), send1.at[a, sem], recv1.at[a, sem], (*frm, c))

    def onward(a, chip_id, sem):
        return remote(slab(a, chip_id, c), slab(a, chip_id, c), send2.at[a, sem], recv2.at[a, sem], sibling)

    own = [remote(_half_view(ins[a], c, kinds[a]), slab(a, chip, c), send1.at[a, j], recv1.at[a, j], (*to, c))
           for a in range(n) for j, to in enumerate((near, far))]
    relay = [remote(slab(a, near_id, c), slab(a, near_id, c), send1.at[a, 2], recv1.at[a, 2], (*far, c)) for a in range(n)]
    to_sibling = [[onward(a, cid, sem) for sem, cid in enumerate((near_id, far_id, diag_id))] for a in range(n)]
    from_sibling = [remote(slab(a, cid, 1 - c), slab(a, cid, 1 - c), send2.at[a, sem], recv2.at[a, sem], sibling)
                    for a in range(n) for sem, cid in enumerate((far_id, near_id, diag_id))]

    def start_own():
        for cp in own:
            cp.start()

    def pass_near():
        for a in range(n):
            landed(a, near_id, 0, near).wait_recv()
            relay[a].start()
            to_sibling[a][0].start()

    def pass_far():
        for sem, cid in ((1, far_id), (2, diag_id)):
            for a in range(n):
                landed(a, cid, sem, far).wait_recv()
                to_sibling[a][sem].start()

    def finish():
        for cp in from_sibling:
            cp.wait_recv()
        for cp in own + relay + [cp for row in to_sibling for cp in row]:
            cp.wait_send()

    return [start_own, pass_near, pass_far, finish]


def _gather_shapes(shards):
    return ([jax.ShapeDtypeStruct((4,) + s.shape, s.dtype) for s in shards], [pltpu.SemaphoreType.DMA((len(shards), 3))] * 4)


def gather_weights(shards, kinds):
    n = len(shards)

    def body(*refs):
        for emit in _gather_plan(refs[:n], refs[n:2 * n], refs[2 * n:], kinds):
            emit()

    out_shapes, sems = _gather_shapes(shards)
    return _pc(body, name="gather_weights", in_specs=[_ANY] * n, out_specs=[_ANY] * n, out_shape=out_shapes, scratch_shapes=sems)(*shards)


def hosted_gather(shards, kinds):
    out_shapes, sems = _gather_shapes(shards)
    return Hosted(shards, out_shapes, sems, lambda i, o, s: list(zip((0.0, 0.45, 0.8, 1.0), _gather_plan(i, o, s, kinds))))


def swap_with_sibling(name, sends):
    n = len(sends)

    def body(*refs):
        ins, outs = refs[:n], refs[n:2 * n]
        send, recv = refs[2 * n:]
        x, y, c = _place()
        cps = [pltpu.make_async_remote_copy(src_ref=ins[a], dst_ref=outs[a], send_sem=send.at[a], recv_sem=recv.at[a],
                                            device_id=(x, y, 1 - c), device_id_type=MESH) for a in range(n)]
        for cp in cps:
            cp.start()
        for cp in cps:
            cp.wait()

    return _pc(body, name=name, in_specs=[_ANY] * n, out_specs=[_ANY] * n,
               out_shape=[jax.ShapeDtypeStruct(s.shape, s.dtype) for s in sends],
               scratch_shapes=[pltpu.SemaphoreType.DMA((n,))] * 2)(*sends)


def send_grad_halves(name, grads, kinds):
    n = len(grads)

    def body(*refs):
        ins, outs = refs[:n], refs[n:2 * n]
        send, recv = refs[2 * n:]
        x, y, c = _place()
        cps = [pltpu.make_async_remote_copy(src_ref=_half_view(ins[a], 1 - c, kinds[a]), dst_ref=outs[a], send_sem=send.at[a],
                                            recv_sem=recv.at[a], device_id=(x, y, 1 - c), device_id_type=MESH) for a in range(n)]
        for cp in cps:
            cp.start()
        for cp in cps:
            cp.wait()

    shape = lambda g, kind: g.shape[1:] if kind == "lead" else g.shape[:-1] + (g.shape[-1] // 2,)
    return _pc(body, name=name, in_specs=[_ANY] * n, out_specs=[_ANY] * n,
               out_shape=[jax.ShapeDtypeStruct(shape(g, k), g.dtype) for g, k in zip(grads, kinds)],
               scratch_shapes=[pltpu.SemaphoreType.DMA((n,))] * 2)(*grads)


def _scatter_first_plan(ins, outs, sems):
    n = len(ins)
    send, recv = sems
    x, y, c = _place()
    near, _, diag = _route(x, y, c)
    cps = [pltpu.make_async_remote_copy(src_ref=ins[a].at[2 * cx + cy], dst_ref=outs[a].at[j], send_sem=send.at[a, j],
                                        recv_sem=recv.at[a, j], device_id=(*near, c), device_id_type=MESH)
           for a in range(n) for j, (cx, cy) in enumerate((near, diag))]

    def start():
        for cp in cps:
            cp.start()

    def wait():
        for cp in cps:
            cp.wait()

    return [start, wait]


def _scatter_first_shapes(parts):
    return ([jax.ShapeDtypeStruct((2,) + p.shape[1:], p.dtype) for p in parts], [pltpu.SemaphoreType.DMA((len(parts), 2))] * 2)


def hosted_scatter_first(parts):
    out_shapes, sems = _scatter_first_shapes(parts)
    return Hosted(parts, out_shapes, sems, lambda i, o, s: list(zip((0.0, 1.0), _scatter_first_plan(i, o, s))))


def _far_swap_plan(ins, outs, sems):
    send, recv = sems
    x, y, c = _place()
    _, far, _ = _route(x, y, c)
    cps = [pltpu.make_async_remote_copy(src_ref=ins[a], dst_ref=outs[a], send_sem=send.at[a], recv_sem=recv.at[a],
                                        device_id=(*far, c), device_id_type=MESH) for a in range(len(ins))]

    def start():
        for cp in cps:
            cp.start()

    def wait():
        for cp in cps:
            cp.wait()

    return [start, wait]


def hosted_scatter_second(parts):
    return Hosted(parts, [jax.ShapeDtypeStruct(p.shape, p.dtype) for p in parts], [pltpu.SemaphoreType.DMA((len(parts),))] * 2,
                  lambda i, o, s: list(zip((0.0, 1.0), _far_swap_plan(i, o, s))))


def gather_small(pack):
    def body(pack_ref, packs_ref, send, recv):
        x, y, c = _place()
        me = 4 * x + 2 * y + c
        flips = [(fx, fy, fc) for fx in (0, 1) for fy in (0, 1) for fc in (0, 1)][1:]
        peers = [(x ^ fx, y ^ fy, c ^ fc) for fx, fy, fc in flips]
        cps = [pltpu.make_async_remote_copy(src_ref=pack_ref, dst_ref=packs_ref.at[me], send_sem=send.at[j], recv_sem=recv.at[j],
                                            device_id=p, device_id_type=MESH) for j, p in enumerate(peers)]
        for cp in cps:
            cp.start()
        for j, (px, py, pc) in enumerate(peers):
            slab = packs_ref.at[4 * px + 2 * py + pc]
            pltpu.make_async_remote_copy(src_ref=slab, dst_ref=slab, send_sem=send.at[j], recv_sem=recv.at[j],
                                         device_id=(px, py, pc), device_id_type=MESH).wait_recv()
        for cp in cps:
            cp.wait_send()

    return _pc(body, name="gather_small", in_specs=[_ANY], out_specs=_ANY, out_shape=jax.ShapeDtypeStruct((8,) + pack.shape, pack.dtype),
               scratch_shapes=[pltpu.SemaphoreType.DMA((7,))] * 2)(pack)


def _pack(arrays, width):
    flat = jnp.concatenate([a.reshape(-1).astype(f32) for a in arrays])
    return jnp.pad(flat, (0, (-flat.shape[0]) % (8 * width))).reshape(-1, width)


def _unpack(pack, like, width):
    flat, out, at = pack.reshape(-1), [], 0
    for a in like:
        size = 1
        for s in a.shape:
            size *= s
        out.append(flat[at:at + size].reshape(a.shape))
        at += size
    return out


def _halves(a2d):
    r, c = a2d.shape
    return a2d.reshape(2, r // 2, c)


def local_step(x, meta, norm_w, w_main, w_ba, conv_w, a_log, dt_bias, pool_mix, pool_scale, dn_norm_w,
               late_weights, final_w, target, grad_hooks=None):
    seq, d = x.shape
    pw = pool_scale.shape[1]
    dw = conv_w.shape[1] // 3
    n_heads = dw // HEAD_DIM
    tp = FRONT_PAD + N_META + seq
    hp = jnp.concatenate([jnp.zeros((FRONT_PAD, d), f32), meta, x], axis=0)
    tm_big = tp // 2 if tp % 32 == 0 else tp
    tm_norm = max(t for t in range(16, min(tp, 352) + 1, 16) if tp % t == 0)
    tile = min(512, d)
    off_q = 2 * pw
    off_zd = off_q + 3 * dw
    off_gp = off_zd + dw
    off_gd = off_gp + d
    a_log128 = jnp.pad(a_log, ((0, 0), (0, LANES - n_heads)))
    dt128 = jnp.pad(dt_bias, ((0, 0), (0, LANES - n_heads)))

    xn = norm_fwd(hp, norm_w, tm_norm)
    if isinstance(late_weights[0], Hosted):
        proj, *fetched = matmul("proj", xn, w_main, tb=True, tm=tp, tn=tile, hosted=late_weights[0])
        w_pool_out, w_dn_out, w_o = late_weights[1](fetched)
    else:
        proj = matmul("proj", xn, w_main, tb=True, tm=tp, tn=tile)
        w_pool_out, w_dn_out, w_o = late_weights
    ba = matmul("proj_ba", xn, w_ba, tb=True, tm=tp, tn=LANES)
    y_pool = pool_fwd(proj, pool_mix, pool_scale, pw)
    q, k, v, beta_b, g_b = dn_pre_fwd(proj, ba, conv_w, a_log128, dt128, n_heads, off_q // HEAD_DIM)
    assert off_zd % dw == 0
    *inter, tmats = dn_intra_fwd(q, k, v, beta_b, g_b, n_heads)
    y_dn, states = dn_seq_fwd(inter, proj, dn_norm_w, n_heads, off_zd // dw)
    a_mat = matmul("pool_out", y_pool, w_pool_out, tm=tp, tn=tile)

    def merge(acc, a_t, gp_t, gd_t):
        return acc, sigmoid(gp_t) * a_t + sigmoid(gd_t) * acc

    b_mat, merged = matmul("dn_out_merge", y_dn, w_dn_out, tm=tm_big, tn=tile, extras=[(a_mat, 0), (proj, off_gp), (proj, off_gd)],
                           epi=merge, out_dtypes=(f32, bf16))
    out = matmul("out_proj", merged, w_o, tm=tm_big, tn=tile, extras=[(hp, 0)], epi=lambda acc, h_t: (acc + h_t,))
    loss, dout, dout_b, dfinal_w = loss_stage(out, final_w, target)

    def unmerge(dm, a_t, b_t, gp_t, gd_t):
        sp, sd = sigmoid(gp_t), sigmoid(gd_t)
        return dm * sp, dm * sd, dm * a_t * sp * (1.0 - sp), dm * b_t * sd * (1.0 - sd)

    d_a, d_b, d_gp, d_gd = matmul("d_merged", dout_b, w_o, tb=True, tm=tm_big, tn=tile,
                                  extras=[(a_mat, 0), (b_mat, 0), (proj, off_gp), (proj, off_gd)], epi=unmerge,
                                  out_dtypes=(bf16,) * 4)
    g_w_o = matmul("g_w_o", merged, dout_b, ta=True, tm=tile, tn=d, out_dtypes=(bf16,))
    d_y_pool = matmul("d_y_pool", d_a, w_pool_out, tb=True, tm=tp, tn=tile)
    g_w_pool_out = matmul("g_w_pool_out", y_pool, d_a, ta=True, tm=tile, tn=d, out_dtypes=(bf16,))
    d_y_dn = matmul("d_y_dn", d_b, w_dn_out, tb=True, tm=tp, tn=tile)
    g_w_dn_out = matmul("g_w_dn_out", y_dn, d_b, ta=True, tm=tile, tn=d, out_dtypes=(bf16,))
    d_u, d_zp, g_pool_mix, g_pool_scale = pool_bwd(proj, pool_mix, pool_scale, d_y_pool, pw)
    *d_inter, d_zd, g_dn_norm_w = dn_seq_bwd(inter, proj, dn_norm_w, states, d_y_dn, n_heads, off_zd // dw)
    d_q, d_k, d_v, d_beta, d_g = dn_intra_bwd(q, k, v, beta_b, g_b, tmats, d_inter, n_heads)
    d_qr, d_kr, d_vr, d_ba, g_cq, g_ck, g_cv, g_a_log, g_dt = dn_pre_bwd(
        proj, ba, conv_w, a_log128, dt128, (d_q, d_k, d_v, d_beta, d_g), n_heads, off_q // HEAD_DIM)
    d_proj = jnp.concatenate([d_u, d_zp, d_qr, d_kr, d_vr, d_zd, d_gp, d_gd], axis=1)
    d_ba_b = cast_bf16("cast_d_ba", d_ba)
    early = grad_hooks[0](g_w_pool_out, g_w_dn_out, g_w_o, g_pool_mix) if grad_hooks else None
    res = matmul("g_w_main", d_proj, xn, ta=True, tm=tile, tn=d, out_dtypes=(bf16,), hosted=early)
    g_w_main, early_landed = (res[0], list(res[1:])) if early else (res, [])
    g_w_ba = matmul("g_w_ba", d_ba_b, xn, ta=True, tm=LANES, tn=tile, out_dtypes=(bf16,))
    hosted = grad_hooks[1](g_w_main, g_w_ba, early_landed) if grad_hooks else None
    dxn_ba = matmul("dxn_ba", d_ba_b, w_ba, tm=tp, tn=tile)
    n_cols = d_proj.shape[1]
    tk_dxn = max(t for t in range(LANES, min(2048, n_cols) + 1, LANES) if n_cols % t == 0)
    res = matmul("dxn", d_proj, w_main, tm=tp, tn=tile, tk=tk_dxn, extras=[(dxn_ba, 0)], epi=lambda acc, e: (acc + e,), hosted=hosted)
    dxn, landed = (res[0], list(res[1:])) if hosted else (res, [])
    dh, g_norm_w, last_landed = norm_bwd(hp, norm_w, dxn, dout, tm_norm, hosted=grad_hooks[2](landed) if grad_hooks else None)
    g_conv = jnp.concatenate([g_cq, g_ck, g_cv], axis=1)
    return (loss, dh, g_norm_w, g_w_main, g_w_ba, g_conv, g_a_log[:, :n_heads], g_dt[:, :n_heads], g_pool_mix, g_pool_scale,
            g_dn_norm_w, g_w_pool_out, g_w_dn_out, g_w_o, dfinal_w, last_landed)


def kernel(x, meta_tokens, norm_w, w_in, conv_w, A_log, dt_bias, pool_mix, pool_scale, dn_norm_w, w_pool_out, w_dn_out, w_o, final_norm_w, loss_target, m_meta_tokens, m_norm_w, m_w_in, m_conv_w, m_A_log, m_dt_bias, m_pool_mix, m_pool_scale, m_dn_norm_w, m_w_pool_out, m_w_dn_out, m_w_o, m_final_norm_w, v_meta_tokens, v_norm_w, v_w_in, v_conv_w, v_A_log, v_dt_bias, v_pool_mix, v_pool_scale, v_dn_norm_w, v_w_pool_out, v_w_dn_out, v_w_o, v_final_norm_w):
    d = x.shape[-1]
    pw = pool_scale.shape[-1]
    dw = w_dn_out.shape[1] * 4
    n_heads = dw // HEAD_DIM
    gdim = pw // POOL_GROUPS
    chip = 2 * lax.axis_index("x") + lax.axis_index("y")
    core = lax.axis_index("c")

    half = core.astype(jnp.int32).reshape(1)
    me = 4 * lax.axis_index("x") + 2 * lax.axis_index("y") + core

    w_in_t = w_in[0].T
    mix_s = pool_mix[0].reshape(POOL_GROUPS * (gdim // 4), gdim)
    small_s = _pack([meta_tokens, conv_w[0]], d)
    small_rows = small_s.shape[0]
    small_s = jnp.pad(small_s, ((0, (-small_rows) % 16), (0, 0)))
    sw = w_in_t.shape[0]
    n_main, n_ba = 2 * pw + 4 * dw, 2 * n_heads

    early = [cast_bf16("cast_w_in", w_in_t), _halves(cast_bf16("cast_mix", mix_s)), _halves(small_s)]
    late = [_halves(cast_bf16("cast_w_po", w_pool_out[0])), _halves(cast_bf16("cast_w_do", w_dn_out[0])),
            _halves(cast_bf16("cast_w_o", w_o[0]))]
    fill = lambda g, own: lax.dynamic_update_slice(g, own[None], (chip,) + (0,) * own.ndim)
    g_in, g_mix, g_small = [fill(g, own) for g, own in zip(gather_weights(early, ["cols", "lead", "lead"]), early)]

    def shard_rows(lo, hi):
        cut = [(max(lo, j * sw), min(hi, (j + 1) * sw), j) for j in range(4)]
        return [g_in[j, a - j * sw:b - j * sw] for a, b, j in cut if a < b]

    w_main = jnp.concatenate(shard_rows(0, n_main) + shard_rows(n_main + n_ba, 4 * sw), axis=0)
    w_ba = jnp.pad(jnp.concatenate(shard_rows(n_main, n_main + n_ba), axis=0), ((0, LANES - n_ba), (0, 0)))
    cat_cols = lambda g: jnp.concatenate([g[j].reshape(-1, g.shape[-1]) for j in range(4)], axis=1)

    def late_weights(fetched):
        g_po, g_do, g_o = [fill(g, own) for g, own in zip(fetched, late)]
        return cat_cols(g_po), g_do.reshape(-1, g_do.shape[-1]), g_o.reshape(-1, g_o.shape[-1])

    mix_full = g_mix.reshape(4, POOL_GROUPS, gdim // 4, gdim).transpose(1, 0, 2, 3).reshape(POOL_GROUPS, gdim, gdim)
    smalls = [_unpack(g_small[j].reshape(-1, d)[:small_rows], [meta_tokens, conv_w[0]], d) for j in range(4)]
    meta_full = jnp.concatenate([s[0] for s in smalls], axis=1)
    conv_full = jnp.concatenate([s[1] for s in smalls], axis=1)

    names = ["w_in", "w_po", "w_do", "w_o", "mix"]
    kinds = ["cols", "lead", "lead", "lead", "lead"]

    far_chip = 2 * (lax.axis_index("x") ^ core) + (lax.axis_index("y") ^ (1 - core))
    chips = jnp.stack([chip, far_chip]).astype(jnp.int32)
    col_parts = lambda g: g.reshape(2, g.shape[0] // 2, 4, g.shape[1] // 4).transpose(0, 2, 1, 3)
    row_parts = lambda g: g.reshape(4, 2, g.shape[0] // 8, g.shape[1]).transpose(1, 0, 2, 3)

    def pair_sums(tag, group, parts):
        group_kinds = [kinds[names.index(nm)] for nm in group]
        from_sibling = send_grad_halves("send_grad_halves_" + tag, parts, group_kinds)
        return [add_halves("add_" + nm, p, r, half, k) for nm, p, r, k in zip(group, parts, from_sibling, group_kinds)]

    later = names[1:]
    state = {}

    def later_grads(g_w_po, g_w_do, g_w_o_full, g_mix_full):
        mix_rows = POOL_GROUPS * (gdim // 4)
        mix_parts = (g_mix_full.astype(bf16).reshape(POOL_GROUPS, 4, gdim // 4, gdim).transpose(1, 0, 2, 3)
                     .reshape(4, 2, mix_rows // 2, gdim).transpose(1, 0, 2, 3))
        state["later_sums"] = pair_sums("later", later, [col_parts(g_w_po), row_parts(g_w_do), row_parts(g_w_o_full), mix_parts])
        return hosted_scatter_first(state["later_sums"])

    def input_grads(g_w_main, g_w_ba, later_from_near):
        def grad_rows(lo, hi):
            segs = [(0, n_main, g_w_main, 0), (n_main, n_main + n_ba, g_w_ba, 0), (n_main + n_ba, 4 * sw, g_w_main, n_main)]
            cut = [(max(lo, s0), min(hi, s1), s0, arr, off) for s0, s1, arr, off in segs]
            return [arr[a - s0 + off:b - s0 + off] for a, b, s0, arr, off in cut if a < b]

        kept, passed = zip(*[add_first("add1_" + nm, p, g, chips) for nm, p, g in zip(later, state["later_sums"], later_from_near)])
        state["later_kept"] = list(kept)
        in_parts = jnp.stack([jnp.concatenate(grad_rows(j * sw, (j + 1) * sw), axis=0) for j in range(4)])
        state["in_sums"] = pair_sums("input", names[:1], [in_parts])
        return hosted_scatter_first(state["in_sums"]) + hosted_scatter_second(list(passed))

    def input_second(landed):
        state["later_from_far"] = landed[1:]
        state["in_kept"], in_passed = add_first("add1_w_in", state["in_sums"][0], landed[0], chips)
        return hosted_scatter_second([in_passed])

    (loss, dh, g_norm_w, _, _, g_conv, g_a_log, g_dt, _, g_pool_scale, g_dn_norm_w, _, _, _, g_final_w,
     (in_from_far,)) = local_step(x[0], meta_full, norm_w, w_main, w_ba, conv_full, A_log, dt_bias, mix_full, pool_scale,
                                  dn_norm_w, (hosted_gather(late, ["lead"] * 3), late_weights), final_norm_w.reshape(1, d),
                                  loss_target[0], grad_hooks=(later_grads, input_grads, input_second))
    grad_x = dh[FRONT_PAD + N_META:][None]
    g_meta = dh[FRONT_PAD:FRONT_PAD + N_META]

    mine = [add_second("add2_" + nm, k_, g) for nm, k_, g in
            zip(names, [state["in_kept"]] + state["later_kept"], [in_from_far] + list(state["later_from_far"]))]
    small_like = [loss, g_norm_w, g_a_log, g_dt, g_pool_scale, g_dn_norm_w, g_final_w, g_conv, g_meta]
    pack = _pack(small_like, d)
    packs = gather_small(pack)
    theirs = swap_with_sibling("swap_grad_halves", mine)
    total = sum_leading("sum_small", lax.dynamic_update_slice(packs, pack[None], (me, 0, 0)))
    (loss_t, g_norm_w, g_a_log, g_dt, g_pool_scale, g_dn_norm_w, g_final_w, g_conv, g_meta) = _unpack(total, small_like, d)
    loss_out = loss_t[0, 0]
    g_conv_s = lax.dynamic_slice_in_dim(g_conv, chip * (g_conv.shape[1] // 4), g_conv.shape[1] // 4, axis=1)
    g_meta_s = lax.dynamic_slice_in_dim(g_meta, chip * (d // 4), d // 4, axis=1)

    weights = [meta_tokens, norm_w, w_in, conv_w, A_log, dt_bias, pool_mix, pool_scale, dn_norm_w, w_pool_out, w_dn_out, w_o, final_norm_w]
    ms = [m_meta_tokens, m_norm_w, m_w_in, m_conv_w, m_A_log, m_dt_bias, m_pool_mix, m_pool_scale, m_dn_norm_w, m_w_pool_out, m_w_dn_out, m_w_o, m_final_norm_w]
    vs = [v_meta_tokens, v_norm_w, v_w_in, v_conv_w, v_A_log, v_dt_bias, v_pool_mix, v_pool_scale, v_dn_norm_w, v_w_pool_out, v_w_dn_out, v_w_o, v_final_norm_w]
    grads = [g_meta_s, g_norm_w, None, g_conv_s[None], g_a_log, g_dt, None, g_pool_scale, g_dn_norm_w, None, None, None, g_final_w.reshape(d)]
    deltas, new_ms, new_vs = [None] * 13, [None] * 13, [None] * 13
    big = [2, 9, 10, 11, 6]
    for i, nm, g_mine, g_theirs in zip(big, names, mine, theirs):
        if nm == "w_in":
            to2d, back, axis = (lambda t: t[0].T), (lambda t: t.T[None]), 1
        else:
            to2d, back, axis = (lambda t: t.reshape(-1, t.shape[-1])), (lambda t, i=i: t.reshape(weights[i].shape)), 0
        res = adamw_joined("adamw_" + nm, to2d(weights[i]), g_mine, g_theirs, to2d(ms[i]), to2d(vs[i]), half, axis)
        grads[i], deltas[i], new_ms[i], new_vs[i] = [back(t) for t in res]
    small_idx = [i for i in range(13) if i not in big]
    packs = [_pack([arrs[i] for i in small_idx], d) for arrs in (weights, grads, ms, vs)]
    outs = adamw("adamw_small", *packs)
    like = [weights[i] for i in small_idx]
    for res, dest in zip(outs, (deltas, new_ms, new_vs)):
        for i, val in zip(small_idx, _unpack(res, like, d)):
            dest[i] = val
    return (loss_out, grad_x, *grads, *deltas, *new_ms, *new_vs)
```

```python
import functools

import jax
import jax.numpy as jnp
from jax import lax
from jax.experimental import pallas as pl
from jax.experimental.pallas import tpu as pltpu

f32 = jnp.float32
bf16 = jnp.bfloat16
MESH = pl.DeviceIdType.MESH

N_META = 16
CHUNK = 64
FRONT_PAD = (-N_META) % CHUNK
HEAD_DIM = 128
POOL_GROUPS = 4
POOL_WINDOWS = (2, 4, 8, 16)
CONV_WIDTH = 4
NORM_EPS = 1e-6
ADAM_LR, ADAM_B1, ADAM_B2, ADAM_EPS, ADAM_WD, ADAM_STEP = 0.001, 0.9, 0.999, 1e-08, 0.01, 10
LANES = 128
V7X_VMEM_BYTES = 64 * 2**20
VMEM_LIMIT = V7X_VMEM_BYTES - 8 * 2**20


def _pc(body, **kw):
    return pl.pallas_call(body, **kw)


def _params(*sem, **kw):
    return pltpu.CompilerParams(dimension_semantics=sem or None, vmem_limit_bytes=VMEM_LIMIT, **kw)


def _dg(a, b, dims):
    return lax.dot_general(a, b, (dims, ((), ())), preferred_element_type=f32)


@jax.custom_vjp
def mm_nn(a, b):
    return _dg(a.astype(bf16), b.astype(bf16), ((1,), (0,)))


@jax.custom_vjp
def mm_nt(a, b):
    return _dg(a.astype(bf16), b.astype(bf16), ((1,), (1,)))


@jax.custom_vjp
def mm_tn(a, b):
    return _dg(a.astype(bf16).T, b.astype(bf16), ((1,), (0,)))


mm_nn.defvjp(lambda a, b: (mm_nn(a, b), (a, b)), lambda r, dy: (mm_nt(dy, r[1]), mm_tn(r[0], dy)))
mm_nt.defvjp(lambda a, b: (mm_nt(a, b), (a, b)), lambda r, dy: (mm_nn(dy, r[1]), mm_tn(dy, r[0])))
mm_tn.defvjp(lambda a, b: (mm_tn(a, b), (a, b)), lambda r, dy: (mm_nt(r[1], dy), mm_nn(r[0], dy)))


def _split3(x):
    hi = x.astype(bf16)
    r1 = x - hi.astype(f32)
    mid = r1.astype(bf16)
    lo = (r1 - mid.astype(f32)).astype(bf16)
    return hi, mid, lo


def _split2(x):
    hi = x.astype(bf16)
    return hi, (x - hi.astype(f32)).astype(bf16)


@jax.custom_vjp
def mm_sel(sel, x):
    s = sel.astype(bf16)
    d = ((1,), (0,))
    hi, lo = _split2(x)
    return _dg(s, hi, d) + _dg(s, lo, d)


def _mm_sel_bwd(sel, dy):
    s = sel.astype(bf16)
    d = ((0,), (0,))
    hi, lo = _split2(dy)
    return jnp.zeros_like(sel), _dg(s, hi, d) + _dg(s, lo, d)


mm_sel.defvjp(lambda sel, x: (mm_sel(sel, x), sel), _mm_sel_bwd)


@jax.custom_vjp
def mm_pick(x, sel):
    s = sel.astype(bf16)
    d = ((1,), (0,))
    hi, mid, lo = _split3(x)
    return _dg(hi, s, d) + _dg(mid, s, d) + _dg(lo, s, d)


def _mm_pick_bwd(sel, dy):
    s = sel.astype(bf16)
    d = ((1,), (1,))
    hi, mid, lo = _split3(dy)
    return _dg(hi, s, d) + _dg(mid, s, d) + _dg(lo, s, d), jnp.zeros_like(sel)


mm_pick.defvjp(lambda x, sel: (mm_pick(x, sel), sel), _mm_pick_bwd)


def tri_inv(ls):
    n = ls[0].shape[0]
    eye = (lax.broadcasted_iota(jnp.int32, (n, n), 0) == lax.broadcasted_iota(jnp.int32, (n, n), 1)).astype(f32)
    ms = [-l for l in ls]
    ts = [eye + m for m in ms]
    k = 1
    while 2 * k < CHUNK:
        ms = [mm_nn(m, m) for m in ms]
        ts = [t + mm_nn(t, m) for t, m in zip(ts, ms)]
        k *= 2
    return ts


@functools.partial(jax.custom_vjp, nondiff_argnums=(1,))
def shift_rows(x, j):
    n = x.shape[0]
    rows = lax.broadcasted_iota(jnp.int32, x.shape, 0)
    if j >= 0:
        return jnp.where(rows >= j, pltpu.roll(x, j, 0), 0.0)
    return jnp.where(rows < n + j, pltpu.roll(x, n + j, 0), 0.0)


shift_rows.defvjp(lambda x, j: (shift_rows(x, j), None), lambda j, _, dy: (shift_rows(dy, -j),))


def sigmoid(x):
    return 1.0 / (1.0 + jnp.exp(-x))


def silu(x):
    return x * sigmoid(x)


def softplus(x):
    return jnp.maximum(x, 0.0) + jnp.log(1.0 + jnp.exp(-jnp.abs(x)))


def rmsnorm(x, w):
    return x * lax.rsqrt(jnp.mean(x * x, axis=-1, keepdims=True) + NORM_EPS) * w


def l2norm(x):
    return x * lax.rsqrt(jnp.sum(x * x, axis=-1, keepdims=True) + NORM_EPS)


def pool_fn(u, zp, mix, scale, group):
    rows = lax.broadcasted_iota(jnp.int32, u.shape, 0)
    sums = []
    s, w = u, 1
    while w < POOL_WINDOWS[-1]:
        s = s + shift_rows(s, w)
        w *= 2
        sums.append(s)
    total = sums[-1]
    for gi in range(POOL_GROUPS - 2, -1, -1):
        total = jnp.where(group == gi, sums[gi], total)
    window = jnp.left_shift(2, group)
    cnt = jnp.clip(rows - (FRONT_PAD - 1), 1, window).astype(f32)
    pooled = total / cnt - u
    return mm_nn(pooled, mix) * scale * silu(zp)


PRE_HALO = 8


def conv_silu(x, w):
    k = CONV_WIDTH
    y = x * w[k - 1:k, :]
    for kk in range(k - 1):
        y = y + shift_rows(x, k - 1 - kk) * w[kk:kk + 1, :]
    return silu(y[PRE_HALO:])


def _lane_pick(row, idx):
    lanes = lax.broadcasted_iota(jnp.int32, row.shape, 1)
    return jnp.sum(jnp.where(lanes == idx, row, 0.0), axis=1, keepdims=True)


def dn_pre_fn(qr, kr, vr, ba, cwq, cwk, cwv, a_log, dt_bias, head, n_heads, row0):
    q = l2norm(conv_silu(qr, cwq)) * (HEAD_DIM ** -0.5)
    k = l2norm(conv_silu(kr, cwk))
    v = conv_silu(vr, cwv)
    r = lax.broadcasted_iota(jnp.int32, (LANES, LANES), 0)
    b_b = mm_pick(ba, (r == head).astype(f32))
    a_b = mm_pick(ba, (r == head + n_heads).astype(f32))
    real = lax.broadcasted_iota(jnp.int32, ba.shape, 0) + row0 >= FRONT_PAD
    beta_b = jnp.where(real, sigmoid(b_b), 0.0)
    g_b = jnp.where(real, -jnp.exp(_lane_pick(a_log, head)) * softplus(a_b + _lane_pick(dt_bias, head)), 0.0)
    return q, k, v, beta_b, g_b


def _chunk_masks(rows):
    r = lax.broadcasted_iota(jnp.int32, (rows, rows), 0)
    c = lax.broadcasted_iota(jnp.int32, (rows, rows), 1)
    same = (r // CHUNK) == (c // CHUNK)
    return same, jnp.logical_and(same, r >= c), jnp.logical_and(same, r > c)


def _lane0(rows):
    return (lax.broadcasted_iota(jnp.int32, (rows, LANES), 1) == 0).astype(bf16)


@jax.custom_vjp
def lane0_as_row(x):
    sel = _lane0(x.shape[0])
    d = ((1,), (1,))
    hi, lo = _split2(x)
    return _dg(sel, hi, d) + _dg(sel, lo, d)


def _lane0_as_row_bwd(rows, dy):
    sel = _lane0(rows)
    d = ((0,), (0,))
    hi, lo = _split2(dy)
    return (_dg(hi, sel, d) + _dg(lo, sel, d),)


lane0_as_row.defvjp(lambda x: (lane0_as_row(x), x.shape[0]), _lane0_as_row_bwd)


def gate_fn(g_b):
    rows = g_b.shape[0]
    same, causal, _ = _chunk_masks(rows)
    gcum_b = mm_sel(causal.astype(f32), g_b)
    glast_b = mm_sel(same.astype(f32), g_b)
    g_rows = jnp.broadcast_to(gcum_b[:, :1], (rows, rows))
    decay = jnp.where(causal, jnp.exp(jnp.where(causal, g_rows - lane0_as_row(gcum_b), 0.0)), 0.0)
    return decay, jnp.exp(gcum_b), jnp.exp(glast_b - gcum_b), jnp.exp(glast_b)


def _fold_matrix(rows):
    r = lax.broadcasted_iota(jnp.int32, (rows, LANES), 0)
    c = lax.broadcasted_iota(jnp.int32, (rows, LANES), 1)
    return (r % CHUNK == c).astype(bf16)


@jax.custom_vjp
def fold_chunks(x):
    return _dg(x.astype(bf16), _fold_matrix(x.shape[0]), ((1,), (0,)))


def _fold_chunks_bwd(rows, dy):
    fold = _fold_matrix(rows)
    d = ((1,), (1,))
    hi, mid, lo = _split3(dy)
    return (_dg(hi, fold, d) + _dg(mid, fold, d) + _dg(lo, fold, d),)


fold_chunks.defvjp(lambda x: (fold_chunks(x), x.shape[0]), _fold_chunks_bwd)


def lmat_fn(k, beta_b, decay):
    _, _, strict = _chunk_masks(k.shape[0])
    return jnp.where(strict, mm_nt(k * beta_b, k) * decay, 0.0)


def intra_fn(tmat, q, k, v, beta_b, decay, eg, kfac):
    _, causal, _ = _chunk_masks(q.shape[0])
    k_beta = k * beta_b
    u_c = mm_nn(tmat, v * beta_b)
    w_c = mm_nn(tmat, k_beta * eg)
    qk = jnp.where(causal, mm_nt(q, k) * decay, 0.0)
    return u_c, w_c, q * eg, k * kfac, fold_chunks(qk)


def gated_norm(o, norm_w, zd):
    return rmsnorm(o, norm_w) * silu(zd)


def loss_fn(o, w, tgt):
    err = rmsnorm(o, w) - tgt
    return 0.5 * jnp.sum(jnp.mean(err * err, axis=-1))


_ANY = pl.BlockSpec(memory_space=pl.ANY)


class Hosted:
    def __init__(self, arrays, out_shapes, sems, stages):
        self.arrays, self.out_shapes, self.sems, self.stages = list(arrays), list(out_shapes), list(sems), stages

    def __add__(self, other):
        ni, no, ns = len(self.arrays), len(self.out_shapes), len(self.sems)
        stages = lambda i, o, s: self.stages(i[:ni], o[:no], s[:ns]) + other.stages(i[ni:], o[no:], s[ns:])
        return Hosted(self.arrays + other.arrays, self.out_shapes + other.out_shapes, self.sems + other.sems, stages)


def matmul(name, a, b, *, ta=False, tb=False, tm, tn, tk=None, extras=(), epi=None, out_dtypes=(f32,), hosted=None):
    m, k = (a.shape[1], a.shape[0]) if ta else a.shape
    n = b.shape[0] if tb else b.shape[1]
    tm, tn, tk = min(tm, m), min(tn, n), min(tk or k, k)
    assert m % tm == 0 and n % tn == 0 and k % tk == 0, (name, m, n, k, tm, tn, tk)
    nm, nn, nk = m // tm, n // tn, k // tk
    a_spec = pl.BlockSpec((tk, tm), lambda i, j, kk: (kk, i)) if ta else pl.BlockSpec((tm, tk), lambda i, j, kk: (i, kk))
    b_spec = pl.BlockSpec((tn, tk), lambda i, j, kk: (j, kk)) if tb else pl.BlockSpec((tk, tn), lambda i, j, kk: (kk, j))
    ex_specs = []
    for _, off in extras:
        assert off % tn == 0, (name, off, tn)
        ex_specs.append(pl.BlockSpec((tm, tn), functools.partial(lambda i, j, kk, o: (i, o + j), o=off // tn)))
    n_ex, n_out = len(extras), len(out_dtypes)
    dims = ((0 if ta else 1,), (1 if tb else 0,))
    n_hin = len(hosted.arrays) if hosted else 0
    n_hout = len(hosted.out_shapes) if hosted else 0
    n_sem = len(hosted.sems) if hosted else 0

    def body(a_ref, b_ref, *rest):
        ex_refs, rest = rest[:n_ex], rest[n_ex:]
        hin_refs, rest = rest[:n_hin], rest[n_hin:]
        out_refs, rest = rest[:n_out], rest[n_out:]
        hout_refs, rest = rest[:n_hout], rest[n_hout:]
        sem_refs = rest[len(rest) - n_sem:] if n_sem else ()
        step = (pl.program_id(0) * nn + pl.program_id(1)) * nk + pl.program_id(2)
        stages = hosted.stages(hin_refs, hout_refs, sem_refs) if hosted else []
        for frac, emit in stages:
            if frac < 1.0:
                pl.when(step == int(frac * (nm * nn * nk - 1)))(emit)

        def finish(acc):
            res = epi(acc, *[r[...] for r in ex_refs]) if epi is not None else (acc,)
            for o_ref, val in zip(out_refs, res):
                o_ref[...] = val.astype(o_ref.dtype)

        p = _dg(a_ref[...], b_ref[...], dims)
        if nk == 1:
            finish(p)
        else:
            acc_ref = rest[0]
            kk = pl.program_id(2)

            @pl.when(kk == 0)
            def _():
                acc_ref[...] = p

            @pl.when(kk > 0)
            def _():
                acc_ref[...] += p

            @pl.when(kk == nk - 1)
            def _():
                finish(acc_ref[...])

        for frac, emit in stages:
            if frac >= 1.0:
                pl.when(step == nm * nn * nk - 1)(emit)

    outs = _pc(
        body, name=name, grid=(nm, nn, nk),
        in_specs=[a_spec, b_spec] + ex_specs + [_ANY] * n_hin,
        out_specs=[pl.BlockSpec((tm, tn), lambda i, j, kk: (i, j))] * n_out + [_ANY] * n_hout,
        out_shape=[jax.ShapeDtypeStruct((m, n), dt) for dt in out_dtypes] + (hosted.out_shapes if hosted else []),
        scratch_shapes=([pltpu.VMEM((tm, tn), f32)] if nk > 1 else []) + (hosted.sems if hosted else []),
        compiler_params=_params(*(("arbitrary",) * 3 if hosted else ("parallel", "parallel", "arbitrary"))),
    )(a, b, *[e for e, _ in extras], *(hosted.arrays if hosted else []))
    return outs[0] if len(outs) == 1 else outs


def _row_tile(rows, cols, n_arrays, itemsize=4, budget=24 * 2**20):
    best = None
    for t in range(16, rows + 1, 16):
        if rows % t == 0 and 2 * n_arrays * t * cols * itemsize <= budget:
            best = t
    return best or rows


def _tile(rows, cols, n_arrays, budget=24 * 2**20):
    if rows % 16 == 0 or cols % LANES != 0:
        return _row_tile(rows, cols, n_arrays, budget=budget), cols
    fits = [t for t in range(LANES, cols + 1, LANES) if cols % t == 0 and 2 * n_arrays * rows * t * 4 <= budget]
    return rows, (max(fits) if fits else LANES)


def cast_bf16(name, x):
    rows, cols = x.shape
    tr, tc = _tile(rows, cols, 2)

    def body(x_ref, o_ref):
        o_ref[...] = x_ref[...].astype(bf16)

    blk = pl.BlockSpec((tr, tc), lambda i, j: (i, j))
    return _pc(body, name=name, grid=(rows // tr, cols // tc), in_specs=[blk], out_specs=blk,
               out_shape=jax.ShapeDtypeStruct(x.shape, bf16), compiler_params=_params("parallel", "parallel"))(x)


def norm_fwd(hp, norm_w, tm):
    tp, d = hp.shape

    def body(h_ref, w_ref, o_ref):
        o_ref[...] = rmsnorm(h_ref[...], w_ref[...]).astype(bf16)

    return _pc(body, name="norm_fwd", grid=(tp // tm,),
               in_specs=[pl.BlockSpec((tm, d), lambda i: (i, 0)), pl.BlockSpec((1, d), lambda i: (0, 0))],
               out_specs=pl.BlockSpec((tm, d), lambda i: (i, 0)), out_shape=jax.ShapeDtypeStruct((tp, d), bf16),
               compiler_params=_params("parallel"))(hp, norm_w)


def norm_bwd(hp, norm_w, dxn, dout, tm, hosted=None):
    tp, d = hp.shape
    steps = tp // tm
    n_hin = len(hosted.arrays) if hosted else 0
    n_hout = len(hosted.out_shapes) if hosted else 0

    def body(h_ref, w_ref, dxn_ref, dout_ref, *rest):
        hin_refs, (dh_ref, dw_ref), rest = rest[:n_hin], rest[n_hin:n_hin + 2], rest[n_hin + 2:]
        stages = hosted.stages(hin_refs, rest[:n_hout], rest[n_hout:]) if hosted else []
        for frac, emit in stages:
            if frac < 1.0:
                pl.when(pl.program_id(0) == int(frac * (steps - 1)))(emit)
        _, vjp = jax.vjp(rmsnorm, h_ref[...], w_ref[...])
        dh, dw = vjp(dxn_ref[...])
        dh_ref[...] = dh + dout_ref[...]

        @pl.when(pl.program_id(0) == 0)
        def _():
            dw_ref[...] = jnp.zeros_like(dw_ref)

        dw_ref[...] += dw
        for frac, emit in stages:
            if frac >= 1.0:
                pl.when(pl.program_id(0) == steps - 1)(emit)

    row = pl.BlockSpec((tm, d), lambda i: (i, 0))
    vec = pl.BlockSpec((1, d), lambda i: (0, 0))
    outs = _pc(body, name="norm_bwd", grid=(steps,), in_specs=[row, vec, row, row] + [_ANY] * n_hin,
               out_specs=[row, vec] + [_ANY] * n_hout,
               out_shape=[jax.ShapeDtypeStruct((tp, d), f32), jax.ShapeDtypeStruct((1, d), f32)] + (hosted.out_shapes if hosted else []),
               scratch_shapes=hosted.sems if hosted else [],
               compiler_params=_params("arbitrary"))(hp, norm_w, dxn, dout, *(hosted.arrays if hosted else []))
    return outs[0], outs[1], list(outs[2:])


def pool_fwd(proj, mix, scale, pw):
    tp = proj.shape[0]
    g = pw // POOL_GROUPS

    def body(u_ref, z_ref, mix_ref, s_ref, y_ref):
        y_ref[...] = pool_fn(u_ref[...], z_ref[...], mix_ref[0], s_ref[...], pl.program_id(0)).astype(bf16)

    return _pc(body, name="pool_fwd", grid=(POOL_GROUPS,),
               in_specs=[pl.BlockSpec((tp, g), lambda i: (0, i)), pl.BlockSpec((tp, g), lambda i: (0, POOL_GROUPS + i)),
                         pl.BlockSpec((1, g, g), lambda i: (i, 0, 0)), pl.BlockSpec((1, g), lambda i: (0, i))],
               out_specs=pl.BlockSpec((tp, g), lambda i: (0, i)), out_shape=jax.ShapeDtypeStruct((tp, pw), bf16),
               compiler_params=_params("parallel"))(proj, proj, mix, scale)


def pool_bwd(proj, mix, scale, dy, pw):
    tp = proj.shape[0]
    g = pw // POOL_GROUPS

    def body(u_ref, z_ref, mix_ref, s_ref, dy_ref, du_ref, dz_ref, dmix_ref, ds_ref):
        grp = pl.program_id(0)
        _, vjp = jax.vjp(lambda u, z, m, s: pool_fn(u, z, m, s, grp), u_ref[...], z_ref[...], mix_ref[0].astype(f32), s_ref[...])
        du, dz, dmix, ds = vjp(dy_ref[...])
        du_ref[...] = du.astype(bf16)
        dz_ref[...] = dz.astype(bf16)
        dmix_ref[0] = dmix
        ds_ref[...] = ds

    col = pl.BlockSpec((tp, g), lambda i: (0, i))
    return _pc(body, name="pool_bwd", grid=(POOL_GROUPS,),
               in_specs=[col, pl.BlockSpec((tp, g), lambda i: (0, POOL_GROUPS + i)),
                         pl.BlockSpec((1, g, g), lambda i: (i, 0, 0)), pl.BlockSpec((1, g), lambda i: (0, i)), col],
               out_specs=[col, col, pl.BlockSpec((1, g, g), lambda i: (i, 0, 0)), pl.BlockSpec((1, g), lambda i: (0, i))],
               out_shape=[jax.ShapeDtypeStruct((tp, pw), bf16), jax.ShapeDtypeStruct((tp, pw), bf16),
                          jax.ShapeDtypeStruct((POOL_GROUPS, g, g), f32), jax.ShapeDtypeStruct((1, pw), f32)],
               compiler_params=_params("parallel"))(proj, proj, mix, scale, dy)


def _dn_pre_specs(tp, n_heads, q_off):
    hb = lambda off: pl.BlockSpec((tp, HEAD_DIM), functools.partial(lambda h, o: (0, o + h), o=off))
    cw = lambda off: pl.BlockSpec((CONV_WIDTH, HEAD_DIM), functools.partial(lambda h, o: (0, o + h), o=off))
    whole = lambda shape: pl.BlockSpec(shape, lambda h: (0, 0))
    return ([hb(q_off), hb(q_off + n_heads), hb(q_off + 2 * n_heads), whole((tp, LANES)),
             cw(0), cw(n_heads), cw(2 * n_heads), whole((1, LANES)), whole((1, LANES))], hb, cw, whole)


def _pre_rows(tp):
    return max(t for t in range(16, min(tp, 192) + 1, 16) if tp % t == 0)


def _with_history(ref, r0, rows):
    if r0 == 0:
        return jnp.concatenate([jnp.zeros((PRE_HALO, ref.shape[1]), f32), ref[0:rows, :]], axis=0)
    return ref[r0 - PRE_HALO:r0 + rows, :]


def dn_pre_fwd(proj, ba, conv_w, a_log, dt_bias, n_heads, q_off):
    tp = proj.shape[0]
    rows = _pre_rows(tp)
    in_specs, hb, _, _ = _dn_pre_specs(tp, n_heads, q_off)

    def body(q_ref, k_ref, v_ref, ba_ref, cq_ref, ck_ref, cv_ref, al_ref, dt_ref, *out_refs):
        for r0 in range(0, tp, rows):
            outs = dn_pre_fn(_with_history(q_ref, r0, rows), _with_history(k_ref, r0, rows), _with_history(v_ref, r0, rows),
                             ba_ref[r0:r0 + rows, :], cq_ref[...], ck_ref[...], cv_ref[...], al_ref[...], dt_ref[...],
                             pl.program_id(0), n_heads, r0)
            for o_ref, val in zip(out_refs, outs):
                o_ref[r0:r0 + rows, :] = val

    return _pc(body, name="dn_pre_fwd", grid=(n_heads,), in_specs=in_specs, out_specs=[hb(0)] * 5,
               out_shape=[jax.ShapeDtypeStruct((tp, n_heads * HEAD_DIM), f32)] * 5,
               compiler_params=_params("parallel"))(proj, proj, proj, ba, conv_w, conv_w, conv_w, a_log, dt_bias)


def dn_pre_bwd(proj, ba, conv_w, a_log, dt_bias, cots, n_heads, q_off):
    tp = proj.shape[0]
    rows = _pre_rows(tp)
    in_specs, hb, cw, whole = _dn_pre_specs(tp, n_heads, q_off)

    def body(q_ref, k_ref, v_ref, ba_ref, cq_ref, ck_ref, cv_ref, al_ref, dt_ref, dq_ref, dk_ref, dv_ref, db_ref, dg_ref,
             dqr_ref, dkr_ref, dvr_ref, dba_ref, dcq_ref, dck_ref, dcv_ref, dal_ref, ddt_ref):
        head = pl.program_id(0)

        @pl.when(head == 0)
        def _():
            dba_ref[...] = jnp.zeros_like(dba_ref)
            dal_ref[...] = jnp.zeros_like(dal_ref)
            ddt_ref[...] = jnp.zeros_like(ddt_ref)

        owed = [jnp.zeros((PRE_HALO, HEAD_DIM), f32)] * 3
        d_conv = [jnp.zeros((CONV_WIDTH, HEAD_DIM), f32)] * 3
        d_al, d_dt = jnp.zeros((1, LANES), f32), jnp.zeros((1, LANES), f32)
        for r0 in reversed(range(0, tp, rows)):
            fn = lambda *args, r0=r0: dn_pre_fn(*args, head, n_heads, r0)
            _, vjp = jax.vjp(fn, _with_history(q_ref, r0, rows), _with_history(k_ref, r0, rows), _with_history(v_ref, r0, rows),
                             ba_ref[r0:r0 + rows, :], cq_ref[...], ck_ref[...], cv_ref[...], al_ref[...], dt_ref[...])
            tile = slice(r0, r0 + rows)
            *d_raw, dba, dcq, dck, dcv, dal, ddt = vjp((dq_ref[tile, :], dk_ref[tile, :], dv_ref[tile, :], db_ref[tile, :], dg_ref[tile, :]))
            for i, (out_ref, d) in enumerate(zip((dqr_ref, dkr_ref, dvr_ref), d_raw)):
                out_ref[tile, :] = jnp.concatenate([d[PRE_HALO:rows], d[rows:] + owed[i]], axis=0).astype(bf16)
                owed[i] = d[:PRE_HALO]
            dba_ref[tile, :] += dba
            d_conv = [acc + d for acc, d in zip(d_conv, (dcq, dck, dcv))]
            d_al, d_dt = d_al + dal, d_dt + ddt
        dcq_ref[...], dck_ref[...], dcv_ref[...] = d_conv
        dal_ref[...] += d_al
        ddt_ref[...] += d_dt

    w = n_heads * HEAD_DIM
    return _pc(body, name="dn_pre_bwd", grid=(n_heads,), in_specs=in_specs + [hb(0)] * 5,
               out_specs=[hb(0)] * 3 + [whole((tp, LANES)), cw(0), cw(0), cw(0), whole((1, LANES)), whole((1, LANES))],
               out_shape=[jax.ShapeDtypeStruct((tp, w), bf16)] * 3 + [jax.ShapeDtypeStruct((tp, LANES), f32)]
               + [jax.ShapeDtypeStruct((CONV_WIDTH, w), f32)] * 3 + [jax.ShapeDtypeStruct((1, LANES), f32)] * 2,
               compiler_params=_params("arbitrary"))(proj, proj, proj, ba, conv_w, conv_w, conv_w, a_log, dt_bias, *cots)


def _super_rows(tp):
    n = tp // CHUNK
    return CHUNK * max(j for j in (4, 3, 2, 1) if n % j == 0)


def _heads_per_step(n_heads):
    return max(j for j in (8, 4, 2, 1) if n_heads % j == 0)


def dn_intra_fwd(q, k, v, beta_b, g_b, n_heads):
    tp = q.shape[0]
    rows = _super_rows(tp)
    ns = tp // rows

    hps = _heads_per_step(n_heads)

    def body(q_ref, k_ref, v_ref, b_ref, g_ref, u_ref, w_ref, qd_ref, kd_ref, qk_ref, gl_ref, t_ref):
        lanes = [slice(i * HEAD_DIM, (i + 1) * HEAD_DIM) for i in range(hps)]
        gates = [gate_fn(g_ref[:, sl]) for sl in lanes]
        tmats = tri_inv([lmat_fn(k_ref[:, sl], b_ref[:, sl], gt[0]) for sl, gt in zip(lanes, gates)])
        for i, (sl, (decay, eg, kfac, gl), tmat) in enumerate(zip(lanes, gates, tmats)):
            u_c, w_c, q_dec, k_dec, qk_c = intra_fn(tmat, q_ref[:, sl], k_ref[:, sl], v_ref[:, sl], b_ref[:, sl], decay, eg, kfac)
            u_ref[:, sl] = u_c
            w_ref[:, sl] = w_c.astype(bf16)
            qd_ref[:, sl] = q_dec.astype(bf16)
            kd_ref[:, sl] = k_dec.astype(bf16)
            qk_ref[:, sl] = qk_c.astype(bf16)
            gl_ref[:, sl] = gl
            t_ref[i, 0] = tmat

    blk = pl.BlockSpec((rows, hps * HEAD_DIM), lambda h, s: (s, h))
    return _pc(body, name="dn_intra_fwd", grid=(n_heads // hps, ns), in_specs=[blk] * 5,
               out_specs=[blk] * 6 + [pl.BlockSpec((hps, 1, rows, rows), lambda h, s: (h, s, 0, 0))],
               out_shape=[jax.ShapeDtypeStruct(q.shape, dt) for dt in (f32, bf16, bf16, bf16, bf16, f32)]
               + [jax.ShapeDtypeStruct((n_heads, ns, rows, rows), f32)],
               compiler_params=_params("parallel", "parallel"))(q, k, v, beta_b, g_b)


def dn_intra_bwd(q, k, v, beta_b, g_b, tmats, cots, n_heads):
    tp = q.shape[0]
    rows = _super_rows(tp)
    ns = tp // rows

    hps = _heads_per_step(n_heads)

    def body(q_ref, k_ref, v_ref, b_ref, g_ref, t_ref, du_ref, dw_ref, dqd_ref, dkd_ref, dqk_ref, dgl_ref,
             dq_ref, dk_ref, dv_ref, db_ref, dg_ref):
        lanes = [slice(i * HEAD_DIM, (i + 1) * HEAD_DIM) for i in range(hps)]
        tmats = [t_ref[i, 0] for i in range(hps)]
        gates = [jax.vjp(gate_fn, g_ref[:, sl]) for sl in lanes]
        intra = [jax.vjp(intra_fn, tmat, q_ref[:, sl], k_ref[:, sl], v_ref[:, sl], b_ref[:, sl], gt[0][0], gt[0][1], gt[0][2])[1](
            (du_ref[:, sl].astype(f32), dw_ref[:, sl].astype(f32), dqd_ref[:, sl], dkd_ref[:, sl], dqk_ref[:, sl]))
            for sl, tmat, gt in zip(lanes, tmats, gates)]
        tts = [tmat.T for tmat in tmats]
        dls = [mm_nn(tt, res[0]) for tt, res in zip(tts, intra)]
        dls = [-mm_nn(dl, tt) for dl, tt in zip(dls, tts)]
        for sl, gt, res, dl in zip(lanes, gates, intra, dls):
            _, dq, dk, dv, db, ddecay, deg, dkfac = res
            dk2, db2, ddecay2 = jax.vjp(lmat_fn, k_ref[:, sl], b_ref[:, sl], gt[0][0])[1](dl)
            (dg,) = gt[1]((ddecay + ddecay2, deg, dkfac, dgl_ref[:, sl]))
            dq_ref[:, sl] = dq
            dk_ref[:, sl] = dk + dk2
            dv_ref[:, sl] = dv
            db_ref[:, sl] = db + db2
            dg_ref[:, sl] = dg

    blk = pl.BlockSpec((rows, hps * HEAD_DIM), lambda h, s: (s, h))
    return _pc(body, name="dn_intra_bwd", grid=(n_heads // hps, ns),
               in_specs=[blk] * 5 + [pl.BlockSpec((hps, 1, rows, rows), lambda h, s: (h, s, 0, 0))] + [blk] * 6,
               out_specs=[blk] * 5, out_shape=[jax.ShapeDtypeStruct(q.shape, f32)] * 5,
               compiler_params=_params("parallel", "parallel"))(q, k, v, beta_b, g_b, tmats, *cots)


def dn_seq_fwd(inter, proj, dn_norm_w, n_heads, zd_off):
    tp, width = inter[0].shape
    n_chunks = tp // CHUNK

    def body(u_ref, w_ref, qd_ref, kd_ref, qk_ref, gl_ref, z_ref, nw_ref, y_ref, s_ref, state):
        @pl.when(pl.program_id(0) == 0)
        def _():
            state[...] = jnp.zeros_like(state)

        lanes = [slice(h * HEAD_DIM, (h + 1) * HEAD_DIM) for h in range(n_heads)]
        sts = [state[h] for h in range(n_heads)]
        for h, st in enumerate(sts):
            s_ref[0, h] = st
        v_new = [u_ref[:, sl] - mm_nn(w_ref[:, sl], st) for sl, st in zip(lanes, sts)]
        outs = [mm_nn(qd_ref[:, sl], st) + mm_nn(qk_ref[:, sl][:, :CHUNK], vn) for sl, st, vn in zip(lanes, sts, v_new)]
        for h, (sl, st, vn) in enumerate(zip(lanes, sts, v_new)):
            state[h] = st * gl_ref[0:1, sl] + mm_tn(kd_ref[:, sl], vn)
        for sl, o in zip(lanes, outs):
            y_ref[:, sl] = gated_norm(o, nw_ref[...], z_ref[:, sl]).astype(bf16)

    blk = pl.BlockSpec((CHUNK, width), lambda n: (n, 0))
    return _pc(body, name="dn_seq_fwd", grid=(n_chunks,),
               in_specs=[blk] * 6 + [pl.BlockSpec((CHUNK, width), lambda n: (n, zd_off)), pl.BlockSpec((1, HEAD_DIM), lambda n: (0, 0))],
               out_specs=[blk, pl.BlockSpec((1, n_heads, HEAD_DIM, HEAD_DIM), lambda n: (n, 0, 0, 0))],
               out_shape=[jax.ShapeDtypeStruct((tp, width), bf16), jax.ShapeDtypeStruct((n_chunks, n_heads, HEAD_DIM, HEAD_DIM), f32)],
               scratch_shapes=[pltpu.VMEM((n_heads, HEAD_DIM, HEAD_DIM), f32)],
               compiler_params=_params("arbitrary"))(*inter, proj, dn_norm_w)


def dn_seq_bwd(inter, proj, dn_norm_w, states, dy, n_heads, zd_off):
    tp, width = inter[0].shape
    n_chunks = tp // CHUNK
    last = n_chunks - 1

    def body(u_ref, w_ref, qd_ref, kd_ref, qk_ref, gl_ref, z_ref, nw_ref, s_ref, dy_ref,
             du_ref, dw_ref, dqd_ref, dkd_ref, dqk_ref, dgl_ref, dz_ref, dnw_ref, dstate):
        @pl.when(pl.program_id(0) == 0)
        def _():
            dstate[...] = jnp.zeros_like(dstate)
            dnw_ref[...] = jnp.zeros_like(dnw_ref)

        lanes = [slice(h * HEAD_DIM, (h + 1) * HEAD_DIM) for h in range(n_heads)]
        sts = [s_ref[0, h] for h in range(n_heads)]
        dsts = [dstate[h] for h in range(n_heads)]
        v_new = [u_ref[:, sl] - mm_nn(w_ref[:, sl], st) for sl, st in zip(lanes, sts)]
        outs = [mm_nn(qd_ref[:, sl], st) + mm_nn(qk_ref[:, sl][:, :CHUNK], vn) for sl, st, vn in zip(lanes, sts, v_new)]
        dnw = jnp.zeros((1, HEAD_DIM), f32)
        d_outs = []
        for sl, o in zip(lanes, outs):
            do, dn, dz = jax.vjp(gated_norm, o, nw_ref[...], z_ref[:, sl])[1](dy_ref[:, sl])
            dz_ref[:, sl] = dz.astype(bf16)
            dnw = dnw + dn
            d_outs.append(do)
        dnw_ref[...] += dnw
        d_vn = [mm_tn(qk_ref[:, sl][:, :CHUNK], do) + mm_nn(kd_ref[:, sl], ds) for sl, do, ds in zip(lanes, d_outs, dsts)]
        zeros = jnp.zeros((HEAD_DIM - CHUNK, HEAD_DIM), f32)
        rows = lax.broadcasted_iota(jnp.int32, (CHUNK, HEAD_DIM), 0)
        for h, (sl, st, vn, do, ds, dvn) in enumerate(zip(lanes, sts, v_new, d_outs, dsts, d_vn)):
            du_ref[:, sl] = dvn.astype(bf16)
            dw_ref[:, sl] = (-mm_nt(dvn, st)).astype(bf16)
            dqd_ref[:, sl] = mm_nt(do, st)
            dkd_ref[:, sl] = mm_nt(vn, ds)
            dqk_ref[:, sl] = mm_nt(do, jnp.concatenate([vn, zeros], axis=0))
            dgl_ref[:, sl] = jnp.where(rows == 0, jnp.sum(st * ds, axis=0, keepdims=True), 0.0)
            dstate[h] = ds * gl_ref[0:1, sl] + mm_tn(qd_ref[:, sl], do) - mm_tn(w_ref[:, sl], dvn)

    blk = pl.BlockSpec((CHUNK, width), lambda n: (last - n, 0))
    return _pc(body, name="dn_seq_bwd", grid=(n_chunks,),
               in_specs=[blk] * 6 + [pl.BlockSpec((CHUNK, width), lambda n: (last - n, zd_off)), pl.BlockSpec((1, HEAD_DIM), lambda n: (0, 0)),
                         pl.BlockSpec((1, n_heads, HEAD_DIM, HEAD_DIM), lambda n: (last - n, 0, 0, 0)), blk],
               out_specs=[blk] * 7 + [pl.BlockSpec((1, HEAD_DIM), lambda n: (0, 0))],
               out_shape=[jax.ShapeDtypeStruct((tp, width), dt) for dt in (bf16, bf16, f32, f32, f32, f32, bf16)]
               + [jax.ShapeDtypeStruct((1, HEAD_DIM), f32)],
               scratch_shapes=[pltpu.VMEM((n_heads, HEAD_DIM, HEAD_DIM), f32)],
               compiler_params=_params("arbitrary"))(*inter, proj, dn_norm_w, states, dy)


def loss_stage(out, final_w, target):
    tp, d = out.shape
    n_tiles = tp // CHUNK

    def body(o_ref, w_ref, t_ref, loss_ref, do_ref, dob_ref, dw_ref):
        i = pl.program_id(0)

        @pl.when(i == 0)
        def _():
            loss_ref[...] = jnp.zeros_like(loss_ref)
            dw_ref[...] = jnp.zeros_like(dw_ref)

        scored = (i > 0).astype(f32)
        val, (do, dw) = jax.value_and_grad(lambda o, w: scored * loss_fn(o, w, t_ref[...]), argnums=(0, 1))(o_ref[...], w_ref[...])
        loss_ref[...] += jnp.full(loss_ref.shape, val, f32)
        do_ref[...] = do
        dob_ref[...] = do.astype(bf16)
        dw_ref[...] += dw

    row = pl.BlockSpec((CHUNK, d), lambda i: (i, 0))
    vec = pl.BlockSpec((1, d), lambda i: (0, 0))
    return _pc(body, name="loss_stage", grid=(n_tiles,),
               in_specs=[row, vec, pl.BlockSpec((CHUNK, d), lambda i: (jnp.maximum(i - 1, 0), 0))],
               out_specs=[pl.BlockSpec((1, LANES), lambda i: (0, 0)), row, row, vec],
               out_shape=[jax.ShapeDtypeStruct((1, LANES), f32), jax.ShapeDtypeStruct((tp, d), f32),
                          jax.ShapeDtypeStruct((tp, d), bf16), jax.ShapeDtypeStruct((1, d), f32)],
               compiler_params=_params("arbitrary"))(out, final_w, target)


def _adam_update(w, g, m, v):
    nm = ADAM_B1 * m + (1.0 - ADAM_B1) * g
    nv = ADAM_B2 * v + (1.0 - ADAM_B2) * jnp.square(g)
    m_hat = nm / (1.0 - ADAM_B1 ** ADAM_STEP)
    v_hat = nv / (1.0 - ADAM_B2 ** ADAM_STEP)
    return -ADAM_LR * (m_hat / (jnp.sqrt(v_hat) + ADAM_EPS) + ADAM_WD * w), nm, nv


def adamw(name, w, g, m, v):
    rows, cols = w.shape
    t = _row_tile(rows, cols, 7)

    def body(w_ref, g_ref, m_ref, v_ref, d_ref, nm_ref, nv_ref):
        d_ref[...], nm_ref[...], nv_ref[...] = _adam_update(w_ref[...], g_ref[...], m_ref[...], v_ref[...])

    blk = pl.BlockSpec((t, cols), lambda i: (i, 0))
    return _pc(body, name=name, grid=(rows // t,), in_specs=[blk] * 4, out_specs=[blk] * 3,
               out_shape=[jax.ShapeDtypeStruct(w.shape, f32)] * 3, compiler_params=_params("parallel"))(w, g, m, v)


def adamw_joined(name, w, g_mine, g_theirs, m, v, half, axis):
    rows, cols = w.shape
    hr, hc = g_mine.shape
    tr, tc = _tile(hr, hc, 9)
    nr, nc = hr // tr, hc // tc

    def body(half_ref, w_ref, gm_ref, gt_ref, m_ref, v_ref, g_ref, d_ref, nm_ref, nv_ref):
        pos = pl.program_id(axis) // (nr if axis == 0 else nc)
        g = jnp.where(pos == half_ref[0], gm_ref[...], gt_ref[...])
        delta, nm, nv = _adam_update(w_ref[...], g, m_ref[...], v_ref[...])
        g_ref[...] = g
        d_ref[...] = delta
        nm_ref[...] = nm
        nv_ref[...] = nv

    whole = pl.BlockSpec((tr, tc), lambda i, j, hf: (i, j))
    part = pl.BlockSpec((tr, tc), lambda i, j, hf: (i % nr, j % nc))
    grid_spec = pltpu.PrefetchScalarGridSpec(num_scalar_prefetch=1, grid=(rows // tr, cols // tc),
                                             in_specs=[whole, part, part, whole, whole], out_specs=[whole] * 4)
    return _pc(body, name=name, grid_spec=grid_spec, out_shape=[jax.ShapeDtypeStruct(w.shape, f32)] * 4,
               compiler_params=_params("parallel", "parallel"))(half, w, g_mine, g_theirs, m, v)


def add_halves(name, g, recv, half, kind):
    s, r, c = recv.shape
    tr, tc = _tile(r, c, 3)
    nc = c // tc

    def body(half_ref, g_ref, r_ref, o_ref):
        o_ref[...] = (g_ref[...].reshape(r_ref.shape).astype(f32) + r_ref[...].astype(f32)).astype(bf16)

    if kind == "lead":
        g_spec = pl.BlockSpec((1, 1, tr, tc), lambda i, j, k, hf: (hf[0], i, j, k))
    else:
        g_spec = pl.BlockSpec((1, tr, tc), lambda i, j, k, hf: (i, j, hf[0] * nc + k))
    blk = pl.BlockSpec((1, tr, tc), lambda i, j, k, hf: (i, j, k))
    grid_spec = pltpu.PrefetchScalarGridSpec(num_scalar_prefetch=1, grid=(s, r // tr, nc), in_specs=[g_spec, blk], out_specs=blk)
    return _pc(body, name=name, grid_spec=grid_spec, out_shape=jax.ShapeDtypeStruct((s, r, c), bf16),
               compiler_params=_params("parallel", "parallel", "parallel"))(half, g, recv)


def add_first(name, parts, got, chips):
    _, r, c = parts.shape
    tr, tc = _tile(r, c, 6)

    def body(chips_ref, mine_ref, theirs_ref, got_ref, keep_ref, pass_ref):
        keep_ref[...] = mine_ref[0].astype(f32) + got_ref[0].astype(f32)
        pass_ref[...] = (theirs_ref[0].astype(f32) + got_ref[1].astype(f32)).astype(bf16)

    blk = pl.BlockSpec((tr, tc), lambda i, j, ch: (i, j))
    grid_spec = pltpu.PrefetchScalarGridSpec(
        num_scalar_prefetch=1, grid=(r // tr, c // tc),
        in_specs=[pl.BlockSpec((1, tr, tc), lambda i, j, ch: (ch[0], i, j)), pl.BlockSpec((1, tr, tc), lambda i, j, ch: (ch[1], i, j)),
                  pl.BlockSpec((2, tr, tc), lambda i, j, ch: (0, i, j))],
        out_specs=[blk, blk])
    return _pc(body, name=name, grid_spec=grid_spec, out_shape=[jax.ShapeDtypeStruct((r, c), f32), jax.ShapeDtypeStruct((r, c), bf16)],
               compiler_params=_params("parallel", "parallel"))(chips, parts, parts, got)


def add_second(name, kept, got):
    r, c = kept.shape
    tr, tc = _tile(r, c, 3)

    def body(k_ref, g_ref, o_ref):
        o_ref[...] = k_ref[...] + g_ref[...].astype(f32)

    blk = pl.BlockSpec((tr, tc), lambda i, j: (i, j))
    return _pc(body, name=name, grid=(r // tr, c // tc), in_specs=[blk, blk], out_specs=blk,
               out_shape=jax.ShapeDtypeStruct((r, c), f32), compiler_params=_params("parallel", "parallel"))(kept, got)


def sum_leading(name, x, out_dtype=f32):
    s, r, c = x.shape
    tr, tc = _tile(r, c, s + 1)

    def body(x_ref, o_ref):
        acc = x_ref[0].astype(f32)
        for i in range(1, s):
            acc = acc + x_ref[i].astype(f32)
        o_ref[...] = acc.astype(out_dtype)

    return _pc(body, name=name, grid=(r // tr, c // tc), in_specs=[pl.BlockSpec((s, tr, tc), lambda i, j: (0, i, j))],
               out_specs=pl.BlockSpec((tr, tc), lambda i, j: (i, j)), out_shape=jax.ShapeDtypeStruct((r, c), out_dtype),
               compiler_params=_params("parallel", "parallel"))(x)


def _place():
    x, y, c = lax.axis_index("x"), lax.axis_index("y"), lax.axis_index("c")
    return x, y, c


def _route(x, y, c):
    return (x ^ (1 - c), y ^ c), (x ^ c, y ^ (1 - c)), (1 - x, 1 - y)


def _half_view(ref, half, kind, lead=()):
    if kind == "lead":
        return ref.at[(*lead, half)]
    width = ref.shape[-1] // 2
    return ref.at[(*lead, *([slice(None)] * (len(ref.shape) - len(lead) - 1)), pl.ds(half * width, width))]


def _gather_plan(ins, outs, sems, kinds):
    n = len(ins)
    send1, recv1, send2, recv2 = sems
    x, y, c = _place()
    chip = 2 * x + y
    sibling = (x, y, 1 - c)
    near, far, diag = _route(x, y, c)
    near_id, far_id, diag_id = [2 * cx + cy for cx, cy in (near, far, diag)]

    def remote(src, dst, s_sem, r_sem, to):
        return pltpu.make_async_remote_copy(src_ref=src, dst_ref=dst, send_sem=s_sem, recv_sem=r_sem, device_id=to, device_id_type=MESH)

    def slab(a, chip_id, half):
        return _half_view(outs[a], half, kinds[a], (chip_id,))

    def landed(a, chip_id, sem, frm):
        return remote(slab(a, chip_id, c), slab(a, chip_id, c), send1.at[a, sem], recv1.at[a, sem], (*frm, c))

    def onward(a, chip_id, sem):
        return remote(slab(a, chip_id, c), slab(a, chip_id, c), send2.at[a, sem], recv2.at[a, sem], sibling)

    own = [remote(_half_view(ins[a], c, kinds[a]), slab(a, chip, c), send1.at[a, j], recv1.at[a, j], (*to, c))
           for a in range(n) for j, to in enumerate((near, far))]
    relay = [remote(slab(a, near_id, c), slab(a, near_id, c), send1.at[a, 2], recv1.at[a, 2], (*far, c)) for a in range(n)]
    to_sibling = [[onward(a, cid, sem) for sem, cid in enumerate((near_id, far_id, diag_id))] for a in range(n)]
    from_sibling = [remote(slab(a, cid, 1 - c), slab(a, cid, 1 - c), send2.at[a, sem], recv2.at[a, sem], sibling)
                    for a in range(n) for sem, cid in enumerate((far_id, near_id, diag_id))]

    def start_own():
        for cp in own:
            cp.start()

    def pass_near():
        for a in range(n):
            landed(a, near_id, 0, near).wait_recv()
            relay[a].start()
            to_sibling[a][0].start()

    def pass_far():
        for sem, cid in ((1, far_id), (2, diag_id)):
            for a in range(n):
                landed(a, cid, sem, far).wait_recv()
                to_sibling[a][sem].start()

    def finish():
        for cp in from_sibling:
            cp.wait_recv()
        for cp in own + relay + [cp for row in to_sibling for cp in row]:
            cp.wait_send()

    return [start_own, pass_near, pass_far, finish]


def _gather_shapes(shards):
    return ([jax.ShapeDtypeStruct((4,) + s.shape, s.dtype) for s in shards], [pltpu.SemaphoreType.DMA((len(shards), 3))] * 4)


def gather_weights(shards, kinds):
    n = len(shards)

    def body(*refs):
        for emit in _gather_plan(refs[:n], refs[n:2 * n], refs[2 * n:], kinds):
            emit()

    out_shapes, sems = _gather_shapes(shards)
    return _pc(body, name="gather_weights", in_specs=[_ANY] * n, out_specs=[_ANY] * n, out_shape=out_shapes, scratch_shapes=sems)(*shards)


def hosted_gather(shards, kinds):
    out_shapes, sems = _gather_shapes(shards)
    return Hosted(shards, out_shapes, sems, lambda i, o, s: list(zip((0.0, 0.45, 0.8, 1.0), _gather_plan(i, o, s, kinds))))


def swap_with_sibling(name, sends):
    n = len(sends)

    def body(*refs):
        ins, outs = refs[:n], refs[n:2 * n]
        send, recv = refs[2 * n:]
        x, y, c = _place()
        cps = [pltpu.make_async_remote_copy(src_ref=ins[a], dst_ref=outs[a], send_sem=send.at[a], recv_sem=recv.at[a],
                                            device_id=(x, y, 1 - c), device_id_type=MESH) for a in range(n)]
        for cp in cps:
            cp.start()
        for cp in cps:
            cp.wait()

    return _pc(body, name=name, in_specs=[_ANY] * n, out_specs=[_ANY] * n,
               out_shape=[jax.ShapeDtypeStruct(s.shape, s.dtype) for s in sends],
               scratch_shapes=[pltpu.SemaphoreType.DMA((n,))] * 2)(*sends)


def send_grad_halves(name, grads, kinds):
    n = len(grads)

    def body(*refs):
        ins, outs = refs[:n], refs[n:2 * n]
        send, recv = refs[2 * n:]
        x, y, c = _place()
        cps = [pltpu.make_async_remote_copy(src_ref=_half_view(ins[a], 1 - c, kinds[a]), dst_ref=outs[a], send_sem=send.at[a],
                                            recv_sem=recv.at[a], device_id=(x, y, 1 - c), device_id_type=MESH) for a in range(n)]
        for cp in cps:
            cp.start()
        for cp in cps:
            cp.wait()

    shape = lambda g, kind: g.shape[1:] if kind == "lead" else g.shape[:-1] + (g.shape[-1] // 2,)
    return _pc(body, name=name, in_specs=[_ANY] * n, out_specs=[_ANY] * n,
               out_shape=[jax.ShapeDtypeStruct(shape(g, k), g.dtype) for g, k in zip(grads, kinds)],
               scratch_shapes=[pltpu.SemaphoreType.DMA((n,))] * 2)(*grads)


def _scatter_first_plan(ins, outs, sems):
    n = len(ins)
    send, recv = sems
    x, y, c = _place()
    near, _, diag = _route(x, y, c)
    cps = [pltpu.make_async_remote_copy(src_ref=ins[a].at[2 * cx + cy], dst_ref=outs[a].at[j], send_sem=send.at[a, j],
                                        recv_sem=recv.at[a, j], device_id=(*near, c), device_id_type=MESH)
           for a in range(n) for j, (cx, cy) in enumerate((near, diag))]

    def start():
        for cp in cps:
            cp.start()

    def wait():
        for cp in cps:
            cp.wait()

    return [start, wait]


def _scatter_first_shapes(parts):
    return ([jax.ShapeDtypeStruct((2,) + p.shape[1:], p.dtype) for p in parts], [pltpu.SemaphoreType.DMA((len(parts), 2))] * 2)


def hosted_scatter_first(parts):
    out_shapes, sems = _scatter_first_shapes(parts)
    return Hosted(parts, out_shapes, sems, lambda i, o, s: list(zip((0.0, 1.0), _scatter_first_plan(i, o, s))))


def _far_swap_plan(ins, outs, sems):
    send, recv = sems
    x, y, c = _place()
    _, far, _ = _route(x, y, c)
    cps = [pltpu.make_async_remote_copy(src_ref=ins[a], dst_ref=outs[a], send_sem=send.at[a], recv_sem=recv.at[a],
                                        device_id=(*far, c), device_id_type=MESH) for a in range(len(ins))]

    def start():
        for cp in cps:
            cp.start()

    def wait():
        for cp in cps:
            cp.wait()

    return [start, wait]


def hosted_scatter_second(parts):
    return Hosted(parts, [jax.ShapeDtypeStruct(p.shape, p.dtype) for p in parts], [pltpu.SemaphoreType.DMA((len(parts),))] * 2,
                  lambda i, o, s: list(zip((0.0, 1.0), _far_swap_plan(i, o, s))))


def gather_small(pack):
    def body(pack_ref, packs_ref, send, recv):
        x, y, c = _place()
        me = 4 * x + 2 * y + c
        flips = [(fx, fy, fc) for fx in (0, 1) for fy in (0, 1) for fc in (0, 1)][1:]
        peers = [(x ^ fx, y ^ fy, c ^ fc) for fx, fy, fc in flips]
        cps = [pltpu.make_async_remote_copy(src_ref=pack_ref, dst_ref=packs_ref.at[me], send_sem=send.at[j], recv_sem=recv.at[j],
                                            device_id=p, device_id_type=MESH) for j, p in enumerate(peers)]
        for cp in cps:
            cp.start()
        for j, (px, py, pc) in enumerate(peers):
            slab = packs_ref.at[4 * px + 2 * py + pc]
            pltpu.make_async_remote_copy(src_ref=slab, dst_ref=slab, send_sem=send.at[j], recv_sem=recv.at[j],
                                         device_id=(px, py, pc), device_id_type=MESH).wait_recv()
        for cp in cps:
            cp.wait_send()

    return _pc(body, name="gather_small", in_specs=[_ANY], out_specs=_ANY, out_shape=jax.ShapeDtypeStruct((8,) + pack.shape, pack.dtype),
               scratch_shapes=[pltpu.SemaphoreType.DMA((7,))] * 2)(pack)


def _pack(arrays, width):
    flat = jnp.concatenate([a.reshape(-1).astype(f32) for a in arrays])
    return jnp.pad(flat, (0, (-flat.shape[0]) % (8 * width))).reshape(-1, width)


def _unpack(pack, like, width):
    flat, out, at = pack.reshape(-1), [], 0
    for a in like:
        size = 1
        for s in a.shape:
            size *= s
        out.append(flat[at:at + size].reshape(a.shape))
        at += size
    return out


def _halves(a2d):
    r, c = a2d.shape
    return a2d.reshape(2, r // 2, c)


def local_step(x, meta, norm_w, w_main, w_ba, conv_w, a_log, dt_bias, pool_mix, pool_scale, dn_norm_w,
               late_weights, final_w, target, grad_hooks=None):
    seq, d = x.shape
    pw = pool_scale.shape[1]
    dw = conv_w.shape[1] // 3
    n_heads = dw // HEAD_DIM
    tp = FRONT_PAD + N_META + seq
    hp = jnp.concatenate([jnp.zeros((FRONT_PAD, d), f32), meta, x], axis=0)
    tm_big = tp // 2 if tp % 32 == 0 else tp
    tm_norm = max(t for t in range(16, min(tp, 352) + 1, 16) if tp % t == 0)
    tile = min(512, d)
    off_q = 2 * pw
    off_zd = off_q + 3 * dw
    off_gp = off_zd + dw
    off_gd = off_gp + d
    a_log128 = jnp.pad(a_log, ((0, 0), (0, LANES - n_heads)))
    dt128 = jnp.pad(dt_bias, ((0, 0), (0, LANES - n_heads)))

    xn = norm_fwd(hp, norm_w, tm_norm)
    if isinstance(late_weights[0], Hosted):
        proj, *fetched = matmul("proj", xn, w_main, tb=True, tm=tp, tn=tile, hosted=late_weights[0])
        w_pool_out, w_dn_out, w_o = late_weights[1](fetched)
    else:
        proj = matmul("proj", xn, w_main, tb=True, tm=tp, tn=tile)
        w_pool_out, w_dn_out, w_o = late_weights
    ba = matmul("proj_ba", xn, w_ba, tb=True, tm=tp, tn=LANES)
    y_pool = pool_fwd(proj, pool_mix, pool_scale, pw)
    q, k, v, beta_b, g_b = dn_pre_fwd(proj, ba, conv_w, a_log128, dt128, n_heads, off_q // HEAD_DIM)
    assert off_zd % dw == 0
    *inter, tmats = dn_intra_fwd(q, k, v, beta_b, g_b, n_heads)
    y_dn, states = dn_seq_fwd(inter, proj, dn_norm_w, n_heads, off_zd // dw)
    a_mat = matmul("pool_out", y_pool, w_pool_out, tm=tp, tn=tile)

    def merge(acc, a_t, gp_t, gd_t):
        return acc, sigmoid(gp_t) * a_t + sigmoid(gd_t) * acc

    b_mat, merged = matmul("dn_out_merge", y_dn, w_dn_out, tm=tm_big, tn=tile, extras=[(a_mat, 0), (proj, off_gp), (proj, off_gd)],
                           epi=merge, out_dtypes=(f32, bf16))
    out = matmul("out_proj", merged, w_o, tm=tm_big, tn=tile, extras=[(hp, 0)], epi=lambda acc, h_t: (acc + h_t,))
    loss, dout, dout_b, dfinal_w = loss_stage(out, final_w, target)

    def unmerge(dm, a_t, b_t, gp_t, gd_t):
        sp, sd = sigmoid(gp_t), sigmoid(gd_t)
        return dm * sp, dm * sd, dm * a_t * sp * (1.0 - sp), dm * b_t * sd * (1.0 - sd)

    d_a, d_b, d_gp, d_gd = matmul("d_merged", dout_b, w_o, tb=True, tm=tm_big, tn=tile,
                                  extras=[(a_mat, 0), (b_mat, 0), (proj, off_gp), (proj, off_gd)], epi=unmerge,
                                  out_dtypes=(bf16,) * 4)
    g_w_o = matmul("g_w_o", merged, dout_b, ta=True, tm=tile, tn=d, out_dtypes=(bf16,))
    d_y_pool = matmul("d_y_pool", d_a, w_pool_out, tb=True, tm=tp, tn=tile)
    g_w_pool_out = matmul("g_w_pool_out", y_pool, d_a, ta=True, tm=tile, tn=d, out_dtypes=(bf16,))
    d_y_dn = matmul("d_y_dn", d_b, w_dn_out, tb=True, tm=tp, tn=tile)
    g_w_dn_out = matmul("g_w_dn_out", y_dn, d_b, ta=True, tm=tile, tn=d, out_dtypes=(bf16,))
    d_u, d_zp, g_pool_mix, g_pool_scale = pool_bwd(proj, pool_mix, pool_scale, d_y_pool, pw)
    *d_inter, d_zd, g_dn_norm_w = dn_seq_bwd(inter, proj, dn_norm_w, states, d_y_dn, n_heads, off_zd // dw)
    d_q, d_k, d_v, d_beta, d_g = dn_intra_bwd(q, k, v, beta_b, g_b, tmats, d_inter, n_heads)
    d_qr, d_kr, d_vr, d_ba, g_cq, g_ck, g_cv, g_a_log, g_dt = dn_pre_bwd(
        proj, ba, conv_w, a_log128, dt128, (d_q, d_k, d_v, d_beta, d_g), n_heads, off_q // HEAD_DIM)
    d_proj = jnp.concatenate([d_u, d_zp, d_qr, d_kr, d_vr, d_zd, d_gp, d_gd], axis=1)
    d_ba_b = cast_bf16("cast_d_ba", d_ba)
    early = grad_hooks[0](g_w_pool_out, g_w_dn_out, g_w_o, g_pool_mix) if grad_hooks else None
    res = matmul("g_w_main", d_proj, xn, ta=True, tm=tile, tn=d, out_dtypes=(bf16,), hosted=early)
    g_w_main, early_landed = (res[0], list(res[1:])) if early else (res, [])
    g_w_ba = matmul("g_w_ba", d_ba_b, xn, ta=True, tm=LANES, tn=tile, out_dtypes=(bf16,))
    hosted = grad_hooks[1](g_w_main, g_w_ba, early_landed) if grad_hooks else None
    dxn_ba = matmul("dxn_ba", d_ba_b, w_ba, tm=tp, tn=tile)
    n_cols = d_proj.shape[1]
    tk_dxn = max(t for t in range(LANES, min(2048, n_cols) + 1, LANES) if n_cols % t == 0)
    res = matmul("dxn", d_proj, w_main, tm=tp, tn=tile, tk=tk_dxn, extras=[(dxn_ba, 0)], epi=lambda acc, e: (acc + e,), hosted=hosted)
    dxn, landed = (res[0], list(res[1:])) if hosted else (res, [])
    dh, g_norm_w, last_landed = norm_bwd(hp, norm_w, dxn, dout, tm_norm, hosted=grad_hooks[2](landed) if grad_hooks else None)
    g_conv = jnp.concatenate([g_cq, g_ck, g_cv], axis=1)
    return (loss, dh, g_norm_w, g_w_main, g_w_ba, g_conv, g_a_log[:, :n_heads], g_dt[:, :n_heads], g_pool_mix, g_pool_scale,
            g_dn_norm_w, g_w_pool_out, g_w_dn_out, g_w_o, dfinal_w, last_landed)


def kernel(x, meta_tokens, norm_w, w_in, conv_w, A_log, dt_bias, pool_mix, pool_scale, dn_norm_w, w_pool_out, w_dn_out, w_o, final_norm_w, loss_target, m_meta_tokens, m_norm_w, m_w_in, m_conv_w, m_A_log, m_dt_bias, m_pool_mix, m_pool_scale, m_dn_norm_w, m_w_pool_out, m_w_dn_out, m_w_o, m_final_norm_w, v_meta_tokens, v_norm_w, v_w_in, v_conv_w, v_A_log, v_dt_bias, v_pool_mix, v_pool_scale, v_dn_norm_w, v_w_pool_out, v_w_dn_out, v_w_o, v_final_norm_w):
    d = x.shape[-1]
    pw = pool_scale.shape[-1]
    dw = w_dn_out.shape[1] * 4
    n_heads = dw // HEAD_DIM
    gdim = pw // POOL_GROUPS
    chip = 2 * lax.axis_index("x") + lax.axis_index("y")
    core = lax.axis_index("c")

    half = core.astype(jnp.int32).reshape(1)
    me = 4 * lax.axis_index("x") + 2 * lax.axis_index("y") + core

    w_in_t = w_in[0].T
    mix_s = pool_mix[0].reshape(POOL_GROUPS * (gdim // 4), gdim)
    small_s = _pack([meta_tokens, conv_w[0]], d)
    small_rows = small_s.shape[0]
    small_s = jnp.pad(small_s, ((0, (-small_rows) % 16), (0, 0)))
    sw = w_in_t.shape[0]
    n_main, n_ba = 2 * pw + 4 * dw, 2 * n_heads

    early = [cast_bf16("cast_w_in", w_in_t), _halves(cast_bf16("cast_mix", mix_s)), _halves(small_s)]
    late = [_halves(cast_bf16("cast_w_po", w_pool_out[0])), _halves(cast_bf16("cast_w_do", w_dn_out[0])),
            _halves(cast_bf16("cast_w_o", w_o[0]))]
    fill = lambda g, own: lax.dynamic_update_slice(g, own[None], (chip,) + (0,) * own.ndim)
    g_in, g_mix, g_small = [fill(g, own) for g, own in zip(gather_weights(early, ["cols", "lead", "lead"]), early)]

    def shard_rows(lo, hi):
        cut = [(max(lo, j * sw), min(hi, (j + 1) * sw), j) for j in range(4)]
        return [g_in[j, a - j * sw:b - j * sw] for a, b, j in cut if a < b]

    w_main = jnp.concatenate(shard_rows(0, n_main) + shard_rows(n_main + n_ba, 4 * sw), axis=0)
    w_ba = jnp.pad(jnp.concatenate(shard_rows(n_main, n_main + n_ba), axis=0), ((0, LANES - n_ba), (0, 0)))
    cat_cols = lambda g: jnp.concatenate([g[j].reshape(-1, g.shape[-1]) for j in range(4)], axis=1)

    def late_weights(fetched):
        g_po, g_do, g_o = [fill(g, own) for g, own in zip(fetched, late)]
        return cat_cols(g_po), g_do.reshape(-1, g_do.shape[-1]), g_o.reshape(-1, g_o.shape[-1])

    mix_full = g_mix.reshape(4, POOL_GROUPS, gdim // 4, gdim).transpose(1, 0, 2, 3).reshape(POOL_GROUPS, gdim, gdim)
    smalls = [_unpack(g_small[j].reshape(-1, d)[:small_rows], [meta_tokens, conv_w[0]], d) for j in range(4)]
    meta_full = jnp.concatenate([s[0] for s in smalls], axis=1)
    conv_full = jnp.concatenate([s[1] for s in smalls], axis=1)

    names = ["w_in", "w_po", "w_do", "w_o", "mix"]
    kinds = ["cols", "lead", "lead", "lead", "lead"]

    far_chip = 2 * (lax.axis_index("x") ^ core) + (lax.axis_index("y") ^ (1 - core))
    chips = jnp.stack([chip, far_chip]).astype(jnp.int32)
    col_parts = lambda g: g.reshape(2, g.shape[0] // 2, 4, g.shape[1] // 4).transpose(0, 2, 1, 3)
    row_parts = lambda g: g.reshape(4, 2, g.shape[0] // 8, g.shape[1]).transpose(1, 0, 2, 3)

    def pair_sums(tag, group, parts):
        group_kinds = [kinds[names.index(nm)] for nm in group]
        from_sibling = send_grad_halves("send_grad_halves_" + tag, parts, group_kinds)
        return [add_halves("add_" + nm, p, r, half, k) for nm, p, r, k in zip(group, parts, from_sibling, group_kinds)]

    later = names[1:]
    state = {}

    def later_grads(g_w_po, g_w_do, g_w_o_full, g_mix_full):
        mix_rows = POOL_GROUPS * (gdim // 4)
        mix_parts = (g_mix_full.astype(bf16).reshape(POOL_GROUPS, 4, gdim // 4, gdim).transpose(1, 0, 2, 3)
                     .reshape(4, 2, mix_rows // 2, gdim).transpose(1, 0, 2, 3))
        state["later_sums"] = pair_sums("later", later, [col_parts(g_w_po), row_parts(g_w_do), row_parts(g_w_o_full), mix_parts])
        return hosted_scatter_first(state["later_sums"])

    def input_grads(g_w_main, g_w_ba, later_from_near):
        def grad_rows(lo, hi):
            segs = [(0, n_main, g_w_main, 0), (n_main, n_main + n_ba, g_w_ba, 0), (n_main + n_ba, 4 * sw, g_w_main, n_main)]
            cut = [(max(lo, s0), min(hi, s1), s0, arr, off) for s0, s1, arr, off in segs]
            return [arr[a - s0 + off:b - s0 + off] for a, b, s0, arr, off in cut if a < b]

        kept, passed = zip(*[add_first("add1_" + nm, p, g, chips) for nm, p, g in zip(later, state["later_sums"], later_from_near)])
        state["later_kept"] = list(kept)
        in_parts = jnp.stack([jnp.concatenate(grad_rows(j * sw, (j + 1) * sw), axis=0) for j in range(4)])
        state["in_sums"] = pair_sums("input", names[:1], [in_parts])
        return hosted_scatter_first(state["in_sums"]) + hosted_scatter_second(list(passed))

    def input_second(landed):
        state["later_from_far"] = landed[1:]
        state["in_kept"], in_passed = add_first("add1_w_in", state["in_sums"][0], landed[0], chips)
        return hosted_scatter_second([in_passed])

    (loss, dh, g_norm_w, _, _, g_conv, g_a_log, g_dt, _, g_pool_scale, g_dn_norm_w, _, _, _, g_final_w,
     (in_from_far,)) = local_step(x[0], meta_full, norm_w, w_main, w_ba, conv_full, A_log, dt_bias, mix_full, pool_scale,
                                  dn_norm_w, (hosted_gather(late, ["lead"] * 3), late_weights), final_norm_w.reshape(1, d),
                                  loss_target[0], grad_hooks=(later_grads, input_grads, input_second))
    grad_x = dh[FRONT_PAD + N_META:][None]
    g_meta = dh[FRONT_PAD:FRONT_PAD + N_META]

    mine = [add_second("add2_" + nm, k_, g) for nm, k_, g in
            zip(names, [state["in_kept"]] + state["later_kept"], [in_from_far] + list(state["later_from_far"]))]
    small_like = [loss, g_norm_w, g_a_log, g_dt, g_pool_scale, g_dn_norm_w, g_final_w, g_conv, g_meta]
    pack = _pack(small_like, d)
    packs = gather_small(pack)
    theirs = swap_with_sibling("swap_grad_halves", mine)
    total = sum_leading("sum_small", lax.dynamic_update_slice(packs, pack[None], (me, 0, 0)))
    (loss_t, g_norm_w, g_a_log, g_dt, g_pool_scale, g_dn_norm_w, g_final_w, g_conv, g_meta) = _unpack(total, small_like, d)
    loss_out = loss_t[0, 0]
    g_conv_s = lax.dynamic_slice_in_dim(g_conv, chip * (g_conv.shape[1] // 4), g_conv.shape[1] // 4, axis=1)
    g_meta_s = lax.dynamic_slice_in_dim(g_meta, chip * (d // 4), d // 4, axis=1)

    weights = [meta_tokens, norm_w, w_in, conv_w, A_log, dt_bias, pool_mix, pool_scale, dn_norm_w, w_pool_out, w_dn_out, w_o, final_norm_w]
    ms = [m_meta_tokens, m_norm_w, m_w_in, m_conv_w, m_A_log, m_dt_bias, m_pool_mix, m_pool_scale, m_dn_norm_w, m_w_pool_out, m_w_dn_out, m_w_o, m_final_norm_w]
    vs = [v_meta_tokens, v_norm_w, v_w_in, v_conv_w, v_A_log, v_dt_bias, v_pool_mix, v_pool_scale, v_dn_norm_w, v_w_pool_out, v_w_dn_out, v_w_o, v_final_norm_w]
    grads = [g_meta_s, g_norm_w, None, g_conv_s[None], g_a_log, g_dt, None, g_pool_scale, g_dn_norm_w, None, None, None, g_final_w.reshape(d)]
    deltas, new_ms, new_vs = [None] * 13, [None] * 13, [None] * 13
    big = [2, 9, 10, 11, 6]
    for i, nm, g_mine, g_theirs in zip(big, names, mine, theirs):
        if nm == "w_in":
            to2d, back, axis = (lambda t: t[0].T), (lambda t: t.T[None]), 1
        else:
            to2d, back, axis = (lambda t: t.reshape(-1, t.shape[-1])), (lambda t, i=i: t.reshape(weights[i].shape)), 0
        res = adamw_joined("adamw_" + nm, to2d(weights[i]), g_mine, g_theirs, to2d(ms[i]), to2d(vs[i]), half, axis)
        grads[i], deltas[i], new_ms[i], new_vs[i] = [back(t) for t in res]
    small_idx = [i for i in range(13) if i not in big]
    packs = [_pack([arrs[i] for i in small_idx], d) for arrs in (weights, grads, ms, vs)]
    outs = adamw("adamw_small", *packs)
    like = [weights[i] for i in small_idx]
    for res, dest in zip(outs, (deltas, new_ms, new_vs)):
        for i, val in zip(small_idx, _unpack(res, like, d)):
            dest[i] = val
    return (loss_out, grad_x, *grads, *deltas, *new_ms, *new_vs)
```
